```python
import math
import jax, jax.numpy as jnp
from jax import lax
import numpy as np

D_MODEL = 1024
BATCH = 8
SEQ = 4096
DEPTH = 1

MIX_WIDTH = 2 * D_MODEL
HEAD_DIM = 64
SSD_WIDTH = MIX_WIDTH // 2
SSD_HEADS = SSD_WIDTH // HEAD_DIM
SSD_GROUPS = 2
SSD_STATE = 128
SSD_CHUNK = 128
CONV_K = 4
CONV_DIM = SSD_WIDTH + 2 * SSD_GROUPS * SSD_STATE
ATTN_WIDTH = MIX_WIDTH - SSD_WIDTH
ATTN_HEADS = ATTN_WIDTH // HEAD_DIM
ATTN_KV_HEADS = ATTN_HEADS // 8
WINDOW = 128
ATTN_BLOCK = WINDOW
D_FF = 4 * D_MODEL
PROJ_WIDTH = SSD_WIDTH + CONV_DIM + SSD_HEADS + ATTN_WIDTH + 2 * ATTN_KV_HEADS * HEAD_DIM
ALPHA = (2.0 * DEPTH) ** 0.25
BETA = (8.0 * DEPTH) ** -0.25
LN_EPS = 1e-5
RMS_EPS = 1e-5

kernel_name = "hymba_ssd_swa_sink_alibi_deepnorm_adaln"


def layer_norm(x, g, b):
    xf = x.astype(jnp.float32)
    mu = jnp.mean(xf, axis=-1, keepdims=True)
    var = jnp.mean(jnp.square(xf - mu), axis=-1, keepdims=True)
    return ((xf - mu) * lax.rsqrt(var + LN_EPS)).astype(x.dtype) * g + b


def gated_group_rmsnorm(y, z, w):
    bsz, L, _ = y.shape
    h = (y * jax.nn.silu(z)).astype(jnp.float32).reshape(bsz, L, SSD_GROUPS, -1)
    h = h * lax.rsqrt(jnp.mean(jnp.square(h), axis=-1, keepdims=True) + RMS_EPS)
    return h.reshape(bsz, L, -1).astype(y.dtype) * w


def causal_dwconv(u, w, b):
    K = w.shape[0]
    y = lax.conv_general_dilated(u, w[:, None, :].astype(u.dtype), window_strides=(1,),
                                 padding=[(K - 1, 0)], dimension_numbers=("NWC", "WIO", "NWC"),
                                 feature_group_count=u.shape[-1])
    return y + b


def segsum(a):
    T = a.shape[-1]
    ae = jnp.broadcast_to(a[..., None], a.shape + (T,))
    ae = jnp.where(jnp.tril(jnp.ones((T, T), dtype=bool), k=-1), ae, 0.0)
    cs = jnp.cumsum(ae, axis=-2)
    return jnp.where(jnp.tril(jnp.ones((T, T), dtype=bool), k=0), cs, -jnp.inf)


def ssd_chunked(xs, dt, A, Bm, Cm):
    b, L, H, P = xs.shape
    G, N = Bm.shape[2], Bm.shape[3]
    R = H // G
    T = SSD_CHUNK
    nc = L // T
    X = (xs * dt[..., None]).reshape(b, nc, T, G, R, P)
    a = jnp.transpose((dt * A).reshape(b, nc, T, G, R), (0, 3, 4, 1, 2))
    a_cum = jnp.cumsum(a, axis=-1)
    Lm = jnp.exp(segsum(a))
    Bc = Bm.reshape(b, nc, T, G, N)
    Cc = Cm.reshape(b, nc, T, G, N)
    cb = jnp.einsum("bclgn,bcsgn->bcgls", Cc, Bc)
    y_diag = jnp.einsum("bcgls,bgrcls,bcsgrp->bclgrp", cb, Lm, X)
    decay_states = jnp.exp(a_cum[..., -1:] - a_cum)
    states = jnp.einsum("bclgn,bgrcl,bclgrp->bcgrpn", Bc, decay_states, X)
    chunk_decay = jnp.exp(a_cum[..., -1])

    def step(S, inp):
        st, dec = inp
        return S * dec[..., None, None] + st, S

    S0 = jnp.zeros((b, G, R, P, N), dtype=X.dtype)
    _, prev = lax.scan(step, S0, (jnp.moveaxis(states, 1, 0), jnp.moveaxis(chunk_decay, -1, 0)))
    prev = jnp.moveaxis(prev, 0, 1)
    y_off = jnp.einsum("bclgn,bcgrpn,bgrcl->bclgrp", Cc, prev, jnp.exp(a_cum))
    return (y_diag + y_off).reshape(b, L, H, P)


def alibi_slopes(n):
    def pow2(m):
        start = 2.0 ** (-8.0 / m)
        return [start ** (i + 1) for i in range(m)]
    if math.log2(n).is_integer():
        s = pow2(n)
    else:
        c = 2 ** math.floor(math.log2(n))
        s = pow2(c) + pow2(2 * c)[0::2][: n - c]
    return jnp.asarray(np.array(s, dtype=np.float32))


def swa_sink_alibi(q, k, v, sinks):
    b, L, H, d = q.shape
    KV = k.shape[2]
    R = H // KV
    W = ATTN_BLOCK
    nb = L // W
    qb = q.reshape(b, nb, W, KV, R, d)
    kb = k.reshape(b, nb, W, KV, d)
    vb = v.reshape(b, nb, W, KV, d)
    prev = lambda t: jnp.concatenate([jnp.zeros_like(t[:, :1]), t[:, :-1]], axis=1)
    kk = jnp.concatenate([prev(kb), kb], axis=2)
    vv = jnp.concatenate([prev(vb), vb], axis=2)
    s = jnp.einsum("bnqkrd,bnskd->bnkrqs", qb, kk).astype(jnp.float32) * (d ** -0.5)
    dist = jnp.arange(W)[:, None] + W - jnp.arange(2 * W)[None, :]
    key_pos = jnp.arange(nb)[:, None] * W + jnp.arange(2 * W)[None, :] - W
    valid = ((dist >= 0) & (dist < WINDOW))[None] & (key_pos >= 0)[:, None, :]
    slopes = alibi_slopes(H).reshape(KV, R)
    s = s - slopes[:, :, None, None] * dist.astype(jnp.float32)
    s = jnp.where(valid[None, :, None, None], s, -jnp.inf)
    sink = sinks.astype(jnp.float32).reshape(KV, R)[:, :, None]
    m = jnp.maximum(jnp.max(s, axis=-1), sink)
    p = jnp.exp(s - m[..., None])
    denom = jnp.sum(p, axis=-1) + jnp.exp(sink - m)
    p = (p / denom[..., None]).astype(v.dtype)
    o = jnp.einsum("bnkrqs,bnskd->bnqkrd", p, vv)
    return o.reshape(b, L, H * d)


def hybrid_mixer(u, w_in, conv_w, conv_b, dt_bias, a_log, d_skip, norm_w, sinks, w_out):
    bsz, L, _ = u.shape
    proj = u @ w_in
    i1 = SSD_WIDTH
    i2 = i1 + CONV_DIM
    i3 = i2 + SSD_HEADS
    i4 = i3 + ATTN_WIDTH
    i5 = i4 + ATTN_KV_HEADS * HEAD_DIM
    z, xbc, dt_raw, q, k, v = jnp.split(proj, [i1, i2, i3, i4, i5], axis=-1)
    xbc = jax.nn.silu(causal_dwconv(xbc, conv_w, conv_b))
    xs, Bm, Cm = jnp.split(xbc, [SSD_WIDTH, SSD_WIDTH + SSD_GROUPS * SSD_STATE], axis=-1)
    f32 = jnp.float32
    dt = jax.nn.softplus(dt_raw.astype(f32) + dt_bias.astype(f32))
    A = -jnp.exp(a_log.astype(f32))
    xs_h = xs.astype(f32).reshape(bsz, L, SSD_HEADS, HEAD_DIM)
    y = ssd_chunked(xs_h, dt, A,
                    Bm.astype(f32).reshape(bsz, L, SSD_GROUPS, SSD_STATE),
                    Cm.astype(f32).reshape(bsz, L, SSD_GROUPS, SSD_STATE))
    y = y + d_skip.astype(f32)[:, None] * xs_h
    y = gated_group_rmsnorm(y.reshape(bsz, L, SSD_WIDTH).astype(u.dtype), z, norm_w)
    o = swa_sink_alibi(q.reshape(bsz, L, ATTN_HEADS, HEAD_DIM),
                       k.reshape(bsz, L, ATTN_KV_HEADS, HEAD_DIM),
                       v.reshape(bsz, L, ATTN_KV_HEADS, HEAD_DIM), sinks)
    return jnp.concatenate([y, o.astype(y.dtype)], axis=-1) @ w_out


def _fwd_setup_inputs(seed: int = 0) -> dict:
    key = jax.random.key(seed)
    ks = jax.random.split(key, 24)
    nrm = lambda k, shape, s: jax.random.normal(k, shape, jnp.float32) * s
    Dp = DEPTH
    dt0 = jnp.exp(jax.random.uniform(ks[10], (Dp, SSD_HEADS), jnp.float32,
                                     math.log(1e-3), math.log(1e-1)))
    dt_bias = dt0 + jnp.log(-jnp.expm1(-dt0))
    a_log = jnp.log(jax.random.uniform(ks[11], (Dp, SSD_HEADS), jnp.float32, 1.0, 16.0))
    return {
        "x": nrm(ks[0], (BATCH, SEQ, D_MODEL), 1.0),
        "c": nrm(ks[1], (BATCH, D_MODEL), 1.0),
        "ln_in_g": 1.0 + nrm(ks[2], (D_MODEL,), 0.02),
        "ln_in_b": nrm(ks[3], (D_MODEL,), 0.02),
        "ada_w": nrm(ks[4], (Dp, D_MODEL, 6 * D_MODEL), 0.1 * D_MODEL ** -0.5),
        "ada_b": nrm(ks[5], (Dp, 6 * D_MODEL), 0.02),
        "w_in": nrm(ks[6], (Dp, D_MODEL, PROJ_WIDTH), D_MODEL ** -0.5),
        "conv_w": nrm(ks[7], (Dp, CONV_K, CONV_DIM), CONV_K ** -0.5),
        "conv_b": nrm(ks[8], (Dp, CONV_DIM), 0.02),
        "dt_bias": dt_bias,
        "a_log": a_log,
        "d_skip": 1.0 + nrm(ks[12], (Dp, SSD_HEADS), 0.1),
        "ssd_norm_w": 1.0 + nrm(ks[13], (Dp, SSD_WIDTH), 0.02),
        "attn_sinks": nrm(ks[14], (Dp, ATTN_HEADS), 0.5),
        "w_out": nrm(ks[15], (Dp, MIX_WIDTH, D_MODEL), BETA * MIX_WIDTH ** -0.5),
        "ln1_g": 1.0 + nrm(ks[16], (Dp, D_MODEL), 0.02),
        "ln1_b": nrm(ks[17], (Dp, D_MODEL), 0.02),
        "w_ff1": nrm(ks[18], (Dp, D_MODEL, D_FF), D_MODEL ** -0.5),
        "b_ff1": nrm(ks[19], (Dp, D_FF), 0.02),
        "w_ff2": nrm(ks[20], (Dp, D_FF, D_MODEL), BETA * D_FF ** -0.5),
        "b_ff2": nrm(ks[21], (Dp, D_MODEL), 0.02),
        "ln2_g": 1.0 + nrm(ks[22], (Dp, D_MODEL), 0.02),
        "ln2_b": nrm(ks[23], (Dp, D_MODEL), 0.02),
    }


def _fwd_reference(x, c, ln_in_g, ln_in_b, ada_w, ada_b, w_in, conv_w, conv_b, dt_bias, a_log, d_skip,
              ssd_norm_w, attn_sinks, w_out, ln1_g, ln1_b, w_ff1, b_ff1, w_ff2, b_ff2, ln2_g, ln2_b):
    h = layer_norm(x, ln_in_g, ln_in_b)
    cs = jax.nn.silu(c)
    for l in range(DEPTH):
        mod = cs @ ada_w[l] + ada_b[l]
        sh1, sc1, g1, sh2, sc2, g2 = jnp.split(mod[:, None, :], 6, axis=-1)
        u = h * (1.0 + sc1) + sh1
        mix = hybrid_mixer(u, w_in[l], conv_w[l], conv_b[l], dt_bias[l], a_log[l], d_skip[l],
                           ssd_norm_w[l], attn_sinks[l], w_out[l])
        h = layer_norm(ALPHA * h + (1.0 + g1) * mix, ln1_g[l], ln1_b[l])
        u = h * (1.0 + sc2) + sh2
        f = jnp.square(jax.nn.relu(u @ w_ff1[l] + b_ff1[l])) @ w_ff2[l] + b_ff2[l]
        h = layer_norm(ALPHA * h + (1.0 + g2) * f, ln2_g[l], ln2_b[l])
    return h


import jax as _jax
import jax.numpy as _jnp

TWIN_FORMAT = 'train_step'
FWD_PARAMS = ['x', 'c', 'ln_in_g', 'ln_in_b', 'ada_w', 'ada_b', 'w_in', 'conv_w', 'conv_b', 'dt_bias', 'a_log', 'd_skip', 'ssd_norm_w', 'attn_sinks', 'w_out', 'ln1_g', 'ln1_b', 'w_ff1', 'b_ff1', 'w_ff2', 'b_ff2', 'ln2_g', 'ln2_b']
TWIN_WEIGHTS = ['ln_in_g', 'ln_in_b', 'ada_w', 'ada_b', 'w_in', 'conv_w', 'conv_b', 'dt_bias', 'a_log', 'd_skip', 'ssd_norm_w', 'attn_sinks', 'w_out', 'ln1_g', 'ln1_b', 'w_ff1', 'b_ff1', 'w_ff2', 'b_ff2', 'ln2_g', 'ln2_b']
TWIN_DIFF_INPUT = 'x'
TWIN_INPUTS = ['x', 'c', 'ln_in_g', 'ln_in_b', 'ada_w', 'ada_b', 'w_in', 'conv_w', 'conv_b', 'dt_bias', 'a_log', 'd_skip', 'ssd_norm_w', 'attn_sinks', 'w_out', 'ln1_g', 'ln1_b', 'w_ff1', 'b_ff1', 'w_ff2', 'b_ff2', 'ln2_g', 'ln2_b', 'loss_target', 'm_ln_in_g', 'm_ln_in_b', 'm_ada_w', 'm_ada_b', 'm_w_in', 'm_conv_w', 'm_conv_b', 'm_dt_bias', 'm_a_log', 'm_d_skip', 'm_ssd_norm_w', 'm_attn_sinks', 'm_w_out', 'm_ln1_g', 'm_ln1_b', 'm_w_ff1', 'm_b_ff1', 'm_w_ff2', 'm_b_ff2', 'm_ln2_g', 'm_ln2_b', 'v_ln_in_g', 'v_ln_in_b', 'v_ada_w', 'v_ada_b', 'v_w_in', 'v_conv_w', 'v_conv_b', 'v_dt_bias', 'v_a_log', 'v_d_skip', 'v_ssd_norm_w', 'v_attn_sinks', 'v_w_out', 'v_ln1_g', 'v_ln1_b', 'v_w_ff1', 'v_b_ff1', 'v_w_ff2', 'v_b_ff2', 'v_ln2_g', 'v_ln2_b']
TWIN_OUTPUTS = ['loss', 'grad_x', 'grad_ln_in_g', 'grad_ln_in_b', 'grad_ada_w', 'grad_ada_b', 'grad_w_in', 'grad_conv_w', 'grad_conv_b', 'grad_dt_bias', 'grad_a_log', 'grad_d_skip', 'grad_ssd_norm_w', 'grad_attn_sinks', 'grad_w_out', 'grad_ln1_g', 'grad_ln1_b', 'grad_w_ff1', 'grad_b_ff1', 'grad_w_ff2', 'grad_b_ff2', 'grad_ln2_g', 'grad_ln2_b', 'delta_ln_in_g', 'delta_ln_in_b', 'delta_ada_w', 'delta_ada_b', 'delta_w_in', 'delta_conv_w', 'delta_conv_b', 'delta_dt_bias', 'delta_a_log', 'delta_d_skip', 'delta_ssd_norm_w', 'delta_attn_sinks', 'delta_w_out', 'delta_ln1_g', 'delta_ln1_b', 'delta_w_ff1', 'delta_b_ff1', 'delta_w_ff2', 'delta_b_ff2', 'delta_ln2_g', 'delta_ln2_b', 'new_m_ln_in_g', 'new_m_ln_in_b', 'new_m_ada_w', 'new_m_ada_b', 'new_m_w_in', 'new_m_conv_w', 'new_m_conv_b', 'new_m_dt_bias', 'new_m_a_log', 'new_m_d_skip', 'new_m_ssd_norm_w', 'new_m_attn_sinks', 'new_m_w_out', 'new_m_ln1_g', 'new_m_ln1_b', 'new_m_w_ff1', 'new_m_b_ff1', 'new_m_w_ff2', 'new_m_b_ff2', 'new_m_ln2_g', 'new_m_ln2_b', 'new_v_ln_in_g', 'new_v_ln_in_b', 'new_v_ada_w', 'new_v_ada_b', 'new_v_w_in', 'new_v_conv_w', 'new_v_conv_b', 'new_v_dt_bias', 'new_v_a_log', 'new_v_d_skip', 'new_v_ssd_norm_w', 'new_v_attn_sinks', 'new_v_w_out', 'new_v_ln1_g', 'new_v_ln1_b', 'new_v_w_ff1', 'new_v_b_ff1', 'new_v_w_ff2', 'new_v_b_ff2', 'new_v_ln2_g', 'new_v_ln2_b']
TWIN_LEAF_KINDS = {'loss': 'loss', 'grad_x': 'grad_x', 'grad_ln_in_g': 'grad_w', 'grad_ln_in_b': 'grad_w', 'grad_ada_w': 'grad_w', 'grad_ada_b': 'grad_w', 'grad_w_in': 'grad_w', 'grad_conv_w': 'grad_w', 'grad_conv_b': 'grad_w', 'grad_dt_bias': 'grad_w', 'grad_a_log': 'grad_w', 'grad_d_skip': 'grad_w', 'grad_ssd_norm_w': 'grad_w', 'grad_attn_sinks': 'grad_w', 'grad_w_out': 'grad_w', 'grad_ln1_g': 'grad_w', 'grad_ln1_b': 'grad_w', 'grad_w_ff1': 'grad_w', 'grad_b_ff1': 'grad_w', 'grad_w_ff2': 'grad_w', 'grad_b_ff2': 'grad_w', 'grad_ln2_g': 'grad_w', 'grad_ln2_b': 'grad_w', 'delta_ln_in_g': 'delta_w', 'delta_ln_in_b': 'delta_w', 'delta_ada_w': 'delta_w', 'delta_ada_b': 'delta_w', 'delta_w_in': 'delta_w', 'delta_conv_w': 'delta_w', 'delta_conv_b': 'delta_w', 'delta_dt_bias': 'delta_w', 'delta_a_log': 'delta_w', 'delta_d_skip': 'delta_w', 'delta_ssd_norm_w': 'delta_w', 'delta_attn_sinks': 'delta_w', 'delta_w_out': 'delta_w', 'delta_ln1_g': 'delta_w', 'delta_ln1_b': 'delta_w', 'delta_w_ff1': 'delta_w', 'delta_b_ff1': 'delta_w', 'delta_w_ff2': 'delta_w', 'delta_b_ff2': 'delta_w', 'delta_ln2_g': 'delta_w', 'delta_ln2_b': 'delta_w', 'new_m_ln_in_g': 'new_m', 'new_m_ln_in_b': 'new_m', 'new_m_ada_w': 'new_m', 'new_m_ada_b': 'new_m', 'new_m_w_in': 'new_m', 'new_m_conv_w': 'new_m', 'new_m_conv_b': 'new_m', 'new_m_dt_bias': 'new_m', 'new_m_a_log': 'new_m', 'new_m_d_skip': 'new_m', 'new_m_ssd_norm_w': 'new_m', 'new_m_attn_sinks': 'new_m', 'new_m_w_out': 'new_m', 'new_m_ln1_g': 'new_m', 'new_m_ln1_b': 'new_m', 'new_m_w_ff1': 'new_m', 'new_m_b_ff1': 'new_m', 'new_m_w_ff2': 'new_m', 'new_m_b_ff2': 'new_m', 'new_m_ln2_g': 'new_m', 'new_m_ln2_b': 'new_m', 'new_v_ln_in_g': 'new_v', 'new_v_ln_in_b': 'new_v', 'new_v_ada_w': 'new_v', 'new_v_ada_b': 'new_v', 'new_v_w_in': 'new_v', 'new_v_conv_w': 'new_v', 'new_v_conv_b': 'new_v', 'new_v_dt_bias': 'new_v', 'new_v_a_log': 'new_v', 'new_v_d_skip': 'new_v', 'new_v_ssd_norm_w': 'new_v', 'new_v_attn_sinks': 'new_v', 'new_v_w_out': 'new_v', 'new_v_ln1_g': 'new_v', 'new_v_ln1_b': 'new_v', 'new_v_w_ff1': 'new_v', 'new_v_b_ff1': 'new_v', 'new_v_w_ff2': 'new_v', 'new_v_b_ff2': 'new_v', 'new_v_ln2_g': 'new_v', 'new_v_ln2_b': 'new_v'}


def _forward(args):
    return _fwd_reference(*[args[k] for k in FWD_PARAMS])


def _output_shape():
    def fwd():
        inp = _fwd_setup_inputs(0)
        return _fwd_reference(*[inp[k] for k in FWD_PARAMS])
    out = _jax.eval_shape(fwd)
    return out.shape, out.dtype

N_MICROBATCH = 1
ADAM_LR = 0.001
ADAM_B1 = 0.9
ADAM_B2 = 0.999
ADAM_EPS = 1e-08
ADAM_WD = 0.01
ADAM_STEP = 10
PER_EXAMPLE_BATCH_AXIS = {'x': 0, 'c': 0, 'loss_target': 0}
SHARED_INPUTS = []
_WEIGHT_DTYPES = {'ln_in_g': _jnp.float32, 'ln_in_b': _jnp.float32, 'ada_w': _jnp.float32, 'ada_b': _jnp.float32, 'w_in': _jnp.float32, 'conv_w': _jnp.float32, 'conv_b': _jnp.float32, 'dt_bias': _jnp.float32, 'a_log': _jnp.float32, 'd_skip': _jnp.float32, 'ssd_norm_w': _jnp.float32, 'attn_sinks': _jnp.float32, 'w_out': _jnp.float32, 'ln1_g': _jnp.float32, 'ln1_b': _jnp.float32, 'w_ff1': _jnp.float32, 'b_ff1': _jnp.float32, 'w_ff2': _jnp.float32, 'b_ff2': _jnp.float32, 'ln2_g': _jnp.float32, 'ln2_b': _jnp.float32}
MOMENT_SCALE = {'ln_in_g': 7.344489e-01, 'ln_in_b': 5.639978e-01, 'ada_w': 7.489306e-02, 'ada_b': 1.977908e-01, 'w_in': 4.963178e-02, 'conv_w': 5.356994e-02, 'conv_b': 9.236942e-02, 'dt_bias': 1.482016e-01, 'a_log': 1.985387e-01, 'd_skip': 4.357929e-01, 'ssd_norm_w': 6.040688e-02, 'attn_sinks': 1.705775e-02, 'w_out': 1.194141e-01, 'ln1_g': 8.491492e-01, 'ln1_b': 5.869549e-01, 'w_ff1': 5.446135e-02, 'b_ff1': 1.131698e-01, 'w_ff2': 2.488285e-01, 'b_ff2': 4.669343e-01, 'ln2_g': 3.212774e+01, 'ln2_b': 7.248928e+00}


def _to_microbatches(a, axis):
    t = _jnp.moveaxis(a, axis, 0)
    t = t.reshape((N_MICROBATCH, t.shape[0] // N_MICROBATCH) + t.shape[1:])
    return _jnp.moveaxis(t, 1, axis + 1)


def setup_inputs(seed: int = 0) -> dict:
    inp = _fwd_setup_inputs(seed)
    key = _jax.random.fold_in(_jax.random.key(seed), 7919)
    shape, _ = _output_shape()
    out = dict(inp)
    out["loss_target"] = _jax.random.normal(_jax.random.fold_in(key, 0), shape, _jnp.float32)
    for i, name in enumerate(TWIN_WEIGHTS):
        w = inp[name].astype(_jnp.float32)
        if MOMENT_SCALE is None:
            s = _jnp.sqrt(_jnp.mean(_jnp.square(w)) + 1e-30)
        else:
            s = MOMENT_SCALE[name]
        km, kv = _jax.random.split(_jax.random.fold_in(key, i + 1))
        out[name] = w
        out["m_" + name] = s * _jax.random.normal(km, w.shape, _jnp.float32)
        out["v_" + name] = (s * s) * _jax.random.uniform(kv, w.shape, _jnp.float32, 0.5, 1.5)
    if N_MICROBATCH > 1:
        for name, axis in PER_EXAMPLE_BATCH_AXIS.items():
            out[name] = _to_microbatches(out[name], axis)
    return {'x': out['x'], 'c': out['c'], 'ln_in_g': out['ln_in_g'], 'ln_in_b': out['ln_in_b'], 'ada_w': out['ada_w'], 'ada_b': out['ada_b'], 'w_in': out['w_in'], 'conv_w': out['conv_w'], 'conv_b': out['conv_b'], 'dt_bias': out['dt_bias'], 'a_log': out['a_log'], 'd_skip': out['d_skip'], 'ssd_norm_w': out['ssd_norm_w'], 'attn_sinks': out['attn_sinks'], 'w_out': out['w_out'], 'ln1_g': out['ln1_g'], 'ln1_b': out['ln1_b'], 'w_ff1': out['w_ff1'], 'b_ff1': out['b_ff1'], 'w_ff2': out['w_ff2'], 'b_ff2': out['b_ff2'], 'ln2_g': out['ln2_g'], 'ln2_b': out['ln2_b'], 'loss_target': out['loss_target'], 'm_ln_in_g': out['m_ln_in_g'], 'm_ln_in_b': out['m_ln_in_b'], 'm_ada_w': out['m_ada_w'], 'm_ada_b': out['m_ada_b'], 'm_w_in': out['m_w_in'], 'm_conv_w': out['m_conv_w'], 'm_conv_b': out['m_conv_b'], 'm_dt_bias': out['m_dt_bias'], 'm_a_log': out['m_a_log'], 'm_d_skip': out['m_d_skip'], 'm_ssd_norm_w': out['m_ssd_norm_w'], 'm_attn_sinks': out['m_attn_sinks'], 'm_w_out': out['m_w_out'], 'm_ln1_g': out['m_ln1_g'], 'm_ln1_b': out['m_ln1_b'], 'm_w_ff1': out['m_w_ff1'], 'm_b_ff1': out['m_b_ff1'], 'm_w_ff2': out['m_w_ff2'], 'm_b_ff2': out['m_b_ff2'], 'm_ln2_g': out['m_ln2_g'], 'm_ln2_b': out['m_ln2_b'], 'v_ln_in_g': out['v_ln_in_g'], 'v_ln_in_b': out['v_ln_in_b'], 'v_ada_w': out['v_ada_w'], 'v_ada_b': out['v_ada_b'], 'v_w_in': out['v_w_in'], 'v_conv_w': out['v_conv_w'], 'v_conv_b': out['v_conv_b'], 'v_dt_bias': out['v_dt_bias'], 'v_a_log': out['v_a_log'], 'v_d_skip': out['v_d_skip'], 'v_ssd_norm_w': out['v_ssd_norm_w'], 'v_attn_sinks': out['v_attn_sinks'], 'v_w_out': out['v_w_out'], 'v_ln1_g': out['v_ln1_g'], 'v_ln1_b': out['v_ln1_b'], 'v_w_ff1': out['v_w_ff1'], 'v_b_ff1': out['v_b_ff1'], 'v_w_ff2': out['v_w_ff2'], 'v_b_ff2': out['v_b_ff2'], 'v_ln2_g': out['v_ln2_g'], 'v_ln2_b': out['v_ln2_b']}


def _loss(weights, diff, rest, loss_target):
    with _jax.named_scope("forward"):
        args = {**rest, TWIN_DIFF_INPUT: diff, **{k: w.astype(_WEIGHT_DTYPES[k]) for k, w in weights.items()}}
        y = _forward(args)
    with _jax.named_scope("loss_head"):
        err = _jnp.square(y.astype(_jnp.float32) - loss_target)
        return 0.5 * _jnp.sum(_jnp.mean(err, axis=-1)) if err.ndim else 0.5 * err


def _adamw(w, g, m, v):
    m = ADAM_B1 * m + (1.0 - ADAM_B1) * g
    v = ADAM_B2 * v + (1.0 - ADAM_B2) * _jnp.square(g)
    m_hat = m / (1.0 - ADAM_B1 ** ADAM_STEP)
    v_hat = v / (1.0 - ADAM_B2 ** ADAM_STEP)
    delta = -ADAM_LR * (m_hat / (_jnp.sqrt(v_hat) + ADAM_EPS) + ADAM_WD * w)
    return delta, m, v


def reference(x, c, ln_in_g, ln_in_b, ada_w, ada_b, w_in, conv_w, conv_b, dt_bias, a_log, d_skip, ssd_norm_w, attn_sinks, w_out, ln1_g, ln1_b, w_ff1, b_ff1, w_ff2, b_ff2, ln2_g, ln2_b, loss_target, m_ln_in_g, m_ln_in_b, m_ada_w, m_ada_b, m_w_in, m_conv_w, m_conv_b, m_dt_bias, m_a_log, m_d_skip, m_ssd_norm_w, m_attn_sinks, m_w_out, m_ln1_g, m_ln1_b, m_w_ff1, m_b_ff1, m_w_ff2, m_b_ff2, m_ln2_g, m_ln2_b, v_ln_in_g, v_ln_in_b, v_ada_w, v_ada_b, v_w_in, v_conv_w, v_conv_b, v_dt_bias, v_a_log, v_d_skip, v_ssd_norm_w, v_attn_sinks, v_w_out, v_ln1_g, v_ln1_b, v_w_ff1, v_b_ff1, v_w_ff2, v_b_ff2, v_ln2_g, v_ln2_b):
    given = dict(x=x, c=c, ln_in_g=ln_in_g, ln_in_b=ln_in_b, ada_w=ada_w, ada_b=ada_b, w_in=w_in, conv_w=conv_w, conv_b=conv_b, dt_bias=dt_bias, a_log=a_log, d_skip=d_skip, ssd_norm_w=ssd_norm_w, attn_sinks=attn_sinks, w_out=w_out, ln1_g=ln1_g, ln1_b=ln1_b, w_ff1=w_ff1, b_ff1=b_ff1, w_ff2=w_ff2, b_ff2=b_ff2, ln2_g=ln2_g, ln2_b=ln2_b, loss_target=loss_target, m_ln_in_g=m_ln_in_g, m_ln_in_b=m_ln_in_b, m_ada_w=m_ada_w, m_ada_b=m_ada_b, m_w_in=m_w_in, m_conv_w=m_conv_w, m_conv_b=m_conv_b, m_dt_bias=m_dt_bias, m_a_log=m_a_log, m_d_skip=m_d_skip, m_ssd_norm_w=m_ssd_norm_w, m_attn_sinks=m_attn_sinks, m_w_out=m_w_out, m_ln1_g=m_ln1_g, m_ln1_b=m_ln1_b, m_w_ff1=m_w_ff1, m_b_ff1=m_b_ff1, m_w_ff2=m_w_ff2, m_b_ff2=m_b_ff2, m_ln2_g=m_ln2_g, m_ln2_b=m_ln2_b, v_ln_in_g=v_ln_in_g, v_ln_in_b=v_ln_in_b, v_ada_w=v_ada_w, v_ada_b=v_ada_b, v_w_in=v_w_in, v_conv_w=v_conv_w, v_conv_b=v_conv_b, v_dt_bias=v_dt_bias, v_a_log=v_a_log, v_d_skip=v_d_skip, v_ssd_norm_w=v_ssd_norm_w, v_attn_sinks=v_attn_sinks, v_w_out=v_w_out, v_ln1_g=v_ln1_g, v_ln1_b=v_ln1_b, v_w_ff1=v_w_ff1, v_b_ff1=v_b_ff1, v_w_ff2=v_w_ff2, v_b_ff2=v_b_ff2, v_ln2_g=v_ln2_g, v_ln2_b=v_ln2_b)
    weights = {n: given[n] for n in TWIN_WEIGHTS}
    shared = {n: given[n] for n in SHARED_INPUTS}
    per_example = {n: given[n] for n in ['x', 'c']}
    grad_fn = _jax.value_and_grad(_loss, argnums=(0, 1))

    def one_microbatch(ex, loss_target):
        ex = dict(ex)
        diff = ex.pop(TWIN_DIFF_INPUT)
        return grad_fn(weights, diff, {**shared, **ex}, loss_target)

    if N_MICROBATCH == 1:
        loss, (grad_w, grad_x) = one_microbatch(per_example, given["loss_target"])
    else:
        def body(carry, xs):
            loss_sum, grad_sum = carry
            l_k, (gw_k, gx_k) = one_microbatch(xs[0], xs[1])
            with _jax.named_scope("update"):
                return (loss_sum + l_k, _jax.tree.map(_jnp.add, grad_sum, gw_k)), gx_k

        init = (_jnp.zeros((), _jnp.float32), _jax.tree.map(_jnp.zeros_like, weights))
        (loss, grad_w), grad_x = _jax.lax.scan(body, init, (per_example, given["loss_target"]))
    with _jax.named_scope("update"):
        delta_w, new_m, new_v = {}, {}, {}
        for n in TWIN_WEIGHTS:
            delta_w[n], new_m[n], new_v[n] = _adamw(weights[n], grad_w[n], given["m_" + n], given["v_" + n])
    return (loss, grad_x, *[grad_w[n] for n in TWIN_WEIGHTS], *[delta_w[n] for n in TWIN_WEIGHTS],
            *[new_m[n] for n in TWIN_WEIGHTS], *[new_v[n] for n in TWIN_WEIGHTS])
```

```python
import functools
import math

import numpy as np
import jax
import jax.numpy as jnp
from jax import lax
from jax.experimental import pallas as pl
from jax.experimental.pallas import tpu as pltpu

F32 = jnp.float32
BF16 = jnp.bfloat16
MESH = pl.DeviceIdType.MESH

N_DEV = 8
N_CHIP = 4
HEAD_DIM = 64
SSD_GROUPS = 2
SSD_STATE = 128
CHUNK = 128
CONV_K = 4
GQA = 8
LANE = 128
HALO = 8
LN_EPS = 1e-5
RMS_EPS = 1e-5
NEG = -1e30
ADAM_LR, ADAM_B1, ADAM_B2, ADAM_EPS, ADAM_WD, ADAM_STEP = 0.001, 0.9, 0.999, 1e-08, 0.01, 10
V7X_VMEM_BYTES = 64 * 1024 * 1024
VMEM_LIMIT = V7X_VMEM_BYTES - 8 * 1024 * 1024
HI = lax.Precision.HIGHEST


def _alibi_slopes(n):
    def pow2(m):
        start = 2.0 ** (-8.0 / m)
        return [start ** (i + 1) for i in range(m)]
    if math.log2(n).is_integer():
        s = pow2(n)
    else:
        c = 2 ** math.floor(math.log2(n))
        s = pow2(c) + pow2(2 * c)[0::2][: n - c]
    return [float(v) for v in np.array(s, dtype=np.float32)]


def _dot(a, b):
    return jnp.dot(a, b, preferred_element_type=F32)


def _dot_nt(a, b):
    return lax.dot_general(a, b, (((1,), (1,)), ((), ())), preferred_element_type=F32)


def _dot_tn(a, b):
    return lax.dot_general(a, b, (((0,), (0,)), ((), ())), preferred_element_type=F32)


def _dot_hi(a, b):
    return jnp.dot(a, b, precision=HI, preferred_element_type=F32)


def _sigmoid(x):
    return 1.0 / (1.0 + jnp.exp(-x))


def _softplus(x):
    return jnp.maximum(x, 0.0) + jnp.log(1.0 + jnp.exp(-jnp.abs(x)))


def _mean(x):
    return jnp.mean(x, axis=-1, keepdims=True)


def _ln_fwd(x):
    xc = x - _mean(x)
    rstd = lax.rsqrt(_mean(xc * xc) + LN_EPS)
    return xc * rstd, rstd


def _ln_bwd(dxhat, xhat, rstd):
    return rstd * (dxhat - _mean(dxhat) - xhat * _mean(dxhat * xhat))


def _colsum(x):
    return jnp.sum(x, axis=0, keepdims=True)


def _params(sem):
    return pltpu.CompilerParams(dimension_semantics=sem, vmem_limit_bytes=VMEM_LIMIT)


def _tile(i_map_cols, tm):
    return pl.BlockSpec((tm, i_map_cols), lambda i: (i, 0))


def _res(shape):
    return pl.BlockSpec(shape, lambda *_: (0,) * len(shape), pipeline_mode=pl.Buffered(1))


def _acc(shape):
    return pl.BlockSpec(shape, lambda *_: (0,) * len(shape))


def _S(shape, dtype):
    return jax.ShapeDtypeStruct(shape, dtype)


def _my_pos():
    return lax.axis_index("x"), lax.axis_index("y"), lax.axis_index("c")


def _peer(pos, k):
    x, y, c = pos
    px = 1 - x if k & 4 else x
    py = 1 - y if k & 2 else y
    pc = 1 - c if k & 1 else c
    return (px, py, pc)


def _lin(p):
    return 4 * p[0] + 2 * p[1] + p[2]


def _mod_fwd(c_loc, ada_w_loc, ada_b_cols):
    D = c_loc.shape[1]
    C6 = ada_w_loc.shape[1]

    def body(c_ref, w_ref, b_ref, cs_ref, mod_ref, call_ref, modp_ref, ssem, rsem):
        pos = _my_pos()
        me = _lin(pos)
        call_ref[me] = c_ref[...]
        sends = []
        for k in range(1, N_DEV):
            cp = pltpu.make_async_remote_copy(src_ref=c_ref, dst_ref=call_ref.at[me], send_sem=ssem.at[k - 1],
                                              recv_sem=rsem.at[k - 1], device_id=_peer(pos, k), device_id_type=MESH)
            cp.start()
            sends.append(cp)
        for k in range(1, N_DEV):
            src = _lin(_peer(pos, k))
            pltpu.make_async_remote_copy(src_ref=c_ref, dst_ref=call_ref.at[src], send_sem=ssem.at[k - 1],
                                         recv_sem=rsem.at[k - 1], device_id=pos, device_id_type=MESH).wait_recv()
        for cp in sends:
            cp.wait_send()
        call = jnp.concatenate([call_ref[b] for b in range(N_DEV)], axis=0)
        cs = call * _sigmoid(call)
        cs_ref[...] = cs
        modp = _dot(cs.astype(BF16), w_ref[...].astype(BF16)) + b_ref[...]
        for b in range(N_DEV):
            modp_ref[b] = modp[b:b + 1, :]
        mod_ref[me] = modp_ref[me]
        sends = []
        for k in range(1, N_DEV):
            peer = _peer(pos, k)
            cp = pltpu.make_async_remote_copy(src_ref=modp_ref.at[_lin(peer)], dst_ref=mod_ref.at[me],
                                              send_sem=ssem.at[N_DEV - 2 + k], recv_sem=rsem.at[N_DEV - 2 + k],
                                              device_id=peer, device_id_type=MESH)
            cp.start()
            sends.append(cp)
        for k in range(1, N_DEV):
            src = _lin(_peer(pos, k))
            pltpu.make_async_remote_copy(src_ref=modp_ref.at[src], dst_ref=mod_ref.at[src],
                                         send_sem=ssem.at[N_DEV - 2 + k], recv_sem=rsem.at[N_DEV - 2 + k],
                                         device_id=pos, device_id_type=MESH).wait_recv()
        for cp in sends:
            cp.wait_send()

    vm = pl.BlockSpec(memory_space=pltpu.VMEM)
    return pl.pallas_call(
        body, name="mod_fwd",
        out_shape=(_S((N_DEV, D), F32), _S((N_DEV, 1, C6), F32)),
        in_specs=[vm, vm, vm], out_specs=(vm, vm),
        scratch_shapes=[pltpu.VMEM((N_DEV, 1, D), F32), pltpu.VMEM((N_DEV, 1, C6), F32),
                        pltpu.SemaphoreType.DMA((2 * (N_DEV - 1),)), pltpu.SemaphoreType.DMA((2 * (N_DEV - 1),))],
        compiler_params=pltpu.CompilerParams(vmem_limit_bytes=VMEM_LIMIT),
    )(c_loc, ada_w_loc, ada_b_cols)


def _small_gather_sum(pack):
    P8 = pack.shape[1]

    def body(p_ref, gat_ref, sum_ref, ssem, rsem):
        pos = _my_pos()
        me = _lin(pos)
        gat_ref[me] = p_ref[...]
        sends = []
        for k in range(1, N_DEV):
            cp = pltpu.make_async_remote_copy(src_ref=p_ref, dst_ref=gat_ref.at[me], send_sem=ssem.at[k - 1],
                                              recv_sem=rsem.at[k - 1], device_id=_peer(pos, k), device_id_type=MESH)
            cp.start()
            sends.append(cp)
        for k in range(1, N_DEV):
            src = _lin(_peer(pos, k))
            pltpu.make_async_remote_copy(src_ref=p_ref, dst_ref=gat_ref.at[src], send_sem=ssem.at[k - 1],
                                         recv_sem=rsem.at[k - 1], device_id=pos, device_id_type=MESH).wait_recv()
        for cp in sends:
            cp.wait_send()
        acc = gat_ref[0]
        for j in range(1, N_DEV):
            acc = acc + gat_ref[j]
        sum_ref[...] = acc

    vm = pl.BlockSpec(memory_space=pltpu.VMEM)
    return pl.pallas_call(
        body, name="small_gather_sum",
        out_shape=(_S((N_DEV, 8, P8), F32), _S((8, P8), F32)),
        in_specs=[vm], out_specs=(vm, vm),
        scratch_shapes=[pltpu.SemaphoreType.DMA((N_DEV - 1,)), pltpu.SemaphoreType.DMA((N_DEV - 1,))],
        compiler_params=pltpu.CompilerParams(vmem_limit_bytes=VMEM_LIMIT),
    )(pack)


def _ag_weights(shards):
    n = len(shards)

    def body(*refs):
        ins, outs = refs[:n], refs[n:2 * n]
        ssem, rsem, lsem = refs[2 * n:]
        x, y, c = pos = _my_pos()
        me = _lin(pos)
        sib = (x, y, 1 - c)
        chips = [(1 - x, y), (x, 1 - y), (1 - x, 1 - y)]

        def copy(a, k, block, to, src=None):
            return pltpu.make_async_remote_copy(
                src_ref=outs[a].at[block] if src is None else src, dst_ref=outs[a].at[block],
                send_sem=ssem.at[a * 7 + k], recv_sem=rsem.at[a * 7 + k], device_id=to, device_id_type=MESH)

        local = [pltpu.make_async_copy(ins[a], outs[a].at[me], lsem.at[a]) for a in range(n)]
        for cp in local:
            cp.start()
        first = []
        for a in range(n):
            first.append(copy(a, 0, me, sib, src=ins[a]))
            first += [copy(a, 1 + j, me, (*chip, c), src=ins[a]) for j, chip in enumerate(chips)]
        for cp in first:
            cp.start()
        passed = []
        for a in range(n):
            for j, chip in enumerate(chips):
                blk = _lin((*chip, c))
                copy(a, 1 + j, blk, pos).wait_recv()
                cp = copy(a, 4 + j, blk, sib)
                cp.start()
                passed.append(cp)
        for a in range(n):
            copy(a, 0, _lin(sib), pos).wait_recv()
            for j, chip in enumerate(chips):
                copy(a, 4 + j, _lin((*chip, 1 - c)), pos).wait_recv()
        for cp in first + passed:
            cp.wait_send()
        for cp in local:
            cp.wait()

    hbm = pl.BlockSpec(memory_space=pl.ANY)
    return pl.pallas_call(
        body, name="ag_weights",
        out_shape=tuple(_S((N_DEV,) + s.shape, s.dtype) for s in shards),
        in_specs=[hbm] * n, out_specs=tuple([hbm] * n),
        scratch_shapes=[pltpu.SemaphoreType.DMA((7 * n,)), pltpu.SemaphoreType.DMA((7 * n,)),
                        pltpu.SemaphoreType.DMA((n,))],
    )(*shards)


def _rs_d2d(blocked):
    n = len(blocked)

    def body(*refs):
        ins, outs = refs[:n], refs[n:2 * n]
        ssem, rsem = refs[2 * n:]
        x, y, c = pos = _my_pos()
        sib = (x, y, 1 - c)
        cps = []
        for a in range(n):
            for j in range(N_CHIP):
                cp = pltpu.make_async_remote_copy(
                    src_ref=ins[a].at[2 * j + (1 - c)], dst_ref=outs[a].at[j], send_sem=ssem.at[a * N_CHIP + j],
                    recv_sem=rsem.at[a * N_CHIP + j], device_id=sib, device_id_type=MESH)
                cp.start()
                cps.append(cp)
        for cp in cps:
            cp.wait_recv()
        for cp in cps:
            cp.wait_send()

    hbm = pl.BlockSpec(memory_space=pl.ANY)
    return pl.pallas_call(
        body, name="rs_d2d",
        out_shape=tuple(_S((N_CHIP,) + b.shape[1:], b.dtype) for b in blocked),
        in_specs=[hbm] * n, out_specs=tuple([hbm] * n),
        scratch_shapes=[pltpu.SemaphoreType.DMA((N_CHIP * n,)), pltpu.SemaphoreType.DMA((N_CHIP * n,))],
    )(*blocked)


def _rs_ici(pairs):
    n = len(pairs)

    def body(*refs):
        ins, outs = refs[:n], refs[n:2 * n]
        ssem, rsem, lsem = refs[2 * n:]
        x, y, c = pos = _my_pos()
        mychip = 2 * x + y
        local = [pltpu.make_async_copy(ins[a].at[mychip], outs[a].at[mychip], lsem.at[a]) for a in range(n)]
        for cp in local:
            cp.start()
        sends = []
        for a in range(n):
            for k in range(1, N_CHIP):
                tx, ty, _ = _peer(pos, 2 * k)
                cp = pltpu.make_async_remote_copy(
                    src_ref=ins[a].at[2 * tx + ty], dst_ref=outs[a].at[mychip], send_sem=ssem.at[a * 3 + k - 1],
                    recv_sem=rsem.at[a * 3 + k - 1], device_id=(tx, ty, c), device_id_type=MESH)
                cp.start()
                sends.append(cp)
        for a in range(n):
            for k in range(1, N_CHIP):
                tx, ty, _ = _peer(pos, 2 * k)
                pltpu.make_async_remote_copy(
                    src_ref=ins[a].at[2 * tx + ty], dst_ref=outs[a].at[2 * tx + ty], send_sem=ssem.at[a * 3 + k - 1],
                    recv_sem=rsem.at[a * 3 + k - 1], device_id=pos, device_id_type=MESH).wait_recv()
        for cp in sends:
            cp.wait_send()
        for cp in local:
            cp.wait()

    hbm = pl.BlockSpec(memory_space=pl.ANY)
    return pl.pallas_call(
        body, name="rs_ici",
        out_shape=tuple(_S(p.shape, p.dtype) for p in pairs),
        in_specs=[hbm] * n, out_specs=tuple([hbm] * n),
        scratch_shapes=[pltpu.SemaphoreType.DMA((3 * n,)), pltpu.SemaphoreType.DMA((3 * n,)),
                        pltpu.SemaphoreType.DMA((n,))],
    )(*pairs)


def _row_tile(R, itemsize_rows=16, cap=256):
    t = min(R, cap)
    while R % t or t % itemsize_rows:
        t -= itemsize_rows
    return t


def _pair_sum(blocked, recv, core):
    _, R, C = blocked.shape
    tr = _row_tile(R)

    def body(ids_ref, a_ref, b_ref, o_ref):
        del ids_ref
        o_ref[...] = (a_ref[...] + b_ref[...]).astype(BF16)

    gs = pltpu.PrefetchScalarGridSpec(
        num_scalar_prefetch=1, grid=(N_CHIP, R // tr),
        in_specs=[pl.BlockSpec((1, tr, C), lambda j, r, ids: (2 * j + ids[0], r, 0)),
                  pl.BlockSpec((1, tr, C), lambda j, r, ids: (j, r, 0))],
        out_specs=pl.BlockSpec((1, tr, C), lambda j, r, ids: (j, r, 0)))
    return pl.pallas_call(body, name="pair_sum", grid_spec=gs, out_shape=_S((N_CHIP, R, C), BF16),
                          compiler_params=_params(("arbitrary", "arbitrary")))(core, blocked, recv)


def _adamw_math(w, g, m, v):
    m2 = ADAM_B1 * m + (1.0 - ADAM_B1) * g
    v2 = ADAM_B2 * v + (1.0 - ADAM_B2) * (g * g)
    m_hat = m2 / (1.0 - ADAM_B1 ** ADAM_STEP)
    v_hat = v2 / (1.0 - ADAM_B2 ** ADAM_STEP)
    delta = -ADAM_LR * (m_hat / (jnp.sqrt(v_hat) + ADAM_EPS) + ADAM_WD * w)
    return delta, m2, v2


def _sum_adamw(parts, w, m, v):
    R, C = w.shape
    tr = _row_tile(R)

    def body(p_ref, w_ref, m_ref, v_ref, g_ref, d_ref, m2_ref, v2_ref):
        g = p_ref[0].astype(F32)
        for j in range(1, N_CHIP):
            g = g + p_ref[j].astype(F32)
        g_ref[...] = g
        d_ref[...], m2_ref[...], v2_ref[...] = _adamw_math(w_ref[...], g, m_ref[...], v_ref[...])

    t = pl.BlockSpec((tr, C), lambda r: (r, 0))
    return pl.pallas_call(
        body, name="sum_adamw", grid=(R // tr,),
        in_specs=[pl.BlockSpec((N_CHIP, tr, C), lambda r: (0, r, 0)), t, t, t], out_specs=(t, t, t, t),
        out_shape=tuple(_S((R, C), F32) for _ in range(4)), compiler_params=_params(("arbitrary",)))(parts, w, m, v)


def _adamw(w, g, m, v):
    R, C = w.shape
    tr = _row_tile(R, 8)

    def body(w_ref, g_ref, m_ref, v_ref, d_ref, m2_ref, v2_ref):
        d_ref[...], m2_ref[...], v2_ref[...] = _adamw_math(w_ref[...], g_ref[...], m_ref[...], v_ref[...])

    t = pl.BlockSpec((tr, C), lambda r: (r, 0))
    return pl.pallas_call(body, name="adamw", grid=(R // tr,), in_specs=[t, t, t, t], out_specs=(t, t, t),
                          out_shape=tuple(_S((R, C), F32) for _ in range(3)),
                          compiler_params=_params(("arbitrary",)))(w, g, m, v)


def _ada_grad_adamw(cs16, dmod16, w, m, v):
    D, C6 = w.shape
    tr = _row_tile(D, 8, 256)

    def body(cs_ref, dm_ref, w_ref, m_ref, v_ref, g_ref, d_ref, m2_ref, v2_ref):
        g = _dot_tn(cs_ref[...].astype(BF16), dm_ref[...].astype(BF16))
        g_ref[...] = g
        d_ref[...], m2_ref[...], v2_ref[...] = _adamw_math(w_ref[...], g, m_ref[...], v_ref[...])

    t = pl.BlockSpec((tr, C6), lambda r: (r, 0))
    return pl.pallas_call(
        body, name="ada_grad_adamw", grid=(D // tr,),
        in_specs=[pl.BlockSpec((16, tr), lambda r: (0, r)), _acc((16, C6)), t, t, t], out_specs=(t, t, t, t),
        out_shape=tuple(_S((D, C6), F32) for _ in range(4)), compiler_params=_params(("arbitrary",)))(cs16, dmod16, w, m, v)


def _pick(n, cands):
    for c in cands:
        if n % c == 0:
            return c
    return n


def _matmul_tn(a, b, name, square_a=False):
    L, K = a.shape
    N = b.shape[1]
    bk = _pick(K, (1024, 512, 256, 128))
    bn = _pick(N, (1024, 768, 512, 256, 128))
    tl = _pick(L, (1024, 512, 256, 128))
    n_l = L // tl

    def body(a_ref, b_ref, o_ref):
        @pl.when(pl.program_id(2) == 0)
        def _():
            o_ref[...] = jnp.zeros_like(o_ref)
        av = a_ref[...]
        if square_a:
            av = av.astype(F32)
            av = av * av
        o_ref[...] += _dot_tn(av.astype(BF16), b_ref[...].astype(BF16))

    return pl.pallas_call(
        body, name=name, grid=(K // bk, N // bn, n_l),
        in_specs=[pl.BlockSpec((tl, bk), lambda k, n, l: (l, k)), pl.BlockSpec((tl, bn), lambda k, n, l: (l, n))],
        out_specs=pl.BlockSpec((bk, bn), lambda k, n, l: (k, n)), out_shape=_S((K, N), F32),
        compiler_params=_params(("arbitrary", "arbitrary", "arbitrary")))(a, b)


def _ln_in_proj(x, g, b, sc, sh, w_pad, dims):
    L, D = x.shape
    W, CD, AW, KVW2 = dims["W"], dims["CD"], dims["AW"], dims["KVW2"]
    NP = w_pad.shape[1]
    tm = _pick(L, (256, 128))
    o_z, o_xbc, o_q, o_kv, o_dt = 0, W, W + CD, W + CD + AW, W + CD + AW + KVW2

    def body(x_ref, g_ref, b_ref, sc_ref, sh_ref, w_ref, xhat_ref, rstd_ref, u1_ref, z_ref, xbc_ref, q_ref, kv_ref, dt_ref):
        xhat, rstd = _ln_fwd(x_ref[...])
        xhat_ref[...] = xhat
        rstd_ref[...] = rstd
        h0 = xhat * g_ref[...] + b_ref[...]
        u1 = (h0 * (1.0 + sc_ref[...]) + sh_ref[...]).astype(BF16)
        u1_ref[...] = u1
        z_ref[...] = _dot(u1, w_ref[:, o_z:o_xbc])
        xbc_ref[...] = _dot(u1, w_ref[:, o_xbc:o_q])
        q_ref[...] = _dot(u1, w_ref[:, o_q:o_kv]).astype(BF16)
        kv_ref[...] = _dot(u1, w_ref[:, o_kv:o_dt]).astype(BF16)
        dt_ref[...] = _dot(u1, w_ref[:, o_dt:NP])

    v = _acc((1, D))
    return pl.pallas_call(
        body, name="ln_in_proj", grid=(L // tm,),
        in_specs=[_tile(D, tm), v, v, v, v, _res((D, NP))],
        out_specs=(_tile(D, tm), _tile(1, tm), _tile(D, tm), _tile(W, tm), _tile(CD, tm), _tile(AW, tm),
                   _tile(KVW2, tm), _tile(LANE, tm)),
        out_shape=(_S((L, D), F32), _S((L, 1), F32), _S((L, D), BF16), _S((L, W), F32), _S((L, CD), F32),
                   _S((L, AW), BF16), _S((L, KVW2), BF16), _S((L, LANE), F32)),
        compiler_params=_params(("arbitrary",)))(x, g, b, sc, sh, w_pad)


def _conv_act(cur_ref, prev_ref, cw_ref, cb_ref, ext_ref, first):
    T = cur_ref.shape[0]
    ext_ref[0:HALO, :] = jnp.where(first, 0.0, prev_ref[...])
    ext_ref[HALO:HALO + T, :] = cur_ref[...]
    pre = cb_ref[...] + cw_ref[0:1, :] * ext_ref[HALO - 3:HALO - 3 + T, :]
    for k in range(1, CONV_K):
        pre = pre + cw_ref[k:k + 1, :] * ext_ref[HALO - 3 + k:HALO - 3 + k + T, :]
    return pre * _sigmoid(pre), pre


def _tri(T, upper=False):
    r = lax.broadcasted_iota(jnp.int32, (T, T), 0)
    c = lax.broadcasted_iota(jnp.int32, (T, T), 1)
    return (r <= c) if upper else (r >= c)


def _conv_ssd(xbc, dt_raw, z, cw, cb, dtb, alog, dsk, nw, dims):
    L, CD = xbc.shape
    W, H, G, N = dims["W"], dims["H"], SSD_GROUPS, SSD_STATE
    T = CHUNK
    R = H // G
    GW = W // G
    nc = L // T
    HP = H * HEAD_DIM

    def body(xbc_ref, prev_ref, dt_ref, z_ref, cw_ref, cb_ref, dtb_ref, alog_ref, dsk_ref, nw_ref,
             y_ref, yn_ref, sp_ref, ext_ref, s_ref, ybuf_ref):
        i = pl.program_id(0)

        @pl.when(i == 0)
        def _():
            s_ref[...] = jnp.zeros_like(s_ref)

        act, _ = _conv_act(xbc_ref, prev_ref, cw_ref, cb_ref, ext_ref, i == 0)
        xs = act[:, :W]
        dt = _softplus(dt_ref[...] + dtb_ref[...])
        a = dt * (-jnp.exp(alog_ref[...]))
        low = _tri(T)
        acum = _dot_hi(low.astype(F32), a)
        acum_t = _dot_hi(a.T, _tri(T, upper=True).astype(F32))
        for g in range(G):
            bg = act[:, W + g * N:W + (g + 1) * N]
            cg = act[:, W + G * N + g * N:W + G * N + (g + 1) * N]
            bgb, cgb = bg.astype(BF16), cg.astype(BF16)
            cb_g = _dot_nt(cgb, bgb)
            for r in range(R):
                h = g * R + r
                hs = slice(h * HEAD_DIM, (h + 1) * HEAD_DIM)
                col = acum[:, h:h + 1]
                last = acum[T - 1:T, h:h + 1]
                lm = jnp.where(low, jnp.exp(col - acum_t[h:h + 1, :]), 0.0)
                xb = (xs[:, hs] * dt[:, h:h + 1]).astype(BF16)
                sp = s_ref[hs, :]
                y_h = _dot((cb_g * lm).astype(BF16), xb) + _dot_nt(cgb, sp.astype(BF16)) * jnp.exp(col)
                ybuf_ref[:, hs] = y_h
                bd = (bg * jnp.exp(last - col)).astype(BF16)
                sp_ref[0, hs, :] = sp
                s_ref[hs, :] = sp * jnp.exp(last) + _dot_tn(xb, bd)
        y = ybuf_ref[...] + dsk_ref[...] * xs
        y_ref[...] = y
        zz = z_ref[...]
        hh = y * (zz * _sigmoid(zz))
        for g in range(G):
            gs = slice(g * GW, (g + 1) * GW)
            hg = hh[:, gs]
            yn_ref[:, gs] = (hg * lax.rsqrt(_mean(hg * hg) + RMS_EPS) * nw_ref[:, gs]).astype(BF16)

    return pl.pallas_call(
        body, name="conv_ssd", grid=(nc,),
        in_specs=[_tile(CD, T), pl.BlockSpec((HALO, CD), lambda i: (jnp.maximum(i * (T // HALO) - 1, 0), 0)),
                  _tile(LANE, T), _tile(W, T), _acc((CONV_K, CD)), _acc((1, CD)), _acc((1, LANE)), _acc((1, LANE)),
                  _acc((1, W)), _acc((1, W))],
        out_specs=(_tile(W, T), _tile(W, T), pl.BlockSpec((1, HP, N), lambda i: (i, 0, 0))),
        out_shape=(_S((L, W), F32), _S((L, W), BF16), _S((nc, HP, N), F32)),
        scratch_shapes=[pltpu.VMEM((T + HALO, CD), F32), pltpu.VMEM((HP, N), F32), pltpu.VMEM((T, W), F32)],
        compiler_params=_params(("arbitrary",)))(xbc, xbc, dt_raw, z, cw, cb, dtb, alog, dsk, nw)


def _attn_mask(T, i):
    r = lax.broadcasted_iota(jnp.int32, (T, 2 * T), 0)
    c = lax.broadcasted_iota(jnp.int32, (T, 2 * T), 1)
    dist = r + T - c
    valid = (dist >= 0) & (dist < CHUNK) & ((c >= T) | (i > 0))
    return dist.astype(F32), valid


def _attn_probs(qh, kk, dist, valid, slope, sink):
    s = _dot_nt(qh, kk) * (HEAD_DIM ** -0.5) - slope * dist
    s = jnp.where(valid, s, NEG)
    m = jnp.maximum(jnp.max(s, axis=-1, keepdims=True), sink)
    p = jnp.exp(s - m)
    e_sink = jnp.exp(sink - m)
    inv = 1.0 / (jnp.sum(p, axis=-1, keepdims=True) + e_sink)
    return p * inv, e_sink * inv


def _swa_fwd(q, kv, sinks, dims):
    L, AW = q.shape
    KV, KVW2 = dims["KV"], dims["KVW2"]
    T = CHUNK
    nb = L // T
    slopes = _alibi_slopes(dims["AH"])

    def body(q_ref, kvc_ref, kvp_ref, sink_ref, o_ref):
        i = pl.program_id(0)
        dist, valid = _attn_mask(T, i)
        for g in range(KV):
            ks = slice(g * HEAD_DIM, (g + 1) * HEAD_DIM)
            vs = slice(KV * HEAD_DIM + g * HEAD_DIM, KV * HEAD_DIM + (g + 1) * HEAD_DIM)
            kk = jnp.concatenate([kvp_ref[:, ks], kvc_ref[:, ks]], axis=0)
            vv = jnp.concatenate([kvp_ref[:, vs], kvc_ref[:, vs]], axis=0)
            for r in range(GQA):
                h = g * GQA + r
                hs = slice(h * HEAD_DIM, (h + 1) * HEAD_DIM)
                p, _ = _attn_probs(q_ref[:, hs], kk, dist, valid, slopes[h], sink_ref[h])
                o_ref[:, hs] = _dot(p.astype(BF16), vv).astype(BF16)

    return pl.pallas_call(
        body, name="swa_fwd", grid=(nb,),
        in_specs=[_tile(AW, T), _tile(KVW2, T), pl.BlockSpec((T, KVW2), lambda i: (jnp.maximum(i - 1, 0), 0)),
                  pl.BlockSpec(memory_space=pltpu.SMEM)],
        out_specs=_tile(AW, T), out_shape=_S((L, AW), BF16),
        compiler_params=_params(("arbitrary",)))(q, kv, kv, sinks)


def _out_proj_ln1(yn, o, w_out, xhat0, vecs, alpha):
    L, W = yn.shape
    D = xhat0.shape[1]
    MIX = w_out.shape[0]
    tm = _pick(L, (256, 128))

    def body(yn_ref, o_ref, w_ref, xh_ref, v_ref, mix_ref, xhat1_ref, rstd1_ref, u2_ref):
        mix = _dot(yn_ref[...], w_ref[0:W, :]) + _dot(o_ref[...], w_ref[W:MIX, :])
        mix_ref[...] = mix
        h0 = xh_ref[...] * v_ref[0:1, :] + v_ref[1:2, :]
        xhat1, rstd1 = _ln_fwd(alpha * h0 + (1.0 + v_ref[2:3, :]) * mix)
        xhat1_ref[...] = xhat1
        rstd1_ref[...] = rstd1
        h1 = xhat1 * v_ref[3:4, :] + v_ref[4:5, :]
        u2_ref[...] = (h1 * (1.0 + v_ref[5:6, :]) + v_ref[6:7, :]).astype(BF16)

    return pl.pallas_call(
        body, name="out_proj_ln1", grid=(L // tm,),
        in_specs=[_tile(W, tm), _tile(MIX - W, tm), _res((MIX, D)), _tile(D, tm), _acc((8, D))],
        out_specs=(_tile(D, tm), _tile(D, tm), _tile(1, tm), _tile(D, tm)),
        out_shape=(_S((L, D), F32), _S((L, D), F32), _S((L, 1), F32), _S((L, D), BF16)),
        compiler_params=_params(("arbitrary",)))(yn, o, w_out, xhat0, vecs)


def _mlp_loss(u2, w1, w2, xhat1, tgt, vecs, b1, alpha):
    L, D = xhat1.shape
    FF = w1.shape[1]
    tm = _pick(L, (256, 128))
    fc = _pick(FF, (512, 256, 128))

    def body(u2_ref, w1_ref, w2_ref, xh_ref, t_ref, v_ref, b1_ref, rr_ref, dr2_ref, acc_ref, loss_ref):
        @pl.when(pl.program_id(0) == 0)
        def _():
            acc_ref[...] = jnp.zeros_like(acc_ref)
            loss_ref[...] = jnp.zeros_like(loss_ref)

        u2 = u2_ref[...]
        f = jnp.zeros((tm, D), F32) + v_ref[5:6, :]
        for j in range(FF // fc):
            cs = slice(j * fc, (j + 1) * fc)
            rr = jnp.maximum(_dot(u2, w1_ref[:, cs]) + b1_ref[:, cs], 0.0)
            rr_ref[:, cs] = rr.astype(BF16)
            f = f + _dot((rr * rr).astype(BF16), w2_ref[cs, :])
        xhat1 = xh_ref[...]
        h1 = xhat1 * v_ref[0:1, :] + v_ref[1:2, :]
        xhat2, rstd2 = _ln_fwd(alpha * h1 + (1.0 + v_ref[2:3, :]) * f)
        e = xhat2 * v_ref[3:4, :] + v_ref[4:5, :] - t_ref[...]
        loss_ref[...] += 0.5 * jnp.sum(_mean(e * e))
        dy = e * (1.0 / D)
        dr2 = _ln_bwd(dy * v_ref[3:4, :], xhat2, rstd2)
        dr2_ref[...] = dr2
        acc_ref[0:1, :] += _colsum(dy * xhat2)
        acc_ref[1:2, :] += _colsum(dy)
        acc_ref[2:3, :] += _colsum(dr2 * f)

    return pl.pallas_call(
        body, name="mlp_loss", grid=(L // tm,),
        in_specs=[_tile(D, tm), _res((D, FF)), _res((FF, D)), _tile(D, tm), _tile(D, tm), _acc((8, D)), _acc((1, FF))],
        out_specs=(_tile(FF, tm), _tile(D, tm), _acc((8, D)), _acc((1, LANE))),
        out_shape=(_S((L, FF), BF16), _S((L, D), F32), _S((8, D), F32), _S((1, LANE), F32)),
        compiler_params=_params(("arbitrary",)))(u2, w1, w2, xhat1, tgt, vecs, b1)


def _mlp_bwd_a(dr2, rr, w2, g2):
    L, D = dr2.shape
    FF = w2.shape[0]
    tm = _pick(L, (256, 128))
    fc = _pick(FF, (512, 256, 128))

    def body(dr2_ref, rr_ref, w2_ref, g2_ref, df_ref, da_ref, gb2_ref, gb1_ref):
        @pl.when(pl.program_id(0) == 0)
        def _():
            gb2_ref[...] = jnp.zeros_like(gb2_ref)
            gb1_ref[...] = jnp.zeros_like(gb1_ref)

        df = (1.0 + g2_ref[...]) * dr2_ref[...]
        gb2_ref[...] += _colsum(df)
        dfb = df.astype(BF16)
        df_ref[...] = dfb
        for j in range(FF // fc):
            cs = slice(j * fc, (j + 1) * fc)
            da = _dot_nt(dfb, w2_ref[cs, :]) * (2.0 * rr_ref[:, cs].astype(F32))
            gb1_ref[:, cs] += _colsum(da)
            da_ref[:, cs] = da.astype(BF16)

    return pl.pallas_call(
        body, name="mlp_bwd_a", grid=(L // tm,),
        in_specs=[_tile(D, tm), _tile(FF, tm), _res((FF, D)), _acc((1, D))],
        out_specs=(_tile(D, tm), _tile(FF, tm), _acc((1, D)), _acc((1, FF))),
        out_shape=(_S((L, D), BF16), _S((L, FF), BF16), _S((1, D), F32), _S((1, FF), F32)),
        compiler_params=_params(("arbitrary",)))(dr2, rr, w2, g2)


def _mlp_bwd_b(da, w1, dr2, xhat1, rstd1, mix, w_out, vecs, alpha, W):
    L, FF = da.shape
    D = dr2.shape[1]
    MIX = w_out.shape[0]
    tm = _pick(L, (256, 128))

    def body(da_ref, w1_ref, dr2_ref, xh_ref, rs_ref, mix_ref, wo_ref, v_ref, dmix_ref, dh0_ref, dyn_ref, do_ref, acc_ref):
        @pl.when(pl.program_id(0) == 0)
        def _():
            acc_ref[...] = jnp.zeros_like(acc_ref)

        du2 = _dot_nt(da_ref[...], w1_ref[...])
        xhat1 = xh_ref[...]
        h1 = xhat1 * v_ref[0:1, :] + v_ref[1:2, :]
        acc_ref[0:1, :] += _colsum(du2 * h1)
        acc_ref[1:2, :] += _colsum(du2)
        dh1 = alpha * dr2_ref[...] + du2 * (1.0 + v_ref[2:3, :])
        acc_ref[2:3, :] += _colsum(dh1 * xhat1)
        acc_ref[3:4, :] += _colsum(dh1)
        dr1 = _ln_bwd(dh1 * v_ref[0:1, :], xhat1, rs_ref[...])
        acc_ref[4:5, :] += _colsum(dr1 * mix_ref[...])
        dh0_ref[...] = alpha * dr1
        dmix = ((1.0 + v_ref[3:4, :]) * dr1).astype(BF16)
        dmix_ref[...] = dmix
        dyn_ref[...] = _dot_nt(dmix, wo_ref[0:W, :])
        do_ref[...] = _dot_nt(dmix, wo_ref[W:MIX, :]).astype(BF16)

    return pl.pallas_call(
        body, name="mlp_bwd_b", grid=(L // tm,),
        in_specs=[_tile(FF, tm), _res((D, FF)), _tile(D, tm), _tile(D, tm), _tile(1, tm), _tile(D, tm), _res((MIX, D)),
                  _acc((8, D))],
        out_specs=(_tile(D, tm), _tile(D, tm), _tile(W, tm), _tile(MIX - W, tm), _acc((8, D))),
        out_shape=(_S((L, D), BF16), _S((L, D), F32), _S((L, W), F32), _S((L, MIX - W), BF16), _S((8, D), F32)),
        compiler_params=_params(("arbitrary",)))(da, w1, dr2, xhat1, rstd1, mix, w_out, vecs)


def _swa_bwd(q, kv, do, sinks, dims):
    L, AW = q.shape
    KV, KVW2 = dims["KV"], dims["KVW2"]
    T = CHUNK
    nb = L // T
    slopes = _alibi_slopes(dims["AH"])
    scale = HEAD_DIM ** -0.5

    def body(q_ref, kvc_ref, kvp_ref, do_ref, sink_ref, dq_ref, dkv_ref, dsink_ref, carry_ref):
        i = pl.program_id(0)

        @pl.when(i == 0)
        def _():
            carry_ref[...] = jnp.zeros_like(carry_ref)
            dsink_ref[...] = jnp.zeros_like(dsink_ref)

        @pl.when(i < nb)
        def _():
            dist, valid = _attn_mask(T, i)
            lane = lax.broadcasted_iota(jnp.int32, (1, LANE), 1)
            dsink = jnp.zeros((1, LANE), F32)
            dks, dvs = [], []
            for g in range(KV):
                ks = slice(g * HEAD_DIM, (g + 1) * HEAD_DIM)
                vs = slice(KV * HEAD_DIM + g * HEAD_DIM, KV * HEAD_DIM + (g + 1) * HEAD_DIM)
                kk = jnp.concatenate([kvp_ref[:, ks], kvc_ref[:, ks]], axis=0)
                vv = jnp.concatenate([kvp_ref[:, vs], kvc_ref[:, vs]], axis=0)
                dk = jnp.zeros((2 * T, HEAD_DIM), F32)
                dv = jnp.zeros((2 * T, HEAD_DIM), F32)
                for r in range(GQA):
                    h = g * GQA + r
                    hs = slice(h * HEAD_DIM, (h + 1) * HEAD_DIM)
                    qh = q_ref[:, hs]
                    doh = do_ref[:, hs]
                    p, p_sink = _attn_probs(qh, kk, dist, valid, slopes[h], sink_ref[h])
                    dp = _dot_nt(doh, vv)
                    delta = jnp.sum(p * dp, axis=-1, keepdims=True)
                    dsb = (p * (dp - delta)).astype(BF16)
                    dsink = dsink + jnp.where(lane == h, -jnp.sum(p_sink * delta), 0.0)
                    dq_ref[:, hs] = (_dot(dsb, kk) * scale).astype(BF16)
                    dk = dk + _dot_tn(dsb, qh) * scale
                    dv = dv + _dot_tn(p.astype(BF16), doh)
                dks.append(dk)
                dvs.append(dv)
            dkv = jnp.concatenate(dks + dvs, axis=1)
            dsink_ref[...] += dsink
            dkv_ref[...] = carry_ref[...] + dkv[0:T, :]
            carry_ref[...] = dkv[T:2 * T, :]

        @pl.when(i == nb)
        def _():
            dkv_ref[...] = carry_ref[...]

    last = nb - 1
    return pl.pallas_call(
        body, name="swa_bwd", grid=(nb + 1,),
        in_specs=[pl.BlockSpec((T, AW), lambda i: (jnp.minimum(i, last), 0)),
                  pl.BlockSpec((T, KVW2), lambda i: (jnp.minimum(i, last), 0)),
                  pl.BlockSpec((T, KVW2), lambda i: (jnp.clip(i - 1, 0, last), 0)),
                  pl.BlockSpec((T, AW), lambda i: (jnp.minimum(i, last), 0)),
                  pl.BlockSpec(memory_space=pltpu.SMEM)],
        out_specs=(pl.BlockSpec((T, AW), lambda i: (jnp.minimum(i, last), 0)),
                   pl.BlockSpec((T, KVW2), lambda i: (jnp.maximum(i - 1, 0), 0)), _acc((1, LANE))),
        out_shape=(_S((L, AW), BF16), _S((L, KVW2), F32), _S((1, LANE), F32)),
        scratch_shapes=[pltpu.VMEM((T, KVW2), F32)],
        compiler_params=_params(("arbitrary",)))(q, kv, kv, do, sinks)


def _ssd_bwd(dyn, y, z, xbc, dt_raw, sprev, cw, cb, dtb, alog, dsk, nw, dims):
    L, CD = xbc.shape
    W, H, G, N = dims["W"], dims["H"], SSD_GROUPS, SSD_STATE
    T = CHUNK
    R = H // G
    GW = W // G
    nc = L // T
    HP = H * HEAD_DIM

    def body(dyn_ref, y_ref, z_ref, xbc_ref, prev_ref, dt_ref, sp_ref, cw_ref, cb_ref, dtb_ref, alog_ref, dsk_ref, nw_ref,
             dz_ref, dpre_ref, ddt_ref, acc_ref, hacc_ref, ext_ref, ds_ref, dxs_ref):
        i = pl.program_id(0)

        @pl.when(i == 0)
        def _():
            ds_ref[...] = jnp.zeros_like(ds_ref)
            acc_ref[...] = jnp.zeros_like(acc_ref)
            hacc_ref[...] = jnp.zeros_like(hacc_ref)

        act, pre = _conv_act(xbc_ref, prev_ref, cw_ref, cb_ref, ext_ref, i == nc - 1)
        xs = act[:, :W]
        dt_in = dt_ref[...] + dtb_ref[...]
        dt = _softplus(dt_in)
        a_neg = -jnp.exp(alog_ref[...])
        a = dt * a_neg
        low = _tri(T)
        upf = _tri(T, upper=True).astype(F32)
        acum = _dot_hi(low.astype(F32), a)
        acum_t = _dot_hi(a.T, upf)

        y = y_ref[...]
        zz = z_ref[...]
        sg = _sigmoid(zz)
        sz = zz * sg
        hh = y * sz
        dyn_v = dyn_ref[...]
        parts = []
        for g in range(G):
            gs = slice(g * GW, (g + 1) * GW)
            hg = hh[:, gs]
            hhat = hg * lax.rsqrt(_mean(hg * hg) + RMS_EPS)
            rg = lax.rsqrt(_mean(hg * hg) + RMS_EPS)
            acc_ref[0:1, gs] += _colsum(dyn_v[:, gs] * hhat)
            dhhat = dyn_v[:, gs] * nw_ref[:, gs]
            parts.append(rg * (dhhat - hhat * _mean(dhhat * hhat)))
        dhh = jnp.concatenate(parts, axis=1)
        dy = dhh * sz
        dz_ref[...] = (dhh * y * (sg * (1.0 + zz * (1.0 - sg)))).astype(BF16)
        acc_ref[1:2, :] += _colsum(dy * xs)
        dxs_ref[...] = dsk_ref[...] * dy

        lane = lax.broadcasted_iota(jnp.int32, (T, LANE), 1)
        sub = lax.broadcasted_iota(jnp.int32, (T, LANE), 0)
        subr = lax.broadcasted_iota(jnp.int32, (LANE, T), 0)
        da_col = jnp.zeros((T, LANE), F32)
        da_row = jnp.zeros((LANE, T), F32)
        ddt = jnp.zeros((T, LANE), F32)
        dbs, dcs = [], []
        for g in range(G):
            bg = act[:, W + g * N:W + (g + 1) * N]
            cg = act[:, W + G * N + g * N:W + G * N + (g + 1) * N]
            bgb, cgb = bg.astype(BF16), cg.astype(BF16)
            cb_g = _dot_nt(cgb, bgb)
            dcb = jnp.zeros((T, T), F32)
            db = jnp.zeros((T, N), F32)
            dc = jnp.zeros((T, N), F32)
            for r in range(R):
                h = g * R + r
                hs = slice(h * HEAD_DIM, (h + 1) * HEAD_DIM)
                col = acum[:, h:h + 1]
                last = acum[T - 1:T, h:h + 1]
                e_col = jnp.exp(col)
                e_last = jnp.exp(last)
                lm = jnp.where(low, jnp.exp(col - acum_t[h:h + 1, :]), 0.0)
                mm = cb_g * lm
                mb = mm.astype(BF16)
                dtc = dt[:, h:h + 1]
                xsh = xs[:, hs]
                xb = (xsh * dtc).astype(BF16)
                dyh = dy[:, hs]
                dyb = dyh.astype(BF16)
                sp = sp_ref[0, hs, :]
                spb = sp.astype(BF16)
                dsn = ds_ref[hs, :]
                dsnb = dsn.astype(BF16)
                gmat = _dot_nt(cgb, spb)
                dgb = (e_col * dyh).astype(BF16)
                dc = dc + _dot(dgb, spb)
                dsp = _dot_tn(dgb, cgb) + dsn * e_last
                dac = jnp.sum(dyh * gmat, axis=-1, keepdims=True) * e_col
                dlast = jnp.sum(dsn * sp, keepdims=True) * e_last
                dec = jnp.exp(last - col)
                bdb = (bg * dec).astype(BF16)
                dx = _dot_nt(bdb, dsnb)
                dbd = _dot(xb, dsnb)
                db = db + dbd * dec
                tdec = jnp.sum(dbd * bg, axis=-1, keepdims=True) * dec
                dac = dac - tdec
                dlast = dlast + jnp.sum(tdec, keepdims=True)
                dm = _dot_nt(dyb, xb)
                dx = dx + _dot_tn(mb, dyb)
                dcb = dcb + dm * lm
                qm = dm * mm
                dac = dac + jnp.sum(qm, axis=1, keepdims=True)
                da_row = jnp.where(subr == h, jnp.sum(qm, axis=0, keepdims=True), da_row)
                da_col = jnp.where(lane == h, dac, da_col)
                da_col = da_col + jnp.where((lane == h) & (sub == T - 1), dlast, 0.0)
                ddt = jnp.where(lane == h, jnp.sum(dx * xsh, axis=-1, keepdims=True), ddt)
                dxs_ref[:, hs] += dx * dtc
                ds_ref[hs, :] = dsp
            dcbb = dcb.astype(BF16)
            dcs.append(dc + _dot(dcbb, bgb))
            dbs.append(db + _dot_tn(dcbb, cgb))
        dacum = da_col - da_row.T
        da = _dot_hi(upf, dacum)
        ddt = ddt + da * a_neg
        hacc_ref[1:2, :] += _colsum(da * dt) * a_neg
        ddt_raw = ddt * _sigmoid(dt_in)
        hacc_ref[0:1, :] += _colsum(ddt_raw)
        ddt_ref[...] = ddt_raw
        dact = jnp.concatenate([dxs_ref[...]] + dbs + dcs, axis=1)
        spre = _sigmoid(pre)
        dpre_ref[...] = dact * (spre * (1.0 + pre * (1.0 - spre)))

        @pl.when(i == nc - 1)
        def _():
            ch = lax.broadcasted_iota(jnp.int32, (W, LANE), 0)
            lo = lax.broadcasted_iota(jnp.int32, (W, LANE), 1) * HEAD_DIM
            hacc_ref[2:3, :] = _dot_hi(acc_ref[1:2, :], ((ch >= lo) & (ch < lo + HEAD_DIM)).astype(F32))

    rev = lambda i: (nc - 1 - i, 0)
    return pl.pallas_call(
        body, name="ssd_bwd", grid=(nc,),
        in_specs=[pl.BlockSpec((T, W), rev), pl.BlockSpec((T, W), rev), pl.BlockSpec((T, W), rev), pl.BlockSpec((T, CD), rev),
                  pl.BlockSpec((HALO, CD), lambda i: (jnp.maximum((nc - 1 - i) * (T // HALO) - 1, 0), 0)),
                  pl.BlockSpec((T, LANE), rev), pl.BlockSpec((1, HP, N), lambda i: (nc - 1 - i, 0, 0)),
                  _acc((CONV_K, CD)), _acc((1, CD)), _acc((1, LANE)), _acc((1, LANE)), _acc((1, W)), _acc((1, W))],
        out_specs=(pl.BlockSpec((T, W), rev), pl.BlockSpec((T, CD), rev), pl.BlockSpec((T, LANE), rev), _acc((8, W)),
                   _acc((8, LANE))),
        out_shape=(_S((L, W), BF16), _S((L, CD), F32), _S((L, LANE), F32), _S((8, W), F32), _S((8, LANE), F32)),
        scratch_shapes=[pltpu.VMEM((T + HALO, CD), F32), pltpu.VMEM((HP, N), F32), pltpu.VMEM((T, W), F32)],
        compiler_params=_params(("arbitrary",)))(dyn, y, z, xbc, xbc, dt_raw, sprev, cw, cb, dtb, alog, dsk, nw)


def _conv_bwd(dpre, xbc, cw):
    L, CD = xbc.shape
    tm = _pick(L, (256, 128))
    nt = L // tm
    hb = tm // HALO

    def body(dp_ref, dn_ref, u_ref, up_ref, cw_ref, du_ref, acc_ref, extu_ref, extd_ref):
        i = pl.program_id(0)

        @pl.when(i == 0)
        def _():
            acc_ref[...] = jnp.zeros_like(acc_ref)

        dp = dp_ref[...]
        extu_ref[0:HALO, :] = jnp.where(i == 0, 0.0, up_ref[...])
        extu_ref[HALO:HALO + tm, :] = u_ref[...]
        extd_ref[0:tm, :] = dp
        extd_ref[tm:tm + HALO, :] = jnp.where(i == nt - 1, 0.0, dn_ref[...])
        du = cw_ref[CONV_K - 1:CONV_K, :] * dp
        acc_ref[CONV_K - 1:CONV_K, :] += _colsum(dp * u_ref[...])
        for k in range(CONV_K - 1):
            s = CONV_K - 1 - k
            du = du + cw_ref[k:k + 1, :] * extd_ref[s:s + tm, :]
            acc_ref[k:k + 1, :] += _colsum(dp * extu_ref[HALO - s:HALO - s + tm, :])
        acc_ref[CONV_K:CONV_K + 1, :] += _colsum(dp)
        du_ref[...] = du.astype(BF16)

    return pl.pallas_call(
        body, name="conv_bwd", grid=(nt,),
        in_specs=[_tile(CD, tm), pl.BlockSpec((HALO, CD), lambda i: (jnp.minimum((i + 1) * hb, nt * hb - 1), 0)),
                  _tile(CD, tm), pl.BlockSpec((HALO, CD), lambda i: (jnp.maximum(i * hb - 1, 0), 0)), _acc((CONV_K, CD))],
        out_specs=(_tile(CD, tm), _acc((8, CD))),
        out_shape=(_S((L, CD), BF16), _S((8, CD), F32)),
        scratch_shapes=[pltpu.VMEM((tm + HALO, CD), F32), pltpu.VMEM((tm + HALO, CD), F32)],
        compiler_params=_params(("arbitrary",)))(dpre, dpre, xbc, xbc, cw)


def _in_proj_bwd(dz, dxbc, dq, dkv, ddt, w_pad, xhat0, rstd0, dh0p, vecs, dims):
    L, D = xhat0.shape
    W, CD, AW, KVW2 = dims["W"], dims["CD"], dims["AW"], dims["KVW2"]
    NP = w_pad.shape[1]
    tm = _pick(L, (256, 128))
    o_xbc, o_q, o_kv, o_dt = W, W + CD, W + CD + AW, W + CD + AW + KVW2

    def body(dz_ref, dxbc_ref, dq_ref, dkv_ref, ddt_ref, w_ref, xh_ref, rs_ref, dh0_ref, v_ref, gx_ref, acc_ref):
        @pl.when(pl.program_id(0) == 0)
        def _():
            acc_ref[...] = jnp.zeros_like(acc_ref)

        du1 = _dot_nt(dz_ref[...], w_ref[:, 0:o_xbc])
        du1 = du1 + _dot_nt(dxbc_ref[...], w_ref[:, o_xbc:o_q])
        du1 = du1 + _dot_nt(dq_ref[...], w_ref[:, o_q:o_kv])
        du1 = du1 + _dot_nt(dkv_ref[...].astype(BF16), w_ref[:, o_kv:o_dt])
        du1 = du1 + _dot_nt(ddt_ref[...].astype(BF16), w_ref[:, o_dt:NP])
        xhat0 = xh_ref[...]
        h0 = xhat0 * v_ref[0:1, :] + v_ref[1:2, :]
        acc_ref[0:1, :] += _colsum(du1 * h0)
        acc_ref[1:2, :] += _colsum(du1)
        dh0 = dh0_ref[...] + du1 * (1.0 + v_ref[2:3, :])
        acc_ref[2:3, :] += _colsum(dh0 * xhat0)
        acc_ref[3:4, :] += _colsum(dh0)
        gx_ref[...] = _ln_bwd(dh0 * v_ref[0:1, :], xhat0, rs_ref[...])

    return pl.pallas_call(
        body, name="in_proj_bwd", grid=(L // tm,),
        in_specs=[_tile(W, tm), _tile(CD, tm), _tile(AW, tm), _tile(KVW2, tm), _tile(LANE, tm), _res((D, NP)),
                  _tile(D, tm), _tile(1, tm), _tile(D, tm), _acc((8, D))],
        out_specs=(_tile(D, tm), _acc((8, D))),
        out_shape=(_S((L, D), F32), _S((8, D), F32)),
        compiler_params=_params(("arbitrary",)))(dz, dxbc, dq, dkv, ddt, w_pad, xhat0, rstd0, dh0p, vecs)


_WEIGHTS = ['ln_in_g', 'ln_in_b', 'ada_w', 'ada_b', 'w_in', 'conv_w', 'conv_b', 'dt_bias', 'a_log', 'd_skip', 'ssd_norm_w',
            'attn_sinks', 'w_out', 'ln1_g', 'ln1_b', 'w_ff1', 'b_ff1', 'w_ff2', 'b_ff2', 'ln2_g', 'ln2_b']
_BIG = ('w_in', 'w_out', 'w_ff1', 'w_ff2')
_SMALL = ('ada_b', 'ln_in_g', 'ln_in_b', 'conv_b', 'dt_bias', 'a_log', 'd_skip', 'ssd_norm_w', 'attn_sinks', 'ln1_g', 'ln1_b',
          'b_ff1', 'b_ff2', 'ln2_g', 'ln2_b')


def _pad_lanes(v, n=None):
    v = v.reshape(1, -1)
    n = n or -(-v.shape[1] // LANE) * LANE
    return jnp.pad(v, ((0, 0), (0, n - v.shape[1])))


def _vec8(rows, D):
    rows = [r.reshape(1, D) for r in rows]
    return jnp.concatenate(rows + [jnp.zeros((8 - len(rows), D), F32)], axis=0)


def _pack(segs):
    flat, offs, sizes, o = [], [], [], 0
    for s in segs:
        p = _pad_lanes(s)
        flat.append(p)
        offs.append(o)
        sizes.append(s.size)
        o += p.shape[1]
    total = -(-o // (8 * LANE)) * (8 * LANE)
    if total > o:
        flat.append(jnp.zeros((1, total - o), F32))
    return jnp.concatenate(flat, axis=1).reshape(8, total // 8), offs, sizes


def kernel(x, c, ln_in_g, ln_in_b, ada_w, ada_b, w_in, conv_w, conv_b, dt_bias, a_log, d_skip, ssd_norm_w, attn_sinks, w_out, ln1_g, ln1_b, w_ff1, b_ff1, w_ff2, b_ff2, ln2_g, ln2_b, loss_target, m_ln_in_g, m_ln_in_b, m_ada_w, m_ada_b, m_w_in, m_conv_w, m_conv_b, m_dt_bias, m_a_log, m_d_skip, m_ssd_norm_w, m_attn_sinks, m_w_out, m_ln1_g, m_ln1_b, m_w_ff1, m_b_ff1, m_w_ff2, m_b_ff2, m_ln2_g, m_ln2_b, v_ln_in_g, v_ln_in_b, v_ada_w, v_ada_b, v_w_in, v_conv_w, v_conv_b, v_dt_bias, v_a_log, v_d_skip, v_ssd_norm_w, v_attn_sinks, v_w_out, v_ln1_g, v_ln1_b, v_w_ff1, v_b_ff1, v_w_ff2, v_b_ff2, v_ln2_g, v_ln2_b):
    wts = dict(ln_in_g=ln_in_g, ln_in_b=ln_in_b, ada_w=ada_w, ada_b=ada_b, w_in=w_in, conv_w=conv_w, conv_b=conv_b,
               dt_bias=dt_bias, a_log=a_log, d_skip=d_skip, ssd_norm_w=ssd_norm_w, attn_sinks=attn_sinks, w_out=w_out,
               ln1_g=ln1_g, ln1_b=ln1_b, w_ff1=w_ff1, b_ff1=b_ff1, w_ff2=w_ff2, b_ff2=b_ff2, ln2_g=ln2_g, ln2_b=ln2_b)
    ms = dict(ln_in_g=m_ln_in_g, ln_in_b=m_ln_in_b, ada_w=m_ada_w, ada_b=m_ada_b, w_in=m_w_in, conv_w=m_conv_w,
              conv_b=m_conv_b, dt_bias=m_dt_bias, a_log=m_a_log, d_skip=m_d_skip, ssd_norm_w=m_ssd_norm_w,
              attn_sinks=m_attn_sinks, w_out=m_w_out, ln1_g=m_ln1_g, ln1_b=m_ln1_b, w_ff1=m_w_ff1, b_ff1=m_b_ff1,
              w_ff2=m_w_ff2, b_ff2=m_b_ff2, ln2_g=m_ln2_g, ln2_b=m_ln2_b)
    vs = dict(ln_in_g=v_ln_in_g, ln_in_b=v_ln_in_b, ada_w=v_ada_w, ada_b=v_ada_b, w_in=v_w_in, conv_w=v_conv_w,
              conv_b=v_conv_b, dt_bias=v_dt_bias, a_log=v_a_log, d_skip=v_d_skip, ssd_norm_w=v_ssd_norm_w,
              attn_sinks=v_attn_sinks, w_out=v_w_out, ln1_g=v_ln1_g, ln1_b=v_ln1_b, w_ff1=v_w_ff1, b_ff1=v_b_ff1,
              w_ff2=v_w_ff2, b_ff2=v_b_ff2, ln2_g=v_ln2_g, ln2_b=v_ln2_b)

    L, D = x.shape[1], x.shape[2]
    depth = w_in.shape[0]
    assert depth == 1 and x.shape[0] == 1 and L % CHUNK == 0
    W = D
    H = W // HEAD_DIM
    CD = W + 2 * SSD_GROUPS * SSD_STATE
    AW = D
    AH = AW // HEAD_DIM
    KV = AH // GQA
    KVW2 = 2 * KV * HEAD_DIM
    PROJ = W + CD + H + AW + KVW2
    FF = w_ff1.shape[2] * N_DEV
    MIX = w_out.shape[1] * N_DEV
    assert w_in.shape[2] * N_DEV == PROJ and MIX == W + AW and H <= LANE and AH <= LANE
    dims = dict(W=W, H=H, CD=CD, AW=AW, AH=AH, KV=KV, KVW2=KVW2)
    alpha = (2.0 * depth) ** 0.25
    C6 = ada_w.shape[2]
    CW = conv_w.shape[2]

    ax, ay, ac = _my_pos()
    me = 4 * ax + 2 * ay + ac
    x2 = x.reshape(L, D)
    tgt = loss_target.reshape(L, D)
    r1 = lambda a: a.reshape(1, -1)

    ada_b_cols = lax.dynamic_slice(ada_b, (0, me * C6), (1, C6))
    cs_all, mod = _mod_fwd(c, ada_w[0], ada_b_cols)
    sh1, sc1, g1, sh2, sc2, g2 = [r1(t) for t in jnp.split(mod.reshape(-1), 6)]

    wg_in, wg_out, wg_ff1, wg_ff2, cwg = _ag_weights(
        [w_in[0].astype(BF16), w_out[0].astype(BF16), w_ff1[0].astype(BF16), w_ff2[0].astype(BF16), conv_w[0]])
    w_in_full = wg_in.transpose(1, 0, 2).reshape(D, PROJ)
    i1, i2, i3, i4 = W, W + CD, W + CD + H, W + CD + H + AW
    w_pad = jnp.concatenate([w_in_full[:, :i2], w_in_full[:, i3:], w_in_full[:, i2:i3], jnp.zeros((D, LANE - H), BF16)], axis=1)
    w_out_full = wg_out.reshape(MIX, D)
    w1_full = wg_ff1.transpose(1, 0, 2).reshape(D, FF)
    w2_full = wg_ff2.reshape(FF, D)
    cw_full = cwg.transpose(1, 0, 2).reshape(CONV_K, CD)

    dtb = _pad_lanes(dt_bias, LANE)
    alog = _pad_lanes(a_log, LANE)
    dsk = jnp.repeat(d_skip.reshape(-1), HEAD_DIM).reshape(1, W)
    sinks = attn_sinks.reshape(-1)
    g_in, b_in = r1(ln_in_g), r1(ln_in_b)

    xhat0, rstd0, u1, z, xbc, q, kv, dt_raw = _ln_in_proj(x2, g_in, b_in, sc1, sh1, w_pad, dims)
    y, yn, sprev = _conv_ssd(xbc, dt_raw, z, cw_full, conv_b, dtb, alog, dsk, ssd_norm_w, dims)
    o = _swa_fwd(q, kv, sinks, dims)
    mix, xhat1, rstd1, u2 = _out_proj_ln1(yn, o, w_out_full, xhat0, _vec8([g_in, b_in, g1, ln1_g, ln1_b, sc2, sh2], D), alpha)
    rr, dr2, acc_f, loss_loc = _mlp_loss(u2, w1_full, w2_full, xhat1, tgt,
                                         _vec8([ln1_g, ln1_b, g2, ln2_g, ln2_b, b_ff2], D), b_ff1, alpha)

    df, da, gb2, gb1 = _mlp_bwd_a(dr2, rr, w2_full, g2)
    gw_ff2 = _matmul_tn(rr, df, "gw_ff2", square_a=True)
    gw_ff1 = _matmul_tn(u2, da, "gw_ff1")
    dmix, dh0p, dyn, do, acc_b = _mlp_bwd_b(da, w1_full, dr2, xhat1, rstd1, mix, w_out_full,
                                            _vec8([ln1_g, ln1_b, sc2, g1], D), alpha, W)
    gw_out = jnp.concatenate([_matmul_tn(yn, dmix, "gw_out_ssd"), _matmul_tn(o, dmix, "gw_out_attn")], axis=0)
    dq, dkv, dsink = _swa_bwd(q, kv, do, sinks, dims)
    dz, dpre, ddt, acc_s, hacc = _ssd_bwd(dyn, y, z, xbc, dt_raw, sprev, cw_full, conv_b, dtb, alog, dsk, ssd_norm_w, dims)
    dxbc, acc_c = _conv_bwd(dpre, xbc, cw_full)
    grad_x, acc_i = _in_proj_bwd(dz, dxbc, dq, dkv, ddt, w_pad, xhat0, rstd0, dh0p, _vec8([g_in, b_in, sc1], D), dims)
    gz = _matmul_tn(u1, dz, "gw_in_z")
    gxbc = _matmul_tn(u1, dxbc, "gw_in_xbc")
    gq = _matmul_tn(u1, dq, "gw_in_q")
    gkv = _matmul_tn(u1, dkv, "gw_in_kv")
    gdt = _matmul_tn(u1, ddt, "gw_in_dt")
    gw_in = jnp.concatenate([gz, gxbc, gdt[:, :H], gq, gkv], axis=1)

    dmod = jnp.concatenate([acc_i[1], acc_i[0], acc_b[4], acc_b[1], acc_b[0], acc_f[2]])
    small_g = dict(ada_b=dmod, ln_in_g=acc_i[2], ln_in_b=acc_i[3], conv_b=acc_c[CONV_K], dt_bias=hacc[0, :H], a_log=hacc[1, :H],
                   d_skip=hacc[2, :H], ssd_norm_w=acc_s[0], attn_sinks=dsink[0, :AH], ln1_g=acc_b[2], ln1_b=acc_b[3],
                   b_ff1=gb1[0], b_ff2=gb2[0], ln2_g=acc_f[0], ln2_b=acc_f[1])
    segs = [small_g[n] for n in _SMALL] + [acc_c[:CONV_K].reshape(-1), loss_loc[0, :1]]
    pack, offs, sizes = _pack(segs)
    gathered, summed = _small_gather_sum(pack)
    gathered = gathered.reshape(N_DEV, -1)
    summed = summed.reshape(-1)
    seg = lambda k: summed[offs[k]:offs[k] + sizes[k]]
    grads = {n: seg(k).reshape(wts[n].shape) for k, n in enumerate(_SMALL)}
    gcw_full = seg(len(_SMALL)).reshape(CONV_K, CD)
    grads['conv_w'] = lax.dynamic_slice(gcw_full, (0, me * CW), (CONV_K, CW)).reshape(conv_w.shape)
    loss = seg(len(_SMALL) + 1)[0]

    names = list(_SMALL) + ['conv_w']
    pw, poffs, psizes = _pack([wts[n] for n in names])
    pg, _, _ = _pack([grads[n] for n in names])
    pm, _, _ = _pack([ms[n] for n in names])
    pv, _, _ = _pack([vs[n] for n in names])
    pd, pm2, pv2 = [t.reshape(-1) for t in _adamw(pw, pg, pm, pv)]
    deltas, new_m, new_v = {}, {}, {}
    for k, n in enumerate(names):
        sl = slice(poffs[k], poffs[k] + psizes[k])
        deltas[n], new_m[n], new_v[n] = (t[sl].reshape(wts[n].shape) for t in (pd, pm2, pv2))

    dmod_cols = lax.dynamic_slice(gathered, (0, offs[0] + me * C6), (N_DEV, C6))
    pad16 = lambda t: jnp.concatenate([t, jnp.zeros((16 - N_DEV,) + t.shape[1:], t.dtype)], axis=0)
    g_, d_, m_, v_ = _ada_grad_adamw(pad16(cs_all), pad16(dmod_cols), ada_w[0], m_ada_w[0], v_ada_w[0])
    grads['ada_w'], deltas['ada_w'], new_m['ada_w'], new_v['ada_w'] = (t[None] for t in (g_, d_, m_, v_))

    blocked = [gw_in.reshape(D, N_DEV, PROJ // N_DEV).transpose(1, 0, 2), gw_out.reshape(N_DEV, MIX // N_DEV, D),
               gw_ff1.reshape(D, N_DEV, FF // N_DEV).transpose(1, 0, 2), gw_ff2.reshape(N_DEV, FF // N_DEV, D)]
    recv = _rs_d2d(blocked)
    core = jnp.reshape(ac, (1,)).astype(jnp.int32)
    pairs = [_pair_sum(b, r, core) for b, r in zip(blocked, recv)]
    parts = _rs_ici(pairs)
    for n, p in zip(_BIG, parts):
        g_, d_, m_, v_ = _sum_adamw(p, wts[n][0], ms[n][0], vs[n][0])
        grads[n], deltas[n], new_m[n], new_v[n] = (t[None] for t in (g_, d_, m_, v_))

    return (loss, grad_x.reshape(x.shape), *[grads[n] for n in _WEIGHTS], *[deltas[n] for n in _WEIGHTS],
            *[new_m[n] for n in _WEIGHTS], *[new_v[n] for n in _WEIGHTS])
```

```python
import functools
import math

import numpy as np
import jax
import jax.numpy as jnp
from jax import lax
from jax.experimental import pallas as pl
from jax.experimental.pallas import tpu as pltpu

F32 = jnp.float32
BF16 = jnp.bfloat16
MESH = pl.DeviceIdType.MESH

N_DEV = 8
N_CHIP = 4
HEAD_DIM = 64
SSD_GROUPS = 2
SSD_STATE = 128
CHUNK = 128
CONV_K = 4
GQA = 8
LANE = 128
HALO = 8
LN_EPS = 1e-5
RMS_EPS = 1e-5
NEG = -1e30
ADAM_LR, ADAM_B1, ADAM_B2, ADAM_EPS, ADAM_WD, ADAM_STEP = 0.001, 0.9, 0.999, 1e-08, 0.01, 10
V7X_VMEM_BYTES = 64 * 1024 * 1024
VMEM_LIMIT = V7X_VMEM_BYTES - 8 * 1024 * 1024
HI = lax.Precision.HIGHEST


def _alibi_slopes(n):
    def pow2(m):
        start = 2.0 ** (-8.0 / m)
        return [start ** (i + 1) for i in range(m)]
    if math.log2(n).is_integer():
        s = pow2(n)
    else:
        c = 2 ** math.floor(math.log2(n))
        s = pow2(c) + pow2(2 * c)[0::2][: n - c]
    return [float(v) for v in np.array(s, dtype=np.float32)]


def _dot(a, b):
    return jnp.dot(a, b, preferred_element_type=F32)


def _dot_nt(a, b):
    return lax.dot_general(a, b, (((1,), (1,)), ((), ())), preferred_element_type=F32)


def _dot_tn(a, b):
    return lax.dot_general(a, b, (((0,), (0,)), ((), ())), preferred_element_type=F32)


def _dot_hi(a, b):
    return jnp.dot(a, b, precision=HI, preferred_element_type=F32)


def _sigmoid(x):
    return 1.0 / (1.0 + jnp.exp(-x))


def _softplus(x):
    return jnp.maximum(x, 0.0) + jnp.log(1.0 + jnp.exp(-jnp.abs(x)))


def _mean(x):
    return jnp.mean(x, axis=-1, keepdims=True)


def _ln_fwd(x):
    xc = x - _mean(x)
    rstd = lax.rsqrt(_mean(xc * xc) + LN_EPS)
    return xc * rstd, rstd


def _ln_bwd(dxhat, xhat, rstd):
    return rstd * (dxhat - _mean(dxhat) - xhat * _mean(dxhat * xhat))


def _colsum(x):
    return jnp.sum(x, axis=0, keepdims=True)


def _params(sem):
    return pltpu.CompilerParams(dimension_semantics=sem, vmem_limit_bytes=VMEM_LIMIT)


def _tile(i_map_cols, tm):
    return pl.BlockSpec((tm, i_map_cols), lambda i: (i, 0))


def _res(shape):
    return pl.BlockSpec(shape, lambda *_: (0,) * len(shape), pipeline_mode=pl.Buffered(1))


def _acc(shape):
    return pl.BlockSpec(shape, lambda *_: (0,) * len(shape))


def _S(shape, dtype):
    return jax.ShapeDtypeStruct(shape, dtype)


def _my_pos():
    return lax.axis_index("x"), lax.axis_index("y"), lax.axis_index("c")


def _peer(pos, k):
    x, y, c = pos
    px = 1 - x if k & 4 else x
    py = 1 - y if k & 2 else y
    pc = 1 - c if k & 1 else c
    return (px, py, pc)


def _lin(p):
    return 4 * p[0] + 2 * p[1] + p[2]


def _mod_fwd(c_loc, ada_w_loc, ada_b_cols):
    D = c_loc.shape[1]
    C6 = ada_w_loc.shape[1]

    def body(c_ref, w_ref, b_ref, cs_ref, mod_ref, call_ref, modp_ref, ssem, rsem):
        pos = _my_pos()
        me = _lin(pos)
        call_ref[me] = c_ref[...]
        sends = []
        for k in range(1, N_DEV):
            cp = pltpu.make_async_remote_copy(src_ref=c_ref, dst_ref=call_ref.at[me], send_sem=ssem.at[k - 1],
                                              recv_sem=rsem.at[k - 1], device_id=_peer(pos, k), device_id_type=MESH)
            cp.start()
            sends.append(cp)
        for k in range(1, N_DEV):
            src = _lin(_peer(pos, k))
            pltpu.make_async_remote_copy(src_ref=c_ref, dst_ref=call_ref.at[src], send_sem=ssem.at[k - 1],
                                         recv_sem=rsem.at[k - 1], device_id=pos, device_id_type=MESH).wait_recv()
        for cp in sends:
            cp.wait_send()
        call = jnp.concatenate([call_ref[b] for b in range(N_DEV)], axis=0)
        cs = call * _sigmoid(call)
        cs_ref[...] = cs
        modp = _dot(cs.astype(BF16), w_ref[...].astype(BF16)) + b_ref[...]
        for b in range(N_DEV):
            modp_ref[b] = modp[b:b + 1, :]
        mod_ref[me] = modp_ref[me]
        sends = []
        for k in range(1, N_DEV):
            peer = _peer(pos, k)
            cp = pltpu.make_async_remote_copy(src_ref=modp_ref.at[_lin(peer)], dst_ref=mod_ref.at[me],
                                              send_sem=ssem.at[N_DEV - 2 + k], recv_sem=rsem.at[N_DEV - 2 + k],
                                              device_id=peer, device_id_type=MESH)
            cp.start()
            sends.append(cp)
        for k in range(1, N_DEV):
            src = _lin(_peer(pos, k))
            pltpu.make_async_remote_copy(src_ref=modp_ref.at[src], dst_ref=mod_ref.at[src],
                                         send_sem=ssem.at[N_DEV - 2 + k], recv_sem=rsem.at[N_DEV - 2 + k],
                                         device_id=pos, device_id_type=MESH).wait_recv()
        for cp in sends:
            cp.wait_send()

    vm = pl.BlockSpec(memory_space=pltpu.VMEM)
    return pl.pallas_call(
        body, name="mod_fwd",
        out_shape=(_S((N_DEV, D), F32), _S((N_DEV, 1, C6), F32)),
        in_specs=[vm, vm, vm], out_specs=(vm, vm),
        scratch_shapes=[pltpu.VMEM((N_DEV, 1, D), F32), pltpu.VMEM((N_DEV, 1, C6), F32),
                        pltpu.SemaphoreType.DMA((2 * (N_DEV - 1),)), pltpu.SemaphoreType.DMA((2 * (N_DEV - 1),))],
        compiler_params=pltpu.CompilerParams(vmem_limit_bytes=VMEM_LIMIT),
    )(c_loc, ada_w_loc, ada_b_cols)


def _small_gather_sum(pack):
    P8 = pack.shape[1]

    def body(p_ref, gat_ref, sum_ref, ssem, rsem):
        pos = _my_pos()
        me = _lin(pos)
        gat_ref[me] = p_ref[...]
        sends = []
        for k in range(1, N_DEV):
            cp = pltpu.make_async_remote_copy(src_ref=p_ref, dst_ref=gat_ref.at[me], send_sem=ssem.at[k - 1],
                                              recv_sem=rsem.at[k - 1], device_id=_peer(pos, k), device_id_type=MESH)
            cp.start()
            sends.append(cp)
        for k in range(1, N_DEV):
            src = _lin(_peer(pos, k))
            pltpu.make_async_remote_copy(src_ref=p_ref, dst_ref=gat_ref.at[src], send_sem=ssem.at[k - 1],
                                         recv_sem=rsem.at[k - 1], device_id=pos, device_id_type=MESH).wait_recv()
        for cp in sends:
            cp.wait_send()
        acc = gat_ref[0]
        for j in range(1, N_DEV):
            acc = acc + gat_ref[j]
        sum_ref[...] = acc

    vm = pl.BlockSpec(memory_space=pltpu.VMEM)
    return pl.pallas_call(
        body, name="small_gather_sum",
        out_shape=(_S((N_DEV, 8, P8), F32), _S((8, P8), F32)),
        in_specs=[vm], out_specs=(vm, vm),
        scratch_shapes=[pltpu.SemaphoreType.DMA((N_DEV - 1,)), pltpu.SemaphoreType.DMA((N_DEV - 1,))],
        compiler_params=pltpu.CompilerParams(vmem_limit_bytes=VMEM_LIMIT),
    )(pack)


def _ag_weights(shards):
    n = len(shards)

    def body(*refs):
        ins, outs = refs[:n], refs[n:2 * n]
        ssem, rsem, lsem = refs[2 * n:]
        x, y, c = pos = _my_pos()
        me = _lin(pos)
        sib = (x, y, 1 - c)
        chips = [(1 - x, y), (x, 1 - y), (1 - x, 1 - y)]

        def copy(a, k, block, to, src=None):
            return pltpu.make_async_remote_copy(
                src_ref=outs[a].at[block] if src is None else src, dst_ref=outs[a].at[block],
                send_sem=ssem.at[a * 7 + k], recv_sem=rsem.at[a * 7 + k], device_id=to, device_id_type=MESH)

        local = [pltpu.make_async_copy(ins[a], outs[a].at[me], lsem.at[a]) for a in range(n)]
        for cp in local:
            cp.start()
        first = []
        for a in range(n):
            first.append(copy(a, 0, me, sib, src=ins[a]))
            first += [copy(a, 1 + j, me, (*chip, c), src=ins[a]) for j, chip in enumerate(chips)]
        for cp in first:
            cp.start()
        passed = []
        for a in range(n):
            for j, chip in enumerate(chips):
                blk = _lin((*chip, c))
                copy(a, 1 + j, blk, pos).wait_recv()
                cp = copy(a, 4 + j, blk, sib)
                cp.start()
                passed.append(cp)
        for a in range(n):
            copy(a, 0, _lin(sib), pos).wait_recv()
            for j, chip in enumerate(chips):
                copy(a, 4 + j, _lin((*chip, 1 - c)), pos).wait_recv()
        for cp in first + passed:
            cp.wait_send()
        for cp in local:
            cp.wait()

    hbm = pl.BlockSpec(memory_space=pl.ANY)
    return pl.pallas_call(
        body, name="ag_weights",
        out_shape=tuple(_S((N_DEV,) + s.shape, s.dtype) for s in shards),
        in_specs=[hbm] * n, out_specs=tuple([hbm] * n),
        scratch_shapes=[pltpu.SemaphoreType.DMA((7 * n,)), pltpu.SemaphoreType.DMA((7 * n,)),
                        pltpu.SemaphoreType.DMA((n,))],
    )(*shards)


def _rs_d2d(blocked):
    n = len(blocked)

    def body(*refs):
        ins, outs = refs[:n], refs[n:2 * n]
        ssem, rsem = refs[2 * n:]
        x, y, c = pos = _my_pos()
        sib = (x, y, 1 - c)
        cps = []
        for a in range(n):
            for j in range(N_CHIP):
                cp = pltpu.make_async_remote_copy(
                    src_ref=ins[a].at[2 * j + (1 - c)], dst_ref=outs[a].at[j], send_sem=ssem.at[a * N_CHIP + j],
                    recv_sem=rsem.at[a * N_CHIP + j], device_id=sib, device_id_type=MESH)
                cp.start()
                cps.append(cp)
        for cp in cps:
            cp.wait_recv()
        for cp in cps:
            cp.wait_send()

    hbm = pl.BlockSpec(memory_space=pl.ANY)
    return pl.pallas_call(
        body, name="rs_d2d",
        out_shape=tuple(_S((N_CHIP,) + b.shape[1:], b.dtype) for b in blocked),
        in_specs=[hbm] * n, out_specs=tuple([hbm] * n),
        scratch_shapes=[pltpu.SemaphoreType.DMA((N_CHIP * n,)), pltpu.SemaphoreType.DMA((N_CHIP * n,))],
    )(*blocked)


def _rs_ici(pairs):
    n = len(pairs)

    def body(*refs):
        ins, outs = refs[:n], refs[n:2 * n]
        ssem, rsem, lsem = refs[2 * n:]
        x, y, c = pos = _my_pos()
        mychip = 2 * x + y
        local = [pltpu.make_async_copy(ins[a].at[mychip], outs[a].at[mychip], lsem.at[a]) for a in range(n)]
        for cp in local:
            cp.start()
        sends = []
        for a in range(n):
            for k in range(1, N_CHIP):
                tx, ty, _ = _peer(pos, 2 * k)
                cp = pltpu.make_async_remote_copy(
                    src_ref=ins[a].at[2 * tx + ty], dst_ref=outs[a].at[mychip], send_sem=ssem.at[a * 3 + k - 1],
                    recv_sem=rsem.at[a * 3 + k - 1], device_id=(tx, ty, c), device_id_type=MESH)
                cp.start()
                sends.append(cp)
        for a in range(n):
            for k in range(1, N_CHIP):
                tx, ty, _ = _peer(pos, 2 * k)
                pltpu.make_async_remote_copy(
                    src_ref=ins[a].at[2 * tx + ty], dst_ref=outs[a].at[2 * tx + ty], send_sem=ssem.at[a * 3 + k - 1],
                    recv_sem=rsem.at[a * 3 + k - 1], device_id=pos, device_id_type=MESH).wait_recv()
        for cp in sends:
            cp.wait_send()
        for cp in local:
            cp.wait()

    hbm = pl.BlockSpec(memory_space=pl.ANY)
    return pl.pallas_call(
        body, name="rs_ici",
        out_shape=tuple(_S(p.shape, p.dtype) for p in pairs),
        in_specs=[hbm] * n, out_specs=tuple([hbm] * n),
        scratch_shapes=[pltpu.SemaphoreType.DMA((3 * n,)), pltpu.SemaphoreType.DMA((3 * n,)),
                        pltpu.SemaphoreType.DMA((n,))],
    )(*pairs)


def _row_tile(R, itemsize_rows=16, cap=256):
    t = min(R, cap)
    while R % t or t % itemsize_rows:
        t -= itemsize_rows
    return t


def _pair_sum(blocked, recv, core):
    _, R, C = blocked.shape
    tr = _row_tile(R)

    def body(ids_ref, a_ref, b_ref, o_ref):
        del ids_ref
        o_ref[...] = (a_ref[...] + b_ref[...]).astype(BF16)

    gs = pltpu.PrefetchScalarGridSpec(
        num_scalar_prefetch=1, grid=(N_CHIP, R // tr),
        in_specs=[pl.BlockSpec((1, tr, C), lambda j, r, ids: (2 * j + ids[0], r, 0)),
                  pl.BlockSpec((1, tr, C), lambda j, r, ids: (j, r, 0))],
        out_specs=pl.BlockSpec((1, tr, C), lambda j, r, ids: (j, r, 0)))
    return pl.pallas_call(body, name="pair_sum", grid_spec=gs, out_shape=_S((N_CHIP, R, C), BF16),
                          compiler_params=_params(("arbitrary", "arbitrary")))(core, blocked, recv)


def _adamw_math(w, g, m, v):
    m2 = ADAM_B1 * m + (1.0 - ADAM_B1) * g
    v2 = ADAM_B2 * v + (1.0 - ADAM_B2) * (g * g)
    m_hat = m2 / (1.0 - ADAM_B1 ** ADAM_STEP)
    v_hat = v2 / (1.0 - ADAM_B2 ** ADAM_STEP)
    delta = -ADAM_LR * (m_hat / (jnp.sqrt(v_hat) + ADAM_EPS) + ADAM_WD * w)
    return delta, m2, v2


def _sum_adamw(parts, w, m, v):
    R, C = w.shape
    tr = _row_tile(R)

    def body(p_ref, w_ref, m_ref, v_ref, g_ref, d_ref, m2_ref, v2_ref):
        g = p_ref[0].astype(F32)
        for j in range(1, N_CHIP):
            g = g + p_ref[j].astype(F32)
        g_ref[...] = g
        d_ref[...], m2_ref[...], v2_ref[...] = _adamw_math(w_ref[...], g, m_ref[...], v_ref[...])

    t = pl.BlockSpec((tr, C), lambda r: (r, 0))
    return pl.pallas_call(
        body, name="sum_adamw", grid=(R // tr,),
        in_specs=[pl.BlockSpec((N_CHIP, tr, C), lambda r: (0, r, 0)), t, t, t], out_specs=(t, t, t, t),
        out_shape=tuple(_S((R, C), F32) for _ in range(4)), compiler_params=_params(("arbitrary",)))(parts, w, m, v)


def _adamw(w, g, m, v):
    R, C = w.shape
    tr = _row_tile(R, 8)

    def body(w_ref, g_ref, m_ref, v_ref, d_ref, m2_ref, v2_ref):
        d_ref[...], m2_ref[...], v2_ref[...] = _adamw_math(w_ref[...], g_ref[...], m_ref[...], v_ref[...])

    t = pl.BlockSpec((tr, C), lambda r: (r, 0))
    return pl.pallas_call(body, name="adamw", grid=(R // tr,), in_specs=[t, t, t, t], out_specs=(t, t, t),
                          out_shape=tuple(_S((R, C), F32) for _ in range(3)),
                          compiler_params=_params(("arbitrary",)))(w, g, m, v)


def _ada_grad_adamw(cs16, dmod16, w, m, v):
    D, C6 = w.shape
    tr = _row_tile(D, 8, 256)

    def body(cs_ref, dm_ref, w_ref, m_ref, v_ref, g_ref, d_ref, m2_ref, v2_ref):
        g = _dot_tn(cs_ref[...].astype(BF16), dm_ref[...].astype(BF16))
        g_ref[...] = g
        d_ref[...], m2_ref[...], v2_ref[...] = _adamw_math(w_ref[...], g, m_ref[...], v_ref[...])

    t = pl.BlockSpec((tr, C6), lambda r: (r, 0))
    return pl.pallas_call(
        body, name="ada_grad_adamw", grid=(D // tr,),
        in_specs=[pl.BlockSpec((16, tr), lambda r: (0, r)), _acc((16, C6)), t, t, t], out_specs=(t, t, t, t),
        out_shape=tuple(_S((D, C6), F32) for _ in range(4)), compiler_params=_params(("arbitrary",)))(cs16, dmod16, w, m, v)


def _pick(n, cands):
    for c in cands:
        if n % c == 0:
            return c
    return n


def _matmul_tn(a, b, name, square_a=False):
    L, K = a.shape
    N = b.shape[1]
    bk = _pick(K, (1024, 512, 256, 128))
    bn = _pick(N, (1024, 768, 512, 256, 128))
    tl = _pick(L, (1024, 512, 256, 128))
    n_l = L // tl

    def body(a_ref, b_ref, o_ref):
        @pl.when(pl.program_id(2) == 0)
        def _():
            o_ref[...] = jnp.zeros_like(o_ref)
        av = a_ref[...]
        if square_a:
            av = av.astype(F32)
            av = av * av
        o_ref[...] += _dot_tn(av.astype(BF16), b_ref[...].astype(BF16))

    return pl.pallas_call(
        body, name=name, grid=(K // bk, N // bn, n_l),
        in_specs=[pl.BlockSpec((tl, bk), lambda k, n, l: (l, k)), pl.BlockSpec((tl, bn), lambda k, n, l: (l, n))],
        out_specs=pl.BlockSpec((bk, bn), lambda k, n, l: (k, n)), out_shape=_S((K, N), F32),
        compiler_params=_params(("arbitrary", "arbitrary", "arbitrary")))(a, b)


def _ln_in_proj(x, g, b, sc, sh, w_pad, dims):
    L, D = x.shape
    W, CD, AW, KVW2 = dims["W"], dims["CD"], dims["AW"], dims["KVW2"]
    NP = w_pad.shape[1]
    tm = _pick(L, (256, 128))
    o_z, o_xbc, o_q, o_kv, o_dt = 0, W, W + CD, W + CD + AW, W + CD + AW + KVW2

    def body(x_ref, g_ref, b_ref, sc_ref, sh_ref, w_ref, xhat_ref, rstd_ref, u1_ref, z_ref, xbc_ref, q_ref, kv_ref, dt_ref):
        xhat, rstd = _ln_fwd(x_ref[...])
        xhat_ref[...] = xhat
        rstd_ref[...] = rstd
        h0 = xhat * g_ref[...] + b_ref[...]
        u1 = (h0 * (1.0 + sc_ref[...]) + sh_ref[...]).astype(BF16)
        u1_ref[...] = u1
        z_ref[...] = _dot(u1, w_ref[:, o_z:o_xbc])
        xbc_ref[...] = _dot(u1, w_ref[:, o_xbc:o_q])
        q_ref[...] = _dot(u1, w_ref[:, o_q:o_kv]).astype(BF16)
        kv_ref[...] = _dot(u1, w_ref[:, o_kv:o_dt]).astype(BF16)
        dt_ref[...] = _dot(u1, w_ref[:, o_dt:NP])

    v = _acc((1, D))
    return pl.pallas_call(
        body, name="ln_in_proj", grid=(L // tm,),
        in_specs=[_tile(D, tm), v, v, v, v, _res((D, NP))],
        out_specs=(_tile(D, tm), _tile(1, tm), _tile(D, tm), _tile(W, tm), _tile(CD, tm), _tile(AW, tm),
                   _tile(KVW2, tm), _tile(LANE, tm)),
        out_shape=(_S((L, D), F32), _S((L, 1), F32), _S((L, D), BF16), _S((L, W), F32), _S((L, CD), F32),
                   _S((L, AW), BF16), _S((L, KVW2), BF16), _S((L, LANE), F32)),
        compiler_params=_params(("arbitrary",)))(x, g, b, sc, sh, w_pad)


def _conv_act(cur_ref, prev_ref, cw_ref, cb_ref, ext_ref, first):
    T = cur_ref.shape[0]
    ext_ref[0:HALO, :] = jnp.where(first, 0.0, prev_ref[...])
    ext_ref[HALO:HALO + T, :] = cur_ref[...]
    pre = cb_ref[...] + cw_ref[0:1, :] * ext_ref[HALO - 3:HALO - 3 + T, :]
    for k in range(1, CONV_K):
        pre = pre + cw_ref[k:k + 1, :] * ext_ref[HALO - 3 + k:HALO - 3 + k + T, :]
    return pre * _sigmoid(pre), pre


def _tri(T, upper=False):
    r = lax.broadcasted_iota(jnp.int32, (T, T), 0)
    c = lax.broadcasted_iota(jnp.int32, (T, T), 1)
    return (r <= c) if upper else (r >= c)


def _expand_heads(dst_ref, v, n_heads):
    for h in range(n_heads):
        dst_ref[:, h * HEAD_DIM:(h + 1) * HEAD_DIM] = jnp.broadcast_to(v[:, h:h + 1], (v.shape[0], HEAD_DIM))


def _head_reduce(v):
    wdt = v.shape[1]
    ch = lax.broadcasted_iota(jnp.int32, (wdt, LANE), 0)
    lo = lax.broadcasted_iota(jnp.int32, (wdt, LANE), 1) * HEAD_DIM
    onehot = ((ch >= lo) & (ch < lo + HEAD_DIM)).astype(BF16)
    hi = v.astype(BF16)
    rest = (v - hi.astype(F32)).astype(BF16)
    return _dot(hi, onehot) + _dot(rest, onehot)


def _conv_ssd(xbc, dt_raw, z, cw, cb, dtb, alog, dsk, nw, dims):
    L, CD = xbc.shape
    W, H, G, N = dims["W"], dims["H"], SSD_GROUPS, SSD_STATE
    T = CHUNK
    R = H // G
    GW = W // G
    nc = L // T
    HP = H * HEAD_DIM

    def body(xbc_ref, prev_ref, dt_ref, z_ref, cw_ref, cb_ref, dtb_ref, alog_ref, dsk_ref, nw_ref,
             y_ref, yn_ref, sp_ref, ext_ref, s_ref, ybuf_ref, dtx_ref, acx_ref, xb_ref):
        i = pl.program_id(0)

        @pl.when(i == 0)
        def _():
            s_ref[...] = jnp.zeros_like(s_ref)

        act, _ = _conv_act(xbc_ref, prev_ref, cw_ref, cb_ref, ext_ref, i == 0)
        xs = act[:, :W]
        dt = _softplus(dt_ref[...] + dtb_ref[...])
        a = dt * (-jnp.exp(alog_ref[...]))
        low = _tri(T)
        acum = _dot_hi(low.astype(F32), a)
        acum_t = _dot_hi(a.T, _tri(T, upper=True).astype(F32))
        _expand_heads(dtx_ref, dt, H)
        _expand_heads(acx_ref, acum, H)
        acx = acx_ref[...]
        lastx = acx[T - 1:T, :]
        xd = xs * dtx_ref[...]
        xb_ref[...] = xd.astype(BF16)
        xdb = (xd * jnp.exp(lastx - acx)).astype(BF16)
        ex = jnp.exp(acx)
        elx = jnp.exp(lastx)
        for g in range(G):
            gs = slice(g * GW, (g + 1) * GW)
            bgb = act[:, W + g * N:W + (g + 1) * N].astype(BF16)
            cgb = act[:, W + G * N + g * N:W + G * N + (g + 1) * N].astype(BF16)
            stg = s_ref[:, gs]
            sp_ref[0, :, gs] = stg
            yoff = ex[:, gs] * _dot(cgb, stg.astype(BF16))
            s_ref[:, gs] = stg * elx[:, gs] + _dot_tn(bgb, xdb[:, gs])
            cb_g = _dot_nt(cgb, bgb)
            for r in range(R):
                h = g * R + r
                hs = slice(h * HEAD_DIM, (h + 1) * HEAD_DIM)
                lm = jnp.where(low, jnp.exp(acum[:, h:h + 1] - acum_t[h:h + 1, :]), 0.0)
                ybuf_ref[:, hs] = _dot((cb_g * lm).astype(BF16), xb_ref[:, hs]) + yoff[:, r * HEAD_DIM:(r + 1) * HEAD_DIM]
        y = ybuf_ref[...] + dsk_ref[...] * xs
        y_ref[...] = y
        zz = z_ref[...]
        hh = y * (zz * _sigmoid(zz))
        for g in range(G):
            gs = slice(g * GW, (g + 1) * GW)
            hg = hh[:, gs]
            yn_ref[:, gs] = (hg * lax.rsqrt(_mean(hg * hg) + RMS_EPS) * nw_ref[:, gs]).astype(BF16)

    return pl.pallas_call(
        body, name="conv_ssd", grid=(nc,),
        in_specs=[_tile(CD, T), pl.BlockSpec((HALO, CD), lambda i: (jnp.maximum(i * (T // HALO) - 1, 0), 0)),
                  _tile(LANE, T), _tile(W, T), _acc((CONV_K, CD)), _acc((1, CD)), _acc((1, LANE)), _acc((1, LANE)),
                  _acc((1, W)), _acc((1, W))],
        out_specs=(_tile(W, T), _tile(W, T), pl.BlockSpec((1, N, HP), lambda i: (i, 0, 0))),
        out_shape=(_S((L, W), F32), _S((L, W), BF16), _S((nc, N, HP), F32)),
        scratch_shapes=[pltpu.VMEM((T + HALO, CD), F32), pltpu.VMEM((N, HP), F32), pltpu.VMEM((T, W), F32),
                        pltpu.VMEM((T, W), F32), pltpu.VMEM((T, W), F32), pltpu.VMEM((T, W), BF16)],
        compiler_params=_params(("arbitrary",)))(xbc, xbc, dt_raw, z, cw, cb, dtb, alog, dsk, nw)


def _attn_mask(T, i):
    r = lax.broadcasted_iota(jnp.int32, (T, 2 * T), 0)
    c = lax.broadcasted_iota(jnp.int32, (T, 2 * T), 1)
    dist = r + T - c
    valid = (dist >= 0) & (dist < CHUNK) & ((c >= T) | (i > 0))
    return dist.astype(F32), valid


def _attn_probs(s_raw, dist, valid, slope, sink, axis):
    s = s_raw * (HEAD_DIM ** -0.5) - slope * dist
    s = jnp.where(valid, s, NEG)
    m = jnp.maximum(jnp.max(s, axis=axis, keepdims=True), sink)
    p = jnp.exp(s - m)
    e_sink = jnp.exp(sink - m)
    inv = 1.0 / (jnp.sum(p, axis=axis, keepdims=True) + e_sink)
    return p * inv, e_sink * inv


def _kv_heads(kvc_ref, kvp_ref, g, n_kv):
    ks = slice(g * HEAD_DIM, (g + 1) * HEAD_DIM)
    vs = slice((n_kv + g) * HEAD_DIM, (n_kv + g + 1) * HEAD_DIM)
    kk = jnp.concatenate([kvp_ref[:, ks], kvc_ref[:, ks]], axis=0)
    vv = jnp.concatenate([kvp_ref[:, vs], kvc_ref[:, vs]], axis=0)
    return kk, vv


def _swa_fwd(q, kv, sinks, dims):
    L, AW = q.shape
    KV, KVW2 = dims["KV"], dims["KVW2"]
    T = CHUNK
    nb = L // T
    slopes = _alibi_slopes(dims["AH"])

    def body(q_ref, kvc_ref, kvp_ref, sink_ref, o_ref, qg_ref, p_ref):
        i = pl.program_id(0)
        dist, valid = _attn_mask(T, i)
        for g in range(KV):
            kk, vv = _kv_heads(kvc_ref, kvp_ref, g, KV)
            for r in range(GQA):
                h = g * GQA + r
                qg_ref[r * T:(r + 1) * T, :] = q_ref[:, h * HEAD_DIM:(h + 1) * HEAD_DIM]
            s_all = _dot_nt(qg_ref[...], kk)
            for r in range(GQA):
                h = g * GQA + r
                p, _ = _attn_probs(s_all[r * T:(r + 1) * T, :], dist, valid, slopes[h], sink_ref[h], -1)
                p_ref[r * T:(r + 1) * T, :] = p.astype(BF16)
            o_all = _dot(p_ref[...], vv)
            for r in range(GQA):
                h = g * GQA + r
                o_ref[:, h * HEAD_DIM:(h + 1) * HEAD_DIM] = o_all[r * T:(r + 1) * T, :].astype(BF16)

    return pl.pallas_call(
        body, name="swa_fwd", grid=(nb,),
        in_specs=[_tile(AW, T), _tile(KVW2, T), pl.BlockSpec((T, KVW2), lambda i: (jnp.maximum(i - 1, 0), 0)),
                  pl.BlockSpec(memory_space=pltpu.SMEM)],
        out_specs=_tile(AW, T), out_shape=_S((L, AW), BF16),
        scratch_shapes=[pltpu.VMEM((GQA * T, HEAD_DIM), BF16), pltpu.VMEM((GQA * T, 2 * T), BF16)],
        compiler_params=_params(("arbitrary",)))(q, kv, kv, sinks)


def _out_proj_ln1(yn, o, w_out, xhat0, vecs, alpha):
    L, W = yn.shape
    D = xhat0.shape[1]
    MIX = w_out.shape[0]
    tm = _pick(L, (256, 128))

    def body(yn_ref, o_ref, w_ref, xh_ref, v_ref, mix_ref, xhat1_ref, rstd1_ref, u2_ref):
        mix = _dot(yn_ref[...], w_ref[0:W, :]) + _dot(o_ref[...], w_ref[W:MIX, :])
        mix_ref[...] = mix
        h0 = xh_ref[...] * v_ref[0:1, :] + v_ref[1:2, :]
        xhat1, rstd1 = _ln_fwd(alpha * h0 + (1.0 + v_ref[2:3, :]) * mix)
        xhat1_ref[...] = xhat1
        rstd1_ref[...] = rstd1
        h1 = xhat1 * v_ref[3:4, :] + v_ref[4:5, :]
        u2_ref[...] = (h1 * (1.0 + v_ref[5:6, :]) + v_ref[6:7, :]).astype(BF16)

    return pl.pallas_call(
        body, name="out_proj_ln1", grid=(L // tm,),
        in_specs=[_tile(W, tm), _tile(MIX - W, tm), _res((MIX, D)), _tile(D, tm), _acc((8, D))],
        out_specs=(_tile(D, tm), _tile(D, tm), _tile(1, tm), _tile(D, tm)),
        out_shape=(_S((L, D), F32), _S((L, D), F32), _S((L, 1), F32), _S((L, D), BF16)),
        compiler_params=_params(("arbitrary",)))(yn, o, w_out, xhat0, vecs)


def _mlp_loss(u2, w1, w2, xhat1, tgt, vecs, b1, alpha):
    L, D = xhat1.shape
    FF = w1.shape[1]
    tm = _pick(L, (256, 128))
    fc = _pick(FF, (512, 256, 128))

    def body(u2_ref, w1_ref, w2_ref, xh_ref, t_ref, v_ref, b1_ref, rr_ref, dr2_ref, acc_ref, loss_ref):
        @pl.when(pl.program_id(0) == 0)
        def _():
            acc_ref[...] = jnp.zeros_like(acc_ref)
            loss_ref[...] = jnp.zeros_like(loss_ref)

        u2 = u2_ref[...]
        f = jnp.zeros((tm, D), F32) + v_ref[5:6, :]
        for j in range(FF // fc):
            cs = slice(j * fc, (j + 1) * fc)
            rr = jnp.maximum(_dot(u2, w1_ref[:, cs]) + b1_ref[:, cs], 0.0)
            rr_ref[:, cs] = rr.astype(BF16)
            f = f + _dot((rr * rr).astype(BF16), w2_ref[cs, :])
        xhat1 = xh_ref[...]
        h1 = xhat1 * v_ref[0:1, :] + v_ref[1:2, :]
        xhat2, rstd2 = _ln_fwd(alpha * h1 + (1.0 + v_ref[2:3, :]) * f)
        e = xhat2 * v_ref[3:4, :] + v_ref[4:5, :] - t_ref[...]
        loss_ref[...] += 0.5 * jnp.sum(_mean(e * e))
        dy = e * (1.0 / D)
        dr2 = _ln_bwd(dy * v_ref[3:4, :], xhat2, rstd2)
        dr2_ref[...] = dr2
        acc_ref[0:1, :] += _colsum(dy * xhat2)
        acc_ref[1:2, :] += _colsum(dy)
        acc_ref[2:3, :] += _colsum(dr2 * f)

    return pl.pallas_call(
        body, name="mlp_loss", grid=(L // tm,),
        in_specs=[_tile(D, tm), _res((D, FF)), _res((FF, D)), _tile(D, tm), _tile(D, tm), _acc((8, D)), _acc((1, FF))],
        out_specs=(_tile(FF, tm), _tile(D, tm), _acc((8, D)), _acc((1, LANE))),
        out_shape=(_S((L, FF), BF16), _S((L, D), F32), _S((8, D), F32), _S((1, LANE), F32)),
        compiler_params=_params(("arbitrary",)))(u2, w1, w2, xhat1, tgt, vecs, b1)


def _mlp_bwd_a(dr2, rr, w2, g2):
    L, D = dr2.shape
    FF = w2.shape[0]
    tm = _pick(L, (256, 128))
    fc = _pick(FF, (512, 256, 128))

    def body(dr2_ref, rr_ref, w2_ref, g2_ref, df_ref, da_ref, gb2_ref, gb1_ref):
        @pl.when(pl.program_id(0) == 0)
        def _():
            gb2_ref[...] = jnp.zeros_like(gb2_ref)
            gb1_ref[...] = jnp.zeros_like(gb1_ref)

        df = (1.0 + g2_ref[...]) * dr2_ref[...]
        gb2_ref[...] += _colsum(df)
        dfb = df.astype(BF16)
        df_ref[...] = dfb
        for j in range(FF // fc):
            cs = slice(j * fc, (j + 1) * fc)
            da = _dot_nt(dfb, w2_ref[cs, :]) * (2.0 * rr_ref[:, cs].astype(F32))
            gb1_ref[:, cs] += _colsum(da)
            da_ref[:, cs] = da.astype(BF16)

    return pl.pallas_call(
        body, name="mlp_bwd_a", grid=(L // tm,),
        in_specs=[_tile(D, tm), _tile(FF, tm), _res((FF, D)), _acc((1, D))],
        out_specs=(_tile(D, tm), _tile(FF, tm), _acc((1, D)), _acc((1, FF))),
        out_shape=(_S((L, D), BF16), _S((L, FF), BF16), _S((1, D), F32), _S((1, FF), F32)),
        compiler_params=_params(("arbitrary",)))(dr2, rr, w2, g2)


def _mlp_bwd_b(da, w1, dr2, xhat1, rstd1, mix, w_out, vecs, alpha, W):
    L, FF = da.shape
    D = dr2.shape[1]
    MIX = w_out.shape[0]
    tm = _pick(L, (256, 128))

    def body(da_ref, w1_ref, dr2_ref, xh_ref, rs_ref, mix_ref, wo_ref, v_ref, dmix_ref, dh0_ref, dyn_ref, do_ref, acc_ref):
        @pl.when(pl.program_id(0) == 0)
        def _():
            acc_ref[...] = jnp.zeros_like(acc_ref)

        du2 = _dot_nt(da_ref[...], w1_ref[...])
        xhat1 = xh_ref[...]
        h1 = xhat1 * v_ref[0:1, :] + v_ref[1:2, :]
        acc_ref[0:1, :] += _colsum(du2 * h1)
        acc_ref[1:2, :] += _colsum(du2)
        dh1 = alpha * dr2_ref[...] + du2 * (1.0 + v_ref[2:3, :])
        acc_ref[2:3, :] += _colsum(dh1 * xhat1)
        acc_ref[3:4, :] += _colsum(dh1)
        dr1 = _ln_bwd(dh1 * v_ref[0:1, :], xhat1, rs_ref[...])
        acc_ref[4:5, :] += _colsum(dr1 * mix_ref[...])
        dh0_ref[...] = alpha * dr1
        dmix = ((1.0 + v_ref[3:4, :]) * dr1).astype(BF16)
        dmix_ref[...] = dmix
        dyn_ref[...] = _dot_nt(dmix, wo_ref[0:W, :])
        do_ref[...] = _dot_nt(dmix, wo_ref[W:MIX, :]).astype(BF16)

    return pl.pallas_call(
        body, name="mlp_bwd_b", grid=(L // tm,),
        in_specs=[_tile(FF, tm), _res((D, FF)), _tile(D, tm), _tile(D, tm), _tile(1, tm), _tile(D, tm), _res((MIX, D)),
                  _acc((8, D))],
        out_specs=(_tile(D, tm), _tile(D, tm), _tile(W, tm), _tile(MIX - W, tm), _acc((8, D))),
        out_shape=(_S((L, D), BF16), _S((L, D), F32), _S((L, W), F32), _S((L, MIX - W), BF16), _S((8, D), F32)),
        compiler_params=_params(("arbitrary",)))(da, w1, dr2, xhat1, rstd1, mix, w_out, vecs)


def _swa_bwd(q, kv, do, sinks, dims):
    L, AW = q.shape
    KV, KVW2 = dims["KV"], dims["KVW2"]
    T = CHUNK
    nb = L // T
    slopes = _alibi_slopes(dims["AH"])
    scale = HEAD_DIM ** -0.5

    def body(q_ref, kvc_ref, kvp_ref, do_ref, sink_ref, dq_ref, dkv_ref, dsink_ref, carry_ref,
             qg_ref, dog_ref, pt_ref, dst_ref):
        i = pl.program_id(0)

        @pl.when(i == 0)
        def _():
            carry_ref[...] = jnp.zeros_like(carry_ref)
            dsink_ref[...] = jnp.zeros_like(dsink_ref)

        @pl.when(i < nb)
        def _():
            c = lax.broadcasted_iota(jnp.int32, (2 * T, T), 0)
            r_ = lax.broadcasted_iota(jnp.int32, (2 * T, T), 1)
            dist_i = r_ + T - c
            valid = (dist_i >= 0) & (dist_i < CHUNK) & ((c >= T) | (i > 0))
            dist = dist_i.astype(F32)
            lane = lax.broadcasted_iota(jnp.int32, (1, LANE), 1)
            dsink = jnp.zeros((1, LANE), F32)
            dks, dvs = [], []
            for g in range(KV):
                kk, vv = _kv_heads(kvc_ref, kvp_ref, g, KV)
                for r in range(GQA):
                    hs = slice((g * GQA + r) * HEAD_DIM, (g * GQA + r + 1) * HEAD_DIM)
                    qg_ref[r * T:(r + 1) * T, :] = q_ref[:, hs]
                    dog_ref[r * T:(r + 1) * T, :] = do_ref[:, hs]
                st_all = _dot_nt(kk, qg_ref[...])
                dpt_all = _dot_nt(vv, dog_ref[...])
                for r in range(GQA):
                    h = g * GQA + r
                    cs = slice(r * T, (r + 1) * T)
                    p, p_sink = _attn_probs(st_all[:, cs], dist, valid, slopes[h], sink_ref[h], 0)
                    dp = dpt_all[:, cs]
                    delta = jnp.sum(p * dp, axis=0, keepdims=True)
                    pt_ref[:, cs] = p.astype(BF16)
                    dst_ref[:, cs] = (p * (dp - delta)).astype(BF16)
                    dsink = dsink + jnp.where(lane == h, -jnp.sum(p_sink * delta), 0.0)
                dst = dst_ref[...]
                dks.append(_dot(dst, qg_ref[...]) * scale)
                dvs.append(_dot(pt_ref[...], dog_ref[...]))
                dq_all = _dot_tn(dst, kk) * scale
                for r in range(GQA):
                    hs = slice((g * GQA + r) * HEAD_DIM, (g * GQA + r + 1) * HEAD_DIM)
                    dq_ref[:, hs] = dq_all[r * T:(r + 1) * T, :].astype(BF16)
            dkv = jnp.concatenate(dks + dvs, axis=1)
            dsink_ref[...] += dsink
            dkv_ref[...] = carry_ref[...] + dkv[0:T, :]
            carry_ref[...] = dkv[T:2 * T, :]

        @pl.when(i == nb)
        def _():
            dkv_ref[...] = carry_ref[...]

    last = nb - 1
    return pl.pallas_call(
        body, name="swa_bwd", grid=(nb + 1,),
        in_specs=[pl.BlockSpec((T, AW), lambda i: (jnp.minimum(i, last), 0)),
                  pl.BlockSpec((T, KVW2), lambda i: (jnp.minimum(i, last), 0)),
                  pl.BlockSpec((T, KVW2), lambda i: (jnp.clip(i - 1, 0, last), 0)),
                  pl.BlockSpec((T, AW), lambda i: (jnp.minimum(i, last), 0)),
                  pl.BlockSpec(memory_space=pltpu.SMEM)],
        out_specs=(pl.BlockSpec((T, AW), lambda i: (jnp.minimum(i, last), 0)),
                   pl.BlockSpec((T, KVW2), lambda i: (jnp.maximum(i - 1, 0), 0)), _acc((1, LANE))),
        out_shape=(_S((L, AW), BF16), _S((L, KVW2), F32), _S((1, LANE), F32)),
        scratch_shapes=[pltpu.VMEM((T, KVW2), F32), pltpu.VMEM((GQA * T, HEAD_DIM), BF16),
                        pltpu.VMEM((GQA * T, HEAD_DIM), BF16), pltpu.VMEM((2 * T, GQA * T), BF16),
                        pltpu.VMEM((2 * T, GQA * T), BF16)],
        compiler_params=_params(("arbitrary",)))(q, kv, kv, do, sinks)


def _ssd_bwd(dyn, y, z, xbc, dt_raw, sprev, cw, cb, dtb, alog, dsk, nw, dims):
    L, CD = xbc.shape
    W, H, G, N = dims["W"], dims["H"], SSD_GROUPS, SSD_STATE
    T = CHUNK
    R = H // G
    GW = W // G
    nc = L // T
    HP = H * HEAD_DIM

    def body(dyn_ref, y_ref, z_ref, xbc_ref, prev_ref, dt_ref, sp_ref, cw_ref, cb_ref, dtb_ref, alog_ref, dsk_ref, nw_ref,
             dz_ref, dpre_ref, ddt_ref, acc_ref, hacc_ref, ext_ref, ds_ref, dtx_ref, acx_ref, xb_ref, dyb_ref, r12_ref,
             dx_ref, rows_ref):
        i = pl.program_id(0)

        @pl.when(i == 0)
        def _():
            ds_ref[...] = jnp.zeros_like(ds_ref)
            acc_ref[...] = jnp.zeros_like(acc_ref)
            hacc_ref[...] = jnp.zeros_like(hacc_ref)

        act, pre = _conv_act(xbc_ref, prev_ref, cw_ref, cb_ref, ext_ref, i == nc - 1)
        xs = act[:, :W]
        dt_in = dt_ref[...] + dtb_ref[...]
        dt = _softplus(dt_in)
        a_neg = -jnp.exp(alog_ref[...])
        a = dt * a_neg
        low = _tri(T)
        upf = _tri(T, upper=True).astype(F32)
        acum = _dot_hi(low.astype(F32), a)
        acum_t = _dot_hi(a.T, upf)

        y = y_ref[...]
        zz = z_ref[...]
        sg = _sigmoid(zz)
        sz = zz * sg
        hh = y * sz
        dyn_v = dyn_ref[...]
        parts = []
        for g in range(G):
            gs = slice(g * GW, (g + 1) * GW)
            hg = hh[:, gs]
            hhat = hg * lax.rsqrt(_mean(hg * hg) + RMS_EPS)
            rg = lax.rsqrt(_mean(hg * hg) + RMS_EPS)
            acc_ref[0:1, gs] += _colsum(dyn_v[:, gs] * hhat)
            dhhat = dyn_v[:, gs] * nw_ref[:, gs]
            parts.append(rg * (dhhat - hhat * _mean(dhhat * hhat)))
        dhh = jnp.concatenate(parts, axis=1)
        dy = dhh * sz
        dz_ref[...] = (dhh * y * (sg * (1.0 + zz * (1.0 - sg)))).astype(BF16)
        acc_ref[1:2, :] += _colsum(dy * xs)
        dyb_ref[...] = dy.astype(BF16)

        _expand_heads(dtx_ref, dt, H)
        _expand_heads(acx_ref, acum, H)
        dtx = dtx_ref[...]
        acx = acx_ref[...]
        lastx = acx[T - 1:T, :]
        ex = jnp.exp(acx)
        decx = jnp.exp(lastx - acx)
        elx = jnp.exp(lastx)
        xd = xs * dtx
        xb_ref[...] = xd.astype(BF16)
        xdecb = (xd * decx).astype(BF16)
        dgb = (ex * dy).astype(BF16)
        rows_ref[...] = jnp.zeros_like(rows_ref)

        lane = lax.broadcasted_iota(jnp.int32, (T, LANE), 1)
        sub = lax.broadcasted_iota(jnp.int32, (T, LANE), 0)
        subr = lax.broadcasted_iota(jnp.int32, (LANE, T), 0)
        da_col = jnp.zeros((T, LANE), F32)
        da_row = jnp.zeros((LANE, T), F32)
        dbs, dcs = [], []
        for g in range(G):
            gs = slice(g * GW, (g + 1) * GW)
            bgb = act[:, W + g * N:W + (g + 1) * N].astype(BF16)
            cgb = act[:, W + G * N + g * N:W + G * N + (g + 1) * N].astype(BF16)
            stg = sp_ref[0, :, gs]
            stb = stg.astype(BF16)
            dsn = ds_ref[:, gs]
            dsnb = dsn.astype(BF16)
            gm = _dot(cgb, stb)
            dc = _dot_nt(dgb[:, gs], stb)
            dsp = _dot_tn(cgb, dgb[:, gs])
            dxs_ = decx[:, gs] * _dot(bgb, dsnb)
            db = _dot_nt(xdecb[:, gs], dsnb)
            xdg = xd[:, gs]
            r12_ref[:, gs] = dy[:, gs] * ex[:, gs] * gm - xdg * dxs_
            rows_ref[0:1, gs] = _colsum(dsn * stg) * elx[:, gs]
            rows_ref[1:2, gs] = _colsum(xdg * dxs_)
            ds_ref[:, gs] = dsp + dsn * elx[:, gs]
            cb_g = _dot_nt(cgb, bgb)
            dcb = jnp.zeros((T, T), F32)
            for r in range(R):
                h = g * R + r
                hs = slice(h * HEAD_DIM, (h + 1) * HEAD_DIM)
                lm = jnp.where(low, jnp.exp(acum[:, h:h + 1] - acum_t[h:h + 1, :]), 0.0)
                mm = cb_g * lm
                dyb = dyb_ref[:, hs]
                dm = _dot_nt(dyb, xb_ref[:, hs])
                dx_ref[:, hs] = dxs_[:, r * HEAD_DIM:(r + 1) * HEAD_DIM] + _dot_tn(mm.astype(BF16), dyb)
                dcb = dcb + dm * lm
                qm = dm * mm
                da_col = jnp.where(lane == h, jnp.sum(qm, axis=1, keepdims=True), da_col)
                da_row = jnp.where(subr == h, jnp.sum(qm, axis=0, keepdims=True), da_row)
            dcbb = dcb.astype(BF16)
            dcs.append(dc + _dot(dcbb, bgb))
            dbs.append(db + _dot_tn(dcbb, cgb))
        dx = dx_ref[...]
        rows = _head_reduce(rows_ref[...])
        dlast = rows[0:1, :] + rows[1:2, :]
        da_col = da_col + _head_reduce(r12_ref[...]) + jnp.where(sub == T - 1, dlast, 0.0)
        dacum = da_col - da_row.T
        da = _dot_hi(upf, dacum)
        ddt = _head_reduce(dx * xs) + da * a_neg
        hacc_ref[1:2, :] += _colsum(da * dt) * a_neg
        ddt_raw = ddt * _sigmoid(dt_in)
        hacc_ref[0:1, :] += _colsum(ddt_raw)
        ddt_ref[...] = ddt_raw
        dact = jnp.concatenate([dsk_ref[...] * dy + dx * dtx] + dbs + dcs, axis=1)
        spre = _sigmoid(pre)
        dpre_ref[...] = dact * (spre * (1.0 + pre * (1.0 - spre)))

        @pl.when(i == nc - 1)
        def _():
            ch = lax.broadcasted_iota(jnp.int32, (W, LANE), 0)
            lo = lax.broadcasted_iota(jnp.int32, (W, LANE), 1) * HEAD_DIM
            hacc_ref[2:3, :] = _dot_hi(acc_ref[1:2, :], ((ch >= lo) & (ch < lo + HEAD_DIM)).astype(F32))

    rev = lambda i: (nc - 1 - i, 0)
    return pl.pallas_call(
        body, name="ssd_bwd", grid=(nc,),
        in_specs=[pl.BlockSpec((T, W), rev), pl.BlockSpec((T, W), rev), pl.BlockSpec((T, W), rev), pl.BlockSpec((T, CD), rev),
                  pl.BlockSpec((HALO, CD), lambda i: (jnp.maximum((nc - 1 - i) * (T // HALO) - 1, 0), 0)),
                  pl.BlockSpec((T, LANE), rev), pl.BlockSpec((1, N, HP), lambda i: (nc - 1 - i, 0, 0)),
                  _acc((CONV_K, CD)), _acc((1, CD)), _acc((1, LANE)), _acc((1, LANE)), _acc((1, W)), _acc((1, W))],
        out_specs=(pl.BlockSpec((T, W), rev), pl.BlockSpec((T, CD), rev), pl.BlockSpec((T, LANE), rev), _acc((8, W)),
                   _acc((8, LANE))),
        out_shape=(_S((L, W), BF16), _S((L, CD), F32), _S((L, LANE), F32), _S((8, W), F32), _S((8, LANE), F32)),
        scratch_shapes=[pltpu.VMEM((T + HALO, CD), F32), pltpu.VMEM((N, HP), F32), pltpu.VMEM((T, W), F32),
                        pltpu.VMEM((T, W), F32), pltpu.VMEM((T, W), BF16), pltpu.VMEM((T, W), BF16), pltpu.VMEM((T, W), F32),
                        pltpu.VMEM((T, W), F32), pltpu.VMEM((8, W), F32)],
        compiler_params=_params(("arbitrary",)))(dyn, y, z, xbc, xbc, dt_raw, sprev, cw, cb, dtb, alog, dsk, nw)


def _conv_bwd(dpre, xbc, cw):
    L, CD = xbc.shape
    tm = _pick(L, (256, 128))
    nt = L // tm
    hb = tm // HALO

    def body(dp_ref, dn_ref, u_ref, up_ref, cw_ref, du_ref, acc_ref, extu_ref, extd_ref):
        i = pl.program_id(0)

        @pl.when(i == 0)
        def _():
            acc_ref[...] = jnp.zeros_like(acc_ref)

        dp = dp_ref[...]
        extu_ref[0:HALO, :] = jnp.where(i == 0, 0.0, up_ref[...])
        extu_ref[HALO:HALO + tm, :] = u_ref[...]
        extd_ref[0:tm, :] = dp
        extd_ref[tm:tm + HALO, :] = jnp.where(i == nt - 1, 0.0, dn_ref[...])
        du = cw_ref[CONV_K - 1:CONV_K, :] * dp
        acc_ref[CONV_K - 1:CONV_K, :] += _colsum(dp * u_ref[...])
        for k in range(CONV_K - 1):
            s = CONV_K - 1 - k
            du = du + cw_ref[k:k + 1, :] * extd_ref[s:s + tm, :]
            acc_ref[k:k + 1, :] += _colsum(dp * extu_ref[HALO - s:HALO - s + tm, :])
        acc_ref[CONV_K:CONV_K + 1, :] += _colsum(dp)
        du_ref[...] = du.astype(BF16)

    return pl.pallas_call(
        body, name="conv_bwd", grid=(nt,),
        in_specs=[_tile(CD, tm), pl.BlockSpec((HALO, CD), lambda i: (jnp.minimum((i + 1) * hb, nt * hb - 1), 0)),
                  _tile(CD, tm), pl.BlockSpec((HALO, CD), lambda i: (jnp.maximum(i * hb - 1, 0), 0)), _acc((CONV_K, CD))],
        out_specs=(_tile(CD, tm), _acc((8, CD))),
        out_shape=(_S((L, CD), BF16), _S((8, CD), F32)),
        scratch_shapes=[pltpu.VMEM((tm + HALO, CD), F32), pltpu.VMEM((tm + HALO, CD), F32)],
        compiler_params=_params(("arbitrary",)))(dpre, dpre, xbc, xbc, cw)


def _in_proj_bwd(dz, dxbc, dq, dkv, ddt, w_pad, xhat0, rstd0, dh0p, vecs, dims):
    L, D = xhat0.shape
    W, CD, AW, KVW2 = dims["W"], dims["CD"], dims["AW"], dims["KVW2"]
    NP = w_pad.shape[1]
    tm = _pick(L, (256, 128))
    o_xbc, o_q, o_kv, o_dt = W, W + CD, W + CD + AW, W + CD + AW + KVW2

    def body(dz_ref, dxbc_ref, dq_ref, dkv_ref, ddt_ref, w_ref, xh_ref, rs_ref, dh0_ref, v_ref, gx_ref, acc_ref):
        @pl.when(pl.program_id(0) == 0)
        def _():
            acc_ref[...] = jnp.zeros_like(acc_ref)

        du1 = _dot_nt(dz_ref[...], w_ref[:, 0:o_xbc])
        du1 = du1 + _dot_nt(dxbc_ref[...], w_ref[:, o_xbc:o_q])
        du1 = du1 + _dot_nt(dq_ref[...], w_ref[:, o_q:o_kv])
        du1 = du1 + _dot_nt(dkv_ref[...].astype(BF16), w_ref[:, o_kv:o_dt])
        du1 = du1 + _dot_nt(ddt_ref[...].astype(BF16), w_ref[:, o_dt:NP])
        xhat0 = xh_ref[...]
        h0 = xhat0 * v_ref[0:1, :] + v_ref[1:2, :]
        acc_ref[0:1, :] += _colsum(du1 * h0)
        acc_ref[1:2, :] += _colsum(du1)
        dh0 = dh0_ref[...] + du1 * (1.0 + v_ref[2:3, :])
        acc_ref[2:3, :] += _colsum(dh0 * xhat0)
        acc_ref[3:4, :] += _colsum(dh0)
        gx_ref[...] = _ln_bwd(dh0 * v_ref[0:1, :], xhat0, rs_ref[...])

    return pl.pallas_call(
        body, name="in_proj_bwd", grid=(L // tm,),
        in_specs=[_tile(W, tm), _tile(CD, tm), _tile(AW, tm), _tile(KVW2, tm), _tile(LANE, tm), _res((D, NP)),
                  _tile(D, tm), _tile(1, tm), _tile(D, tm), _acc((8, D))],
        out_specs=(_tile(D, tm), _acc((8, D))),
        out_shape=(_S((L, D), F32), _S((8, D), F32)),
        compiler_params=_params(("arbitrary",)))(dz, dxbc, dq, dkv, ddt, w_pad, xhat0, rstd0, dh0p, vecs)


_WEIGHTS = ['ln_in_g', 'ln_in_b', 'ada_w', 'ada_b', 'w_in', 'conv_w', 'conv_b', 'dt_bias', 'a_log', 'd_skip', 'ssd_norm_w',
            'attn_sinks', 'w_out', 'ln1_g', 'ln1_b', 'w_ff1', 'b_ff1', 'w_ff2', 'b_ff2', 'ln2_g', 'ln2_b']
_BIG = ('w_in', 'w_out', 'w_ff1', 'w_ff2')
_SMALL = ('ada_b', 'ln_in_g', 'ln_in_b', 'conv_b', 'dt_bias', 'a_log', 'd_skip', 'ssd_norm_w', 'attn_sinks', 'ln1_g', 'ln1_b',
          'b_ff1', 'b_ff2', 'ln2_g', 'ln2_b')


def _pad_lanes(v, n=None):
    v = v.reshape(1, -1)
    n = n or -(-v.shape[1] // LANE) * LANE
    return jnp.pad(v, ((0, 0), (0, n - v.shape[1])))


def _vec8(rows, D):
    rows = [r.reshape(1, D) for r in rows]
    return jnp.concatenate(rows + [jnp.zeros((8 - len(rows), D), F32)], axis=0)


def _pack(segs):
    flat, offs, sizes, o = [], [], [], 0
    for s in segs:
        p = _pad_lanes(s)
        flat.append(p)
        offs.append(o)
        sizes.append(s.size)
        o += p.shape[1]
    total = -(-o // (8 * LANE)) * (8 * LANE)
    if total > o:
        flat.append(jnp.zeros((1, total - o), F32))
    return jnp.concatenate(flat, axis=1).reshape(8, total // 8), offs, sizes


def kernel(x, c, ln_in_g, ln_in_b, ada_w, ada_b, w_in, conv_w, conv_b, dt_bias, a_log, d_skip, ssd_norm_w, attn_sinks, w_out, ln1_g, ln1_b, w_ff1, b_ff1, w_ff2, b_ff2, ln2_g, ln2_b, loss_target, m_ln_in_g, m_ln_in_b, m_ada_w, m_ada_b, m_w_in, m_conv_w, m_conv_b, m_dt_bias, m_a_log, m_d_skip, m_ssd_norm_w, m_attn_sinks, m_w_out, m_ln1_g, m_ln1_b, m_w_ff1, m_b_ff1, m_w_ff2, m_b_ff2, m_ln2_g, m_ln2_b, v_ln_in_g, v_ln_in_b, v_ada_w, v_ada_b, v_w_in, v_conv_w, v_conv_b, v_dt_bias, v_a_log, v_d_skip, v_ssd_norm_w, v_attn_sinks, v_w_out, v_ln1_g, v_ln1_b, v_w_ff1, v_b_ff1, v_w_ff2, v_b_ff2, v_ln2_g, v_ln2_b):
    wts = dict(ln_in_g=ln_in_g, ln_in_b=ln_in_b, ada_w=ada_w, ada_b=ada_b, w_in=w_in, conv_w=conv_w, conv_b=conv_b,
               dt_bias=dt_bias, a_log=a_log, d_skip=d_skip, ssd_norm_w=ssd_norm_w, attn_sinks=attn_sinks, w_out=w_out,
               ln1_g=ln1_g, ln1_b=ln1_b, w_ff1=w_ff1, b_ff1=b_ff1, w_ff2=w_ff2, b_ff2=b_ff2, ln2_g=ln2_g, ln2_b=ln2_b)
    ms = dict(ln_in_g=m_ln_in_g, ln_in_b=m_ln_in_b, ada_w=m_ada_w, ada_b=m_ada_b, w_in=m_w_in, conv_w=m_conv_w,
              conv_b=m_conv_b, dt_bias=m_dt_bias, a_log=m_a_log, d_skip=m_d_skip, ssd_norm_w=m_ssd_norm_w,
              attn_sinks=m_attn_sinks, w_out=m_w_out, ln1_g=m_ln1_g, ln1_b=m_ln1_b, w_ff1=m_w_ff1, b_ff1=m_b_ff1,
              w_ff2=m_w_ff2, b_ff2=m_b_ff2, ln2_g=m_ln2_g, ln2_b=m_ln2_b)
    vs = dict(ln_in_g=v_ln_in_g, ln_in_b=v_ln_in_b, ada_w=v_ada_w, ada_b=v_ada_b, w_in=v_w_in, conv_w=v_conv_w,
              conv_b=v_conv_b, dt_bias=v_dt_bias, a_log=v_a_log, d_skip=v_d_skip, ssd_norm_w=v_ssd_norm_w,
              attn_sinks=v_attn_sinks, w_out=v_w_out, ln1_g=v_ln1_g, ln1_b=v_ln1_b, w_ff1=v_w_ff1, b_ff1=v_b_ff1,
              w_ff2=v_w_ff2, b_ff2=v_b_ff2, ln2_g=v_ln2_g, ln2_b=v_ln2_b)

    L, D = x.shape[1], x.shape[2]
    depth = w_in.shape[0]
    assert depth == 1 and x.shape[0] == 1 and L % CHUNK == 0
    W = D
    H = W // HEAD_DIM
    CD = W + 2 * SSD_GROUPS * SSD_STATE
    AW = D
    AH = AW // HEAD_DIM
    KV = AH // GQA
    KVW2 = 2 * KV * HEAD_DIM
    PROJ = W + CD + H + AW + KVW2
    FF = w_ff1.shape[2] * N_DEV
    MIX = w_out.shape[1] * N_DEV
    assert w_in.shape[2] * N_DEV == PROJ and MIX == W + AW and H <= LANE and AH <= LANE
    dims = dict(W=W, H=H, CD=CD, AW=AW, AH=AH, KV=KV, KVW2=KVW2)
    alpha = (2.0 * depth) ** 0.25
    C6 = ada_w.shape[2]
    CW = conv_w.shape[2]

    ax, ay, ac = _my_pos()
    me = 4 * ax + 2 * ay + ac
    x2 = x.reshape(L, D)
    tgt = loss_target.reshape(L, D)
    r1 = lambda a: a.reshape(1, -1)

    ada_b_cols = lax.dynamic_slice(ada_b, (0, me * C6), (1, C6))
    cs_all, mod = _mod_fwd(c, ada_w[0], ada_b_cols)
    sh1, sc1, g1, sh2, sc2, g2 = [r1(t) for t in jnp.split(mod.reshape(-1), 6)]

    wg_in, wg_out, wg_ff1, wg_ff2, cwg = _ag_weights(
        [w_in[0].astype(BF16), w_out[0].astype(BF16), w_ff1[0].astype(BF16), w_ff2[0].astype(BF16), conv_w[0]])
    w_in_full = wg_in.transpose(1, 0, 2).reshape(D, PROJ)
    i1, i2, i3, i4 = W, W + CD, W + CD + H, W + CD + H + AW
    w_pad = jnp.concatenate([w_in_full[:, :i2], w_in_full[:, i3:], w_in_full[:, i2:i3], jnp.zeros((D, LANE - H), BF16)], axis=1)
    w_out_full = wg_out.reshape(MIX, D)
    w1_full = wg_ff1.transpose(1, 0, 2).reshape(D, FF)
    w2_full = wg_ff2.reshape(FF, D)
    cw_full = cwg.transpose(1, 0, 2).reshape(CONV_K, CD)

    dtb = _pad_lanes(dt_bias, LANE)
    alog = _pad_lanes(a_log, LANE)
    dsk = jnp.repeat(d_skip.reshape(-1), HEAD_DIM).reshape(1, W)
    sinks = attn_sinks.reshape(-1)
    g_in, b_in = r1(ln_in_g), r1(ln_in_b)

    xhat0, rstd0, u1, z, xbc, q, kv, dt_raw = _ln_in_proj(x2, g_in, b_in, sc1, sh1, w_pad, dims)
    y, yn, sprev = _conv_ssd(xbc, dt_raw, z, cw_full, conv_b, dtb, alog, dsk, ssd_norm_w, dims)
    o = _swa_fwd(q, kv, sinks, dims)
    mix, xhat1, rstd1, u2 = _out_proj_ln1(yn, o, w_out_full, xhat0, _vec8([g_in, b_in, g1, ln1_g, ln1_b, sc2, sh2], D), alpha)
    rr, dr2, acc_f, loss_loc = _mlp_loss(u2, w1_full, w2_full, xhat1, tgt,
                                         _vec8([ln1_g, ln1_b, g2, ln2_g, ln2_b, b_ff2], D), b_ff1, alpha)

    df, da, gb2, gb1 = _mlp_bwd_a(dr2, rr, w2_full, g2)
    gw_ff2 = _matmul_tn(rr, df, "gw_ff2", square_a=True)
    gw_ff1 = _matmul_tn(u2, da, "gw_ff1")
    dmix, dh0p, dyn, do, acc_b = _mlp_bwd_b(da, w1_full, dr2, xhat1, rstd1, mix, w_out_full,
                                            _vec8([ln1_g, ln1_b, sc2, g1], D), alpha, W)
    gw_out = jnp.concatenate([_matmul_tn(yn, dmix, "gw_out_ssd"), _matmul_tn(o, dmix, "gw_out_attn")], axis=0)
    dq, dkv, dsink = _swa_bwd(q, kv, do, sinks, dims)
    dz, dpre, ddt, acc_s, hacc = _ssd_bwd(dyn, y, z, xbc, dt_raw, sprev, cw_full, conv_b, dtb, alog, dsk, ssd_norm_w, dims)
    dxbc, acc_c = _conv_bwd(dpre, xbc, cw_full)
    grad_x, acc_i = _in_proj_bwd(dz, dxbc, dq, dkv, ddt, w_pad, xhat0, rstd0, dh0p, _vec8([g_in, b_in, sc1], D), dims)
    gz = _matmul_tn(u1, dz, "gw_in_z")
    gxbc = _matmul_tn(u1, dxbc, "gw_in_xbc")
    gq = _matmul_tn(u1, dq, "gw_in_q")
    gkv = _matmul_tn(u1, dkv, "gw_in_kv")
    gdt = _matmul_tn(u1, ddt, "gw_in_dt")
    gw_in = jnp.concatenate([gz, gxbc, gdt[:, :H], gq, gkv], axis=1)

    dmod = jnp.concatenate([acc_i[1], acc_i[0], acc_b[4], acc_b[1], acc_b[0], acc_f[2]])
    small_g = dict(ada_b=dmod, ln_in_g=acc_i[2], ln_in_b=acc_i[3], conv_b=acc_c[CONV_K], dt_bias=hacc[0, :H], a_log=hacc[1, :H],
                   d_skip=hacc[2, :H], ssd_norm_w=acc_s[0], attn_sinks=dsink[0, :AH], ln1_g=acc_b[2], ln1_b=acc_b[3],
                   b_ff1=gb1[0], b_ff2=gb2[0], ln2_g=acc_f[0], ln2_b=acc_f[1])
    segs = [small_g[n] for n in _SMALL] + [acc_c[:CONV_K].reshape(-1), loss_loc[0, :1]]
    pack, offs, sizes = _pack(segs)
    gathered, summed = _small_gather_sum(pack)
    gathered = gathered.reshape(N_DEV, -1)
    summed = summed.reshape(-1)
    seg = lambda k: summed[offs[k]:offs[k] + sizes[k]]
    grads = {n: seg(k).reshape(wts[n].shape) for k, n in enumerate(_SMALL)}
    gcw_full = seg(len(_SMALL)).reshape(CONV_K, CD)
    grads['conv_w'] = lax.dynamic_slice(gcw_full, (0, me * CW), (CONV_K, CW)).reshape(conv_w.shape)
    loss = seg(len(_SMALL) + 1)[0]

    names = list(_SMALL) + ['conv_w']
    pw, poffs, psizes = _pack([wts[n] for n in names])
    pg, _, _ = _pack([grads[n] for n in names])
    pm, _, _ = _pack([ms[n] for n in names])
    pv, _, _ = _pack([vs[n] for n in names])
    pd, pm2, pv2 = [t.reshape(-1) for t in _adamw(pw, pg, pm, pv)]
    deltas, new_m, new_v = {}, {}, {}
    for k, n in enumerate(names):
        sl = slice(poffs[k], poffs[k] + psizes[k])
        deltas[n], new_m[n], new_v[n] = (t[sl].reshape(wts[n].shape) for t in (pd, pm2, pv2))

    dmod_cols = lax.dynamic_slice(gathered, (0, offs[0] + me * C6), (N_DEV, C6))
    pad16 = lambda t: jnp.concatenate([t, jnp.zeros((16 - N_DEV,) + t.shape[1:], t.dtype)], axis=0)
    g_, d_, m_, v_ = _ada_grad_adamw(pad16(cs_all), pad16(dmod_cols), ada_w[0], m_ada_w[0], v_ada_w[0])
    grads['ada_w'], deltas['ada_w'], new_m['ada_w'], new_v['ada_w'] = (t[None] for t in (g_, d_, m_, v_))

    blocked = [gw_in.reshape(D, N_DEV, PROJ // N_DEV).transpose(1, 0, 2), gw_out.reshape(N_DEV, MIX // N_DEV, D),
               gw_ff1.reshape(D, N_DEV, FF // N_DEV).transpose(1, 0, 2), gw_ff2.reshape(N_DEV, FF // N_DEV, D)]
    recv = _rs_d2d(blocked)
    core = jnp.reshape(ac, (1,)).astype(jnp.int32)
    pairs = [_pair_sum(b, r, core) for b, r in zip(blocked, recv)]
    parts = _rs_ici(pairs)
    for n, p in zip(_BIG, parts):
        g_, d_, m_, v_ = _sum_adamw(p, wts[n][0], ms[n][0], vs[n][0])
        grads[n], deltas[n], new_m[n], new_v[n] = (t[None] for t in (g_, d_, m_, v_))

    return (loss, grad_x.reshape(x.shape), *[grads[n] for n in _WEIGHTS], *[deltas[n] for n in _WEIGHTS],
            *[new_m[n] for n in _WEIGHTS], *[new_v[n] for n in _WEIGHTS])
```

```python
import functools
import math

import numpy as np
import jax
import jax.numpy as jnp
from jax import lax
from jax.experimental import pallas as pl
from jax.experimental.pallas import tpu as pltpu

F32 = jnp.float32
BF16 = jnp.bfloat16
MESH = pl.DeviceIdType.MESH

N_DEV = 8
N_CHIP = 4
HEAD_DIM = 64
SSD_GROUPS = 2
SSD_STATE = 128
CHUNK = 128
CONV_K = 4
GQA = 8
LANE = 128
HALO = 8
LN_EPS = 1e-5
RMS_EPS = 1e-5
NEG = -1e30
ADAM_LR, ADAM_B1, ADAM_B2, ADAM_EPS, ADAM_WD, ADAM_STEP = 0.001, 0.9, 0.999, 1e-08, 0.01, 10
V7X_VMEM_BYTES = 64 * 1024 * 1024
VMEM_LIMIT = V7X_VMEM_BYTES - 8 * 1024 * 1024
HI = lax.Precision.HIGHEST


def _alibi_slopes(n):
    def pow2(m):
        start = 2.0 ** (-8.0 / m)
        return [start ** (i + 1) for i in range(m)]
    if math.log2(n).is_integer():
        s = pow2(n)
    else:
        c = 2 ** math.floor(math.log2(n))
        s = pow2(c) + pow2(2 * c)[0::2][: n - c]
    return [float(v) for v in np.array(s, dtype=np.float32)]


def _dot(a, b):
    return jnp.dot(a, b, preferred_element_type=F32)


def _dot_nt(a, b):
    return lax.dot_general(a, b, (((1,), (1,)), ((), ())), preferred_element_type=F32)


def _dot_tn(a, b):
    return lax.dot_general(a, b, (((0,), (0,)), ((), ())), preferred_element_type=F32)


def _dot_hi(a, b):
    return jnp.dot(a, b, precision=HI, preferred_element_type=F32)


def _sigmoid(x):
    return 1.0 / (1.0 + jnp.exp(-x))


def _softplus(x):
    return jnp.maximum(x, 0.0) + jnp.log(1.0 + jnp.exp(-jnp.abs(x)))


def _mean(x):
    return jnp.mean(x, axis=-1, keepdims=True)


def _ln_fwd(x):
    xc = x - _mean(x)
    rstd = lax.rsqrt(_mean(xc * xc) + LN_EPS)
    return xc * rstd, rstd


def _ln_bwd(dxhat, xhat, rstd):
    return rstd * (dxhat - _mean(dxhat) - xhat * _mean(dxhat * xhat))


def _colsum(x):
    return jnp.sum(x, axis=0, keepdims=True)


def _params(sem):
    return pltpu.CompilerParams(dimension_semantics=sem, vmem_limit_bytes=VMEM_LIMIT)


def _tile(i_map_cols, tm):
    return pl.BlockSpec((tm, i_map_cols), lambda i: (i, 0))


def _res(shape):
    return pl.BlockSpec(shape, lambda *_: (0,) * len(shape), pipeline_mode=pl.Buffered(1))


def _acc(shape):
    return pl.BlockSpec(shape, lambda *_: (0,) * len(shape))


def _S(shape, dtype):
    return jax.ShapeDtypeStruct(shape, dtype)


def _my_pos():
    return lax.axis_index("x"), lax.axis_index("y"), lax.axis_index("c")


def _peer(pos, k):
    x, y, c = pos
    px = 1 - x if k & 4 else x
    py = 1 - y if k & 2 else y
    pc = 1 - c if k & 1 else c
    return (px, py, pc)


def _lin(p):
    return 4 * p[0] + 2 * p[1] + p[2]


def _mod_fwd(c_loc, ada_w_loc, ada_b_cols):
    D = c_loc.shape[1]
    C6 = ada_w_loc.shape[1]

    def body(c_ref, w_ref, b_ref, cs_ref, mod_ref, call_ref, modp_ref, ssem, rsem):
        pos = _my_pos()
        me = _lin(pos)
        call_ref[me] = c_ref[...]
        sends = []
        for k in range(1, N_DEV):
            cp = pltpu.make_async_remote_copy(src_ref=c_ref, dst_ref=call_ref.at[me], send_sem=ssem.at[k - 1],
                                              recv_sem=rsem.at[k - 1], device_id=_peer(pos, k), device_id_type=MESH)
            cp.start()
            sends.append(cp)
        for k in range(1, N_DEV):
            src = _lin(_peer(pos, k))
            pltpu.make_async_remote_copy(src_ref=c_ref, dst_ref=call_ref.at[src], send_sem=ssem.at[k - 1],
                                         recv_sem=rsem.at[k - 1], device_id=pos, device_id_type=MESH).wait_recv()
        for cp in sends:
            cp.wait_send()
        call = jnp.concatenate([call_ref[b] for b in range(N_DEV)], axis=0)
        cs = call * _sigmoid(call)
        cs_ref[...] = cs
        modp = _dot(cs.astype(BF16), w_ref[...].astype(BF16)) + b_ref[...]
        for b in range(N_DEV):
            modp_ref[b] = modp[b:b + 1, :]
        mod_ref[me] = modp_ref[me]
        sends = []
        for k in range(1, N_DEV):
            peer = _peer(pos, k)
            cp = pltpu.make_async_remote_copy(src_ref=modp_ref.at[_lin(peer)], dst_ref=mod_ref.at[me],
                                              send_sem=ssem.at[N_DEV - 2 + k], recv_sem=rsem.at[N_DEV - 2 + k],
                                              device_id=peer, device_id_type=MESH)
            cp.start()
            sends.append(cp)
        for k in range(1, N_DEV):
            src = _lin(_peer(pos, k))
            pltpu.make_async_remote_copy(src_ref=modp_ref.at[src], dst_ref=mod_ref.at[src],
                                         send_sem=ssem.at[N_DEV - 2 + k], recv_sem=rsem.at[N_DEV - 2 + k],
                                         device_id=pos, device_id_type=MESH).wait_recv()
        for cp in sends:
            cp.wait_send()

    vm = pl.BlockSpec(memory_space=pltpu.VMEM)
    return pl.pallas_call(
        body, name="mod_fwd",
        out_shape=(_S((N_DEV, D), F32), _S((N_DEV, 1, C6), F32)),
        in_specs=[vm, vm, vm], out_specs=(vm, vm),
        scratch_shapes=[pltpu.VMEM((N_DEV, 1, D), F32), pltpu.VMEM((N_DEV, 1, C6), F32),
                        pltpu.SemaphoreType.DMA((2 * (N_DEV - 1),)), pltpu.SemaphoreType.DMA((2 * (N_DEV - 1),))],
        compiler_params=pltpu.CompilerParams(vmem_limit_bytes=VMEM_LIMIT),
    )(c_loc, ada_w_loc, ada_b_cols)


def _small_gather_sum(pack):
    P8 = pack.shape[1]

    def body(p_ref, gat_ref, sum_ref, ssem, rsem):
        pos = _my_pos()
        me = _lin(pos)
        gat_ref[me] = p_ref[...]
        sends = []
        for k in range(1, N_DEV):
            cp = pltpu.make_async_remote_copy(src_ref=p_ref, dst_ref=gat_ref.at[me], send_sem=ssem.at[k - 1],
                                              recv_sem=rsem.at[k - 1], device_id=_peer(pos, k), device_id_type=MESH)
            cp.start()
            sends.append(cp)
        for k in range(1, N_DEV):
            src = _lin(_peer(pos, k))
            pltpu.make_async_remote_copy(src_ref=p_ref, dst_ref=gat_ref.at[src], send_sem=ssem.at[k - 1],
                                         recv_sem=rsem.at[k - 1], device_id=pos, device_id_type=MESH).wait_recv()
        for cp in sends:
            cp.wait_send()
        acc = gat_ref[0]
        for j in range(1, N_DEV):
            acc = acc + gat_ref[j]
        sum_ref[...] = acc

    vm = pl.BlockSpec(memory_space=pltpu.VMEM)
    return pl.pallas_call(
        body, name="small_gather_sum",
        out_shape=(_S((N_DEV, 8, P8), F32), _S((8, P8), F32)),
        in_specs=[vm], out_specs=(vm, vm),
        scratch_shapes=[pltpu.SemaphoreType.DMA((N_DEV - 1,)), pltpu.SemaphoreType.DMA((N_DEV - 1,))],
        compiler_params=pltpu.CompilerParams(vmem_limit_bytes=VMEM_LIMIT),
    )(pack)


def _ag_weights(shards):
    n = len(shards)

    def body(*refs):
        ins, outs = refs[:n], refs[n:2 * n]
        ssem, rsem, lsem = refs[2 * n:]
        x, y, c = pos = _my_pos()
        me = _lin(pos)
        sib = (x, y, 1 - c)
        chips = [(1 - x, y), (x, 1 - y), (1 - x, 1 - y)]

        def copy(a, k, block, to, src=None):
            return pltpu.make_async_remote_copy(
                src_ref=outs[a].at[block] if src is None else src, dst_ref=outs[a].at[block],
                send_sem=ssem.at[a * 7 + k], recv_sem=rsem.at[a * 7 + k], device_id=to, device_id_type=MESH)

        local = [pltpu.make_async_copy(ins[a], outs[a].at[me], lsem.at[a]) for a in range(n)]
        for cp in local:
            cp.start()
        first = []
        for a in range(n):
            first.append(copy(a, 0, me, sib, src=ins[a]))
            first += [copy(a, 1 + j, me, (*chip, c), src=ins[a]) for j, chip in enumerate(chips)]
        for cp in first:
            cp.start()
        passed = []
        for a in range(n):
            for j, chip in enumerate(chips):
                blk = _lin((*chip, c))
                copy(a, 1 + j, blk, pos).wait_recv()
                cp = copy(a, 4 + j, blk, sib)
                cp.start()
                passed.append(cp)
        for a in range(n):
            copy(a, 0, _lin(sib), pos).wait_recv()
            for j, chip in enumerate(chips):
                copy(a, 4 + j, _lin((*chip, 1 - c)), pos).wait_recv()
        for cp in first + passed:
            cp.wait_send()
        for cp in local:
            cp.wait()

    hbm = pl.BlockSpec(memory_space=pl.ANY)
    return pl.pallas_call(
        body, name="ag_weights",
        out_shape=tuple(_S((N_DEV,) + s.shape, s.dtype) for s in shards),
        in_specs=[hbm] * n, out_specs=tuple([hbm] * n),
        scratch_shapes=[pltpu.SemaphoreType.DMA((7 * n,)), pltpu.SemaphoreType.DMA((7 * n,)),
                        pltpu.SemaphoreType.DMA((n,))],
    )(*shards)


def _rs_d2d(blocked, name):
    n = len(blocked)

    def body(*refs):
        ins, outs = refs[:n], refs[n:2 * n]
        ssem, rsem = refs[2 * n:]
        x, y, c = pos = _my_pos()
        sib = (x, y, 1 - c)
        cps = []
        for a in range(n):
            for j in range(N_CHIP):
                cp = pltpu.make_async_remote_copy(
                    src_ref=ins[a].at[2 * j + (1 - c)], dst_ref=outs[a].at[j], send_sem=ssem.at[a * N_CHIP + j],
                    recv_sem=rsem.at[a * N_CHIP + j], device_id=sib, device_id_type=MESH)
                cp.start()
                cps.append(cp)
        for cp in cps:
            cp.wait_recv()
        for cp in cps:
            cp.wait_send()

    hbm = pl.BlockSpec(memory_space=pl.ANY)
    return pl.pallas_call(
        body, name=name,
        out_shape=tuple(_S((N_CHIP,) + b.shape[1:], b.dtype) for b in blocked),
        in_specs=[hbm] * n, out_specs=tuple([hbm] * n),
        scratch_shapes=[pltpu.SemaphoreType.DMA((N_CHIP * n,)), pltpu.SemaphoreType.DMA((N_CHIP * n,))],
    )(*blocked)


def _rs_ici(pairs):
    n = len(pairs)

    def body(*refs):
        ins, outs = refs[:n], refs[n:2 * n]
        ssem, rsem, lsem = refs[2 * n:]
        x, y, c = pos = _my_pos()
        mychip = 2 * x + y
        local = [pltpu.make_async_copy(ins[a].at[mychip], outs[a].at[mychip], lsem.at[a]) for a in range(n)]
        for cp in local:
            cp.start()
        sends = []
        for a in range(n):
            for k in range(1, N_CHIP):
                tx, ty, _ = _peer(pos, 2 * k)
                cp = pltpu.make_async_remote_copy(
                    src_ref=ins[a].at[2 * tx + ty], dst_ref=outs[a].at[mychip], send_sem=ssem.at[a * 3 + k - 1],
                    recv_sem=rsem.at[a * 3 + k - 1], device_id=(tx, ty, c), device_id_type=MESH)
                cp.start()
                sends.append(cp)
        for a in range(n):
            for k in range(1, N_CHIP):
                tx, ty, _ = _peer(pos, 2 * k)
                pltpu.make_async_remote_copy(
                    src_ref=ins[a].at[2 * tx + ty], dst_ref=outs[a].at[2 * tx + ty], send_sem=ssem.at[a * 3 + k - 1],
                    recv_sem=rsem.at[a * 3 + k - 1], device_id=pos, device_id_type=MESH).wait_recv()
        for cp in sends:
            cp.wait_send()
        for cp in local:
            cp.wait()

    hbm = pl.BlockSpec(memory_space=pl.ANY)
    return pl.pallas_call(
        body, name="rs_ici",
        out_shape=tuple(_S(p.shape, p.dtype) for p in pairs),
        in_specs=[hbm] * n, out_specs=tuple([hbm] * n),
        scratch_shapes=[pltpu.SemaphoreType.DMA((3 * n,)), pltpu.SemaphoreType.DMA((3 * n,)),
                        pltpu.SemaphoreType.DMA((n,))],
    )(*pairs)


_HBM = pl.BlockSpec(memory_space=pltpu.HBM)
_SEM = pl.BlockSpec(memory_space=pltpu.SEMAPHORE)
_ANY = pl.BlockSpec(memory_space=pl.ANY)
_EFFECT = pltpu.SideEffectType.DATAFLOW_SIDE_EFFECTING


def _in_hbm(a):
    return pltpu.with_memory_space_constraint(a, pltpu.HBM)


def _ici_plan(kind, pos):
    x, y, c = pos
    plan = []
    for k in range(1, N_CHIP):
        tx, ty, _ = _peer(pos, 2 * k)
        if kind == "gather":
            plan.append(((tx, ty, c), None, _lin(pos), _lin((tx, ty, c))))
        else:
            plan.append(((tx, ty, c), 2 * tx + ty, 2 * x + y, 2 * tx + ty))
    return plan


def _ici_start(srcs, n_slots, after, kind, name):
    n = len(srcs)
    lands = [lax.empty((n_slots,) + (s.shape if kind == "gather" else s.shape[1:]), s.dtype) for s in srcs]

    def body(*refs):
        ins, lnd = refs[:n], refs[n:2 * n]
        ssem, rsem = refs[2 * n + 1], refs[2 * n + 2]
        token = refs[-1]
        for a in range(n):
            for k, (dev, src_slot, dst_slot, _) in enumerate(_ici_plan(kind, _my_pos())):
                pltpu.make_async_remote_copy(
                    src_ref=ins[a] if src_slot is None else ins[a].at[src_slot], dst_ref=lnd[a].at[dst_slot],
                    send_sem=ssem.at[a * 3 + k], recv_sem=rsem.at[a * 3 + k], device_id=dev, device_id_type=MESH).start()
        token[...] = jnp.zeros_like(token)

    res = pl.pallas_call(
        body, name=name,
        out_shape=(pltpu.SemaphoreType.DMA((3 * n,)), pltpu.SemaphoreType.DMA((3 * n,)),
                   *[pltpu.HBM(s.shape, s.dtype) for s in srcs], *[pltpu.HBM(l.shape, l.dtype) for l in lands],
                   _S((8, LANE), F32)),
        in_specs=[_HBM] * (2 * n) + [_ANY],
        out_specs=(_SEM, _SEM, *[_HBM] * (2 * n), pl.BlockSpec(memory_space=pltpu.VMEM)),
        input_output_aliases={a: 2 + a for a in range(2 * n)},
        compiler_params=pltpu.CompilerParams(has_side_effects=_EFFECT),
    )(*[_in_hbm(s) for s in srcs], *[_in_hbm(l) for l in lands], after)
    return res[0], res[1], list(res[2:2 + n]), list(res[2 + n:2 + 2 * n]), res[-1]


def _ici_wait(ssem, rsem, srcs, lands, after, kind, name):
    n = len(srcs)

    def body(*refs):
        ins, lnd = refs[:n], refs[n:2 * n]
        ss, rs = refs[2 * n], refs[2 * n + 1]
        for a in range(n):
            for k, (dev, src_slot, _, arr_slot) in enumerate(_ici_plan(kind, _my_pos())):
                cp = pltpu.make_async_remote_copy(
                    src_ref=ins[a] if src_slot is None else ins[a].at[src_slot], dst_ref=lnd[a].at[arr_slot],
                    send_sem=ss.at[a * 3 + k], recv_sem=rs.at[a * 3 + k], device_id=dev, device_id_type=MESH)
                cp.wait_send()
                cp.wait_recv()

    res = pl.pallas_call(
        body, name=name,
        out_shape=(*[pltpu.HBM(s.shape, s.dtype) for s in srcs], *[pltpu.HBM(l.shape, l.dtype) for l in lands]),
        in_specs=[_HBM] * (2 * n) + [_SEM, _SEM, _ANY], out_specs=tuple([_HBM] * (2 * n)),
        input_output_aliases={a: a for a in range(2 * n)},
        compiler_params=pltpu.CompilerParams(has_side_effects=_EFFECT),
    )(*srcs, *lands, ssem, rsem, after)
    return list(res[:n]), list(res[n:])


def _ag_d2d(shards, lands):
    n = len(shards)

    def body(*refs):
        ins, lnd_in, outs = refs[:n], refs[n:2 * n], refs[2 * n:3 * n]
        ssem, rsem, lsem = refs[3 * n:]
        del lnd_in
        x, y, c = pos = _my_pos()
        me = _lin(pos)
        sib = (x, y, 1 - c)
        plan = _ici_plan("gather", pos)
        local = [pltpu.make_async_copy(ins[a], outs[a].at[me], lsem.at[a]) for a in range(n)]
        for cp in local:
            cp.start()
        sends = []
        for a in range(n):
            cp = pltpu.make_async_remote_copy(src_ref=ins[a], dst_ref=outs[a].at[me], send_sem=ssem.at[a * 4],
                                              recv_sem=rsem.at[a * 4], device_id=sib, device_id_type=MESH)
            cp.start()
            sends.append(cp)
            for k, (_, _, _, arr_slot) in enumerate(plan):
                cp = pltpu.make_async_remote_copy(src_ref=outs[a].at[arr_slot], dst_ref=outs[a].at[arr_slot],
                                                  send_sem=ssem.at[a * 4 + 1 + k], recv_sem=rsem.at[a * 4 + 1 + k],
                                                  device_id=sib, device_id_type=MESH)
                cp.start()
                sends.append(cp)
        for a in range(n):
            pltpu.make_async_remote_copy(src_ref=ins[a], dst_ref=outs[a].at[_lin(sib)], send_sem=ssem.at[a * 4],
                                         recv_sem=rsem.at[a * 4], device_id=pos, device_id_type=MESH).wait_recv()
            for k, (dev, _, _, _) in enumerate(plan):
                slot = _lin((dev[0], dev[1], 1 - c))
                pltpu.make_async_remote_copy(src_ref=outs[a].at[slot], dst_ref=outs[a].at[slot],
                                             send_sem=ssem.at[a * 4 + 1 + k], recv_sem=rsem.at[a * 4 + 1 + k],
                                             device_id=pos, device_id_type=MESH).wait_recv()
        for cp in sends:
            cp.wait_send()
        for cp in local:
            cp.wait()

    return pl.pallas_call(
        body, name="ag_d2d",
        out_shape=tuple(_S(l.shape, l.dtype) for l in lands),
        in_specs=[_ANY] * (2 * n), out_specs=tuple([_ANY] * n),
        input_output_aliases={n + a: a for a in range(n)},
        scratch_shapes=[pltpu.SemaphoreType.DMA((4 * n,)), pltpu.SemaphoreType.DMA((4 * n,)),
                        pltpu.SemaphoreType.DMA((n,))],
    )(*shards, *lands)


def _row_tile(R, itemsize_rows=16, cap=256):
    t = min(R, cap)
    while R % t or t % itemsize_rows:
        t -= itemsize_rows
    return t


def _pair_sum(blocked, recv, core):
    _, R, C = blocked.shape
    tr = _row_tile(R)

    def body(ids_ref, a_ref, b_ref, o_ref):
        del ids_ref
        o_ref[...] = (a_ref[...] + b_ref[...]).astype(BF16)

    gs = pltpu.PrefetchScalarGridSpec(
        num_scalar_prefetch=1, grid=(N_CHIP, R // tr),
        in_specs=[pl.BlockSpec((1, tr, C), lambda j, r, ids: (2 * j + ids[0], r, 0)),
                  pl.BlockSpec((1, tr, C), lambda j, r, ids: (j, r, 0))],
        out_specs=pl.BlockSpec((1, tr, C), lambda j, r, ids: (j, r, 0)))
    return pl.pallas_call(body, name="pair_sum", grid_spec=gs, out_shape=_S((N_CHIP, R, C), BF16),
                          compiler_params=_params(("arbitrary", "arbitrary")))(core, blocked, recv)


def _adamw_math(w, g, m, v):
    m2 = ADAM_B1 * m + (1.0 - ADAM_B1) * g
    v2 = ADAM_B2 * v + (1.0 - ADAM_B2) * (g * g)
    m_hat = m2 / (1.0 - ADAM_B1 ** ADAM_STEP)
    v_hat = v2 / (1.0 - ADAM_B2 ** ADAM_STEP)
    delta = -ADAM_LR * (m_hat / (jnp.sqrt(v_hat) + ADAM_EPS) + ADAM_WD * w)
    return delta, m2, v2


def _sum_adamw(parts, w, m, v):
    R, C = w.shape
    tr = _row_tile(R)

    def body(p_ref, w_ref, m_ref, v_ref, g_ref, d_ref, m2_ref, v2_ref):
        g = p_ref[0].astype(F32)
        for j in range(1, N_CHIP):
            g = g + p_ref[j].astype(F32)
        g_ref[...] = g
        d_ref[...], m2_ref[...], v2_ref[...] = _adamw_math(w_ref[...], g, m_ref[...], v_ref[...])

    t = pl.BlockSpec((tr, C), lambda r: (r, 0))
    return pl.pallas_call(
        body, name="sum_adamw", grid=(R // tr,),
        in_specs=[pl.BlockSpec((N_CHIP, tr, C), lambda r: (0, r, 0)), t, t, t], out_specs=(t, t, t, t),
        out_shape=tuple(_S((R, C), F32) for _ in range(4)), compiler_params=_params(("arbitrary",)))(parts, w, m, v)


def _sum_adamw_split(pairs, land, chips, w, m, v):
    R, C = w.shape
    tr = _row_tile(R)

    def body(ids_ref, own_ref, p1_ref, p2_ref, p3_ref, w_ref, m_ref, v_ref, g_ref, d_ref, m2_ref, v2_ref):
        del ids_ref
        g = own_ref[0].astype(F32) + p1_ref[0].astype(F32) + p2_ref[0].astype(F32) + p3_ref[0].astype(F32)
        g_ref[...] = g
        d_ref[...], m2_ref[...], v2_ref[...] = _adamw_math(w_ref[...], g, m_ref[...], v_ref[...])

    t = pl.BlockSpec((tr, C), lambda r, ids: (r, 0))
    slot = lambda k: pl.BlockSpec((1, tr, C), lambda r, ids: (ids[k], r, 0))
    gs = pltpu.PrefetchScalarGridSpec(num_scalar_prefetch=1, grid=(R // tr,),
                                      in_specs=[slot(0), slot(1), slot(2), slot(3), t, t, t], out_specs=(t, t, t, t))
    return pl.pallas_call(body, name="sum_adamw_split", grid_spec=gs, out_shape=tuple(_S((R, C), F32) for _ in range(4)),
                          compiler_params=_params(("arbitrary",)))(chips, pairs, land, land, land, w, m, v)


def _adamw(w, g, m, v):
    R, C = w.shape
    tr = _row_tile(R, 8)

    def body(w_ref, g_ref, m_ref, v_ref, d_ref, m2_ref, v2_ref):
        d_ref[...], m2_ref[...], v2_ref[...] = _adamw_math(w_ref[...], g_ref[...], m_ref[...], v_ref[...])

    t = pl.BlockSpec((tr, C), lambda r: (r, 0))
    return pl.pallas_call(body, name="adamw", grid=(R // tr,), in_specs=[t, t, t, t], out_specs=(t, t, t),
                          out_shape=tuple(_S((R, C), F32) for _ in range(3)),
                          compiler_params=_params(("arbitrary",)))(w, g, m, v)


def _ada_grad_adamw(cs16, dmod16, w, m, v):
    D, C6 = w.shape
    tr = _row_tile(D, 8, 256)

    def body(cs_ref, dm_ref, w_ref, m_ref, v_ref, g_ref, d_ref, m2_ref, v2_ref):
        g = _dot_tn(cs_ref[...].astype(BF16), dm_ref[...].astype(BF16))
        g_ref[...] = g
        d_ref[...], m2_ref[...], v2_ref[...] = _adamw_math(w_ref[...], g, m_ref[...], v_ref[...])

    t = pl.BlockSpec((tr, C6), lambda r: (r, 0))
    return pl.pallas_call(
        body, name="ada_grad_adamw", grid=(D // tr,),
        in_specs=[pl.BlockSpec((16, tr), lambda r: (0, r)), _acc((16, C6)), t, t, t], out_specs=(t, t, t, t),
        out_shape=tuple(_S((D, C6), F32) for _ in range(4)), compiler_params=_params(("arbitrary",)))(cs16, dmod16, w, m, v)


def _pick(n, cands):
    for c in cands:
        if n % c == 0:
            return c
    return n


def _matmul_tn(a, b, name, square_a=False):
    L, K = a.shape
    N = b.shape[1]
    bk = _pick(K, (1024, 512, 256, 128))
    bn = _pick(N, (1024, 768, 512, 256, 128))
    tl = _pick(L, (1024, 512, 256, 128))
    n_l = L // tl

    def body(a_ref, b_ref, o_ref):
        @pl.when(pl.program_id(2) == 0)
        def _():
            o_ref[...] = jnp.zeros_like(o_ref)
        av = a_ref[...]
        if square_a:
            av = av.astype(F32)
            av = av * av
        o_ref[...] += _dot_tn(av.astype(BF16), b_ref[...].astype(BF16))

    return pl.pallas_call(
        body, name=name, grid=(K // bk, N // bn, n_l),
        in_specs=[pl.BlockSpec((tl, bk), lambda k, n, l: (l, k)), pl.BlockSpec((tl, bn), lambda k, n, l: (l, n))],
        out_specs=pl.BlockSpec((bk, bn), lambda k, n, l: (k, n)), out_shape=_S((K, N), F32),
        compiler_params=_params(("arbitrary", "arbitrary", "arbitrary")))(a, b)


def _ln_in_proj(x, g, b, sc, sh, w_pad, dims):
    L, D = x.shape
    W, CD, AW, KVW2 = dims["W"], dims["CD"], dims["AW"], dims["KVW2"]
    NP = w_pad.shape[1]
    tm = _pick(L, (256, 128))
    o_z, o_xbc, o_q, o_kv, o_dt = 0, W, W + CD, W + CD + AW, W + CD + AW + KVW2

    def body(x_ref, g_ref, b_ref, sc_ref, sh_ref, w_ref, xhat_ref, rstd_ref, u1_ref, z_ref, xbc_ref, q_ref, kv_ref, dt_ref):
        xhat, rstd = _ln_fwd(x_ref[...])
        xhat_ref[...] = xhat
        rstd_ref[...] = rstd
        h0 = xhat * g_ref[...] + b_ref[...]
        u1 = (h0 * (1.0 + sc_ref[...]) + sh_ref[...]).astype(BF16)
        u1_ref[...] = u1
        z_ref[...] = _dot(u1, w_ref[:, o_z:o_xbc])
        xbc_ref[...] = _dot(u1, w_ref[:, o_xbc:o_q])
        q_ref[...] = _dot(u1, w_ref[:, o_q:o_kv]).astype(BF16)
        kv_ref[...] = _dot(u1, w_ref[:, o_kv:o_dt]).astype(BF16)
        dt_ref[...] = _dot(u1, w_ref[:, o_dt:NP])

    v = _acc((1, D))
    return pl.pallas_call(
        body, name="ln_in_proj", grid=(L // tm,),
        in_specs=[_tile(D, tm), v, v, v, v, _res((D, NP))],
        out_specs=(_tile(D, tm), _tile(1, tm), _tile(D, tm), _tile(W, tm), _tile(CD, tm), _tile(AW, tm),
                   _tile(KVW2, tm), _tile(LANE, tm)),
        out_shape=(_S((L, D), F32), _S((L, 1), F32), _S((L, D), BF16), _S((L, W), F32), _S((L, CD), F32),
                   _S((L, AW), BF16), _S((L, KVW2), BF16), _S((L, LANE), F32)),
        compiler_params=_params(("arbitrary",)))(x, g, b, sc, sh, w_pad)


def _conv_act(cur_ref, prev_ref, cw_ref, cb_ref, ext_ref, first):
    T = cur_ref.shape[0]
    ext_ref[0:HALO, :] = jnp.where(first, 0.0, prev_ref[...])
    ext_ref[HALO:HALO + T, :] = cur_ref[...]
    pre = cb_ref[...] + cw_ref[0:1, :] * ext_ref[HALO - 3:HALO - 3 + T, :]
    for k in range(1, CONV_K):
        pre = pre + cw_ref[k:k + 1, :] * ext_ref[HALO - 3 + k:HALO - 3 + k + T, :]
    return pre * _sigmoid(pre), pre


def _tri(T, upper=False):
    r = lax.broadcasted_iota(jnp.int32, (T, T), 0)
    c = lax.broadcasted_iota(jnp.int32, (T, T), 1)
    return (r <= c) if upper else (r >= c)


def _expand_heads(dst_ref, v, n_heads):
    for h in range(n_heads):
        dst_ref[:, h * HEAD_DIM:(h + 1) * HEAD_DIM] = jnp.broadcast_to(v[:, h:h + 1], (v.shape[0], HEAD_DIM))


def _head_reduce(v):
    wdt = v.shape[1]
    ch = lax.broadcasted_iota(jnp.int32, (wdt, LANE), 0)
    lo = lax.broadcasted_iota(jnp.int32, (wdt, LANE), 1) * HEAD_DIM
    onehot = ((ch >= lo) & (ch < lo + HEAD_DIM)).astype(BF16)
    hi = v.astype(BF16)
    rest = (v - hi.astype(F32)).astype(BF16)
    return _dot(hi, onehot) + _dot(rest, onehot)


def _conv_ssd(xbc, dt_raw, z, cw, cb, dtb, alog, dsk, nw, dims):
    L, CD = xbc.shape
    W, H, G, N = dims["W"], dims["H"], SSD_GROUPS, SSD_STATE
    T = CHUNK
    R = H // G
    GW = W // G
    nc = L // T
    HP = H * HEAD_DIM

    def body(xbc_ref, prev_ref, dt_ref, z_ref, cw_ref, cb_ref, dtb_ref, alog_ref, dsk_ref, nw_ref,
             y_ref, yn_ref, sp_ref, ext_ref, s_ref, ybuf_ref, dtx_ref, acx_ref, xb_ref):
        i = pl.program_id(0)

        @pl.when(i == 0)
        def _():
            s_ref[...] = jnp.zeros_like(s_ref)

        act, _ = _conv_act(xbc_ref, prev_ref, cw_ref, cb_ref, ext_ref, i == 0)
        xs = act[:, :W]
        dt = _softplus(dt_ref[...] + dtb_ref[...])
        a = dt * (-jnp.exp(alog_ref[...]))
        low = _tri(T)
        acum = _dot_hi(low.astype(F32), a)
        acum_t = _dot_hi(a.T, _tri(T, upper=True).astype(F32))
        _expand_heads(dtx_ref, dt, H)
        _expand_heads(acx_ref, acum, H)
        acx = acx_ref[...]
        lastx = acx[T - 1:T, :]
        xd = xs * dtx_ref[...]
        xb_ref[...] = xd.astype(BF16)
        xdb = (xd * jnp.exp(lastx - acx)).astype(BF16)
        ex = jnp.exp(acx)
        elx = jnp.exp(lastx)
        for g in range(G):
            gs = slice(g * GW, (g + 1) * GW)
            bgb = act[:, W + g * N:W + (g + 1) * N].astype(BF16)
            cgb = act[:, W + G * N + g * N:W + G * N + (g + 1) * N].astype(BF16)
            stg = s_ref[:, gs]
            sp_ref[0, :, gs] = stg
            yoff = ex[:, gs] * _dot(cgb, stg.astype(BF16))
            s_ref[:, gs] = stg * elx[:, gs] + _dot_tn(bgb, xdb[:, gs])
            cb_g = _dot_nt(cgb, bgb)
            for r in range(R):
                h = g * R + r
                hs = slice(h * HEAD_DIM, (h + 1) * HEAD_DIM)
                lm = jnp.where(low, jnp.exp(acum[:, h:h + 1] - acum_t[h:h + 1, :]), 0.0)
                ybuf_ref[:, hs] = _dot((cb_g * lm).astype(BF16), xb_ref[:, hs]) + yoff[:, r * HEAD_DIM:(r + 1) * HEAD_DIM]
        y = ybuf_ref[...] + dsk_ref[...] * xs
        y_ref[...] = y
        zz = z_ref[...]
        hh = y * (zz * _sigmoid(zz))
        for g in range(G):
            gs = slice(g * GW, (g + 1) * GW)
            hg = hh[:, gs]
            yn_ref[:, gs] = (hg * lax.rsqrt(_mean(hg * hg) + RMS_EPS) * nw_ref[:, gs]).astype(BF16)

    return pl.pallas_call(
        body, name="conv_ssd", grid=(nc,),
        in_specs=[_tile(CD, T), pl.BlockSpec((HALO, CD), lambda i: (jnp.maximum(i * (T // HALO) - 1, 0), 0)),
                  _tile(LANE, T), _tile(W, T), _acc((CONV_K, CD)), _acc((1, CD)), _acc((1, LANE)), _acc((1, LANE)),
                  _acc((1, W)), _acc((1, W))],
        out_specs=(_tile(W, T), _tile(W, T), pl.BlockSpec((1, N, HP), lambda i: (i, 0, 0))),
        out_shape=(_S((L, W), F32), _S((L, W), BF16), _S((nc, N, HP), F32)),
        scratch_shapes=[pltpu.VMEM((T + HALO, CD), F32), pltpu.VMEM((N, HP), F32), pltpu.VMEM((T, W), F32),
                        pltpu.VMEM((T, W), F32), pltpu.VMEM((T, W), F32), pltpu.VMEM((T, W), BF16)],
        compiler_params=_params(("arbitrary",)))(xbc, xbc, dt_raw, z, cw, cb, dtb, alog, dsk, nw)


def _attn_mask(T, i):
    r = lax.broadcasted_iota(jnp.int32, (T, 2 * T), 0)
    c = lax.broadcasted_iota(jnp.int32, (T, 2 * T), 1)
    dist = r + T - c
    valid = (dist >= 0) & (dist < CHUNK) & ((c >= T) | (i > 0))
    return dist.astype(F32), valid


def _attn_probs(s_raw, dist, valid, slope, sink, axis):
    s = s_raw * (HEAD_DIM ** -0.5) - slope * dist
    s = jnp.where(valid, s, NEG)
    m = jnp.maximum(jnp.max(s, axis=axis, keepdims=True), sink)
    p = jnp.exp(s - m)
    e_sink = jnp.exp(sink - m)
    inv = 1.0 / (jnp.sum(p, axis=axis, keepdims=True) + e_sink)
    return p * inv, e_sink * inv


def _kv_heads(kvc_ref, kvp_ref, g, n_kv):
    ks = slice(g * HEAD_DIM, (g + 1) * HEAD_DIM)
    vs = slice((n_kv + g) * HEAD_DIM, (n_kv + g + 1) * HEAD_DIM)
    kk = jnp.concatenate([kvp_ref[:, ks], kvc_ref[:, ks]], axis=0)
    vv = jnp.concatenate([kvp_ref[:, vs], kvc_ref[:, vs]], axis=0)
    return kk, vv


def _swa_fwd(q, kv, sinks, dims):
    L, AW = q.shape
    KV, KVW2 = dims["KV"], dims["KVW2"]
    T = CHUNK
    nb = L // T
    slopes = _alibi_slopes(dims["AH"])

    def body(q_ref, kvc_ref, kvp_ref, sink_ref, o_ref, qg_ref, p_ref):
        i = pl.program_id(0)
        dist, valid = _attn_mask(T, i)
        for g in range(KV):
            kk, vv = _kv_heads(kvc_ref, kvp_ref, g, KV)
            for r in range(GQA):
                h = g * GQA + r
                qg_ref[r * T:(r + 1) * T, :] = q_ref[:, h * HEAD_DIM:(h + 1) * HEAD_DIM]
            s_all = _dot_nt(qg_ref[...], kk)
            for r in range(GQA):
                h = g * GQA + r
                p, _ = _attn_probs(s_all[r * T:(r + 1) * T, :], dist, valid, slopes[h], sink_ref[h], -1)
                p_ref[r * T:(r + 1) * T, :] = p.astype(BF16)
            o_all = _dot(p_ref[...], vv)
            for r in range(GQA):
                h = g * GQA + r
                o_ref[:, h * HEAD_DIM:(h + 1) * HEAD_DIM] = o_all[r * T:(r + 1) * T, :].astype(BF16)

    return pl.pallas_call(
        body, name="swa_fwd", grid=(nb,),
        in_specs=[_tile(AW, T), _tile(KVW2, T), pl.BlockSpec((T, KVW2), lambda i: (jnp.maximum(i - 1, 0), 0)),
                  pl.BlockSpec(memory_space=pltpu.SMEM)],
        out_specs=_tile(AW, T), out_shape=_S((L, AW), BF16),
        scratch_shapes=[pltpu.VMEM((GQA * T, HEAD_DIM), BF16), pltpu.VMEM((GQA * T, 2 * T), BF16)],
        compiler_params=_params(("arbitrary",)))(q, kv, kv, sinks)


def _out_proj_ln1(yn, o, w_out, xhat0, vecs, alpha):
    L, W = yn.shape
    D = xhat0.shape[1]
    MIX = w_out.shape[0]
    tm = _pick(L, (256, 128))

    def body(yn_ref, o_ref, w_ref, xh_ref, v_ref, mix_ref, xhat1_ref, rstd1_ref, u2_ref):
        mix = _dot(yn_ref[...], w_ref[0:W, :]) + _dot(o_ref[...], w_ref[W:MIX, :])
        mix_ref[...] = mix
        h0 = xh_ref[...] * v_ref[0:1, :] + v_ref[1:2, :]
        xhat1, rstd1 = _ln_fwd(alpha * h0 + (1.0 + v_ref[2:3, :]) * mix)
        xhat1_ref[...] = xhat1
        rstd1_ref[...] = rstd1
        h1 = xhat1 * v_ref[3:4, :] + v_ref[4:5, :]
        u2_ref[...] = (h1 * (1.0 + v_ref[5:6, :]) + v_ref[6:7, :]).astype(BF16)

    return pl.pallas_call(
        body, name="out_proj_ln1", grid=(L // tm,),
        in_specs=[_tile(W, tm), _tile(MIX - W, tm), _res((MIX, D)), _tile(D, tm), _acc((8, D))],
        out_specs=(_tile(D, tm), _tile(D, tm), _tile(1, tm), _tile(D, tm)),
        out_shape=(_S((L, D), F32), _S((L, D), F32), _S((L, 1), F32), _S((L, D), BF16)),
        compiler_params=_params(("arbitrary",)))(yn, o, w_out, xhat0, vecs)


def _mlp_loss(u2, w1, w2, xhat1, tgt, vecs, b1, alpha):
    L, D = xhat1.shape
    FF = w1.shape[1]
    tm = _pick(L, (256, 128))
    fc = _pick(FF, (512, 256, 128))

    def body(u2_ref, w1_ref, w2_ref, xh_ref, t_ref, v_ref, b1_ref, rr_ref, dr2_ref, acc_ref, loss_ref):
        @pl.when(pl.program_id(0) == 0)
        def _():
            acc_ref[...] = jnp.zeros_like(acc_ref)
            loss_ref[...] = jnp.zeros_like(loss_ref)

        u2 = u2_ref[...]
        f = jnp.zeros((tm, D), F32) + v_ref[5:6, :]
        for j in range(FF // fc):
            cs = slice(j * fc, (j + 1) * fc)
            rr = jnp.maximum(_dot(u2, w1_ref[:, cs]) + b1_ref[:, cs], 0.0)
            rr_ref[:, cs] = rr.astype(BF16)
            f = f + _dot((rr * rr).astype(BF16), w2_ref[cs, :])
        xhat1 = xh_ref[...]
        h1 = xhat1 * v_ref[0:1, :] + v_ref[1:2, :]
        xhat2, rstd2 = _ln_fwd(alpha * h1 + (1.0 + v_ref[2:3, :]) * f)
        e = xhat2 * v_ref[3:4, :] + v_ref[4:5, :] - t_ref[...]
        loss_ref[...] += 0.5 * jnp.sum(_mean(e * e))
        dy = e * (1.0 / D)
        dr2 = _ln_bwd(dy * v_ref[3:4, :], xhat2, rstd2)
        dr2_ref[...] = dr2
        acc_ref[0:1, :] += _colsum(dy * xhat2)
        acc_ref[1:2, :] += _colsum(dy)
        acc_ref[2:3, :] += _colsum(dr2 * f)

    return pl.pallas_call(
        body, name="mlp_loss", grid=(L // tm,),
        in_specs=[_tile(D, tm), _res((D, FF)), _res((FF, D)), _tile(D, tm), _tile(D, tm), _acc((8, D)), _acc((1, FF))],
        out_specs=(_tile(FF, tm), _tile(D, tm), _acc((8, D)), _acc((1, LANE))),
        out_shape=(_S((L, FF), BF16), _S((L, D), F32), _S((8, D), F32), _S((1, LANE), F32)),
        compiler_params=_params(("arbitrary",)))(u2, w1, w2, xhat1, tgt, vecs, b1)


def _mlp_bwd_a(dr2, rr, w2, g2):
    L, D = dr2.shape
    FF = w2.shape[0]
    tm = _pick(L, (256, 128))
    fc = _pick(FF, (512, 256, 128))

    def body(dr2_ref, rr_ref, w2_ref, g2_ref, df_ref, da_ref, gb2_ref, gb1_ref):
        @pl.when(pl.program_id(0) == 0)
        def _():
            gb2_ref[...] = jnp.zeros_like(gb2_ref)
            gb1_ref[...] = jnp.zeros_like(gb1_ref)

        df = (1.0 + g2_ref[...]) * dr2_ref[...]
        gb2_ref[...] += _colsum(df)
        dfb = df.astype(BF16)
        df_ref[...] = dfb
        for j in range(FF // fc):
            cs = slice(j * fc, (j + 1) * fc)
            da = _dot_nt(dfb, w2_ref[cs, :]) * (2.0 * rr_ref[:, cs].astype(F32))
            gb1_ref[:, cs] += _colsum(da)
            da_ref[:, cs] = da.astype(BF16)

    return pl.pallas_call(
        body, name="mlp_bwd_a", grid=(L // tm,),
        in_specs=[_tile(D, tm), _tile(FF, tm), _res((FF, D)), _acc((1, D))],
        out_specs=(_tile(D, tm), _tile(FF, tm), _acc((1, D)), _acc((1, FF))),
        out_shape=(_S((L, D), BF16), _S((L, FF), BF16), _S((1, D), F32), _S((1, FF), F32)),
        compiler_params=_params(("arbitrary",)))(dr2, rr, w2, g2)


def _mlp_bwd_b(da, w1, dr2, xhat1, rstd1, mix, w_out, vecs, alpha, W):
    L, FF = da.shape
    D = dr2.shape[1]
    MIX = w_out.shape[0]
    tm = _pick(L, (256, 128))

    def body(da_ref, w1_ref, dr2_ref, xh_ref, rs_ref, mix_ref, wo_ref, v_ref, dmix_ref, dh0_ref, dyn_ref, do_ref, acc_ref):
        @pl.when(pl.program_id(0) == 0)
        def _():
            acc_ref[...] = jnp.zeros_like(acc_ref)

        du2 = _dot_nt(da_ref[...], w1_ref[...])
        xhat1 = xh_ref[...]
        h1 = xhat1 * v_ref[0:1, :] + v_ref[1:2, :]
        acc_ref[0:1, :] += _colsum(du2 * h1)
        acc_ref[1:2, :] += _colsum(du2)
        dh1 = alpha * dr2_ref[...] + du2 * (1.0 + v_ref[2:3, :])
        acc_ref[2:3, :] += _colsum(dh1 * xhat1)
        acc_ref[3:4, :] += _colsum(dh1)
        dr1 = _ln_bwd(dh1 * v_ref[0:1, :], xhat1, rs_ref[...])
        acc_ref[4:5, :] += _colsum(dr1 * mix_ref[...])
        dh0_ref[...] = alpha * dr1
        dmix = ((1.0 + v_ref[3:4, :]) * dr1).astype(BF16)
        dmix_ref[...] = dmix
        dyn_ref[...] = _dot_nt(dmix, wo_ref[0:W, :])
        do_ref[...] = _dot_nt(dmix, wo_ref[W:MIX, :]).astype(BF16)

    return pl.pallas_call(
        body, name="mlp_bwd_b", grid=(L // tm,),
        in_specs=[_tile(FF, tm), _res((D, FF)), _tile(D, tm), _tile(D, tm), _tile(1, tm), _tile(D, tm), _res((MIX, D)),
                  _acc((8, D))],
        out_specs=(_tile(D, tm), _tile(D, tm), _tile(W, tm), _tile(MIX - W, tm), _acc((8, D))),
        out_shape=(_S((L, D), BF16), _S((L, D), F32), _S((L, W), F32), _S((L, MIX - W), BF16), _S((8, D), F32)),
        compiler_params=_params(("arbitrary",)))(da, w1, dr2, xhat1, rstd1, mix, w_out, vecs)


def _swa_bwd(q, kv, do, sinks, dims):
    L, AW = q.shape
    KV, KVW2 = dims["KV"], dims["KVW2"]
    T = CHUNK
    nb = L // T
    slopes = _alibi_slopes(dims["AH"])
    scale = HEAD_DIM ** -0.5

    def body(q_ref, kvc_ref, kvp_ref, do_ref, sink_ref, dq_ref, dkv_ref, dsink_ref, carry_ref,
             qg_ref, dog_ref, pt_ref, dst_ref):
        i = pl.program_id(0)

        @pl.when(i == 0)
        def _():
            carry_ref[...] = jnp.zeros_like(carry_ref)
            dsink_ref[...] = jnp.zeros_like(dsink_ref)

        @pl.when(i < nb)
        def _():
            c = lax.broadcasted_iota(jnp.int32, (2 * T, T), 0)
            r_ = lax.broadcasted_iota(jnp.int32, (2 * T, T), 1)
            dist_i = r_ + T - c
            valid = (dist_i >= 0) & (dist_i < CHUNK) & ((c >= T) | (i > 0))
            dist = dist_i.astype(F32)
            lane = lax.broadcasted_iota(jnp.int32, (1, LANE), 1)
            dsink = jnp.zeros((1, LANE), F32)
            dks, dvs = [], []
            for g in range(KV):
                kk, vv = _kv_heads(kvc_ref, kvp_ref, g, KV)
                for r in range(GQA):
                    hs = slice((g * GQA + r) * HEAD_DIM, (g * GQA + r + 1) * HEAD_DIM)
                    qg_ref[r * T:(r + 1) * T, :] = q_ref[:, hs]
                    dog_ref[r * T:(r + 1) * T, :] = do_ref[:, hs]
                st_all = _dot_nt(kk, qg_ref[...])
                dpt_all = _dot_nt(vv, dog_ref[...])
                for r in range(GQA):
                    h = g * GQA + r
                    cs = slice(r * T, (r + 1) * T)
                    p, p_sink = _attn_probs(st_all[:, cs], dist, valid, slopes[h], sink_ref[h], 0)
                    dp = dpt_all[:, cs]
                    delta = jnp.sum(p * dp, axis=0, keepdims=True)
                    pt_ref[:, cs] = p.astype(BF16)
                    dst_ref[:, cs] = (p * (dp - delta)).astype(BF16)
                    dsink = dsink + jnp.where(lane == h, -jnp.sum(p_sink * delta), 0.0)
                dst = dst_ref[...]
                dks.append(_dot(dst, qg_ref[...]) * scale)
                dvs.append(_dot(pt_ref[...], dog_ref[...]))
                dq_all = _dot_tn(dst, kk) * scale
                for r in range(GQA):
                    hs = slice((g * GQA + r) * HEAD_DIM, (g * GQA + r + 1) * HEAD_DIM)
                    dq_ref[:, hs] = dq_all[r * T:(r + 1) * T, :].astype(BF16)
            dkv = jnp.concatenate(dks + dvs, axis=1)
            dsink_ref[...] += dsink
            dkv_ref[...] = carry_ref[...] + dkv[0:T, :]
            carry_ref[...] = dkv[T:2 * T, :]

        @pl.when(i == nb)
        def _():
            dkv_ref[...] = carry_ref[...]

    last = nb - 1
    return pl.pallas_call(
        body, name="swa_bwd", grid=(nb + 1,),
        in_specs=[pl.BlockSpec((T, AW), lambda i: (jnp.minimum(i, last), 0)),
                  pl.BlockSpec((T, KVW2), lambda i: (jnp.minimum(i, last), 0)),
                  pl.BlockSpec((T, KVW2), lambda i: (jnp.clip(i - 1, 0, last), 0)),
                  pl.BlockSpec((T, AW), lambda i: (jnp.minimum(i, last), 0)),
                  pl.BlockSpec(memory_space=pltpu.SMEM)],
        out_specs=(pl.BlockSpec((T, AW), lambda i: (jnp.minimum(i, last), 0)),
                   pl.BlockSpec((T, KVW2), lambda i: (jnp.maximum(i - 1, 0), 0)), _acc((1, LANE))),
        out_shape=(_S((L, AW), BF16), _S((L, KVW2), F32), _S((1, LANE), F32)),
        scratch_shapes=[pltpu.VMEM((T, KVW2), F32), pltpu.VMEM((GQA * T, HEAD_DIM), BF16),
                        pltpu.VMEM((GQA * T, HEAD_DIM), BF16), pltpu.VMEM((2 * T, GQA * T), BF16),
                        pltpu.VMEM((2 * T, GQA * T), BF16)],
        compiler_params=_params(("arbitrary",)))(q, kv, kv, do, sinks)


def _ssd_bwd(dyn, y, z, xbc, dt_raw, sprev, cw, cb, dtb, alog, dsk, nw, dims):
    L, CD = xbc.shape
    W, H, G, N = dims["W"], dims["H"], SSD_GROUPS, SSD_STATE
    T = CHUNK
    R = H // G
    GW = W // G
    nc = L // T
    HP = H * HEAD_DIM

    def body(dyn_ref, y_ref, z_ref, xbc_ref, prev_ref, dt_ref, sp_ref, cw_ref, cb_ref, dtb_ref, alog_ref, dsk_ref, nw_ref,
             dz_ref, dpre_ref, ddt_ref, acc_ref, hacc_ref, ext_ref, ds_ref, dtx_ref, acx_ref, xb_ref, dyb_ref, r12_ref,
             dx_ref, rows_ref):
        i = pl.program_id(0)

        @pl.when(i == 0)
        def _():
            ds_ref[...] = jnp.zeros_like(ds_ref)
            acc_ref[...] = jnp.zeros_like(acc_ref)
            hacc_ref[...] = jnp.zeros_like(hacc_ref)

        act, pre = _conv_act(xbc_ref, prev_ref, cw_ref, cb_ref, ext_ref, i == nc - 1)
        xs = act[:, :W]
        dt_in = dt_ref[...] + dtb_ref[...]
        dt = _softplus(dt_in)
        a_neg = -jnp.exp(alog_ref[...])
        a = dt * a_neg
        low = _tri(T)
        upf = _tri(T, upper=True).astype(F32)
        acum = _dot_hi(low.astype(F32), a)
        acum_t = _dot_hi(a.T, upf)

        y = y_ref[...]
        zz = z_ref[...]
        sg = _sigmoid(zz)
        sz = zz * sg
        hh = y * sz
        dyn_v = dyn_ref[...]
        parts = []
        for g in range(G):
            gs = slice(g * GW, (g + 1) * GW)
            hg = hh[:, gs]
            hhat = hg * lax.rsqrt(_mean(hg * hg) + RMS_EPS)
            rg = lax.rsqrt(_mean(hg * hg) + RMS_EPS)
            acc_ref[0:1, gs] += _colsum(dyn_v[:, gs] * hhat)
            dhhat = dyn_v[:, gs] * nw_ref[:, gs]
            parts.append(rg * (dhhat - hhat * _mean(dhhat * hhat)))
        dhh = jnp.concatenate(parts, axis=1)
        dy = dhh * sz
        dz_ref[...] = (dhh * y * (sg * (1.0 + zz * (1.0 - sg)))).astype(BF16)
        acc_ref[1:2, :] += _colsum(dy * xs)
        dyb_ref[...] = dy.astype(BF16)

        _expand_heads(dtx_ref, dt, H)
        _expand_heads(acx_ref, acum, H)
        dtx = dtx_ref[...]
        acx = acx_ref[...]
        lastx = acx[T - 1:T, :]
        ex = jnp.exp(acx)
        decx = jnp.exp(lastx - acx)
        elx = jnp.exp(lastx)
        xd = xs * dtx
        xb_ref[...] = xd.astype(BF16)
        xdecb = (xd * decx).astype(BF16)
        dgb = (ex * dy).astype(BF16)
        rows_ref[...] = jnp.zeros_like(rows_ref)

        lane = lax.broadcasted_iota(jnp.int32, (T, LANE), 1)
        sub = lax.broadcasted_iota(jnp.int32, (T, LANE), 0)
        subr = lax.broadcasted_iota(jnp.int32, (LANE, T), 0)
        da_col = jnp.zeros((T, LANE), F32)
        da_row = jnp.zeros((LANE, T), F32)
        dbs, dcs = [], []
        for g in range(G):
            gs = slice(g * GW, (g + 1) * GW)
            bgb = act[:, W + g * N:W + (g + 1) * N].astype(BF16)
            cgb = act[:, W + G * N + g * N:W + G * N + (g + 1) * N].astype(BF16)
            stg = sp_ref[0, :, gs]
            stb = stg.astype(BF16)
            dsn = ds_ref[:, gs]
            dsnb = dsn.astype(BF16)
            gm = _dot(cgb, stb)
            dc = _dot_nt(dgb[:, gs], stb)
            dsp = _dot_tn(cgb, dgb[:, gs])
            dxs_ = decx[:, gs] * _dot(bgb, dsnb)
            db = _dot_nt(xdecb[:, gs], dsnb)
            xdg = xd[:, gs]
            r12_ref[:, gs] = dy[:, gs] * ex[:, gs] * gm - xdg * dxs_
            rows_ref[0:1, gs] = _colsum(dsn * stg) * elx[:, gs]
            rows_ref[1:2, gs] = _colsum(xdg * dxs_)
            ds_ref[:, gs] = dsp + dsn * elx[:, gs]
            cb_g = _dot_nt(cgb, bgb)
            dcb = jnp.zeros((T, T), F32)
            for r in range(R):
                h = g * R + r
                hs = slice(h * HEAD_DIM, (h + 1) * HEAD_DIM)
                lm = jnp.where(low, jnp.exp(acum[:, h:h + 1] - acum_t[h:h + 1, :]), 0.0)
                mm = cb_g * lm
                dyb = dyb_ref[:, hs]
                dm = _dot_nt(dyb, xb_ref[:, hs])
                dx_ref[:, hs] = dxs_[:, r * HEAD_DIM:(r + 1) * HEAD_DIM] + _dot_tn(mm.astype(BF16), dyb)
                dcb = dcb + dm * lm
                qm = dm * mm
                da_col = jnp.where(lane == h, jnp.sum(qm, axis=1, keepdims=True), da_col)
                da_row = jnp.where(subr == h, jnp.sum(qm, axis=0, keepdims=True), da_row)
            dcbb = dcb.astype(BF16)
            dcs.append(dc + _dot(dcbb, bgb))
            dbs.append(db + _dot_tn(dcbb, cgb))
        dx = dx_ref[...]
        rows = _head_reduce(rows_ref[...])
        dlast = rows[0:1, :] + rows[1:2, :]
        da_col = da_col + _head_reduce(r12_ref[...]) + jnp.where(sub == T - 1, dlast, 0.0)
        dacum = da_col - da_row.T
        da = _dot_hi(upf, dacum)
        ddt = _head_reduce(dx * xs) + da * a_neg
        hacc_ref[1:2, :] += _colsum(da * dt) * a_neg
        ddt_raw = ddt * _sigmoid(dt_in)
        hacc_ref[0:1, :] += _colsum(ddt_raw)
        ddt_ref[...] = ddt_raw
        dact = jnp.concatenate([dsk_ref[...] * dy + dx * dtx] + dbs + dcs, axis=1)
        spre = _sigmoid(pre)
        dpre_ref[...] = dact * (spre * (1.0 + pre * (1.0 - spre)))

        @pl.when(i == nc - 1)
        def _():
            ch = lax.broadcasted_iota(jnp.int32, (W, LANE), 0)
            lo = lax.broadcasted_iota(jnp.int32, (W, LANE), 1) * HEAD_DIM
            hacc_ref[2:3, :] = _dot_hi(acc_ref[1:2, :], ((ch >= lo) & (ch < lo + HEAD_DIM)).astype(F32))

    rev = lambda i: (nc - 1 - i, 0)
    return pl.pallas_call(
        body, name="ssd_bwd", grid=(nc,),
        in_specs=[pl.BlockSpec((T, W), rev), pl.BlockSpec((T, W), rev), pl.BlockSpec((T, W), rev), pl.BlockSpec((T, CD), rev),
                  pl.BlockSpec((HALO, CD), lambda i: (jnp.maximum((nc - 1 - i) * (T // HALO) - 1, 0), 0)),
                  pl.BlockSpec((T, LANE), rev), pl.BlockSpec((1, N, HP), lambda i: (nc - 1 - i, 0, 0)),
                  _acc((CONV_K, CD)), _acc((1, CD)), _acc((1, LANE)), _acc((1, LANE)), _acc((1, W)), _acc((1, W))],
        out_specs=(pl.BlockSpec((T, W), rev), pl.BlockSpec((T, CD), rev), pl.BlockSpec((T, LANE), rev), _acc((8, W)),
                   _acc((8, LANE))),
        out_shape=(_S((L, W), BF16), _S((L, CD), F32), _S((L, LANE), F32), _S((8, W), F32), _S((8, LANE), F32)),
        scratch_shapes=[pltpu.VMEM((T + HALO, CD), F32), pltpu.VMEM((N, HP), F32), pltpu.VMEM((T, W), F32),
                        pltpu.VMEM((T, W), F32), pltpu.VMEM((T, W), BF16), pltpu.VMEM((T, W), BF16), pltpu.VMEM((T, W), F32),
                        pltpu.VMEM((T, W), F32), pltpu.VMEM((8, W), F32)],
        compiler_params=_params(("arbitrary",)))(dyn, y, z, xbc, xbc, dt_raw, sprev, cw, cb, dtb, alog, dsk, nw)


def _conv_bwd(dpre, xbc, cw):
    L, CD = xbc.shape
    tm = _pick(L, (256, 128))
    nt = L // tm
    hb = tm // HALO

    def body(dp_ref, dn_ref, u_ref, up_ref, cw_ref, du_ref, acc_ref, extu_ref, extd_ref):
        i = pl.program_id(0)

        @pl.when(i == 0)
        def _():
            acc_ref[...] = jnp.zeros_like(acc_ref)

        dp = dp_ref[...]
        extu_ref[0:HALO, :] = jnp.where(i == 0, 0.0, up_ref[...])
        extu_ref[HALO:HALO + tm, :] = u_ref[...]
        extd_ref[0:tm, :] = dp
        extd_ref[tm:tm + HALO, :] = jnp.where(i == nt - 1, 0.0, dn_ref[...])
        du = cw_ref[CONV_K - 1:CONV_K, :] * dp
        acc_ref[CONV_K - 1:CONV_K, :] += _colsum(dp * u_ref[...])
        for k in range(CONV_K - 1):
            s = CONV_K - 1 - k
            du = du + cw_ref[k:k + 1, :] * extd_ref[s:s + tm, :]
            acc_ref[k:k + 1, :] += _colsum(dp * extu_ref[HALO - s:HALO - s + tm, :])
        acc_ref[CONV_K:CONV_K + 1, :] += _colsum(dp)
        du_ref[...] = du.astype(BF16)

    return pl.pallas_call(
        body, name="conv_bwd", grid=(nt,),
        in_specs=[_tile(CD, tm), pl.BlockSpec((HALO, CD), lambda i: (jnp.minimum((i + 1) * hb, nt * hb - 1), 0)),
                  _tile(CD, tm), pl.BlockSpec((HALO, CD), lambda i: (jnp.maximum(i * hb - 1, 0), 0)), _acc((CONV_K, CD))],
        out_specs=(_tile(CD, tm), _acc((8, CD))),
        out_shape=(_S((L, CD), BF16), _S((8, CD), F32)),
        scratch_shapes=[pltpu.VMEM((tm + HALO, CD), F32), pltpu.VMEM((tm + HALO, CD), F32)],
        compiler_params=_params(("arbitrary",)))(dpre, dpre, xbc, xbc, cw)


def _in_proj_bwd(dz, dxbc, dq, dkv, ddt, w_pad, xhat0, rstd0, dh0p, vecs, dims):
    L, D = xhat0.shape
    W, CD, AW, KVW2 = dims["W"], dims["CD"], dims["AW"], dims["KVW2"]
    NP = w_pad.shape[1]
    tm = _pick(L, (256, 128))
    o_xbc, o_q, o_kv, o_dt = W, W + CD, W + CD + AW, W + CD + AW + KVW2

    def body(dz_ref, dxbc_ref, dq_ref, dkv_ref, ddt_ref, w_ref, xh_ref, rs_ref, dh0_ref, v_ref, gx_ref, acc_ref):
        @pl.when(pl.program_id(0) == 0)
        def _():
            acc_ref[...] = jnp.zeros_like(acc_ref)

        du1 = _dot_nt(dz_ref[...], w_ref[:, 0:o_xbc])
        du1 = du1 + _dot_nt(dxbc_ref[...], w_ref[:, o_xbc:o_q])
        du1 = du1 + _dot_nt(dq_ref[...], w_ref[:, o_q:o_kv])
        du1 = du1 + _dot_nt(dkv_ref[...].astype(BF16), w_ref[:, o_kv:o_dt])
        du1 = du1 + _dot_nt(ddt_ref[...].astype(BF16), w_ref[:, o_dt:NP])
        xhat0 = xh_ref[...]
        h0 = xhat0 * v_ref[0:1, :] + v_ref[1:2, :]
        acc_ref[0:1, :] += _colsum(du1 * h0)
        acc_ref[1:2, :] += _colsum(du1)
        dh0 = dh0_ref[...] + du1 * (1.0 + v_ref[2:3, :])
        acc_ref[2:3, :] += _colsum(dh0 * xhat0)
        acc_ref[3:4, :] += _colsum(dh0)
        gx_ref[...] = _ln_bwd(dh0 * v_ref[0:1, :], xhat0, rs_ref[...])

    return pl.pallas_call(
        body, name="in_proj_bwd", grid=(L // tm,),
        in_specs=[_tile(W, tm), _tile(CD, tm), _tile(AW, tm), _tile(KVW2, tm), _tile(LANE, tm), _res((D, NP)),
                  _tile(D, tm), _tile(1, tm), _tile(D, tm), _acc((8, D))],
        out_specs=(_tile(D, tm), _acc((8, D))),
        out_shape=(_S((L, D), F32), _S((8, D), F32)),
        compiler_params=_params(("arbitrary",)))(dz, dxbc, dq, dkv, ddt, w_pad, xhat0, rstd0, dh0p, vecs)


_WEIGHTS = ['ln_in_g', 'ln_in_b', 'ada_w', 'ada_b', 'w_in', 'conv_w', 'conv_b', 'dt_bias', 'a_log', 'd_skip', 'ssd_norm_w',
            'attn_sinks', 'w_out', 'ln1_g', 'ln1_b', 'w_ff1', 'b_ff1', 'w_ff2', 'b_ff2', 'ln2_g', 'ln2_b']
_BIG = ('w_in', 'w_out', 'w_ff1', 'w_ff2')
_SMALL = ('ada_b', 'ln_in_g', 'ln_in_b', 'conv_b', 'dt_bias', 'a_log', 'd_skip', 'ssd_norm_w', 'attn_sinks', 'ln1_g', 'ln1_b',
          'b_ff1', 'b_ff2', 'ln2_g', 'ln2_b')


def _pad_lanes(v, n=None):
    v = v.reshape(1, -1)
    n = n or -(-v.shape[1] // LANE) * LANE
    return jnp.pad(v, ((0, 0), (0, n - v.shape[1])))


def _vec8(rows, D):
    rows = [r.reshape(1, D) for r in rows]
    return jnp.concatenate(rows + [jnp.zeros((8 - len(rows), D), F32)], axis=0)


def _pack(segs):
    flat, offs, sizes, o = [], [], [], 0
    for s in segs:
        p = _pad_lanes(s)
        flat.append(p)
        offs.append(o)
        sizes.append(s.size)
        o += p.shape[1]
    total = -(-o // (8 * LANE)) * (8 * LANE)
    if total > o:
        flat.append(jnp.zeros((1, total - o), F32))
    return jnp.concatenate(flat, axis=1).reshape(8, total // 8), offs, sizes


def kernel(x, c, ln_in_g, ln_in_b, ada_w, ada_b, w_in, conv_w, conv_b, dt_bias, a_log, d_skip, ssd_norm_w, attn_sinks, w_out, ln1_g, ln1_b, w_ff1, b_ff1, w_ff2, b_ff2, ln2_g, ln2_b, loss_target, m_ln_in_g, m_ln_in_b, m_ada_w, m_ada_b, m_w_in, m_conv_w, m_conv_b, m_dt_bias, m_a_log, m_d_skip, m_ssd_norm_w, m_attn_sinks, m_w_out, m_ln1_g, m_ln1_b, m_w_ff1, m_b_ff1, m_w_ff2, m_b_ff2, m_ln2_g, m_ln2_b, v_ln_in_g, v_ln_in_b, v_ada_w, v_ada_b, v_w_in, v_conv_w, v_conv_b, v_dt_bias, v_a_log, v_d_skip, v_ssd_norm_w, v_attn_sinks, v_w_out, v_ln1_g, v_ln1_b, v_w_ff1, v_b_ff1, v_w_ff2, v_b_ff2, v_ln2_g, v_ln2_b):
    wts = dict(ln_in_g=ln_in_g, ln_in_b=ln_in_b, ada_w=ada_w, ada_b=ada_b, w_in=w_in, conv_w=conv_w, conv_b=conv_b,
               dt_bias=dt_bias, a_log=a_log, d_skip=d_skip, ssd_norm_w=ssd_norm_w, attn_sinks=attn_sinks, w_out=w_out,
               ln1_g=ln1_g, ln1_b=ln1_b, w_ff1=w_ff1, b_ff1=b_ff1, w_ff2=w_ff2, b_ff2=b_ff2, ln2_g=ln2_g, ln2_b=ln2_b)
    ms = dict(ln_in_g=m_ln_in_g, ln_in_b=m_ln_in_b, ada_w=m_ada_w, ada_b=m_ada_b, w_in=m_w_in, conv_w=m_conv_w,
              conv_b=m_conv_b, dt_bias=m_dt_bias, a_log=m_a_log, d_skip=m_d_skip, ssd_norm_w=m_ssd_norm_w,
              attn_sinks=m_attn_sinks, w_out=m_w_out, ln1_g=m_ln1_g, ln1_b=m_ln1_b, w_ff1=m_w_ff1, b_ff1=m_b_ff1,
              w_ff2=m_w_ff2, b_ff2=m_b_ff2, ln2_g=m_ln2_g, ln2_b=m_ln2_b)
    vs = dict(ln_in_g=v_ln_in_g, ln_in_b=v_ln_in_b, ada_w=v_ada_w, ada_b=v_ada_b, w_in=v_w_in, conv_w=v_conv_w,
              conv_b=v_conv_b, dt_bias=v_dt_bias, a_log=v_a_log, d_skip=v_d_skip, ssd_norm_w=v_ssd_norm_w,
              attn_sinks=v_attn_sinks, w_out=v_w_out, ln1_g=v_ln1_g, ln1_b=v_ln1_b, w_ff1=v_w_ff1, b_ff1=v_b_ff1,
              w_ff2=v_w_ff2, b_ff2=v_b_ff2, ln2_g=v_ln2_g, ln2_b=v_ln2_b)

    L, D = x.shape[1], x.shape[2]
    depth = w_in.shape[0]
    assert depth == 1 and x.shape[0] == 1 and L % CHUNK == 0
    W = D
    H = W // HEAD_DIM
    CD = W + 2 * SSD_GROUPS * SSD_STATE
    AW = D
    AH = AW // HEAD_DIM
    KV = AH // GQA
    KVW2 = 2 * KV * HEAD_DIM
    PROJ = W + CD + H + AW + KVW2
    FF = w_ff1.shape[2] * N_DEV
    MIX = w_out.shape[1] * N_DEV
    assert w_in.shape[2] * N_DEV == PROJ and MIX == W + AW and H <= LANE and AH <= LANE
    dims = dict(W=W, H=H, CD=CD, AW=AW, AH=AH, KV=KV, KVW2=KVW2)
    alpha = (2.0 * depth) ** 0.25
    C6 = ada_w.shape[2]
    CW = conv_w.shape[2]

    ax, ay, ac = _my_pos()
    me = 4 * ax + 2 * ay + ac
    x2 = x.reshape(L, D)
    tgt = loss_target.reshape(L, D)
    r1 = lambda a: a.reshape(1, -1)

    ada_b_cols = lax.dynamic_slice(ada_b, (0, me * C6), (1, C6))
    cs_all, mod = _mod_fwd(c, ada_w[0], ada_b_cols)
    sh1, sc1, g1, sh2, sc2, g2 = [r1(t) for t in jnp.split(mod.reshape(-1), 6)]

    wg_in, cwg = _ag_weights([w_in[0].astype(BF16), conv_w[0]])
    ag_ss, ag_rs, ag_src, ag_land, ag_token = _ici_start(
        [w_out[0].astype(BF16), w_ff1[0].astype(BF16), w_ff2[0].astype(BF16)], N_DEV, cwg, "gather", "ag_ici_start")
    sh1 = sh1 + ag_token[0:1, 0:1]
    w_in_full = wg_in.transpose(1, 0, 2).reshape(D, PROJ)
    i1, i2, i3, i4 = W, W + CD, W + CD + H, W + CD + H + AW
    w_pad = jnp.concatenate([w_in_full[:, :i2], w_in_full[:, i3:], w_in_full[:, i2:i3], jnp.zeros((D, LANE - H), BF16)], axis=1)
    cw_full = cwg.transpose(1, 0, 2).reshape(CONV_K, CD)

    dtb = _pad_lanes(dt_bias, LANE)
    alog = _pad_lanes(a_log, LANE)
    dsk = jnp.repeat(d_skip.reshape(-1), HEAD_DIM).reshape(1, W)
    sinks = attn_sinks.reshape(-1)
    g_in, b_in = r1(ln_in_g), r1(ln_in_b)

    xhat0, rstd0, u1, z, xbc, q, kv, dt_raw = _ln_in_proj(x2, g_in, b_in, sc1, sh1, w_pad, dims)
    y, yn, sprev = _conv_ssd(xbc, dt_raw, z, cw_full, conv_b, dtb, alog, dsk, ssd_norm_w, dims)
    o = _swa_fwd(q, kv, sinks, dims)
    ag_src, ag_land = _ici_wait(ag_ss, ag_rs, ag_src, ag_land, o, "gather", "ag_ici_wait")
    wg_out, wg_ff1, wg_ff2 = _ag_d2d(ag_src, ag_land)
    w_out_full = wg_out.reshape(MIX, D)
    w1_full = wg_ff1.transpose(1, 0, 2).reshape(D, FF)
    w2_full = wg_ff2.reshape(FF, D)
    mix, xhat1, rstd1, u2 = _out_proj_ln1(yn, o, w_out_full, xhat0, _vec8([g_in, b_in, g1, ln1_g, ln1_b, sc2, sh2], D), alpha)
    rr, dr2, acc_f, loss_loc = _mlp_loss(u2, w1_full, w2_full, xhat1, tgt,
                                         _vec8([ln1_g, ln1_b, g2, ln2_g, ln2_b, b_ff2], D), b_ff1, alpha)

    df, da, gb2, gb1 = _mlp_bwd_a(dr2, rr, w2_full, g2)
    gw_ff2 = _matmul_tn(rr, df, "gw_ff2", square_a=True)
    gw_ff1 = _matmul_tn(u2, da, "gw_ff1")
    dmix, dh0p, dyn, do, acc_b = _mlp_bwd_b(da, w1_full, dr2, xhat1, rstd1, mix, w_out_full,
                                            _vec8([ln1_g, ln1_b, sc2, g1], D), alpha, W)
    gw_out = jnp.concatenate([_matmul_tn(yn, dmix, "gw_out_ssd"), _matmul_tn(o, dmix, "gw_out_attn")], axis=0)

    core = jnp.reshape(ac, (1,)).astype(jnp.int32)
    blocked1 = [gw_out.reshape(N_DEV, MIX // N_DEV, D), gw_ff1.reshape(D, N_DEV, FF // N_DEV).transpose(1, 0, 2),
                gw_ff2.reshape(N_DEV, FF // N_DEV, D)]
    pairs1 = [_pair_sum(b, r, core) for b, r in zip(blocked1, _rs_d2d(blocked1, "rs_d2d_1"))]
    rs_ss, rs_rs, pairs1, rs_land, rs_token = _ici_start(pairs1, N_CHIP, do, "scatter", "rs_ici_start")
    sinks_b = sinks + rs_token[0, 0]

    dq, dkv, dsink = _swa_bwd(q, kv, do, sinks_b, dims)
    dz, dpre, ddt, acc_s, hacc = _ssd_bwd(dyn, y, z, xbc, dt_raw, sprev, cw_full, conv_b, dtb, alog, dsk, ssd_norm_w, dims)
    dxbc, acc_c = _conv_bwd(dpre, xbc, cw_full)
    grad_x, acc_i = _in_proj_bwd(dz, dxbc, dq, dkv, ddt, w_pad, xhat0, rstd0, dh0p, _vec8([g_in, b_in, sc1], D), dims)
    gz = _matmul_tn(u1, dz, "gw_in_z")
    gxbc = _matmul_tn(u1, dxbc, "gw_in_xbc")
    gq = _matmul_tn(u1, dq, "gw_in_q")
    gkv = _matmul_tn(u1, dkv, "gw_in_kv")
    gdt = _matmul_tn(u1, ddt, "gw_in_dt")
    gw_in = jnp.concatenate([gz, gxbc, gdt[:, :H], gq, gkv], axis=1)

    dmod = jnp.concatenate([acc_i[1], acc_i[0], acc_b[4], acc_b[1], acc_b[0], acc_f[2]])
    small_g = dict(ada_b=dmod, ln_in_g=acc_i[2], ln_in_b=acc_i[3], conv_b=acc_c[CONV_K], dt_bias=hacc[0, :H], a_log=hacc[1, :H],
                   d_skip=hacc[2, :H], ssd_norm_w=acc_s[0], attn_sinks=dsink[0, :AH], ln1_g=acc_b[2], ln1_b=acc_b[3],
                   b_ff1=gb1[0], b_ff2=gb2[0], ln2_g=acc_f[0], ln2_b=acc_f[1])
    segs = [small_g[n] for n in _SMALL] + [acc_c[:CONV_K].reshape(-1), loss_loc[0, :1]]
    pack, offs, sizes = _pack(segs)
    gathered, summed = _small_gather_sum(pack)
    gathered = gathered.reshape(N_DEV, -1)
    summed = summed.reshape(-1)
    seg = lambda k: summed[offs[k]:offs[k] + sizes[k]]
    grads = {n: seg(k).reshape(wts[n].shape) for k, n in enumerate(_SMALL)}
    gcw_full = seg(len(_SMALL)).reshape(CONV_K, CD)
    grads['conv_w'] = lax.dynamic_slice(gcw_full, (0, me * CW), (CONV_K, CW)).reshape(conv_w.shape)
    loss = seg(len(_SMALL) + 1)[0]

    names = list(_SMALL) + ['conv_w']
    pw, poffs, psizes = _pack([wts[n] for n in names])
    pg, _, _ = _pack([grads[n] for n in names])
    pm, _, _ = _pack([ms[n] for n in names])
    pv, _, _ = _pack([vs[n] for n in names])
    pd, pm2, pv2 = [t.reshape(-1) for t in _adamw(pw, pg, pm, pv)]
    deltas, new_m, new_v = {}, {}, {}
    for k, n in enumerate(names):
        sl = slice(poffs[k], poffs[k] + psizes[k])
        deltas[n], new_m[n], new_v[n] = (t[sl].reshape(wts[n].shape) for t in (pd, pm2, pv2))

    dmod_cols = lax.dynamic_slice(gathered, (0, offs[0] + me * C6), (N_DEV, C6))
    pad16 = lambda t: jnp.concatenate([t, jnp.zeros((16 - N_DEV,) + t.shape[1:], t.dtype)], axis=0)
    g_, d_, m_, v_ = _ada_grad_adamw(pad16(cs_all), pad16(dmod_cols), ada_w[0], m_ada_w[0], v_ada_w[0])
    grads['ada_w'], deltas['ada_w'], new_m['ada_w'], new_v['ada_w'] = (t[None] for t in (g_, d_, m_, v_))

    blocked2 = [gw_in.reshape(D, N_DEV, PROJ // N_DEV).transpose(1, 0, 2)]
    pairs2 = [_pair_sum(b, r, core) for b, r in zip(blocked2, _rs_d2d(blocked2, "rs_d2d_2"))]
    g_, d_, m_, v_ = _sum_adamw(_rs_ici(pairs2)[0], wts['w_in'][0], ms['w_in'][0], vs['w_in'][0])
    grads['w_in'], deltas['w_in'], new_m['w_in'], new_v['w_in'] = (t[None] for t in (g_, d_, m_, v_))
    pairs1, rs_land = _ici_wait(rs_ss, rs_rs, pairs1, rs_land, g_, "scatter", "rs_ici_wait")
    mychip = 2 * ax + ay
    chips = jnp.stack([(mychip + k) % N_CHIP for k in range(N_CHIP)]).astype(jnp.int32)
    for n, own, land in zip(('w_out', 'w_ff1', 'w_ff2'), pairs1, rs_land):
        g_, d_, m_, v_ = _sum_adamw_split(own, land, chips, wts[n][0], ms[n][0], vs[n][0])
        grads[n], deltas[n], new_m[n], new_v[n] = (t[None] for t in (g_, d_, m_, v_))

    return (loss, grad_x.reshape(x.shape), *[grads[n] for n in _WEIGHTS], *[deltas[n] for n in _WEIGHTS],
            *[new_m[n] for n in _WEIGHTS], *[new_v[n] for n in _WEIGHTS])
```

```python
import functools
import math

import numpy as np
import jax
import jax.numpy as jnp
from jax import lax
from jax.experimental import pallas as pl
from jax.experimental.pallas import tpu as pltpu

F32 = jnp.float32
BF16 = jnp.bfloat16
MESH = pl.DeviceIdType.MESH

N_DEV = 8
N_CHIP = 4
HEAD_DIM = 64
SSD_GROUPS = 2
SSD_STATE = 128
CHUNK = 128
CONV_K = 4
GQA = 8
LANE = 128
HALO = 8
LN_EPS = 1e-5
RMS_EPS = 1e-5
NEG = -1e30
ADAM_LR, ADAM_B1, ADAM_B2, ADAM_EPS, ADAM_WD, ADAM_STEP = 0.001, 0.9, 0.999, 1e-08, 0.01, 10
V7X_VMEM_BYTES = 64 * 1024 * 1024
VMEM_LIMIT = V7X_VMEM_BYTES - 8 * 1024 * 1024
HI = lax.Precision.HIGHEST


def _alibi_slopes(n):
    def pow2(m):
        start = 2.0 ** (-8.0 / m)
        return [start ** (i + 1) for i in range(m)]
    if math.log2(n).is_integer():
        s = pow2(n)
    else:
        c = 2 ** math.floor(math.log2(n))
        s = pow2(c) + pow2(2 * c)[0::2][: n - c]
    return [float(v) for v in np.array(s, dtype=np.float32)]


def _dot(a, b):
    return jnp.dot(a, b, preferred_element_type=F32)


def _dot_nt(a, b):
    return lax.dot_general(a, b, (((1,), (1,)), ((), ())), preferred_element_type=F32)


def _dot_tn(a, b):
    return lax.dot_general(a, b, (((0,), (0,)), ((), ())), preferred_element_type=F32)


def _dot_hi(a, b):
    return jnp.dot(a, b, precision=HI, preferred_element_type=F32)


def _sigmoid(x):
    return 1.0 / (1.0 + jnp.exp(-x))


def _softplus(x):
    return jnp.maximum(x, 0.0) + jnp.log(1.0 + jnp.exp(-jnp.abs(x)))


def _mean(x):
    return jnp.mean(x, axis=-1, keepdims=True)


def _ln_fwd(x):
    xc = x - _mean(x)
    rstd = lax.rsqrt(_mean(xc * xc) + LN_EPS)
    return xc * rstd, rstd


def _ln_bwd(dxhat, xhat, rstd):
    return rstd * (dxhat - _mean(dxhat) - xhat * _mean(dxhat * xhat))


def _colsum(x):
    return jnp.sum(x, axis=0, keepdims=True)


def _params(sem):
    return pltpu.CompilerParams(dimension_semantics=sem, vmem_limit_bytes=VMEM_LIMIT)


def _tile(i_map_cols, tm):
    return pl.BlockSpec((tm, i_map_cols), lambda i: (i, 0))


def _res(shape):
    return pl.BlockSpec(shape, lambda *_: (0,) * len(shape), pipeline_mode=pl.Buffered(1))


def _acc(shape):
    return pl.BlockSpec(shape, lambda *_: (0,) * len(shape))


def _S(shape, dtype):
    return jax.ShapeDtypeStruct(shape, dtype)


def _my_pos():
    return lax.axis_index("x"), lax.axis_index("y"), lax.axis_index("c")


def _peer(pos, k):
    x, y, c = pos
    px = 1 - x if k & 4 else x
    py = 1 - y if k & 2 else y
    pc = 1 - c if k & 1 else c
    return (px, py, pc)


def _lin(p):
    return 4 * p[0] + 2 * p[1] + p[2]


def _mod_fwd(c_loc, ada_w_loc, ada_b_cols):
    D = c_loc.shape[1]
    C6 = ada_w_loc.shape[1]

    def body(c_ref, w_ref, b_ref, cs_ref, mod_ref, call_ref, modp_ref, ssem, rsem):
        pos = _my_pos()
        me = _lin(pos)
        call_ref[me] = c_ref[...]
        sends = []
        for k in range(1, N_DEV):
            cp = pltpu.make_async_remote_copy(src_ref=c_ref, dst_ref=call_ref.at[me], send_sem=ssem.at[k - 1],
                                              recv_sem=rsem.at[k - 1], device_id=_peer(pos, k), device_id_type=MESH)
            cp.start()
            sends.append(cp)
        for k in range(1, N_DEV):
            src = _lin(_peer(pos, k))
            pltpu.make_async_remote_copy(src_ref=c_ref, dst_ref=call_ref.at[src], send_sem=ssem.at[k - 1],
                                         recv_sem=rsem.at[k - 1], device_id=pos, device_id_type=MESH).wait_recv()
        for cp in sends:
            cp.wait_send()
        call = jnp.concatenate([call_ref[b] for b in range(N_DEV)], axis=0)
        cs = call * _sigmoid(call)
        cs_ref[...] = cs
        modp = _dot(cs.astype(BF16), w_ref[...].astype(BF16)) + b_ref[...]
        for b in range(N_DEV):
            modp_ref[b] = modp[b:b + 1, :]
        mod_ref[me] = modp_ref[me]
        sends = []
        for k in range(1, N_DEV):
            peer = _peer(pos, k)
            cp = pltpu.make_async_remote_copy(src_ref=modp_ref.at[_lin(peer)], dst_ref=mod_ref.at[me],
                                              send_sem=ssem.at[N_DEV - 2 + k], recv_sem=rsem.at[N_DEV - 2 + k],
                                              device_id=peer, device_id_type=MESH)
            cp.start()
            sends.append(cp)
        for k in range(1, N_DEV):
            src = _lin(_peer(pos, k))
            pltpu.make_async_remote_copy(src_ref=modp_ref.at[src], dst_ref=mod_ref.at[src],
                                         send_sem=ssem.at[N_DEV - 2 + k], recv_sem=rsem.at[N_DEV - 2 + k],
                                         device_id=pos, device_id_type=MESH).wait_recv()
        for cp in sends:
            cp.wait_send()

    vm = pl.BlockSpec(memory_space=pltpu.VMEM)
    return pl.pallas_call(
        body, name="mod_fwd",
        out_shape=(_S((N_DEV, D), F32), _S((N_DEV, 1, C6), F32)),
        in_specs=[vm, vm, vm], out_specs=(vm, vm),
        scratch_shapes=[pltpu.VMEM((N_DEV, 1, D), F32), pltpu.VMEM((N_DEV, 1, C6), F32),
                        pltpu.SemaphoreType.DMA((2 * (N_DEV - 1),)), pltpu.SemaphoreType.DMA((2 * (N_DEV - 1),))],
        compiler_params=pltpu.CompilerParams(vmem_limit_bytes=VMEM_LIMIT),
    )(c_loc, ada_w_loc, ada_b_cols)


def _small_gather_sum(pack):
    P8 = pack.shape[1]

    def body(p_ref, gat_ref, sum_ref, ssem, rsem):
        pos = _my_pos()
        me = _lin(pos)
        gat_ref[me] = p_ref[...]
        sends = []
        for k in range(1, N_DEV):
            cp = pltpu.make_async_remote_copy(src_ref=p_ref, dst_ref=gat_ref.at[me], send_sem=ssem.at[k - 1],
                                              recv_sem=rsem.at[k - 1], device_id=_peer(pos, k), device_id_type=MESH)
            cp.start()
            sends.append(cp)
        for k in range(1, N_DEV):
            src = _lin(_peer(pos, k))
            pltpu.make_async_remote_copy(src_ref=p_ref, dst_ref=gat_ref.at[src], send_sem=ssem.at[k - 1],
                                         recv_sem=rsem.at[k - 1], device_id=pos, device_id_type=MESH).wait_recv()
        for cp in sends:
            cp.wait_send()
        acc = gat_ref[0]
        for j in range(1, N_DEV):
            acc = acc + gat_ref[j]
        sum_ref[...] = acc

    vm = pl.BlockSpec(memory_space=pltpu.VMEM)
    return pl.pallas_call(
        body, name="small_gather_sum",
        out_shape=(_S((N_DEV, 8, P8), F32), _S((8, P8), F32)),
        in_specs=[vm], out_specs=(vm, vm),
        scratch_shapes=[pltpu.SemaphoreType.DMA((N_DEV - 1,)), pltpu.SemaphoreType.DMA((N_DEV - 1,))],
        compiler_params=pltpu.CompilerParams(vmem_limit_bytes=VMEM_LIMIT),
    )(pack)


def _ag_weights(shards, after):
    n = len(shards)

    def body(*refs):
        ins, outs = refs[:n], refs[n + 1:2 * n + 1]
        ssem, rsem, lsem = refs[2 * n + 1:]
        x, y, c = pos = _my_pos()
        me = _lin(pos)
        sib = (x, y, 1 - c)
        chips = [(1 - x, y), (x, 1 - y), (1 - x, 1 - y)]

        def copy(a, k, block, to, src=None):
            return pltpu.make_async_remote_copy(
                src_ref=outs[a].at[block] if src is None else src, dst_ref=outs[a].at[block],
                send_sem=ssem.at[a * 7 + k], recv_sem=rsem.at[a * 7 + k], device_id=to, device_id_type=MESH)

        local = [pltpu.make_async_copy(ins[a], outs[a].at[me], lsem.at[a]) for a in range(n)]
        for cp in local:
            cp.start()
        first = []
        for a in range(n):
            first.append(copy(a, 0, me, sib, src=ins[a]))
            first += [copy(a, 1 + j, me, (*chip, c), src=ins[a]) for j, chip in enumerate(chips)]
        for cp in first:
            cp.start()
        passed = []
        for a in range(n):
            for j, chip in enumerate(chips):
                blk = _lin((*chip, c))
                copy(a, 1 + j, blk, pos).wait_recv()
                cp = copy(a, 4 + j, blk, sib)
                cp.start()
                passed.append(cp)
        for a in range(n):
            copy(a, 0, _lin(sib), pos).wait_recv()
            for j, chip in enumerate(chips):
                copy(a, 4 + j, _lin((*chip, 1 - c)), pos).wait_recv()
        for cp in first + passed:
            cp.wait_send()
        for cp in local:
            cp.wait()

    hbm = pl.BlockSpec(memory_space=pl.ANY)
    return pl.pallas_call(
        body, name="ag_weights",
        out_shape=tuple(_S((N_DEV,) + s.shape, s.dtype) for s in shards),
        in_specs=[hbm] * (n + 1), out_specs=tuple([hbm] * n),
        scratch_shapes=[pltpu.SemaphoreType.DMA((7 * n,)), pltpu.SemaphoreType.DMA((7 * n,)),
                        pltpu.SemaphoreType.DMA((n,))],
    )(*shards, after)


def _rs_d2d(blocked, name):
    n = len(blocked)

    def body(*refs):
        ins, outs = refs[:n], refs[n:2 * n]
        ssem, rsem = refs[2 * n:]
        x, y, c = pos = _my_pos()
        sib = (x, y, 1 - c)
        cps = []
        for a in range(n):
            for j in range(N_CHIP):
                cp = pltpu.make_async_remote_copy(
                    src_ref=ins[a].at[2 * j + (1 - c)], dst_ref=outs[a].at[j], send_sem=ssem.at[a * N_CHIP + j],
                    recv_sem=rsem.at[a * N_CHIP + j], device_id=sib, device_id_type=MESH)
                cp.start()
                cps.append(cp)
        for cp in cps:
            cp.wait_recv()
        for cp in cps:
            cp.wait_send()

    hbm = pl.BlockSpec(memory_space=pl.ANY)
    return pl.pallas_call(
        body, name=name,
        out_shape=tuple(_S((N_CHIP,) + b.shape[1:], b.dtype) for b in blocked),
        in_specs=[hbm] * n, out_specs=tuple([hbm] * n),
        scratch_shapes=[pltpu.SemaphoreType.DMA((N_CHIP * n,)), pltpu.SemaphoreType.DMA((N_CHIP * n,))],
    )(*blocked)


def _rs_ici(pairs):
    n = len(pairs)

    def body(*refs):
        ins, outs = refs[:n], refs[n:2 * n]
        ssem, rsem, lsem = refs[2 * n:]
        x, y, c = pos = _my_pos()
        mychip = 2 * x + y
        local = [pltpu.make_async_copy(ins[a].at[mychip], outs[a].at[mychip], lsem.at[a]) for a in range(n)]
        for cp in local:
            cp.start()
        sends = []
        for a in range(n):
            for k in range(1, N_CHIP):
                tx, ty, _ = _peer(pos, 2 * k)
                cp = pltpu.make_async_remote_copy(
                    src_ref=ins[a].at[2 * tx + ty], dst_ref=outs[a].at[mychip], send_sem=ssem.at[a * 3 + k - 1],
                    recv_sem=rsem.at[a * 3 + k - 1], device_id=(tx, ty, c), device_id_type=MESH)
                cp.start()
                sends.append(cp)
        for a in range(n):
            for k in range(1, N_CHIP):
                tx, ty, _ = _peer(pos, 2 * k)
                pltpu.make_async_remote_copy(
                    src_ref=ins[a].at[2 * tx + ty], dst_ref=outs[a].at[2 * tx + ty], send_sem=ssem.at[a * 3 + k - 1],
                    recv_sem=rsem.at[a * 3 + k - 1], device_id=pos, device_id_type=MESH).wait_recv()
        for cp in sends:
            cp.wait_send()
        for cp in local:
            cp.wait()

    hbm = pl.BlockSpec(memory_space=pl.ANY)
    return pl.pallas_call(
        body, name="rs_ici",
        out_shape=tuple(_S(p.shape, p.dtype) for p in pairs),
        in_specs=[hbm] * n, out_specs=tuple([hbm] * n),
        scratch_shapes=[pltpu.SemaphoreType.DMA((3 * n,)), pltpu.SemaphoreType.DMA((3 * n,)),
                        pltpu.SemaphoreType.DMA((n,))],
    )(*pairs)


_HBM = pl.BlockSpec(memory_space=pltpu.HBM)
_SEM = pl.BlockSpec(memory_space=pltpu.SEMAPHORE)
_ANY = pl.BlockSpec(memory_space=pl.ANY)
_EFFECT = pltpu.SideEffectType.DATAFLOW_SIDE_EFFECTING


def _in_hbm(a):
    return pltpu.with_memory_space_constraint(a, pltpu.HBM)


def _plan_gather(n):
    def copies(pos):
        x, y, c = pos
        out = []
        for a in range(n):
            for dev in [(x, y, 1 - c)] + [(*_peer(pos, 2 * k)[:2], c) for k in range(1, N_CHIP)]:
                out.append((a, None, n + a, _lin(pos), dev, _lin(dev)))
        return out
    return copies


def _plan_forward(n):
    def copies(pos):
        x, y, c = pos
        out = []
        for a in range(n):
            for k in range(1, N_CHIP):
                tx, ty, _ = _peer(pos, 2 * k)
                out.append((a, _lin((tx, ty, c)), a, _lin((tx, ty, c)), (x, y, 1 - c), _lin((tx, ty, 1 - c))))
        return out
    return copies


def _plan_scatter(n):
    def copies(pos):
        x, y, c = pos
        out = []
        for a in range(n):
            for k in range(1, N_CHIP):
                tx, ty, _ = _peer(pos, 2 * k)
                out.append((a, 2 * tx + ty, n + a, 2 * x + y, (tx, ty, c), 2 * tx + ty))
        return out
    return copies


def _split_copy(refs, cp, ssem, rsem, i, arrival):
    si, s_slot, di, d_slot, dev, a_slot = cp
    return pltpu.make_async_remote_copy(
        src_ref=refs[si] if s_slot is None else refs[si].at[s_slot], dst_ref=refs[di].at[a_slot if arrival else d_slot],
        send_sem=ssem.at[i], recv_sem=rsem.at[i], device_id=dev, device_id_type=MESH)


def _split_start(arrays, copies, after, name):
    n = len(arrays)
    n_cp = len(copies((0, 0, 0)))

    def body(*refs):
        ssem, rsem, token = refs[n + 1], refs[n + 2], refs[-1]
        for i, cp in enumerate(copies(_my_pos())):
            _split_copy(refs, cp, ssem, rsem, i, False).start()
        token[...] = jnp.zeros_like(token)

    res = pl.pallas_call(
        body, name=name,
        out_shape=(pltpu.SemaphoreType.DMA((n_cp,)), pltpu.SemaphoreType.DMA((n_cp,)),
                   *[pltpu.HBM(a.shape, a.dtype) for a in arrays], _S((8, LANE), F32)),
        in_specs=[_HBM] * n + [_ANY],
        out_specs=(_SEM, _SEM, *[_HBM] * n, pl.BlockSpec(memory_space=pltpu.VMEM)),
        input_output_aliases={a: 2 + a for a in range(n)},
        compiler_params=pltpu.CompilerParams(has_side_effects=_EFFECT),
    )(*[_in_hbm(a) for a in arrays], after)
    return res[0], res[1], list(res[2:2 + n]), res[-1]


def _split_wait(ssem, rsem, arrays, copies, after, name):
    n = len(arrays)

    def body(*refs):
        for i, cp in enumerate(copies(_my_pos())):
            d = _split_copy(refs, cp, refs[n], refs[n + 1], i, True)
            d.wait_send()
            d.wait_recv()

    res = pl.pallas_call(
        body, name=name,
        out_shape=tuple(pltpu.HBM(a.shape, a.dtype) for a in arrays),
        in_specs=[_HBM] * n + [_SEM, _SEM, _ANY], out_specs=tuple([_HBM] * n),
        input_output_aliases={a: a for a in range(n)},
        compiler_params=pltpu.CompilerParams(has_side_effects=_EFFECT),
    )(*arrays, ssem, rsem, after)
    return list(res)


def _row_tile(R, itemsize_rows=16, cap=256):
    t = min(R, cap)
    while R % t or t % itemsize_rows:
        t -= itemsize_rows
    return t


def _pair_sum(blocked, recv, core):
    _, R, C = blocked.shape
    tr = _row_tile(R)

    def body(ids_ref, a_ref, b_ref, o_ref):
        del ids_ref
        o_ref[...] = (a_ref[...] + b_ref[...]).astype(BF16)

    gs = pltpu.PrefetchScalarGridSpec(
        num_scalar_prefetch=1, grid=(N_CHIP, R // tr),
        in_specs=[pl.BlockSpec((1, tr, C), lambda j, r, ids: (2 * j + ids[0], r, 0)),
                  pl.BlockSpec((1, tr, C), lambda j, r, ids: (j, r, 0))],
        out_specs=pl.BlockSpec((1, tr, C), lambda j, r, ids: (j, r, 0)))
    return pl.pallas_call(body, name="pair_sum", grid_spec=gs, out_shape=_S((N_CHIP, R, C), BF16),
                          compiler_params=_params(("arbitrary", "arbitrary")))(core, blocked, recv)


def _adamw_math(w, g, m, v):
    m2 = ADAM_B1 * m + (1.0 - ADAM_B1) * g
    v2 = ADAM_B2 * v + (1.0 - ADAM_B2) * (g * g)
    m_hat = m2 / (1.0 - ADAM_B1 ** ADAM_STEP)
    v_hat = v2 / (1.0 - ADAM_B2 ** ADAM_STEP)
    delta = -ADAM_LR * (m_hat / (jnp.sqrt(v_hat) + ADAM_EPS) + ADAM_WD * w)
    return delta, m2, v2


def _sum_adamw(parts, w, m, v):
    R, C = w.shape
    tr = _row_tile(R)

    def body(p_ref, w_ref, m_ref, v_ref, g_ref, d_ref, m2_ref, v2_ref):
        g = p_ref[0].astype(F32)
        for j in range(1, N_CHIP):
            g = g + p_ref[j].astype(F32)
        g_ref[...] = g
        d_ref[...], m2_ref[...], v2_ref[...] = _adamw_math(w_ref[...], g, m_ref[...], v_ref[...])

    t = pl.BlockSpec((tr, C), lambda r: (r, 0))
    return pl.pallas_call(
        body, name="sum_adamw", grid=(R // tr,),
        in_specs=[pl.BlockSpec((N_CHIP, tr, C), lambda r: (0, r, 0)), t, t, t], out_specs=(t, t, t, t),
        out_shape=tuple(_S((R, C), F32) for _ in range(4)), compiler_params=_params(("arbitrary",)))(parts, w, m, v)


def _sum_adamw_split(pairs, land, chips, w, m, v):
    R, C = w.shape
    tr = _row_tile(R)

    def body(ids_ref, own_ref, p1_ref, p2_ref, p3_ref, w_ref, m_ref, v_ref, g_ref, d_ref, m2_ref, v2_ref):
        del ids_ref
        g = own_ref[0].astype(F32) + p1_ref[0].astype(F32) + p2_ref[0].astype(F32) + p3_ref[0].astype(F32)
        g_ref[...] = g
        d_ref[...], m2_ref[...], v2_ref[...] = _adamw_math(w_ref[...], g, m_ref[...], v_ref[...])

    t = pl.BlockSpec((tr, C), lambda r, ids: (r, 0))
    slot = lambda k: pl.BlockSpec((1, tr, C), lambda r, ids: (ids[k], r, 0))
    gs = pltpu.PrefetchScalarGridSpec(num_scalar_prefetch=1, grid=(R // tr,),
                                      in_specs=[slot(0), slot(1), slot(2), slot(3), t, t, t], out_specs=(t, t, t, t))
    return pl.pallas_call(body, name="sum_adamw_split", grid_spec=gs, out_shape=tuple(_S((R, C), F32) for _ in range(4)),
                          compiler_params=_params(("arbitrary",)))(chips, pairs, land, land, land, w, m, v)


def _adamw(w, g, m, v):
    R, C = w.shape
    tr = _row_tile(R, 8)

    def body(w_ref, g_ref, m_ref, v_ref, d_ref, m2_ref, v2_ref):
        d_ref[...], m2_ref[...], v2_ref[...] = _adamw_math(w_ref[...], g_ref[...], m_ref[...], v_ref[...])

    t = pl.BlockSpec((tr, C), lambda r: (r, 0))
    return pl.pallas_call(body, name="adamw", grid=(R // tr,), in_specs=[t, t, t, t], out_specs=(t, t, t),
                          out_shape=tuple(_S((R, C), F32) for _ in range(3)),
                          compiler_params=_params(("arbitrary",)))(w, g, m, v)


def _ada_grad_adamw(cs16, dmod16, w, m, v):
    D, C6 = w.shape
    tr = _row_tile(D, 8, 256)

    def body(cs_ref, dm_ref, w_ref, m_ref, v_ref, g_ref, d_ref, m2_ref, v2_ref):
        g = _dot_tn(cs_ref[...].astype(BF16), dm_ref[...].astype(BF16))
        g_ref[...] = g
        d_ref[...], m2_ref[...], v2_ref[...] = _adamw_math(w_ref[...], g, m_ref[...], v_ref[...])

    t = pl.BlockSpec((tr, C6), lambda r: (r, 0))
    return pl.pallas_call(
        body, name="ada_grad_adamw", grid=(D // tr,),
        in_specs=[pl.BlockSpec((16, tr), lambda r: (0, r)), _acc((16, C6)), t, t, t], out_specs=(t, t, t, t),
        out_shape=tuple(_S((D, C6), F32) for _ in range(4)), compiler_params=_params(("arbitrary",)))(cs16, dmod16, w, m, v)


def _pick(n, cands):
    for c in cands:
        if n % c == 0:
            return c
    return n


def _matmul_tn(a, b, name, square_a=False):
    L, K = a.shape
    N = b.shape[1]
    bk = _pick(K, (1024, 512, 256, 128))
    bn = _pick(N, (1024, 768, 512, 256, 128))
    tl = _pick(L, (1024, 512, 256, 128))
    n_l = L // tl

    def body(a_ref, b_ref, o_ref):
        @pl.when(pl.program_id(2) == 0)
        def _():
            o_ref[...] = jnp.zeros_like(o_ref)
        av = a_ref[...]
        if square_a:
            av = av.astype(F32)
            av = av * av
        o_ref[...] += _dot_tn(av.astype(BF16), b_ref[...].astype(BF16))

    return pl.pallas_call(
        body, name=name, grid=(K // bk, N // bn, n_l),
        in_specs=[pl.BlockSpec((tl, bk), lambda k, n, l: (l, k)), pl.BlockSpec((tl, bn), lambda k, n, l: (l, n))],
        out_specs=pl.BlockSpec((bk, bn), lambda k, n, l: (k, n)), out_shape=_S((K, N), F32),
        compiler_params=_params(("arbitrary", "arbitrary", "arbitrary")))(a, b)


def _ln_in_proj(x, g, b, sc, sh, w_pad, dims):
    L, D = x.shape
    W, CD, AW, KVW2 = dims["W"], dims["CD"], dims["AW"], dims["KVW2"]
    NP = w_pad.shape[1]
    tm = _pick(L, (256, 128))
    o_z, o_xbc, o_q, o_kv, o_dt = 0, W, W + CD, W + CD + AW, W + CD + AW + KVW2

    def body(x_ref, g_ref, b_ref, sc_ref, sh_ref, w_ref, xhat_ref, rstd_ref, u1_ref, z_ref, xbc_ref, q_ref, kv_ref, dt_ref):
        xhat, rstd = _ln_fwd(x_ref[...])
        xhat_ref[...] = xhat
        rstd_ref[...] = rstd
        h0 = xhat * g_ref[...] + b_ref[...]
        u1 = (h0 * (1.0 + sc_ref[...]) + sh_ref[...]).astype(BF16)
        u1_ref[...] = u1
        z_ref[...] = _dot(u1, w_ref[:, o_z:o_xbc])
        xbc_ref[...] = _dot(u1, w_ref[:, o_xbc:o_q])
        q_ref[...] = _dot(u1, w_ref[:, o_q:o_kv]).astype(BF16)
        kv_ref[...] = _dot(u1, w_ref[:, o_kv:o_dt]).astype(BF16)
        dt_ref[...] = _dot(u1, w_ref[:, o_dt:NP])

    v = _acc((1, D))
    return pl.pallas_call(
        body, name="ln_in_proj", grid=(L // tm,),
        in_specs=[_tile(D, tm), v, v, v, v, _res((D, NP))],
        out_specs=(_tile(D, tm), _tile(1, tm), _tile(D, tm), _tile(W, tm), _tile(CD, tm), _tile(AW, tm),
                   _tile(KVW2, tm), _tile(LANE, tm)),
        out_shape=(_S((L, D), F32), _S((L, 1), F32), _S((L, D), BF16), _S((L, W), F32), _S((L, CD), F32),
                   _S((L, AW), BF16), _S((L, KVW2), BF16), _S((L, LANE), F32)),
        compiler_params=_params(("arbitrary",)))(x, g, b, sc, sh, w_pad)


def _conv_act(cur_ref, prev_ref, cw_ref, cb_ref, ext_ref, first):
    T = cur_ref.shape[0]
    ext_ref[0:HALO, :] = jnp.where(first, 0.0, prev_ref[...])
    ext_ref[HALO:HALO + T, :] = cur_ref[...]
    pre = cb_ref[...] + cw_ref[0:1, :] * ext_ref[HALO - 3:HALO - 3 + T, :]
    for k in range(1, CONV_K):
        pre = pre + cw_ref[k:k + 1, :] * ext_ref[HALO - 3 + k:HALO - 3 + k + T, :]
    return pre * _sigmoid(pre), pre


def _tri(T, upper=False):
    r = lax.broadcasted_iota(jnp.int32, (T, T), 0)
    c = lax.broadcasted_iota(jnp.int32, (T, T), 1)
    return (r <= c) if upper else (r >= c)


def _expand_heads(dst_ref, v, n_heads):
    for h in range(n_heads):
        dst_ref[:, h * HEAD_DIM:(h + 1) * HEAD_DIM] = jnp.broadcast_to(v[:, h:h + 1], (v.shape[0], HEAD_DIM))


def _head_reduce(v):
    wdt = v.shape[1]
    ch = lax.broadcasted_iota(jnp.int32, (wdt, LANE), 0)
    lo = lax.broadcasted_iota(jnp.int32, (wdt, LANE), 1) * HEAD_DIM
    onehot = ((ch >= lo) & (ch < lo + HEAD_DIM)).astype(BF16)
    hi = v.astype(BF16)
    rest = (v - hi.astype(F32)).astype(BF16)
    return _dot(hi, onehot) + _dot(rest, onehot)


def _conv_ssd(xbc, dt_raw, z, cw, cb, dtb, alog, dsk, nw, dims):
    L, CD = xbc.shape
    W, H, G, N = dims["W"], dims["H"], SSD_GROUPS, SSD_STATE
    T = CHUNK
    R = H // G
    GW = W // G
    nc = L // T
    HP = H * HEAD_DIM

    def body(xbc_ref, prev_ref, dt_ref, z_ref, cw_ref, cb_ref, dtb_ref, alog_ref, dsk_ref, nw_ref,
             y_ref, yn_ref, sp_ref, ext_ref, s_ref, ybuf_ref, dtx_ref, acx_ref, xb_ref):
        i = pl.program_id(0)

        @pl.when(i == 0)
        def _():
            s_ref[...] = jnp.zeros_like(s_ref)

        act, _ = _conv_act(xbc_ref, prev_ref, cw_ref, cb_ref, ext_ref, i == 0)
        xs = act[:, :W]
        dt = _softplus(dt_ref[...] + dtb_ref[...])
        a = dt * (-jnp.exp(alog_ref[...]))
        low = _tri(T)
        acum = _dot_hi(low.astype(F32), a)
        acum_t = _dot_hi(a.T, _tri(T, upper=True).astype(F32))
        _expand_heads(dtx_ref, dt, H)
        _expand_heads(acx_ref, acum, H)
        acx = acx_ref[...]
        lastx = acx[T - 1:T, :]
        xd = xs * dtx_ref[...]
        xb_ref[...] = xd.astype(BF16)
        xdb = (xd * jnp.exp(lastx - acx)).astype(BF16)
        ex = jnp.exp(acx)
        elx = jnp.exp(lastx)
        for g in range(G):
            gs = slice(g * GW, (g + 1) * GW)
            bgb = act[:, W + g * N:W + (g + 1) * N].astype(BF16)
            cgb = act[:, W + G * N + g * N:W + G * N + (g + 1) * N].astype(BF16)
            stg = s_ref[:, gs]
            sp_ref[0, :, gs] = stg
            yoff = ex[:, gs] * _dot(cgb, stg.astype(BF16))
            s_ref[:, gs] = stg * elx[:, gs] + _dot_tn(bgb, xdb[:, gs])
            cb_g = _dot_nt(cgb, bgb)
            for r in range(R):
                h = g * R + r
                hs = slice(h * HEAD_DIM, (h + 1) * HEAD_DIM)
                lm = jnp.where(low, jnp.exp(acum[:, h:h + 1] - acum_t[h:h + 1, :]), 0.0)
                ybuf_ref[:, hs] = _dot((cb_g * lm).astype(BF16), xb_ref[:, hs]) + yoff[:, r * HEAD_DIM:(r + 1) * HEAD_DIM]
        y = ybuf_ref[...] + dsk_ref[...] * xs
        y_ref[...] = y
        zz = z_ref[...]
        hh = y * (zz * _sigmoid(zz))
        for g in range(G):
            gs = slice(g * GW, (g + 1) * GW)
            hg = hh[:, gs]
            yn_ref[:, gs] = (hg * lax.rsqrt(_mean(hg * hg) + RMS_EPS) * nw_ref[:, gs]).astype(BF16)

    return pl.pallas_call(
        body, name="conv_ssd", grid=(nc,),
        in_specs=[_tile(CD, T), pl.BlockSpec((HALO, CD), lambda i: (jnp.maximum(i * (T // HALO) - 1, 0), 0)),
                  _tile(LANE, T), _tile(W, T), _acc((CONV_K, CD)), _acc((1, CD)), _acc((1, LANE)), _acc((1, LANE)),
                  _acc((1, W)), _acc((1, W))],
        out_specs=(_tile(W, T), _tile(W, T), pl.BlockSpec((1, N, HP), lambda i: (i, 0, 0))),
        out_shape=(_S((L, W), F32), _S((L, W), BF16), _S((nc, N, HP), F32)),
        scratch_shapes=[pltpu.VMEM((T + HALO, CD), F32), pltpu.VMEM((N, HP), F32), pltpu.VMEM((T, W), F32),
                        pltpu.VMEM((T, W), F32), pltpu.VMEM((T, W), F32), pltpu.VMEM((T, W), BF16)],
        compiler_params=_params(("arbitrary",)))(xbc, xbc, dt_raw, z, cw, cb, dtb, alog, dsk, nw)


def _attn_mask(T, i):
    r = lax.broadcasted_iota(jnp.int32, (T, 2 * T), 0)
    c = lax.broadcasted_iota(jnp.int32, (T, 2 * T), 1)
    dist = r + T - c
    valid = (dist >= 0) & (dist < CHUNK) & ((c >= T) | (i > 0))
    return dist.astype(F32), valid


def _attn_probs(s_raw, dist, valid, slope, sink, axis):
    s = s_raw * (HEAD_DIM ** -0.5) - slope * dist
    s = jnp.where(valid, s, NEG)
    m = jnp.maximum(jnp.max(s, axis=axis, keepdims=True), sink)
    p = jnp.exp(s - m)
    e_sink = jnp.exp(sink - m)
    inv = 1.0 / (jnp.sum(p, axis=axis, keepdims=True) + e_sink)
    return p * inv, e_sink * inv


def _kv_heads(kvc_ref, kvp_ref, g, n_kv):
    ks = slice(g * HEAD_DIM, (g + 1) * HEAD_DIM)
    vs = slice((n_kv + g) * HEAD_DIM, (n_kv + g + 1) * HEAD_DIM)
    kk = jnp.concatenate([kvp_ref[:, ks], kvc_ref[:, ks]], axis=0)
    vv = jnp.concatenate([kvp_ref[:, vs], kvc_ref[:, vs]], axis=0)
    return kk, vv


def _swa_fwd(q, kv, sinks, dims):
    L, AW = q.shape
    KV, KVW2 = dims["KV"], dims["KVW2"]
    T = CHUNK
    nb = L // T
    slopes = _alibi_slopes(dims["AH"])

    def body(q_ref, kvc_ref, kvp_ref, sink_ref, o_ref, qg_ref, p_ref):
        i = pl.program_id(0)
        dist, valid = _attn_mask(T, i)
        for g in range(KV):
            kk, vv = _kv_heads(kvc_ref, kvp_ref, g, KV)
            for r in range(GQA):
                h = g * GQA + r
                qg_ref[r * T:(r + 1) * T, :] = q_ref[:, h * HEAD_DIM:(h + 1) * HEAD_DIM]
            s_all = _dot_nt(qg_ref[...], kk)
            for r in range(GQA):
                h = g * GQA + r
                p, _ = _attn_probs(s_all[r * T:(r + 1) * T, :], dist, valid, slopes[h], sink_ref[h], -1)
                p_ref[r * T:(r + 1) * T, :] = p.astype(BF16)
            o_all = _dot(p_ref[...], vv)
            for r in range(GQA):
                h = g * GQA + r
                o_ref[:, h * HEAD_DIM:(h + 1) * HEAD_DIM] = o_all[r * T:(r + 1) * T, :].astype(BF16)

    return pl.pallas_call(
        body, name="swa_fwd", grid=(nb,),
        in_specs=[_tile(AW, T), _tile(KVW2, T), pl.BlockSpec((T, KVW2), lambda i: (jnp.maximum(i - 1, 0), 0)),
                  pl.BlockSpec(memory_space=pltpu.SMEM)],
        out_specs=_tile(AW, T), out_shape=_S((L, AW), BF16),
        scratch_shapes=[pltpu.VMEM((GQA * T, HEAD_DIM), BF16), pltpu.VMEM((GQA * T, 2 * T), BF16)],
        compiler_params=_params(("arbitrary",)))(q, kv, kv, sinks)


def _out_proj_ln1(yn, o, w_out, xhat0, vecs, alpha):
    L, W = yn.shape
    D = xhat0.shape[1]
    MIX = w_out.shape[0]
    tm = _pick(L, (256, 128))

    def body(yn_ref, o_ref, w_ref, xh_ref, v_ref, mix_ref, xhat1_ref, rstd1_ref, u2_ref):
        mix = _dot(yn_ref[...], w_ref[0:W, :]) + _dot(o_ref[...], w_ref[W:MIX, :])
        mix_ref[...] = mix
        h0 = xh_ref[...] * v_ref[0:1, :] + v_ref[1:2, :]
        xhat1, rstd1 = _ln_fwd(alpha * h0 + (1.0 + v_ref[2:3, :]) * mix)
        xhat1_ref[...] = xhat1
        rstd1_ref[...] = rstd1
        h1 = xhat1 * v_ref[3:4, :] + v_ref[4:5, :]
        u2_ref[...] = (h1 * (1.0 + v_ref[5:6, :]) + v_ref[6:7, :]).astype(BF16)

    return pl.pallas_call(
        body, name="out_proj_ln1", grid=(L // tm,),
        in_specs=[_tile(W, tm), _tile(MIX - W, tm), _res((MIX, D)), _tile(D, tm), _acc((8, D))],
        out_specs=(_tile(D, tm), _tile(D, tm), _tile(1, tm), _tile(D, tm)),
        out_shape=(_S((L, D), F32), _S((L, D), F32), _S((L, 1), F32), _S((L, D), BF16)),
        compiler_params=_params(("arbitrary",)))(yn, o, w_out, xhat0, vecs)


def _mlp_loss(u2, w1, w2, xhat1, tgt, vecs, b1, alpha):
    L, D = xhat1.shape
    FF = w1.shape[1]
    tm = _pick(L, (256, 128))
    fc = _pick(FF, (512, 256, 128))

    def body(u2_ref, w1_ref, w2_ref, xh_ref, t_ref, v_ref, b1_ref, rr_ref, dr2_ref, acc_ref, loss_ref):
        @pl.when(pl.program_id(0) == 0)
        def _():
            acc_ref[...] = jnp.zeros_like(acc_ref)
            loss_ref[...] = jnp.zeros_like(loss_ref)

        u2 = u2_ref[...]
        f = jnp.zeros((tm, D), F32) + v_ref[5:6, :]
        for j in range(FF // fc):
            cs = slice(j * fc, (j + 1) * fc)
            rr = jnp.maximum(_dot(u2, w1_ref[:, cs]) + b1_ref[:, cs], 0.0)
            rr_ref[:, cs] = rr.astype(BF16)
            f = f + _dot((rr * rr).astype(BF16), w2_ref[cs, :])
        xhat1 = xh_ref[...]
        h1 = xhat1 * v_ref[0:1, :] + v_ref[1:2, :]
        xhat2, rstd2 = _ln_fwd(alpha * h1 + (1.0 + v_ref[2:3, :]) * f)
        e = xhat2 * v_ref[3:4, :] + v_ref[4:5, :] - t_ref[...]
        loss_ref[...] += 0.5 * jnp.sum(_mean(e * e))
        dy = e * (1.0 / D)
        dr2 = _ln_bwd(dy * v_ref[3:4, :], xhat2, rstd2)
        dr2_ref[...] = dr2
        acc_ref[0:1, :] += _colsum(dy * xhat2)
        acc_ref[1:2, :] += _colsum(dy)
        acc_ref[2:3, :] += _colsum(dr2 * f)

    return pl.pallas_call(
        body, name="mlp_loss", grid=(L // tm,),
        in_specs=[_tile(D, tm), _res((D, FF)), _res((FF, D)), _tile(D, tm), _tile(D, tm), _acc((8, D)), _acc((1, FF))],
        out_specs=(_tile(FF, tm), _tile(D, tm), _acc((8, D)), _acc((1, LANE))),
        out_shape=(_S((L, FF), BF16), _S((L, D), F32), _S((8, D), F32), _S((1, LANE), F32)),
        compiler_params=_params(("arbitrary",)))(u2, w1, w2, xhat1, tgt, vecs, b1)


def _mlp_bwd_a(dr2, rr, w2, g2):
    L, D = dr2.shape
    FF = w2.shape[0]
    tm = _pick(L, (256, 128))
    fc = _pick(FF, (512, 256, 128))

    def body(dr2_ref, rr_ref, w2_ref, g2_ref, df_ref, da_ref, gb2_ref, gb1_ref):
        @pl.when(pl.program_id(0) == 0)
        def _():
            gb2_ref[...] = jnp.zeros_like(gb2_ref)
            gb1_ref[...] = jnp.zeros_like(gb1_ref)

        df = (1.0 + g2_ref[...]) * dr2_ref[...]
        gb2_ref[...] += _colsum(df)
        dfb = df.astype(BF16)
        df_ref[...] = dfb
        for j in range(FF // fc):
            cs = slice(j * fc, (j + 1) * fc)
            da = _dot_nt(dfb, w2_ref[cs, :]) * (2.0 * rr_ref[:, cs].astype(F32))
            gb1_ref[:, cs] += _colsum(da)
            da_ref[:, cs] = da.astype(BF16)

    return pl.pallas_call(
        body, name="mlp_bwd_a", grid=(L // tm,),
        in_specs=[_tile(D, tm), _tile(FF, tm), _res((FF, D)), _acc((1, D))],
        out_specs=(_tile(D, tm), _tile(FF, tm), _acc((1, D)), _acc((1, FF))),
        out_shape=(_S((L, D), BF16), _S((L, FF), BF16), _S((1, D), F32), _S((1, FF), F32)),
        compiler_params=_params(("arbitrary",)))(dr2, rr, w2, g2)


def _mlp_bwd_b(da, w1, dr2, xhat1, rstd1, mix, w_out, vecs, alpha, W):
    L, FF = da.shape
    D = dr2.shape[1]
    MIX = w_out.shape[0]
    tm = _pick(L, (256, 128))

    def body(da_ref, w1_ref, dr2_ref, xh_ref, rs_ref, mix_ref, wo_ref, v_ref, dmix_ref, dh0_ref, dyn_ref, do_ref, acc_ref):
        @pl.when(pl.program_id(0) == 0)
        def _():
            acc_ref[...] = jnp.zeros_like(acc_ref)

        du2 = _dot_nt(da_ref[...], w1_ref[...])
        xhat1 = xh_ref[...]
        h1 = xhat1 * v_ref[0:1, :] + v_ref[1:2, :]
        acc_ref[0:1, :] += _colsum(du2 * h1)
        acc_ref[1:2, :] += _colsum(du2)
        dh1 = alpha * dr2_ref[...] + du2 * (1.0 + v_ref[2:3, :])
        acc_ref[2:3, :] += _colsum(dh1 * xhat1)
        acc_ref[3:4, :] += _colsum(dh1)
        dr1 = _ln_bwd(dh1 * v_ref[0:1, :], xhat1, rs_ref[...])
        acc_ref[4:5, :] += _colsum(dr1 * mix_ref[...])
        dh0_ref[...] = alpha * dr1
        dmix = ((1.0 + v_ref[3:4, :]) * dr1).astype(BF16)
        dmix_ref[...] = dmix
        dyn_ref[...] = _dot_nt(dmix, wo_ref[0:W, :])
        do_ref[...] = _dot_nt(dmix, wo_ref[W:MIX, :]).astype(BF16)

    return pl.pallas_call(
        body, name="mlp_bwd_b", grid=(L // tm,),
        in_specs=[_tile(FF, tm), _res((D, FF)), _tile(D, tm), _tile(D, tm), _tile(1, tm), _tile(D, tm), _res((MIX, D)),
                  _acc((8, D))],
        out_specs=(_tile(D, tm), _tile(D, tm), _tile(W, tm), _tile(MIX - W, tm), _acc((8, D))),
        out_shape=(_S((L, D), BF16), _S((L, D), F32), _S((L, W), F32), _S((L, MIX - W), BF16), _S((8, D), F32)),
        compiler_params=_params(("arbitrary",)))(da, w1, dr2, xhat1, rstd1, mix, w_out, vecs)


def _swa_bwd(q, kv, do, sinks, dims):
    L, AW = q.shape
    KV, KVW2 = dims["KV"], dims["KVW2"]
    T = CHUNK
    nb = L // T
    slopes = _alibi_slopes(dims["AH"])
    scale = HEAD_DIM ** -0.5

    def body(q_ref, kvc_ref, kvp_ref, do_ref, sink_ref, dq_ref, dkv_ref, dsink_ref, carry_ref,
             qg_ref, dog_ref, pt_ref, dst_ref):
        i = pl.program_id(0)

        @pl.when(i == 0)
        def _():
            carry_ref[...] = jnp.zeros_like(carry_ref)
            dsink_ref[...] = jnp.zeros_like(dsink_ref)

        @pl.when(i < nb)
        def _():
            c = lax.broadcasted_iota(jnp.int32, (2 * T, T), 0)
            r_ = lax.broadcasted_iota(jnp.int32, (2 * T, T), 1)
            dist_i = r_ + T - c
            valid = (dist_i >= 0) & (dist_i < CHUNK) & ((c >= T) | (i > 0))
            dist = dist_i.astype(F32)
            lane = lax.broadcasted_iota(jnp.int32, (1, LANE), 1)
            dsink = jnp.zeros((1, LANE), F32)
            dks, dvs = [], []
            for g in range(KV):
                kk, vv = _kv_heads(kvc_ref, kvp_ref, g, KV)
                for r in range(GQA):
                    hs = slice((g * GQA + r) * HEAD_DIM, (g * GQA + r + 1) * HEAD_DIM)
                    qg_ref[r * T:(r + 1) * T, :] = q_ref[:, hs]
                    dog_ref[r * T:(r + 1) * T, :] = do_ref[:, hs]
                st_all = _dot_nt(kk, qg_ref[...])
                dpt_all = _dot_nt(vv, dog_ref[...])
                for r in range(GQA):
                    h = g * GQA + r
                    cs = slice(r * T, (r + 1) * T)
                    p, p_sink = _attn_probs(st_all[:, cs], dist, valid, slopes[h], sink_ref[h], 0)
                    dp = dpt_all[:, cs]
                    delta = jnp.sum(p * dp, axis=0, keepdims=True)
                    pt_ref[:, cs] = p.astype(BF16)
                    dst_ref[:, cs] = (p * (dp - delta)).astype(BF16)
                    dsink = dsink + jnp.where(lane == h, -jnp.sum(p_sink * delta), 0.0)
                dst = dst_ref[...]
                dks.append(_dot(dst, qg_ref[...]) * scale)
                dvs.append(_dot(pt_ref[...], dog_ref[...]))
                dq_all = _dot_tn(dst, kk) * scale
                for r in range(GQA):
                    hs = slice((g * GQA + r) * HEAD_DIM, (g * GQA + r + 1) * HEAD_DIM)
                    dq_ref[:, hs] = dq_all[r * T:(r + 1) * T, :].astype(BF16)
            dkv = jnp.concatenate(dks + dvs, axis=1)
            dsink_ref[...] += dsink
            dkv_ref[...] = carry_ref[...] + dkv[0:T, :]
            carry_ref[...] = dkv[T:2 * T, :]

        @pl.when(i == nb)
        def _():
            dkv_ref[...] = carry_ref[...]

    last = nb - 1
    return pl.pallas_call(
        body, name="swa_bwd", grid=(nb + 1,),
        in_specs=[pl.BlockSpec((T, AW), lambda i: (jnp.minimum(i, last), 0)),
                  pl.BlockSpec((T, KVW2), lambda i: (jnp.minimum(i, last), 0)),
                  pl.BlockSpec((T, KVW2), lambda i: (jnp.clip(i - 1, 0, last), 0)),
                  pl.BlockSpec((T, AW), lambda i: (jnp.minimum(i, last), 0)),
                  pl.BlockSpec(memory_space=pltpu.SMEM)],
        out_specs=(pl.BlockSpec((T, AW), lambda i: (jnp.minimum(i, last), 0)),
                   pl.BlockSpec((T, KVW2), lambda i: (jnp.maximum(i - 1, 0), 0)), _acc((1, LANE))),
        out_shape=(_S((L, AW), BF16), _S((L, KVW2), F32), _S((1, LANE), F32)),
        scratch_shapes=[pltpu.VMEM((T, KVW2), F32), pltpu.VMEM((GQA * T, HEAD_DIM), BF16),
                        pltpu.VMEM((GQA * T, HEAD_DIM), BF16), pltpu.VMEM((2 * T, GQA * T), BF16),
                        pltpu.VMEM((2 * T, GQA * T), BF16)],
        compiler_params=_params(("arbitrary",)))(q, kv, kv, do, sinks)


def _ssd_bwd(dyn, y, z, xbc, dt_raw, sprev, cw, cb, dtb, alog, dsk, nw, dims):
    L, CD = xbc.shape
    W, H, G, N = dims["W"], dims["H"], SSD_GROUPS, SSD_STATE
    T = CHUNK
    R = H // G
    GW = W // G
    nc = L // T
    HP = H * HEAD_DIM

    def body(dyn_ref, y_ref, z_ref, xbc_ref, prev_ref, dt_ref, sp_ref, cw_ref, cb_ref, dtb_ref, alog_ref, dsk_ref, nw_ref,
             dz_ref, dpre_ref, ddt_ref, acc_ref, hacc_ref, ext_ref, ds_ref, dtx_ref, acx_ref, xb_ref, dyb_ref, r12_ref,
             dx_ref, rows_ref):
        i = pl.program_id(0)

        @pl.when(i == 0)
        def _():
            ds_ref[...] = jnp.zeros_like(ds_ref)
            acc_ref[...] = jnp.zeros_like(acc_ref)
            hacc_ref[...] = jnp.zeros_like(hacc_ref)

        act, pre = _conv_act(xbc_ref, prev_ref, cw_ref, cb_ref, ext_ref, i == nc - 1)
        xs = act[:, :W]
        dt_in = dt_ref[...] + dtb_ref[...]
        dt = _softplus(dt_in)
        a_neg = -jnp.exp(alog_ref[...])
        a = dt * a_neg
        low = _tri(T)
        upf = _tri(T, upper=True).astype(F32)
        acum = _dot_hi(low.astype(F32), a)
        acum_t = _dot_hi(a.T, upf)

        y = y_ref[...]
        zz = z_ref[...]
        sg = _sigmoid(zz)
        sz = zz * sg
        hh = y * sz
        dyn_v = dyn_ref[...]
        parts = []
        for g in range(G):
            gs = slice(g * GW, (g + 1) * GW)
            hg = hh[:, gs]
            hhat = hg * lax.rsqrt(_mean(hg * hg) + RMS_EPS)
            rg = lax.rsqrt(_mean(hg * hg) + RMS_EPS)
            acc_ref[0:1, gs] += _colsum(dyn_v[:, gs] * hhat)
            dhhat = dyn_v[:, gs] * nw_ref[:, gs]
            parts.append(rg * (dhhat - hhat * _mean(dhhat * hhat)))
        dhh = jnp.concatenate(parts, axis=1)
        dy = dhh * sz
        dz_ref[...] = (dhh * y * (sg * (1.0 + zz * (1.0 - sg)))).astype(BF16)
        acc_ref[1:2, :] += _colsum(dy * xs)
        dyb_ref[...] = dy.astype(BF16)

        _expand_heads(dtx_ref, dt, H)
        _expand_heads(acx_ref, acum, H)
        dtx = dtx_ref[...]
        acx = acx_ref[...]
        lastx = acx[T - 1:T, :]
        ex = jnp.exp(acx)
        decx = jnp.exp(lastx - acx)
        elx = jnp.exp(lastx)
        xd = xs * dtx
        xb_ref[...] = xd.astype(BF16)
        xdecb = (xd * decx).astype(BF16)
        dgb = (ex * dy).astype(BF16)
        rows_ref[...] = jnp.zeros_like(rows_ref)

        lane = lax.broadcasted_iota(jnp.int32, (T, LANE), 1)
        sub = lax.broadcasted_iota(jnp.int32, (T, LANE), 0)
        subr = lax.broadcasted_iota(jnp.int32, (LANE, T), 0)
        da_col = jnp.zeros((T, LANE), F32)
        da_row = jnp.zeros((LANE, T), F32)
        dbs, dcs = [], []
        for g in range(G):
            gs = slice(g * GW, (g + 1) * GW)
            bgb = act[:, W + g * N:W + (g + 1) * N].astype(BF16)
            cgb = act[:, W + G * N + g * N:W + G * N + (g + 1) * N].astype(BF16)
            stg = sp_ref[0, :, gs]
            stb = stg.astype(BF16)
            dsn = ds_ref[:, gs]
            dsnb = dsn.astype(BF16)
            gm = _dot(cgb, stb)
            dc = _dot_nt(dgb[:, gs], stb)
            dsp = _dot_tn(cgb, dgb[:, gs])
            dxs_ = decx[:, gs] * _dot(bgb, dsnb)
            db = _dot_nt(xdecb[:, gs], dsnb)
            xdg = xd[:, gs]
            r12_ref[:, gs] = dy[:, gs] * ex[:, gs] * gm - xdg * dxs_
            rows_ref[0:1, gs] = _colsum(dsn * stg) * elx[:, gs]
            rows_ref[1:2, gs] = _colsum(xdg * dxs_)
            ds_ref[:, gs] = dsp + dsn * elx[:, gs]
            cb_g = _dot_nt(cgb, bgb)
            dcb = jnp.zeros((T, T), F32)
            for r in range(R):
                h = g * R + r
                hs = slice(h * HEAD_DIM, (h + 1) * HEAD_DIM)
                lm = jnp.where(low, jnp.exp(acum[:, h:h + 1] - acum_t[h:h + 1, :]), 0.0)
                mm = cb_g * lm
                dyb = dyb_ref[:, hs]
                dm = _dot_nt(dyb, xb_ref[:, hs])
                dx_ref[:, hs] = dxs_[:, r * HEAD_DIM:(r + 1) * HEAD_DIM] + _dot_tn(mm.astype(BF16), dyb)
                dcb = dcb + dm * lm
                qm = dm * mm
                da_col = jnp.where(lane == h, jnp.sum(qm, axis=1, keepdims=True), da_col)
                da_row = jnp.where(subr == h, jnp.sum(qm, axis=0, keepdims=True), da_row)
            dcbb = dcb.astype(BF16)
            dcs.append(dc + _dot(dcbb, bgb))
            dbs.append(db + _dot_tn(dcbb, cgb))
        dx = dx_ref[...]
        rows = _head_reduce(rows_ref[...])
        dlast = rows[0:1, :] + rows[1:2, :]
        da_col = da_col + _head_reduce(r12_ref[...]) + jnp.where(sub == T - 1, dlast, 0.0)
        dacum = da_col - da_row.T
        da = _dot_hi(upf, dacum)
        ddt = _head_reduce(dx * xs) + da * a_neg
        hacc_ref[1:2, :] += _colsum(da * dt) * a_neg
        ddt_raw = ddt * _sigmoid(dt_in)
        hacc_ref[0:1, :] += _colsum(ddt_raw)
        ddt_ref[...] = ddt_raw
        dact = jnp.concatenate([dsk_ref[...] * dy + dx * dtx] + dbs + dcs, axis=1)
        spre = _sigmoid(pre)
        dpre_ref[...] = dact * (spre * (1.0 + pre * (1.0 - spre)))

        @pl.when(i == nc - 1)
        def _():
            ch = lax.broadcasted_iota(jnp.int32, (W, LANE), 0)
            lo = lax.broadcasted_iota(jnp.int32, (W, LANE), 1) * HEAD_DIM
            hacc_ref[2:3, :] = _dot_hi(acc_ref[1:2, :], ((ch >= lo) & (ch < lo + HEAD_DIM)).astype(F32))

    rev = lambda i: (nc - 1 - i, 0)
    return pl.pallas_call(
        body, name="ssd_bwd", grid=(nc,),
        in_specs=[pl.BlockSpec((T, W), rev), pl.BlockSpec((T, W), rev), pl.BlockSpec((T, W), rev), pl.BlockSpec((T, CD), rev),
                  pl.BlockSpec((HALO, CD), lambda i: (jnp.maximum((nc - 1 - i) * (T // HALO) - 1, 0), 0)),
                  pl.BlockSpec((T, LANE), rev), pl.BlockSpec((1, N, HP), lambda i: (nc - 1 - i, 0, 0)),
                  _acc((CONV_K, CD)), _acc((1, CD)), _acc((1, LANE)), _acc((1, LANE)), _acc((1, W)), _acc((1, W))],
        out_specs=(pl.BlockSpec((T, W), rev), pl.BlockSpec((T, CD), rev), pl.BlockSpec((T, LANE), rev), _acc((8, W)),
                   _acc((8, LANE))),
        out_shape=(_S((L, W), BF16), _S((L, CD), F32), _S((L, LANE), F32), _S((8, W), F32), _S((8, LANE), F32)),
        scratch_shapes=[pltpu.VMEM((T + HALO, CD), F32), pltpu.VMEM((N, HP), F32), pltpu.VMEM((T, W), F32),
                        pltpu.VMEM((T, W), F32), pltpu.VMEM((T, W), BF16), pltpu.VMEM((T, W), BF16), pltpu.VMEM((T, W), F32),
                        pltpu.VMEM((T, W), F32), pltpu.VMEM((8, W), F32)],
        compiler_params=_params(("arbitrary",)))(dyn, y, z, xbc, xbc, dt_raw, sprev, cw, cb, dtb, alog, dsk, nw)


def _conv_bwd(dpre, xbc, cw):
    L, CD = xbc.shape
    tm = _pick(L, (256, 128))
    nt = L // tm
    hb = tm // HALO

    def body(dp_ref, dn_ref, u_ref, up_ref, cw_ref, du_ref, acc_ref, extu_ref, extd_ref):
        i = pl.program_id(0)

        @pl.when(i == 0)
        def _():
            acc_ref[...] = jnp.zeros_like(acc_ref)

        dp = dp_ref[...]
        extu_ref[0:HALO, :] = jnp.where(i == 0, 0.0, up_ref[...])
        extu_ref[HALO:HALO + tm, :] = u_ref[...]
        extd_ref[0:tm, :] = dp
        extd_ref[tm:tm + HALO, :] = jnp.where(i == nt - 1, 0.0, dn_ref[...])
        du = cw_ref[CONV_K - 1:CONV_K, :] * dp
        acc_ref[CONV_K - 1:CONV_K, :] += _colsum(dp * u_ref[...])
        for k in range(CONV_K - 1):
            s = CONV_K - 1 - k
            du = du + cw_ref[k:k + 1, :] * extd_ref[s:s + tm, :]
            acc_ref[k:k + 1, :] += _colsum(dp * extu_ref[HALO - s:HALO - s + tm, :])
        acc_ref[CONV_K:CONV_K + 1, :] += _colsum(dp)
        du_ref[...] = du.astype(BF16)

    return pl.pallas_call(
        body, name="conv_bwd", grid=(nt,),
        in_specs=[_tile(CD, tm), pl.BlockSpec((HALO, CD), lambda i: (jnp.minimum((i + 1) * hb, nt * hb - 1), 0)),
                  _tile(CD, tm), pl.BlockSpec((HALO, CD), lambda i: (jnp.maximum(i * hb - 1, 0), 0)), _acc((CONV_K, CD))],
        out_specs=(_tile(CD, tm), _acc((8, CD))),
        out_shape=(_S((L, CD), BF16), _S((8, CD), F32)),
        scratch_shapes=[pltpu.VMEM((tm + HALO, CD), F32), pltpu.VMEM((tm + HALO, CD), F32)],
        compiler_params=_params(("arbitrary",)))(dpre, dpre, xbc, xbc, cw)


def _in_proj_bwd(dz, dxbc, dq, dkv, ddt, w_pad, xhat0, rstd0, dh0p, vecs, dims):
    L, D = xhat0.shape
    W, CD, AW, KVW2 = dims["W"], dims["CD"], dims["AW"], dims["KVW2"]
    NP = w_pad.shape[1]
    tm = _pick(L, (256, 128))
    o_xbc, o_q, o_kv, o_dt = W, W + CD, W + CD + AW, W + CD + AW + KVW2

    def body(dz_ref, dxbc_ref, dq_ref, dkv_ref, ddt_ref, w_ref, xh_ref, rs_ref, dh0_ref, v_ref, gx_ref, acc_ref):
        @pl.when(pl.program_id(0) == 0)
        def _():
            acc_ref[...] = jnp.zeros_like(acc_ref)

        du1 = _dot_nt(dz_ref[...], w_ref[:, 0:o_xbc])
        du1 = du1 + _dot_nt(dxbc_ref[...], w_ref[:, o_xbc:o_q])
        du1 = du1 + _dot_nt(dq_ref[...], w_ref[:, o_q:o_kv])
        du1 = du1 + _dot_nt(dkv_ref[...].astype(BF16), w_ref[:, o_kv:o_dt])
        du1 = du1 + _dot_nt(ddt_ref[...].astype(BF16), w_ref[:, o_dt:NP])
        xhat0 = xh_ref[...]
        h0 = xhat0 * v_ref[0:1, :] + v_ref[1:2, :]
        acc_ref[0:1, :] += _colsum(du1 * h0)
        acc_ref[1:2, :] += _colsum(du1)
        dh0 = dh0_ref[...] + du1 * (1.0 + v_ref[2:3, :])
        acc_ref[2:3, :] += _colsum(dh0 * xhat0)
        acc_ref[3:4, :] += _colsum(dh0)
        gx_ref[...] = _ln_bwd(dh0 * v_ref[0:1, :], xhat0, rs_ref[...])

    return pl.pallas_call(
        body, name="in_proj_bwd", grid=(L // tm,),
        in_specs=[_tile(W, tm), _tile(CD, tm), _tile(AW, tm), _tile(KVW2, tm), _tile(LANE, tm), _res((D, NP)),
                  _tile(D, tm), _tile(1, tm), _tile(D, tm), _acc((8, D))],
        out_specs=(_tile(D, tm), _acc((8, D))),
        out_shape=(_S((L, D), F32), _S((8, D), F32)),
        compiler_params=_params(("arbitrary",)))(dz, dxbc, dq, dkv, ddt, w_pad, xhat0, rstd0, dh0p, vecs)


_WEIGHTS = ['ln_in_g', 'ln_in_b', 'ada_w', 'ada_b', 'w_in', 'conv_w', 'conv_b', 'dt_bias', 'a_log', 'd_skip', 'ssd_norm_w',
            'attn_sinks', 'w_out', 'ln1_g', 'ln1_b', 'w_ff1', 'b_ff1', 'w_ff2', 'b_ff2', 'ln2_g', 'ln2_b']
_BIG = ('w_in', 'w_out', 'w_ff1', 'w_ff2')
_SMALL = ('ada_b', 'ln_in_g', 'ln_in_b', 'conv_b', 'dt_bias', 'a_log', 'd_skip', 'ssd_norm_w', 'attn_sinks', 'ln1_g', 'ln1_b',
          'b_ff1', 'b_ff2', 'ln2_g', 'ln2_b')


def _pad_lanes(v, n=None):
    v = v.reshape(1, -1)
    n = n or -(-v.shape[1] // LANE) * LANE
    return jnp.pad(v, ((0, 0), (0, n - v.shape[1])))


def _vec8(rows, D):
    rows = [r.reshape(1, D) for r in rows]
    return jnp.concatenate(rows + [jnp.zeros((8 - len(rows), D), F32)], axis=0)


def _pack(segs):
    flat, offs, sizes, o = [], [], [], 0
    for s in segs:
        p = _pad_lanes(s)
        flat.append(p)
        offs.append(o)
        sizes.append(s.size)
        o += p.shape[1]
    total = -(-o // (8 * LANE)) * (8 * LANE)
    if total > o:
        flat.append(jnp.zeros((1, total - o), F32))
    return jnp.concatenate(flat, axis=1).reshape(8, total // 8), offs, sizes


def kernel(x, c, ln_in_g, ln_in_b, ada_w, ada_b, w_in, conv_w, conv_b, dt_bias, a_log, d_skip, ssd_norm_w, attn_sinks, w_out, ln1_g, ln1_b, w_ff1, b_ff1, w_ff2, b_ff2, ln2_g, ln2_b, loss_target, m_ln_in_g, m_ln_in_b, m_ada_w, m_ada_b, m_w_in, m_conv_w, m_conv_b, m_dt_bias, m_a_log, m_d_skip, m_ssd_norm_w, m_attn_sinks, m_w_out, m_ln1_g, m_ln1_b, m_w_ff1, m_b_ff1, m_w_ff2, m_b_ff2, m_ln2_g, m_ln2_b, v_ln_in_g, v_ln_in_b, v_ada_w, v_ada_b, v_w_in, v_conv_w, v_conv_b, v_dt_bias, v_a_log, v_d_skip, v_ssd_norm_w, v_attn_sinks, v_w_out, v_ln1_g, v_ln1_b, v_w_ff1, v_b_ff1, v_w_ff2, v_b_ff2, v_ln2_g, v_ln2_b):
    wts = dict(ln_in_g=ln_in_g, ln_in_b=ln_in_b, ada_w=ada_w, ada_b=ada_b, w_in=w_in, conv_w=conv_w, conv_b=conv_b,
               dt_bias=dt_bias, a_log=a_log, d_skip=d_skip, ssd_norm_w=ssd_norm_w, attn_sinks=attn_sinks, w_out=w_out,
               ln1_g=ln1_g, ln1_b=ln1_b, w_ff1=w_ff1, b_ff1=b_ff1, w_ff2=w_ff2, b_ff2=b_ff2, ln2_g=ln2_g, ln2_b=ln2_b)
    ms = dict(ln_in_g=m_ln_in_g, ln_in_b=m_ln_in_b, ada_w=m_ada_w, ada_b=m_ada_b, w_in=m_w_in, conv_w=m_conv_w,
              conv_b=m_conv_b, dt_bias=m_dt_bias, a_log=m_a_log, d_skip=m_d_skip, ssd_norm_w=m_ssd_norm_w,
              attn_sinks=m_attn_sinks, w_out=m_w_out, ln1_g=m_ln1_g, ln1_b=m_ln1_b, w_ff1=m_w_ff1, b_ff1=m_b_ff1,
              w_ff2=m_w_ff2, b_ff2=m_b_ff2, ln2_g=m_ln2_g, ln2_b=m_ln2_b)
    vs = dict(ln_in_g=v_ln_in_g, ln_in_b=v_ln_in_b, ada_w=v_ada_w, ada_b=v_ada_b, w_in=v_w_in, conv_w=v_conv_w,
              conv_b=v_conv_b, dt_bias=v_dt_bias, a_log=v_a_log, d_skip=v_d_skip, ssd_norm_w=v_ssd_norm_w,
              attn_sinks=v_attn_sinks, w_out=v_w_out, ln1_g=v_ln1_g, ln1_b=v_ln1_b, w_ff1=v_w_ff1, b_ff1=v_b_ff1,
              w_ff2=v_w_ff2, b_ff2=v_b_ff2, ln2_g=v_ln2_g, ln2_b=v_ln2_b)

    L, D = x.shape[1], x.shape[2]
    depth = w_in.shape[0]
    assert depth == 1 and x.shape[0] == 1 and L % CHUNK == 0
    W = D
    H = W // HEAD_DIM
    CD = W + 2 * SSD_GROUPS * SSD_STATE
    AW = D
    AH = AW // HEAD_DIM
    KV = AH // GQA
    KVW2 = 2 * KV * HEAD_DIM
    PROJ = W + CD + H + AW + KVW2
    FF = w_ff1.shape[2] * N_DEV
    MIX = w_out.shape[1] * N_DEV
    assert w_in.shape[2] * N_DEV == PROJ and MIX == W + AW and H <= LANE and AH <= LANE
    dims = dict(W=W, H=H, CD=CD, AW=AW, AH=AH, KV=KV, KVW2=KVW2)
    alpha = (2.0 * depth) ** 0.25
    C6 = ada_w.shape[2]
    CW = conv_w.shape[2]

    ax, ay, ac = _my_pos()
    me = 4 * ax + 2 * ay + ac
    x2 = x.reshape(L, D)
    tgt = loss_target.reshape(L, D)
    r1 = lambda a: a.reshape(1, -1)

    ada_b_cols = lax.dynamic_slice(ada_b, (0, me * C6), (1, C6))
    cs_all, mod = _mod_fwd(c, ada_w[0], ada_b_cols)
    sh1, sc1, g1, sh2, sc2, g2 = [r1(t) for t in jnp.split(mod.reshape(-1), 6)]

    wg_in, cwg = _ag_weights([w_in[0].astype(BF16), conv_w[0]], cs_all)
    shards2 = [w_out[0].astype(BF16), w_ff1[0].astype(BF16), w_ff2[0].astype(BF16)]
    lands2 = [lax.dynamic_update_slice(lax.empty((N_DEV,) + s.shape, s.dtype), s[None], (me, 0, 0)) for s in shards2]
    ag_ss, ag_rs, ag_arr, ag_token = _split_start(shards2 + lands2, _plan_gather(3), cwg, "ag_ici_start")
    sh1 = sh1 + ag_token[0:1, 0:1]
    w_in_full = wg_in.transpose(1, 0, 2).reshape(D, PROJ)
    i1, i2, i3, i4 = W, W + CD, W + CD + H, W + CD + H + AW
    w_pad = jnp.concatenate([w_in_full[:, :i2], w_in_full[:, i3:], w_in_full[:, i2:i3], jnp.zeros((D, LANE - H), BF16)], axis=1)
    cw_full = cwg.transpose(1, 0, 2).reshape(CONV_K, CD)

    dtb = _pad_lanes(dt_bias, LANE)
    alog = _pad_lanes(a_log, LANE)
    dsk = jnp.repeat(d_skip.reshape(-1), HEAD_DIM).reshape(1, W)
    sinks = attn_sinks.reshape(-1)
    g_in, b_in = r1(ln_in_g), r1(ln_in_b)

    xhat0, rstd0, u1, z, xbc, q, kv, dt_raw = _ln_in_proj(x2, g_in, b_in, sc1, sh1, w_pad, dims)
    y, yn, sprev = _conv_ssd(xbc, dt_raw, z, cw_full, conv_b, dtb, alog, dsk, ssd_norm_w, dims)
    ag_arr = _split_wait(ag_ss, ag_rs, ag_arr, _plan_gather(3), yn, "ag_ici_wait")
    fw_ss, fw_rs, ag_land, fw_token = _split_start(ag_arr[3:], _plan_forward(3), yn, "ag_fwd_start")
    o = _swa_fwd(q, kv, sinks + fw_token[0, 0], dims)
    wg_out, wg_ff1, wg_ff2 = _split_wait(fw_ss, fw_rs, ag_land, _plan_forward(3), o, "ag_fwd_wait")
    w_out_full = wg_out.reshape(MIX, D)
    w1_full = wg_ff1.transpose(1, 0, 2).reshape(D, FF)
    w2_full = wg_ff2.reshape(FF, D)
    mix, xhat1, rstd1, u2 = _out_proj_ln1(yn, o, w_out_full, xhat0, _vec8([g_in, b_in, g1, ln1_g, ln1_b, sc2, sh2], D), alpha)
    rr, dr2, acc_f, loss_loc = _mlp_loss(u2, w1_full, w2_full, xhat1, tgt,
                                         _vec8([ln1_g, ln1_b, g2, ln2_g, ln2_b, b_ff2], D), b_ff1, alpha)

    df, da, gb2, gb1 = _mlp_bwd_a(dr2, rr, w2_full, g2)
    gw_ff2 = _matmul_tn(rr, df, "gw_ff2", square_a=True)
    gw_ff1 = _matmul_tn(u2, da, "gw_ff1")
    dmix, dh0p, dyn, do, acc_b = _mlp_bwd_b(da, w1_full, dr2, xhat1, rstd1, mix, w_out_full,
                                            _vec8([ln1_g, ln1_b, sc2, g1], D), alpha, W)
    gw_out = jnp.concatenate([_matmul_tn(yn, dmix, "gw_out_ssd"), _matmul_tn(o, dmix, "gw_out_attn")], axis=0)

    core = jnp.reshape(ac, (1,)).astype(jnp.int32)
    blocked1 = [gw_out.reshape(N_DEV, MIX // N_DEV, D), gw_ff1.reshape(D, N_DEV, FF // N_DEV).transpose(1, 0, 2),
                gw_ff2.reshape(N_DEV, FF // N_DEV, D)]
    pairs1 = [_pair_sum(b, r, core) for b, r in zip(blocked1, _rs_d2d(blocked1, "rs_d2d_1"))]
    lands1 = [lax.empty(p.shape, p.dtype) for p in pairs1]
    rs_ss, rs_rs, rs_arr, rs_token = _split_start(pairs1 + lands1, _plan_scatter(3), do, "rs_ici_start")
    sinks_b = sinks + rs_token[0, 0]

    dq, dkv, dsink = _swa_bwd(q, kv, do, sinks_b, dims)
    dz, dpre, ddt, acc_s, hacc = _ssd_bwd(dyn, y, z, xbc, dt_raw, sprev, cw_full, conv_b, dtb + rs_token[0:1, 0:1], alog, dsk,
                                          ssd_norm_w, dims)
    dxbc, acc_c = _conv_bwd(dpre, xbc, cw_full)
    grad_x, acc_i = _in_proj_bwd(dz, dxbc, dq, dkv, ddt, w_pad, xhat0, rstd0, dh0p, _vec8([g_in, b_in, sc1], D), dims)
    gz = _matmul_tn(u1, dz, "gw_in_z")
    gxbc = _matmul_tn(u1, dxbc, "gw_in_xbc")
    gq = _matmul_tn(u1, dq, "gw_in_q")
    gkv = _matmul_tn(u1, dkv, "gw_in_kv")
    gdt = _matmul_tn(u1, ddt, "gw_in_dt")
    gw_in = jnp.concatenate([gz, gxbc, gdt[:, :H], gq, gkv], axis=1)

    dmod = jnp.concatenate([acc_i[1], acc_i[0], acc_b[4], acc_b[1], acc_b[0], acc_f[2]])
    small_g = dict(ada_b=dmod, ln_in_g=acc_i[2], ln_in_b=acc_i[3], conv_b=acc_c[CONV_K], dt_bias=hacc[0, :H], a_log=hacc[1, :H],
                   d_skip=hacc[2, :H], ssd_norm_w=acc_s[0], attn_sinks=dsink[0, :AH], ln1_g=acc_b[2], ln1_b=acc_b[3],
                   b_ff1=gb1[0], b_ff2=gb2[0], ln2_g=acc_f[0], ln2_b=acc_f[1])
    segs = [small_g[n] for n in _SMALL] + [acc_c[:CONV_K].reshape(-1), loss_loc[0, :1]]
    pack, offs, sizes = _pack(segs)
    gathered, summed = _small_gather_sum(pack)
    gathered = gathered.reshape(N_DEV, -1)
    summed = summed.reshape(-1)
    seg = lambda k: summed[offs[k]:offs[k] + sizes[k]]
    grads = {n: seg(k).reshape(wts[n].shape) for k, n in enumerate(_SMALL)}
    gcw_full = seg(len(_SMALL)).reshape(CONV_K, CD)
    grads['conv_w'] = lax.dynamic_slice(gcw_full, (0, me * CW), (CONV_K, CW)).reshape(conv_w.shape)
    loss = seg(len(_SMALL) + 1)[0]

    names = list(_SMALL) + ['conv_w']
    pw, poffs, psizes = _pack([wts[n] for n in names])
    pg, _, _ = _pack([grads[n] for n in names])
    pm, _, _ = _pack([ms[n] for n in names])
    pv, _, _ = _pack([vs[n] for n in names])
    pd, pm2, pv2 = [t.reshape(-1) for t in _adamw(pw, pg, pm, pv)]
    deltas, new_m, new_v = {}, {}, {}
    for k, n in enumerate(names):
        sl = slice(poffs[k], poffs[k] + psizes[k])
        deltas[n], new_m[n], new_v[n] = (t[sl].reshape(wts[n].shape) for t in (pd, pm2, pv2))

    dmod_cols = lax.dynamic_slice(gathered, (0, offs[0] + me * C6), (N_DEV, C6))
    pad16 = lambda t: jnp.concatenate([t, jnp.zeros((16 - N_DEV,) + t.shape[1:], t.dtype)], axis=0)
    g_, d_, m_, v_ = _ada_grad_adamw(pad16(cs_all), pad16(dmod_cols), ada_w[0], m_ada_w[0], v_ada_w[0])
    grads['ada_w'], deltas['ada_w'], new_m['ada_w'], new_v['ada_w'] = (t[None] for t in (g_, d_, m_, v_))

    blocked2 = [gw_in.reshape(D, N_DEV, PROJ // N_DEV).transpose(1, 0, 2)]
    pairs2 = [_pair_sum(b, r, core) for b, r in zip(blocked2, _rs_d2d(blocked2, "rs_d2d_2"))]
    g_, d_, m_, v_ = _sum_adamw(_rs_ici(pairs2)[0], wts['w_in'][0], ms['w_in'][0], vs['w_in'][0])
    grads['w_in'], deltas['w_in'], new_m['w_in'], new_v['w_in'] = (t[None] for t in (g_, d_, m_, v_))
    rs_arr = _split_wait(rs_ss, rs_rs, rs_arr, _plan_scatter(3), g_, "rs_ici_wait")
    pairs1, rs_land = rs_arr[:3], rs_arr[3:]
    mychip = 2 * ax + ay
    chips = jnp.stack([(mychip + k) % N_CHIP for k in range(N_CHIP)]).astype(jnp.int32)
    for n, own, land in zip(('w_out', 'w_ff1', 'w_ff2'), pairs1, rs_land):
        g_, d_, m_, v_ = _sum_adamw_split(own, land, chips, wts[n][0], ms[n][0], vs[n][0])
        grads[n], deltas[n], new_m[n], new_v[n] = (t[None] for t in (g_, d_, m_, v_))

    return (loss, grad_x.reshape(x.shape), *[grads[n] for n in _WEIGHTS], *[deltas[n] for n in _WEIGHTS],
            *[new_m[n] for n in _WEIGHTS], *[new_v[n] for n in _WEIGHTS])
```

```python
import functools
import math

import numpy as np
import jax
import jax.numpy as jnp
from jax import lax
from jax.experimental import pallas as pl
from jax.experimental.pallas import tpu as pltpu

F32 = jnp.float32
BF16 = jnp.bfloat16
MESH = pl.DeviceIdType.MESH

N_DEV = 8
N_CHIP = 4
HEAD_DIM = 64
SSD_GROUPS = 2
SSD_STATE = 128
CHUNK = 128
CONV_K = 4
GQA = 8
LANE = 128
HALO = 8
LN_EPS = 1e-5
RMS_EPS = 1e-5
NEG = -1e30
ADAM_LR, ADAM_B1, ADAM_B2, ADAM_EPS, ADAM_WD, ADAM_STEP = 0.001, 0.9, 0.999, 1e-08, 0.01, 10
V7X_VMEM_BYTES = 64 * 1024 * 1024
VMEM_LIMIT = V7X_VMEM_BYTES - 8 * 1024 * 1024
HI = lax.Precision.HIGHEST


def _alibi_slopes(n):
    def pow2(m):
        start = 2.0 ** (-8.0 / m)
        return [start ** (i + 1) for i in range(m)]
    if math.log2(n).is_integer():
        s = pow2(n)
    else:
        c = 2 ** math.floor(math.log2(n))
        s = pow2(c) + pow2(2 * c)[0::2][: n - c]
    return [float(v) for v in np.array(s, dtype=np.float32)]


def _dot(a, b):
    return jnp.dot(a, b, preferred_element_type=F32)


def _dot_nt(a, b):
    return lax.dot_general(a, b, (((1,), (1,)), ((), ())), preferred_element_type=F32)


def _dot_tn(a, b):
    return lax.dot_general(a, b, (((0,), (0,)), ((), ())), preferred_element_type=F32)


def _dot_hi(a, b):
    return jnp.dot(a, b, precision=HI, preferred_element_type=F32)


def _sigmoid(x):
    return 1.0 / (1.0 + jnp.exp(-x))


def _softplus(x):
    return jnp.maximum(x, 0.0) + jnp.log(1.0 + jnp.exp(-jnp.abs(x)))


def _mean(x):
    return jnp.mean(x, axis=-1, keepdims=True)


def _ln_fwd(x):
    xc = x - _mean(x)
    rstd = lax.rsqrt(_mean(xc * xc) + LN_EPS)
    return xc * rstd, rstd


def _ln_bwd(dxhat, xhat, rstd):
    return rstd * (dxhat - _mean(dxhat) - xhat * _mean(dxhat * xhat))


def _colsum(x):
    return jnp.sum(x, axis=0, keepdims=True)


def _params(sem):
    return pltpu.CompilerParams(dimension_semantics=sem, vmem_limit_bytes=VMEM_LIMIT)


def _tile(i_map_cols, tm):
    return pl.BlockSpec((tm, i_map_cols), lambda i: (i, 0))


def _res(shape):
    return pl.BlockSpec(shape, lambda *_: (0,) * len(shape), pipeline_mode=pl.Buffered(1))


def _acc(shape):
    return pl.BlockSpec(shape, lambda *_: (0,) * len(shape))


def _S(shape, dtype):
    return jax.ShapeDtypeStruct(shape, dtype)


def _my_pos():
    return lax.axis_index("x"), lax.axis_index("y"), lax.axis_index("c")


def _peer(pos, k):
    x, y, c = pos
    px = 1 - x if k & 4 else x
    py = 1 - y if k & 2 else y
    pc = 1 - c if k & 1 else c
    return (px, py, pc)


def _lin(p):
    return 4 * p[0] + 2 * p[1] + p[2]


def _mod_fwd(c_loc, ada_w_loc, ada_b_cols):
    D = c_loc.shape[1]
    C6 = ada_w_loc.shape[1]

    def body(c_ref, w_ref, b_ref, cs_ref, mod_ref, call_ref, modp_ref, ssem, rsem):
        pos = _my_pos()
        me = _lin(pos)
        call_ref[me] = c_ref[...]
        sends = []
        for k in range(1, N_DEV):
            cp = pltpu.make_async_remote_copy(src_ref=c_ref, dst_ref=call_ref.at[me], send_sem=ssem.at[k - 1],
                                              recv_sem=rsem.at[k - 1], device_id=_peer(pos, k), device_id_type=MESH)
            cp.start()
            sends.append(cp)
        for k in range(1, N_DEV):
            src = _lin(_peer(pos, k))
            pltpu.make_async_remote_copy(src_ref=c_ref, dst_ref=call_ref.at[src], send_sem=ssem.at[k - 1],
                                         recv_sem=rsem.at[k - 1], device_id=pos, device_id_type=MESH).wait_recv()
        for cp in sends:
            cp.wait_send()
        call = jnp.concatenate([call_ref[b] for b in range(N_DEV)], axis=0)
        cs = call * _sigmoid(call)
        cs_ref[...] = cs
        modp = _dot(cs.astype(BF16), w_ref[...].astype(BF16)) + b_ref[...]
        for b in range(N_DEV):
            modp_ref[b] = modp[b:b + 1, :]
        mod_ref[me] = modp_ref[me]
        sends = []
        for k in range(1, N_DEV):
            peer = _peer(pos, k)
            cp = pltpu.make_async_remote_copy(src_ref=modp_ref.at[_lin(peer)], dst_ref=mod_ref.at[me],
                                              send_sem=ssem.at[N_DEV - 2 + k], recv_sem=rsem.at[N_DEV - 2 + k],
                                              device_id=peer, device_id_type=MESH)
            cp.start()
            sends.append(cp)
        for k in range(1, N_DEV):
            src = _lin(_peer(pos, k))
            pltpu.make_async_remote_copy(src_ref=modp_ref.at[src], dst_ref=mod_ref.at[src],
                                         send_sem=ssem.at[N_DEV - 2 + k], recv_sem=rsem.at[N_DEV - 2 + k],
                                         device_id=pos, device_id_type=MESH).wait_recv()
        for cp in sends:
            cp.wait_send()

    vm = pl.BlockSpec(memory_space=pltpu.VMEM)
    return pl.pallas_call(
        body, name="mod_fwd",
        out_shape=(_S((N_DEV, D), F32), _S((N_DEV, 1, C6), F32)),
        in_specs=[vm, vm, vm], out_specs=(vm, vm),
        scratch_shapes=[pltpu.VMEM((N_DEV, 1, D), F32), pltpu.VMEM((N_DEV, 1, C6), F32),
                        pltpu.SemaphoreType.DMA((2 * (N_DEV - 1),)), pltpu.SemaphoreType.DMA((2 * (N_DEV - 1),))],
        compiler_params=pltpu.CompilerParams(vmem_limit_bytes=VMEM_LIMIT),
    )(c_loc, ada_w_loc, ada_b_cols)


def _small_gather_sum(pack):
    P8 = pack.shape[1]

    def body(p_ref, gat_ref, sum_ref, ssem, rsem):
        pos = _my_pos()
        me = _lin(pos)
        gat_ref[me] = p_ref[...]
        sends = []
        for k in range(1, N_DEV):
            cp = pltpu.make_async_remote_copy(src_ref=p_ref, dst_ref=gat_ref.at[me], send_sem=ssem.at[k - 1],
                                              recv_sem=rsem.at[k - 1], device_id=_peer(pos, k), device_id_type=MESH)
            cp.start()
            sends.append(cp)
        for k in range(1, N_DEV):
            src = _lin(_peer(pos, k))
            pltpu.make_async_remote_copy(src_ref=p_ref, dst_ref=gat_ref.at[src], send_sem=ssem.at[k - 1],
                                         recv_sem=rsem.at[k - 1], device_id=pos, device_id_type=MESH).wait_recv()
        for cp in sends:
            cp.wait_send()
        acc = gat_ref[0]
        for j in range(1, N_DEV):
            acc = acc + gat_ref[j]
        sum_ref[...] = acc

    vm = pl.BlockSpec(memory_space=pltpu.VMEM)
    return pl.pallas_call(
        body, name="small_gather_sum",
        out_shape=(_S((N_DEV, 8, P8), F32), _S((8, P8), F32)),
        in_specs=[vm], out_specs=(vm, vm),
        scratch_shapes=[pltpu.SemaphoreType.DMA((N_DEV - 1,)), pltpu.SemaphoreType.DMA((N_DEV - 1,))],
        compiler_params=pltpu.CompilerParams(vmem_limit_bytes=VMEM_LIMIT),
    )(pack)


def _ag_weights(shards, after):
    n = len(shards)

    def body(*refs):
        ins, outs = refs[:n], refs[n + 1:2 * n + 1]
        ssem, rsem, lsem = refs[2 * n + 1:]
        x, y, c = pos = _my_pos()
        me = _lin(pos)
        sib = (x, y, 1 - c)
        chips = [(1 - x, y), (x, 1 - y), (1 - x, 1 - y)]

        def copy(a, k, block, to, src=None):
            return pltpu.make_async_remote_copy(
                src_ref=outs[a].at[block] if src is None else src, dst_ref=outs[a].at[block],
                send_sem=ssem.at[a * 7 + k], recv_sem=rsem.at[a * 7 + k], device_id=to, device_id_type=MESH)

        local = [pltpu.make_async_copy(ins[a], outs[a].at[me], lsem.at[a]) for a in range(n)]
        for cp in local:
            cp.start()
        first = []
        for a in range(n):
            first.append(copy(a, 0, me, sib, src=ins[a]))
            first += [copy(a, 1 + j, me, (*chip, c), src=ins[a]) for j, chip in enumerate(chips)]
        for cp in first:
            cp.start()
        passed = []
        for a in range(n):
            for j, chip in enumerate(chips):
                blk = _lin((*chip, c))
                copy(a, 1 + j, blk, pos).wait_recv()
                cp = copy(a, 4 + j, blk, sib)
                cp.start()
                passed.append(cp)
        for a in range(n):
            copy(a, 0, _lin(sib), pos).wait_recv()
            for j, chip in enumerate(chips):
                copy(a, 4 + j, _lin((*chip, 1 - c)), pos).wait_recv()
        for cp in first + passed:
            cp.wait_send()
        for cp in local:
            cp.wait()

    hbm = pl.BlockSpec(memory_space=pl.ANY)
    return pl.pallas_call(
        body, name="ag_weights",
        out_shape=tuple(_S((N_DEV,) + s.shape, s.dtype) for s in shards),
        in_specs=[hbm] * (n + 1), out_specs=tuple([hbm] * n),
        scratch_shapes=[pltpu.SemaphoreType.DMA((7 * n,)), pltpu.SemaphoreType.DMA((7 * n,)),
                        pltpu.SemaphoreType.DMA((n,))],
    )(*shards, after)


def _rs_d2d(blocked, name):
    n = len(blocked)

    def body(*refs):
        ins, outs = refs[:n], refs[n:2 * n]
        ssem, rsem = refs[2 * n:]
        x, y, c = pos = _my_pos()
        sib = (x, y, 1 - c)
        cps = []
        for a in range(n):
            for j in range(N_CHIP):
                cp = pltpu.make_async_remote_copy(
                    src_ref=ins[a].at[2 * j + (1 - c)], dst_ref=outs[a].at[j], send_sem=ssem.at[a * N_CHIP + j],
                    recv_sem=rsem.at[a * N_CHIP + j], device_id=sib, device_id_type=MESH)
                cp.start()
                cps.append(cp)
        for cp in cps:
            cp.wait_recv()
        for cp in cps:
            cp.wait_send()

    hbm = pl.BlockSpec(memory_space=pl.ANY)
    return pl.pallas_call(
        body, name=name,
        out_shape=tuple(_S((N_CHIP,) + b.shape[1:], b.dtype) for b in blocked),
        in_specs=[hbm] * n, out_specs=tuple([hbm] * n),
        scratch_shapes=[pltpu.SemaphoreType.DMA((N_CHIP * n,)), pltpu.SemaphoreType.DMA((N_CHIP * n,))],
    )(*blocked)


_HBM = pl.BlockSpec(memory_space=pltpu.HBM)
_SEM = pl.BlockSpec(memory_space=pltpu.SEMAPHORE)
_ANY = pl.BlockSpec(memory_space=pl.ANY)
_EFFECT = pltpu.SideEffectType.DATAFLOW_SIDE_EFFECTING


def _in_hbm(a):
    return pltpu.with_memory_space_constraint(a, pltpu.HBM)


def _plan_gather(n):
    def copies(pos):
        x, y, c = pos
        out = []
        for a in range(n):
            for dev in [(x, y, 1 - c)] + [(*_peer(pos, 2 * k)[:2], c) for k in range(1, N_CHIP)]:
                out.append((a, None, n + a, _lin(pos), dev, _lin(dev)))
        return out
    return copies


def _plan_forward(n):
    def copies(pos):
        x, y, c = pos
        out = []
        for a in range(n):
            for k in range(1, N_CHIP):
                tx, ty, _ = _peer(pos, 2 * k)
                out.append((a, _lin((tx, ty, c)), a, _lin((tx, ty, c)), (x, y, 1 - c), _lin((tx, ty, 1 - c))))
        return out
    return copies


def _plan_d2d(n):
    def copies(pos):
        x, y, c = pos
        return [(a, 2 * j + (1 - c), n + a, j, (x, y, 1 - c), j) for a in range(n) for j in range(N_CHIP)]
    return copies


def _plan_scatter(n):
    def copies(pos):
        x, y, c = pos
        out = []
        for a in range(n):
            for k in range(1, N_CHIP):
                tx, ty, _ = _peer(pos, 2 * k)
                out.append((a, 2 * tx + ty, n + a, 2 * x + y, (tx, ty, c), 2 * tx + ty))
        return out
    return copies


def _split_copy(refs, cp, ssem, rsem, i, arrival):
    si, s_slot, di, d_slot, dev, a_slot = cp
    return pltpu.make_async_remote_copy(
        src_ref=refs[si] if s_slot is None else refs[si].at[s_slot], dst_ref=refs[di].at[a_slot if arrival else d_slot],
        send_sem=ssem.at[i], recv_sem=rsem.at[i], device_id=dev, device_id_type=MESH)


def _split_start(arrays, copies, after, name):
    n = len(arrays)
    n_cp = len(copies((0, 0, 0)))

    def body(*refs):
        ssem, rsem, token = refs[n + 1], refs[n + 2], refs[-1]
        for i, cp in enumerate(copies(_my_pos())):
            _split_copy(refs, cp, ssem, rsem, i, False).start()
        token[...] = jnp.zeros_like(token)

    res = pl.pallas_call(
        body, name=name,
        out_shape=(pltpu.SemaphoreType.DMA((n_cp,)), pltpu.SemaphoreType.DMA((n_cp,)),
                   *[pltpu.HBM(a.shape, a.dtype) for a in arrays], _S((8, LANE), F32)),
        in_specs=[_HBM] * n + [_ANY],
        out_specs=(_SEM, _SEM, *[_HBM] * n, pl.BlockSpec(memory_space=pltpu.VMEM)),
        input_output_aliases={a: 2 + a for a in range(n)},
        compiler_params=pltpu.CompilerParams(has_side_effects=_EFFECT),
    )(*[_in_hbm(a) for a in arrays], after)
    return res[0], res[1], list(res[2:2 + n]), res[-1]


def _split_wait(ssem, rsem, arrays, copies, after, name):
    n = len(arrays)

    def body(*refs):
        for i, cp in enumerate(copies(_my_pos())):
            d = _split_copy(refs, cp, refs[n], refs[n + 1], i, True)
            d.wait_send()
            d.wait_recv()

    res = pl.pallas_call(
        body, name=name,
        out_shape=tuple(pltpu.HBM(a.shape, a.dtype) for a in arrays),
        in_specs=[_HBM] * n + [_SEM, _SEM, _ANY], out_specs=tuple([_HBM] * n),
        input_output_aliases={a: a for a in range(n)},
        compiler_params=pltpu.CompilerParams(has_side_effects=_EFFECT),
    )(*arrays, ssem, rsem, after)
    return list(res)


def _row_tile(R, itemsize_rows=16, cap=256):
    t = min(R, cap)
    while R % t or t % itemsize_rows:
        t -= itemsize_rows
    return t


def _pair_sum(blocked, recv, core):
    _, R, C = blocked.shape
    tr = _row_tile(R)

    def body(ids_ref, a_ref, b_ref, o_ref):
        del ids_ref
        o_ref[...] = (a_ref[...] + b_ref[...]).astype(BF16)

    gs = pltpu.PrefetchScalarGridSpec(
        num_scalar_prefetch=1, grid=(N_CHIP, R // tr),
        in_specs=[pl.BlockSpec((1, tr, C), lambda j, r, ids: (2 * j + ids[0], r, 0)),
                  pl.BlockSpec((1, tr, C), lambda j, r, ids: (j, r, 0))],
        out_specs=pl.BlockSpec((1, tr, C), lambda j, r, ids: (j, r, 0)))
    return pl.pallas_call(body, name="pair_sum", grid_spec=gs, out_shape=_S((N_CHIP, R, C), BF16),
                          compiler_params=_params(("arbitrary", "arbitrary")))(core, blocked, recv)


def _adamw_math(w, g, m, v):
    m2 = ADAM_B1 * m + (1.0 - ADAM_B1) * g
    v2 = ADAM_B2 * v + (1.0 - ADAM_B2) * (g * g)
    m_hat = m2 / (1.0 - ADAM_B1 ** ADAM_STEP)
    v_hat = v2 / (1.0 - ADAM_B2 ** ADAM_STEP)
    delta = -ADAM_LR * (m_hat / (jnp.sqrt(v_hat) + ADAM_EPS) + ADAM_WD * w)
    return delta, m2, v2


def _sum_adamw_split(pairs, land, chips, w, m, v, transposed=False):
    R, C = w.shape
    tr = _row_tile(R, 128 if transposed else 16)

    def body(ids_ref, own_ref, p1_ref, p2_ref, p3_ref, w_ref, m_ref, v_ref, g_ref, d_ref, m2_ref, v2_ref):
        del ids_ref
        g = own_ref[0].astype(F32) + p1_ref[0].astype(F32) + p2_ref[0].astype(F32) + p3_ref[0].astype(F32)
        if transposed:
            g = g.T
        g_ref[...] = g
        d_ref[...], m2_ref[...], v2_ref[...] = _adamw_math(w_ref[...], g, m_ref[...], v_ref[...])

    t = pl.BlockSpec((tr, C), lambda r, ids: (r, 0))
    if transposed:
        slot = lambda k: pl.BlockSpec((1, C, tr), lambda r, ids: (ids[k], 0, r))
    else:
        slot = lambda k: pl.BlockSpec((1, tr, C), lambda r, ids: (ids[k], r, 0))
    gs = pltpu.PrefetchScalarGridSpec(num_scalar_prefetch=1, grid=(R // tr,),
                                      in_specs=[slot(0), slot(1), slot(2), slot(3), t, t, t], out_specs=(t, t, t, t))
    return pl.pallas_call(body, name="sum_adamw_split", grid_spec=gs, out_shape=tuple(_S((R, C), F32) for _ in range(4)),
                          compiler_params=_params(("arbitrary",)))(chips, pairs, land, land, land, w, m, v)


def _adamw(w, g, m, v):
    R, C = w.shape
    tr = _row_tile(R, 8)

    def body(w_ref, g_ref, m_ref, v_ref, d_ref, m2_ref, v2_ref):
        d_ref[...], m2_ref[...], v2_ref[...] = _adamw_math(w_ref[...], g_ref[...], m_ref[...], v_ref[...])

    t = pl.BlockSpec((tr, C), lambda r: (r, 0))
    return pl.pallas_call(body, name="adamw", grid=(R // tr,), in_specs=[t, t, t, t], out_specs=(t, t, t),
                          out_shape=tuple(_S((R, C), F32) for _ in range(3)),
                          compiler_params=_params(("arbitrary",)))(w, g, m, v)


def _ada_grad_adamw(cs16, dmod16, w, m, v):
    D, C6 = w.shape
    tr = _row_tile(D, 8, 256)

    def body(cs_ref, dm_ref, w_ref, m_ref, v_ref, g_ref, d_ref, m2_ref, v2_ref):
        g = _dot_tn(cs_ref[...].astype(BF16), dm_ref[...].astype(BF16))
        g_ref[...] = g
        d_ref[...], m2_ref[...], v2_ref[...] = _adamw_math(w_ref[...], g, m_ref[...], v_ref[...])

    t = pl.BlockSpec((tr, C6), lambda r: (r, 0))
    return pl.pallas_call(
        body, name="ada_grad_adamw", grid=(D // tr,),
        in_specs=[pl.BlockSpec((16, tr), lambda r: (0, r)), _acc((16, C6)), t, t, t], out_specs=(t, t, t, t),
        out_shape=tuple(_S((D, C6), F32) for _ in range(4)), compiler_params=_params(("arbitrary",)))(cs16, dmod16, w, m, v)


def _pick(n, cands):
    for c in cands:
        if n % c == 0:
            return c
    return n


def _matmul_tn(a, b, name, square_a=False):
    L, K = a.shape
    N = b.shape[1]
    bk = _pick(K, (1024, 512, 256, 128))
    bn = _pick(N, (1024, 768, 512, 256, 128))
    tl = _pick(L, (1024, 512, 256, 128))
    n_l = L // tl

    def body(a_ref, b_ref, o_ref):
        @pl.when(pl.program_id(2) == 0)
        def _():
            o_ref[...] = jnp.zeros_like(o_ref)
        av = a_ref[...]
        if square_a:
            av = av.astype(F32)
            av = av * av
        o_ref[...] += _dot_tn(av.astype(BF16), b_ref[...].astype(BF16))

    return pl.pallas_call(
        body, name=name, grid=(K // bk, N // bn, n_l),
        in_specs=[pl.BlockSpec((tl, bk), lambda k, n, l: (l, k)), pl.BlockSpec((tl, bn), lambda k, n, l: (l, n))],
        out_specs=pl.BlockSpec((bk, bn), lambda k, n, l: (k, n)), out_shape=_S((K, N), F32),
        compiler_params=_params(("arbitrary", "arbitrary", "arbitrary")))(a, b)


def _ln_in_proj(x, g, b, sc, sh, w_pad, dims):
    L, D = x.shape
    W, CD, AW, KVW2 = dims["W"], dims["CD"], dims["AW"], dims["KVW2"]
    NP = w_pad.shape[1]
    tm = _pick(L, (256, 128))
    o_z, o_xbc, o_q, o_kv, o_dt = 0, W, W + CD, W + CD + AW, W + CD + AW + KVW2

    def body(x_ref, g_ref, b_ref, sc_ref, sh_ref, w_ref, xhat_ref, rstd_ref, u1_ref, z_ref, xbc_ref, q_ref, kv_ref, dt_ref):
        xhat, rstd = _ln_fwd(x_ref[...])
        xhat_ref[...] = xhat
        rstd_ref[...] = rstd
        h0 = xhat * g_ref[...] + b_ref[...]
        u1 = (h0 * (1.0 + sc_ref[...]) + sh_ref[...]).astype(BF16)
        u1_ref[...] = u1
        z_ref[...] = _dot(u1, w_ref[:, o_z:o_xbc])
        xbc_ref[...] = _dot(u1, w_ref[:, o_xbc:o_q])
        q_ref[...] = _dot(u1, w_ref[:, o_q:o_kv]).astype(BF16)
        kv_ref[...] = _dot(u1, w_ref[:, o_kv:o_dt]).astype(BF16)
        dt_ref[...] = _dot(u1, w_ref[:, o_dt:NP])

    v = _acc((1, D))
    return pl.pallas_call(
        body, name="ln_in_proj", grid=(L // tm,),
        in_specs=[_tile(D, tm), v, v, v, v, _res((D, NP))],
        out_specs=(_tile(D, tm), _tile(1, tm), _tile(D, tm), _tile(W, tm), _tile(CD, tm), _tile(AW, tm),
                   _tile(KVW2, tm), _tile(LANE, tm)),
        out_shape=(_S((L, D), F32), _S((L, 1), F32), _S((L, D), BF16), _S((L, W), F32), _S((L, CD), F32),
                   _S((L, AW), BF16), _S((L, KVW2), BF16), _S((L, LANE), F32)),
        compiler_params=_params(("arbitrary",)))(x, g, b, sc, sh, w_pad)


def _conv_act(cur_ref, prev_ref, cw_ref, cb_ref, ext_ref, first):
    T = cur_ref.shape[0]
    ext_ref[0:HALO, :] = jnp.where(first, 0.0, prev_ref[...])
    ext_ref[HALO:HALO + T, :] = cur_ref[...]
    pre = cb_ref[...] + cw_ref[0:1, :] * ext_ref[HALO - 3:HALO - 3 + T, :]
    for k in range(1, CONV_K):
        pre = pre + cw_ref[k:k + 1, :] * ext_ref[HALO - 3 + k:HALO - 3 + k + T, :]
    return pre * _sigmoid(pre), pre


def _tri(T, upper=False):
    r = lax.broadcasted_iota(jnp.int32, (T, T), 0)
    c = lax.broadcasted_iota(jnp.int32, (T, T), 1)
    return (r <= c) if upper else (r >= c)


def _expand_heads(dst_ref, v, n_heads):
    for h in range(n_heads):
        dst_ref[:, h * HEAD_DIM:(h + 1) * HEAD_DIM] = jnp.broadcast_to(v[:, h:h + 1], (v.shape[0], HEAD_DIM))


def _head_reduce(v):
    wdt = v.shape[1]
    ch = lax.broadcasted_iota(jnp.int32, (wdt, LANE), 0)
    lo = lax.broadcasted_iota(jnp.int32, (wdt, LANE), 1) * HEAD_DIM
    onehot = ((ch >= lo) & (ch < lo + HEAD_DIM)).astype(BF16)
    hi = v.astype(BF16)
    rest = (v - hi.astype(F32)).astype(BF16)
    return _dot(hi, onehot) + _dot(rest, onehot)


def _conv_ssd(xbc, dt_raw, z, cw, cb, dtb, alog, dsk, nw, dims):
    L, CD = xbc.shape
    W, H, G, N = dims["W"], dims["H"], SSD_GROUPS, SSD_STATE
    T = CHUNK
    R = H // G
    GW = W // G
    nc = L // T
    HP = H * HEAD_DIM

    def body(xbc_ref, prev_ref, dt_ref, z_ref, cw_ref, cb_ref, dtb_ref, alog_ref, dsk_ref, nw_ref,
             y_ref, yn_ref, sp_ref, ext_ref, s_ref, ybuf_ref, dtx_ref, acx_ref, xb_ref):
        i = pl.program_id(0)

        @pl.when(i == 0)
        def _():
            s_ref[...] = jnp.zeros_like(s_ref)

        act, _ = _conv_act(xbc_ref, prev_ref, cw_ref, cb_ref, ext_ref, i == 0)
        xs = act[:, :W]
        dt = _softplus(dt_ref[...] + dtb_ref[...])
        a = dt * (-jnp.exp(alog_ref[...]))
        low = _tri(T)
        acum = _dot_hi(low.astype(F32), a)
        acum_t = _dot_hi(a.T, _tri(T, upper=True).astype(F32))
        _expand_heads(dtx_ref, dt, H)
        _expand_heads(acx_ref, acum, H)
        acx = acx_ref[...]
        lastx = acx[T - 1:T, :]
        xd = xs * dtx_ref[...]
        xb_ref[...] = xd.astype(BF16)
        xdb = (xd * jnp.exp(lastx - acx)).astype(BF16)
        ex = jnp.exp(acx)
        elx = jnp.exp(lastx)
        for g in range(G):
            gs = slice(g * GW, (g + 1) * GW)
            bgb = act[:, W + g * N:W + (g + 1) * N].astype(BF16)
            cgb = act[:, W + G * N + g * N:W + G * N + (g + 1) * N].astype(BF16)
            stg = s_ref[:, gs]
            sp_ref[0, :, gs] = stg
            yoff = ex[:, gs] * _dot(cgb, stg.astype(BF16))
            s_ref[:, gs] = stg * elx[:, gs] + _dot_tn(bgb, xdb[:, gs])
            cb_g = _dot_nt(cgb, bgb)
            for r in range(R):
                h = g * R + r
                hs = slice(h * HEAD_DIM, (h + 1) * HEAD_DIM)
                lm = jnp.where(low, jnp.exp(acum[:, h:h + 1] - acum_t[h:h + 1, :]), 0.0)
                ybuf_ref[:, hs] = _dot((cb_g * lm).astype(BF16), xb_ref[:, hs]) + yoff[:, r * HEAD_DIM:(r + 1) * HEAD_DIM]
        y = ybuf_ref[...] + dsk_ref[...] * xs
        y_ref[...] = y
        zz = z_ref[...]
        hh = y * (zz * _sigmoid(zz))
        for g in range(G):
            gs = slice(g * GW, (g + 1) * GW)
            hg = hh[:, gs]
            yn_ref[:, gs] = (hg * lax.rsqrt(_mean(hg * hg) + RMS_EPS) * nw_ref[:, gs]).astype(BF16)

    return pl.pallas_call(
        body, name="conv_ssd", grid=(nc,),
        in_specs=[_tile(CD, T), pl.BlockSpec((HALO, CD), lambda i: (jnp.maximum(i * (T // HALO) - 1, 0), 0)),
                  _tile(LANE, T), _tile(W, T), _acc((CONV_K, CD)), _acc((1, CD)), _acc((1, LANE)), _acc((1, LANE)),
                  _acc((1, W)), _acc((1, W))],
        out_specs=(_tile(W, T), _tile(W, T), pl.BlockSpec((1, N, HP), lambda i: (i, 0, 0))),
        out_shape=(_S((L, W), F32), _S((L, W), BF16), _S((nc, N, HP), F32)),
        scratch_shapes=[pltpu.VMEM((T + HALO, CD), F32), pltpu.VMEM((N, HP), F32), pltpu.VMEM((T, W), F32),
                        pltpu.VMEM((T, W), F32), pltpu.VMEM((T, W), F32), pltpu.VMEM((T, W), BF16)],
        compiler_params=_params(("arbitrary",)))(xbc, xbc, dt_raw, z, cw, cb, dtb, alog, dsk, nw)


def _attn_mask(T, i):
    r = lax.broadcasted_iota(jnp.int32, (T, 2 * T), 0)
    c = lax.broadcasted_iota(jnp.int32, (T, 2 * T), 1)
    dist = r + T - c
    valid = (dist >= 0) & (dist < CHUNK) & ((c >= T) | (i > 0))
    return dist.astype(F32), valid


def _attn_probs(s_raw, dist, valid, slope, sink, axis):
    s = s_raw * (HEAD_DIM ** -0.5) - slope * dist
    s = jnp.where(valid, s, NEG)
    m = jnp.maximum(jnp.max(s, axis=axis, keepdims=True), sink)
    p = jnp.exp(s - m)
    e_sink = jnp.exp(sink - m)
    inv = 1.0 / (jnp.sum(p, axis=axis, keepdims=True) + e_sink)
    return p * inv, e_sink * inv


def _kv_heads(kvc_ref, kvp_ref, g, n_kv):
    ks = slice(g * HEAD_DIM, (g + 1) * HEAD_DIM)
    vs = slice((n_kv + g) * HEAD_DIM, (n_kv + g + 1) * HEAD_DIM)
    kk = jnp.concatenate([kvp_ref[:, ks], kvc_ref[:, ks]], axis=0)
    vv = jnp.concatenate([kvp_ref[:, vs], kvc_ref[:, vs]], axis=0)
    return kk, vv


def _swa_fwd(q, kv, sinks, dims):
    L, AW = q.shape
    KV, KVW2 = dims["KV"], dims["KVW2"]
    T = CHUNK
    nb = L // T
    slopes = _alibi_slopes(dims["AH"])

    def body(q_ref, kvc_ref, kvp_ref, sink_ref, o_ref, qg_ref, p_ref):
        i = pl.program_id(0)
        dist, valid = _attn_mask(T, i)
        for g in range(KV):
            kk, vv = _kv_heads(kvc_ref, kvp_ref, g, KV)
            for r in range(GQA):
                h = g * GQA + r
                qg_ref[r * T:(r + 1) * T, :] = q_ref[:, h * HEAD_DIM:(h + 1) * HEAD_DIM]
            s_all = _dot_nt(qg_ref[...], kk)
            for r in range(GQA):
                h = g * GQA + r
                p, _ = _attn_probs(s_all[r * T:(r + 1) * T, :], dist, valid, slopes[h], sink_ref[h], -1)
                p_ref[r * T:(r + 1) * T, :] = p.astype(BF16)
            o_all = _dot(p_ref[...], vv)
            for r in range(GQA):
                h = g * GQA + r
                o_ref[:, h * HEAD_DIM:(h + 1) * HEAD_DIM] = o_all[r * T:(r + 1) * T, :].astype(BF16)

    return pl.pallas_call(
        body, name="swa_fwd", grid=(nb,),
        in_specs=[_tile(AW, T), _tile(KVW2, T), pl.BlockSpec((T, KVW2), lambda i: (jnp.maximum(i - 1, 0), 0)),
                  pl.BlockSpec(memory_space=pltpu.SMEM)],
        out_specs=_tile(AW, T), out_shape=_S((L, AW), BF16),
        scratch_shapes=[pltpu.VMEM((GQA * T, HEAD_DIM), BF16), pltpu.VMEM((GQA * T, 2 * T), BF16)],
        compiler_params=_params(("arbitrary",)))(q, kv, kv, sinks)


def _out_proj_ln1(yn, o, w_out, xhat0, vecs, alpha):
    L, W = yn.shape
    D = xhat0.shape[1]
    MIX = w_out.shape[0]
    tm = _pick(L, (256, 128))

    def body(yn_ref, o_ref, w_ref, xh_ref, v_ref, mix_ref, xhat1_ref, rstd1_ref, u2_ref):
        mix = _dot(yn_ref[...], w_ref[0:W, :]) + _dot(o_ref[...], w_ref[W:MIX, :])
        mix_ref[...] = mix
        h0 = xh_ref[...] * v_ref[0:1, :] + v_ref[1:2, :]
        xhat1, rstd1 = _ln_fwd(alpha * h0 + (1.0 + v_ref[2:3, :]) * mix)
        xhat1_ref[...] = xhat1
        rstd1_ref[...] = rstd1
        h1 = xhat1 * v_ref[3:4, :] + v_ref[4:5, :]
        u2_ref[...] = (h1 * (1.0 + v_ref[5:6, :]) + v_ref[6:7, :]).astype(BF16)

    return pl.pallas_call(
        body, name="out_proj_ln1", grid=(L // tm,),
        in_specs=[_tile(W, tm), _tile(MIX - W, tm), _res((MIX, D)), _tile(D, tm), _acc((8, D))],
        out_specs=(_tile(D, tm), _tile(D, tm), _tile(1, tm), _tile(D, tm)),
        out_shape=(_S((L, D), F32), _S((L, D), F32), _S((L, 1), F32), _S((L, D), BF16)),
        compiler_params=_params(("arbitrary",)))(yn, o, w_out, xhat0, vecs)


def _mlp_loss(u2, w1, w2, xhat1, tgt, vecs, b1, alpha):
    L, D = xhat1.shape
    FF = w1.shape[1]
    tm = _pick(L, (256, 128))
    fc = _pick(FF, (512, 256, 128))

    def body(u2_ref, w1_ref, w2_ref, xh_ref, t_ref, v_ref, b1_ref, rr_ref, dr2_ref, acc_ref, loss_ref):
        @pl.when(pl.program_id(0) == 0)
        def _():
            acc_ref[...] = jnp.zeros_like(acc_ref)
            loss_ref[...] = jnp.zeros_like(loss_ref)

        u2 = u2_ref[...]
        f = jnp.zeros((tm, D), F32) + v_ref[5:6, :]
        for j in range(FF // fc):
            cs = slice(j * fc, (j + 1) * fc)
            rr = jnp.maximum(_dot(u2, w1_ref[:, cs]) + b1_ref[:, cs], 0.0)
            rr_ref[:, cs] = rr.astype(BF16)
            f = f + _dot((rr * rr).astype(BF16), w2_ref[cs, :])
        xhat1 = xh_ref[...]
        h1 = xhat1 * v_ref[0:1, :] + v_ref[1:2, :]
        xhat2, rstd2 = _ln_fwd(alpha * h1 + (1.0 + v_ref[2:3, :]) * f)
        e = xhat2 * v_ref[3:4, :] + v_ref[4:5, :] - t_ref[...]
        loss_ref[...] += 0.5 * jnp.sum(_mean(e * e))
        dy = e * (1.0 / D)
        dr2 = _ln_bwd(dy * v_ref[3:4, :], xhat2, rstd2)
        dr2_ref[...] = dr2
        acc_ref[0:1, :] += _colsum(dy * xhat2)
        acc_ref[1:2, :] += _colsum(dy)
        acc_ref[2:3, :] += _colsum(dr2 * f)

    return pl.pallas_call(
        body, name="mlp_loss", grid=(L // tm,),
        in_specs=[_tile(D, tm), _res((D, FF)), _res((FF, D)), _tile(D, tm), _tile(D, tm), _acc((8, D)), _acc((1, FF))],
        out_specs=(_tile(FF, tm), _tile(D, tm), _acc((8, D)), _acc((1, LANE))),
        out_shape=(_S((L, FF), BF16), _S((L, D), F32), _S((8, D), F32), _S((1, LANE), F32)),
        compiler_params=_params(("arbitrary",)))(u2, w1, w2, xhat1, tgt, vecs, b1)


def _mlp_bwd_a(dr2, rr, w2, g2):
    L, D = dr2.shape
    FF = w2.shape[0]
    tm = _pick(L, (256, 128))
    fc = _pick(FF, (512, 256, 128))

    def body(dr2_ref, rr_ref, w2_ref, g2_ref, df_ref, da_ref, gb2_ref, gb1_ref):
        @pl.when(pl.program_id(0) == 0)
        def _():
            gb2_ref[...] = jnp.zeros_like(gb2_ref)
            gb1_ref[...] = jnp.zeros_like(gb1_ref)

        df = (1.0 + g2_ref[...]) * dr2_ref[...]
        gb2_ref[...] += _colsum(df)
        dfb = df.astype(BF16)
        df_ref[...] = dfb
        for j in range(FF // fc):
            cs = slice(j * fc, (j + 1) * fc)
            da = _dot_nt(dfb, w2_ref[cs, :]) * (2.0 * rr_ref[:, cs].astype(F32))
            gb1_ref[:, cs] += _colsum(da)
            da_ref[:, cs] = da.astype(BF16)

    return pl.pallas_call(
        body, name="mlp_bwd_a", grid=(L // tm,),
        in_specs=[_tile(D, tm), _tile(FF, tm), _res((FF, D)), _acc((1, D))],
        out_specs=(_tile(D, tm), _tile(FF, tm), _acc((1, D)), _acc((1, FF))),
        out_shape=(_S((L, D), BF16), _S((L, FF), BF16), _S((1, D), F32), _S((1, FF), F32)),
        compiler_params=_params(("arbitrary",)))(dr2, rr, w2, g2)


def _mlp_bwd_b(da, w1, dr2, xhat1, rstd1, mix, w_out, vecs, alpha, W):
    L, FF = da.shape
    D = dr2.shape[1]
    MIX = w_out.shape[0]
    tm = _pick(L, (256, 128))

    def body(da_ref, w1_ref, dr2_ref, xh_ref, rs_ref, mix_ref, wo_ref, v_ref, dmix_ref, dh0_ref, dyn_ref, do_ref, acc_ref):
        @pl.when(pl.program_id(0) == 0)
        def _():
            acc_ref[...] = jnp.zeros_like(acc_ref)

        du2 = _dot_nt(da_ref[...], w1_ref[...])
        xhat1 = xh_ref[...]
        h1 = xhat1 * v_ref[0:1, :] + v_ref[1:2, :]
        acc_ref[0:1, :] += _colsum(du2 * h1)
        acc_ref[1:2, :] += _colsum(du2)
        dh1 = alpha * dr2_ref[...] + du2 * (1.0 + v_ref[2:3, :])
        acc_ref[2:3, :] += _colsum(dh1 * xhat1)
        acc_ref[3:4, :] += _colsum(dh1)
        dr1 = _ln_bwd(dh1 * v_ref[0:1, :], xhat1, rs_ref[...])
        acc_ref[4:5, :] += _colsum(dr1 * mix_ref[...])
        dh0_ref[...] = alpha * dr1
        dmix = ((1.0 + v_ref[3:4, :]) * dr1).astype(BF16)
        dmix_ref[...] = dmix
        dyn_ref[...] = _dot_nt(dmix, wo_ref[0:W, :])
        do_ref[...] = _dot_nt(dmix, wo_ref[W:MIX, :]).astype(BF16)

    return pl.pallas_call(
        body, name="mlp_bwd_b", grid=(L // tm,),
        in_specs=[_tile(FF, tm), _res((D, FF)), _tile(D, tm), _tile(D, tm), _tile(1, tm), _tile(D, tm), _res((MIX, D)),
                  _acc((8, D))],
        out_specs=(_tile(D, tm), _tile(D, tm), _tile(W, tm), _tile(MIX - W, tm), _acc((8, D))),
        out_shape=(_S((L, D), BF16), _S((L, D), F32), _S((L, W), F32), _S((L, MIX - W), BF16), _S((8, D), F32)),
        compiler_params=_params(("arbitrary",)))(da, w1, dr2, xhat1, rstd1, mix, w_out, vecs)


def _swa_bwd(q, kv, do, sinks, dims):
    L, AW = q.shape
    KV, KVW2 = dims["KV"], dims["KVW2"]
    T = CHUNK
    nb = L // T
    slopes = _alibi_slopes(dims["AH"])
    scale = HEAD_DIM ** -0.5

    def body(q_ref, kvc_ref, kvp_ref, do_ref, sink_ref, dq_ref, dkv_ref, dsink_ref, carry_ref,
             qg_ref, dog_ref, pt_ref, dst_ref):
        i = pl.program_id(0)

        @pl.when(i == 0)
        def _():
            carry_ref[...] = jnp.zeros_like(carry_ref)
            dsink_ref[...] = jnp.zeros_like(dsink_ref)

        @pl.when(i < nb)
        def _():
            c = lax.broadcasted_iota(jnp.int32, (2 * T, T), 0)
            r_ = lax.broadcasted_iota(jnp.int32, (2 * T, T), 1)
            dist_i = r_ + T - c
            valid = (dist_i >= 0) & (dist_i < CHUNK) & ((c >= T) | (i > 0))
            dist = dist_i.astype(F32)
            lane = lax.broadcasted_iota(jnp.int32, (1, LANE), 1)
            dsink = jnp.zeros((1, LANE), F32)
            dks, dvs = [], []
            for g in range(KV):
                kk, vv = _kv_heads(kvc_ref, kvp_ref, g, KV)
                for r in range(GQA):
                    hs = slice((g * GQA + r) * HEAD_DIM, (g * GQA + r + 1) * HEAD_DIM)
                    qg_ref[r * T:(r + 1) * T, :] = q_ref[:, hs]
                    dog_ref[r * T:(r + 1) * T, :] = do_ref[:, hs]
                st_all = _dot_nt(kk, qg_ref[...])
                dpt_all = _dot_nt(vv, dog_ref[...])
                for r in range(GQA):
                    h = g * GQA + r
                    cs = slice(r * T, (r + 1) * T)
                    p, p_sink = _attn_probs(st_all[:, cs], dist, valid, slopes[h], sink_ref[h], 0)
                    dp = dpt_all[:, cs]
                    delta = jnp.sum(p * dp, axis=0, keepdims=True)
                    pt_ref[:, cs] = p.astype(BF16)
                    dst_ref[:, cs] = (p * (dp - delta)).astype(BF16)
                    dsink = dsink + jnp.where(lane == h, -jnp.sum(p_sink * delta), 0.0)
                dst = dst_ref[...]
                dks.append(_dot(dst, qg_ref[...]) * scale)
                dvs.append(_dot(pt_ref[...], dog_ref[...]))
                dq_all = _dot_tn(dst, kk) * scale
                for r in range(GQA):
                    hs = slice((g * GQA + r) * HEAD_DIM, (g * GQA + r + 1) * HEAD_DIM)
                    dq_ref[:, hs] = dq_all[r * T:(r + 1) * T, :].astype(BF16)
            dkv = jnp.concatenate(dks + dvs, axis=1)
            dsink_ref[...] += dsink
            dkv_ref[...] = carry_ref[...] + dkv[0:T, :]
            carry_ref[...] = dkv[T:2 * T, :]

        @pl.when(i == nb)
        def _():
            dkv_ref[...] = carry_ref[...]

    last = nb - 1
    return pl.pallas_call(
        body, name="swa_bwd", grid=(nb + 1,),
        in_specs=[pl.BlockSpec((T, AW), lambda i: (jnp.minimum(i, last), 0)),
                  pl.BlockSpec((T, KVW2), lambda i: (jnp.minimum(i, last), 0)),
                  pl.BlockSpec((T, KVW2), lambda i: (jnp.clip(i - 1, 0, last), 0)),
                  pl.BlockSpec((T, AW), lambda i: (jnp.minimum(i, last), 0)),
                  pl.BlockSpec(memory_space=pltpu.SMEM)],
        out_specs=(pl.BlockSpec((T, AW), lambda i: (jnp.minimum(i, last), 0)),
                   pl.BlockSpec((T, KVW2), lambda i: (jnp.maximum(i - 1, 0), 0)), _acc((1, LANE))),
        out_shape=(_S((L, AW), BF16), _S((L, KVW2), F32), _S((1, LANE), F32)),
        scratch_shapes=[pltpu.VMEM((T, KVW2), F32), pltpu.VMEM((GQA * T, HEAD_DIM), BF16),
                        pltpu.VMEM((GQA * T, HEAD_DIM), BF16), pltpu.VMEM((2 * T, GQA * T), BF16),
                        pltpu.VMEM((2 * T, GQA * T), BF16)],
        compiler_params=_params(("arbitrary",)))(q, kv, kv, do, sinks)


def _ssd_bwd(dyn, y, z, xbc, dt_raw, sprev, cw, cb, dtb, alog, dsk, nw, dims):
    L, CD = xbc.shape
    W, H, G, N = dims["W"], dims["H"], SSD_GROUPS, SSD_STATE
    T = CHUNK
    R = H // G
    GW = W // G
    nc = L // T
    HP = H * HEAD_DIM

    def body(dyn_ref, y_ref, z_ref, xbc_ref, prev_ref, dt_ref, sp_ref, cw_ref, cb_ref, dtb_ref, alog_ref, dsk_ref, nw_ref,
             dz_ref, dpre_ref, ddt_ref, acc_ref, hacc_ref, ext_ref, ds_ref, dtx_ref, acx_ref, xb_ref, dyb_ref, r12_ref,
             dx_ref, rows_ref):
        i = pl.program_id(0)

        @pl.when(i == 0)
        def _():
            ds_ref[...] = jnp.zeros_like(ds_ref)
            acc_ref[...] = jnp.zeros_like(acc_ref)
            hacc_ref[...] = jnp.zeros_like(hacc_ref)

        act, pre = _conv_act(xbc_ref, prev_ref, cw_ref, cb_ref, ext_ref, i == nc - 1)
        xs = act[:, :W]
        dt_in = dt_ref[...] + dtb_ref[...]
        dt = _softplus(dt_in)
        a_neg = -jnp.exp(alog_ref[...])
        a = dt * a_neg
        low = _tri(T)
        upf = _tri(T, upper=True).astype(F32)
        acum = _dot_hi(low.astype(F32), a)
        acum_t = _dot_hi(a.T, upf)

        y = y_ref[...]
        zz = z_ref[...]
        sg = _sigmoid(zz)
        sz = zz * sg
        hh = y * sz
        dyn_v = dyn_ref[...]
        parts = []
        for g in range(G):
            gs = slice(g * GW, (g + 1) * GW)
            hg = hh[:, gs]
            hhat = hg * lax.rsqrt(_mean(hg * hg) + RMS_EPS)
            rg = lax.rsqrt(_mean(hg * hg) + RMS_EPS)
            acc_ref[0:1, gs] += _colsum(dyn_v[:, gs] * hhat)
            dhhat = dyn_v[:, gs] * nw_ref[:, gs]
            parts.append(rg * (dhhat - hhat * _mean(dhhat * hhat)))
        dhh = jnp.concatenate(parts, axis=1)
        dy = dhh * sz
        dz_ref[...] = (dhh * y * (sg * (1.0 + zz * (1.0 - sg)))).astype(BF16)
        acc_ref[1:2, :] += _colsum(dy * xs)
        dyb_ref[...] = dy.astype(BF16)

        _expand_heads(dtx_ref, dt, H)
        _expand_heads(acx_ref, acum, H)
        dtx = dtx_ref[...]
        acx = acx_ref[...]
        lastx = acx[T - 1:T, :]
        ex = jnp.exp(acx)
        decx = jnp.exp(lastx - acx)
        elx = jnp.exp(lastx)
        xd = xs * dtx
        xb_ref[...] = xd.astype(BF16)
        xdecb = (xd * decx).astype(BF16)
        dgb = (ex * dy).astype(BF16)
        rows_ref[...] = jnp.zeros_like(rows_ref)

        lane = lax.broadcasted_iota(jnp.int32, (T, LANE), 1)
        sub = lax.broadcasted_iota(jnp.int32, (T, LANE), 0)
        subr = lax.broadcasted_iota(jnp.int32, (LANE, T), 0)
        da_col = jnp.zeros((T, LANE), F32)
        da_row = jnp.zeros((LANE, T), F32)
        dbs, dcs = [], []
        for g in range(G):
            gs = slice(g * GW, (g + 1) * GW)
            bgb = act[:, W + g * N:W + (g + 1) * N].astype(BF16)
            cgb = act[:, W + G * N + g * N:W + G * N + (g + 1) * N].astype(BF16)
            stg = sp_ref[0, :, gs]
            stb = stg.astype(BF16)
            dsn = ds_ref[:, gs]
            dsnb = dsn.astype(BF16)
            gm = _dot(cgb, stb)
            dc = _dot_nt(dgb[:, gs], stb)
            dsp = _dot_tn(cgb, dgb[:, gs])
            dxs_ = decx[:, gs] * _dot(bgb, dsnb)
            db = _dot_nt(xdecb[:, gs], dsnb)
            xdg = xd[:, gs]
            r12_ref[:, gs] = dy[:, gs] * ex[:, gs] * gm - xdg * dxs_
            rows_ref[0:1, gs] = _colsum(dsn * stg) * elx[:, gs]
            rows_ref[1:2, gs] = _colsum(xdg * dxs_)
            ds_ref[:, gs] = dsp + dsn * elx[:, gs]
            cb_g = _dot_nt(cgb, bgb)
            dcb = jnp.zeros((T, T), F32)
            for r in range(R):
                h = g * R + r
                hs = slice(h * HEAD_DIM, (h + 1) * HEAD_DIM)
                lm = jnp.where(low, jnp.exp(acum[:, h:h + 1] - acum_t[h:h + 1, :]), 0.0)
                mm = cb_g * lm
                dyb = dyb_ref[:, hs]
                dm = _dot_nt(dyb, xb_ref[:, hs])
                dx_ref[:, hs] = dxs_[:, r * HEAD_DIM:(r + 1) * HEAD_DIM] + _dot_tn(mm.astype(BF16), dyb)
                dcb = dcb + dm * lm
                qm = dm * mm
                da_col = jnp.where(lane == h, jnp.sum(qm, axis=1, keepdims=True), da_col)
                da_row = jnp.where(subr == h, jnp.sum(qm, axis=0, keepdims=True), da_row)
            dcbb = dcb.astype(BF16)
            dcs.append(dc + _dot(dcbb, bgb))
            dbs.append(db + _dot_tn(dcbb, cgb))
        dx = dx_ref[...]
        rows = _head_reduce(rows_ref[...])
        dlast = rows[0:1, :] + rows[1:2, :]
        da_col = da_col + _head_reduce(r12_ref[...]) + jnp.where(sub == T - 1, dlast, 0.0)
        dacum = da_col - da_row.T
        da = _dot_hi(upf, dacum)
        ddt = _head_reduce(dx * xs) + da * a_neg
        hacc_ref[1:2, :] += _colsum(da * dt) * a_neg
        ddt_raw = ddt * _sigmoid(dt_in)
        hacc_ref[0:1, :] += _colsum(ddt_raw)
        ddt_ref[...] = ddt_raw
        dact = jnp.concatenate([dsk_ref[...] * dy + dx * dtx] + dbs + dcs, axis=1)
        spre = _sigmoid(pre)
        dpre_ref[...] = dact * (spre * (1.0 + pre * (1.0 - spre)))

        @pl.when(i == nc - 1)
        def _():
            ch = lax.broadcasted_iota(jnp.int32, (W, LANE), 0)
            lo = lax.broadcasted_iota(jnp.int32, (W, LANE), 1) * HEAD_DIM
            hacc_ref[2:3, :] = _dot_hi(acc_ref[1:2, :], ((ch >= lo) & (ch < lo + HEAD_DIM)).astype(F32))

    rev = lambda i: (nc - 1 - i, 0)
    return pl.pallas_call(
        body, name="ssd_bwd", grid=(nc,),
        in_specs=[pl.BlockSpec((T, W), rev), pl.BlockSpec((T, W), rev), pl.BlockSpec((T, W), rev), pl.BlockSpec((T, CD), rev),
                  pl.BlockSpec((HALO, CD), lambda i: (jnp.maximum((nc - 1 - i) * (T // HALO) - 1, 0), 0)),
                  pl.BlockSpec((T, LANE), rev), pl.BlockSpec((1, N, HP), lambda i: (nc - 1 - i, 0, 0)),
                  _acc((CONV_K, CD)), _acc((1, CD)), _acc((1, LANE)), _acc((1, LANE)), _acc((1, W)), _acc((1, W))],
        out_specs=(pl.BlockSpec((T, W), rev), pl.BlockSpec((T, CD), rev), pl.BlockSpec((T, LANE), rev), _acc((8, W)),
                   _acc((8, LANE))),
        out_shape=(_S((L, W), BF16), _S((L, CD), F32), _S((L, LANE), F32), _S((8, W), F32), _S((8, LANE), F32)),
        scratch_shapes=[pltpu.VMEM((T + HALO, CD), F32), pltpu.VMEM((N, HP), F32), pltpu.VMEM((T, W), F32),
                        pltpu.VMEM((T, W), F32), pltpu.VMEM((T, W), BF16), pltpu.VMEM((T, W), BF16), pltpu.VMEM((T, W), F32),
                        pltpu.VMEM((T, W), F32), pltpu.VMEM((8, W), F32)],
        compiler_params=_params(("arbitrary",)))(dyn, y, z, xbc, xbc, dt_raw, sprev, cw, cb, dtb, alog, dsk, nw)


def _conv_bwd(dpre, xbc, cw):
    L, CD = xbc.shape
    tm = _pick(L, (256, 128))
    nt = L // tm
    hb = tm // HALO

    def body(dp_ref, dn_ref, u_ref, up_ref, cw_ref, du_ref, acc_ref, extu_ref, extd_ref):
        i = pl.program_id(0)

        @pl.when(i == 0)
        def _():
            acc_ref[...] = jnp.zeros_like(acc_ref)

        dp = dp_ref[...]
        extu_ref[0:HALO, :] = jnp.where(i == 0, 0.0, up_ref[...])
        extu_ref[HALO:HALO + tm, :] = u_ref[...]
        extd_ref[0:tm, :] = dp
        extd_ref[tm:tm + HALO, :] = jnp.where(i == nt - 1, 0.0, dn_ref[...])
        du = cw_ref[CONV_K - 1:CONV_K, :] * dp
        acc_ref[CONV_K - 1:CONV_K, :] += _colsum(dp * u_ref[...])
        for k in range(CONV_K - 1):
            s = CONV_K - 1 - k
            du = du + cw_ref[k:k + 1, :] * extd_ref[s:s + tm, :]
            acc_ref[k:k + 1, :] += _colsum(dp * extu_ref[HALO - s:HALO - s + tm, :])
        acc_ref[CONV_K:CONV_K + 1, :] += _colsum(dp)
        du_ref[...] = du.astype(BF16)

    return pl.pallas_call(
        body, name="conv_bwd", grid=(nt,),
        in_specs=[_tile(CD, tm), pl.BlockSpec((HALO, CD), lambda i: (jnp.minimum((i + 1) * hb, nt * hb - 1), 0)),
                  _tile(CD, tm), pl.BlockSpec((HALO, CD), lambda i: (jnp.maximum(i * hb - 1, 0), 0)), _acc((CONV_K, CD))],
        out_specs=(_tile(CD, tm), _acc((8, CD))),
        out_shape=(_S((L, CD), BF16), _S((8, CD), F32)),
        scratch_shapes=[pltpu.VMEM((tm + HALO, CD), F32), pltpu.VMEM((tm + HALO, CD), F32)],
        compiler_params=_params(("arbitrary",)))(dpre, dpre, xbc, xbc, cw)


def _in_proj_bwd(dz, dxbc, dq, dkv, ddt, w_pad, xhat0, rstd0, dh0p, vecs, dims):
    L, D = xhat0.shape
    W, CD, AW, KVW2 = dims["W"], dims["CD"], dims["AW"], dims["KVW2"]
    NP = w_pad.shape[1]
    tm = _pick(L, (256, 128))
    o_xbc, o_q, o_kv, o_dt = W, W + CD, W + CD + AW, W + CD + AW + KVW2

    def body(dz_ref, dxbc_ref, dq_ref, dkv_ref, ddt_ref, w_ref, xh_ref, rs_ref, dh0_ref, v_ref, gx_ref, acc_ref):
        @pl.when(pl.program_id(0) == 0)
        def _():
            acc_ref[...] = jnp.zeros_like(acc_ref)

        du1 = _dot_nt(dz_ref[...], w_ref[:, 0:o_xbc])
        du1 = du1 + _dot_nt(dxbc_ref[...], w_ref[:, o_xbc:o_q])
        du1 = du1 + _dot_nt(dq_ref[...], w_ref[:, o_q:o_kv])
        du1 = du1 + _dot_nt(dkv_ref[...].astype(BF16), w_ref[:, o_kv:o_dt])
        du1 = du1 + _dot_nt(ddt_ref[...].astype(BF16), w_ref[:, o_dt:NP])
        xhat0 = xh_ref[...]
        h0 = xhat0 * v_ref[0:1, :] + v_ref[1:2, :]
        acc_ref[0:1, :] += _colsum(du1 * h0)
        acc_ref[1:2, :] += _colsum(du1)
        dh0 = dh0_ref[...] + du1 * (1.0 + v_ref[2:3, :])
        acc_ref[2:3, :] += _colsum(dh0 * xhat0)
        acc_ref[3:4, :] += _colsum(dh0)
        gx_ref[...] = _ln_bwd(dh0 * v_ref[0:1, :], xhat0, rs_ref[...])

    return pl.pallas_call(
        body, name="in_proj_bwd", grid=(L // tm,),
        in_specs=[_tile(W, tm), _tile(CD, tm), _tile(AW, tm), _tile(KVW2, tm), _tile(LANE, tm), _res((D, NP)),
                  _tile(D, tm), _tile(1, tm), _tile(D, tm), _acc((8, D))],
        out_specs=(_tile(D, tm), _acc((8, D))),
        out_shape=(_S((L, D), F32), _S((8, D), F32)),
        compiler_params=_params(("arbitrary",)))(dz, dxbc, dq, dkv, ddt, w_pad, xhat0, rstd0, dh0p, vecs)


_WEIGHTS = ['ln_in_g', 'ln_in_b', 'ada_w', 'ada_b', 'w_in', 'conv_w', 'conv_b', 'dt_bias', 'a_log', 'd_skip', 'ssd_norm_w',
            'attn_sinks', 'w_out', 'ln1_g', 'ln1_b', 'w_ff1', 'b_ff1', 'w_ff2', 'b_ff2', 'ln2_g', 'ln2_b']
_BIG = ('w_in', 'w_out', 'w_ff1', 'w_ff2')
_SMALL = ('ada_b', 'ln_in_g', 'ln_in_b', 'conv_b', 'dt_bias', 'a_log', 'd_skip', 'ssd_norm_w', 'attn_sinks', 'ln1_g', 'ln1_b',
          'b_ff1', 'b_ff2', 'ln2_g', 'ln2_b')


def _pad_lanes(v, n=None):
    v = v.reshape(1, -1)
    n = n or -(-v.shape[1] // LANE) * LANE
    return jnp.pad(v, ((0, 0), (0, n - v.shape[1])))


def _vec8(rows, D):
    rows = [r.reshape(1, D) for r in rows]
    return jnp.concatenate(rows + [jnp.zeros((8 - len(rows), D), F32)], axis=0)


def _pack(segs):
    flat, offs, sizes, o = [], [], [], 0
    for s in segs:
        p = _pad_lanes(s)
        flat.append(p)
        offs.append(o)
        sizes.append(s.size)
        o += p.shape[1]
    total = -(-o // (8 * LANE)) * (8 * LANE)
    if total > o:
        flat.append(jnp.zeros((1, total - o), F32))
    return jnp.concatenate(flat, axis=1).reshape(8, total // 8), offs, sizes


def kernel(x, c, ln_in_g, ln_in_b, ada_w, ada_b, w_in, conv_w, conv_b, dt_bias, a_log, d_skip, ssd_norm_w, attn_sinks, w_out, ln1_g, ln1_b, w_ff1, b_ff1, w_ff2, b_ff2, ln2_g, ln2_b, loss_target, m_ln_in_g, m_ln_in_b, m_ada_w, m_ada_b, m_w_in, m_conv_w, m_conv_b, m_dt_bias, m_a_log, m_d_skip, m_ssd_norm_w, m_attn_sinks, m_w_out, m_ln1_g, m_ln1_b, m_w_ff1, m_b_ff1, m_w_ff2, m_b_ff2, m_ln2_g, m_ln2_b, v_ln_in_g, v_ln_in_b, v_ada_w, v_ada_b, v_w_in, v_conv_w, v_conv_b, v_dt_bias, v_a_log, v_d_skip, v_ssd_norm_w, v_attn_sinks, v_w_out, v_ln1_g, v_ln1_b, v_w_ff1, v_b_ff1, v_w_ff2, v_b_ff2, v_ln2_g, v_ln2_b):
    wts = dict(ln_in_g=ln_in_g, ln_in_b=ln_in_b, ada_w=ada_w, ada_b=ada_b, w_in=w_in, conv_w=conv_w, conv_b=conv_b,
               dt_bias=dt_bias, a_log=a_log, d_skip=d_skip, ssd_norm_w=ssd_norm_w, attn_sinks=attn_sinks, w_out=w_out,
               ln1_g=ln1_g, ln1_b=ln1_b, w_ff1=w_ff1, b_ff1=b_ff1, w_ff2=w_ff2, b_ff2=b_ff2, ln2_g=ln2_g, ln2_b=ln2_b)
    ms = dict(ln_in_g=m_ln_in_g, ln_in_b=m_ln_in_b, ada_w=m_ada_w, ada_b=m_ada_b, w_in=m_w_in, conv_w=m_conv_w,
              conv_b=m_conv_b, dt_bias=m_dt_bias, a_log=m_a_log, d_skip=m_d_skip, ssd_norm_w=m_ssd_norm_w,
              attn_sinks=m_attn_sinks, w_out=m_w_out, ln1_g=m_ln1_g, ln1_b=m_ln1_b, w_ff1=m_w_ff1, b_ff1=m_b_ff1,
              w_ff2=m_w_ff2, b_ff2=m_b_ff2, ln2_g=m_ln2_g, ln2_b=m_ln2_b)
    vs = dict(ln_in_g=v_ln_in_g, ln_in_b=v_ln_in_b, ada_w=v_ada_w, ada_b=v_ada_b, w_in=v_w_in, conv_w=v_conv_w,
              conv_b=v_conv_b, dt_bias=v_dt_bias, a_log=v_a_log, d_skip=v_d_skip, ssd_norm_w=v_ssd_norm_w,
              attn_sinks=v_attn_sinks, w_out=v_w_out, ln1_g=v_ln1_g, ln1_b=v_ln1_b, w_ff1=v_w_ff1, b_ff1=v_b_ff1,
              w_ff2=v_w_ff2, b_ff2=v_b_ff2, ln2_g=v_ln2_g, ln2_b=v_ln2_b)

    L, D = x.shape[1], x.shape[2]
    depth = w_in.shape[0]
    assert depth == 1 and x.shape[0] == 1 and L % CHUNK == 0
    W = D
    H = W // HEAD_DIM
    CD = W + 2 * SSD_GROUPS * SSD_STATE
    AW = D
    AH = AW // HEAD_DIM
    KV = AH // GQA
    KVW2 = 2 * KV * HEAD_DIM
    PROJ = W + CD + H + AW + KVW2
    FF = w_ff1.shape[2] * N_DEV
    MIX = w_out.shape[1] * N_DEV
    assert w_in.shape[2] * N_DEV == PROJ and MIX == W + AW and H <= LANE and AH <= LANE
    dims = dict(W=W, H=H, CD=CD, AW=AW, AH=AH, KV=KV, KVW2=KVW2)
    alpha = (2.0 * depth) ** 0.25
    C6 = ada_w.shape[2]
    CW = conv_w.shape[2]

    ax, ay, ac = _my_pos()
    me = 4 * ax + 2 * ay + ac
    x2 = x.reshape(L, D)
    tgt = loss_target.reshape(L, D)
    r1 = lambda a: a.reshape(1, -1)

    ada_b_cols = lax.dynamic_slice(ada_b, (0, me * C6), (1, C6))
    cs_all, mod = _mod_fwd(c, ada_w[0], ada_b_cols)
    sh1, sc1, g1, sh2, sc2, g2 = [r1(t) for t in jnp.split(mod.reshape(-1), 6)]

    wg_in, cwg = _ag_weights([w_in[0].astype(BF16), conv_w[0]], cs_all)
    shards2 = [w_out[0].astype(BF16), w_ff1[0].astype(BF16), w_ff2[0].astype(BF16)]
    lands2 = [lax.dynamic_update_slice(lax.empty((N_DEV,) + s.shape, s.dtype), s[None], (me, 0, 0)) for s in shards2]
    ag_ss, ag_rs, ag_arr, ag_token = _split_start(shards2 + lands2, _plan_gather(3), cwg, "ag_ici_start")
    sh1 = sh1 + ag_token[0:1, 0:1]
    w_in_full = wg_in.transpose(1, 0, 2).reshape(D, PROJ)
    i1, i2, i3, i4 = W, W + CD, W + CD + H, W + CD + H + AW
    w_pad = jnp.concatenate([w_in_full[:, :i2], w_in_full[:, i3:], w_in_full[:, i2:i3], jnp.zeros((D, LANE - H), BF16)], axis=1)
    cw_full = cwg.transpose(1, 0, 2).reshape(CONV_K, CD)

    dtb = _pad_lanes(dt_bias, LANE)
    alog = _pad_lanes(a_log, LANE)
    dsk = jnp.repeat(d_skip.reshape(-1), HEAD_DIM).reshape(1, W)
    sinks = attn_sinks.reshape(-1)
    g_in, b_in = r1(ln_in_g), r1(ln_in_b)

    xhat0, rstd0, u1, z, xbc, q, kv, dt_raw = _ln_in_proj(x2, g_in, b_in, sc1, sh1, w_pad, dims)
    y, yn, sprev = _conv_ssd(xbc, dt_raw, z, cw_full, conv_b, dtb, alog, dsk, ssd_norm_w, dims)
    ag_arr = _split_wait(ag_ss, ag_rs, ag_arr, _plan_gather(3), yn, "ag_ici_wait")
    fw_ss, fw_rs, ag_land, fw_token = _split_start(ag_arr[3:], _plan_forward(3), yn, "ag_fwd_start")
    o = _swa_fwd(q, kv, sinks + fw_token[0, 0], dims)
    wg_out, wg_ff1, wg_ff2 = _split_wait(fw_ss, fw_rs, ag_land, _plan_forward(3), o, "ag_fwd_wait")
    w_out_full = wg_out.reshape(MIX, D)
    w1_full = wg_ff1.transpose(1, 0, 2).reshape(D, FF)
    w2_full = wg_ff2.reshape(FF, D)
    mix, xhat1, rstd1, u2 = _out_proj_ln1(yn, o, w_out_full, xhat0, _vec8([g_in, b_in, g1, ln1_g, ln1_b, sc2, sh2], D), alpha)
    rr, dr2, acc_f, loss_loc = _mlp_loss(u2, w1_full, w2_full, xhat1, tgt,
                                         _vec8([ln1_g, ln1_b, g2, ln2_g, ln2_b, b_ff2], D), b_ff1, alpha)

    df, da, gb2, gb1 = _mlp_bwd_a(dr2, rr, w2_full, g2)
    gw_ff2 = _matmul_tn(rr, df, "gw_ff2", square_a=True)
    gw_ff1t = _matmul_tn(da, u2, "gw_ff1")
    dmix, dh0p, dyn, do, acc_b = _mlp_bwd_b(da, w1_full, dr2, xhat1, rstd1, mix, w_out_full,
                                            _vec8([ln1_g, ln1_b, sc2, g1], D), alpha, W)
    gw_out = jnp.concatenate([_matmul_tn(yn, dmix, "gw_out_ssd"), _matmul_tn(o, dmix, "gw_out_attn")], axis=0)

    core = jnp.reshape(ac, (1,)).astype(jnp.int32)
    blocked1 = [gw_out.reshape(N_DEV, MIX // N_DEV, D), gw_ff1t.reshape(N_DEV, FF // N_DEV, D),
                gw_ff2.reshape(N_DEV, FF // N_DEV, D)]
    recv1 = [lax.empty((N_CHIP,) + b.shape[1:], b.dtype) for b in blocked1]
    d_ss, d_rs, d_arr, d_token = _split_start(blocked1 + recv1, _plan_d2d(3), do, "rs_d2d_start")
    dq, dkv, dsink = _swa_bwd(q, kv, do, sinks + d_token[0, 0], dims)
    d_arr = _split_wait(d_ss, d_rs, d_arr, _plan_d2d(3), dq, "rs_d2d_wait")
    pairs1 = [_pair_sum(b, r, core) for b, r in zip(d_arr[:3], d_arr[3:])]
    lands1 = [lax.empty(p.shape, p.dtype) for p in pairs1]
    rs_ss, rs_rs, rs_arr, rs_token = _split_start(pairs1 + lands1, _plan_scatter(3), dq, "rs_ici_start")

    dz, dpre, ddt, acc_s, hacc = _ssd_bwd(dyn, y, z, xbc, dt_raw, sprev, cw_full, conv_b, dtb + rs_token[0:1, 0:1], alog, dsk,
                                          ssd_norm_w, dims)
    dxbc, acc_c = _conv_bwd(dpre, xbc, cw_full)
    gz = _matmul_tn(u1, dz, "gw_in_z")
    gxbc = _matmul_tn(u1, dxbc, "gw_in_xbc")
    gq = _matmul_tn(u1, dq, "gw_in_q")
    gkv = _matmul_tn(u1, dkv, "gw_in_kv")
    gdt = _matmul_tn(u1, ddt, "gw_in_dt")
    gw_in = jnp.concatenate([gz, gxbc, gdt[:, :H], gq, gkv], axis=1)

    blocked2 = [gw_in.reshape(D, N_DEV, PROJ // N_DEV).transpose(1, 0, 2)]
    pairs2 = [_pair_sum(b, r, core) for b, r in zip(blocked2, _rs_d2d(blocked2, "rs_d2d_2"))]
    lands2 = [lax.empty(p.shape, p.dtype) for p in pairs2]
    r2_ss, r2_rs, r2_arr, r2_token = _split_start(pairs2 + lands2, _plan_scatter(1), gz, "rs_ici_start_2")
    grad_x, acc_i = _in_proj_bwd(dz, dxbc, dq, dkv, ddt, w_pad, xhat0, rstd0, dh0p,
                                 _vec8([g_in, b_in, sc1], D) + r2_token[0:1, 0:1], dims)

    dmod = jnp.concatenate([acc_i[1], acc_i[0], acc_b[4], acc_b[1], acc_b[0], acc_f[2]])
    small_g = dict(ada_b=dmod, ln_in_g=acc_i[2], ln_in_b=acc_i[3], conv_b=acc_c[CONV_K], dt_bias=hacc[0, :H], a_log=hacc[1, :H],
                   d_skip=hacc[2, :H], ssd_norm_w=acc_s[0], attn_sinks=dsink[0, :AH], ln1_g=acc_b[2], ln1_b=acc_b[3],
                   b_ff1=gb1[0], b_ff2=gb2[0], ln2_g=acc_f[0], ln2_b=acc_f[1])
    segs = [small_g[n] for n in _SMALL] + [acc_c[:CONV_K].reshape(-1), loss_loc[0, :1]]
    pack, offs, sizes = _pack(segs)
    gathered, summed = _small_gather_sum(pack)
    gathered = gathered.reshape(N_DEV, -1)
    summed = summed.reshape(-1)
    seg = lambda k: summed[offs[k]:offs[k] + sizes[k]]
    grads = {n: seg(k).reshape(wts[n].shape) for k, n in enumerate(_SMALL)}
    gcw_full = seg(len(_SMALL)).reshape(CONV_K, CD)
    grads['conv_w'] = lax.dynamic_slice(gcw_full, (0, me * CW), (CONV_K, CW)).reshape(conv_w.shape)
    loss = seg(len(_SMALL) + 1)[0]

    names = list(_SMALL) + ['conv_w']
    pw, poffs, psizes = _pack([wts[n] for n in names])
    pg, _, _ = _pack([grads[n] for n in names])
    pm, _, _ = _pack([ms[n] for n in names])
    pv, _, _ = _pack([vs[n] for n in names])
    pd, pm2, pv2 = [t.reshape(-1) for t in _adamw(pw, pg, pm, pv)]
    deltas, new_m, new_v = {}, {}, {}
    for k, n in enumerate(names):
        sl = slice(poffs[k], poffs[k] + psizes[k])
        deltas[n], new_m[n], new_v[n] = (t[sl].reshape(wts[n].shape) for t in (pd, pm2, pv2))

    dmod_cols = lax.dynamic_slice(gathered, (0, offs[0] + me * C6), (N_DEV, C6))
    pad16 = lambda t: jnp.concatenate([t, jnp.zeros((16 - N_DEV,) + t.shape[1:], t.dtype)], axis=0)
    g_, d_, m_, v_ = _ada_grad_adamw(pad16(cs_all), pad16(dmod_cols), ada_w[0], m_ada_w[0], v_ada_w[0])
    grads['ada_w'], deltas['ada_w'], new_m['ada_w'], new_v['ada_w'] = (t[None] for t in (g_, d_, m_, v_))

    rs_arr = _split_wait(rs_ss, rs_rs, rs_arr, _plan_scatter(3), g_, "rs_ici_wait")
    mychip = 2 * ax + ay
    chips = jnp.stack([(mychip + k) % N_CHIP for k in range(N_CHIP)]).astype(jnp.int32)
    for n, own, land in zip(('w_out', 'w_ff1', 'w_ff2'), rs_arr[:3], rs_arr[3:]):
        g_, d_, m_, v_ = _sum_adamw_split(own, land, chips, wts[n][0], ms[n][0], vs[n][0], transposed=(n == 'w_ff1'))
        grads[n], deltas[n], new_m[n], new_v[n] = (t[None] for t in (g_, d_, m_, v_))
    r2_arr = _split_wait(r2_ss, r2_rs, r2_arr, _plan_scatter(1), g_, "rs_ici_wait_2")
    g_, d_, m_, v_ = _sum_adamw_split(r2_arr[0], r2_arr[1], chips, wts['w_in'][0], ms['w_in'][0], vs['w_in'][0])
    grads['w_in'], deltas['w_in'], new_m['w_in'], new_v['w_in'] = (t[None] for t in (g_, d_, m_, v_))

    return (loss, grad_x.reshape(x.shape), *[grads[n] for n in _WEIGHTS], *[deltas[n] for n in _WEIGHTS],
            *[new_m[n] for n in _WEIGHTS], *[new_v[n] for n in _WEIGHTS])
```

```python
import functools
import math

import numpy as np
import jax
import jax.numpy as jnp
from jax import lax
from jax.experimental import pallas as pl
from jax.experimental.pallas import tpu as pltpu

F32 = jnp.float32
BF16 = jnp.bfloat16
MESH = pl.DeviceIdType.MESH

N_DEV = 8
N_CHIP = 4
HEAD_DIM = 64
SSD_GROUPS = 2
SSD_STATE = 128
CHUNK = 128
CONV_K = 4
GQA = 8
LANE = 128
HALO = 8
LN_EPS = 1e-5
RMS_EPS = 1e-5
NEG = -1e30
ADAM_LR, ADAM_B1, ADAM_B2, ADAM_EPS, ADAM_WD, ADAM_STEP = 0.001, 0.9, 0.999, 1e-08, 0.01, 10
V7X_VMEM_BYTES = 64 * 1024 * 1024
VMEM_LIMIT = V7X_VMEM_BYTES - 8 * 1024 * 1024
HI = lax.Precision.HIGHEST
MLP_TM = 512
MLP_SUB = 512


def _alibi_slopes(n):
    def pow2(m):
        start = 2.0 ** (-8.0 / m)
        return [start ** (i + 1) for i in range(m)]
    if math.log2(n).is_integer():
        s = pow2(n)
    else:
        c = 2 ** math.floor(math.log2(n))
        s = pow2(c) + pow2(2 * c)[0::2][: n - c]
    return [float(v) for v in np.array(s, dtype=np.float32)]


def _dot(a, b):
    return jnp.dot(a, b, preferred_element_type=F32)


def _dot_nt(a, b):
    return lax.dot_general(a, b, (((1,), (1,)), ((), ())), preferred_element_type=F32)


def _dot_tn(a, b):
    return lax.dot_general(a, b, (((0,), (0,)), ((), ())), preferred_element_type=F32)


def _dot_hi(a, b):
    return jnp.dot(a, b, precision=HI, preferred_element_type=F32)


def _sigmoid(x):
    return 1.0 / (1.0 + jnp.exp(-x))


def _softplus(x):
    return jnp.maximum(x, 0.0) + jnp.log(1.0 + jnp.exp(-jnp.abs(x)))


def _mean(x):
    return jnp.mean(x, axis=-1, keepdims=True)


def _ln_fwd(x):
    xc = x - _mean(x)
    rstd = lax.rsqrt(_mean(xc * xc) + LN_EPS)
    return xc * rstd, rstd


def _ln_bwd(dxhat, xhat, rstd):
    return rstd * (dxhat - _mean(dxhat) - xhat * _mean(dxhat * xhat))


def _colsum(x):
    return jnp.sum(x, axis=0, keepdims=True)


def _params(sem):
    return pltpu.CompilerParams(dimension_semantics=sem, vmem_limit_bytes=VMEM_LIMIT)


def _tile(i_map_cols, tm):
    return pl.BlockSpec((tm, i_map_cols), lambda i: (i, 0))


def _res(shape):
    return pl.BlockSpec(shape, lambda *_: (0,) * len(shape), pipeline_mode=pl.Buffered(1))


def _acc(shape):
    return pl.BlockSpec(shape, lambda *_: (0,) * len(shape))


def _S(shape, dtype):
    return jax.ShapeDtypeStruct(shape, dtype)


def _my_pos():
    return lax.axis_index("x"), lax.axis_index("y"), lax.axis_index("c")


def _peer(pos, k):
    x, y, c = pos
    px = 1 - x if k & 4 else x
    py = 1 - y if k & 2 else y
    pc = 1 - c if k & 1 else c
    return (px, py, pc)


def _lin(p):
    return 4 * p[0] + 2 * p[1] + p[2]


def _mod_fwd(c_loc, ada_w_loc, ada_b_cols):
    D = c_loc.shape[1]
    C6 = ada_w_loc.shape[1]

    def body(c_ref, w_ref, b_ref, cs_ref, mod_ref, call_ref, modp_ref, ssem, rsem):
        pos = _my_pos()
        me = _lin(pos)
        call_ref[me] = c_ref[...]
        sends = []
        for k in range(1, N_DEV):
            cp = pltpu.make_async_remote_copy(src_ref=c_ref, dst_ref=call_ref.at[me], send_sem=ssem.at[k - 1],
                                              recv_sem=rsem.at[k - 1], device_id=_peer(pos, k), device_id_type=MESH)
            cp.start()
            sends.append(cp)
        for k in range(1, N_DEV):
            src = _lin(_peer(pos, k))
            pltpu.make_async_remote_copy(src_ref=c_ref, dst_ref=call_ref.at[src], send_sem=ssem.at[k - 1],
                                         recv_sem=rsem.at[k - 1], device_id=pos, device_id_type=MESH).wait_recv()
        for cp in sends:
            cp.wait_send()
        call = jnp.concatenate([call_ref[b] for b in range(N_DEV)], axis=0)
        cs = call * _sigmoid(call)
        cs_ref[...] = cs
        modp = _dot(cs.astype(BF16), w_ref[...].astype(BF16)) + b_ref[...]
        for b in range(N_DEV):
            modp_ref[b] = modp[b:b + 1, :]
        mod_ref[me] = modp_ref[me]
        sends = []
        for k in range(1, N_DEV):
            peer = _peer(pos, k)
            cp = pltpu.make_async_remote_copy(src_ref=modp_ref.at[_lin(peer)], dst_ref=mod_ref.at[me],
                                              send_sem=ssem.at[N_DEV - 2 + k], recv_sem=rsem.at[N_DEV - 2 + k],
                                              device_id=peer, device_id_type=MESH)
            cp.start()
            sends.append(cp)
        for k in range(1, N_DEV):
            src = _lin(_peer(pos, k))
            pltpu.make_async_remote_copy(src_ref=modp_ref.at[src], dst_ref=mod_ref.at[src],
                                         send_sem=ssem.at[N_DEV - 2 + k], recv_sem=rsem.at[N_DEV - 2 + k],
                                         device_id=pos, device_id_type=MESH).wait_recv()
        for cp in sends:
            cp.wait_send()

    vm = pl.BlockSpec(memory_space=pltpu.VMEM)
    return pl.pallas_call(
        body, name="mod_fwd",
        out_shape=(_S((N_DEV, D), F32), _S((N_DEV, 1, C6), F32)),
        in_specs=[vm, vm, vm], out_specs=(vm, vm),
        scratch_shapes=[pltpu.VMEM((N_DEV, 1, D), F32), pltpu.VMEM((N_DEV, 1, C6), F32),
                        pltpu.SemaphoreType.DMA((2 * (N_DEV - 1),)), pltpu.SemaphoreType.DMA((2 * (N_DEV - 1),))],
        compiler_params=pltpu.CompilerParams(vmem_limit_bytes=VMEM_LIMIT),
    )(c_loc, ada_w_loc, ada_b_cols)


def _small_gather_sum(pack):
    P8 = pack.shape[1]

    def body(p_ref, gat_ref, sum_ref, ssem, rsem):
        pos = _my_pos()
        me = _lin(pos)
        gat_ref[me] = p_ref[...]
        sends = []
        for k in range(1, N_DEV):
            cp = pltpu.make_async_remote_copy(src_ref=p_ref, dst_ref=gat_ref.at[me], send_sem=ssem.at[k - 1],
                                              recv_sem=rsem.at[k - 1], device_id=_peer(pos, k), device_id_type=MESH)
            cp.start()
            sends.append(cp)
        for k in range(1, N_DEV):
            src = _lin(_peer(pos, k))
            pltpu.make_async_remote_copy(src_ref=p_ref, dst_ref=gat_ref.at[src], send_sem=ssem.at[k - 1],
                                         recv_sem=rsem.at[k - 1], device_id=pos, device_id_type=MESH).wait_recv()
        for cp in sends:
            cp.wait_send()
        acc = gat_ref[0]
        for j in range(1, N_DEV):
            acc = acc + gat_ref[j]
        sum_ref[...] = acc

    vm = pl.BlockSpec(memory_space=pltpu.VMEM)
    return pl.pallas_call(
        body, name="small_gather_sum",
        out_shape=(_S((N_DEV, 8, P8), F32), _S((8, P8), F32)),
        in_specs=[vm], out_specs=(vm, vm),
        scratch_shapes=[pltpu.SemaphoreType.DMA((N_DEV - 1,)), pltpu.SemaphoreType.DMA((N_DEV - 1,))],
        compiler_params=pltpu.CompilerParams(vmem_limit_bytes=VMEM_LIMIT),
    )(pack)


def _ag_weights(shards, after):
    n = len(shards)

    def body(*refs):
        ins, outs = refs[:n], refs[n + 1:2 * n + 1]
        ssem, rsem, lsem = refs[2 * n + 1:]
        x, y, c = pos = _my_pos()
        me = _lin(pos)
        sib = (x, y, 1 - c)
        chips = [(1 - x, y), (x, 1 - y), (1 - x, 1 - y)]

        def copy(a, k, block, to, src=None):
            return pltpu.make_async_remote_copy(
                src_ref=outs[a].at[block] if src is None else src, dst_ref=outs[a].at[block],
                send_sem=ssem.at[a * 7 + k], recv_sem=rsem.at[a * 7 + k], device_id=to, device_id_type=MESH)

        local = [pltpu.make_async_copy(ins[a], outs[a].at[me], lsem.at[a]) for a in range(n)]
        for cp in local:
            cp.start()
        first = []
        for a in range(n):
            first.append(copy(a, 0, me, sib, src=ins[a]))
            first += [copy(a, 1 + j, me, (*chip, c), src=ins[a]) for j, chip in enumerate(chips)]
        for cp in first:
            cp.start()
        passed = []
        for a in range(n):
            for j, chip in enumerate(chips):
                blk = _lin((*chip, c))
                copy(a, 1 + j, blk, pos).wait_recv()
                cp = copy(a, 4 + j, blk, sib)
                cp.start()
                passed.append(cp)
        for a in range(n):
            copy(a, 0, _lin(sib), pos).wait_recv()
            for j, chip in enumerate(chips):
                copy(a, 4 + j, _lin((*chip, 1 - c)), pos).wait_recv()
        for cp in first + passed:
            cp.wait_send()
        for cp in local:
            cp.wait()

    hbm = pl.BlockSpec(memory_space=pl.ANY)
    return pl.pallas_call(
        body, name="ag_weights",
        out_shape=tuple(_S((N_DEV,) + s.shape, s.dtype) for s in shards),
        in_specs=[hbm] * (n + 1), out_specs=tuple([hbm] * n),
        scratch_shapes=[pltpu.SemaphoreType.DMA((7 * n,)), pltpu.SemaphoreType.DMA((7 * n,)),
                        pltpu.SemaphoreType.DMA((n,))],
    )(*shards, after)


def _rs_d2d(blocked, name):
    n = len(blocked)

    def body(*refs):
        ins, outs = refs[:n], refs[n:2 * n]
        ssem, rsem = refs[2 * n:]
        x, y, c = pos = _my_pos()
        sib = (x, y, 1 - c)
        cps = []
        for a in range(n):
            for j in range(N_CHIP):
                cp = pltpu.make_async_remote_copy(
                    src_ref=ins[a].at[2 * j + (1 - c)], dst_ref=outs[a].at[j], send_sem=ssem.at[a * N_CHIP + j],
                    recv_sem=rsem.at[a * N_CHIP + j], device_id=sib, device_id_type=MESH)
                cp.start()
                cps.append(cp)
        for cp in cps:
            cp.wait_recv()
        for cp in cps:
            cp.wait_send()

    hbm = pl.BlockSpec(memory_space=pl.ANY)
    return pl.pallas_call(
        body, name=name,
        out_shape=tuple(_S((N_CHIP,) + b.shape[1:], b.dtype) for b in blocked),
        in_specs=[hbm] * n, out_specs=tuple([hbm] * n),
        scratch_shapes=[pltpu.SemaphoreType.DMA((N_CHIP * n,)), pltpu.SemaphoreType.DMA((N_CHIP * n,))],
    )(*blocked)


_HBM = pl.BlockSpec(memory_space=pltpu.HBM)
_SEM = pl.BlockSpec(memory_space=pltpu.SEMAPHORE)
_ANY = pl.BlockSpec(memory_space=pl.ANY)
_EFFECT = pltpu.SideEffectType.DATAFLOW_SIDE_EFFECTING


def _in_hbm(a):
    return pltpu.with_memory_space_constraint(a, pltpu.HBM)


def _plan_gather(n):
    def copies(pos):
        x, y, c = pos
        out = []
        for a in range(n):
            for dev in [(x, y, 1 - c)] + [(*_peer(pos, 2 * k)[:2], c) for k in range(1, N_CHIP)]:
                out.append((a, None, n + a, _lin(pos), dev, _lin(dev)))
        return out
    return copies


def _plan_forward(n):
    def copies(pos):
        x, y, c = pos
        out = []
        for a in range(n):
            for k in range(1, N_CHIP):
                tx, ty, _ = _peer(pos, 2 * k)
                out.append((a, _lin((tx, ty, c)), a, _lin((tx, ty, c)), (x, y, 1 - c), _lin((tx, ty, 1 - c))))
        return out
    return copies


def _plan_d2d(n):
    def copies(pos):
        x, y, c = pos
        return [(a, 2 * j + (1 - c), n + a, j, (x, y, 1 - c), j) for a in range(n) for j in range(N_CHIP)]
    return copies


def _plan_scatter(n):
    def copies(pos):
        x, y, c = pos
        out = []
        for a in range(n):
            for k in range(1, N_CHIP):
                tx, ty, _ = _peer(pos, 2 * k)
                out.append((a, 2 * tx + ty, n + a, 2 * x + y, (tx, ty, c), 2 * tx + ty))
        return out
    return copies


def _split_copy(refs, cp, ssem, rsem, i, arrival):
    si, s_slot, di, d_slot, dev, a_slot = cp
    return pltpu.make_async_remote_copy(
        src_ref=refs[si] if s_slot is None else refs[si].at[s_slot], dst_ref=refs[di].at[a_slot if arrival else d_slot],
        send_sem=ssem.at[i], recv_sem=rsem.at[i], device_id=dev, device_id_type=MESH)


def _split_start(arrays, copies, after, name):
    n = len(arrays)
    n_cp = len(copies((0, 0, 0)))

    def body(*refs):
        ssem, rsem, token = refs[n + 1], refs[n + 2], refs[-1]
        for i, cp in enumerate(copies(_my_pos())):
            _split_copy(refs, cp, ssem, rsem, i, False).start()
        token[...] = jnp.zeros_like(token)

    res = pl.pallas_call(
        body, name=name,
        out_shape=(pltpu.SemaphoreType.DMA((n_cp,)), pltpu.SemaphoreType.DMA((n_cp,)),
                   *[pltpu.HBM(a.shape, a.dtype) for a in arrays], _S((8, LANE), F32)),
        in_specs=[_HBM] * n + [_ANY],
        out_specs=(_SEM, _SEM, *[_HBM] * n, pl.BlockSpec(memory_space=pltpu.VMEM)),
        input_output_aliases={a: 2 + a for a in range(n)},
        compiler_params=pltpu.CompilerParams(has_side_effects=_EFFECT),
    )(*[_in_hbm(a) for a in arrays], after)
    return res[0], res[1], list(res[2:2 + n]), res[-1]


def _split_wait(ssem, rsem, arrays, copies, after, name):
    n = len(arrays)

    def body(*refs):
        for i, cp in enumerate(copies(_my_pos())):
            d = _split_copy(refs, cp, refs[n], refs[n + 1], i, True)
            d.wait_send()
            d.wait_recv()

    res = pl.pallas_call(
        body, name=name,
        out_shape=tuple(pltpu.HBM(a.shape, a.dtype) for a in arrays),
        in_specs=[_HBM] * n + [_SEM, _SEM, _ANY], out_specs=tuple([_HBM] * n),
        input_output_aliases={a: a for a in range(n)},
        compiler_params=pltpu.CompilerParams(has_side_effects=_EFFECT),
    )(*arrays, ssem, rsem, after)
    return list(res)


def _row_tile(R, itemsize_rows=16, cap=256):
    t = cap - cap % itemsize_rows
    while t >= itemsize_rows:
        if R % t == 0:
            return t
        t -= itemsize_rows
    return R


def _pair_sum(blocked, recv, core):
    _, R, C = blocked.shape
    tr = _row_tile(R)

    def body(ids_ref, a_ref, b_ref, o_ref):
        del ids_ref
        o_ref[...] = (a_ref[...] + b_ref[...]).astype(BF16)

    gs = pltpu.PrefetchScalarGridSpec(
        num_scalar_prefetch=1, grid=(N_CHIP, R // tr),
        in_specs=[pl.BlockSpec((1, tr, C), lambda j, r, ids: (2 * j + ids[0], r, 0)),
                  pl.BlockSpec((1, tr, C), lambda j, r, ids: (j, r, 0))],
        out_specs=pl.BlockSpec((1, tr, C), lambda j, r, ids: (j, r, 0)))
    return pl.pallas_call(body, name="pair_sum", grid_spec=gs, out_shape=_S((N_CHIP, R, C), BF16),
                          compiler_params=_params(("arbitrary", "arbitrary")))(core, blocked, recv)


def _adamw_math(w, g, m, v):
    m2 = ADAM_B1 * m + (1.0 - ADAM_B1) * g
    v2 = ADAM_B2 * v + (1.0 - ADAM_B2) * (g * g)
    m_hat = m2 / (1.0 - ADAM_B1 ** ADAM_STEP)
    v_hat = v2 / (1.0 - ADAM_B2 ** ADAM_STEP)
    delta = -ADAM_LR * (m_hat / (jnp.sqrt(v_hat) + ADAM_EPS) + ADAM_WD * w)
    return delta, m2, v2


def _sum_adamw_split(pairs, land, chips, w, m, v, transposed=False):
    R, C = w.shape
    tr = _row_tile(R, 128 if transposed else 16)

    def body(ids_ref, own_ref, p1_ref, p2_ref, p3_ref, w_ref, m_ref, v_ref, g_ref, d_ref, m2_ref, v2_ref):
        del ids_ref
        g = own_ref[0].astype(F32) + p1_ref[0].astype(F32) + p2_ref[0].astype(F32) + p3_ref[0].astype(F32)
        if transposed:
            g = g.T
        g_ref[...] = g
        d_ref[...], m2_ref[...], v2_ref[...] = _adamw_math(w_ref[...], g, m_ref[...], v_ref[...])

    t = pl.BlockSpec((tr, C), lambda r, ids: (r, 0))
    if transposed:
        slot = lambda k: pl.BlockSpec((1, C, tr), lambda r, ids: (ids[k], 0, r))
    else:
        slot = lambda k: pl.BlockSpec((1, tr, C), lambda r, ids: (ids[k], r, 0))
    gs = pltpu.PrefetchScalarGridSpec(num_scalar_prefetch=1, grid=(R // tr,),
                                      in_specs=[slot(0), slot(1), slot(2), slot(3), t, t, t], out_specs=(t, t, t, t))
    return pl.pallas_call(body, name="sum_adamw_split", grid_spec=gs, out_shape=tuple(_S((R, C), F32) for _ in range(4)),
                          compiler_params=_params(("arbitrary",)))(chips, pairs, land, land, land, w, m, v)


def _adamw(w, g, m, v):
    R, C = w.shape
    tr = _row_tile(R, 8)

    def body(w_ref, g_ref, m_ref, v_ref, d_ref, m2_ref, v2_ref):
        d_ref[...], m2_ref[...], v2_ref[...] = _adamw_math(w_ref[...], g_ref[...], m_ref[...], v_ref[...])

    t = pl.BlockSpec((tr, C), lambda r: (r, 0))
    return pl.pallas_call(body, name="adamw", grid=(R // tr,), in_specs=[t, t, t, t], out_specs=(t, t, t),
                          out_shape=tuple(_S((R, C), F32) for _ in range(3)),
                          compiler_params=_params(("arbitrary",)))(w, g, m, v)


def _ada_grad_adamw(cs16, dmod16, w, m, v):
    D, C6 = w.shape
    tr = _row_tile(D, 8, 256)

    def body(cs_ref, dm_ref, w_ref, m_ref, v_ref, g_ref, d_ref, m2_ref, v2_ref):
        g = _dot_tn(cs_ref[...].astype(BF16), dm_ref[...].astype(BF16))
        g_ref[...] = g
        d_ref[...], m2_ref[...], v2_ref[...] = _adamw_math(w_ref[...], g, m_ref[...], v_ref[...])

    t = pl.BlockSpec((tr, C6), lambda r: (r, 0))
    return pl.pallas_call(
        body, name="ada_grad_adamw", grid=(D // tr,),
        in_specs=[pl.BlockSpec((16, tr), lambda r: (0, r)), _acc((16, C6)), t, t, t], out_specs=(t, t, t, t),
        out_shape=tuple(_S((D, C6), F32) for _ in range(4)), compiler_params=_params(("arbitrary",)))(cs16, dmod16, w, m, v)


def _pick(n, cands):
    for c in cands:
        if n % c == 0:
            return c
    return n


def _matmul_tn(a, b, name, square_a=False):
    L, K = a.shape
    N = b.shape[1]
    bk = _pick(K, (1024, 512, 256, 128))
    bn = _pick(N, (1024, 768, 512, 256, 128))
    tl = _pick(L, (1024, 512, 256, 128))
    n_l = L // tl

    def body(a_ref, b_ref, o_ref):
        @pl.when(pl.program_id(2) == 0)
        def _():
            o_ref[...] = jnp.zeros_like(o_ref)
        av = a_ref[...]
        if square_a:
            av = av.astype(F32)
            av = av * av
        o_ref[...] += _dot_tn(av.astype(BF16), b_ref[...].astype(BF16))

    return pl.pallas_call(
        body, name=name, grid=(K // bk, N // bn, n_l),
        in_specs=[pl.BlockSpec((tl, bk), lambda k, n, l: (l, k)), pl.BlockSpec((tl, bn), lambda k, n, l: (l, n))],
        out_specs=pl.BlockSpec((bk, bn), lambda k, n, l: (k, n)), out_shape=_S((K, N), F32),
        compiler_params=_params(("arbitrary", "arbitrary", "arbitrary")))(a, b)


def _proj_rows(dims):
    W, CD, H, AW, KVW2 = dims["W"], dims["CD"], dims["H"], dims["AW"], dims["KVW2"]
    o_dt = W + CD
    o_q = o_dt + H
    return (0, W), (W, o_dt), (o_dt, o_dt + LANE), (o_q, o_q + AW), (o_q + AW, o_q + AW + KVW2)


def _ln_in_proj(x, g, b, sc, sh, w_t, dims):
    L, D = x.shape
    W, CD, AW, KVW2 = dims["W"], dims["CD"], dims["AW"], dims["KVW2"]
    PROJ = w_t.shape[0]
    tm = _pick(L, (MLP_TM, 128))
    r_z, r_xbc, r_dt, r_q, r_kv = _proj_rows(dims)

    def body(x_ref, g_ref, b_ref, sc_ref, sh_ref, w_ref, xhat_ref, rstd_ref, u1_ref, z_ref, xbc_ref, q_ref, kv_ref, dt_ref):
        xhat, rstd = _ln_fwd(x_ref[...])
        xhat_ref[...] = xhat
        rstd_ref[...] = rstd
        h0 = xhat * g_ref[...] + b_ref[...]
        u1 = (h0 * (1.0 + sc_ref[...]) + sh_ref[...]).astype(BF16)
        u1_ref[...] = u1
        z_ref[...] = _dot_nt(u1, w_ref[r_z[0]:r_z[1], :])
        xbc_ref[...] = _dot_nt(u1, w_ref[r_xbc[0]:r_xbc[1], :])
        q_ref[...] = _dot_nt(u1, w_ref[r_q[0]:r_q[1], :]).astype(BF16)
        kv_ref[...] = _dot_nt(u1, w_ref[r_kv[0]:r_kv[1], :]).astype(BF16)
        dt_ref[...] = _dot_nt(u1, w_ref[r_dt[0]:r_dt[1], :])

    v = _acc((1, D))
    return pl.pallas_call(
        body, name="ln_in_proj", grid=(L // tm,),
        in_specs=[_tile(D, tm), v, v, v, v, _res((PROJ, D))],
        out_specs=(_tile(D, tm), _tile(1, tm), _tile(D, tm), _tile(W, tm), _tile(CD, tm), _tile(AW, tm),
                   _tile(KVW2, tm), _tile(LANE, tm)),
        out_shape=(_S((L, D), F32), _S((L, 1), F32), _S((L, D), BF16), _S((L, W), F32), _S((L, CD), F32),
                   _S((L, AW), BF16), _S((L, KVW2), BF16), _S((L, LANE), F32)),
        compiler_params=_params(("arbitrary",)))(x, g, b, sc, sh, w_t)


def _conv_act(cur_ref, prev_ref, cw_ref, cb_ref, ext_ref, first):
    T = cur_ref.shape[0]
    ext_ref[0:HALO, :] = jnp.where(first, 0.0, prev_ref[...])
    ext_ref[HALO:HALO + T, :] = cur_ref[...]
    pre = cb_ref[...] + cw_ref[0:1, :] * ext_ref[HALO - 3:HALO - 3 + T, :]
    for k in range(1, CONV_K):
        pre = pre + cw_ref[k:k + 1, :] * ext_ref[HALO - 3 + k:HALO - 3 + k + T, :]
    return pre * _sigmoid(pre), pre


def _tri(T, upper=False):
    r = lax.broadcasted_iota(jnp.int32, (T, T), 0)
    c = lax.broadcasted_iota(jnp.int32, (T, T), 1)
    return (r <= c) if upper else (r >= c)


def _expand_heads(dst_ref, v, n_heads):
    for h in range(n_heads):
        dst_ref[:, h * HEAD_DIM:(h + 1) * HEAD_DIM] = jnp.broadcast_to(v[:, h:h + 1], (v.shape[0], HEAD_DIM))


def _head_reduce(v):
    wdt = v.shape[1]
    ch = lax.broadcasted_iota(jnp.int32, (wdt, LANE), 0)
    lo = lax.broadcasted_iota(jnp.int32, (wdt, LANE), 1) * HEAD_DIM
    onehot = ((ch >= lo) & (ch < lo + HEAD_DIM)).astype(BF16)
    hi = v.astype(BF16)
    rest = (v - hi.astype(F32)).astype(BF16)
    return _dot(hi, onehot) + _dot(rest, onehot)


def _conv_ssd(xbc, dt_raw, z, cw, cb, dtb, alog, dsk, nw, dims):
    L, CD = xbc.shape
    W, H, G, N = dims["W"], dims["H"], SSD_GROUPS, SSD_STATE
    T = CHUNK
    R = H // G
    GW = W // G
    nc = L // T
    HP = H * HEAD_DIM

    def body(xbc_ref, prev_ref, dt_ref, z_ref, cw_ref, cb_ref, dtb_ref, alog_ref, dsk_ref, nw_ref,
             y_ref, yn_ref, sp_ref, ext_ref, s_ref, ybuf_ref, dtx_ref, acx_ref, xb_ref):
        i = pl.program_id(0)

        @pl.when(i == 0)
        def _():
            s_ref[...] = jnp.zeros_like(s_ref)

        act, _ = _conv_act(xbc_ref, prev_ref, cw_ref, cb_ref, ext_ref, i == 0)
        xs = act[:, :W]
        dt = _softplus(dt_ref[...] + dtb_ref[...])
        a = dt * (-jnp.exp(alog_ref[...]))
        low = _tri(T)
        acum = _dot_hi(low.astype(F32), a)
        acum_t = _dot_hi(a.T, _tri(T, upper=True).astype(F32))
        _expand_heads(dtx_ref, dt, H)
        _expand_heads(acx_ref, acum, H)
        acx = acx_ref[...]
        lastx = acx[T - 1:T, :]
        xd = xs * dtx_ref[...]
        xb_ref[...] = xd.astype(BF16)
        xdb = (xd * jnp.exp(lastx - acx)).astype(BF16)
        ex = jnp.exp(acx)
        elx = jnp.exp(lastx)
        for g in range(G):
            gs = slice(g * GW, (g + 1) * GW)
            bgb = act[:, W + g * N:W + (g + 1) * N].astype(BF16)
            cgb = act[:, W + G * N + g * N:W + G * N + (g + 1) * N].astype(BF16)
            stg = s_ref[:, gs]
            sp_ref[0, :, gs] = stg
            yoff = ex[:, gs] * _dot(cgb, stg.astype(BF16))
            s_ref[:, gs] = stg * elx[:, gs] + _dot_tn(bgb, xdb[:, gs])
            cb_g = _dot_nt(cgb, bgb)
            for r in range(R):
                h = g * R + r
                hs = slice(h * HEAD_DIM, (h + 1) * HEAD_DIM)
                lm = jnp.where(low, jnp.exp(acum[:, h:h + 1] - acum_t[h:h + 1, :]), 0.0)
                ybuf_ref[:, hs] = _dot((cb_g * lm).astype(BF16), xb_ref[:, hs]) + yoff[:, r * HEAD_DIM:(r + 1) * HEAD_DIM]
        y = ybuf_ref[...] + dsk_ref[...] * xs
        y_ref[...] = y
        zz = z_ref[...]
        hh = y * (zz * _sigmoid(zz))
        for g in range(G):
            gs = slice(g * GW, (g + 1) * GW)
            hg = hh[:, gs]
            yn_ref[:, gs] = (hg * lax.rsqrt(_mean(hg * hg) + RMS_EPS) * nw_ref[:, gs]).astype(BF16)

    return pl.pallas_call(
        body, name="conv_ssd", grid=(nc,),
        in_specs=[_tile(CD, T), pl.BlockSpec((HALO, CD), lambda i: (jnp.maximum(i * (T // HALO) - 1, 0), 0)),
                  _tile(LANE, T), _tile(W, T), _acc((CONV_K, CD)), _acc((1, CD)), _acc((1, LANE)), _acc((1, LANE)),
                  _acc((1, W)), _acc((1, W))],
        out_specs=(_tile(W, T), _tile(W, T), pl.BlockSpec((1, N, HP), lambda i: (i, 0, 0))),
        out_shape=(_S((L, W), F32), _S((L, W), BF16), _S((nc, N, HP), F32)),
        scratch_shapes=[pltpu.VMEM((T + HALO, CD), F32), pltpu.VMEM((N, HP), F32), pltpu.VMEM((T, W), F32),
                        pltpu.VMEM((T, W), F32), pltpu.VMEM((T, W), F32), pltpu.VMEM((T, W), BF16)],
        compiler_params=_params(("arbitrary",)))(xbc, xbc, dt_raw, z, cw, cb, dtb, alog, dsk, nw)


def _attn_mask(T, i):
    r = lax.broadcasted_iota(jnp.int32, (T, 2 * T), 0)
    c = lax.broadcasted_iota(jnp.int32, (T, 2 * T), 1)
    dist = r + T - c
    valid = (dist >= 0) & (dist < CHUNK) & ((c >= T) | (i > 0))
    return dist.astype(F32), valid


def _attn_probs(s_raw, dist, valid, slope, sink, axis):
    s = s_raw * (HEAD_DIM ** -0.5) - slope * dist
    s = jnp.where(valid, s, NEG)
    m = jnp.maximum(jnp.max(s, axis=axis, keepdims=True), sink)
    p = jnp.exp(s - m)
    e_sink = jnp.exp(sink - m)
    inv = 1.0 / (jnp.sum(p, axis=axis, keepdims=True) + e_sink)
    return p * inv, e_sink * inv


def _kv_heads(kvc_ref, kvp_ref, g, n_kv):
    ks = slice(g * HEAD_DIM, (g + 1) * HEAD_DIM)
    vs = slice((n_kv + g) * HEAD_DIM, (n_kv + g + 1) * HEAD_DIM)
    kk = jnp.concatenate([kvp_ref[:, ks], kvc_ref[:, ks]], axis=0)
    vv = jnp.concatenate([kvp_ref[:, vs], kvc_ref[:, vs]], axis=0)
    return kk, vv


def _swa_fwd(q, kv, sinks, dims):
    L, AW = q.shape
    KV, KVW2 = dims["KV"], dims["KVW2"]
    T = CHUNK
    nb = L // T
    slopes = _alibi_slopes(dims["AH"])

    def body(q_ref, kvc_ref, kvp_ref, sink_ref, o_ref, qg_ref, p_ref):
        i = pl.program_id(0)
        dist, valid = _attn_mask(T, i)
        for g in range(KV):
            kk, vv = _kv_heads(kvc_ref, kvp_ref, g, KV)
            for r in range(GQA):
                h = g * GQA + r
                qg_ref[r * T:(r + 1) * T, :] = q_ref[:, h * HEAD_DIM:(h + 1) * HEAD_DIM]
            s_all = _dot_nt(qg_ref[...], kk)
            for r in range(GQA):
                h = g * GQA + r
                p, _ = _attn_probs(s_all[r * T:(r + 1) * T, :], dist, valid, slopes[h], sink_ref[h], -1)
                p_ref[r * T:(r + 1) * T, :] = p.astype(BF16)
            o_all = _dot(p_ref[...], vv)
            for r in range(GQA):
                h = g * GQA + r
                o_ref[:, h * HEAD_DIM:(h + 1) * HEAD_DIM] = o_all[r * T:(r + 1) * T, :].astype(BF16)

    return pl.pallas_call(
        body, name="swa_fwd", grid=(nb,),
        in_specs=[_tile(AW, T), _tile(KVW2, T), pl.BlockSpec((T, KVW2), lambda i: (jnp.maximum(i - 1, 0), 0)),
                  pl.BlockSpec(memory_space=pltpu.SMEM)],
        out_specs=_tile(AW, T), out_shape=_S((L, AW), BF16),
        scratch_shapes=[pltpu.VMEM((GQA * T, HEAD_DIM), BF16), pltpu.VMEM((GQA * T, 2 * T), BF16)],
        compiler_params=_params(("arbitrary",)))(q, kv, kv, sinks)


def _out_proj_ln1(yn, o, w_out, xhat0, vecs, alpha):
    L, W = yn.shape
    D = xhat0.shape[1]
    MIX = w_out.shape[0]
    tm = _pick(L, (MLP_TM, 128))

    def body(yn_ref, o_ref, w_ref, xh_ref, v_ref, mix_ref, xhat1_ref, rstd1_ref, u2_ref):
        mix = _dot(yn_ref[...], w_ref[0:W, :]) + _dot(o_ref[...], w_ref[W:MIX, :])
        mix_ref[...] = mix
        h0 = xh_ref[...] * v_ref[0:1, :] + v_ref[1:2, :]
        xhat1, rstd1 = _ln_fwd(alpha * h0 + (1.0 + v_ref[2:3, :]) * mix)
        xhat1_ref[...] = xhat1
        rstd1_ref[...] = rstd1
        h1 = xhat1 * v_ref[3:4, :] + v_ref[4:5, :]
        u2_ref[...] = (h1 * (1.0 + v_ref[5:6, :]) + v_ref[6:7, :]).astype(BF16)

    return pl.pallas_call(
        body, name="out_proj_ln1", grid=(L // tm,),
        in_specs=[_tile(W, tm), _tile(MIX - W, tm), _res((MIX, D)), _tile(D, tm), _acc((8, D))],
        out_specs=(_tile(D, tm), _tile(D, tm), _tile(1, tm), _tile(D, tm)),
        out_shape=(_S((L, D), F32), _S((L, D), F32), _S((L, 1), F32), _S((L, D), BF16)),
        compiler_params=_params(("arbitrary",)))(yn, o, w_out, xhat0, vecs)


def _mlp_loss(u2, w1, w2, xhat1, tgt, vecs, b1, alpha):
    L, D = xhat1.shape
    FF = w1.shape[1]
    tm = _pick(L, (MLP_TM, 128))
    sub = min(tm, MLP_SUB)
    fc = _pick(FF, (512, 256, 128))

    def body(u2_ref, w1_ref, w2_ref, xh_ref, t_ref, v_ref, b1_ref, rr_ref, dr2_ref, acc_ref, loss_ref):
        @pl.when(pl.program_id(0) == 0)
        def _():
            acc_ref[...] = jnp.zeros_like(acc_ref)
            loss_ref[...] = jnp.zeros_like(loss_ref)

        for s in range(tm // sub):
            rs = slice(s * sub, (s + 1) * sub)
            u2 = u2_ref[rs, :]
            f = jnp.zeros((sub, D), F32) + v_ref[5:6, :]
            for j in range(FF // fc):
                cs = slice(j * fc, (j + 1) * fc)
                rr = jnp.maximum(_dot(u2, w1_ref[:, cs]) + b1_ref[:, cs], 0.0)
                rr_ref[rs, cs] = rr.astype(BF16)
                f = f + _dot((rr * rr).astype(BF16), w2_ref[cs, :])
            xhat1 = xh_ref[rs, :]
            h1 = xhat1 * v_ref[0:1, :] + v_ref[1:2, :]
            xhat2, rstd2 = _ln_fwd(alpha * h1 + (1.0 + v_ref[2:3, :]) * f)
            e = xhat2 * v_ref[3:4, :] + v_ref[4:5, :] - t_ref[rs, :]
            loss_ref[...] += 0.5 * jnp.sum(_mean(e * e))
            dy = e * (1.0 / D)
            dr2 = _ln_bwd(dy * v_ref[3:4, :], xhat2, rstd2)
            dr2_ref[rs, :] = dr2
            acc_ref[0:1, :] += _colsum(dy * xhat2)
            acc_ref[1:2, :] += _colsum(dy)
            acc_ref[2:3, :] += _colsum(dr2 * f)

    return pl.pallas_call(
        body, name="mlp_loss", grid=(L // tm,),
        in_specs=[_tile(D, tm), _res((D, FF)), _res((FF, D)), _tile(D, tm), _tile(D, tm), _acc((8, D)), _acc((1, FF))],
        out_specs=(_tile(FF, tm), _tile(D, tm), _acc((8, D)), _acc((1, LANE))),
        out_shape=(_S((L, FF), BF16), _S((L, D), F32), _S((8, D), F32), _S((1, LANE), F32)),
        compiler_params=_params(("arbitrary",)))(u2, w1, w2, xhat1, tgt, vecs, b1)


def _mlp_bwd_a(dr2, rr, w2, g2):
    L, D = dr2.shape
    FF = w2.shape[0]
    tm = _pick(L, (MLP_TM, 128))
    fc = _pick(FF, (512, 256, 128))

    def body(dr2_ref, rr_ref, w2_ref, g2_ref, df_ref, da_ref, gb2_ref, gb1_ref):
        @pl.when(pl.program_id(0) == 0)
        def _():
            gb2_ref[...] = jnp.zeros_like(gb2_ref)
            gb1_ref[...] = jnp.zeros_like(gb1_ref)

        df = (1.0 + g2_ref[...]) * dr2_ref[...]
        gb2_ref[...] += _colsum(df)
        dfb = df.astype(BF16)
        df_ref[...] = dfb
        for j in range(FF // fc):
            cs = slice(j * fc, (j + 1) * fc)
            da = _dot_nt(dfb, w2_ref[cs, :]) * (2.0 * rr_ref[:, cs].astype(F32))
            gb1_ref[:, cs] += _colsum(da)
            da_ref[:, cs] = da.astype(BF16)

    return pl.pallas_call(
        body, name="mlp_bwd_a", grid=(L // tm,),
        in_specs=[_tile(D, tm), _tile(FF, tm), _res((FF, D)), _acc((1, D))],
        out_specs=(_tile(D, tm), _tile(FF, tm), _acc((1, D)), _acc((1, FF))),
        out_shape=(_S((L, D), BF16), _S((L, FF), BF16), _S((1, D), F32), _S((1, FF), F32)),
        compiler_params=_params(("arbitrary",)))(dr2, rr, w2, g2)


def _mlp_bwd_b(da, w1, dr2, xhat1, rstd1, mix, w_out, vecs, alpha, W):
    L, FF = da.shape
    D = dr2.shape[1]
    MIX = w_out.shape[0]
    tm = _pick(L, (MLP_TM, 128))

    def body(da_ref, w1_ref, dr2_ref, xh_ref, rs_ref, mix_ref, wo_ref, v_ref, dmix_ref, dh0_ref, dyn_ref, do_ref, acc_ref):
        @pl.when(pl.program_id(0) == 0)
        def _():
            acc_ref[...] = jnp.zeros_like(acc_ref)

        du2 = _dot_nt(da_ref[...], w1_ref[...])
        xhat1 = xh_ref[...]
        h1 = xhat1 * v_ref[0:1, :] + v_ref[1:2, :]
        acc_ref[0:1, :] += _colsum(du2 * h1)
        acc_ref[1:2, :] += _colsum(du2)
        dh1 = alpha * dr2_ref[...] + du2 * (1.0 + v_ref[2:3, :])
        acc_ref[2:3, :] += _colsum(dh1 * xhat1)
        acc_ref[3:4, :] += _colsum(dh1)
        dr1 = _ln_bwd(dh1 * v_ref[0:1, :], xhat1, rs_ref[...])
        acc_ref[4:5, :] += _colsum(dr1 * mix_ref[...])
        dh0_ref[...] = alpha * dr1
        dmix = ((1.0 + v_ref[3:4, :]) * dr1).astype(BF16)
        dmix_ref[...] = dmix
        dyn_ref[...] = _dot_nt(dmix, wo_ref[0:W, :])
        do_ref[...] = _dot_nt(dmix, wo_ref[W:MIX, :]).astype(BF16)

    return pl.pallas_call(
        body, name="mlp_bwd_b", grid=(L // tm,),
        in_specs=[_tile(FF, tm), _res((D, FF)), _tile(D, tm), _tile(D, tm), _tile(1, tm), _tile(D, tm), _res((MIX, D)),
                  _acc((8, D))],
        out_specs=(_tile(D, tm), _tile(D, tm), _tile(W, tm), _tile(MIX - W, tm), _acc((8, D))),
        out_shape=(_S((L, D), BF16), _S((L, D), F32), _S((L, W), F32), _S((L, MIX - W), BF16), _S((8, D), F32)),
        compiler_params=_params(("arbitrary",)))(da, w1, dr2, xhat1, rstd1, mix, w_out, vecs)


def _swa_bwd(q, kv, do, sinks, dims):
    L, AW = q.shape
    KV, KVW2 = dims["KV"], dims["KVW2"]
    T = CHUNK
    nb = L // T
    slopes = _alibi_slopes(dims["AH"])
    scale = HEAD_DIM ** -0.5

    def body(q_ref, kvc_ref, kvp_ref, do_ref, sink_ref, dq_ref, dkv_ref, dsink_ref, carry_ref,
             qg_ref, dog_ref, pt_ref, dst_ref):
        i = pl.program_id(0)

        @pl.when(i == 0)
        def _():
            carry_ref[...] = jnp.zeros_like(carry_ref)
            dsink_ref[...] = jnp.zeros_like(dsink_ref)

        @pl.when(i < nb)
        def _():
            c = lax.broadcasted_iota(jnp.int32, (2 * T, T), 0)
            r_ = lax.broadcasted_iota(jnp.int32, (2 * T, T), 1)
            dist_i = r_ + T - c
            valid = (dist_i >= 0) & (dist_i < CHUNK) & ((c >= T) | (i > 0))
            dist = dist_i.astype(F32)
            lane = lax.broadcasted_iota(jnp.int32, (1, LANE), 1)
            dsink = jnp.zeros((1, LANE), F32)
            dks, dvs = [], []
            for g in range(KV):
                kk, vv = _kv_heads(kvc_ref, kvp_ref, g, KV)
                for r in range(GQA):
                    hs = slice((g * GQA + r) * HEAD_DIM, (g * GQA + r + 1) * HEAD_DIM)
                    qg_ref[r * T:(r + 1) * T, :] = q_ref[:, hs]
                    dog_ref[r * T:(r + 1) * T, :] = do_ref[:, hs]
                st_all = _dot_nt(kk, qg_ref[...])
                dpt_all = _dot_nt(vv, dog_ref[...])
                for r in range(GQA):
                    h = g * GQA + r
                    cs = slice(r * T, (r + 1) * T)
                    p, p_sink = _attn_probs(st_all[:, cs], dist, valid, slopes[h], sink_ref[h], 0)
                    dp = dpt_all[:, cs]
                    delta = jnp.sum(p * dp, axis=0, keepdims=True)
                    pt_ref[:, cs] = p.astype(BF16)
                    dst_ref[:, cs] = (p * (dp - delta)).astype(BF16)
                    dsink = dsink + jnp.where(lane == h, -jnp.sum(p_sink * delta), 0.0)
                dst = dst_ref[...]
                dks.append(_dot(dst, qg_ref[...]) * scale)
                dvs.append(_dot(pt_ref[...], dog_ref[...]))
                dq_all = _dot_tn(dst, kk) * scale
                for r in range(GQA):
                    hs = slice((g * GQA + r) * HEAD_DIM, (g * GQA + r + 1) * HEAD_DIM)
                    dq_ref[:, hs] = dq_all[r * T:(r + 1) * T, :].astype(BF16)
            dkv = jnp.concatenate(dks + dvs, axis=1)
            dsink_ref[...] += dsink
            dkv_ref[...] = carry_ref[...] + dkv[0:T, :]
            carry_ref[...] = dkv[T:2 * T, :]

        @pl.when(i == nb)
        def _():
            dkv_ref[...] = carry_ref[...]

    last = nb - 1
    return pl.pallas_call(
        body, name="swa_bwd", grid=(nb + 1,),
        in_specs=[pl.BlockSpec((T, AW), lambda i: (jnp.minimum(i, last), 0)),
                  pl.BlockSpec((T, KVW2), lambda i: (jnp.minimum(i, last), 0)),
                  pl.BlockSpec((T, KVW2), lambda i: (jnp.clip(i - 1, 0, last), 0)),
                  pl.BlockSpec((T, AW), lambda i: (jnp.minimum(i, last), 0)),
                  pl.BlockSpec(memory_space=pltpu.SMEM)],
        out_specs=(pl.BlockSpec((T, AW), lambda i: (jnp.minimum(i, last), 0)),
                   pl.BlockSpec((T, KVW2), lambda i: (jnp.maximum(i - 1, 0), 0)), _acc((1, LANE))),
        out_shape=(_S((L, AW), BF16), _S((L, KVW2), F32), _S((1, LANE), F32)),
        scratch_shapes=[pltpu.VMEM((T, KVW2), F32), pltpu.VMEM((GQA * T, HEAD_DIM), BF16),
                        pltpu.VMEM((GQA * T, HEAD_DIM), BF16), pltpu.VMEM((2 * T, GQA * T), BF16),
                        pltpu.VMEM((2 * T, GQA * T), BF16)],
        compiler_params=_params(("arbitrary",)))(q, kv, kv, do, sinks)


def _ssd_bwd(dyn, y, z, xbc, dt_raw, sprev, cw, cb, dtb, alog, dsk, nw, dims):
    L, CD = xbc.shape
    W, H, G, N = dims["W"], dims["H"], SSD_GROUPS, SSD_STATE
    T = CHUNK
    R = H // G
    GW = W // G
    nc = L // T
    HP = H * HEAD_DIM

    def body(dyn_ref, y_ref, z_ref, xbc_ref, prev_ref, dt_ref, sp_ref, cw_ref, cb_ref, dtb_ref, alog_ref, dsk_ref, nw_ref,
             dz_ref, dpre_ref, ddt_ref, acc_ref, hacc_ref, ext_ref, ds_ref, dtx_ref, acx_ref, xb_ref, dyb_ref, r12_ref,
             dx_ref, rows_ref):
        i = pl.program_id(0)

        @pl.when(i == 0)
        def _():
            ds_ref[...] = jnp.zeros_like(ds_ref)
            acc_ref[...] = jnp.zeros_like(acc_ref)
            hacc_ref[...] = jnp.zeros_like(hacc_ref)

        act, pre = _conv_act(xbc_ref, prev_ref, cw_ref, cb_ref, ext_ref, i == nc - 1)
        xs = act[:, :W]
        dt_in = dt_ref[...] + dtb_ref[...]
        dt = _softplus(dt_in)
        a_neg = -jnp.exp(alog_ref[...])
        a = dt * a_neg
        low = _tri(T)
        upf = _tri(T, upper=True).astype(F32)
        acum = _dot_hi(low.astype(F32), a)
        acum_t = _dot_hi(a.T, upf)

        y = y_ref[...]
        zz = z_ref[...]
        sg = _sigmoid(zz)
        sz = zz * sg
        hh = y * sz
        dyn_v = dyn_ref[...]
        parts = []
        for g in range(G):
            gs = slice(g * GW, (g + 1) * GW)
            hg = hh[:, gs]
            hhat = hg * lax.rsqrt(_mean(hg * hg) + RMS_EPS)
            rg = lax.rsqrt(_mean(hg * hg) + RMS_EPS)
            acc_ref[0:1, gs] += _colsum(dyn_v[:, gs] * hhat)
            dhhat = dyn_v[:, gs] * nw_ref[:, gs]
            parts.append(rg * (dhhat - hhat * _mean(dhhat * hhat)))
        dhh = jnp.concatenate(parts, axis=1)
        dy = dhh * sz
        dz_ref[...] = (dhh * y * (sg * (1.0 + zz * (1.0 - sg)))).astype(BF16)
        acc_ref[1:2, :] += _colsum(dy * xs)
        dyb_ref[...] = dy.astype(BF16)

        _expand_heads(dtx_ref, dt, H)
        _expand_heads(acx_ref, acum, H)
        dtx = dtx_ref[...]
        acx = acx_ref[...]
        lastx = acx[T - 1:T, :]
        ex = jnp.exp(acx)
        decx = jnp.exp(lastx - acx)
        elx = jnp.exp(lastx)
        xd = xs * dtx
        xb_ref[...] = xd.astype(BF16)
        xdecb = (xd * decx).astype(BF16)
        dgb = (ex * dy).astype(BF16)
        rows_ref[...] = jnp.zeros_like(rows_ref)

        lane = lax.broadcasted_iota(jnp.int32, (T, LANE), 1)
        sub = lax.broadcasted_iota(jnp.int32, (T, LANE), 0)
        subr = lax.broadcasted_iota(jnp.int32, (LANE, T), 0)
        da_col = jnp.zeros((T, LANE), F32)
        da_row = jnp.zeros((LANE, T), F32)
        dbs, dcs = [], []
        for g in range(G):
            gs = slice(g * GW, (g + 1) * GW)
            bgb = act[:, W + g * N:W + (g + 1) * N].astype(BF16)
            cgb = act[:, W + G * N + g * N:W + G * N + (g + 1) * N].astype(BF16)
            stg = sp_ref[0, :, gs]
            stb = stg.astype(BF16)
            dsn = ds_ref[:, gs]
            dsnb = dsn.astype(BF16)
            gm = _dot(cgb, stb)
            dc = _dot_nt(dgb[:, gs], stb)
            dsp = _dot_tn(cgb, dgb[:, gs])
            dxs_ = decx[:, gs] * _dot(bgb, dsnb)
            db = _dot_nt(xdecb[:, gs], dsnb)
            xdg = xd[:, gs]
            r12_ref[:, gs] = dy[:, gs] * ex[:, gs] * gm - xdg * dxs_
            rows_ref[0:1, gs] = _colsum(dsn * stg) * elx[:, gs]
            rows_ref[1:2, gs] = _colsum(xdg * dxs_)
            ds_ref[:, gs] = dsp + dsn * elx[:, gs]
            cb_g = _dot_nt(cgb, bgb)
            dcb = jnp.zeros((T, T), F32)
            for r in range(R):
                h = g * R + r
                hs = slice(h * HEAD_DIM, (h + 1) * HEAD_DIM)
                lm = jnp.where(low, jnp.exp(acum[:, h:h + 1] - acum_t[h:h + 1, :]), 0.0)
                mm = cb_g * lm
                dyb = dyb_ref[:, hs]
                dm = _dot_nt(dyb, xb_ref[:, hs])
                dx_ref[:, hs] = dxs_[:, r * HEAD_DIM:(r + 1) * HEAD_DIM] + _dot_tn(mm.astype(BF16), dyb)
                dcb = dcb + dm * lm
                qm = dm * mm
                da_col = jnp.where(lane == h, jnp.sum(qm, axis=1, keepdims=True), da_col)
                da_row = jnp.where(subr == h, jnp.sum(qm, axis=0, keepdims=True), da_row)
            dcbb = dcb.astype(BF16)
            dcs.append(dc + _dot(dcbb, bgb))
            dbs.append(db + _dot_tn(dcbb, cgb))
        dx = dx_ref[...]
        rows = _head_reduce(rows_ref[...])
        dlast = rows[0:1, :] + rows[1:2, :]
        da_col = da_col + _head_reduce(r12_ref[...]) + jnp.where(sub == T - 1, dlast, 0.0)
        dacum = da_col - da_row.T
        da = _dot_hi(upf, dacum)
        ddt = _head_reduce(dx * xs) + da * a_neg
        hacc_ref[1:2, :] += _colsum(da * dt) * a_neg
        ddt_raw = ddt * _sigmoid(dt_in)
        hacc_ref[0:1, :] += _colsum(ddt_raw)
        ddt_ref[...] = ddt_raw
        dact = jnp.concatenate([dsk_ref[...] * dy + dx * dtx] + dbs + dcs, axis=1)
        spre = _sigmoid(pre)
        dpre_ref[...] = dact * (spre * (1.0 + pre * (1.0 - spre)))

        @pl.when(i == nc - 1)
        def _():
            ch = lax.broadcasted_iota(jnp.int32, (W, LANE), 0)
            lo = lax.broadcasted_iota(jnp.int32, (W, LANE), 1) * HEAD_DIM
            hacc_ref[2:3, :] = _dot_hi(acc_ref[1:2, :], ((ch >= lo) & (ch < lo + HEAD_DIM)).astype(F32))

    rev = lambda i: (nc - 1 - i, 0)
    return pl.pallas_call(
        body, name="ssd_bwd", grid=(nc,),
        in_specs=[pl.BlockSpec((T, W), rev), pl.BlockSpec((T, W), rev), pl.BlockSpec((T, W), rev), pl.BlockSpec((T, CD), rev),
                  pl.BlockSpec((HALO, CD), lambda i: (jnp.maximum((nc - 1 - i) * (T // HALO) - 1, 0), 0)),
                  pl.BlockSpec((T, LANE), rev), pl.BlockSpec((1, N, HP), lambda i: (nc - 1 - i, 0, 0)),
                  _acc((CONV_K, CD)), _acc((1, CD)), _acc((1, LANE)), _acc((1, LANE)), _acc((1, W)), _acc((1, W))],
        out_specs=(pl.BlockSpec((T, W), rev), pl.BlockSpec((T, CD), rev), pl.BlockSpec((T, LANE), rev), _acc((8, W)),
                   _acc((8, LANE))),
        out_shape=(_S((L, W), BF16), _S((L, CD), F32), _S((L, LANE), F32), _S((8, W), F32), _S((8, LANE), F32)),
        scratch_shapes=[pltpu.VMEM((T + HALO, CD), F32), pltpu.VMEM((N, HP), F32), pltpu.VMEM((T, W), F32),
                        pltpu.VMEM((T, W), F32), pltpu.VMEM((T, W), BF16), pltpu.VMEM((T, W), BF16), pltpu.VMEM((T, W), F32),
                        pltpu.VMEM((T, W), F32), pltpu.VMEM((8, W), F32)],
        compiler_params=_params(("arbitrary",)))(dyn, y, z, xbc, xbc, dt_raw, sprev, cw, cb, dtb, alog, dsk, nw)


def _conv_bwd(dpre, xbc, cw):
    L, CD = xbc.shape
    tm = _pick(L, (MLP_TM, 128))
    nt = L // tm
    hb = tm // HALO

    def body(dp_ref, dn_ref, u_ref, up_ref, cw_ref, du_ref, acc_ref, extu_ref, extd_ref):
        i = pl.program_id(0)

        @pl.when(i == 0)
        def _():
            acc_ref[...] = jnp.zeros_like(acc_ref)

        dp = dp_ref[...]
        extu_ref[0:HALO, :] = jnp.where(i == 0, 0.0, up_ref[...])
        extu_ref[HALO:HALO + tm, :] = u_ref[...]
        extd_ref[0:tm, :] = dp
        extd_ref[tm:tm + HALO, :] = jnp.where(i == nt - 1, 0.0, dn_ref[...])
        du = cw_ref[CONV_K - 1:CONV_K, :] * dp
        acc_ref[CONV_K - 1:CONV_K, :] += _colsum(dp * u_ref[...])
        for k in range(CONV_K - 1):
            s = CONV_K - 1 - k
            du = du + cw_ref[k:k + 1, :] * extd_ref[s:s + tm, :]
            acc_ref[k:k + 1, :] += _colsum(dp * extu_ref[HALO - s:HALO - s + tm, :])
        acc_ref[CONV_K:CONV_K + 1, :] += _colsum(dp)
        du_ref[...] = du.astype(BF16)

    return pl.pallas_call(
        body, name="conv_bwd", grid=(nt,),
        in_specs=[_tile(CD, tm), pl.BlockSpec((HALO, CD), lambda i: (jnp.minimum((i + 1) * hb, nt * hb - 1), 0)),
                  _tile(CD, tm), pl.BlockSpec((HALO, CD), lambda i: (jnp.maximum(i * hb - 1, 0), 0)), _acc((CONV_K, CD))],
        out_specs=(_tile(CD, tm), _acc((8, CD))),
        out_shape=(_S((L, CD), BF16), _S((8, CD), F32)),
        scratch_shapes=[pltpu.VMEM((tm + HALO, CD), F32), pltpu.VMEM((tm + HALO, CD), F32)],
        compiler_params=_params(("arbitrary",)))(dpre, dpre, xbc, xbc, cw)


def _in_proj_bwd(dz, dxbc, dq, dkv, ddt, w_t, xhat0, rstd0, dh0p, vecs, dims):
    L, D = xhat0.shape
    W, CD, AW, KVW2 = dims["W"], dims["CD"], dims["AW"], dims["KVW2"]
    PROJ = w_t.shape[0]
    tm = _pick(L, (MLP_TM, 128))
    r_z, r_xbc, r_dt, r_q, r_kv = _proj_rows(dims)

    def body(dz_ref, dxbc_ref, dq_ref, dkv_ref, ddt_ref, w_ref, xh_ref, rs_ref, dh0_ref, v_ref, gx_ref, acc_ref):
        @pl.when(pl.program_id(0) == 0)
        def _():
            acc_ref[...] = jnp.zeros_like(acc_ref)

        du1 = _dot(dz_ref[...], w_ref[r_z[0]:r_z[1], :])
        du1 = du1 + _dot(dxbc_ref[...], w_ref[r_xbc[0]:r_xbc[1], :])
        du1 = du1 + _dot(dq_ref[...], w_ref[r_q[0]:r_q[1], :])
        du1 = du1 + _dot(dkv_ref[...].astype(BF16), w_ref[r_kv[0]:r_kv[1], :])
        du1 = du1 + _dot(ddt_ref[...].astype(BF16), w_ref[r_dt[0]:r_dt[1], :])
        xhat0 = xh_ref[...]
        h0 = xhat0 * v_ref[0:1, :] + v_ref[1:2, :]
        acc_ref[0:1, :] += _colsum(du1 * h0)
        acc_ref[1:2, :] += _colsum(du1)
        dh0 = dh0_ref[...] + du1 * (1.0 + v_ref[2:3, :])
        acc_ref[2:3, :] += _colsum(dh0 * xhat0)
        acc_ref[3:4, :] += _colsum(dh0)
        gx_ref[...] = _ln_bwd(dh0 * v_ref[0:1, :], xhat0, rs_ref[...])

    return pl.pallas_call(
        body, name="in_proj_bwd", grid=(L // tm,),
        in_specs=[_tile(W, tm), _tile(CD, tm), _tile(AW, tm), _tile(KVW2, tm), _tile(LANE, tm), _res((PROJ, D)),
                  _tile(D, tm), _tile(1, tm), _tile(D, tm), _acc((8, D))],
        out_specs=(_tile(D, tm), _acc((8, D))),
        out_shape=(_S((L, D), F32), _S((8, D), F32)),
        compiler_params=_params(("arbitrary",)))(dz, dxbc, dq, dkv, ddt, w_t, xhat0, rstd0, dh0p, vecs)


_WEIGHTS = ['ln_in_g', 'ln_in_b', 'ada_w', 'ada_b', 'w_in', 'conv_w', 'conv_b', 'dt_bias', 'a_log', 'd_skip', 'ssd_norm_w',
            'attn_sinks', 'w_out', 'ln1_g', 'ln1_b', 'w_ff1', 'b_ff1', 'w_ff2', 'b_ff2', 'ln2_g', 'ln2_b']
_BIG = ('w_in', 'w_out', 'w_ff1', 'w_ff2')
_SMALL = ('ada_b', 'ln_in_g', 'ln_in_b', 'conv_b', 'dt_bias', 'a_log', 'd_skip', 'ssd_norm_w', 'attn_sinks', 'ln1_g', 'ln1_b',
          'b_ff1', 'b_ff2', 'ln2_g', 'ln2_b')


def _pad_lanes(v, n=None):
    v = v.reshape(1, -1)
    n = n or -(-v.shape[1] // LANE) * LANE
    return jnp.pad(v, ((0, 0), (0, n - v.shape[1])))


def _vec8(rows, D):
    rows = [r.reshape(1, D) for r in rows]
    return jnp.concatenate(rows + [jnp.zeros((8 - len(rows), D), F32)], axis=0)


def _pack(segs):
    flat, offs, sizes, o = [], [], [], 0
    for s in segs:
        p = _pad_lanes(s)
        flat.append(p)
        offs.append(o)
        sizes.append(s.size)
        o += p.shape[1]
    total = -(-o // (8 * LANE)) * (8 * LANE)
    if total > o:
        flat.append(jnp.zeros((1, total - o), F32))
    return jnp.concatenate(flat, axis=1).reshape(8, total // 8), offs, sizes


def kernel(x, c, ln_in_g, ln_in_b, ada_w, ada_b, w_in, conv_w, conv_b, dt_bias, a_log, d_skip, ssd_norm_w, attn_sinks, w_out, ln1_g, ln1_b, w_ff1, b_ff1, w_ff2, b_ff2, ln2_g, ln2_b, loss_target, m_ln_in_g, m_ln_in_b, m_ada_w, m_ada_b, m_w_in, m_conv_w, m_conv_b, m_dt_bias, m_a_log, m_d_skip, m_ssd_norm_w, m_attn_sinks, m_w_out, m_ln1_g, m_ln1_b, m_w_ff1, m_b_ff1, m_w_ff2, m_b_ff2, m_ln2_g, m_ln2_b, v_ln_in_g, v_ln_in_b, v_ada_w, v_ada_b, v_w_in, v_conv_w, v_conv_b, v_dt_bias, v_a_log, v_d_skip, v_ssd_norm_w, v_attn_sinks, v_w_out, v_ln1_g, v_ln1_b, v_w_ff1, v_b_ff1, v_w_ff2, v_b_ff2, v_ln2_g, v_ln2_b):
    wts = dict(ln_in_g=ln_in_g, ln_in_b=ln_in_b, ada_w=ada_w, ada_b=ada_b, w_in=w_in, conv_w=conv_w, conv_b=conv_b,
               dt_bias=dt_bias, a_log=a_log, d_skip=d_skip, ssd_norm_w=ssd_norm_w, attn_sinks=attn_sinks, w_out=w_out,
               ln1_g=ln1_g, ln1_b=ln1_b, w_ff1=w_ff1, b_ff1=b_ff1, w_ff2=w_ff2, b_ff2=b_ff2, ln2_g=ln2_g, ln2_b=ln2_b)
    ms = dict(ln_in_g=m_ln_in_g, ln_in_b=m_ln_in_b, ada_w=m_ada_w, ada_b=m_ada_b, w_in=m_w_in, conv_w=m_conv_w,
              conv_b=m_conv_b, dt_bias=m_dt_bias, a_log=m_a_log, d_skip=m_d_skip, ssd_norm_w=m_ssd_norm_w,
              attn_sinks=m_attn_sinks, w_out=m_w_out, ln1_g=m_ln1_g, ln1_b=m_ln1_b, w_ff1=m_w_ff1, b_ff1=m_b_ff1,
              w_ff2=m_w_ff2, b_ff2=m_b_ff2, ln2_g=m_ln2_g, ln2_b=m_ln2_b)
    vs = dict(ln_in_g=v_ln_in_g, ln_in_b=v_ln_in_b, ada_w=v_ada_w, ada_b=v_ada_b, w_in=v_w_in, conv_w=v_conv_w,
              conv_b=v_conv_b, dt_bias=v_dt_bias, a_log=v_a_log, d_skip=v_d_skip, ssd_norm_w=v_ssd_norm_w,
              attn_sinks=v_attn_sinks, w_out=v_w_out, ln1_g=v_ln1_g, ln1_b=v_ln1_b, w_ff1=v_w_ff1, b_ff1=v_b_ff1,
              w_ff2=v_w_ff2, b_ff2=v_b_ff2, ln2_g=v_ln2_g, ln2_b=v_ln2_b)

    L, D = x.shape[1], x.shape[2]
    depth = w_in.shape[0]
    assert depth == 1 and x.shape[0] == 1 and L % CHUNK == 0
    W = D
    H = W // HEAD_DIM
    CD = W + 2 * SSD_GROUPS * SSD_STATE
    AW = D
    AH = AW // HEAD_DIM
    KV = AH // GQA
    KVW2 = 2 * KV * HEAD_DIM
    PROJ = W + CD + H + AW + KVW2
    FF = w_ff1.shape[2] * N_DEV
    MIX = w_out.shape[1] * N_DEV
    assert w_in.shape[2] * N_DEV == PROJ and MIX == W + AW and H <= LANE and AH <= LANE
    dims = dict(W=W, H=H, CD=CD, AW=AW, AH=AH, KV=KV, KVW2=KVW2)
    alpha = (2.0 * depth) ** 0.25
    C6 = ada_w.shape[2]
    CW = conv_w.shape[2]

    ax, ay, ac = _my_pos()
    me = 4 * ax + 2 * ay + ac
    x2 = x.reshape(L, D)
    tgt = loss_target.reshape(L, D)
    r1 = lambda a: a.reshape(1, -1)

    ada_b_cols = lax.dynamic_slice(ada_b, (0, me * C6), (1, C6))
    cs_all, mod = _mod_fwd(c, ada_w[0], ada_b_cols)
    sh1, sc1, g1, sh2, sc2, g2 = [r1(t) for t in jnp.split(mod.reshape(-1), 6)]

    wg_in, cwg = _ag_weights([w_in[0].T.astype(BF16), conv_w[0]], cs_all)
    shards2 = [w_out[0].astype(BF16), w_ff1[0].astype(BF16), w_ff2[0].astype(BF16)]
    lands2 = [lax.dynamic_update_slice(lax.empty((N_DEV,) + s.shape, s.dtype), s[None], (me, 0, 0)) for s in shards2]
    ag_ss, ag_rs, ag_arr, ag_token = _split_start(shards2 + lands2, _plan_gather(3), cwg, "ag_ici_start")
    sh1 = sh1 + ag_token[0:1, 0:1]
    w_pad = wg_in.reshape(PROJ, D)
    cw_full = cwg.transpose(1, 0, 2).reshape(CONV_K, CD)

    dtb = _pad_lanes(dt_bias, LANE)
    alog = _pad_lanes(a_log, LANE)
    dsk = jnp.repeat(d_skip.reshape(-1), HEAD_DIM).reshape(1, W)
    sinks = attn_sinks.reshape(-1)
    g_in, b_in = r1(ln_in_g), r1(ln_in_b)

    xhat0, rstd0, u1, z, xbc, q, kv, dt_raw = _ln_in_proj(x2, g_in, b_in, sc1, sh1, w_pad, dims)
    y, yn, sprev = _conv_ssd(xbc, dt_raw, z, cw_full, conv_b, dtb, alog, dsk, ssd_norm_w, dims)
    ag_arr = _split_wait(ag_ss, ag_rs, ag_arr, _plan_gather(3), yn, "ag_ici_wait")
    fw_ss, fw_rs, ag_land, fw_token = _split_start(ag_arr[3:], _plan_forward(3), yn, "ag_fwd_start")
    o = _swa_fwd(q, kv, sinks + fw_token[0, 0], dims)
    wg_out, wg_ff1, wg_ff2 = _split_wait(fw_ss, fw_rs, ag_land, _plan_forward(3), o, "ag_fwd_wait")
    w_out_full = wg_out.reshape(MIX, D)
    w1_full = wg_ff1.transpose(1, 0, 2).reshape(D, FF)
    w2_full = wg_ff2.reshape(FF, D)
    mix, xhat1, rstd1, u2 = _out_proj_ln1(yn, o, w_out_full, xhat0, _vec8([g_in, b_in, g1, ln1_g, ln1_b, sc2, sh2], D), alpha)
    rr, dr2, acc_f, loss_loc = _mlp_loss(u2, w1_full, w2_full, xhat1, tgt,
                                         _vec8([ln1_g, ln1_b, g2, ln2_g, ln2_b, b_ff2], D), b_ff1, alpha)

    df, da, gb2, gb1 = _mlp_bwd_a(dr2, rr, w2_full, g2)
    gw_ff2 = _matmul_tn(rr, df, "gw_ff2", square_a=True)
    gw_ff1t = _matmul_tn(da, u2, "gw_ff1")
    dmix, dh0p, dyn, do, acc_b = _mlp_bwd_b(da, w1_full, dr2, xhat1, rstd1, mix, w_out_full,
                                            _vec8([ln1_g, ln1_b, sc2, g1], D), alpha, W)
    gw_out = jnp.concatenate([_matmul_tn(yn, dmix, "gw_out_ssd"), _matmul_tn(o, dmix, "gw_out_attn")], axis=0)

    core = jnp.reshape(ac, (1,)).astype(jnp.int32)
    blocked1 = [gw_out.reshape(N_DEV, MIX // N_DEV, D), gw_ff1t.reshape(N_DEV, FF // N_DEV, D),
                gw_ff2.reshape(N_DEV, FF // N_DEV, D)]
    recv1 = [lax.empty((N_CHIP,) + b.shape[1:], b.dtype) for b in blocked1]
    d_ss, d_rs, d_arr, d_token = _split_start(blocked1 + recv1, _plan_d2d(3), do, "rs_d2d_start")
    dq, dkv, dsink = _swa_bwd(q, kv, do, sinks + d_token[0, 0], dims)
    d_arr = _split_wait(d_ss, d_rs, d_arr, _plan_d2d(3), dq, "rs_d2d_wait")
    pairs1 = [_pair_sum(b, r, core) for b, r in zip(d_arr[:3], d_arr[3:])]
    lands1 = [lax.empty(p.shape, p.dtype) for p in pairs1]
    rs_ss, rs_rs, rs_arr, rs_token = _split_start(pairs1 + lands1, _plan_scatter(3), dq, "rs_ici_start")

    dz, dpre, ddt, acc_s, hacc = _ssd_bwd(dyn, y, z, xbc, dt_raw, sprev, cw_full, conv_b, dtb + rs_token[0:1, 0:1], alog, dsk,
                                          ssd_norm_w, dims)
    dxbc, acc_c = _conv_bwd(dpre, xbc, cw_full)
    gz = _matmul_tn(dz, u1, "gw_in_z")
    gxbc = _matmul_tn(dxbc, u1, "gw_in_xbc")
    gq = _matmul_tn(dq, u1, "gw_in_q")
    gkv = _matmul_tn(dkv, u1, "gw_in_kv")
    gdt = _matmul_tn(ddt, u1, "gw_in_dt")
    gw_in = jnp.concatenate([gz, gxbc, gdt[:H], gq, gkv], axis=0)

    blocked2 = [gw_in.reshape(N_DEV, PROJ // N_DEV, D)]
    pairs2 = [_pair_sum(b, r, core) for b, r in zip(blocked2, _rs_d2d(blocked2, "rs_d2d_2"))]
    lands2 = [lax.empty(p.shape, p.dtype) for p in pairs2]
    r2_ss, r2_rs, r2_arr, r2_token = _split_start(pairs2 + lands2, _plan_scatter(1), gz, "rs_ici_start_2")
    grad_x, acc_i = _in_proj_bwd(dz, dxbc, dq, dkv, ddt, w_pad, xhat0, rstd0, dh0p,
                                 _vec8([g_in, b_in, sc1], D) + r2_token[0:1, 0:1], dims)

    dmod = jnp.concatenate([acc_i[1], acc_i[0], acc_b[4], acc_b[1], acc_b[0], acc_f[2]])
    small_g = dict(ada_b=dmod, ln_in_g=acc_i[2], ln_in_b=acc_i[3], conv_b=acc_c[CONV_K], dt_bias=hacc[0, :H], a_log=hacc[1, :H],
                   d_skip=hacc[2, :H], ssd_norm_w=acc_s[0], attn_sinks=dsink[0, :AH], ln1_g=acc_b[2], ln1_b=acc_b[3],
                   b_ff1=gb1[0], b_ff2=gb2[0], ln2_g=acc_f[0], ln2_b=acc_f[1])
    segs = [small_g[n] for n in _SMALL] + [acc_c[:CONV_K].reshape(-1), loss_loc[0, :1]]
    pack, offs, sizes = _pack(segs)
    gathered, summed = _small_gather_sum(pack)
    gathered = gathered.reshape(N_DEV, -1)
    summed = summed.reshape(-1)
    seg = lambda k: summed[offs[k]:offs[k] + sizes[k]]
    grads = {n: seg(k).reshape(wts[n].shape) for k, n in enumerate(_SMALL)}
    gcw_full = seg(len(_SMALL)).reshape(CONV_K, CD)
    grads['conv_w'] = lax.dynamic_slice(gcw_full, (0, me * CW), (CONV_K, CW)).reshape(conv_w.shape)
    loss = seg(len(_SMALL) + 1)[0]

    names = list(_SMALL) + ['conv_w']
    pw, poffs, psizes = _pack([wts[n] for n in names])
    pg, _, _ = _pack([grads[n] for n in names])
    pm, _, _ = _pack([ms[n] for n in names])
    pv, _, _ = _pack([vs[n] for n in names])
    pd, pm2, pv2 = [t.reshape(-1) for t in _adamw(pw, pg, pm, pv)]
    deltas, new_m, new_v = {}, {}, {}
    for k, n in enumerate(names):
        sl = slice(poffs[k], poffs[k] + psizes[k])
        deltas[n], new_m[n], new_v[n] = (t[sl].reshape(wts[n].shape) for t in (pd, pm2, pv2))

    dmod_cols = lax.dynamic_slice(gathered, (0, offs[0] + me * C6), (N_DEV, C6))
    pad16 = lambda t: jnp.concatenate([t, jnp.zeros((16 - N_DEV,) + t.shape[1:], t.dtype)], axis=0)
    g_, d_, m_, v_ = _ada_grad_adamw(pad16(cs_all), pad16(dmod_cols), ada_w[0], m_ada_w[0], v_ada_w[0])
    grads['ada_w'], deltas['ada_w'], new_m['ada_w'], new_v['ada_w'] = (t[None] for t in (g_, d_, m_, v_))

    rs_arr = _split_wait(rs_ss, rs_rs, rs_arr, _plan_scatter(3), g_, "rs_ici_wait")
    mychip = 2 * ax + ay
    chips = jnp.stack([(mychip + k) % N_CHIP for k in range(N_CHIP)]).astype(jnp.int32)
    for n, own, land in zip(('w_out', 'w_ff1', 'w_ff2'), rs_arr[:3], rs_arr[3:]):
        g_, d_, m_, v_ = _sum_adamw_split(own, land, chips, wts[n][0], ms[n][0], vs[n][0], transposed=(n == 'w_ff1'))
        grads[n], deltas[n], new_m[n], new_v[n] = (t[None] for t in (g_, d_, m_, v_))
    r2_arr = _split_wait(r2_ss, r2_rs, r2_arr, _plan_scatter(1), g_, "rs_ici_wait_2")
    g_, d_, m_, v_ = _sum_adamw_split(r2_arr[0], r2_arr[1], chips, wts['w_in'][0].T, ms['w_in'][0].T, vs['w_in'][0].T)
    grads['w_in'], deltas['w_in'], new_m['w_in'], new_v['w_in'] = (t.T[None] for t in (g_, d_, m_, v_))

    return (loss, grad_x.reshape(x.shape), *[grads[n] for n in _WEIGHTS], *[deltas[n] for n in _WEIGHTS],
            *[new_m[n] for n in _WEIGHTS], *[new_v[n] for n in _WEIGHTS])
```

```python
import functools
import math

import numpy as np
import jax
import jax.numpy as jnp
from jax import lax
from jax.experimental import pallas as pl
from jax.experimental.pallas import tpu as pltpu

F32 = jnp.float32
BF16 = jnp.bfloat16
MESH = pl.DeviceIdType.MESH

N_DEV = 8
N_CHIP = 4
HEAD_DIM = 64
SSD_GROUPS = 2
SSD_STATE = 128
CHUNK = 128
CONV_K = 4
GQA = 8
LANE = 128
HALO = 8
LN_EPS = 1e-5
RMS_EPS = 1e-5
NEG = -1e30
ADAM_LR, ADAM_B1, ADAM_B2, ADAM_EPS, ADAM_WD, ADAM_STEP = 0.001, 0.9, 0.999, 1e-08, 0.01, 10
V7X_VMEM_BYTES = 64 * 1024 * 1024
VMEM_LIMIT = V7X_VMEM_BYTES - 8 * 1024 * 1024
HI = lax.Precision.HIGHEST
MLP_TM = 512
MLP_SUB = 512
MLP_FC = 512


def _alibi_slopes(n):
    def pow2(m):
        start = 2.0 ** (-8.0 / m)
        return [start ** (i + 1) for i in range(m)]
    if math.log2(n).is_integer():
        s = pow2(n)
    else:
        c = 2 ** math.floor(math.log2(n))
        s = pow2(c) + pow2(2 * c)[0::2][: n - c]
    return [float(v) for v in np.array(s, dtype=np.float32)]


def _dot(a, b):
    return jnp.dot(a, b, preferred_element_type=F32)


def _dot_nt(a, b):
    return lax.dot_general(a, b, (((1,), (1,)), ((), ())), preferred_element_type=F32)


def _dot_tn(a, b):
    return lax.dot_general(a, b, (((0,), (0,)), ((), ())), preferred_element_type=F32)


def _dot_hi(a, b):
    return jnp.dot(a, b, precision=HI, preferred_element_type=F32)


def _sigmoid(x):
    return 1.0 / (1.0 + jnp.exp(-x))


def _softplus(x):
    return jnp.maximum(x, 0.0) + jnp.log(1.0 + jnp.exp(-jnp.abs(x)))


def _mean(x):
    return jnp.mean(x, axis=-1, keepdims=True)


def _ln_fwd(x):
    xc = x - _mean(x)
    rstd = lax.rsqrt(_mean(xc * xc) + LN_EPS)
    return xc * rstd, rstd


def _ln_bwd(dxhat, xhat, rstd):
    return rstd * (dxhat - _mean(dxhat) - xhat * _mean(dxhat * xhat))


def _colsum(x):
    return jnp.sum(x, axis=0, keepdims=True)


def _params(sem):
    return pltpu.CompilerParams(dimension_semantics=sem, vmem_limit_bytes=VMEM_LIMIT)


def _tile(i_map_cols, tm):
    return pl.BlockSpec((tm, i_map_cols), lambda i: (i, 0))


def _res(shape):
    return pl.BlockSpec(shape, lambda *_: (0,) * len(shape), pipeline_mode=pl.Buffered(1))


def _acc(shape):
    return pl.BlockSpec(shape, lambda *_: (0,) * len(shape))


def _S(shape, dtype):
    return jax.ShapeDtypeStruct(shape, dtype)


def _my_pos():
    return lax.axis_index("x"), lax.axis_index("y"), lax.axis_index("c")


def _peer(pos, k):
    x, y, c = pos
    px = 1 - x if k & 4 else x
    py = 1 - y if k & 2 else y
    pc = 1 - c if k & 1 else c
    return (px, py, pc)


def _lin(p):
    return 4 * p[0] + 2 * p[1] + p[2]


def _mod_fwd(c_loc, ada_w_loc, ada_b_cols):
    D = c_loc.shape[1]
    C6 = ada_w_loc.shape[1]

    def body(c_ref, w_ref, b_ref, cs_ref, mod_ref, call_ref, modp_ref, ssem, rsem):
        pos = _my_pos()
        me = _lin(pos)
        call_ref[me] = c_ref[...]
        sends = []
        for k in range(1, N_DEV):
            cp = pltpu.make_async_remote_copy(src_ref=c_ref, dst_ref=call_ref.at[me], send_sem=ssem.at[k - 1],
                                              recv_sem=rsem.at[k - 1], device_id=_peer(pos, k), device_id_type=MESH)
            cp.start()
            sends.append(cp)
        for k in range(1, N_DEV):
            src = _lin(_peer(pos, k))
            pltpu.make_async_remote_copy(src_ref=c_ref, dst_ref=call_ref.at[src], send_sem=ssem.at[k - 1],
                                         recv_sem=rsem.at[k - 1], device_id=pos, device_id_type=MESH).wait_recv()
        for cp in sends:
            cp.wait_send()
        call = jnp.concatenate([call_ref[b] for b in range(N_DEV)], axis=0)
        cs = call * _sigmoid(call)
        cs_ref[...] = cs
        modp = _dot(cs.astype(BF16), w_ref[...].astype(BF16)) + b_ref[...]
        for b in range(N_DEV):
            modp_ref[b] = modp[b:b + 1, :]
        mod_ref[me] = modp_ref[me]
        sends = []
        for k in range(1, N_DEV):
            peer = _peer(pos, k)
            cp = pltpu.make_async_remote_copy(src_ref=modp_ref.at[_lin(peer)], dst_ref=mod_ref.at[me],
                                              send_sem=ssem.at[N_DEV - 2 + k], recv_sem=rsem.at[N_DEV - 2 + k],
                                              device_id=peer, device_id_type=MESH)
            cp.start()
            sends.append(cp)
        for k in range(1, N_DEV):
            src = _lin(_peer(pos, k))
            pltpu.make_async_remote_copy(src_ref=modp_ref.at[src], dst_ref=mod_ref.at[src],
                                         send_sem=ssem.at[N_DEV - 2 + k], recv_sem=rsem.at[N_DEV - 2 + k],
                                         device_id=pos, device_id_type=MESH).wait_recv()
        for cp in sends:
            cp.wait_send()

    vm = pl.BlockSpec(memory_space=pltpu.VMEM)
    return pl.pallas_call(
        body, name="mod_fwd",
        out_shape=(_S((N_DEV, D), F32), _S((N_DEV, 1, C6), F32)),
        in_specs=[vm, vm, vm], out_specs=(vm, vm),
        scratch_shapes=[pltpu.VMEM((N_DEV, 1, D), F32), pltpu.VMEM((N_DEV, 1, C6), F32),
                        pltpu.SemaphoreType.DMA((2 * (N_DEV - 1),)), pltpu.SemaphoreType.DMA((2 * (N_DEV - 1),))],
        compiler_params=pltpu.CompilerParams(vmem_limit_bytes=VMEM_LIMIT),
    )(c_loc, ada_w_loc, ada_b_cols)


def _small_gather_sum(pack):
    P8 = pack.shape[1]

    def body(p_ref, gat_ref, sum_ref, ssem, rsem):
        pos = _my_pos()
        me = _lin(pos)
        gat_ref[me] = p_ref[...]
        sends = []
        for k in range(1, N_DEV):
            cp = pltpu.make_async_remote_copy(src_ref=p_ref, dst_ref=gat_ref.at[me], send_sem=ssem.at[k - 1],
                                              recv_sem=rsem.at[k - 1], device_id=_peer(pos, k), device_id_type=MESH)
            cp.start()
            sends.append(cp)
        for k in range(1, N_DEV):
            src = _lin(_peer(pos, k))
            pltpu.make_async_remote_copy(src_ref=p_ref, dst_ref=gat_ref.at[src], send_sem=ssem.at[k - 1],
                                         recv_sem=rsem.at[k - 1], device_id=pos, device_id_type=MESH).wait_recv()
        for cp in sends:
            cp.wait_send()
        acc = gat_ref[0]
        for j in range(1, N_DEV):
            acc = acc + gat_ref[j]
        sum_ref[...] = acc

    vm = pl.BlockSpec(memory_space=pltpu.VMEM)
    return pl.pallas_call(
        body, name="small_gather_sum",
        out_shape=(_S((N_DEV, 8, P8), F32), _S((8, P8), F32)),
        in_specs=[vm], out_specs=(vm, vm),
        scratch_shapes=[pltpu.SemaphoreType.DMA((N_DEV - 1,)), pltpu.SemaphoreType.DMA((N_DEV - 1,))],
        compiler_params=pltpu.CompilerParams(vmem_limit_bytes=VMEM_LIMIT),
    )(pack)


def _ag_weights(shards, after):
    n = len(shards)

    def body(*refs):
        ins, outs = refs[:n], refs[n + 1:2 * n + 1]
        ssem, rsem, lsem = refs[2 * n + 1:]
        x, y, c = pos = _my_pos()
        me = _lin(pos)
        sib = (x, y, 1 - c)
        chips = [(1 - x, y), (x, 1 - y), (1 - x, 1 - y)]

        def copy(a, k, block, to, src=None):
            return pltpu.make_async_remote_copy(
                src_ref=outs[a].at[block] if src is None else src, dst_ref=outs[a].at[block],
                send_sem=ssem.at[a * 7 + k], recv_sem=rsem.at[a * 7 + k], device_id=to, device_id_type=MESH)

        local = [pltpu.make_async_copy(ins[a], outs[a].at[me], lsem.at[a]) for a in range(n)]
        for cp in local:
            cp.start()
        first = []
        for a in range(n):
            first.append(copy(a, 0, me, sib, src=ins[a]))
            first += [copy(a, 1 + j, me, (*chip, c), src=ins[a]) for j, chip in enumerate(chips)]
        for cp in first:
            cp.start()
        passed = []
        for a in range(n):
            for j, chip in enumerate(chips):
                blk = _lin((*chip, c))
                copy(a, 1 + j, blk, pos).wait_recv()
                cp = copy(a, 4 + j, blk, sib)
                cp.start()
                passed.append(cp)
        for a in range(n):
            copy(a, 0, _lin(sib), pos).wait_recv()
            for j, chip in enumerate(chips):
                copy(a, 4 + j, _lin((*chip, 1 - c)), pos).wait_recv()
        for cp in first + passed:
            cp.wait_send()
        for cp in local:
            cp.wait()

    hbm = pl.BlockSpec(memory_space=pl.ANY)
    return pl.pallas_call(
        body, name="ag_weights",
        out_shape=tuple(_S((N_DEV,) + s.shape, s.dtype) for s in shards),
        in_specs=[hbm] * (n + 1), out_specs=tuple([hbm] * n),
        scratch_shapes=[pltpu.SemaphoreType.DMA((7 * n,)), pltpu.SemaphoreType.DMA((7 * n,)),
                        pltpu.SemaphoreType.DMA((n,))],
    )(*shards, after)


def _rs_d2d(blocked, name):
    n = len(blocked)

    def body(*refs):
        ins, outs = refs[:n], refs[n:2 * n]
        ssem, rsem = refs[2 * n:]
        x, y, c = pos = _my_pos()
        sib = (x, y, 1 - c)
        cps = []
        for a in range(n):
            for j in range(N_CHIP):
                cp = pltpu.make_async_remote_copy(
                    src_ref=ins[a].at[2 * j + (1 - c)], dst_ref=outs[a].at[j], send_sem=ssem.at[a * N_CHIP + j],
                    recv_sem=rsem.at[a * N_CHIP + j], device_id=sib, device_id_type=MESH)
                cp.start()
                cps.append(cp)
        for cp in cps:
            cp.wait_recv()
        for cp in cps:
            cp.wait_send()

    hbm = pl.BlockSpec(memory_space=pl.ANY)
    return pl.pallas_call(
        body, name=name,
        out_shape=tuple(_S((N_CHIP,) + b.shape[1:], b.dtype) for b in blocked),
        in_specs=[hbm] * n, out_specs=tuple([hbm] * n),
        scratch_shapes=[pltpu.SemaphoreType.DMA((N_CHIP * n,)), pltpu.SemaphoreType.DMA((N_CHIP * n,))],
    )(*blocked)


_HBM = pl.BlockSpec(memory_space=pltpu.HBM)
_SEM = pl.BlockSpec(memory_space=pltpu.SEMAPHORE)
_ANY = pl.BlockSpec(memory_space=pl.ANY)
_EFFECT = pltpu.SideEffectType.DATAFLOW_SIDE_EFFECTING


def _in_hbm(a):
    return pltpu.with_memory_space_constraint(a, pltpu.HBM)


def _plan_gather(n):
    def copies(pos):
        x, y, c = pos
        out = []
        for a in range(n):
            for dev in [(x, y, 1 - c)] + [(*_peer(pos, 2 * k)[:2], c) for k in range(1, N_CHIP)]:
                out.append((a, None, n + a, _lin(pos), dev, _lin(dev)))
        return out
    return copies


def _plan_forward(n):
    def copies(pos):
        x, y, c = pos
        out = []
        for a in range(n):
            for k in range(1, N_CHIP):
                tx, ty, _ = _peer(pos, 2 * k)
                out.append((a, _lin((tx, ty, c)), a, _lin((tx, ty, c)), (x, y, 1 - c), _lin((tx, ty, 1 - c))))
        return out
    return copies


def _plan_scatter_all(n):
    def copies(pos):
        out = []
        for a in range(n):
            for k in range(1, N_DEV):
                dev = _peer(pos, k)
                out.append((a, _lin(dev), n + a, _lin(pos), dev, _lin(dev)))
        return out
    return copies


def _plan_scatter(n):
    def copies(pos):
        x, y, c = pos
        out = []
        for a in range(n):
            for k in range(1, N_CHIP):
                tx, ty, _ = _peer(pos, 2 * k)
                out.append((a, 2 * tx + ty, n + a, 2 * x + y, (tx, ty, c), 2 * tx + ty))
        return out
    return copies


def _split_copy(refs, cp, ssem, rsem, i, arrival):
    si, s_slot, di, d_slot, dev, a_slot = cp
    return pltpu.make_async_remote_copy(
        src_ref=refs[si] if s_slot is None else refs[si].at[s_slot], dst_ref=refs[di].at[a_slot if arrival else d_slot],
        send_sem=ssem.at[i], recv_sem=rsem.at[i], device_id=dev, device_id_type=MESH)


def _split_start(arrays, copies, after, name):
    n = len(arrays)
    n_cp = len(copies((0, 0, 0)))

    def body(*refs):
        ssem, rsem, token = refs[n + 1], refs[n + 2], refs[-1]
        for i, cp in enumerate(copies(_my_pos())):
            _split_copy(refs, cp, ssem, rsem, i, False).start()
        token[...] = jnp.zeros_like(token)

    res = pl.pallas_call(
        body, name=name,
        out_shape=(pltpu.SemaphoreType.DMA((n_cp,)), pltpu.SemaphoreType.DMA((n_cp,)),
                   *[pltpu.HBM(a.shape, a.dtype) for a in arrays], _S((8, LANE), F32)),
        in_specs=[_HBM] * n + [_ANY],
        out_specs=(_SEM, _SEM, *[_HBM] * n, pl.BlockSpec(memory_space=pltpu.VMEM)),
        input_output_aliases={a: 2 + a for a in range(n)},
        compiler_params=pltpu.CompilerParams(has_side_effects=_EFFECT),
    )(*[_in_hbm(a) for a in arrays], after)
    return res[0], res[1], list(res[2:2 + n]), res[-1]


def _split_wait(ssem, rsem, arrays, copies, after, name):
    n = len(arrays)

    def body(*refs):
        for i, cp in enumerate(copies(_my_pos())):
            d = _split_copy(refs, cp, refs[n], refs[n + 1], i, True)
            d.wait_send()
            d.wait_recv()

    res = pl.pallas_call(
        body, name=name,
        out_shape=tuple(pltpu.HBM(a.shape, a.dtype) for a in arrays),
        in_specs=[_HBM] * n + [_SEM, _SEM, _ANY], out_specs=tuple([_HBM] * n),
        input_output_aliases={a: a for a in range(n)},
        compiler_params=pltpu.CompilerParams(has_side_effects=_EFFECT),
    )(*arrays, ssem, rsem, after)
    return list(res)


def _row_tile(R, itemsize_rows=16, cap=256):
    t = cap - cap % itemsize_rows
    while t >= itemsize_rows:
        if R % t == 0:
            return t
        t -= itemsize_rows
    return R


def _pair_sum(blocked, recv, core):
    _, R, C = blocked.shape
    tr = _row_tile(R)

    def body(ids_ref, a_ref, b_ref, o_ref):
        del ids_ref
        o_ref[...] = (a_ref[...] + b_ref[...]).astype(BF16)

    gs = pltpu.PrefetchScalarGridSpec(
        num_scalar_prefetch=1, grid=(N_CHIP, R // tr),
        in_specs=[pl.BlockSpec((1, tr, C), lambda j, r, ids: (2 * j + ids[0], r, 0)),
                  pl.BlockSpec((1, tr, C), lambda j, r, ids: (j, r, 0))],
        out_specs=pl.BlockSpec((1, tr, C), lambda j, r, ids: (j, r, 0)))
    return pl.pallas_call(body, name="pair_sum", grid_spec=gs, out_shape=_S((N_CHIP, R, C), BF16),
                          compiler_params=_params(("arbitrary", "arbitrary")))(core, blocked, recv)


def _adamw_math(w, g, m, v):
    m2 = ADAM_B1 * m + (1.0 - ADAM_B1) * g
    v2 = ADAM_B2 * v + (1.0 - ADAM_B2) * (g * g)
    m_hat = m2 / (1.0 - ADAM_B1 ** ADAM_STEP)
    v_hat = v2 / (1.0 - ADAM_B2 ** ADAM_STEP)
    delta = -ADAM_LR * (m_hat / (jnp.sqrt(v_hat) + ADAM_EPS) + ADAM_WD * w)
    return delta, m2, v2


def _sum_adamw_split(pairs, land, chips, w, m, v, transposed=False):
    R, C = w.shape
    n_slots = chips.shape[0]
    tr = _row_tile(R, 128 if transposed else 16)

    def body(ids_ref, *refs):
        del ids_ref
        parts, (w_ref, m_ref, v_ref, g_ref, d_ref, m2_ref, v2_ref) = refs[:n_slots], refs[n_slots:]
        g = parts[0][0].astype(F32)
        for p_ref in parts[1:]:
            g = g + p_ref[0].astype(F32)
        if transposed:
            g = g.T
        g_ref[...] = g
        d_ref[...], m2_ref[...], v2_ref[...] = _adamw_math(w_ref[...], g, m_ref[...], v_ref[...])

    t = pl.BlockSpec((tr, C), lambda r, ids: (r, 0))
    if transposed:
        slot = lambda k: pl.BlockSpec((1, C, tr), lambda r, ids: (ids[k], 0, r))
    else:
        slot = lambda k: pl.BlockSpec((1, tr, C), lambda r, ids: (ids[k], r, 0))
    gs = pltpu.PrefetchScalarGridSpec(num_scalar_prefetch=1, grid=(R // tr,),
                                      in_specs=[slot(k) for k in range(n_slots)] + [t, t, t], out_specs=(t, t, t, t))
    return pl.pallas_call(body, name="sum_adamw_split", grid_spec=gs, out_shape=tuple(_S((R, C), F32) for _ in range(4)),
                          compiler_params=_params(("arbitrary",)))(chips, pairs, *[land] * (n_slots - 1), w, m, v)


def _adamw(w, g, m, v):
    R, C = w.shape
    tr = _row_tile(R, 8)

    def body(w_ref, g_ref, m_ref, v_ref, d_ref, m2_ref, v2_ref):
        d_ref[...], m2_ref[...], v2_ref[...] = _adamw_math(w_ref[...], g_ref[...], m_ref[...], v_ref[...])

    t = pl.BlockSpec((tr, C), lambda r: (r, 0))
    return pl.pallas_call(body, name="adamw", grid=(R // tr,), in_specs=[t, t, t, t], out_specs=(t, t, t),
                          out_shape=tuple(_S((R, C), F32) for _ in range(3)),
                          compiler_params=_params(("arbitrary",)))(w, g, m, v)


def _ada_grad_adamw(cs16, dmod16, w, m, v):
    D, C6 = w.shape
    tr = _row_tile(D, 8, 256)

    def body(cs_ref, dm_ref, w_ref, m_ref, v_ref, g_ref, d_ref, m2_ref, v2_ref):
        g = _dot_tn(cs_ref[...].astype(BF16), dm_ref[...].astype(BF16))
        g_ref[...] = g
        d_ref[...], m2_ref[...], v2_ref[...] = _adamw_math(w_ref[...], g, m_ref[...], v_ref[...])

    t = pl.BlockSpec((tr, C6), lambda r: (r, 0))
    return pl.pallas_call(
        body, name="ada_grad_adamw", grid=(D // tr,),
        in_specs=[pl.BlockSpec((16, tr), lambda r: (0, r)), _acc((16, C6)), t, t, t], out_specs=(t, t, t, t),
        out_shape=tuple(_S((D, C6), F32) for _ in range(4)), compiler_params=_params(("arbitrary",)))(cs16, dmod16, w, m, v)


def _pick(n, cands):
    for c in cands:
        if n % c == 0:
            return c
    return n


def _matmul_tn(a, b, name, square_a=False):
    L, K = a.shape
    N = b.shape[1]
    bk = _pick(K, (1024, 512, 256, 128))
    bn = _pick(N, (1024, 768, 512, 256, 128))
    tl = _pick(L, (1024, 512, 256, 128))
    n_l = L // tl

    def body(a_ref, b_ref, o_ref, acc_ref):
        l = pl.program_id(2)

        @pl.when(l == 0)
        def _():
            acc_ref[...] = jnp.zeros_like(acc_ref)
        av = a_ref[...]
        if square_a:
            av = av.astype(F32)
            av = av * av
        acc_ref[...] += _dot_tn(av.astype(BF16), b_ref[...].astype(BF16))

        @pl.when(l == n_l - 1)
        def _():
            o_ref[...] = acc_ref[...].astype(BF16)

    return pl.pallas_call(
        body, name=name, grid=(K // bk, N // bn, n_l),
        in_specs=[pl.BlockSpec((tl, bk), lambda k, n, l: (l, k)), pl.BlockSpec((tl, bn), lambda k, n, l: (l, n))],
        out_specs=pl.BlockSpec((bk, bn), lambda k, n, l: (k, n)), out_shape=_S((K, N), BF16),
        scratch_shapes=[pltpu.VMEM((bk, bn), F32)],
        compiler_params=_params(("arbitrary", "arbitrary", "arbitrary")))(a, b)


def _gw_in(pieces, u1, dims):
    L, D = u1.shape
    H = dims["H"]
    r_z, r_xbc, r_dt, r_q, r_kv = _proj_rows(dims)
    PROJ = r_kv[1]
    tl = _pick(L, (512, 256, 128))
    n_l = L // tl

    def body(dz_ref, dxbc_ref, ddt_ref, dq_ref, dkv_ref, u_ref, o_ref):
        @pl.when(pl.program_id(0) == 0)
        def _():
            o_ref[...] = jnp.zeros_like(o_ref)
        u = u_ref[...]
        for ref, (r0, r1) in ((dz_ref, r_z), (dxbc_ref, r_xbc), (dq_ref, r_q), (dkv_ref, r_kv)):
            o_ref[r0:r1, :] += _dot_tn(ref[...].astype(BF16), u)
        o_ref[r_dt[0]:r_dt[0] + H, :] += _dot_tn(ddt_ref[...].astype(BF16), u)[0:H, :]

    return pl.pallas_call(
        body, name="gw_in", grid=(n_l,),
        in_specs=[_tile(p.shape[1], tl) for p in pieces] + [_tile(D, tl)],
        out_specs=_acc((PROJ, D)), out_shape=_S((PROJ, D), F32),
        compiler_params=_params(("arbitrary",)))(*pieces, u1)


def _proj_rows(dims):
    W, CD, H, AW, KVW2 = dims["W"], dims["CD"], dims["H"], dims["AW"], dims["KVW2"]
    o_dt = W + CD
    o_q = o_dt + H
    return (0, W), (W, o_dt), (o_dt, o_dt + LANE), (o_q, o_q + AW), (o_q + AW, o_q + AW + KVW2)


def _ln_in_proj(x, g, b, sc, sh, w_t, dims):
    L, D = x.shape
    W, CD, AW, KVW2 = dims["W"], dims["CD"], dims["AW"], dims["KVW2"]
    PROJ = w_t.shape[0]
    tm = _pick(L, (MLP_TM, 128))
    r_z, r_xbc, r_dt, r_q, r_kv = _proj_rows(dims)

    def body(x_ref, g_ref, b_ref, sc_ref, sh_ref, w_ref, xhat_ref, rstd_ref, u1_ref, z_ref, xbc_ref, q_ref, kv_ref, dt_ref):
        xhat, rstd = _ln_fwd(x_ref[...])
        xhat_ref[...] = xhat
        rstd_ref[...] = rstd
        h0 = xhat * g_ref[...] + b_ref[...]
        u1 = (h0 * (1.0 + sc_ref[...]) + sh_ref[...]).astype(BF16)
        u1_ref[...] = u1
        z_ref[...] = _dot_nt(u1, w_ref[r_z[0]:r_z[1], :])
        xbc_ref[...] = _dot_nt(u1, w_ref[r_xbc[0]:r_xbc[1], :])
        q_ref[...] = _dot_nt(u1, w_ref[r_q[0]:r_q[1], :]).astype(BF16)
        kv_ref[...] = _dot_nt(u1, w_ref[r_kv[0]:r_kv[1], :]).astype(BF16)
        dt_ref[...] = _dot_nt(u1, w_ref[r_dt[0]:r_dt[1], :])

    v = _acc((1, D))
    return pl.pallas_call(
        body, name="ln_in_proj", grid=(L // tm,),
        in_specs=[_tile(D, tm), v, v, v, v, _res((PROJ, D))],
        out_specs=(_tile(D, tm), _tile(1, tm), _tile(D, tm), _tile(W, tm), _tile(CD, tm), _tile(AW, tm),
                   _tile(KVW2, tm), _tile(LANE, tm)),
        out_shape=(_S((L, D), F32), _S((L, 1), F32), _S((L, D), BF16), _S((L, W), F32), _S((L, CD), F32),
                   _S((L, AW), BF16), _S((L, KVW2), BF16), _S((L, LANE), F32)),
        compiler_params=_params(("arbitrary",)))(x, g, b, sc, sh, w_t)


def _conv_act(cur_ref, prev_ref, cw_ref, cb_ref, ext_ref, first):
    T = cur_ref.shape[0]
    ext_ref[0:HALO, :] = jnp.where(first, 0.0, prev_ref[...])
    ext_ref[HALO:HALO + T, :] = cur_ref[...]
    pre = cb_ref[...] + cw_ref[0:1, :] * ext_ref[HALO - 3:HALO - 3 + T, :]
    for k in range(1, CONV_K):
        pre = pre + cw_ref[k:k + 1, :] * ext_ref[HALO - 3 + k:HALO - 3 + k + T, :]
    return pre * _sigmoid(pre), pre


def _tri(T, upper=False):
    r = lax.broadcasted_iota(jnp.int32, (T, T), 0)
    c = lax.broadcasted_iota(jnp.int32, (T, T), 1)
    return (r <= c) if upper else (r >= c)


def _expand_heads(dst_ref, v, n_heads):
    for h in range(n_heads):
        dst_ref[:, h * HEAD_DIM:(h + 1) * HEAD_DIM] = jnp.broadcast_to(v[:, h:h + 1], (v.shape[0], HEAD_DIM))


def _head_reduce(v):
    wdt = v.shape[1]
    ch = lax.broadcasted_iota(jnp.int32, (wdt, LANE), 0)
    lo = lax.broadcasted_iota(jnp.int32, (wdt, LANE), 1) * HEAD_DIM
    onehot = ((ch >= lo) & (ch < lo + HEAD_DIM)).astype(BF16)
    hi = v.astype(BF16)
    rest = (v - hi.astype(F32)).astype(BF16)
    return _dot(hi, onehot) + _dot(rest, onehot)


def _conv_ssd(xbc, dt_raw, z, cw, cb, dtb, alog, dsk, nw, dims):
    L, CD = xbc.shape
    W, H, G, N = dims["W"], dims["H"], SSD_GROUPS, SSD_STATE
    T = CHUNK
    R = H // G
    GW = W // G
    nc = L // T
    HP = H * HEAD_DIM

    def body(xbc_ref, prev_ref, dt_ref, z_ref, cw_ref, cb_ref, dtb_ref, alog_ref, dsk_ref, nw_ref,
             y_ref, yn_ref, sp_ref, ext_ref, s_ref, ybuf_ref, dtx_ref, acx_ref, xb_ref):
        i = pl.program_id(0)

        @pl.when(i == 0)
        def _():
            s_ref[...] = jnp.zeros_like(s_ref)

        act, _ = _conv_act(xbc_ref, prev_ref, cw_ref, cb_ref, ext_ref, i == 0)
        xs = act[:, :W]
        dt = _softplus(dt_ref[...] + dtb_ref[...])
        a = dt * (-jnp.exp(alog_ref[...]))
        low = _tri(T)
        acum = _dot_hi(low.astype(F32), a)
        acum_t = acum.T
        _expand_heads(dtx_ref, dt, H)
        _expand_heads(acx_ref, acum, H)
        acx = acx_ref[...]
        lastx = acx[T - 1:T, :]
        xd = xs * dtx_ref[...]
        xb_ref[...] = xd.astype(BF16)
        xdb = (xd * jnp.exp(lastx - acx)).astype(BF16)
        ex = jnp.exp(acx)
        elx = jnp.exp(lastx)
        for g in range(G):
            gs = slice(g * GW, (g + 1) * GW)
            bgb = act[:, W + g * N:W + (g + 1) * N].astype(BF16)
            cgb = act[:, W + G * N + g * N:W + G * N + (g + 1) * N].astype(BF16)
            stg = s_ref[:, gs]
            sp_ref[0, :, gs] = stg
            yoff = ex[:, gs] * _dot(cgb, stg.astype(BF16))
            s_ref[:, gs] = stg * elx[:, gs] + _dot_tn(bgb, xdb[:, gs])
            cb_g = _dot_nt(cgb, bgb)
            for r in range(R):
                h = g * R + r
                hs = slice(h * HEAD_DIM, (h + 1) * HEAD_DIM)
                lm = jnp.where(low, jnp.exp(acum[:, h:h + 1] - acum_t[h:h + 1, :]), 0.0)
                ybuf_ref[:, hs] = _dot((cb_g * lm).astype(BF16), xb_ref[:, hs]) + yoff[:, r * HEAD_DIM:(r + 1) * HEAD_DIM]
        y = ybuf_ref[...] + dsk_ref[...] * xs
        y_ref[...] = y
        zz = z_ref[...]
        hh = y * (zz * _sigmoid(zz))
        for g in range(G):
            gs = slice(g * GW, (g + 1) * GW)
            hg = hh[:, gs]
            yn_ref[:, gs] = (hg * lax.rsqrt(_mean(hg * hg) + RMS_EPS) * nw_ref[:, gs]).astype(BF16)

    return pl.pallas_call(
        body, name="conv_ssd", grid=(nc,),
        in_specs=[_tile(CD, T), pl.BlockSpec((HALO, CD), lambda i: (jnp.maximum(i * (T // HALO) - 1, 0), 0)),
                  _tile(LANE, T), _tile(W, T), _acc((CONV_K, CD)), _acc((1, CD)), _acc((1, LANE)), _acc((1, LANE)),
                  _acc((1, W)), _acc((1, W))],
        out_specs=(_tile(W, T), _tile(W, T), pl.BlockSpec((1, N, HP), lambda i: (i, 0, 0))),
        out_shape=(_S((L, W), F32), _S((L, W), BF16), _S((nc, N, HP), F32)),
        scratch_shapes=[pltpu.VMEM((T + HALO, CD), F32), pltpu.VMEM((N, HP), F32), pltpu.VMEM((T, W), F32),
                        pltpu.VMEM((T, W), F32), pltpu.VMEM((T, W), F32), pltpu.VMEM((T, W), BF16)],
        compiler_params=_params(("arbitrary",)))(xbc, xbc, dt_raw, z, cw, cb, dtb, alog, dsk, nw)


def _attn_mask(T, i):
    r = lax.broadcasted_iota(jnp.int32, (T, 2 * T), 0)
    c = lax.broadcasted_iota(jnp.int32, (T, 2 * T), 1)
    dist = r + T - c
    valid = (dist >= 0) & (dist < CHUNK) & ((c >= T) | (i > 0))
    return dist.astype(F32), valid


def _attn_probs(s_raw, dist, valid, slope, sink, axis):
    s = s_raw * (HEAD_DIM ** -0.5) - slope * dist
    s = jnp.where(valid, s, NEG)
    m = jnp.maximum(jnp.max(s, axis=axis, keepdims=True), sink)
    p = jnp.exp(s - m)
    e_sink = jnp.exp(sink - m)
    inv = 1.0 / (jnp.sum(p, axis=axis, keepdims=True) + e_sink)
    return p * inv, e_sink * inv


def _kv_heads(kvc_ref, kvp_ref, g, n_kv):
    ks = slice(g * HEAD_DIM, (g + 1) * HEAD_DIM)
    vs = slice((n_kv + g) * HEAD_DIM, (n_kv + g + 1) * HEAD_DIM)
    kk = jnp.concatenate([kvp_ref[:, ks], kvc_ref[:, ks]], axis=0)
    vv = jnp.concatenate([kvp_ref[:, vs], kvc_ref[:, vs]], axis=0)
    return kk, vv


def _swa_fwd(q, kv, sinks, dims):
    L, AW = q.shape
    KV, KVW2 = dims["KV"], dims["KVW2"]
    T = CHUNK
    nb = L // T
    slopes = _alibi_slopes(dims["AH"])

    def body(q_ref, kvc_ref, kvp_ref, sink_ref, o_ref, qg_ref, p_ref):
        i = pl.program_id(0)
        dist, valid = _attn_mask(T, i)
        for g in range(KV):
            kk, vv = _kv_heads(kvc_ref, kvp_ref, g, KV)
            for r in range(GQA):
                h = g * GQA + r
                qg_ref[r * T:(r + 1) * T, :] = q_ref[:, h * HEAD_DIM:(h + 1) * HEAD_DIM]
            s_all = _dot_nt(qg_ref[...], kk)
            for r in range(GQA):
                h = g * GQA + r
                p, _ = _attn_probs(s_all[r * T:(r + 1) * T, :], dist, valid, slopes[h], sink_ref[h], -1)
                p_ref[r * T:(r + 1) * T, :] = p.astype(BF16)
            o_all = _dot(p_ref[...], vv)
            for r in range(GQA):
                h = g * GQA + r
                o_ref[:, h * HEAD_DIM:(h + 1) * HEAD_DIM] = o_all[r * T:(r + 1) * T, :].astype(BF16)

    return pl.pallas_call(
        body, name="swa_fwd", grid=(nb,),
        in_specs=[_tile(AW, T), _tile(KVW2, T), pl.BlockSpec((T, KVW2), lambda i: (jnp.maximum(i - 1, 0), 0)),
                  pl.BlockSpec(memory_space=pltpu.SMEM)],
        out_specs=_tile(AW, T), out_shape=_S((L, AW), BF16),
        scratch_shapes=[pltpu.VMEM((GQA * T, HEAD_DIM), BF16), pltpu.VMEM((GQA * T, 2 * T), BF16)],
        compiler_params=_params(("arbitrary",)))(q, kv, kv, sinks)


def _out_proj_ln1(yn, o, w_out, xhat0, vecs, alpha):
    L, W = yn.shape
    D = xhat0.shape[1]
    MIX = w_out.shape[0]
    tm = _pick(L, (MLP_TM, 128))

    def body(yn_ref, o_ref, w_ref, xh_ref, v_ref, mix_ref, xhat1_ref, rstd1_ref, u2_ref):
        mix = _dot(yn_ref[...], w_ref[0:W, :]) + _dot(o_ref[...], w_ref[W:MIX, :])
        mix_ref[...] = mix
        h0 = xh_ref[...] * v_ref[0:1, :] + v_ref[1:2, :]
        xhat1, rstd1 = _ln_fwd(alpha * h0 + (1.0 + v_ref[2:3, :]) * mix)
        xhat1_ref[...] = xhat1
        rstd1_ref[...] = rstd1
        h1 = xhat1 * v_ref[3:4, :] + v_ref[4:5, :]
        u2_ref[...] = (h1 * (1.0 + v_ref[5:6, :]) + v_ref[6:7, :]).astype(BF16)

    return pl.pallas_call(
        body, name="out_proj_ln1", grid=(L // tm,),
        in_specs=[_tile(W, tm), _tile(MIX - W, tm), _res((MIX, D)), _tile(D, tm), _acc((8, D))],
        out_specs=(_tile(D, tm), _tile(D, tm), _tile(1, tm), _tile(D, tm)),
        out_shape=(_S((L, D), F32), _S((L, D), F32), _S((L, 1), F32), _S((L, D), BF16)),
        compiler_params=_params(("arbitrary",)))(yn, o, w_out, xhat0, vecs)


def _mlp_loss(u2, w1, w2, xhat1, tgt, vecs, b1, alpha):
    L, D = xhat1.shape
    FF = w1.shape[1]
    tm = _pick(L, (MLP_TM, 128))
    sub = min(tm, MLP_SUB)
    fc = _pick(FF, (MLP_FC, 256, 128))

    def body(u2_ref, w1_ref, w2_ref, xh_ref, t_ref, v_ref, b1_ref, rr_ref, dr2_ref, acc_ref, loss_ref):
        @pl.when(pl.program_id(0) == 0)
        def _():
            acc_ref[...] = jnp.zeros_like(acc_ref)
            loss_ref[...] = jnp.zeros_like(loss_ref)

        for s in range(tm // sub):
            rs = slice(s * sub, (s + 1) * sub)
            u2 = u2_ref[rs, :]
            f = jnp.zeros((sub, D), F32) + v_ref[5:6, :]
            for j in range(FF // fc):
                cs = slice(j * fc, (j + 1) * fc)
                rr = jnp.maximum(_dot(u2, w1_ref[:, cs]) + b1_ref[:, cs], 0.0)
                rr_ref[rs, cs] = rr.astype(BF16)
                f = f + _dot((rr * rr).astype(BF16), w2_ref[cs, :])
            xhat1 = xh_ref[rs, :]
            h1 = xhat1 * v_ref[0:1, :] + v_ref[1:2, :]
            xhat2, rstd2 = _ln_fwd(alpha * h1 + (1.0 + v_ref[2:3, :]) * f)
            e = xhat2 * v_ref[3:4, :] + v_ref[4:5, :] - t_ref[rs, :]
            loss_ref[...] += 0.5 * jnp.sum(_mean(e * e))
            dy = e * (1.0 / D)
            dr2 = _ln_bwd(dy * v_ref[3:4, :], xhat2, rstd2)
            dr2_ref[rs, :] = dr2
            acc_ref[0:1, :] += _colsum(dy * xhat2)
            acc_ref[1:2, :] += _colsum(dy)
            acc_ref[2:3, :] += _colsum(dr2 * f)

    return pl.pallas_call(
        body, name="mlp_loss", grid=(L // tm,),
        in_specs=[_tile(D, tm), _res((D, FF)), _res((FF, D)), _tile(D, tm), _tile(D, tm), _acc((8, D)), _acc((1, FF))],
        out_specs=(_tile(FF, tm), _tile(D, tm), _acc((8, D)), _acc((1, LANE))),
        out_shape=(_S((L, FF), BF16), _S((L, D), F32), _S((8, D), F32), _S((1, LANE), F32)),
        compiler_params=_params(("arbitrary",)))(u2, w1, w2, xhat1, tgt, vecs, b1)


def _mlp_bwd_a(dr2, rr, w2, g2):
    L, D = dr2.shape
    FF = w2.shape[0]
    tm = _pick(L, (MLP_TM, 128))
    fc = _pick(FF, (MLP_FC, 256, 128))

    def body(dr2_ref, rr_ref, w2_ref, g2_ref, df_ref, da_ref, gb2_ref, gb1_ref):
        @pl.when(pl.program_id(0) == 0)
        def _():
            gb2_ref[...] = jnp.zeros_like(gb2_ref)
            gb1_ref[...] = jnp.zeros_like(gb1_ref)

        df = (1.0 + g2_ref[...]) * dr2_ref[...]
        gb2_ref[...] += _colsum(df)
        dfb = df.astype(BF16)
        df_ref[...] = dfb
        for j in range(FF // fc):
            cs = slice(j * fc, (j + 1) * fc)
            da = _dot_nt(dfb, w2_ref[cs, :]) * (2.0 * rr_ref[:, cs].astype(F32))
            gb1_ref[:, cs] += _colsum(da)
            da_ref[:, cs] = da.astype(BF16)

    return pl.pallas_call(
        body, name="mlp_bwd_a", grid=(L // tm,),
        in_specs=[_tile(D, tm), _tile(FF, tm), _res((FF, D)), _acc((1, D))],
        out_specs=(_tile(D, tm), _tile(FF, tm), _acc((1, D)), _acc((1, FF))),
        out_shape=(_S((L, D), BF16), _S((L, FF), BF16), _S((1, D), F32), _S((1, FF), F32)),
        compiler_params=_params(("arbitrary",)))(dr2, rr, w2, g2)


def _mlp_bwd_b(da, w1, dr2, xhat1, rstd1, mix, w_out, vecs, alpha, W):
    L, FF = da.shape
    D = dr2.shape[1]
    MIX = w_out.shape[0]
    tm = _pick(L, (MLP_TM, 128))

    def body(da_ref, w1_ref, dr2_ref, xh_ref, rs_ref, mix_ref, wo_ref, v_ref, dmix_ref, dh0_ref, dyn_ref, do_ref, acc_ref):
        @pl.when(pl.program_id(0) == 0)
        def _():
            acc_ref[...] = jnp.zeros_like(acc_ref)

        du2 = _dot_nt(da_ref[...], w1_ref[...])
        xhat1 = xh_ref[...]
        h1 = xhat1 * v_ref[0:1, :] + v_ref[1:2, :]
        acc_ref[0:1, :] += _colsum(du2 * h1)
        acc_ref[1:2, :] += _colsum(du2)
        dh1 = alpha * dr2_ref[...] + du2 * (1.0 + v_ref[2:3, :])
        acc_ref[2:3, :] += _colsum(dh1 * xhat1)
        acc_ref[3:4, :] += _colsum(dh1)
        dr1 = _ln_bwd(dh1 * v_ref[0:1, :], xhat1, rs_ref[...])
        acc_ref[4:5, :] += _colsum(dr1 * mix_ref[...])
        dh0_ref[...] = alpha * dr1
        dmix = ((1.0 + v_ref[3:4, :]) * dr1).astype(BF16)
        dmix_ref[...] = dmix
        dyn_ref[...] = _dot_nt(dmix, wo_ref[0:W, :])
        do_ref[...] = _dot_nt(dmix, wo_ref[W:MIX, :]).astype(BF16)

    return pl.pallas_call(
        body, name="mlp_bwd_b", grid=(L // tm,),
        in_specs=[_tile(FF, tm), _res((D, FF)), _tile(D, tm), _tile(D, tm), _tile(1, tm), _tile(D, tm), _res((MIX, D)),
                  _acc((8, D))],
        out_specs=(_tile(D, tm), _tile(D, tm), _tile(W, tm), _tile(MIX - W, tm), _acc((8, D))),
        out_shape=(_S((L, D), BF16), _S((L, D), F32), _S((L, W), F32), _S((L, MIX - W), BF16), _S((8, D), F32)),
        compiler_params=_params(("arbitrary",)))(da, w1, dr2, xhat1, rstd1, mix, w_out, vecs)


def _swa_bwd(q, kv, do, sinks, dims):
    L, AW = q.shape
    KV, KVW2 = dims["KV"], dims["KVW2"]
    T = CHUNK
    nb = L // T
    slopes = _alibi_slopes(dims["AH"])
    scale = HEAD_DIM ** -0.5

    def body(q_ref, kvc_ref, kvp_ref, do_ref, sink_ref, dq_ref, dkv_ref, dsink_ref, carry_ref,
             qg_ref, dog_ref, pt_ref, dst_ref):
        i = pl.program_id(0)

        @pl.when(i == 0)
        def _():
            carry_ref[...] = jnp.zeros_like(carry_ref)
            dsink_ref[...] = jnp.zeros_like(dsink_ref)

        @pl.when(i < nb)
        def _():
            c = lax.broadcasted_iota(jnp.int32, (2 * T, T), 0)
            r_ = lax.broadcasted_iota(jnp.int32, (2 * T, T), 1)
            dist_i = r_ + T - c
            valid = (dist_i >= 0) & (dist_i < CHUNK) & ((c >= T) | (i > 0))
            dist = dist_i.astype(F32)
            lane = lax.broadcasted_iota(jnp.int32, (1, LANE), 1)
            dsink = jnp.zeros((1, LANE), F32)
            dks, dvs = [], []
            for g in range(KV):
                kk, vv = _kv_heads(kvc_ref, kvp_ref, g, KV)
                for r in range(GQA):
                    hs = slice((g * GQA + r) * HEAD_DIM, (g * GQA + r + 1) * HEAD_DIM)
                    qg_ref[r * T:(r + 1) * T, :] = q_ref[:, hs]
                    dog_ref[r * T:(r + 1) * T, :] = do_ref[:, hs]
                st_all = _dot_nt(kk, qg_ref[...])
                dpt_all = _dot_nt(vv, dog_ref[...])
                for r in range(GQA):
                    h = g * GQA + r
                    cs = slice(r * T, (r + 1) * T)
                    p, p_sink = _attn_probs(st_all[:, cs], dist, valid, slopes[h], sink_ref[h], 0)
                    dp = dpt_all[:, cs]
                    delta = jnp.sum(p * dp, axis=0, keepdims=True)
                    pt_ref[:, cs] = p.astype(BF16)
                    dst_ref[:, cs] = (p * (dp - delta)).astype(BF16)
                    dsink = dsink + jnp.where(lane == h, -jnp.sum(p_sink * delta), 0.0)
                dst = dst_ref[...]
                dks.append(_dot(dst, qg_ref[...]) * scale)
                dvs.append(_dot(pt_ref[...], dog_ref[...]))
                dq_all = _dot_tn(dst, kk) * scale
                for r in range(GQA):
                    hs = slice((g * GQA + r) * HEAD_DIM, (g * GQA + r + 1) * HEAD_DIM)
                    dq_ref[:, hs] = dq_all[r * T:(r + 1) * T, :].astype(BF16)
            dkv = jnp.concatenate(dks + dvs, axis=1)
            dsink_ref[...] += dsink
            dkv_ref[...] = carry_ref[...] + dkv[0:T, :]
            carry_ref[...] = dkv[T:2 * T, :]

        @pl.when(i == nb)
        def _():
            dkv_ref[...] = carry_ref[...]

    last = nb - 1
    return pl.pallas_call(
        body, name="swa_bwd", grid=(nb + 1,),
        in_specs=[pl.BlockSpec((T, AW), lambda i: (jnp.minimum(i, last), 0)),
                  pl.BlockSpec((T, KVW2), lambda i: (jnp.minimum(i, last), 0)),
                  pl.BlockSpec((T, KVW2), lambda i: (jnp.clip(i - 1, 0, last), 0)),
                  pl.BlockSpec((T, AW), lambda i: (jnp.minimum(i, last), 0)),
                  pl.BlockSpec(memory_space=pltpu.SMEM)],
        out_specs=(pl.BlockSpec((T, AW), lambda i: (jnp.minimum(i, last), 0)),
                   pl.BlockSpec((T, KVW2), lambda i: (jnp.maximum(i - 1, 0), 0)), _acc((1, LANE))),
        out_shape=(_S((L, AW), BF16), _S((L, KVW2), F32), _S((1, LANE), F32)),
        scratch_shapes=[pltpu.VMEM((T, KVW2), F32), pltpu.VMEM((GQA * T, HEAD_DIM), BF16),
                        pltpu.VMEM((GQA * T, HEAD_DIM), BF16), pltpu.VMEM((2 * T, GQA * T), BF16),
                        pltpu.VMEM((2 * T, GQA * T), BF16)],
        compiler_params=_params(("arbitrary",)))(q, kv, kv, do, sinks)


def _ssd_bwd(dyn, y, z, xbc, dt_raw, sprev, cw, cb, dtb, alog, dsk, nw, dims):
    L, CD = xbc.shape
    W, H, G, N = dims["W"], dims["H"], SSD_GROUPS, SSD_STATE
    T = CHUNK
    R = H // G
    GW = W // G
    nc = L // T
    HP = H * HEAD_DIM

    def body(dyn_ref, y_ref, z_ref, xbc_ref, prev_ref, dt_ref, sp_ref, cw_ref, cb_ref, dtb_ref, alog_ref, dsk_ref, nw_ref,
             dz_ref, dpre_ref, ddt_ref, acc_ref, hacc_ref, ext_ref, ds_ref, dtx_ref, acx_ref, xb_ref, dyb_ref, r12_ref,
             dx_ref, rows_ref):
        i = pl.program_id(0)

        @pl.when(i == 0)
        def _():
            ds_ref[...] = jnp.zeros_like(ds_ref)
            acc_ref[...] = jnp.zeros_like(acc_ref)
            hacc_ref[...] = jnp.zeros_like(hacc_ref)

        act, pre = _conv_act(xbc_ref, prev_ref, cw_ref, cb_ref, ext_ref, i == nc - 1)
        xs = act[:, :W]
        dt_in = dt_ref[...] + dtb_ref[...]
        dt = _softplus(dt_in)
        a_neg = -jnp.exp(alog_ref[...])
        a = dt * a_neg
        low = _tri(T)
        upf = _tri(T, upper=True).astype(F32)
        acum = _dot_hi(low.astype(F32), a)
        acum_t = acum.T

        y = y_ref[...]
        zz = z_ref[...]
        sg = _sigmoid(zz)
        sz = zz * sg
        hh = y * sz
        dyn_v = dyn_ref[...]
        parts = []
        for g in range(G):
            gs = slice(g * GW, (g + 1) * GW)
            hg = hh[:, gs]
            hhat = hg * lax.rsqrt(_mean(hg * hg) + RMS_EPS)
            rg = lax.rsqrt(_mean(hg * hg) + RMS_EPS)
            acc_ref[0:1, gs] += _colsum(dyn_v[:, gs] * hhat)
            dhhat = dyn_v[:, gs] * nw_ref[:, gs]
            parts.append(rg * (dhhat - hhat * _mean(dhhat * hhat)))
        dhh = jnp.concatenate(parts, axis=1)
        dy = dhh * sz
        dz_ref[...] = (dhh * y * (sg * (1.0 + zz * (1.0 - sg)))).astype(BF16)
        acc_ref[1:2, :] += _colsum(dy * xs)
        dyb_ref[...] = dy.astype(BF16)

        _expand_heads(dtx_ref, dt, H)
        _expand_heads(acx_ref, acum, H)
        dtx = dtx_ref[...]
        acx = acx_ref[...]
        lastx = acx[T - 1:T, :]
        ex = jnp.exp(acx)
        decx = jnp.exp(lastx - acx)
        elx = jnp.exp(lastx)
        xd = xs * dtx
        xb_ref[...] = xd.astype(BF16)
        xdecb = (xd * decx).astype(BF16)
        dgb = (ex * dy).astype(BF16)
        rows_ref[...] = jnp.zeros_like(rows_ref)

        lane = lax.broadcasted_iota(jnp.int32, (T, LANE), 1)
        sub = lax.broadcasted_iota(jnp.int32, (T, LANE), 0)
        subr = lax.broadcasted_iota(jnp.int32, (LANE, T), 0)
        da_col = jnp.zeros((T, LANE), F32)
        da_row = jnp.zeros((LANE, T), F32)
        dbs, dcs = [], []
        for g in range(G):
            gs = slice(g * GW, (g + 1) * GW)
            bgb = act[:, W + g * N:W + (g + 1) * N].astype(BF16)
            cgb = act[:, W + G * N + g * N:W + G * N + (g + 1) * N].astype(BF16)
            stg = sp_ref[0, :, gs]
            stb = stg.astype(BF16)
            dsn = ds_ref[:, gs]
            dsnb = dsn.astype(BF16)
            gm = _dot(cgb, stb)
            dc = _dot_nt(dgb[:, gs], stb)
            dsp = _dot_tn(cgb, dgb[:, gs])
            dxs_ = decx[:, gs] * _dot(bgb, dsnb)
            db = _dot_nt(xdecb[:, gs], dsnb)
            xdg = xd[:, gs]
            r12_ref[:, gs] = dy[:, gs] * ex[:, gs] * gm - xdg * dxs_
            rows_ref[0:1, gs] = _colsum(dsn * stg) * elx[:, gs]
            rows_ref[1:2, gs] = _colsum(xdg * dxs_)
            ds_ref[:, gs] = dsp + dsn * elx[:, gs]
            cb_g = _dot_nt(cgb, bgb)
            dcb = jnp.zeros((T, T), F32)
            for r in range(R):
                h = g * R + r
                hs = slice(h * HEAD_DIM, (h + 1) * HEAD_DIM)
                lm = jnp.where(low, jnp.exp(acum[:, h:h + 1] - acum_t[h:h + 1, :]), 0.0)
                mm = cb_g * lm
                dyb = dyb_ref[:, hs]
                dm = _dot_nt(dyb, xb_ref[:, hs])
                dx_ref[:, hs] = dxs_[:, r * HEAD_DIM:(r + 1) * HEAD_DIM] + _dot_tn(mm.astype(BF16), dyb)
                dcb = dcb + dm * lm
                qm = dm * mm
                da_col = jnp.where(lane == h, jnp.sum(qm, axis=1, keepdims=True), da_col)
                da_row = jnp.where(subr == h, jnp.sum(qm, axis=0, keepdims=True), da_row)
            dcbb = dcb.astype(BF16)
            dcs.append(dc + _dot(dcbb, bgb))
            dbs.append(db + _dot_tn(dcbb, cgb))
        dx = dx_ref[...]
        rows = _head_reduce(rows_ref[...])
        dlast = rows[0:1, :] + rows[1:2, :]
        da_col = da_col + _head_reduce(r12_ref[...]) + jnp.where(sub == T - 1, dlast, 0.0)
        dacum = da_col - da_row.T
        da = _dot_hi(upf, dacum)
        ddt = _head_reduce(dx * xs) + da * a_neg
        hacc_ref[1:2, :] += _colsum(da * dt) * a_neg
        ddt_raw = ddt * _sigmoid(dt_in)
        hacc_ref[0:1, :] += _colsum(ddt_raw)
        ddt_ref[...] = ddt_raw
        dact = jnp.concatenate([dsk_ref[...] * dy + dx * dtx] + dbs + dcs, axis=1)
        spre = _sigmoid(pre)
        dpre_ref[...] = dact * (spre * (1.0 + pre * (1.0 - spre)))

        @pl.when(i == nc - 1)
        def _():
            ch = lax.broadcasted_iota(jnp.int32, (W, LANE), 0)
            lo = lax.broadcasted_iota(jnp.int32, (W, LANE), 1) * HEAD_DIM
            hacc_ref[2:3, :] = _dot_hi(acc_ref[1:2, :], ((ch >= lo) & (ch < lo + HEAD_DIM)).astype(F32))

    rev = lambda i: (nc - 1 - i, 0)
    return pl.pallas_call(
        body, name="ssd_bwd", grid=(nc,),
        in_specs=[pl.BlockSpec((T, W), rev), pl.BlockSpec((T, W), rev), pl.BlockSpec((T, W), rev), pl.BlockSpec((T, CD), rev),
                  pl.BlockSpec((HALO, CD), lambda i: (jnp.maximum((nc - 1 - i) * (T // HALO) - 1, 0), 0)),
                  pl.BlockSpec((T, LANE), rev), pl.BlockSpec((1, N, HP), lambda i: (nc - 1 - i, 0, 0)),
                  _acc((CONV_K, CD)), _acc((1, CD)), _acc((1, LANE)), _acc((1, LANE)), _acc((1, W)), _acc((1, W))],
        out_specs=(pl.BlockSpec((T, W), rev), pl.BlockSpec((T, CD), rev), pl.BlockSpec((T, LANE), rev), _acc((8, W)),
                   _acc((8, LANE))),
        out_shape=(_S((L, W), BF16), _S((L, CD), F32), _S((L, LANE), F32), _S((8, W), F32), _S((8, LANE), F32)),
        scratch_shapes=[pltpu.VMEM((T + HALO, CD), F32), pltpu.VMEM((N, HP), F32), pltpu.VMEM((T, W), F32),
                        pltpu.VMEM((T, W), F32), pltpu.VMEM((T, W), BF16), pltpu.VMEM((T, W), BF16), pltpu.VMEM((T, W), F32),
                        pltpu.VMEM((T, W), F32), pltpu.VMEM((8, W), F32)],
        compiler_params=_params(("arbitrary",)))(dyn, y, z, xbc, xbc, dt_raw, sprev, cw, cb, dtb, alog, dsk, nw)


def _conv_bwd(dpre, xbc, cw):
    L, CD = xbc.shape
    tm = _pick(L, (MLP_TM, 128))
    nt = L // tm
    hb = tm // HALO

    def body(dp_ref, dn_ref, u_ref, up_ref, cw_ref, du_ref, acc_ref, extu_ref, extd_ref):
        i = pl.program_id(0)

        @pl.when(i == 0)
        def _():
            acc_ref[...] = jnp.zeros_like(acc_ref)

        dp = dp_ref[...]
        extu_ref[0:HALO, :] = jnp.where(i == 0, 0.0, up_ref[...])
        extu_ref[HALO:HALO + tm, :] = u_ref[...]
        extd_ref[0:tm, :] = dp
        extd_ref[tm:tm + HALO, :] = jnp.where(i == nt - 1, 0.0, dn_ref[...])
        du = cw_ref[CONV_K - 1:CONV_K, :] * dp
        acc_ref[CONV_K - 1:CONV_K, :] += _colsum(dp * u_ref[...])
        for k in range(CONV_K - 1):
            s = CONV_K - 1 - k
            du = du + cw_ref[k:k + 1, :] * extd_ref[s:s + tm, :]
            acc_ref[k:k + 1, :] += _colsum(dp * extu_ref[HALO - s:HALO - s + tm, :])
        acc_ref[CONV_K:CONV_K + 1, :] += _colsum(dp)
        du_ref[...] = du.astype(BF16)

    return pl.pallas_call(
        body, name="conv_bwd", grid=(nt,),
        in_specs=[_tile(CD, tm), pl.BlockSpec((HALO, CD), lambda i: (jnp.minimum((i + 1) * hb, nt * hb - 1), 0)),
                  _tile(CD, tm), pl.BlockSpec((HALO, CD), lambda i: (jnp.maximum(i * hb - 1, 0), 0)), _acc((CONV_K, CD))],
        out_specs=(_tile(CD, tm), _acc((8, CD))),
        out_shape=(_S((L, CD), BF16), _S((8, CD), F32)),
        scratch_shapes=[pltpu.VMEM((tm + HALO, CD), F32), pltpu.VMEM((tm + HALO, CD), F32)],
        compiler_params=_params(("arbitrary",)))(dpre, dpre, xbc, xbc, cw)


def _in_proj_bwd(dz, dxbc, dq, dkv, ddt, w_t, xhat0, rstd0, dh0p, vecs, dims):
    L, D = xhat0.shape
    W, CD, AW, KVW2 = dims["W"], dims["CD"], dims["AW"], dims["KVW2"]
    PROJ = w_t.shape[0]
    tm = _pick(L, (MLP_TM, 128))
    r_z, r_xbc, r_dt, r_q, r_kv = _proj_rows(dims)

    def body(dz_ref, dxbc_ref, dq_ref, dkv_ref, ddt_ref, w_ref, xh_ref, rs_ref, dh0_ref, v_ref, gx_ref, acc_ref):
        @pl.when(pl.program_id(0) == 0)
        def _():
            acc_ref[...] = jnp.zeros_like(acc_ref)

        du1 = _dot(dz_ref[...], w_ref[r_z[0]:r_z[1], :])
        du1 = du1 + _dot(dxbc_ref[...], w_ref[r_xbc[0]:r_xbc[1], :])
        du1 = du1 + _dot(dq_ref[...], w_ref[r_q[0]:r_q[1], :])
        du1 = du1 + _dot(dkv_ref[...].astype(BF16), w_ref[r_kv[0]:r_kv[1], :])
        du1 = du1 + _dot(ddt_ref[...].astype(BF16), w_ref[r_dt[0]:r_dt[1], :])
        xhat0 = xh_ref[...]
        h0 = xhat0 * v_ref[0:1, :] + v_ref[1:2, :]
        acc_ref[0:1, :] += _colsum(du1 * h0)
        acc_ref[1:2, :] += _colsum(du1)
        dh0 = dh0_ref[...] + du1 * (1.0 + v_ref[2:3, :])
        acc_ref[2:3, :] += _colsum(dh0 * xhat0)
        acc_ref[3:4, :] += _colsum(dh0)
        gx_ref[...] = _ln_bwd(dh0 * v_ref[0:1, :], xhat0, rs_ref[...])

    return pl.pallas_call(
        body, name="in_proj_bwd", grid=(L // tm,),
        in_specs=[_tile(W, tm), _tile(CD, tm), _tile(AW, tm), _tile(KVW2, tm), _tile(LANE, tm), _res((PROJ, D)),
                  _tile(D, tm), _tile(1, tm), _tile(D, tm), _acc((8, D))],
        out_specs=(_tile(D, tm), _acc((8, D))),
        out_shape=(_S((L, D), F32), _S((8, D), F32)),
        compiler_params=_params(("arbitrary",)))(dz, dxbc, dq, dkv, ddt, w_t, xhat0, rstd0, dh0p, vecs)


_WEIGHTS = ['ln_in_g', 'ln_in_b', 'ada_w', 'ada_b', 'w_in', 'conv_w', 'conv_b', 'dt_bias', 'a_log', 'd_skip', 'ssd_norm_w',
            'attn_sinks', 'w_out', 'ln1_g', 'ln1_b', 'w_ff1', 'b_ff1', 'w_ff2', 'b_ff2', 'ln2_g', 'ln2_b']
_BIG = ('w_in', 'w_out', 'w_ff1', 'w_ff2')
_SMALL = ('ada_b', 'ln_in_g', 'ln_in_b', 'conv_b', 'dt_bias', 'a_log', 'd_skip', 'ssd_norm_w', 'attn_sinks', 'ln1_g', 'ln1_b',
          'b_ff1', 'b_ff2', 'ln2_g', 'ln2_b')


def _pad_lanes(v, n=None):
    v = v.reshape(1, -1)
    n = n or -(-v.shape[1] // LANE) * LANE
    return jnp.pad(v, ((0, 0), (0, n - v.shape[1])))


def _vec8(rows, D):
    rows = [r.reshape(1, D) for r in rows]
    return jnp.concatenate(rows + [jnp.zeros((8 - len(rows), D), F32)], axis=0)


def _pack(segs):
    flat, offs, sizes, o = [], [], [], 0
    for s in segs:
        p = _pad_lanes(s)
        flat.append(p)
        offs.append(o)
        sizes.append(s.size)
        o += p.shape[1]
    total = -(-o // (8 * LANE)) * (8 * LANE)
    if total > o:
        flat.append(jnp.zeros((1, total - o), F32))
    return jnp.concatenate(flat, axis=1).reshape(8, total // 8), offs, sizes


def kernel(x, c, ln_in_g, ln_in_b, ada_w, ada_b, w_in, conv_w, conv_b, dt_bias, a_log, d_skip, ssd_norm_w, attn_sinks, w_out, ln1_g, ln1_b, w_ff1, b_ff1, w_ff2, b_ff2, ln2_g, ln2_b, loss_target, m_ln_in_g, m_ln_in_b, m_ada_w, m_ada_b, m_w_in, m_conv_w, m_conv_b, m_dt_bias, m_a_log, m_d_skip, m_ssd_norm_w, m_attn_sinks, m_w_out, m_ln1_g, m_ln1_b, m_w_ff1, m_b_ff1, m_w_ff2, m_b_ff2, m_ln2_g, m_ln2_b, v_ln_in_g, v_ln_in_b, v_ada_w, v_ada_b, v_w_in, v_conv_w, v_conv_b, v_dt_bias, v_a_log, v_d_skip, v_ssd_norm_w, v_attn_sinks, v_w_out, v_ln1_g, v_ln1_b, v_w_ff1, v_b_ff1, v_w_ff2, v_b_ff2, v_ln2_g, v_ln2_b):
    wts = dict(ln_in_g=ln_in_g, ln_in_b=ln_in_b, ada_w=ada_w, ada_b=ada_b, w_in=w_in, conv_w=conv_w, conv_b=conv_b,
               dt_bias=dt_bias, a_log=a_log, d_skip=d_skip, ssd_norm_w=ssd_norm_w, attn_sinks=attn_sinks, w_out=w_out,
               ln1_g=ln1_g, ln1_b=ln1_b, w_ff1=w_ff1, b_ff1=b_ff1, w_ff2=w_ff2, b_ff2=b_ff2, ln2_g=ln2_g, ln2_b=ln2_b)
    ms = dict(ln_in_g=m_ln_in_g, ln_in_b=m_ln_in_b, ada_w=m_ada_w, ada_b=m_ada_b, w_in=m_w_in, conv_w=m_conv_w,
              conv_b=m_conv_b, dt_bias=m_dt_bias, a_log=m_a_log, d_skip=m_d_skip, ssd_norm_w=m_ssd_norm_w,
              attn_sinks=m_attn_sinks, w_out=m_w_out, ln1_g=m_ln1_g, ln1_b=m_ln1_b, w_ff1=m_w_ff1, b_ff1=m_b_ff1,
              w_ff2=m_w_ff2, b_ff2=m_b_ff2, ln2_g=m_ln2_g, ln2_b=m_ln2_b)
    vs = dict(ln_in_g=v_ln_in_g, ln_in_b=v_ln_in_b, ada_w=v_ada_w, ada_b=v_ada_b, w_in=v_w_in, conv_w=v_conv_w,
              conv_b=v_conv_b, dt_bias=v_dt_bias, a_log=v_a_log, d_skip=v_d_skip, ssd_norm_w=v_ssd_norm_w,
              attn_sinks=v_attn_sinks, w_out=v_w_out, ln1_g=v_ln1_g, ln1_b=v_ln1_b, w_ff1=v_w_ff1, b_ff1=v_b_ff1,
              w_ff2=v_w_ff2, b_ff2=v_b_ff2, ln2_g=v_ln2_g, ln2_b=v_ln2_b)

    L, D = x.shape[1], x.shape[2]
    depth = w_in.shape[0]
    assert depth == 1 and x.shape[0] == 1 and L % CHUNK == 0
    W = D
    H = W // HEAD_DIM
    CD = W + 2 * SSD_GROUPS * SSD_STATE
    AW = D
    AH = AW // HEAD_DIM
    KV = AH // GQA
    KVW2 = 2 * KV * HEAD_DIM
    PROJ = W + CD + H + AW + KVW2
    FF = w_ff1.shape[2] * N_DEV
    MIX = w_out.shape[1] * N_DEV
    assert w_in.shape[2] * N_DEV == PROJ and MIX == W + AW and H <= LANE and AH <= LANE
    dims = dict(W=W, H=H, CD=CD, AW=AW, AH=AH, KV=KV, KVW2=KVW2)
    alpha = (2.0 * depth) ** 0.25
    C6 = ada_w.shape[2]
    CW = conv_w.shape[2]

    ax, ay, ac = _my_pos()
    me = 4 * ax + 2 * ay + ac
    x2 = x.reshape(L, D)
    tgt = loss_target.reshape(L, D)
    r1 = lambda a: a.reshape(1, -1)

    ada_b_cols = lax.dynamic_slice(ada_b, (0, me * C6), (1, C6))
    cs_all, mod = _mod_fwd(c, ada_w[0], ada_b_cols)
    sh1, sc1, g1, sh2, sc2, g2 = [r1(t) for t in jnp.split(mod.reshape(-1), 6)]

    wg_in, cwg = _ag_weights([w_in[0].T.astype(BF16), conv_w[0]], cs_all)
    shards2 = [w_out[0].astype(BF16), w_ff1[0].astype(BF16), w_ff2[0].astype(BF16)]
    lands2 = [lax.dynamic_update_slice(lax.empty((N_DEV,) + s.shape, s.dtype), s[None], (me, 0, 0)) for s in shards2]
    ag_ss, ag_rs, ag_arr, ag_token = _split_start(shards2 + lands2, _plan_gather(3), cwg, "ag_ici_start")
    sh1 = sh1 + ag_token[0:1, 0:1]
    w_pad = wg_in.reshape(PROJ, D)
    cw_full = cwg.transpose(1, 0, 2).reshape(CONV_K, CD)

    dtb = _pad_lanes(dt_bias, LANE)
    alog = _pad_lanes(a_log, LANE)
    dsk = jnp.repeat(d_skip.reshape(-1), HEAD_DIM).reshape(1, W)
    sinks = attn_sinks.reshape(-1)
    g_in, b_in = r1(ln_in_g), r1(ln_in_b)

    xhat0, rstd0, u1, z, xbc, q, kv, dt_raw = _ln_in_proj(x2, g_in, b_in, sc1, sh1, w_pad, dims)
    y, yn, sprev = _conv_ssd(xbc, dt_raw, z, cw_full, conv_b, dtb, alog, dsk, ssd_norm_w, dims)
    ag_arr = _split_wait(ag_ss, ag_rs, ag_arr, _plan_gather(3), yn, "ag_ici_wait")
    fw_ss, fw_rs, ag_land, fw_token = _split_start(ag_arr[3:], _plan_forward(3), yn, "ag_fwd_start")
    o = _swa_fwd(q, kv, sinks + fw_token[0, 0], dims)
    wg_out, wg_ff1, wg_ff2 = _split_wait(fw_ss, fw_rs, ag_land, _plan_forward(3), o, "ag_fwd_wait")
    w_out_full = wg_out.reshape(MIX, D)
    w1_full = wg_ff1.transpose(1, 0, 2).reshape(D, FF)
    w2_full = wg_ff2.reshape(FF, D)
    mix, xhat1, rstd1, u2 = _out_proj_ln1(yn, o, w_out_full, xhat0, _vec8([g_in, b_in, g1, ln1_g, ln1_b, sc2, sh2], D), alpha)
    rr, dr2, acc_f, loss_loc = _mlp_loss(u2, w1_full, w2_full, xhat1, tgt,
                                         _vec8([ln1_g, ln1_b, g2, ln2_g, ln2_b, b_ff2], D), b_ff1, alpha)

    df, da, gb2, gb1 = _mlp_bwd_a(dr2, rr, w2_full, g2)
    gw_ff2 = _matmul_tn(rr, df, "gw_ff2", square_a=True)
    gw_ff1t = _matmul_tn(da, u2, "gw_ff1")
    dmix, dh0p, dyn, do, acc_b = _mlp_bwd_b(da, w1_full, dr2, xhat1, rstd1, mix, w_out_full,
                                            _vec8([ln1_g, ln1_b, sc2, g1], D), alpha, W)
    gw_out = jnp.concatenate([_matmul_tn(yn, dmix, "gw_out_ssd"), _matmul_tn(o, dmix, "gw_out_attn")], axis=0)

    core = jnp.reshape(ac, (1,)).astype(jnp.int32)
    blocked1 = [gw_out.reshape(N_DEV, MIX // N_DEV, D), gw_ff1t.reshape(N_DEV, FF // N_DEV, D),
                gw_ff2.reshape(N_DEV, FF // N_DEV, D)]
    lands1 = [lax.empty(b.shape, b.dtype) for b in blocked1]
    rs_ss, rs_rs, rs_arr, rs_token = _split_start(blocked1 + lands1, _plan_scatter_all(3), do, "rs_all_start")
    dq, dkv, dsink = _swa_bwd(q, kv, do, sinks + rs_token[0, 0], dims)
    dz, dpre, ddt, acc_s, hacc = _ssd_bwd(dyn, y, z, xbc, dt_raw, sprev, cw_full, conv_b, dtb + rs_token[0:1, 0:1], alog, dsk,
                                          ssd_norm_w, dims)
    dxbc, acc_c = _conv_bwd(dpre, xbc, cw_full)
    gw_in = _gw_in((dz, dxbc, ddt, dq, dkv), u1, dims)

    blocked2 = [gw_in.reshape(N_DEV, PROJ // N_DEV, D)]
    pairs2 = [_pair_sum(b, r, core) for b, r in zip(blocked2, _rs_d2d(blocked2, "rs_d2d_2"))]
    lands2 = [lax.empty(p.shape, p.dtype) for p in pairs2]
    r2_ss, r2_rs, r2_arr, r2_token = _split_start(pairs2 + lands2, _plan_scatter(1), gw_in, "rs_ici_start_2")
    grad_x, acc_i = _in_proj_bwd(dz, dxbc, dq, dkv, ddt, w_pad, xhat0, rstd0, dh0p,
                                 _vec8([g_in, b_in, sc1], D) + r2_token[0:1, 0:1], dims)

    dmod = jnp.concatenate([acc_i[1], acc_i[0], acc_b[4], acc_b[1], acc_b[0], acc_f[2]])
    small_g = dict(ada_b=dmod, ln_in_g=acc_i[2], ln_in_b=acc_i[3], conv_b=acc_c[CONV_K], dt_bias=hacc[0, :H], a_log=hacc[1, :H],
                   d_skip=hacc[2, :H], ssd_norm_w=acc_s[0], attn_sinks=dsink[0, :AH], ln1_g=acc_b[2], ln1_b=acc_b[3],
                   b_ff1=gb1[0], b_ff2=gb2[0], ln2_g=acc_f[0], ln2_b=acc_f[1])
    segs = [small_g[n] for n in _SMALL] + [acc_c[:CONV_K].reshape(-1), loss_loc[0, :1]]
    pack, offs, sizes = _pack(segs)
    gathered, summed = _small_gather_sum(pack)
    gathered = gathered.reshape(N_DEV, -1)
    summed = summed.reshape(-1)
    seg = lambda k: summed[offs[k]:offs[k] + sizes[k]]
    grads = {n: seg(k).reshape(wts[n].shape) for k, n in enumerate(_SMALL)}
    gcw_full = seg(len(_SMALL)).reshape(CONV_K, CD)
    grads['conv_w'] = lax.dynamic_slice(gcw_full, (0, me * CW), (CONV_K, CW)).reshape(conv_w.shape)
    loss = seg(len(_SMALL) + 1)[0]

    names = list(_SMALL) + ['conv_w']
    pw, poffs, psizes = _pack([wts[n] for n in names])
    pg, _, _ = _pack([grads[n] for n in names])
    pm, _, _ = _pack([ms[n] for n in names])
    pv, _, _ = _pack([vs[n] for n in names])
    pd, pm2, pv2 = [t.reshape(-1) for t in _adamw(pw, pg, pm, pv)]
    deltas, new_m, new_v = {}, {}, {}
    for k, n in enumerate(names):
        sl = slice(poffs[k], poffs[k] + psizes[k])
        deltas[n], new_m[n], new_v[n] = (t[sl].reshape(wts[n].shape) for t in (pd, pm2, pv2))

    dmod_cols = lax.dynamic_slice(gathered, (0, offs[0] + me * C6), (N_DEV, C6))
    pad16 = lambda t: jnp.concatenate([t, jnp.zeros((16 - N_DEV,) + t.shape[1:], t.dtype)], axis=0)
    g_, d_, m_, v_ = _ada_grad_adamw(pad16(cs_all), pad16(dmod_cols), ada_w[0], m_ada_w[0], v_ada_w[0])
    grads['ada_w'], deltas['ada_w'], new_m['ada_w'], new_v['ada_w'] = (t[None] for t in (g_, d_, m_, v_))

    rs_arr = _split_wait(rs_ss, rs_rs, rs_arr, _plan_scatter_all(3), g_, "rs_all_wait")
    mychip = 2 * ax + ay
    chips = jnp.stack([(mychip + k) % N_CHIP for k in range(N_CHIP)]).astype(jnp.int32)
    devs = jnp.stack([(me + k) % N_DEV for k in range(N_DEV)]).astype(jnp.int32)
    for n, own, land in zip(('w_out', 'w_ff1', 'w_ff2'), rs_arr[:3], rs_arr[3:]):
        g_, d_, m_, v_ = _sum_adamw_split(own, land, devs, wts[n][0], ms[n][0], vs[n][0], transposed=(n == 'w_ff1'))
        grads[n], deltas[n], new_m[n], new_v[n] = (t[None] for t in (g_, d_, m_, v_))
    r2_arr = _split_wait(r2_ss, r2_rs, r2_arr, _plan_scatter(1), g_, "rs_ici_wait_2")
    g_, d_, m_, v_ = _sum_adamw_split(r2_arr[0], r2_arr[1], chips, wts['w_in'][0].T, ms['w_in'][0].T, vs['w_in'][0].T)
    grads['w_in'], deltas['w_in'], new_m['w_in'], new_v['w_in'] = (t.T[None] for t in (g_, d_, m_, v_))

    return (loss, grad_x.reshape(x.shape), *[grads[n] for n in _WEIGHTS], *[deltas[n] for n in _WEIGHTS],
            *[new_m[n] for n in _WEIGHTS], *[new_v[n] for n in _WEIGHTS])
```

```python
import functools
import math

import numpy as np
import jax
import jax.numpy as jnp
from jax import lax
from jax.experimental import pallas as pl
from jax.experimental.pallas import tpu as pltpu

F32 = jnp.float32
BF16 = jnp.bfloat16
MESH = pl.DeviceIdType.MESH

N_DEV = 8
N_CHIP = 4
HEAD_DIM = 64
SSD_GROUPS = 2
SSD_STATE = 128
CHUNK = 128
CONV_K = 4
GQA = 8
LANE = 128
HALO = 8
LN_EPS = 1e-5
RMS_EPS = 1e-5
NEG = -1e30
ADAM_LR, ADAM_B1, ADAM_B2, ADAM_EPS, ADAM_WD, ADAM_STEP = 0.001, 0.9, 0.999, 1e-08, 0.01, 10
V7X_VMEM_BYTES = 64 * 1024 * 1024
VMEM_LIMIT = V7X_VMEM_BYTES - 8 * 1024 * 1024
HI = lax.Precision.HIGHEST
MLP_TM = 512
MLP_SUB = 512
MLP_FC = 512
CONV_TM = 512
CONV_CB = 2048
ROW_CHUNK = 32


def _alibi_slopes(n):
    def pow2(m):
        start = 2.0 ** (-8.0 / m)
        return [start ** (i + 1) for i in range(m)]
    if math.log2(n).is_integer():
        s = pow2(n)
    else:
        c = 2 ** math.floor(math.log2(n))
        s = pow2(c) + pow2(2 * c)[0::2][: n - c]
    return [float(v) for v in np.array(s, dtype=np.float32)]


def _dot(a, b):
    return jnp.dot(a, b, preferred_element_type=F32)


def _dot_nt(a, b):
    return lax.dot_general(a, b, (((1,), (1,)), ((), ())), preferred_element_type=F32)


def _dot_tn(a, b):
    return lax.dot_general(a, b, (((0,), (0,)), ((), ())), preferred_element_type=F32)


def _dot_hi(a, b):
    return jnp.dot(a, b, precision=HI, preferred_element_type=F32)


def _sigmoid(x):
    return 1.0 / (1.0 + jnp.exp(-x))


def _softplus(x):
    return jnp.maximum(x, 0.0) + jnp.log(1.0 + jnp.exp(-jnp.abs(x)))


def _mean(x):
    return jnp.mean(x, axis=-1, keepdims=True)


def _ln_fwd(x):
    xc = x - _mean(x)
    rstd = lax.rsqrt(_mean(xc * xc) + LN_EPS)
    return xc * rstd, rstd


def _ln_bwd(dxhat, xhat, rstd):
    return rstd * (dxhat - _mean(dxhat) - xhat * _mean(dxhat * xhat))


def _colsum(x):
    return jnp.sum(x, axis=0, keepdims=True)


def _params(sem):
    return pltpu.CompilerParams(dimension_semantics=sem, vmem_limit_bytes=VMEM_LIMIT)


def _tile(i_map_cols, tm):
    return pl.BlockSpec((tm, i_map_cols), lambda i: (i, 0))


def _res(shape):
    return pl.BlockSpec(shape, lambda *_: (0,) * len(shape), pipeline_mode=pl.Buffered(1))


def _acc(shape):
    return pl.BlockSpec(shape, lambda *_: (0,) * len(shape))


def _S(shape, dtype):
    return jax.ShapeDtypeStruct(shape, dtype)


def _my_pos():
    return lax.axis_index("x"), lax.axis_index("y"), lax.axis_index("c")


def _peer(pos, k):
    x, y, c = pos
    px = 1 - x if k & 4 else x
    py = 1 - y if k & 2 else y
    pc = 1 - c if k & 1 else c
    return (px, py, pc)


def _lin(p):
    return 4 * p[0] + 2 * p[1] + p[2]


def _mod_fwd(c_loc, ada_w_loc, ada_b_cols):
    D = c_loc.shape[1]
    C6 = ada_w_loc.shape[1]

    def body(c_ref, w_ref, b_ref, cs_ref, mod_ref, call_ref, modp_ref, ssem, rsem):
        pos = _my_pos()
        me = _lin(pos)
        call_ref[me] = c_ref[...]
        sends = []
        for k in range(1, N_DEV):
            cp = pltpu.make_async_remote_copy(src_ref=c_ref, dst_ref=call_ref.at[me], send_sem=ssem.at[k - 1],
                                              recv_sem=rsem.at[k - 1], device_id=_peer(pos, k), device_id_type=MESH)
            cp.start()
            sends.append(cp)
        for k in range(1, N_DEV):
            src = _lin(_peer(pos, k))
            pltpu.make_async_remote_copy(src_ref=c_ref, dst_ref=call_ref.at[src], send_sem=ssem.at[k - 1],
                                         recv_sem=rsem.at[k - 1], device_id=pos, device_id_type=MESH).wait_recv()
        for cp in sends:
            cp.wait_send()
        call = jnp.concatenate([call_ref[b] for b in range(N_DEV)], axis=0)
        cs = call * _sigmoid(call)
        cs_ref[...] = cs
        modp = _dot(cs.astype(BF16), w_ref[...].astype(BF16)) + b_ref[...]
        for b in range(N_DEV):
            modp_ref[b] = modp[b:b + 1, :]
        mod_ref[me] = modp_ref[me]
        sends = []
        for k in range(1, N_DEV):
            peer = _peer(pos, k)
            cp = pltpu.make_async_remote_copy(src_ref=modp_ref.at[_lin(peer)], dst_ref=mod_ref.at[me],
                                              send_sem=ssem.at[N_DEV - 2 + k], recv_sem=rsem.at[N_DEV - 2 + k],
                                              device_id=peer, device_id_type=MESH)
            cp.start()
            sends.append(cp)
        for k in range(1, N_DEV):
            src = _lin(_peer(pos, k))
            pltpu.make_async_remote_copy(src_ref=modp_ref.at[src], dst_ref=mod_ref.at[src],
                                         send_sem=ssem.at[N_DEV - 2 + k], recv_sem=rsem.at[N_DEV - 2 + k],
                                         device_id=pos, device_id_type=MESH).wait_recv()
        for cp in sends:
            cp.wait_send()

    vm = pl.BlockSpec(memory_space=pltpu.VMEM)
    return pl.pallas_call(
        body, name="mod_fwd",
        out_shape=(_S((N_DEV, D), F32), _S((N_DEV, 1, C6), F32)),
        in_specs=[vm, vm, vm], out_specs=(vm, vm),
        scratch_shapes=[pltpu.VMEM((N_DEV, 1, D), F32), pltpu.VMEM((N_DEV, 1, C6), F32),
                        pltpu.SemaphoreType.DMA((2 * (N_DEV - 1),)), pltpu.SemaphoreType.DMA((2 * (N_DEV - 1),))],
        compiler_params=pltpu.CompilerParams(vmem_limit_bytes=VMEM_LIMIT),
    )(c_loc, ada_w_loc, ada_b_cols)


def _small_gather_sum(pack):
    P8 = pack.shape[1]

    def body(p_ref, gat_ref, sum_ref, ssem, rsem):
        pos = _my_pos()
        me = _lin(pos)
        gat_ref[me] = p_ref[...]
        sends = []
        for k in range(1, N_DEV):
            cp = pltpu.make_async_remote_copy(src_ref=p_ref, dst_ref=gat_ref.at[me], send_sem=ssem.at[k - 1],
                                              recv_sem=rsem.at[k - 1], device_id=_peer(pos, k), device_id_type=MESH)
            cp.start()
            sends.append(cp)
        for k in range(1, N_DEV):
            src = _lin(_peer(pos, k))
            pltpu.make_async_remote_copy(src_ref=p_ref, dst_ref=gat_ref.at[src], send_sem=ssem.at[k - 1],
                                         recv_sem=rsem.at[k - 1], device_id=pos, device_id_type=MESH).wait_recv()
        for cp in sends:
            cp.wait_send()
        acc = gat_ref[0]
        for j in range(1, N_DEV):
            acc = acc + gat_ref[j]
        sum_ref[...] = acc

    vm = pl.BlockSpec(memory_space=pltpu.VMEM)
    return pl.pallas_call(
        body, name="small_gather_sum",
        out_shape=(_S((N_DEV, 8, P8), F32), _S((8, P8), F32)),
        in_specs=[vm], out_specs=(vm, vm),
        scratch_shapes=[pltpu.SemaphoreType.DMA((N_DEV - 1,)), pltpu.SemaphoreType.DMA((N_DEV - 1,))],
        compiler_params=pltpu.CompilerParams(vmem_limit_bytes=VMEM_LIMIT),
    )(pack)


def _ag_weights(shards, after):
    n = len(shards)

    def body(*refs):
        ins, outs = refs[:n], refs[n + 1:2 * n + 1]
        ssem, rsem, lsem = refs[2 * n + 1:]
        x, y, c = pos = _my_pos()
        me = _lin(pos)
        sib = (x, y, 1 - c)
        chips = [(1 - x, y), (x, 1 - y), (1 - x, 1 - y)]

        def copy(a, k, block, to, src=None):
            return pltpu.make_async_remote_copy(
                src_ref=outs[a].at[block] if src is None else src, dst_ref=outs[a].at[block],
                send_sem=ssem.at[a * 7 + k], recv_sem=rsem.at[a * 7 + k], device_id=to, device_id_type=MESH)

        local = [pltpu.make_async_copy(ins[a], outs[a].at[me], lsem.at[a]) for a in range(n)]
        for cp in local:
            cp.start()
        first = []
        for a in range(n):
            first.append(copy(a, 0, me, sib, src=ins[a]))
            first += [copy(a, 1 + j, me, (*chip, c), src=ins[a]) for j, chip in enumerate(chips)]
        for cp in first:
            cp.start()
        passed = []
        for a in range(n):
            for j, chip in enumerate(chips):
                blk = _lin((*chip, c))
                copy(a, 1 + j, blk, pos).wait_recv()
                cp = copy(a, 4 + j, blk, sib)
                cp.start()
                passed.append(cp)
        for a in range(n):
            copy(a, 0, _lin(sib), pos).wait_recv()
            for j, chip in enumerate(chips):
                copy(a, 4 + j, _lin((*chip, 1 - c)), pos).wait_recv()
        for cp in first + passed:
            cp.wait_send()
        for cp in local:
            cp.wait()

    hbm = pl.BlockSpec(memory_space=pl.ANY)
    return pl.pallas_call(
        body, name="ag_weights",
        out_shape=tuple(_S((N_DEV,) + s.shape, s.dtype) for s in shards),
        in_specs=[hbm] * (n + 1), out_specs=tuple([hbm] * n),
        scratch_shapes=[pltpu.SemaphoreType.DMA((7 * n,)), pltpu.SemaphoreType.DMA((7 * n,)),
                        pltpu.SemaphoreType.DMA((n,))],
    )(*shards, after)


def _rs_d2d(blocked, name):
    n = len(blocked)

    def body(*refs):
        ins, outs = refs[:n], refs[n:2 * n]
        ssem, rsem = refs[2 * n:]
        x, y, c = pos = _my_pos()
        sib = (x, y, 1 - c)
        cps = []
        for a in range(n):
            for j in range(N_CHIP):
                cp = pltpu.make_async_remote_copy(
                    src_ref=ins[a].at[2 * j + (1 - c)], dst_ref=outs[a].at[j], send_sem=ssem.at[a * N_CHIP + j],
                    recv_sem=rsem.at[a * N_CHIP + j], device_id=sib, device_id_type=MESH)
                cp.start()
                cps.append(cp)
        for cp in cps:
            cp.wait_recv()
        for cp in cps:
            cp.wait_send()

    hbm = pl.BlockSpec(memory_space=pl.ANY)
    return pl.pallas_call(
        body, name=name,
        out_shape=tuple(_S((N_CHIP,) + b.shape[1:], b.dtype) for b in blocked),
        in_specs=[hbm] * n, out_specs=tuple([hbm] * n),
        scratch_shapes=[pltpu.SemaphoreType.DMA((N_CHIP * n,)), pltpu.SemaphoreType.DMA((N_CHIP * n,))],
    )(*blocked)


_HBM = pl.BlockSpec(memory_space=pltpu.HBM)
_SEM = pl.BlockSpec(memory_space=pltpu.SEMAPHORE)
_ANY = pl.BlockSpec(memory_space=pl.ANY)
_EFFECT = pltpu.SideEffectType.DATAFLOW_SIDE_EFFECTING


def _in_hbm(a):
    return pltpu.with_memory_space_constraint(a, pltpu.HBM)


def _plan_gather(n):
    def copies(pos):
        x, y, c = pos
        out = []
        for a in range(n):
            for dev in [(x, y, 1 - c)] + [(*_peer(pos, 2 * k)[:2], c) for k in range(1, N_CHIP)]:
                out.append((a, None, n + a, _lin(pos), dev, _lin(dev)))
        return out
    return copies


def _plan_forward(n):
    def copies(pos):
        x, y, c = pos
        out = []
        for a in range(n):
            for k in range(1, N_CHIP):
                tx, ty, _ = _peer(pos, 2 * k)
                out.append((a, _lin((tx, ty, c)), a, _lin((tx, ty, c)), (x, y, 1 - c), _lin((tx, ty, 1 - c))))
        return out
    return copies


def _plan_scatter_all(n):
    def copies(pos):
        out = []
        for a in range(n):
            for k in range(1, N_DEV):
                dev = _peer(pos, k)
                out.append((a, _lin(dev), n + a, _lin(pos), dev, _lin(dev)))
        return out
    return copies


def _plan_scatter(n):
    def copies(pos):
        x, y, c = pos
        out = []
        for a in range(n):
            for k in range(1, N_CHIP):
                tx, ty, _ = _peer(pos, 2 * k)
                out.append((a, 2 * tx + ty, n + a, 2 * x + y, (tx, ty, c), 2 * tx + ty))
        return out
    return copies


def _split_copy(refs, cp, ssem, rsem, i, arrival):
    si, s_slot, di, d_slot, dev, a_slot = cp
    return pltpu.make_async_remote_copy(
        src_ref=refs[si] if s_slot is None else refs[si].at[s_slot], dst_ref=refs[di].at[a_slot if arrival else d_slot],
        send_sem=ssem.at[i], recv_sem=rsem.at[i], device_id=dev, device_id_type=MESH)


def _split_start(arrays, copies, after, name):
    n = len(arrays)
    n_cp = len(copies((0, 0, 0)))

    def body(*refs):
        ssem, rsem, token = refs[n + 1], refs[n + 2], refs[-1]
        for i, cp in enumerate(copies(_my_pos())):
            _split_copy(refs, cp, ssem, rsem, i, False).start()
        token[...] = jnp.zeros_like(token)

    res = pl.pallas_call(
        body, name=name,
        out_shape=(pltpu.SemaphoreType.DMA((n_cp,)), pltpu.SemaphoreType.DMA((n_cp,)),
                   *[pltpu.HBM(a.shape, a.dtype) for a in arrays], _S((8, LANE), F32)),
        in_specs=[_HBM] * n + [_ANY],
        out_specs=(_SEM, _SEM, *[_HBM] * n, pl.BlockSpec(memory_space=pltpu.VMEM)),
        input_output_aliases={a: 2 + a for a in range(n)},
        compiler_params=pltpu.CompilerParams(has_side_effects=_EFFECT),
    )(*[_in_hbm(a) for a in arrays], after)
    return res[0], res[1], list(res[2:2 + n]), res[-1]


def _split_wait(ssem, rsem, arrays, copies, after, name):
    n = len(arrays)

    def body(*refs):
        for i, cp in enumerate(copies(_my_pos())):
            d = _split_copy(refs, cp, refs[n], refs[n + 1], i, True)
            d.wait_send()
            d.wait_recv()

    res = pl.pallas_call(
        body, name=name,
        out_shape=tuple(pltpu.HBM(a.shape, a.dtype) for a in arrays),
        in_specs=[_HBM] * n + [_SEM, _SEM, _ANY], out_specs=tuple([_HBM] * n),
        input_output_aliases={a: a for a in range(n)},
        compiler_params=pltpu.CompilerParams(has_side_effects=_EFFECT),
    )(*arrays, ssem, rsem, after)
    return list(res)


def _row_tile(R, itemsize_rows=16, cap=256):
    t = cap - cap % itemsize_rows
    while t >= itemsize_rows:
        if R % t == 0:
            return t
        t -= itemsize_rows
    return R


def _pair_sum(blocked, recv, core):
    _, R, C = blocked.shape
    tr = _row_tile(R)

    def body(ids_ref, a_ref, b_ref, o_ref):
        del ids_ref
        o_ref[...] = (a_ref[...] + b_ref[...]).astype(BF16)

    gs = pltpu.PrefetchScalarGridSpec(
        num_scalar_prefetch=1, grid=(N_CHIP, R // tr),
        in_specs=[pl.BlockSpec((1, tr, C), lambda j, r, ids: (2 * j + ids[0], r, 0)),
                  pl.BlockSpec((1, tr, C), lambda j, r, ids: (j, r, 0))],
        out_specs=pl.BlockSpec((1, tr, C), lambda j, r, ids: (j, r, 0)))
    return pl.pallas_call(body, name="pair_sum", grid_spec=gs, out_shape=_S((N_CHIP, R, C), BF16),
                          compiler_params=_params(("arbitrary", "arbitrary")))(core, blocked, recv)


def _adamw_math(w, g, m, v):
    m2 = ADAM_B1 * m + (1.0 - ADAM_B1) * g
    v2 = ADAM_B2 * v + (1.0 - ADAM_B2) * (g * g)
    m_hat = m2 / (1.0 - ADAM_B1 ** ADAM_STEP)
    v_hat = v2 / (1.0 - ADAM_B2 ** ADAM_STEP)
    delta = -ADAM_LR * (m_hat / (jnp.sqrt(v_hat) + ADAM_EPS) + ADAM_WD * w)
    return delta, m2, v2


def _sum_adamw_split(pairs, land, chips, w, m, v, transposed=False):
    R, C = w.shape
    n_slots = chips.shape[0]
    tr = _row_tile(R, 128 if transposed else 16)

    def body(ids_ref, *refs):
        del ids_ref
        parts, (w_ref, m_ref, v_ref, g_ref, d_ref, m2_ref, v2_ref) = refs[:n_slots], refs[n_slots:]
        g = parts[0][0].astype(F32)
        for p_ref in parts[1:]:
            g = g + p_ref[0].astype(F32)
        if transposed:
            g = g.T
        g_ref[...] = g
        d_ref[...], m2_ref[...], v2_ref[...] = _adamw_math(w_ref[...], g, m_ref[...], v_ref[...])

    t = pl.BlockSpec((tr, C), lambda r, ids: (r, 0))
    if transposed:
        slot = lambda k: pl.BlockSpec((1, C, tr), lambda r, ids: (ids[k], 0, r))
    else:
        slot = lambda k: pl.BlockSpec((1, tr, C), lambda r, ids: (ids[k], r, 0))
    gs = pltpu.PrefetchScalarGridSpec(num_scalar_prefetch=1, grid=(R // tr,),
                                      in_specs=[slot(k) for k in range(n_slots)] + [t, t, t], out_specs=(t, t, t, t))
    return pl.pallas_call(body, name="sum_adamw_split", grid_spec=gs, out_shape=tuple(_S((R, C), F32) for _ in range(4)),
                          compiler_params=_params(("arbitrary",)))(chips, pairs, *[land] * (n_slots - 1), w, m, v)


def _adamw(w, g, m, v):
    R, C = w.shape
    tr = _row_tile(R, 8)

    def body(w_ref, g_ref, m_ref, v_ref, d_ref, m2_ref, v2_ref):
        d_ref[...], m2_ref[...], v2_ref[...] = _adamw_math(w_ref[...], g_ref[...], m_ref[...], v_ref[...])

    t = pl.BlockSpec((tr, C), lambda r: (r, 0))
    return pl.pallas_call(body, name="adamw", grid=(R // tr,), in_specs=[t, t, t, t], out_specs=(t, t, t),
                          out_shape=tuple(_S((R, C), F32) for _ in range(3)),
                          compiler_params=_params(("arbitrary",)))(w, g, m, v)


def _ada_grad_adamw(cs16, dmod16, w, m, v):
    D, C6 = w.shape
    tr = _row_tile(D, 8, 256)

    def body(cs_ref, dm_ref, w_ref, m_ref, v_ref, g_ref, d_ref, m2_ref, v2_ref):
        g = _dot_tn(cs_ref[...].astype(BF16), dm_ref[...].astype(BF16))
        g_ref[...] = g
        d_ref[...], m2_ref[...], v2_ref[...] = _adamw_math(w_ref[...], g, m_ref[...], v_ref[...])

    t = pl.BlockSpec((tr, C6), lambda r: (r, 0))
    return pl.pallas_call(
        body, name="ada_grad_adamw", grid=(D // tr,),
        in_specs=[pl.BlockSpec((16, tr), lambda r: (0, r)), _acc((16, C6)), t, t, t], out_specs=(t, t, t, t),
        out_shape=tuple(_S((D, C6), F32) for _ in range(4)), compiler_params=_params(("arbitrary",)))(cs16, dmod16, w, m, v)


def _pick(n, cands):
    for c in cands:
        if n % c == 0:
            return c
    return n


def _matmul_tn(a, b, name, square_a=False):
    L, K = a.shape
    N = b.shape[1]
    bk = _pick(K, (1024, 512, 256, 128))
    bn = _pick(N, (1024, 768, 512, 256, 128))
    tl = _pick(L, (1024, 512, 256, 128))
    n_l = L // tl

    def body(a_ref, b_ref, o_ref, acc_ref):
        l = pl.program_id(2)

        @pl.when(l == 0)
        def _():
            acc_ref[...] = jnp.zeros_like(acc_ref)
        av = a_ref[...]
        if square_a:
            av = av.astype(F32)
            av = av * av
        acc_ref[...] += _dot_tn(av.astype(BF16), b_ref[...].astype(BF16))

        @pl.when(l == n_l - 1)
        def _():
            o_ref[...] = acc_ref[...].astype(BF16)

    return pl.pallas_call(
        body, name=name, grid=(K // bk, N // bn, n_l),
        in_specs=[pl.BlockSpec((tl, bk), lambda k, n, l: (l, k)), pl.BlockSpec((tl, bn), lambda k, n, l: (l, n))],
        out_specs=pl.BlockSpec((bk, bn), lambda k, n, l: (k, n)), out_shape=_S((K, N), BF16),
        scratch_shapes=[pltpu.VMEM((bk, bn), F32)],
        compiler_params=_params(("arbitrary", "arbitrary", "arbitrary")))(a, b)


def _merge_blocks(a):
    n, R, C = a.shape
    cb = _pick(C, (256, 128))

    def body(i_ref, o_ref):
        for j in range(n):
            o_ref[R * j:R * (j + 1), :] = i_ref[j]

    return pl.pallas_call(body, name="merge_blocks", out_shape=_S((n * R, C), a.dtype), grid=(C // cb,),
                          in_specs=[pl.BlockSpec((n, R, cb), lambda c: (0, 0, c))],
                          out_specs=pl.BlockSpec((n * R, cb), lambda c: (0, c)),
                          compiler_params=_params(("arbitrary",)))(a)


def _split_blocks(a, n):
    NR, C = a.shape
    R = NR // n
    cb = _pick(C, (256, 128))

    def body(i_ref, o_ref):
        for j in range(n):
            o_ref[j] = i_ref[R * j:R * (j + 1), :]

    return pl.pallas_call(body, name="split_blocks", out_shape=_S((n, R, C), a.dtype), grid=(C // cb,),
                          in_specs=[pl.BlockSpec((NR, cb), lambda c: (0, c))],
                          out_specs=pl.BlockSpec((n, R, cb), lambda c: (0, 0, c)),
                          compiler_params=_params(("arbitrary",)))(a)


def _gw_in(pieces, u1, dims):
    L, D = u1.shape
    H = dims["H"]
    r_z, r_xbc, r_dt, r_q, r_kv = _proj_rows(dims)
    PROJ = r_kv[1]
    tl = _pick(L, (512, 256, 128))
    n_l = L // tl

    def body(dz_ref, dxbc_ref, ddt_ref, dq_ref, dkv_ref, u_ref, o_ref):
        @pl.when(pl.program_id(0) == 0)
        def _():
            o_ref[...] = jnp.zeros_like(o_ref)
        u = u_ref[...]
        for ref, (r0, r1) in ((dz_ref, r_z), (dxbc_ref, r_xbc), (dq_ref, r_q), (dkv_ref, r_kv)):
            o_ref[r0:r1, :] += _dot_tn(ref[...].astype(BF16), u)
        o_ref[r_dt[0]:r_dt[0] + H, :] += _dot_tn(ddt_ref[...].astype(BF16), u)[0:H, :]

    return pl.pallas_call(
        body, name="gw_in", grid=(n_l,),
        in_specs=[_tile(p.shape[1], tl) for p in pieces] + [_tile(D, tl)],
        out_specs=_acc((PROJ, D)), out_shape=_S((PROJ, D), F32),
        compiler_params=_params(("arbitrary",)))(*pieces, u1)


def _proj_rows(dims):
    W, CD, H, AW, KVW2 = dims["W"], dims["CD"], dims["H"], dims["AW"], dims["KVW2"]
    o_dt = W + CD
    o_q = o_dt + H
    return (0, W), (W, o_dt), (o_dt, o_dt + LANE), (o_q, o_q + AW), (o_q + AW, o_q + AW + KVW2)


def _ln_in_proj(x, g, b, sc, sh, w_t, dims):
    L, D = x.shape
    W, CD, AW, KVW2 = dims["W"], dims["CD"], dims["AW"], dims["KVW2"]
    PROJ = w_t.shape[0]
    tm = _pick(L, (MLP_TM, 128))
    r_z, r_xbc, r_dt, r_q, r_kv = _proj_rows(dims)

    def body(x_ref, g_ref, b_ref, sc_ref, sh_ref, w_ref, xhat_ref, rstd_ref, u1_ref, z_ref, xbc_ref, q_ref, kv_ref, dt_ref):
        xhat, rstd = _ln_fwd(x_ref[...])
        xhat_ref[...] = xhat
        rstd_ref[...] = rstd
        h0 = xhat * g_ref[...] + b_ref[...]
        u1 = (h0 * (1.0 + sc_ref[...]) + sh_ref[...]).astype(BF16)
        u1_ref[...] = u1
        z_ref[...] = _dot_nt(u1, w_ref[r_z[0]:r_z[1], :])
        xbc_ref[...] = _dot_nt(u1, w_ref[r_xbc[0]:r_xbc[1], :])
        q_ref[...] = _dot_nt(u1, w_ref[r_q[0]:r_q[1], :]).astype(BF16)
        kv_ref[...] = _dot_nt(u1, w_ref[r_kv[0]:r_kv[1], :]).astype(BF16)
        dt_ref[...] = _dot_nt(u1, w_ref[r_dt[0]:r_dt[1], :])

    v = _acc((1, D))
    return pl.pallas_call(
        body, name="ln_in_proj", grid=(L // tm,),
        in_specs=[_tile(D, tm), v, v, v, v, _res((PROJ, D))],
        out_specs=(_tile(D, tm), _tile(1, tm), _tile(D, tm), _tile(W, tm), _tile(CD, tm), _tile(AW, tm),
                   _tile(KVW2, tm), _tile(LANE, tm)),
        out_shape=(_S((L, D), F32), _S((L, 1), F32), _S((L, D), BF16), _S((L, W), F32), _S((L, CD), F32),
                   _S((L, AW), BF16), _S((L, KVW2), BF16), _S((L, LANE), F32)),
        compiler_params=_params(("arbitrary",)))(x, g, b, sc, sh, w_t)


def _conv_act(cur_ref, prev_ref, cw_ref, cb_ref, ext_ref, first):
    T = cur_ref.shape[0]
    ext_ref[0:HALO, :] = jnp.where(first, 0.0, prev_ref[...])
    ext_ref[HALO:HALO + T, :] = cur_ref[...]
    pre = cb_ref[...] + cw_ref[0:1, :] * ext_ref[HALO - 3:HALO - 3 + T, :]
    for k in range(1, CONV_K):
        pre = pre + cw_ref[k:k + 1, :] * ext_ref[HALO - 3 + k:HALO - 3 + k + T, :]
    return pre * _sigmoid(pre), pre


def _tri(T, upper=False):
    r = lax.broadcasted_iota(jnp.int32, (T, T), 0)
    c = lax.broadcasted_iota(jnp.int32, (T, T), 1)
    return (r <= c) if upper else (r >= c)


def _expand_heads(dst_ref, v, n_heads):
    for h in range(n_heads):
        dst_ref[:, h * HEAD_DIM:(h + 1) * HEAD_DIM] = jnp.broadcast_to(v[:, h:h + 1], (v.shape[0], HEAD_DIM))


def _head_reduce(v):
    wdt = v.shape[1]
    ch = lax.broadcasted_iota(jnp.int32, (wdt, LANE), 0)
    lo = lax.broadcasted_iota(jnp.int32, (wdt, LANE), 1) * HEAD_DIM
    onehot = ((ch >= lo) & (ch < lo + HEAD_DIM)).astype(BF16)
    hi = v.astype(BF16)
    rest = (v - hi.astype(F32)).astype(BF16)
    return _dot(hi, onehot) + _dot(rest, onehot)


def _conv_ssd(xbc, dt_raw, z, cw, cb, dtb, alog, dsk, nw, dims):
    L, CD = xbc.shape
    W, H, G, N = dims["W"], dims["H"], SSD_GROUPS, SSD_STATE
    T = CHUNK
    R = H // G
    GW = W // G
    nc = L // T
    HP = H * HEAD_DIM

    def body(xbc_ref, prev_ref, dt_ref, z_ref, cw_ref, cb_ref, dtb_ref, alog_ref, dsk_ref, nw_ref,
             y_ref, yn_ref, sp_ref, ext_ref, s_ref, ybuf_ref, dtx_ref, acx_ref, xb_ref):
        i = pl.program_id(0)

        @pl.when(i == 0)
        def _():
            s_ref[...] = jnp.zeros_like(s_ref)

        act, _ = _conv_act(xbc_ref, prev_ref, cw_ref, cb_ref, ext_ref, i == 0)
        xs = act[:, :W]
        dt = _softplus(dt_ref[...] + dtb_ref[...])
        a = dt * (-jnp.exp(alog_ref[...]))
        low = _tri(T)
        acum = _dot_hi(low.astype(F32), a)
        acum_t = acum.T
        _expand_heads(dtx_ref, dt, H)
        _expand_heads(acx_ref, acum, H)
        acx = acx_ref[...]
        lastx = acx[T - 1:T, :]
        xd = xs * dtx_ref[...]
        xb_ref[...] = xd.astype(BF16)
        xdb = (xd * jnp.exp(lastx - acx)).astype(BF16)
        ex = jnp.exp(acx)
        elx = jnp.exp(lastx)
        for g in range(G):
            gs = slice(g * GW, (g + 1) * GW)
            bgb = act[:, W + g * N:W + (g + 1) * N].astype(BF16)
            cgb = act[:, W + G * N + g * N:W + G * N + (g + 1) * N].astype(BF16)
            stg = s_ref[:, gs]
            sp_ref[0, :, gs] = stg
            yoff = ex[:, gs] * _dot(cgb, stg.astype(BF16))
            s_ref[:, gs] = stg * elx[:, gs] + _dot_tn(bgb, xdb[:, gs])
            cb_g = _dot_nt(cgb, bgb)
            for r in range(R):
                h = g * R + r
                hs = slice(h * HEAD_DIM, (h + 1) * HEAD_DIM)
                lm = jnp.where(low, jnp.exp(acum[:, h:h + 1] - acum_t[h:h + 1, :]), 0.0)
                ybuf_ref[:, hs] = _dot((cb_g * lm).astype(BF16), xb_ref[:, hs]) + yoff[:, r * HEAD_DIM:(r + 1) * HEAD_DIM]
        y = ybuf_ref[...] + dsk_ref[...] * xs
        y_ref[...] = y
        zz = z_ref[...]
        hh = y * (zz * _sigmoid(zz))
        for g in range(G):
            gs = slice(g * GW, (g + 1) * GW)
            hg = hh[:, gs]
            yn_ref[:, gs] = (hg * lax.rsqrt(_mean(hg * hg) + RMS_EPS) * nw_ref[:, gs]).astype(BF16)

    return pl.pallas_call(
        body, name="conv_ssd", grid=(nc,),
        in_specs=[_tile(CD, T), pl.BlockSpec((HALO, CD), lambda i: (jnp.maximum(i * (T // HALO) - 1, 0), 0)),
                  _tile(LANE, T), _tile(W, T), _acc((CONV_K, CD)), _acc((1, CD)), _acc((1, LANE)), _acc((1, LANE)),
                  _acc((1, W)), _acc((1, W))],
        out_specs=(_tile(W, T), _tile(W, T), pl.BlockSpec((1, N, HP), lambda i: (i, 0, 0))),
        out_shape=(_S((L, W), F32), _S((L, W), BF16), _S((nc, N, HP), F32)),
        scratch_shapes=[pltpu.VMEM((T + HALO, CD), F32), pltpu.VMEM((N, HP), F32), pltpu.VMEM((T, W), F32),
                        pltpu.VMEM((T, W), F32), pltpu.VMEM((T, W), F32), pltpu.VMEM((T, W), BF16)],
        compiler_params=_params(("arbitrary",)))(xbc, xbc, dt_raw, z, cw, cb, dtb, alog, dsk, nw)


def _attn_mask(T, i):
    r = lax.broadcasted_iota(jnp.int32, (T, 2 * T), 0)
    c = lax.broadcasted_iota(jnp.int32, (T, 2 * T), 1)
    dist = r + T - c
    valid = (dist >= 0) & (dist < CHUNK) & ((c >= T) | (i > 0))
    return dist.astype(F32), valid


def _attn_probs(s_raw, dist, valid, slope, sink, axis):
    s = s_raw * (HEAD_DIM ** -0.5) - slope * dist
    s = jnp.where(valid, s, NEG)
    m = jnp.maximum(jnp.max(s, axis=axis, keepdims=True), sink)
    p = jnp.exp(s - m)
    e_sink = jnp.exp(sink - m)
    inv = 1.0 / (jnp.sum(p, axis=axis, keepdims=True) + e_sink)
    return p * inv, e_sink * inv


def _kv_heads(kvc_ref, kvp_ref, g, n_kv):
    ks = slice(g * HEAD_DIM, (g + 1) * HEAD_DIM)
    vs = slice((n_kv + g) * HEAD_DIM, (n_kv + g + 1) * HEAD_DIM)
    kk = jnp.concatenate([kvp_ref[:, ks], kvc_ref[:, ks]], axis=0)
    vv = jnp.concatenate([kvp_ref[:, vs], kvc_ref[:, vs]], axis=0)
    return kk, vv


def _swa_fwd(q, kv, sinks, dims):
    L, AW = q.shape
    KV, KVW2 = dims["KV"], dims["KVW2"]
    T = CHUNK
    nb = L // T
    slopes = _alibi_slopes(dims["AH"])

    def body(q_ref, kvc_ref, kvp_ref, sink_ref, o_ref, qg_ref, p_ref):
        i = pl.program_id(0)
        dist, valid = _attn_mask(T, i)
        for g in range(KV):
            kk, vv = _kv_heads(kvc_ref, kvp_ref, g, KV)
            for r in range(GQA):
                h = g * GQA + r
                qg_ref[r * T:(r + 1) * T, :] = q_ref[:, h * HEAD_DIM:(h + 1) * HEAD_DIM]
            s_all = _dot_nt(qg_ref[...], kk)
            for r in range(GQA):
                h = g * GQA + r
                p, _ = _attn_probs(s_all[r * T:(r + 1) * T, :], dist, valid, slopes[h], sink_ref[h], -1)
                p_ref[r * T:(r + 1) * T, :] = p.astype(BF16)
            o_all = _dot(p_ref[...], vv)
            for r in range(GQA):
                h = g * GQA + r
                o_ref[:, h * HEAD_DIM:(h + 1) * HEAD_DIM] = o_all[r * T:(r + 1) * T, :].astype(BF16)

    return pl.pallas_call(
        body, name="swa_fwd", grid=(nb,),
        in_specs=[_tile(AW, T), _tile(KVW2, T), pl.BlockSpec((T, KVW2), lambda i: (jnp.maximum(i - 1, 0), 0)),
                  pl.BlockSpec(memory_space=pltpu.SMEM)],
        out_specs=_tile(AW, T), out_shape=_S((L, AW), BF16),
        scratch_shapes=[pltpu.VMEM((GQA * T, HEAD_DIM), BF16), pltpu.VMEM((GQA * T, 2 * T), BF16)],
        compiler_params=_params(("arbitrary",)))(q, kv, kv, sinks)


def _out_proj_ln1(yn, o, w_out, xhat0, vecs, alpha):
    L, W = yn.shape
    D = xhat0.shape[1]
    MIX = w_out.shape[0]
    tm = _pick(L, (MLP_TM, 128))

    def body(yn_ref, o_ref, w_ref, xh_ref, v_ref, mix_ref, xhat1_ref, rstd1_ref, u2_ref):
        mix = _dot(yn_ref[...], w_ref[0:W, :]) + _dot(o_ref[...], w_ref[W:MIX, :])
        mix_ref[...] = mix
        h0 = xh_ref[...] * v_ref[0:1, :] + v_ref[1:2, :]
        xhat1, rstd1 = _ln_fwd(alpha * h0 + (1.0 + v_ref[2:3, :]) * mix)
        xhat1_ref[...] = xhat1
        rstd1_ref[...] = rstd1
        h1 = xhat1 * v_ref[3:4, :] + v_ref[4:5, :]
        u2_ref[...] = (h1 * (1.0 + v_ref[5:6, :]) + v_ref[6:7, :]).astype(BF16)

    return pl.pallas_call(
        body, name="out_proj_ln1", grid=(L // tm,),
        in_specs=[_tile(W, tm), _tile(MIX - W, tm), _res((MIX, D)), _tile(D, tm), _acc((8, D))],
        out_specs=(_tile(D, tm), _tile(D, tm), _tile(1, tm), _tile(D, tm)),
        out_shape=(_S((L, D), F32), _S((L, D), F32), _S((L, 1), F32), _S((L, D), BF16)),
        compiler_params=_params(("arbitrary",)))(yn, o, w_out, xhat0, vecs)


def _mlp_loss(u2, w1, w2, xhat1, tgt, vecs, b1, alpha):
    L, D = xhat1.shape
    FF = w1.shape[1]
    tm = _pick(L, (MLP_TM, 128))
    sub = min(tm, MLP_SUB)
    fc = _pick(FF, (MLP_FC, 256, 128))

    def body(u2_ref, w1_ref, w2_ref, xh_ref, t_ref, v_ref, b1_ref, rr_ref, dr2_ref, acc_ref, loss_ref):
        @pl.when(pl.program_id(0) == 0)
        def _():
            acc_ref[...] = jnp.zeros_like(acc_ref)
            loss_ref[...] = jnp.zeros_like(loss_ref)

        for s in range(tm // sub):
            rs = slice(s * sub, (s + 1) * sub)
            u2 = u2_ref[rs, :]
            f = jnp.zeros((sub, D), F32) + v_ref[5:6, :]
            for j in range(FF // fc):
                cs = slice(j * fc, (j + 1) * fc)
                rr = jnp.maximum(_dot(u2, w1_ref[:, cs]) + b1_ref[:, cs], 0.0)
                rr_ref[rs, cs] = rr.astype(BF16)
                f = f + _dot((rr * rr).astype(BF16), w2_ref[cs, :])
            xhat1 = xh_ref[rs, :]
            h1 = xhat1 * v_ref[0:1, :] + v_ref[1:2, :]
            xhat2, rstd2 = _ln_fwd(alpha * h1 + (1.0 + v_ref[2:3, :]) * f)
            e = xhat2 * v_ref[3:4, :] + v_ref[4:5, :] - t_ref[rs, :]
            loss_ref[...] += 0.5 * jnp.sum(_mean(e * e))
            dy = e * (1.0 / D)
            dr2 = _ln_bwd(dy * v_ref[3:4, :], xhat2, rstd2)
            dr2_ref[rs, :] = dr2
            acc_ref[0:1, :] += _colsum(dy * xhat2)
            acc_ref[1:2, :] += _colsum(dy)
            acc_ref[2:3, :] += _colsum(dr2 * f)

    return pl.pallas_call(
        body, name="mlp_loss", grid=(L // tm,),
        in_specs=[_tile(D, tm), _res((D, FF)), _res((FF, D)), _tile(D, tm), _tile(D, tm), _acc((8, D)), _acc((1, FF))],
        out_specs=(_tile(FF, tm), _tile(D, tm), _acc((8, D)), _acc((1, LANE))),
        out_shape=(_S((L, FF), BF16), _S((L, D), F32), _S((8, D), F32), _S((1, LANE), F32)),
        compiler_params=_params(("arbitrary",)))(u2, w1, w2, xhat1, tgt, vecs, b1)


def _mlp_bwd_a(dr2, rr, w2, g2):
    L, D = dr2.shape
    FF = w2.shape[0]
    tm = _pick(L, (MLP_TM, 128))
    fc = _pick(FF, (MLP_FC, 256, 128))

    def body(dr2_ref, rr_ref, w2_ref, g2_ref, df_ref, da_ref, gb2_ref, gb1_ref):
        @pl.when(pl.program_id(0) == 0)
        def _():
            gb2_ref[...] = jnp.zeros_like(gb2_ref)
            gb1_ref[...] = jnp.zeros_like(gb1_ref)

        df = (1.0 + g2_ref[...]) * dr2_ref[...]
        gb2_ref[...] += _colsum(df)
        dfb = df.astype(BF16)
        df_ref[...] = dfb
        for j in range(FF // fc):
            cs = slice(j * fc, (j + 1) * fc)
            da = _dot_nt(dfb, w2_ref[cs, :]) * (2.0 * rr_ref[:, cs].astype(F32))
            gb1_ref[:, cs] += _colsum(da)
            da_ref[:, cs] = da.astype(BF16)

    return pl.pallas_call(
        body, name="mlp_bwd_a", grid=(L // tm,),
        in_specs=[_tile(D, tm), _tile(FF, tm), _res((FF, D)), _acc((1, D))],
        out_specs=(_tile(D, tm), _tile(FF, tm), _acc((1, D)), _acc((1, FF))),
        out_shape=(_S((L, D), BF16), _S((L, FF), BF16), _S((1, D), F32), _S((1, FF), F32)),
        compiler_params=_params(("arbitrary",)))(dr2, rr, w2, g2)


def _mlp_bwd_b(da, w1, dr2, xhat1, rstd1, mix, w_out, vecs, alpha, W):
    L, FF = da.shape
    D = dr2.shape[1]
    MIX = w_out.shape[0]
    tm = _pick(L, (MLP_TM, 128))

    def body(da_ref, w1_ref, dr2_ref, xh_ref, rs_ref, mix_ref, wo_ref, v_ref, dmix_ref, dh0_ref, dyn_ref, do_ref, acc_ref):
        @pl.when(pl.program_id(0) == 0)
        def _():
            acc_ref[...] = jnp.zeros_like(acc_ref)

        du2 = _dot_nt(da_ref[...], w1_ref[...])
        xhat1 = xh_ref[...]
        h1 = xhat1 * v_ref[0:1, :] + v_ref[1:2, :]
        acc_ref[0:1, :] += _colsum(du2 * h1)
        acc_ref[1:2, :] += _colsum(du2)
        dh1 = alpha * dr2_ref[...] + du2 * (1.0 + v_ref[2:3, :])
        acc_ref[2:3, :] += _colsum(dh1 * xhat1)
        acc_ref[3:4, :] += _colsum(dh1)
        dr1 = _ln_bwd(dh1 * v_ref[0:1, :], xhat1, rs_ref[...])
        acc_ref[4:5, :] += _colsum(dr1 * mix_ref[...])
        dh0_ref[...] = alpha * dr1
        dmix = ((1.0 + v_ref[3:4, :]) * dr1).astype(BF16)
        dmix_ref[...] = dmix
        dyn_ref[...] = _dot_nt(dmix, wo_ref[0:W, :])
        do_ref[...] = _dot_nt(dmix, wo_ref[W:MIX, :]).astype(BF16)

    return pl.pallas_call(
        body, name="mlp_bwd_b", grid=(L // tm,),
        in_specs=[_tile(FF, tm), _res((D, FF)), _tile(D, tm), _tile(D, tm), _tile(1, tm), _tile(D, tm), _res((MIX, D)),
                  _acc((8, D))],
        out_specs=(_tile(D, tm), _tile(D, tm), _tile(W, tm), _tile(MIX - W, tm), _acc((8, D))),
        out_shape=(_S((L, D), BF16), _S((L, D), F32), _S((L, W), F32), _S((L, MIX - W), BF16), _S((8, D), F32)),
        compiler_params=_params(("arbitrary",)))(da, w1, dr2, xhat1, rstd1, mix, w_out, vecs)


def _swa_bwd(q, kv, do, sinks, dims):
    L, AW = q.shape
    KV, KVW2 = dims["KV"], dims["KVW2"]
    T = CHUNK
    nb = L // T
    slopes = _alibi_slopes(dims["AH"])
    scale = HEAD_DIM ** -0.5

    def body(q_ref, kvc_ref, kvp_ref, do_ref, sink_ref, dq_ref, dkv_ref, dsink_ref, carry_ref,
             qg_ref, dog_ref, pt_ref, dst_ref):
        i = pl.program_id(0)

        @pl.when(i == 0)
        def _():
            carry_ref[...] = jnp.zeros_like(carry_ref)
            dsink_ref[...] = jnp.zeros_like(dsink_ref)

        @pl.when(i < nb)
        def _():
            c = lax.broadcasted_iota(jnp.int32, (2 * T, T), 0)
            r_ = lax.broadcasted_iota(jnp.int32, (2 * T, T), 1)
            dist_i = r_ + T - c
            valid = (dist_i >= 0) & (dist_i < CHUNK) & ((c >= T) | (i > 0))
            dist = dist_i.astype(F32)
            lane = lax.broadcasted_iota(jnp.int32, (1, LANE), 1)
            dsink = jnp.zeros((1, LANE), F32)
            dks, dvs = [], []
            for g in range(KV):
                kk, vv = _kv_heads(kvc_ref, kvp_ref, g, KV)
                for r in range(GQA):
                    hs = slice((g * GQA + r) * HEAD_DIM, (g * GQA + r + 1) * HEAD_DIM)
                    qg_ref[r * T:(r + 1) * T, :] = q_ref[:, hs]
                    dog_ref[r * T:(r + 1) * T, :] = do_ref[:, hs]
                st_all = _dot_nt(kk, qg_ref[...])
                dpt_all = _dot_nt(vv, dog_ref[...])
                for r in range(GQA):
                    h = g * GQA + r
                    cs = slice(r * T, (r + 1) * T)
                    p, p_sink = _attn_probs(st_all[:, cs], dist, valid, slopes[h], sink_ref[h], 0)
                    dp = dpt_all[:, cs]
                    delta = jnp.sum(p * dp, axis=0, keepdims=True)
                    pt_ref[:, cs] = p.astype(BF16)
                    dst_ref[:, cs] = (p * (dp - delta)).astype(BF16)
                    dsink = dsink + jnp.where(lane == h, -jnp.sum(p_sink * delta), 0.0)
                dst = dst_ref[...]
                dks.append(_dot(dst, qg_ref[...]) * scale)
                dvs.append(_dot(pt_ref[...], dog_ref[...]))
                dq_all = _dot_tn(dst, kk) * scale
                for r in range(GQA):
                    hs = slice((g * GQA + r) * HEAD_DIM, (g * GQA + r + 1) * HEAD_DIM)
                    dq_ref[:, hs] = dq_all[r * T:(r + 1) * T, :].astype(BF16)
            dkv = jnp.concatenate(dks + dvs, axis=1)
            dsink_ref[...] += dsink
            dkv_ref[...] = carry_ref[...] + dkv[0:T, :]
            carry_ref[...] = dkv[T:2 * T, :]

        @pl.when(i == nb)
        def _():
            dkv_ref[...] = carry_ref[...]

    last = nb - 1
    return pl.pallas_call(
        body, name="swa_bwd", grid=(nb + 1,),
        in_specs=[pl.BlockSpec((T, AW), lambda i: (jnp.minimum(i, last), 0)),
                  pl.BlockSpec((T, KVW2), lambda i: (jnp.minimum(i, last), 0)),
                  pl.BlockSpec((T, KVW2), lambda i: (jnp.clip(i - 1, 0, last), 0)),
                  pl.BlockSpec((T, AW), lambda i: (jnp.minimum(i, last), 0)),
                  pl.BlockSpec(memory_space=pltpu.SMEM)],
        out_specs=(pl.BlockSpec((T, AW), lambda i: (jnp.minimum(i, last), 0)),
                   pl.BlockSpec((T, KVW2), lambda i: (jnp.maximum(i - 1, 0), 0)), _acc((1, LANE))),
        out_shape=(_S((L, AW), BF16), _S((L, KVW2), F32), _S((1, LANE), F32)),
        scratch_shapes=[pltpu.VMEM((T, KVW2), F32), pltpu.VMEM((GQA * T, HEAD_DIM), BF16),
                        pltpu.VMEM((GQA * T, HEAD_DIM), BF16), pltpu.VMEM((2 * T, GQA * T), BF16),
                        pltpu.VMEM((2 * T, GQA * T), BF16)],
        compiler_params=_params(("arbitrary",)))(q, kv, kv, do, sinks)


def _ssd_bwd(dyn, y, z, xbc, dt_raw, sprev, cw, cb, dtb, alog, dsk, nw, dims):
    L, CD = xbc.shape
    W, H, G, N = dims["W"], dims["H"], SSD_GROUPS, SSD_STATE
    T = CHUNK
    R = H // G
    GW = W // G
    nc = L // T
    HP = H * HEAD_DIM

    def body(dyn_ref, y_ref, z_ref, xbc_ref, prev_ref, dt_ref, sp_ref, cw_ref, cb_ref, dtb_ref, alog_ref, dsk_ref, nw_ref,
             dz_ref, dpre_ref, ddt_ref, acc_ref, hacc_ref, ext_ref, ds_ref, dtx_ref, acx_ref, xb_ref, dyb_ref, r12_ref,
             dx_ref, rows_ref):
        i = pl.program_id(0)

        @pl.when(i == 0)
        def _():
            ds_ref[...] = jnp.zeros_like(ds_ref)
            acc_ref[...] = jnp.zeros_like(acc_ref)
            hacc_ref[...] = jnp.zeros_like(hacc_ref)

        act, pre = _conv_act(xbc_ref, prev_ref, cw_ref, cb_ref, ext_ref, i == nc - 1)
        xs = act[:, :W]
        dt_in = dt_ref[...] + dtb_ref[...]
        dt = _softplus(dt_in)
        a_neg = -jnp.exp(alog_ref[...])
        a = dt * a_neg
        low = _tri(T)
        upf = _tri(T, upper=True).astype(F32)
        acum = _dot_hi(low.astype(F32), a)
        acum_t = acum.T

        y = y_ref[...]
        zz = z_ref[...]
        sg = _sigmoid(zz)
        sz = zz * sg
        hh = y * sz
        dyn_v = dyn_ref[...]
        parts = []
        for g in range(G):
            gs = slice(g * GW, (g + 1) * GW)
            hg = hh[:, gs]
            hhat = hg * lax.rsqrt(_mean(hg * hg) + RMS_EPS)
            rg = lax.rsqrt(_mean(hg * hg) + RMS_EPS)
            acc_ref[0:1, gs] += _colsum(dyn_v[:, gs] * hhat)
            dhhat = dyn_v[:, gs] * nw_ref[:, gs]
            parts.append(rg * (dhhat - hhat * _mean(dhhat * hhat)))
        dhh = jnp.concatenate(parts, axis=1)
        dy = dhh * sz
        dz_ref[...] = (dhh * y * (sg * (1.0 + zz * (1.0 - sg)))).astype(BF16)
        acc_ref[1:2, :] += _colsum(dy * xs)
        dyb_ref[...] = dy.astype(BF16)

        _expand_heads(dtx_ref, dt, H)
        _expand_heads(acx_ref, acum, H)
        dtx = dtx_ref[...]
        acx = acx_ref[...]
        lastx = acx[T - 1:T, :]
        ex = jnp.exp(acx)
        decx = jnp.exp(lastx - acx)
        elx = jnp.exp(lastx)
        xd = xs * dtx
        xb_ref[...] = xd.astype(BF16)
        xdecb = (xd * decx).astype(BF16)
        dgb = (ex * dy).astype(BF16)
        rows_ref[...] = jnp.zeros_like(rows_ref)

        lane = lax.broadcasted_iota(jnp.int32, (T, LANE), 1)
        sub = lax.broadcasted_iota(jnp.int32, (T, LANE), 0)
        subr = lax.broadcasted_iota(jnp.int32, (LANE, T), 0)
        da_col = jnp.zeros((T, LANE), F32)
        da_row = jnp.zeros((LANE, T), F32)
        dbs, dcs = [], []
        for g in range(G):
            gs = slice(g * GW, (g + 1) * GW)
            bgb = act[:, W + g * N:W + (g + 1) * N].astype(BF16)
            cgb = act[:, W + G * N + g * N:W + G * N + (g + 1) * N].astype(BF16)
            stg = sp_ref[0, :, gs]
            stb = stg.astype(BF16)
            dsn = ds_ref[:, gs]
            dsnb = dsn.astype(BF16)
            gm = _dot(cgb, stb)
            dc = _dot_nt(dgb[:, gs], stb)
            dsp = _dot_tn(cgb, dgb[:, gs])
            dxs_ = decx[:, gs] * _dot(bgb, dsnb)
            db = _dot_nt(xdecb[:, gs], dsnb)
            xdg = xd[:, gs]
            r12_ref[:, gs] = dy[:, gs] * ex[:, gs] * gm - xdg * dxs_
            rows_ref[0:1, gs] = _colsum(dsn * stg) * elx[:, gs]
            rows_ref[1:2, gs] = _colsum(xdg * dxs_)
            ds_ref[:, gs] = dsp + dsn * elx[:, gs]
            cb_g = _dot_nt(cgb, bgb)
            dcb = jnp.zeros((T, T), F32)
            for r in range(R):
                h = g * R + r
                hs = slice(h * HEAD_DIM, (h + 1) * HEAD_DIM)
                lm = jnp.where(low, jnp.exp(acum[:, h:h + 1] - acum_t[h:h + 1, :]), 0.0)
                mm = cb_g * lm
                dyb = dyb_ref[:, hs]
                dm = _dot_nt(dyb, xb_ref[:, hs])
                dx_ref[:, hs] = dxs_[:, r * HEAD_DIM:(r + 1) * HEAD_DIM] + _dot_tn(mm.astype(BF16), dyb)
                dcb = dcb + dm * lm
                qm = dm * mm
                da_col = jnp.where(lane == h, jnp.sum(qm, axis=1, keepdims=True), da_col)
                da_row = jnp.where(subr == h, jnp.sum(qm, axis=0, keepdims=True), da_row)
            dcbb = dcb.astype(BF16)
            dcs.append(dc + _dot(dcbb, bgb))
            dbs.append(db + _dot_tn(dcbb, cgb))
        dx = dx_ref[...]
        rows = _head_reduce(rows_ref[...])
        dlast = rows[0:1, :] + rows[1:2, :]
        da_col = da_col + _head_reduce(r12_ref[...]) + jnp.where(sub == T - 1, dlast, 0.0)
        dacum = da_col - da_row.T
        da = _dot_hi(upf, dacum)
        ddt = _head_reduce(dx * xs) + da * a_neg
        hacc_ref[1:2, :] += _colsum(da * dt) * a_neg
        ddt_raw = ddt * _sigmoid(dt_in)
        hacc_ref[0:1, :] += _colsum(ddt_raw)
        ddt_ref[...] = ddt_raw
        dact = jnp.concatenate([dsk_ref[...] * dy + dx * dtx] + dbs + dcs, axis=1)
        spre = _sigmoid(pre)
        dpre_ref[...] = dact * (spre * (1.0 + pre * (1.0 - spre)))

        @pl.when(i == nc - 1)
        def _():
            ch = lax.broadcasted_iota(jnp.int32, (W, LANE), 0)
            lo = lax.broadcasted_iota(jnp.int32, (W, LANE), 1) * HEAD_DIM
            hacc_ref[2:3, :] = _dot_hi(acc_ref[1:2, :], ((ch >= lo) & (ch < lo + HEAD_DIM)).astype(F32))

    rev = lambda i: (nc - 1 - i, 0)
    return pl.pallas_call(
        body, name="ssd_bwd", grid=(nc,),
        in_specs=[pl.BlockSpec((T, W), rev), pl.BlockSpec((T, W), rev), pl.BlockSpec((T, W), rev), pl.BlockSpec((T, CD), rev),
                  pl.BlockSpec((HALO, CD), lambda i: (jnp.maximum((nc - 1 - i) * (T // HALO) - 1, 0), 0)),
                  pl.BlockSpec((T, LANE), rev), pl.BlockSpec((1, N, HP), lambda i: (nc - 1 - i, 0, 0)),
                  _acc((CONV_K, CD)), _acc((1, CD)), _acc((1, LANE)), _acc((1, LANE)), _acc((1, W)), _acc((1, W))],
        out_specs=(pl.BlockSpec((T, W), rev), pl.BlockSpec((T, CD), rev), pl.BlockSpec((T, LANE), rev), _acc((8, W)),
                   _acc((8, LANE))),
        out_shape=(_S((L, W), BF16), _S((L, CD), F32), _S((L, LANE), F32), _S((8, W), F32), _S((8, LANE), F32)),
        scratch_shapes=[pltpu.VMEM((T + HALO, CD), F32), pltpu.VMEM((N, HP), F32), pltpu.VMEM((T, W), F32),
                        pltpu.VMEM((T, W), F32), pltpu.VMEM((T, W), BF16), pltpu.VMEM((T, W), BF16), pltpu.VMEM((T, W), F32),
                        pltpu.VMEM((T, W), F32), pltpu.VMEM((8, W), F32)],
        compiler_params=_params(("arbitrary",)))(dyn, y, z, xbc, xbc, dt_raw, sprev, cw, cb, dtb, alog, dsk, nw)


def _conv_bwd(dpre, xbc, cw):
    L, CD = xbc.shape
    tm = _pick(L, (CONV_TM, 128))
    cb = CD if CONV_CB >= CD else _pick(CD, (CONV_CB, 128))
    nt = L // tm
    hb = tm // HALO

    def body(dp_ref, dn_ref, u_ref, cw_ref, du_ref, acc_ref, extd_ref):
        i = pl.program_id(1)

        @pl.when(i == 0)
        def _():
            acc_ref[...] = jnp.zeros_like(acc_ref)

        extd_ref[0:tm, :] = dp_ref[...]
        extd_ref[tm:tm + HALO, :] = jnp.where(i == nt - 1, 0.0, dn_ref[...])
        for c in range(tm // ROW_CHUNK):
            r0 = c * ROW_CHUNK
            rows = slice(r0, r0 + ROW_CHUNK)
            dp = dp_ref[rows, :]
            u = u_ref[rows, :]
            du = cw_ref[CONV_K - 1:CONV_K, :] * dp
            acc_ref[CONV_K - 1:CONV_K, :] += _colsum(dp * u)
            for k in range(CONV_K - 1):
                s = CONV_K - 1 - k
                dsh = extd_ref[r0 + s:r0 + s + ROW_CHUNK, :]
                du = du + cw_ref[k:k + 1, :] * dsh
                acc_ref[k:k + 1, :] += _colsum(u * dsh)
            acc_ref[CONV_K:CONV_K + 1, :] += _colsum(dp)
            du_ref[rows, :] = du.astype(BF16)

    tile = pl.BlockSpec((tm, cb), lambda j, i: (i, j))
    return pl.pallas_call(
        body, name="conv_bwd", grid=(CD // cb, nt),
        in_specs=[tile, pl.BlockSpec((HALO, cb), lambda j, i: (jnp.minimum((i + 1) * hb, nt * hb - 1), j)),
                  tile, pl.BlockSpec((CONV_K, cb), lambda j, i: (0, j))],
        out_specs=(tile, pl.BlockSpec((8, cb), lambda j, i: (0, j))),
        out_shape=(_S((L, CD), BF16), _S((8, CD), F32)),
        scratch_shapes=[pltpu.VMEM((tm + HALO, cb), F32)],
        compiler_params=_params(("arbitrary", "arbitrary")))(dpre, dpre, xbc, cw)


def _in_proj_bwd(dz, dxbc, dq, dkv, ddt, w_t, xhat0, rstd0, dh0p, vecs, dims):
    L, D = xhat0.shape
    W, CD, AW, KVW2 = dims["W"], dims["CD"], dims["AW"], dims["KVW2"]
    PROJ = w_t.shape[0]
    tm = _pick(L, (MLP_TM, 128))
    r_z, r_xbc, r_dt, r_q, r_kv = _proj_rows(dims)

    def body(dz_ref, dxbc_ref, dq_ref, dkv_ref, ddt_ref, w_ref, xh_ref, rs_ref, dh0_ref, v_ref, gx_ref, acc_ref):
        @pl.when(pl.program_id(0) == 0)
        def _():
            acc_ref[...] = jnp.zeros_like(acc_ref)

        du1 = _dot(dz_ref[...], w_ref[r_z[0]:r_z[1], :])
        du1 = du1 + _dot(dxbc_ref[...], w_ref[r_xbc[0]:r_xbc[1], :])
        du1 = du1 + _dot(dq_ref[...], w_ref[r_q[0]:r_q[1], :])
        du1 = du1 + _dot(dkv_ref[...].astype(BF16), w_ref[r_kv[0]:r_kv[1], :])
        du1 = du1 + _dot(ddt_ref[...].astype(BF16), w_ref[r_dt[0]:r_dt[1], :])
        xhat0 = xh_ref[...]
        h0 = xhat0 * v_ref[0:1, :] + v_ref[1:2, :]
        acc_ref[0:1, :] += _colsum(du1 * h0)
        acc_ref[1:2, :] += _colsum(du1)
        dh0 = dh0_ref[...] + du1 * (1.0 + v_ref[2:3, :])
        acc_ref[2:3, :] += _colsum(dh0 * xhat0)
        acc_ref[3:4, :] += _colsum(dh0)
        gx_ref[...] = _ln_bwd(dh0 * v_ref[0:1, :], xhat0, rs_ref[...])

    return pl.pallas_call(
        body, name="in_proj_bwd", grid=(L // tm,),
        in_specs=[_tile(W, tm), _tile(CD, tm), _tile(AW, tm), _tile(KVW2, tm), _tile(LANE, tm), _res((PROJ, D)),
                  _tile(D, tm), _tile(1, tm), _tile(D, tm), _acc((8, D))],
        out_specs=(_tile(D, tm), _acc((8, D))),
        out_shape=(_S((L, D), F32), _S((8, D), F32)),
        compiler_params=_params(("arbitrary",)))(dz, dxbc, dq, dkv, ddt, w_t, xhat0, rstd0, dh0p, vecs)


_WEIGHTS = ['ln_in_g', 'ln_in_b', 'ada_w', 'ada_b', 'w_in', 'conv_w', 'conv_b', 'dt_bias', 'a_log', 'd_skip', 'ssd_norm_w',
            'attn_sinks', 'w_out', 'ln1_g', 'ln1_b', 'w_ff1', 'b_ff1', 'w_ff2', 'b_ff2', 'ln2_g', 'ln2_b']
_BIG = ('w_in', 'w_out', 'w_ff1', 'w_ff2')
_SMALL = ('ada_b', 'ln_in_g', 'ln_in_b', 'conv_b', 'dt_bias', 'a_log', 'd_skip', 'ssd_norm_w', 'attn_sinks', 'ln1_g', 'ln1_b',
          'b_ff1', 'b_ff2', 'ln2_g', 'ln2_b')


def _pad_lanes(v, n=None):
    v = v.reshape(1, -1)
    n = n or -(-v.shape[1] // LANE) * LANE
    return jnp.pad(v, ((0, 0), (0, n - v.shape[1])))


def _vec8(rows, D):
    rows = [r.reshape(1, D) for r in rows]
    return jnp.concatenate(rows + [jnp.zeros((8 - len(rows), D), F32)], axis=0)


def _pack(segs):
    flat, offs, sizes, o = [], [], [], 0
    for s in segs:
        p = _pad_lanes(s)
        flat.append(p)
        offs.append(o)
        sizes.append(s.size)
        o += p.shape[1]
    total = -(-o // (8 * LANE)) * (8 * LANE)
    if total > o:
        flat.append(jnp.zeros((1, total - o), F32))
    return jnp.concatenate(flat, axis=1).reshape(8, total // 8), offs, sizes


def kernel(x, c, ln_in_g, ln_in_b, ada_w, ada_b, w_in, conv_w, conv_b, dt_bias, a_log, d_skip, ssd_norm_w, attn_sinks, w_out, ln1_g, ln1_b, w_ff1, b_ff1, w_ff2, b_ff2, ln2_g, ln2_b, loss_target, m_ln_in_g, m_ln_in_b, m_ada_w, m_ada_b, m_w_in, m_conv_w, m_conv_b, m_dt_bias, m_a_log, m_d_skip, m_ssd_norm_w, m_attn_sinks, m_w_out, m_ln1_g, m_ln1_b, m_w_ff1, m_b_ff1, m_w_ff2, m_b_ff2, m_ln2_g, m_ln2_b, v_ln_in_g, v_ln_in_b, v_ada_w, v_ada_b, v_w_in, v_conv_w, v_conv_b, v_dt_bias, v_a_log, v_d_skip, v_ssd_norm_w, v_attn_sinks, v_w_out, v_ln1_g, v_ln1_b, v_w_ff1, v_b_ff1, v_w_ff2, v_b_ff2, v_ln2_g, v_ln2_b):
    wts = dict(ln_in_g=ln_in_g, ln_in_b=ln_in_b, ada_w=ada_w, ada_b=ada_b, w_in=w_in, conv_w=conv_w, conv_b=conv_b,
               dt_bias=dt_bias, a_log=a_log, d_skip=d_skip, ssd_norm_w=ssd_norm_w, attn_sinks=attn_sinks, w_out=w_out,
               ln1_g=ln1_g, ln1_b=ln1_b, w_ff1=w_ff1, b_ff1=b_ff1, w_ff2=w_ff2, b_ff2=b_ff2, ln2_g=ln2_g, ln2_b=ln2_b)
    ms = dict(ln_in_g=m_ln_in_g, ln_in_b=m_ln_in_b, ada_w=m_ada_w, ada_b=m_ada_b, w_in=m_w_in, conv_w=m_conv_w,
              conv_b=m_conv_b, dt_bias=m_dt_bias, a_log=m_a_log, d_skip=m_d_skip, ssd_norm_w=m_ssd_norm_w,
              attn_sinks=m_attn_sinks, w_out=m_w_out, ln1_g=m_ln1_g, ln1_b=m_ln1_b, w_ff1=m_w_ff1, b_ff1=m_b_ff1,
              w_ff2=m_w_ff2, b_ff2=m_b_ff2, ln2_g=m_ln2_g, ln2_b=m_ln2_b)
    vs = dict(ln_in_g=v_ln_in_g, ln_in_b=v_ln_in_b, ada_w=v_ada_w, ada_b=v_ada_b, w_in=v_w_in, conv_w=v_conv_w,
              conv_b=v_conv_b, dt_bias=v_dt_bias, a_log=v_a_log, d_skip=v_d_skip, ssd_norm_w=v_ssd_norm_w,
              attn_sinks=v_attn_sinks, w_out=v_w_out, ln1_g=v_ln1_g, ln1_b=v_ln1_b, w_ff1=v_w_ff1, b_ff1=v_b_ff1,
              w_ff2=v_w_ff2, b_ff2=v_b_ff2, ln2_g=v_ln2_g, ln2_b=v_ln2_b)

    L, D = x.shape[1], x.shape[2]
    depth = w_in.shape[0]
    assert depth == 1 and x.shape[0] == 1 and L % CHUNK == 0
    W = D
    H = W // HEAD_DIM
    CD = W + 2 * SSD_GROUPS * SSD_STATE
    AW = D
    AH = AW // HEAD_DIM
    KV = AH // GQA
    KVW2 = 2 * KV * HEAD_DIM
    PROJ = W + CD + H + AW + KVW2
    FF = w_ff1.shape[2] * N_DEV
    MIX = w_out.shape[1] * N_DEV
    assert w_in.shape[2] * N_DEV == PROJ and MIX == W + AW and H <= LANE and AH <= LANE
    dims = dict(W=W, H=H, CD=CD, AW=AW, AH=AH, KV=KV, KVW2=KVW2)
    alpha = (2.0 * depth) ** 0.25
    C6 = ada_w.shape[2]
    CW = conv_w.shape[2]

    ax, ay, ac = _my_pos()
    me = 4 * ax + 2 * ay + ac
    x2 = x.reshape(L, D)
    tgt = loss_target.reshape(L, D)
    r1 = lambda a: a.reshape(1, -1)

    ada_b_cols = lax.dynamic_slice(ada_b, (0, me * C6), (1, C6))
    cs_all, mod = _mod_fwd(c, ada_w[0], ada_b_cols)
    sh1, sc1, g1, sh2, sc2, g2 = [r1(t) for t in jnp.split(mod.reshape(-1), 6)]

    wg_in, cwg = _ag_weights([w_in[0].T.astype(BF16), conv_w[0]], cs_all)
    shards2 = [w_out[0].astype(BF16), w_ff1[0].astype(BF16), w_ff2[0].astype(BF16)]
    lands2 = [lax.dynamic_update_slice(lax.empty((N_DEV,) + s.shape, s.dtype), s[None], (me, 0, 0)) for s in shards2]
    ag_ss, ag_rs, ag_arr, ag_token = _split_start(shards2 + lands2, _plan_gather(3), cwg, "ag_ici_start")
    sh1 = sh1 + ag_token[0:1, 0:1]
    w_pad = _merge_blocks(wg_in)
    cw_full = cwg.transpose(1, 0, 2).reshape(CONV_K, CD)

    dtb = _pad_lanes(dt_bias, LANE)
    alog = _pad_lanes(a_log, LANE)
    dsk = jnp.repeat(d_skip.reshape(-1), HEAD_DIM).reshape(1, W)
    sinks = attn_sinks.reshape(-1)
    g_in, b_in = r1(ln_in_g), r1(ln_in_b)

    xhat0, rstd0, u1, z, xbc, q, kv, dt_raw = _ln_in_proj(x2, g_in, b_in, sc1, sh1, w_pad, dims)
    y, yn, sprev = _conv_ssd(xbc, dt_raw, z, cw_full, conv_b, dtb, alog, dsk, ssd_norm_w, dims)
    ag_arr = _split_wait(ag_ss, ag_rs, ag_arr, _plan_gather(3), yn, "ag_ici_wait")
    fw_ss, fw_rs, ag_land, fw_token = _split_start(ag_arr[3:], _plan_forward(3), yn, "ag_fwd_start")
    o = _swa_fwd(q, kv, sinks + fw_token[0, 0], dims)
    wg_out, wg_ff1, wg_ff2 = _split_wait(fw_ss, fw_rs, ag_land, _plan_forward(3), o, "ag_fwd_wait")
    w_out_full = wg_out.reshape(MIX, D)
    w1_full = wg_ff1.transpose(1, 0, 2).reshape(D, FF)
    w2_full = wg_ff2.reshape(FF, D)
    mix, xhat1, rstd1, u2 = _out_proj_ln1(yn, o, w_out_full, xhat0, _vec8([g_in, b_in, g1, ln1_g, ln1_b, sc2, sh2], D), alpha)
    rr, dr2, acc_f, loss_loc = _mlp_loss(u2, w1_full, w2_full, xhat1, tgt,
                                         _vec8([ln1_g, ln1_b, g2, ln2_g, ln2_b, b_ff2], D), b_ff1, alpha)

    df, da, gb2, gb1 = _mlp_bwd_a(dr2, rr, w2_full, g2)
    gw_ff2 = _matmul_tn(rr, df, "gw_ff2", square_a=True)
    gw_ff1t = _matmul_tn(da, u2, "gw_ff1")
    dmix, dh0p, dyn, do, acc_b = _mlp_bwd_b(da, w1_full, dr2, xhat1, rstd1, mix, w_out_full,
                                            _vec8([ln1_g, ln1_b, sc2, g1], D), alpha, W)
    gw_out = jnp.concatenate([_matmul_tn(yn, dmix, "gw_out_ssd"), _matmul_tn(o, dmix, "gw_out_attn")], axis=0)

    core = jnp.reshape(ac, (1,)).astype(jnp.int32)
    blocked1 = [gw_out.reshape(N_DEV, MIX // N_DEV, D), gw_ff1t.reshape(N_DEV, FF // N_DEV, D),
                gw_ff2.reshape(N_DEV, FF // N_DEV, D)]
    lands1 = [lax.empty(b.shape, b.dtype) for b in blocked1]
    rs_ss, rs_rs, rs_arr, rs_token = _split_start(blocked1 + lands1, _plan_scatter_all(3), do, "rs_all_start")
    dq, dkv, dsink = _swa_bwd(q, kv, do, sinks + rs_token[0, 0], dims)
    dz, dpre, ddt, acc_s, hacc = _ssd_bwd(dyn, y, z, xbc, dt_raw, sprev, cw_full, conv_b, dtb + rs_token[0:1, 0:1], alog, dsk,
                                          ssd_norm_w, dims)
    dxbc, acc_c = _conv_bwd(dpre, xbc, cw_full)
    gw_in = _gw_in((dz, dxbc, ddt, dq, dkv), u1, dims)

    blocked2 = [_split_blocks(gw_in, N_DEV)]
    pairs2 = [_pair_sum(b, r, core) for b, r in zip(blocked2, _rs_d2d(blocked2, "rs_d2d_2"))]
    lands2 = [lax.empty(p.shape, p.dtype) for p in pairs2]
    r2_ss, r2_rs, r2_arr, r2_token = _split_start(pairs2 + lands2, _plan_scatter(1), gw_in, "rs_ici_start_2")
    grad_x, acc_i = _in_proj_bwd(dz, dxbc, dq, dkv, ddt, w_pad, xhat0, rstd0, dh0p,
                                 _vec8([g_in, b_in, sc1], D) + r2_token[0:1, 0:1], dims)

    dmod = jnp.concatenate([acc_i[1], acc_i[0], acc_b[4], acc_b[1], acc_b[0], acc_f[2]])
    small_g = dict(ada_b=dmod, ln_in_g=acc_i[2], ln_in_b=acc_i[3], conv_b=acc_c[CONV_K], dt_bias=hacc[0, :H], a_log=hacc[1, :H],
                   d_skip=hacc[2, :H], ssd_norm_w=acc_s[0], attn_sinks=dsink[0, :AH], ln1_g=acc_b[2], ln1_b=acc_b[3],
                   b_ff1=gb1[0], b_ff2=gb2[0], ln2_g=acc_f[0], ln2_b=acc_f[1])
    segs = [small_g[n] for n in _SMALL] + [acc_c[:CONV_K].reshape(-1), loss_loc[0, :1]]
    pack, offs, sizes = _pack(segs)
    gathered, summed = _small_gather_sum(pack)
    gathered = gathered.reshape(N_DEV, -1)
    summed = summed.reshape(-1)
    seg = lambda k: summed[offs[k]:offs[k] + sizes[k]]
    grads = {n: seg(k).reshape(wts[n].shape) for k, n in enumerate(_SMALL)}
    gcw_full = seg(len(_SMALL)).reshape(CONV_K, CD)
    grads['conv_w'] = lax.dynamic_slice(gcw_full, (0, me * CW), (CONV_K, CW)).reshape(conv_w.shape)
    loss = seg(len(_SMALL) + 1)[0]

    names = list(_SMALL) + ['conv_w']
    pw, poffs, psizes = _pack([wts[n] for n in names])
    pg, _, _ = _pack([grads[n] for n in names])
    pm, _, _ = _pack([ms[n] for n in names])
    pv, _, _ = _pack([vs[n] for n in names])
    pd, pm2, pv2 = [t.reshape(-1) for t in _adamw(pw, pg, pm, pv)]
    deltas, new_m, new_v = {}, {}, {}
    for k, n in enumerate(names):
        sl = slice(poffs[k], poffs[k] + psizes[k])
        deltas[n], new_m[n], new_v[n] = (t[sl].reshape(wts[n].shape) for t in (pd, pm2, pv2))

    dmod_cols = lax.dynamic_slice(gathered, (0, offs[0] + me * C6), (N_DEV, C6))
    pad16 = lambda t: jnp.concatenate([t, jnp.zeros((16 - N_DEV,) + t.shape[1:], t.dtype)], axis=0)
    g_, d_, m_, v_ = _ada_grad_adamw(pad16(cs_all), pad16(dmod_cols), ada_w[0], m_ada_w[0], v_ada_w[0])
    grads['ada_w'], deltas['ada_w'], new_m['ada_w'], new_v['ada_w'] = (t[None] for t in (g_, d_, m_, v_))

    rs_arr = _split_wait(rs_ss, rs_rs, rs_arr, _plan_scatter_all(3), g_, "rs_all_wait")
    mychip = 2 * ax + ay
    chips = jnp.stack([(mychip + k) % N_CHIP for k in range(N_CHIP)]).astype(jnp.int32)
    devs = jnp.stack([(me + k) % N_DEV for k in range(N_DEV)]).astype(jnp.int32)
    for n, own, land in zip(('w_out', 'w_ff1', 'w_ff2'), rs_arr[:3], rs_arr[3:]):
        g_, d_, m_, v_ = _sum_adamw_split(own, land, devs, wts[n][0], ms[n][0], vs[n][0], transposed=(n == 'w_ff1'))
        grads[n], deltas[n], new_m[n], new_v[n] = (t[None] for t in (g_, d_, m_, v_))
    r2_arr = _split_wait(r2_ss, r2_rs, r2_arr, _plan_scatter(1), g_, "rs_ici_wait_2")
    g_, d_, m_, v_ = _sum_adamw_split(r2_arr[0], r2_arr[1], chips, wts['w_in'][0].T, ms['w_in'][0].T, vs['w_in'][0].T)
    grads['w_in'], deltas['w_in'], new_m['w_in'], new_v['w_in'] = (t.T[None] for t in (g_, d_, m_, v_))

    return (loss, grad_x.reshape(x.shape), *[grads[n] for n in _WEIGHTS], *[deltas[n] for n in _WEIGHTS],
            *[new_m[n] for n in _WEIGHTS], *[new_v[n] for n in _WEIGHTS])
```

```python
import functools
import math

import numpy as np
import jax
import jax.numpy as jnp
from jax import lax
from jax.experimental import pallas as pl
from jax.experimental.pallas import tpu as pltpu

F32 = jnp.float32
BF16 = jnp.bfloat16
MESH = pl.DeviceIdType.MESH

N_DEV = 8
N_CHIP = 4
HEAD_DIM = 64
SSD_GROUPS = 2
SSD_STATE = 128
CHUNK = 128
CONV_K = 4
GQA = 8
LANE = 128
HALO = 8
LN_EPS = 1e-5
RMS_EPS = 1e-5
NEG = -1e30
ADAM_LR, ADAM_B1, ADAM_B2, ADAM_EPS, ADAM_WD, ADAM_STEP = 0.001, 0.9, 0.999, 1e-08, 0.01, 10
V7X_VMEM_BYTES = 64 * 1024 * 1024
VMEM_LIMIT = V7X_VMEM_BYTES - 8 * 1024 * 1024
HI = lax.Precision.HIGHEST
MLP_TM = 512
MLP_SUB = 512
MLP_FC = 512
CONV_TM = 512
CONV_CB = 2048
ROW_CHUNK = 32


def _alibi_slopes(n):
    def pow2(m):
        start = 2.0 ** (-8.0 / m)
        return [start ** (i + 1) for i in range(m)]
    if math.log2(n).is_integer():
        s = pow2(n)
    else:
        c = 2 ** math.floor(math.log2(n))
        s = pow2(c) + pow2(2 * c)[0::2][: n - c]
    return [float(v) for v in np.array(s, dtype=np.float32)]


def _dot(a, b):
    return jnp.dot(a, b, preferred_element_type=F32)


def _dot_nt(a, b):
    return lax.dot_general(a, b, (((1,), (1,)), ((), ())), preferred_element_type=F32)


def _dot_tn(a, b):
    return lax.dot_general(a, b, (((0,), (0,)), ((), ())), preferred_element_type=F32)


def _dot_hi(a, b):
    return jnp.dot(a, b, precision=HI, preferred_element_type=F32)


def _sigmoid(x):
    return 0.5 * jnp.tanh(0.5 * x) + 0.5


def _softplus(x):
    return jnp.maximum(x, 0.0) + jnp.log(1.0 + jnp.exp(-jnp.abs(x)))


def _mean(x):
    return jnp.mean(x, axis=-1, keepdims=True)


def _ln_fwd(x):
    xc = x - _mean(x)
    rstd = lax.rsqrt(_mean(xc * xc) + LN_EPS)
    return xc * rstd, rstd


def _ln_bwd(dxhat, xhat, rstd):
    return rstd * (dxhat - _mean(dxhat) - xhat * _mean(dxhat * xhat))


def _colsum(x):
    return jnp.sum(x, axis=0, keepdims=True)


def _params(sem):
    return pltpu.CompilerParams(dimension_semantics=sem, vmem_limit_bytes=VMEM_LIMIT)


def _tile(i_map_cols, tm):
    return pl.BlockSpec((tm, i_map_cols), lambda i: (i, 0))


def _res(shape):
    return pl.BlockSpec(shape, lambda *_: (0,) * len(shape), pipeline_mode=pl.Buffered(1))


def _acc(shape):
    return pl.BlockSpec(shape, lambda *_: (0,) * len(shape))


def _S(shape, dtype):
    return jax.ShapeDtypeStruct(shape, dtype)


def _my_pos():
    return lax.axis_index("x"), lax.axis_index("y"), lax.axis_index("c")


def _peer(pos, k):
    x, y, c = pos
    px = 1 - x if k & 4 else x
    py = 1 - y if k & 2 else y
    pc = 1 - c if k & 1 else c
    return (px, py, pc)


def _lin(p):
    return 4 * p[0] + 2 * p[1] + p[2]


def _mod_fwd(c_loc, ada_w_loc, ada_b_cols):
    D = c_loc.shape[1]
    C6 = ada_w_loc.shape[1]

    def body(c_ref, w_ref, b_ref, cs_ref, mod_ref, call_ref, modp_ref, ssem, rsem):
        pos = _my_pos()
        me = _lin(pos)
        call_ref[me] = c_ref[...]
        sends = []
        for k in range(1, N_DEV):
            cp = pltpu.make_async_remote_copy(src_ref=c_ref, dst_ref=call_ref.at[me], send_sem=ssem.at[k - 1],
                                              recv_sem=rsem.at[k - 1], device_id=_peer(pos, k), device_id_type=MESH)
            cp.start()
            sends.append(cp)
        for k in range(1, N_DEV):
            src = _lin(_peer(pos, k))
            pltpu.make_async_remote_copy(src_ref=c_ref, dst_ref=call_ref.at[src], send_sem=ssem.at[k - 1],
                                         recv_sem=rsem.at[k - 1], device_id=pos, device_id_type=MESH).wait_recv()
        for cp in sends:
            cp.wait_send()
        call = jnp.concatenate([call_ref[b] for b in range(N_DEV)], axis=0)
        cs = call * _sigmoid(call)
        cs_ref[...] = cs
        modp = _dot(cs.astype(BF16), w_ref[...].astype(BF16)) + b_ref[...]
        for b in range(N_DEV):
            modp_ref[b] = modp[b:b + 1, :]
        mod_ref[me] = modp_ref[me]
        sends = []
        for k in range(1, N_DEV):
            peer = _peer(pos, k)
            cp = pltpu.make_async_remote_copy(src_ref=modp_ref.at[_lin(peer)], dst_ref=mod_ref.at[me],
                                              send_sem=ssem.at[N_DEV - 2 + k], recv_sem=rsem.at[N_DEV - 2 + k],
                                              device_id=peer, device_id_type=MESH)
            cp.start()
            sends.append(cp)
        for k in range(1, N_DEV):
            src = _lin(_peer(pos, k))
            pltpu.make_async_remote_copy(src_ref=modp_ref.at[src], dst_ref=mod_ref.at[src],
                                         send_sem=ssem.at[N_DEV - 2 + k], recv_sem=rsem.at[N_DEV - 2 + k],
                                         device_id=pos, device_id_type=MESH).wait_recv()
        for cp in sends:
            cp.wait_send()

    vm = pl.BlockSpec(memory_space=pltpu.VMEM)
    return pl.pallas_call(
        body, name="mod_fwd",
        out_shape=(_S((N_DEV, D), F32), _S((N_DEV, 1, C6), F32)),
        in_specs=[vm, vm, vm], out_specs=(vm, vm),
        scratch_shapes=[pltpu.VMEM((N_DEV, 1, D), F32), pltpu.VMEM((N_DEV, 1, C6), F32),
                        pltpu.SemaphoreType.DMA((2 * (N_DEV - 1),)), pltpu.SemaphoreType.DMA((2 * (N_DEV - 1),))],
        compiler_params=pltpu.CompilerParams(vmem_limit_bytes=VMEM_LIMIT),
    )(c_loc, ada_w_loc, ada_b_cols)


def _small_gather_sum(pack):
    P8 = pack.shape[1]

    def body(p_ref, gat_ref, sum_ref, ssem, rsem):
        pos = _my_pos()
        me = _lin(pos)
        gat_ref[me] = p_ref[...]
        sends = []
        for k in range(1, N_DEV):
            cp = pltpu.make_async_remote_copy(src_ref=p_ref, dst_ref=gat_ref.at[me], send_sem=ssem.at[k - 1],
                                              recv_sem=rsem.at[k - 1], device_id=_peer(pos, k), device_id_type=MESH)
            cp.start()
            sends.append(cp)
        for k in range(1, N_DEV):
            src = _lin(_peer(pos, k))
            pltpu.make_async_remote_copy(src_ref=p_ref, dst_ref=gat_ref.at[src], send_sem=ssem.at[k - 1],
                                         recv_sem=rsem.at[k - 1], device_id=pos, device_id_type=MESH).wait_recv()
        for cp in sends:
            cp.wait_send()
        acc = gat_ref[0]
        for j in range(1, N_DEV):
            acc = acc + gat_ref[j]
        sum_ref[...] = acc

    vm = pl.BlockSpec(memory_space=pltpu.VMEM)
    return pl.pallas_call(
        body, name="small_gather_sum",
        out_shape=(_S((N_DEV, 8, P8), F32), _S((8, P8), F32)),
        in_specs=[vm], out_specs=(vm, vm),
        scratch_shapes=[pltpu.SemaphoreType.DMA((N_DEV - 1,)), pltpu.SemaphoreType.DMA((N_DEV - 1,))],
        compiler_params=pltpu.CompilerParams(vmem_limit_bytes=VMEM_LIMIT),
    )(pack)


def _small_sync_adamw(srcs, pieces, params, n_mod):
    n_src, n_par = len(srcs), len(params)
    rows_of = [sum(-(-w // LANE) for _, _, w in seg) for seg in pieces]
    starts = [sum(rows_of[:k]) for k in range(len(pieces))]
    NR = -(-sum(rows_of) // 8) * 8
    cd = pieces[n_par][0][2]

    def seg_row(arr, k, width):
        r = starts[k]
        if width <= LANE:
            return arr[r:r + 1, 0:width]
        return jnp.concatenate([arr[r + q:r + q + 1, :] for q in range(width // LANE)], axis=1)

    def exchange(*refs):
        src = refs[:n_src]
        total_ref, dmod_ref, pack_ref, gat_ref, ssem, rsem = refs[n_src:]
        pos = _my_pos()
        me = _lin(pos)
        pack_ref[...] = jnp.zeros_like(pack_ref)
        for k, seg in enumerate(pieces):
            r = starts[k]
            for (si, row, width) in seg:
                for q in range(-(-width // LANE)):
                    wq = min(LANE, width - q * LANE)
                    pack_ref[r:r + 1, 0:wq] = src[si][row:row + 1, q * LANE:q * LANE + wq]
                    r += 1
        gat_ref[me] = pack_ref[...]
        sends = []
        for k in range(1, N_DEV):
            cp = pltpu.make_async_remote_copy(src_ref=pack_ref, dst_ref=gat_ref.at[me], send_sem=ssem.at[k - 1],
                                              recv_sem=rsem.at[k - 1], device_id=_peer(pos, k), device_id_type=MESH)
            cp.start()
            sends.append(cp)
        for k in range(1, N_DEV):
            frm = _lin(_peer(pos, k))
            pltpu.make_async_remote_copy(src_ref=pack_ref, dst_ref=gat_ref.at[frm], send_sem=ssem.at[k - 1],
                                         recv_sem=rsem.at[k - 1], device_id=pos, device_id_type=MESH).wait_recv()
        for cp in sends:
            cp.wait_send()
        total = gat_ref[0]
        for j in range(1, N_DEV):
            total = total + gat_ref[j]
        total_ref[...] = total
        for j in range(N_DEV):
            dmod_ref[j:j + 1, :] = seg_row(gat_ref[j], 0, n_mod)

    def update(*refs):
        total = refs[0][...]
        wmv = refs[1:1 + 3 * n_par]
        outs = refs[1 + 3 * n_par:]
        for k in range(n_par):
            n = params[k][0].shape[1]
            g = seg_row(total, k, n)
            w_ref, m_ref, v_ref = wmv[3 * k:3 * k + 3]
            g_ref, d_ref, m2_ref, v2_ref = outs[4 * k:4 * k + 4]
            g_ref[...] = g
            d_ref[...], m2_ref[...], v2_ref[...] = _adamw_math(w_ref[...], g, m_ref[...], v_ref[...])
        gcw_ref, loss_ref = outs[4 * n_par:]
        for t in range(CONV_K):
            r = starts[n_par] + t * (cd // LANE)
            gcw_ref[t:t + 1, :] = jnp.concatenate([total[r + q:r + q + 1, :] for q in range(cd // LANE)], axis=1)
        loss_ref[...] = total[starts[n_par + 1]:starts[n_par + 1] + 1, :]

    vm = pl.BlockSpec(memory_space=pltpu.VMEM)
    total, dmod_all = pl.pallas_call(
        exchange, name="small_sync", out_shape=(_S((NR, LANE), F32), _S((N_DEV, n_mod), F32)),
        in_specs=[vm] * n_src, out_specs=(vm, vm),
        scratch_shapes=[pltpu.VMEM((NR, LANE), F32), pltpu.VMEM((N_DEV, NR, LANE), F32),
                        pltpu.SemaphoreType.DMA((N_DEV - 1,)), pltpu.SemaphoreType.DMA((N_DEV - 1,))],
        compiler_params=pltpu.CompilerParams(vmem_limit_bytes=VMEM_LIMIT),
    )(*srcs)
    out_shape = []
    for w, _, _ in params:
        out_shape += [_S(w.shape, F32)] * 4
    out_shape += [_S((CONV_K, cd), F32), _S((1, LANE), F32)]
    flat = [t for p in params for t in p]
    res = pl.pallas_call(
        update, name="small_adamw", out_shape=tuple(out_shape),
        in_specs=[vm] * (1 + 3 * n_par), out_specs=tuple([vm] * len(out_shape)),
        compiler_params=pltpu.CompilerParams(vmem_limit_bytes=VMEM_LIMIT),
    )(total, *flat)
    return (*res[:-1], dmod_all, res[-1])


def _ag_weights(shards, after):
    n = len(shards)

    def body(*refs):
        ins, outs = refs[:n], refs[n + 1:2 * n + 1]
        ssem, rsem, lsem = refs[2 * n + 1:]
        x, y, c = pos = _my_pos()
        me = _lin(pos)
        sib = (x, y, 1 - c)
        chips = [(1 - x, y), (x, 1 - y), (1 - x, 1 - y)]

        def copy(a, k, block, to, src=None):
            return pltpu.make_async_remote_copy(
                src_ref=outs[a].at[block] if src is None else src, dst_ref=outs[a].at[block],
                send_sem=ssem.at[a * 7 + k], recv_sem=rsem.at[a * 7 + k], device_id=to, device_id_type=MESH)

        local = [pltpu.make_async_copy(ins[a], outs[a].at[me], lsem.at[a]) for a in range(n)]
        for cp in local:
            cp.start()
        first = []
        for a in range(n):
            first.append(copy(a, 0, me, sib, src=ins[a]))
            first += [copy(a, 1 + j, me, (*chip, c), src=ins[a]) for j, chip in enumerate(chips)]
        for cp in first:
            cp.start()
        passed = []
        for a in range(n):
            for j, chip in enumerate(chips):
                blk = _lin((*chip, c))
                copy(a, 1 + j, blk, pos).wait_recv()
                cp = copy(a, 4 + j, blk, sib)
                cp.start()
                passed.append(cp)
        for a in range(n):
            copy(a, 0, _lin(sib), pos).wait_recv()
            for j, chip in enumerate(chips):
                copy(a, 4 + j, _lin((*chip, 1 - c)), pos).wait_recv()
        for cp in first + passed:
            cp.wait_send()
        for cp in local:
            cp.wait()

    hbm = pl.BlockSpec(memory_space=pl.ANY)
    return pl.pallas_call(
        body, name="ag_weights",
        out_shape=tuple(_S((N_DEV,) + s.shape, s.dtype) for s in shards),
        in_specs=[hbm] * (n + 1), out_specs=tuple([hbm] * n),
        scratch_shapes=[pltpu.SemaphoreType.DMA((7 * n,)), pltpu.SemaphoreType.DMA((7 * n,)),
                        pltpu.SemaphoreType.DMA((n,))],
    )(*shards, after)


def _rs_d2d(blocked, name):
    n = len(blocked)

    def body(*refs):
        ins, outs = refs[:n], refs[n:2 * n]
        ssem, rsem = refs[2 * n:]
        x, y, c = pos = _my_pos()
        sib = (x, y, 1 - c)
        cps = []
        for a in range(n):
            for j in range(N_CHIP):
                cp = pltpu.make_async_remote_copy(
                    src_ref=ins[a].at[2 * j + (1 - c)], dst_ref=outs[a].at[j], send_sem=ssem.at[a * N_CHIP + j],
                    recv_sem=rsem.at[a * N_CHIP + j], device_id=sib, device_id_type=MESH)
                cp.start()
                cps.append(cp)
        for cp in cps:
            cp.wait_recv()
        for cp in cps:
            cp.wait_send()

    hbm = pl.BlockSpec(memory_space=pl.ANY)
    return pl.pallas_call(
        body, name=name,
        out_shape=tuple(_S((N_CHIP,) + b.shape[1:], b.dtype) for b in blocked),
        in_specs=[hbm] * n, out_specs=tuple([hbm] * n),
        scratch_shapes=[pltpu.SemaphoreType.DMA((N_CHIP * n,)), pltpu.SemaphoreType.DMA((N_CHIP * n,))],
    )(*blocked)


_HBM = pl.BlockSpec(memory_space=pltpu.HBM)
_SEM = pl.BlockSpec(memory_space=pltpu.SEMAPHORE)
_ANY = pl.BlockSpec(memory_space=pl.ANY)
_EFFECT = pltpu.SideEffectType.DATAFLOW_SIDE_EFFECTING


def _in_hbm(a):
    return pltpu.with_memory_space_constraint(a, pltpu.HBM)


def _plan_gather(n):
    def copies(pos):
        x, y, c = pos
        out = []
        for a in range(n):
            for dev in [(x, y, 1 - c)] + [(*_peer(pos, 2 * k)[:2], c) for k in range(1, N_CHIP)]:
                out.append((a, None, n + a, _lin(pos), dev, _lin(dev)))
        return out
    return copies


def _plan_forward(n):
    def copies(pos):
        x, y, c = pos
        out = []
        for a in range(n):
            for k in range(1, N_CHIP):
                tx, ty, _ = _peer(pos, 2 * k)
                out.append((a, _lin((tx, ty, c)), a, _lin((tx, ty, c)), (x, y, 1 - c), _lin((tx, ty, 1 - c))))
        return out
    return copies


def _plan_scatter_all(n):
    def copies(pos):
        out = []
        for a in range(n):
            for k in range(1, N_DEV):
                dev = _peer(pos, k)
                out.append((a, _lin(dev), n + a, _lin(pos), dev, _lin(dev)))
        return out
    return copies


def _plan_scatter(n):
    def copies(pos):
        x, y, c = pos
        out = []
        for a in range(n):
            for k in range(1, N_CHIP):
                tx, ty, _ = _peer(pos, 2 * k)
                out.append((a, 2 * tx + ty, n + a, 2 * x + y, (tx, ty, c), 2 * tx + ty))
        return out
    return copies


def _split_copy(refs, cp, ssem, rsem, i, arrival):
    si, s_slot, di, d_slot, dev, a_slot = cp
    return pltpu.make_async_remote_copy(
        src_ref=refs[si] if s_slot is None else refs[si].at[s_slot], dst_ref=refs[di].at[a_slot if arrival else d_slot],
        send_sem=ssem.at[i], recv_sem=rsem.at[i], device_id=dev, device_id_type=MESH)


def _split_start(arrays, copies, after, name):
    n = len(arrays)
    n_cp = len(copies((0, 0, 0)))

    def body(*refs):
        ssem, rsem, token = refs[n + 1], refs[n + 2], refs[-1]
        for i, cp in enumerate(copies(_my_pos())):
            _split_copy(refs, cp, ssem, rsem, i, False).start()
        token[...] = jnp.zeros_like(token)

    res = pl.pallas_call(
        body, name=name,
        out_shape=(pltpu.SemaphoreType.DMA((n_cp,)), pltpu.SemaphoreType.DMA((n_cp,)),
                   *[pltpu.HBM(a.shape, a.dtype) for a in arrays], _S((8, LANE), F32)),
        in_specs=[_HBM] * n + [_ANY],
        out_specs=(_SEM, _SEM, *[_HBM] * n, pl.BlockSpec(memory_space=pltpu.VMEM)),
        input_output_aliases={a: 2 + a for a in range(n)},
        compiler_params=pltpu.CompilerParams(has_side_effects=_EFFECT),
    )(*[_in_hbm(a) for a in arrays], after)
    return res[0], res[1], list(res[2:2 + n]), res[-1]


def _split_wait(ssem, rsem, arrays, copies, after, name):
    n = len(arrays)

    def body(*refs):
        for i, cp in enumerate(copies(_my_pos())):
            d = _split_copy(refs, cp, refs[n], refs[n + 1], i, True)
            d.wait_send()
            d.wait_recv()

    res = pl.pallas_call(
        body, name=name,
        out_shape=tuple(pltpu.HBM(a.shape, a.dtype) for a in arrays),
        in_specs=[_HBM] * n + [_SEM, _SEM, _ANY], out_specs=tuple([_HBM] * n),
        input_output_aliases={a: a for a in range(n)},
        compiler_params=pltpu.CompilerParams(has_side_effects=_EFFECT),
    )(*arrays, ssem, rsem, after)
    return list(res)


def _row_tile(R, itemsize_rows=16, cap=256):
    t = cap - cap % itemsize_rows
    while t >= itemsize_rows:
        if R % t == 0:
            return t
        t -= itemsize_rows
    return R


def _pair_sum(blocked, recv, core):
    _, R, C = blocked.shape
    tr = _row_tile(R)

    def body(ids_ref, a_ref, b_ref, o_ref):
        del ids_ref
        o_ref[...] = (a_ref[...] + b_ref[...]).astype(BF16)

    gs = pltpu.PrefetchScalarGridSpec(
        num_scalar_prefetch=1, grid=(N_CHIP, R // tr),
        in_specs=[pl.BlockSpec((1, tr, C), lambda j, r, ids: (2 * j + ids[0], r, 0)),
                  pl.BlockSpec((1, tr, C), lambda j, r, ids: (j, r, 0))],
        out_specs=pl.BlockSpec((1, tr, C), lambda j, r, ids: (j, r, 0)))
    return pl.pallas_call(body, name="pair_sum", grid_spec=gs, out_shape=_S((N_CHIP, R, C), BF16),
                          compiler_params=_params(("arbitrary", "arbitrary")))(core, blocked, recv)


def _adamw_math(w, g, m, v):
    m2 = ADAM_B1 * m + (1.0 - ADAM_B1) * g
    v2 = ADAM_B2 * v + (1.0 - ADAM_B2) * (g * g)
    m_hat = m2 / (1.0 - ADAM_B1 ** ADAM_STEP)
    v_hat = v2 / (1.0 - ADAM_B2 ** ADAM_STEP)
    delta = -ADAM_LR * (m_hat / (jnp.sqrt(v_hat) + ADAM_EPS) + ADAM_WD * w)
    return delta, m2, v2


def _sum_adamw_split(pairs, land, chips, w, m, v, transposed=False):
    R, C = w.shape
    n_slots = chips.shape[0]
    tr = _row_tile(R, 128 if transposed else 16)

    def body(ids_ref, *refs):
        del ids_ref
        parts, (w_ref, m_ref, v_ref, g_ref, d_ref, m2_ref, v2_ref) = refs[:n_slots], refs[n_slots:]
        g = parts[0][0].astype(F32)
        for p_ref in parts[1:]:
            g = g + p_ref[0].astype(F32)
        if transposed:
            g = g.T
        g_ref[...] = g
        d_ref[...], m2_ref[...], v2_ref[...] = _adamw_math(w_ref[...], g, m_ref[...], v_ref[...])

    t = pl.BlockSpec((tr, C), lambda r, ids: (r, 0))
    if transposed:
        slot = lambda k: pl.BlockSpec((1, C, tr), lambda r, ids: (ids[k], 0, r))
    else:
        slot = lambda k: pl.BlockSpec((1, tr, C), lambda r, ids: (ids[k], r, 0))
    gs = pltpu.PrefetchScalarGridSpec(num_scalar_prefetch=1, grid=(R // tr,),
                                      in_specs=[slot(k) for k in range(n_slots)] + [t, t, t], out_specs=(t, t, t, t))
    return pl.pallas_call(body, name="sum_adamw_split", grid_spec=gs, out_shape=tuple(_S((R, C), F32) for _ in range(4)),
                          compiler_params=_params(("arbitrary",)))(chips, pairs, *[land] * (n_slots - 1), w, m, v)


def _adamw(w, g, m, v):
    R, C = w.shape
    tr = _row_tile(R, 8)

    def body(w_ref, g_ref, m_ref, v_ref, d_ref, m2_ref, v2_ref):
        d_ref[...], m2_ref[...], v2_ref[...] = _adamw_math(w_ref[...], g_ref[...], m_ref[...], v_ref[...])

    t = pl.BlockSpec((tr, C), lambda r: (r, 0))
    return pl.pallas_call(body, name="adamw", grid=(R // tr,), in_specs=[t, t, t, t], out_specs=(t, t, t),
                          out_shape=tuple(_S((R, C), F32) for _ in range(3)),
                          compiler_params=_params(("arbitrary",)))(w, g, m, v)


def _ada_grad_adamw(cs16, dmod16, w, m, v):
    D, C6 = w.shape
    tr = _row_tile(D, 8, 256)

    def body(cs_ref, dm_ref, w_ref, m_ref, v_ref, g_ref, d_ref, m2_ref, v2_ref):
        g = _dot_tn(cs_ref[...].astype(BF16), dm_ref[...].astype(BF16))
        g_ref[...] = g
        d_ref[...], m2_ref[...], v2_ref[...] = _adamw_math(w_ref[...], g, m_ref[...], v_ref[...])

    t = pl.BlockSpec((tr, C6), lambda r: (r, 0))
    return pl.pallas_call(
        body, name="ada_grad_adamw", grid=(D // tr,),
        in_specs=[pl.BlockSpec((16, tr), lambda r: (0, r)), _acc((16, C6)), t, t, t], out_specs=(t, t, t, t),
        out_shape=tuple(_S((D, C6), F32) for _ in range(4)), compiler_params=_params(("arbitrary",)))(cs16, dmod16, w, m, v)


def _pick(n, cands):
    for c in cands:
        if n % c == 0:
            return c
    return n


def _matmul_tn(a, b, name, square_a=False):
    L, K = a.shape
    N = b.shape[1]
    bk = _pick(K, (1024, 512, 256, 128))
    bn = _pick(N, (1024, 768, 512, 256, 128))
    tl = _pick(L, (1024, 512, 256, 128))
    n_l = L // tl

    def body(a_ref, b_ref, o_ref, acc_ref):
        l = pl.program_id(2)

        @pl.when(l == 0)
        def _():
            acc_ref[...] = jnp.zeros_like(acc_ref)
        av = a_ref[...]
        if square_a:
            av = av.astype(F32)
            av = av * av
        acc_ref[...] += _dot_tn(av.astype(BF16), b_ref[...].astype(BF16))

        @pl.when(l == n_l - 1)
        def _():
            o_ref[...] = acc_ref[...].astype(BF16)

    return pl.pallas_call(
        body, name=name, grid=(K // bk, N // bn, n_l),
        in_specs=[pl.BlockSpec((tl, bk), lambda k, n, l: (l, k)), pl.BlockSpec((tl, bn), lambda k, n, l: (l, n))],
        out_specs=pl.BlockSpec((bk, bn), lambda k, n, l: (k, n)), out_shape=_S((K, N), BF16),
        scratch_shapes=[pltpu.VMEM((bk, bn), F32)],
        compiler_params=_params(("arbitrary", "arbitrary", "arbitrary")))(a, b)


def _merge_blocks(a):
    n, R, C = a.shape
    cb = _pick(C, (256, 128))

    def body(i_ref, o_ref):
        for j in range(n):
            o_ref[R * j:R * (j + 1), :] = i_ref[j]

    return pl.pallas_call(body, name="merge_blocks", out_shape=_S((n * R, C), a.dtype), grid=(C // cb,),
                          in_specs=[pl.BlockSpec((n, R, cb), lambda c: (0, 0, c))],
                          out_specs=pl.BlockSpec((n * R, cb), lambda c: (0, c)),
                          compiler_params=_params(("arbitrary",)))(a)


def _split_blocks(a, n):
    NR, C = a.shape
    R = NR // n
    cb = _pick(C, (256, 128))

    def body(i_ref, o_ref):
        for j in range(n):
            o_ref[j] = i_ref[R * j:R * (j + 1), :]

    return pl.pallas_call(body, name="split_blocks", out_shape=_S((n, R, C), a.dtype), grid=(C // cb,),
                          in_specs=[pl.BlockSpec((NR, cb), lambda c: (0, c))],
                          out_specs=pl.BlockSpec((n, R, cb), lambda c: (0, 0, c)),
                          compiler_params=_params(("arbitrary",)))(a)


def _gw_in(pieces, u1, dims):
    L, D = u1.shape
    H = dims["H"]
    r_z, r_xbc, r_dt, r_q, r_kv = _proj_rows(dims)
    PROJ = r_kv[1]
    tl = _pick(L, (512, 256, 128))
    n_l = L // tl

    def body(dz_ref, dxbc_ref, ddt_ref, dq_ref, dkv_ref, u_ref, o_ref):
        @pl.when(pl.program_id(0) == 0)
        def _():
            o_ref[...] = jnp.zeros_like(o_ref)
        u = u_ref[...]
        for ref, (r0, r1) in ((dz_ref, r_z), (dxbc_ref, r_xbc), (dq_ref, r_q), (dkv_ref, r_kv)):
            o_ref[r0:r1, :] += _dot_tn(ref[...].astype(BF16), u)
        o_ref[r_dt[0]:r_dt[0] + H, :] += _dot_tn(ddt_ref[...].astype(BF16), u)[0:H, :]

    return pl.pallas_call(
        body, name="gw_in", grid=(n_l,),
        in_specs=[_tile(p.shape[1], tl) for p in pieces] + [_tile(D, tl)],
        out_specs=_acc((PROJ, D)), out_shape=_S((PROJ, D), F32),
        compiler_params=_params(("arbitrary",)))(*pieces, u1)


def _proj_rows(dims):
    W, CD, H, AW, KVW2 = dims["W"], dims["CD"], dims["H"], dims["AW"], dims["KVW2"]
    o_dt = W + CD
    o_q = o_dt + H
    return (0, W), (W, o_dt), (o_dt, o_dt + LANE), (o_q, o_q + AW), (o_q + AW, o_q + AW + KVW2)


def _ln_in_proj(x, g, b, sc, sh, w_t, dims):
    L, D = x.shape
    W, CD, AW, KVW2 = dims["W"], dims["CD"], dims["AW"], dims["KVW2"]
    PROJ = w_t.shape[0]
    tm = _pick(L, (MLP_TM, 128))
    r_z, r_xbc, r_dt, r_q, r_kv = _proj_rows(dims)

    def body(x_ref, g_ref, b_ref, sc_ref, sh_ref, w_ref, xhat_ref, rstd_ref, u1_ref, z_ref, xbc_ref, q_ref, kv_ref, dt_ref):
        xhat, rstd = _ln_fwd(x_ref[...])
        xhat_ref[...] = xhat
        rstd_ref[...] = rstd
        h0 = xhat * g_ref[...] + b_ref[...]
        u1 = (h0 * (1.0 + sc_ref[...]) + sh_ref[...]).astype(BF16)
        u1_ref[...] = u1
        z_ref[...] = _dot_nt(u1, w_ref[r_z[0]:r_z[1], :])
        xbc_ref[...] = _dot_nt(u1, w_ref[r_xbc[0]:r_xbc[1], :])
        q_ref[...] = _dot_nt(u1, w_ref[r_q[0]:r_q[1], :]).astype(BF16)
        kv_ref[...] = _dot_nt(u1, w_ref[r_kv[0]:r_kv[1], :]).astype(BF16)
        dt_ref[...] = _dot_nt(u1, w_ref[r_dt[0]:r_dt[1], :])

    v = _acc((1, D))
    return pl.pallas_call(
        body, name="ln_in_proj", grid=(L // tm,),
        in_specs=[_tile(D, tm), v, v, v, v, _res((PROJ, D))],
        out_specs=(_tile(D, tm), _tile(1, tm), _tile(D, tm), _tile(W, tm), _tile(CD, tm), _tile(AW, tm),
                   _tile(KVW2, tm), _tile(LANE, tm)),
        out_shape=(_S((L, D), F32), _S((L, 1), F32), _S((L, D), BF16), _S((L, W), F32), _S((L, CD), F32),
                   _S((L, AW), BF16), _S((L, KVW2), BF16), _S((L, LANE), F32)),
        compiler_params=_params(("arbitrary",)))(x, g, b, sc, sh, w_t)


def _conv_act(cur_ref, prev_ref, cw_ref, cb_ref, ext_ref, first):
    T = cur_ref.shape[0]
    ext_ref[0:HALO, :] = jnp.where(first, 0.0, prev_ref[...])
    ext_ref[HALO:HALO + T, :] = cur_ref[...]
    pre = cb_ref[...] + cw_ref[0:1, :] * ext_ref[HALO - 3:HALO - 3 + T, :]
    for k in range(1, CONV_K):
        pre = pre + cw_ref[k:k + 1, :] * ext_ref[HALO - 3 + k:HALO - 3 + k + T, :]
    sig = _sigmoid(pre)
    return pre * sig, pre, sig


def _tri(T, upper=False):
    r = lax.broadcasted_iota(jnp.int32, (T, T), 0)
    c = lax.broadcasted_iota(jnp.int32, (T, T), 1)
    return (r <= c) if upper else (r >= c)


def _expand_heads(dst_ref, v, n_heads):
    for h in range(n_heads):
        dst_ref[:, h * HEAD_DIM:(h + 1) * HEAD_DIM] = jnp.broadcast_to(v[:, h:h + 1], (v.shape[0], HEAD_DIM))


def _head_reduce(v):
    wdt = v.shape[1]
    ch = lax.broadcasted_iota(jnp.int32, (wdt, LANE), 0)
    lo = lax.broadcasted_iota(jnp.int32, (wdt, LANE), 1) * HEAD_DIM
    onehot = ((ch >= lo) & (ch < lo + HEAD_DIM)).astype(BF16)
    hi = v.astype(BF16)
    rest = (v - hi.astype(F32)).astype(BF16)
    return _dot(hi, onehot) + _dot(rest, onehot)


def _conv_ssd(xbc, dt_raw, z, cw, cb, dtb, alog, dsk, nw, dims):
    L, CD = xbc.shape
    W, H, G, N = dims["W"], dims["H"], SSD_GROUPS, SSD_STATE
    T = CHUNK
    R = H // G
    GW = W // G
    nc = L // T
    HP = H * HEAD_DIM

    def body(xbc_ref, prev_ref, dt_ref, z_ref, cw_ref, cb_ref, dtb_ref, alog_ref, dsk_ref, nw_ref,
             y_ref, yn_ref, sp_ref, ext_ref, s_ref, ybuf_ref, dtx_ref, acx_ref, xb_ref):
        i = pl.program_id(0)

        @pl.when(i == 0)
        def _():
            s_ref[...] = jnp.zeros_like(s_ref)

        act, _, _ = _conv_act(xbc_ref, prev_ref, cw_ref, cb_ref, ext_ref, i == 0)
        xs = act[:, :W]
        dt = _softplus(dt_ref[...] + dtb_ref[...])
        a = dt * (-jnp.exp(alog_ref[...]))
        low = _tri(T)
        acum = _dot_hi(low.astype(F32), a)
        acum_t = acum.T
        _expand_heads(dtx_ref, dt, H)
        _expand_heads(acx_ref, acum, H)
        acx = acx_ref[...]
        lastx = acx[T - 1:T, :]
        xd = xs * dtx_ref[...]
        xb_ref[...] = xd.astype(BF16)
        xdb = (xd * jnp.exp(lastx - acx)).astype(BF16)
        ex = jnp.exp(acx)
        elx = jnp.exp(lastx)
        for g in range(G):
            gs = slice(g * GW, (g + 1) * GW)
            bgb = act[:, W + g * N:W + (g + 1) * N].astype(BF16)
            cgb = act[:, W + G * N + g * N:W + G * N + (g + 1) * N].astype(BF16)
            stg = s_ref[:, gs]
            sp_ref[0, :, gs] = stg
            yoff = ex[:, gs] * _dot(cgb, stg.astype(BF16))
            s_ref[:, gs] = stg * elx[:, gs] + _dot_tn(bgb, xdb[:, gs])
            cb_g = _dot_nt(cgb, bgb)
            for r in range(R):
                h = g * R + r
                hs = slice(h * HEAD_DIM, (h + 1) * HEAD_DIM)
                lm = jnp.where(low, jnp.exp(acum[:, h:h + 1] - acum_t[h:h + 1, :]), 0.0)
                ybuf_ref[:, hs] = _dot((cb_g * lm).astype(BF16), xb_ref[:, hs]) + yoff[:, r * HEAD_DIM:(r + 1) * HEAD_DIM]
        y = ybuf_ref[...] + dsk_ref[...] * xs
        y_ref[...] = y
        zz = z_ref[...]
        hh = y * (zz * _sigmoid(zz))
        for g in range(G):
            gs = slice(g * GW, (g + 1) * GW)
            hg = hh[:, gs]
            yn_ref[:, gs] = (hg * lax.rsqrt(_mean(hg * hg) + RMS_EPS) * nw_ref[:, gs]).astype(BF16)

    return pl.pallas_call(
        body, name="conv_ssd", grid=(nc,),
        in_specs=[_tile(CD, T), pl.BlockSpec((HALO, CD), lambda i: (jnp.maximum(i * (T // HALO) - 1, 0), 0)),
                  _tile(LANE, T), _tile(W, T), _acc((CONV_K, CD)), _acc((1, CD)), _acc((1, LANE)), _acc((1, LANE)),
                  _acc((1, W)), _acc((1, W))],
        out_specs=(_tile(W, T), _tile(W, T), pl.BlockSpec((1, N, HP), lambda i: (i, 0, 0))),
        out_shape=(_S((L, W), F32), _S((L, W), BF16), _S((nc, N, HP), F32)),
        scratch_shapes=[pltpu.VMEM((T + HALO, CD), F32), pltpu.VMEM((N, HP), F32), pltpu.VMEM((T, W), F32),
                        pltpu.VMEM((T, W), F32), pltpu.VMEM((T, W), F32), pltpu.VMEM((T, W), BF16)],
        compiler_params=_params(("arbitrary",)))(xbc, xbc, dt_raw, z, cw, cb, dtb, alog, dsk, nw)


def _attn_mask(T, i):
    r = lax.broadcasted_iota(jnp.int32, (T, 2 * T), 0)
    c = lax.broadcasted_iota(jnp.int32, (T, 2 * T), 1)
    dist = r + T - c
    valid = (dist >= 0) & (dist < CHUNK) & ((c >= T) | (i > 0))
    return dist.astype(F32), valid


def _attn_probs(s_raw, dist, valid, slope, sink, axis):
    s = s_raw * (HEAD_DIM ** -0.5) - slope * dist
    s = jnp.where(valid, s, NEG)
    m = jnp.maximum(jnp.max(s, axis=axis, keepdims=True), sink)
    p = jnp.exp(s - m)
    e_sink = jnp.exp(sink - m)
    inv = 1.0 / (jnp.sum(p, axis=axis, keepdims=True) + e_sink)
    return p * inv, e_sink * inv


def _kv_heads(kvc_ref, kvp_ref, g, n_kv):
    ks = slice(g * HEAD_DIM, (g + 1) * HEAD_DIM)
    vs = slice((n_kv + g) * HEAD_DIM, (n_kv + g + 1) * HEAD_DIM)
    kk = jnp.concatenate([kvp_ref[:, ks], kvc_ref[:, ks]], axis=0)
    vv = jnp.concatenate([kvp_ref[:, vs], kvc_ref[:, vs]], axis=0)
    return kk, vv


def _swa_fwd(q, kv, sinks, dims):
    L, AW = q.shape
    KV, KVW2 = dims["KV"], dims["KVW2"]
    T = CHUNK
    nb = L // T
    slopes = _alibi_slopes(dims["AH"])

    def body(q_ref, kvc_ref, kvp_ref, sink_ref, o_ref, qg_ref, p_ref):
        i = pl.program_id(0)
        dist, valid = _attn_mask(T, i)
        for g in range(KV):
            kk, vv = _kv_heads(kvc_ref, kvp_ref, g, KV)
            for r in range(GQA):
                h = g * GQA + r
                qg_ref[r * T:(r + 1) * T, :] = q_ref[:, h * HEAD_DIM:(h + 1) * HEAD_DIM]
            s_all = _dot_nt(qg_ref[...], kk)
            for r in range(GQA):
                h = g * GQA + r
                p, _ = _attn_probs(s_all[r * T:(r + 1) * T, :], dist, valid, slopes[h], sink_ref[h], -1)
                p_ref[r * T:(r + 1) * T, :] = p.astype(BF16)
            o_all = _dot(p_ref[...], vv)
            for r in range(GQA):
                h = g * GQA + r
                o_ref[:, h * HEAD_DIM:(h + 1) * HEAD_DIM] = o_all[r * T:(r + 1) * T, :].astype(BF16)

    return pl.pallas_call(
        body, name="swa_fwd", grid=(nb,),
        in_specs=[_tile(AW, T), _tile(KVW2, T), pl.BlockSpec((T, KVW2), lambda i: (jnp.maximum(i - 1, 0), 0)),
                  pl.BlockSpec(memory_space=pltpu.SMEM)],
        out_specs=_tile(AW, T), out_shape=_S((L, AW), BF16),
        scratch_shapes=[pltpu.VMEM((GQA * T, HEAD_DIM), BF16), pltpu.VMEM((GQA * T, 2 * T), BF16)],
        compiler_params=_params(("arbitrary",)))(q, kv, kv, sinks)


def _out_proj_ln1(yn, o, w_out, xhat0, vecs, alpha):
    L, W = yn.shape
    D = xhat0.shape[1]
    MIX = w_out.shape[0]
    tm = _pick(L, (MLP_TM, 128))

    def body(yn_ref, o_ref, w_ref, xh_ref, v_ref, mix_ref, xhat1_ref, rstd1_ref, u2_ref):
        mix = _dot(yn_ref[...], w_ref[0:W, :]) + _dot(o_ref[...], w_ref[W:MIX, :])
        mix_ref[...] = mix
        h0 = xh_ref[...] * v_ref[0:1, :] + v_ref[1:2, :]
        xhat1, rstd1 = _ln_fwd(alpha * h0 + (1.0 + v_ref[2:3, :]) * mix)
        xhat1_ref[...] = xhat1
        rstd1_ref[...] = rstd1
        h1 = xhat1 * v_ref[3:4, :] + v_ref[4:5, :]
        u2_ref[...] = (h1 * (1.0 + v_ref[5:6, :]) + v_ref[6:7, :]).astype(BF16)

    return pl.pallas_call(
        body, name="out_proj_ln1", grid=(L // tm,),
        in_specs=[_tile(W, tm), _tile(MIX - W, tm), _res((MIX, D)), _tile(D, tm), _acc((8, D))],
        out_specs=(_tile(D, tm), _tile(D, tm), _tile(1, tm), _tile(D, tm)),
        out_shape=(_S((L, D), F32), _S((L, D), F32), _S((L, 1), F32), _S((L, D), BF16)),
        compiler_params=_params(("arbitrary",)))(yn, o, w_out, xhat0, vecs)


def _mlp_loss(u2, w1, w2, xhat1, tgt, vecs, b1, alpha):
    L, D = xhat1.shape
    FF = w1.shape[1]
    tm = _pick(L, (MLP_TM, 128))
    sub = min(tm, MLP_SUB)
    fc = _pick(FF, (MLP_FC, 256, 128))

    def body(u2_ref, w1_ref, w2_ref, xh_ref, t_ref, v_ref, b1_ref, rr_ref, dr2_ref, acc_ref, loss_ref):
        @pl.when(pl.program_id(0) == 0)
        def _():
            acc_ref[...] = jnp.zeros_like(acc_ref)
            loss_ref[...] = jnp.zeros_like(loss_ref)

        for s in range(tm // sub):
            rs = slice(s * sub, (s + 1) * sub)
            u2 = u2_ref[rs, :]
            f = jnp.zeros((sub, D), F32) + v_ref[5:6, :]
            for j in range(FF // fc):
                cs = slice(j * fc, (j + 1) * fc)
                rr = jnp.maximum(_dot(u2, w1_ref[:, cs]) + b1_ref[:, cs], 0.0)
                rr_ref[rs, cs] = rr.astype(BF16)
                f = f + _dot((rr * rr).astype(BF16), w2_ref[cs, :])
            xhat1 = xh_ref[rs, :]
            h1 = xhat1 * v_ref[0:1, :] + v_ref[1:2, :]
            xhat2, rstd2 = _ln_fwd(alpha * h1 + (1.0 + v_ref[2:3, :]) * f)
            e = xhat2 * v_ref[3:4, :] + v_ref[4:5, :] - t_ref[rs, :]
            loss_ref[...] += 0.5 * jnp.sum(_mean(e * e))
            dy = e * (1.0 / D)
            dr2 = _ln_bwd(dy * v_ref[3:4, :], xhat2, rstd2)
            dr2_ref[rs, :] = dr2
            acc_ref[0:1, :] += _colsum(dy * xhat2)
            acc_ref[1:2, :] += _colsum(dy)
            acc_ref[2:3, :] += _colsum(dr2 * f)

    return pl.pallas_call(
        body, name="mlp_loss", grid=(L // tm,),
        in_specs=[_tile(D, tm), _res((D, FF)), _res((FF, D)), _tile(D, tm), _tile(D, tm), _acc((8, D)), _acc((1, FF))],
        out_specs=(_tile(FF, tm), _tile(D, tm), _acc((8, D)), _acc((1, LANE))),
        out_shape=(_S((L, FF), BF16), _S((L, D), F32), _S((8, D), F32), _S((1, LANE), F32)),
        compiler_params=_params(("arbitrary",)))(u2, w1, w2, xhat1, tgt, vecs, b1)


def _mlp_bwd_a(dr2, rr, w2, g2):
    L, D = dr2.shape
    FF = w2.shape[0]
    tm = _pick(L, (MLP_TM, 128))
    fc = _pick(FF, (MLP_FC, 256, 128))

    def body(dr2_ref, rr_ref, w2_ref, g2_ref, df_ref, da_ref, gb2_ref, gb1_ref):
        @pl.when(pl.program_id(0) == 0)
        def _():
            gb2_ref[...] = jnp.zeros_like(gb2_ref)
            gb1_ref[...] = jnp.zeros_like(gb1_ref)

        df = (1.0 + g2_ref[...]) * dr2_ref[...]
        gb2_ref[...] += _colsum(df)
        dfb = df.astype(BF16)
        df_ref[...] = dfb
        for j in range(FF // fc):
            cs = slice(j * fc, (j + 1) * fc)
            da = _dot_nt(dfb, w2_ref[cs, :]) * (2.0 * rr_ref[:, cs].astype(F32))
            gb1_ref[:, cs] += _colsum(da)
            da_ref[:, cs] = da.astype(BF16)

    return pl.pallas_call(
        body, name="mlp_bwd_a", grid=(L // tm,),
        in_specs=[_tile(D, tm), _tile(FF, tm), _res((FF, D)), _acc((1, D))],
        out_specs=(_tile(D, tm), _tile(FF, tm), _acc((1, D)), _acc((1, FF))),
        out_shape=(_S((L, D), BF16), _S((L, FF), BF16), _S((1, D), F32), _S((1, FF), F32)),
        compiler_params=_params(("arbitrary",)))(dr2, rr, w2, g2)


def _mlp_bwd_b(da, w1, dr2, xhat1, rstd1, mix, w_out, vecs, alpha, W):
    L, FF = da.shape
    D = dr2.shape[1]
    MIX = w_out.shape[0]
    tm = _pick(L, (MLP_TM, 128))

    def body(da_ref, w1_ref, dr2_ref, xh_ref, rs_ref, mix_ref, wo_ref, v_ref, dmix_ref, dh0_ref, dyn_ref, do_ref, acc_ref):
        @pl.when(pl.program_id(0) == 0)
        def _():
            acc_ref[...] = jnp.zeros_like(acc_ref)

        du2 = _dot_nt(da_ref[...], w1_ref[...])
        xhat1 = xh_ref[...]
        h1 = xhat1 * v_ref[0:1, :] + v_ref[1:2, :]
        acc_ref[0:1, :] += _colsum(du2 * h1)
        acc_ref[1:2, :] += _colsum(du2)
        dh1 = alpha * dr2_ref[...] + du2 * (1.0 + v_ref[2:3, :])
        acc_ref[2:3, :] += _colsum(dh1 * xhat1)
        acc_ref[3:4, :] += _colsum(dh1)
        dr1 = _ln_bwd(dh1 * v_ref[0:1, :], xhat1, rs_ref[...])
        acc_ref[4:5, :] += _colsum(dr1 * mix_ref[...])
        dh0_ref[...] = alpha * dr1
        dmix = ((1.0 + v_ref[3:4, :]) * dr1).astype(BF16)
        dmix_ref[...] = dmix
        dyn_ref[...] = _dot_nt(dmix, wo_ref[0:W, :])
        do_ref[...] = _dot_nt(dmix, wo_ref[W:MIX, :]).astype(BF16)

    return pl.pallas_call(
        body, name="mlp_bwd_b", grid=(L // tm,),
        in_specs=[_tile(FF, tm), _res((D, FF)), _tile(D, tm), _tile(D, tm), _tile(1, tm), _tile(D, tm), _res((MIX, D)),
                  _acc((8, D))],
        out_specs=(_tile(D, tm), _tile(D, tm), _tile(W, tm), _tile(MIX - W, tm), _acc((8, D))),
        out_shape=(_S((L, D), BF16), _S((L, D), F32), _S((L, W), F32), _S((L, MIX - W), BF16), _S((8, D), F32)),
        compiler_params=_params(("arbitrary",)))(da, w1, dr2, xhat1, rstd1, mix, w_out, vecs)


def _swa_bwd(q, kv, do, sinks, dims):
    L, AW = q.shape
    KV, KVW2 = dims["KV"], dims["KVW2"]
    T = CHUNK
    nb = L // T
    slopes = _alibi_slopes(dims["AH"])
    scale = HEAD_DIM ** -0.5

    def body(q_ref, kvc_ref, kvp_ref, do_ref, sink_ref, dq_ref, dkv_ref, dsink_ref, carry_ref,
             qg_ref, dog_ref, pt_ref, dst_ref):
        i = pl.program_id(0)

        @pl.when(i == 0)
        def _():
            carry_ref[...] = jnp.zeros_like(carry_ref)
            dsink_ref[...] = jnp.zeros_like(dsink_ref)

        @pl.when(i < nb)
        def _():
            c = lax.broadcasted_iota(jnp.int32, (2 * T, T), 0)
            r_ = lax.broadcasted_iota(jnp.int32, (2 * T, T), 1)
            dist_i = r_ + T - c
            valid = (dist_i >= 0) & (dist_i < CHUNK) & ((c >= T) | (i > 0))
            dist = dist_i.astype(F32)
            lane = lax.broadcasted_iota(jnp.int32, (1, LANE), 1)
            dsink = jnp.zeros((1, LANE), F32)
            dks, dvs = [], []
            for g in range(KV):
                kk, vv = _kv_heads(kvc_ref, kvp_ref, g, KV)
                for r in range(GQA):
                    hs = slice((g * GQA + r) * HEAD_DIM, (g * GQA + r + 1) * HEAD_DIM)
                    qg_ref[r * T:(r + 1) * T, :] = q_ref[:, hs]
                    dog_ref[r * T:(r + 1) * T, :] = do_ref[:, hs]
                st_all = _dot_nt(kk, qg_ref[...])
                dpt_all = _dot_nt(vv, dog_ref[...])
                for r in range(GQA):
                    h = g * GQA + r
                    cs = slice(r * T, (r + 1) * T)
                    p, p_sink = _attn_probs(st_all[:, cs], dist, valid, slopes[h], sink_ref[h], 0)
                    dp = dpt_all[:, cs]
                    delta = jnp.sum(p * dp, axis=0, keepdims=True)
                    pt_ref[:, cs] = p.astype(BF16)
                    dst_ref[:, cs] = (p * (dp - delta)).astype(BF16)
                    dsink = dsink + jnp.where(lane == h, -jnp.sum(p_sink * delta), 0.0)
                dst = dst_ref[...]
                dks.append(_dot(dst, qg_ref[...]) * scale)
                dvs.append(_dot(pt_ref[...], dog_ref[...]))
                dq_all = _dot_tn(dst, kk) * scale
                for r in range(GQA):
                    hs = slice((g * GQA + r) * HEAD_DIM, (g * GQA + r + 1) * HEAD_DIM)
                    dq_ref[:, hs] = dq_all[r * T:(r + 1) * T, :].astype(BF16)
            dkv = jnp.concatenate(dks + dvs, axis=1)
            dsink_ref[...] += dsink
            dkv_ref[...] = carry_ref[...] + dkv[0:T, :]
            carry_ref[...] = dkv[T:2 * T, :]

        @pl.when(i == nb)
        def _():
            dkv_ref[...] = carry_ref[...]

    last = nb - 1
    return pl.pallas_call(
        body, name="swa_bwd", grid=(nb + 1,),
        in_specs=[pl.BlockSpec((T, AW), lambda i: (jnp.minimum(i, last), 0)),
                  pl.BlockSpec((T, KVW2), lambda i: (jnp.minimum(i, last), 0)),
                  pl.BlockSpec((T, KVW2), lambda i: (jnp.clip(i - 1, 0, last), 0)),
                  pl.BlockSpec((T, AW), lambda i: (jnp.minimum(i, last), 0)),
                  pl.BlockSpec(memory_space=pltpu.SMEM)],
        out_specs=(pl.BlockSpec((T, AW), lambda i: (jnp.minimum(i, last), 0)),
                   pl.BlockSpec((T, KVW2), lambda i: (jnp.maximum(i - 1, 0), 0)), _acc((1, LANE))),
        out_shape=(_S((L, AW), BF16), _S((L, KVW2), F32), _S((1, LANE), F32)),
        scratch_shapes=[pltpu.VMEM((T, KVW2), F32), pltpu.VMEM((GQA * T, HEAD_DIM), BF16),
                        pltpu.VMEM((GQA * T, HEAD_DIM), BF16), pltpu.VMEM((2 * T, GQA * T), BF16),
                        pltpu.VMEM((2 * T, GQA * T), BF16)],
        compiler_params=_params(("arbitrary",)))(q, kv, kv, do, sinks)


def _ssd_bwd(dyn, y, z, xbc, dt_raw, sprev, cw, cb, dtb, alog, dsk, nw, dims):
    L, CD = xbc.shape
    W, H, G, N = dims["W"], dims["H"], SSD_GROUPS, SSD_STATE
    T = CHUNK
    R = H // G
    GW = W // G
    nc = L // T
    HP = H * HEAD_DIM

    def body(dyn_ref, y_ref, z_ref, xbc_ref, prev_ref, dt_ref, sp_ref, cw_ref, cb_ref, dtb_ref, alog_ref, dsk_ref, nw_ref,
             dz_ref, dpre_ref, ddt_ref, acc_ref, hacc_ref, ext_ref, ds_ref, dtx_ref, acx_ref, xb_ref, dyb_ref, r12_ref,
             dx_ref, rows_ref):
        i = pl.program_id(0)

        @pl.when(i == 0)
        def _():
            ds_ref[...] = jnp.zeros_like(ds_ref)
            acc_ref[...] = jnp.zeros_like(acc_ref)
            hacc_ref[...] = jnp.zeros_like(hacc_ref)

        act, pre, spre = _conv_act(xbc_ref, prev_ref, cw_ref, cb_ref, ext_ref, i == nc - 1)
        xs = act[:, :W]
        dt_in = dt_ref[...] + dtb_ref[...]
        dt = _softplus(dt_in)
        a_neg = -jnp.exp(alog_ref[...])
        a = dt * a_neg
        low = _tri(T)
        upf = _tri(T, upper=True).astype(F32)
        acum = _dot_hi(low.astype(F32), a)
        acum_t = acum.T

        y = y_ref[...]
        zz = z_ref[...]
        sg = _sigmoid(zz)
        sz = zz * sg
        hh = y * sz
        dyn_v = dyn_ref[...]
        parts = []
        for g in range(G):
            gs = slice(g * GW, (g + 1) * GW)
            hg = hh[:, gs]
            hhat = hg * lax.rsqrt(_mean(hg * hg) + RMS_EPS)
            rg = lax.rsqrt(_mean(hg * hg) + RMS_EPS)
            acc_ref[0:1, gs] += _colsum(dyn_v[:, gs] * hhat)
            dhhat = dyn_v[:, gs] * nw_ref[:, gs]
            parts.append(rg * (dhhat - hhat * _mean(dhhat * hhat)))
        dhh = jnp.concatenate(parts, axis=1)
        dy = dhh * sz
        dz_ref[...] = (dhh * y * (sg * (1.0 + zz * (1.0 - sg)))).astype(BF16)
        acc_ref[1:2, :] += _colsum(dy * xs)
        dyb_ref[...] = dy.astype(BF16)

        _expand_heads(dtx_ref, dt, H)
        _expand_heads(acx_ref, acum, H)
        dtx = dtx_ref[...]
        acx = acx_ref[...]
        lastx = acx[T - 1:T, :]
        ex = jnp.exp(acx)
        decx = jnp.exp(lastx - acx)
        elx = jnp.exp(lastx)
        xd = xs * dtx
        xb_ref[...] = xd.astype(BF16)
        xdecb = (xd * decx).astype(BF16)
        dgb = (ex * dy).astype(BF16)
        rows_ref[...] = jnp.zeros_like(rows_ref)

        lane = lax.broadcasted_iota(jnp.int32, (T, LANE), 1)
        sub = lax.broadcasted_iota(jnp.int32, (T, LANE), 0)
        subr = lax.broadcasted_iota(jnp.int32, (LANE, T), 0)
        da_col = jnp.zeros((T, LANE), F32)
        da_row = jnp.zeros((LANE, T), F32)
        dbs, dcs = [], []
        for g in range(G):
            gs = slice(g * GW, (g + 1) * GW)
            bgb = act[:, W + g * N:W + (g + 1) * N].astype(BF16)
            cgb = act[:, W + G * N + g * N:W + G * N + (g + 1) * N].astype(BF16)
            stg = sp_ref[0, :, gs]
            stb = stg.astype(BF16)
            dsn = ds_ref[:, gs]
            dsnb = dsn.astype(BF16)
            gm = _dot(cgb, stb)
            dc = _dot_nt(dgb[:, gs], stb)
            dsp = _dot_tn(cgb, dgb[:, gs])
            dxs_ = decx[:, gs] * _dot(bgb, dsnb)
            db = _dot_nt(xdecb[:, gs], dsnb)
            xdg = xd[:, gs]
            r12_ref[:, gs] = dy[:, gs] * ex[:, gs] * gm - xdg * dxs_
            rows_ref[0:1, gs] = _colsum(dsn * stg) * elx[:, gs]
            rows_ref[1:2, gs] = _colsum(xdg * dxs_)
            ds_ref[:, gs] = dsp + dsn * elx[:, gs]
            cb_g = _dot_nt(cgb, bgb)
            dcb = jnp.zeros((T, T), F32)
            for r in range(R):
                h = g * R + r
                hs = slice(h * HEAD_DIM, (h + 1) * HEAD_DIM)
                lm = jnp.where(low, jnp.exp(acum[:, h:h + 1] - acum_t[h:h + 1, :]), 0.0)
                mm = cb_g * lm
                dyb = dyb_ref[:, hs]
                dm = _dot_nt(dyb, xb_ref[:, hs])
                dx_ref[:, hs] = dxs_[:, r * HEAD_DIM:(r + 1) * HEAD_DIM] + _dot_tn(mm.astype(BF16), dyb)
                dcb = dcb + dm * lm
                qm = dm * mm
                da_col = jnp.where(lane == h, jnp.sum(qm, axis=1, keepdims=True), da_col)
                da_row = jnp.where(subr == h, jnp.sum(qm, axis=0, keepdims=True), da_row)
            dcbb = dcb.astype(BF16)
            dcs.append(dc + _dot(dcbb, bgb))
            dbs.append(db + _dot_tn(dcbb, cgb))
        dx = dx_ref[...]
        rows = _head_reduce(rows_ref[...])
        dlast = rows[0:1, :] + rows[1:2, :]
        da_col = da_col + _head_reduce(r12_ref[...]) + jnp.where(sub == T - 1, dlast, 0.0)
        dacum = da_col - da_row.T
        da = _dot_hi(upf, dacum)
        ddt = _head_reduce(dx * xs) + da * a_neg
        hacc_ref[1:2, :] += _colsum(da * dt) * a_neg
        ddt_raw = ddt * _sigmoid(dt_in)
        hacc_ref[0:1, :] += _colsum(ddt_raw)
        ddt_ref[...] = ddt_raw
        dact = jnp.concatenate([dsk_ref[...] * dy + dx * dtx] + dbs + dcs, axis=1)
        dpre_ref[...] = dact * (spre * (1.0 + pre * (1.0 - spre)))

        @pl.when(i == nc - 1)
        def _():
            ch = lax.broadcasted_iota(jnp.int32, (W, LANE), 0)
            lo = lax.broadcasted_iota(jnp.int32, (W, LANE), 1) * HEAD_DIM
            hacc_ref[2:3, :] = _dot_hi(acc_ref[1:2, :], ((ch >= lo) & (ch < lo + HEAD_DIM)).astype(F32))

    rev = lambda i: (nc - 1 - i, 0)
    return pl.pallas_call(
        body, name="ssd_bwd", grid=(nc,),
        in_specs=[pl.BlockSpec((T, W), rev), pl.BlockSpec((T, W), rev), pl.BlockSpec((T, W), rev), pl.BlockSpec((T, CD), rev),
                  pl.BlockSpec((HALO, CD), lambda i: (jnp.maximum((nc - 1 - i) * (T // HALO) - 1, 0), 0)),
                  pl.BlockSpec((T, LANE), rev), pl.BlockSpec((1, N, HP), lambda i: (nc - 1 - i, 0, 0)),
                  _acc((CONV_K, CD)), _acc((1, CD)), _acc((1, LANE)), _acc((1, LANE)), _acc((1, W)), _acc((1, W))],
        out_specs=(pl.BlockSpec((T, W), rev), pl.BlockSpec((T, CD), rev), pl.BlockSpec((T, LANE), rev), _acc((8, W)),
                   _acc((8, LANE))),
        out_shape=(_S((L, W), BF16), _S((L, CD), F32), _S((L, LANE), F32), _S((8, W), F32), _S((8, LANE), F32)),
        scratch_shapes=[pltpu.VMEM((T + HALO, CD), F32), pltpu.VMEM((N, HP), F32), pltpu.VMEM((T, W), F32),
                        pltpu.VMEM((T, W), F32), pltpu.VMEM((T, W), BF16), pltpu.VMEM((T, W), BF16), pltpu.VMEM((T, W), F32),
                        pltpu.VMEM((T, W), F32), pltpu.VMEM((8, W), F32)],
        compiler_params=_params(("arbitrary",)))(dyn, y, z, xbc, xbc, dt_raw, sprev, cw, cb, dtb, alog, dsk, nw)


def _conv_bwd(dpre, xbc, cw):
    L, CD = xbc.shape
    tm = _pick(L, (CONV_TM, 128))
    cb = CD if CONV_CB >= CD else _pick(CD, (CONV_CB, 128))
    nt = L // tm
    hb = tm // HALO

    def body(dp_ref, dn_ref, u_ref, cw_ref, du_ref, acc_ref, extd_ref):
        i = pl.program_id(1)

        @pl.when(i == 0)
        def _():
            acc_ref[...] = jnp.zeros_like(acc_ref)

        extd_ref[0:tm, :] = dp_ref[...]
        extd_ref[tm:tm + HALO, :] = jnp.where(i == nt - 1, 0.0, dn_ref[...])
        for c in range(tm // ROW_CHUNK):
            r0 = c * ROW_CHUNK
            rows = slice(r0, r0 + ROW_CHUNK)
            dp = dp_ref[rows, :]
            u = u_ref[rows, :]
            du = cw_ref[CONV_K - 1:CONV_K, :] * dp
            acc_ref[CONV_K - 1:CONV_K, :] += _colsum(dp * u)
            for k in range(CONV_K - 1):
                s = CONV_K - 1 - k
                dsh = extd_ref[r0 + s:r0 + s + ROW_CHUNK, :]
                du = du + cw_ref[k:k + 1, :] * dsh
                acc_ref[k:k + 1, :] += _colsum(u * dsh)
            acc_ref[CONV_K:CONV_K + 1, :] += _colsum(dp)
            du_ref[rows, :] = du.astype(BF16)

    tile = pl.BlockSpec((tm, cb), lambda j, i: (i, j))
    return pl.pallas_call(
        body, name="conv_bwd", grid=(CD // cb, nt),
        in_specs=[tile, pl.BlockSpec((HALO, cb), lambda j, i: (jnp.minimum((i + 1) * hb, nt * hb - 1), j)),
                  tile, pl.BlockSpec((CONV_K, cb), lambda j, i: (0, j))],
        out_specs=(tile, pl.BlockSpec((8, cb), lambda j, i: (0, j))),
        out_shape=(_S((L, CD), BF16), _S((8, CD), F32)),
        scratch_shapes=[pltpu.VMEM((tm + HALO, cb), F32)],
        compiler_params=_params(("arbitrary", "arbitrary")))(dpre, dpre, xbc, cw)


def _in_proj_bwd(dz, dxbc, dq, dkv, ddt, w_t, xhat0, rstd0, dh0p, vecs, dims):
    L, D = xhat0.shape
    W, CD, AW, KVW2 = dims["W"], dims["CD"], dims["AW"], dims["KVW2"]
    PROJ = w_t.shape[0]
    tm = _pick(L, (MLP_TM, 128))
    r_z, r_xbc, r_dt, r_q, r_kv = _proj_rows(dims)

    def body(dz_ref, dxbc_ref, dq_ref, dkv_ref, ddt_ref, w_ref, xh_ref, rs_ref, dh0_ref, v_ref, gx_ref, acc_ref):
        @pl.when(pl.program_id(0) == 0)
        def _():
            acc_ref[...] = jnp.zeros_like(acc_ref)

        du1 = _dot(dz_ref[...], w_ref[r_z[0]:r_z[1], :])
        du1 = du1 + _dot(dxbc_ref[...], w_ref[r_xbc[0]:r_xbc[1], :])
        du1 = du1 + _dot(dq_ref[...], w_ref[r_q[0]:r_q[1], :])
        du1 = du1 + _dot(dkv_ref[...].astype(BF16), w_ref[r_kv[0]:r_kv[1], :])
        du1 = du1 + _dot(ddt_ref[...].astype(BF16), w_ref[r_dt[0]:r_dt[1], :])
        xhat0 = xh_ref[...]
        h0 = xhat0 * v_ref[0:1, :] + v_ref[1:2, :]
        acc_ref[0:1, :] += _colsum(du1 * h0)
        acc_ref[1:2, :] += _colsum(du1)
        dh0 = dh0_ref[...] + du1 * (1.0 + v_ref[2:3, :])
        acc_ref[2:3, :] += _colsum(dh0 * xhat0)
        acc_ref[3:4, :] += _colsum(dh0)
        gx_ref[...] = _ln_bwd(dh0 * v_ref[0:1, :], xhat0, rs_ref[...])

    return pl.pallas_call(
        body, name="in_proj_bwd", grid=(L // tm,),
        in_specs=[_tile(W, tm), _tile(CD, tm), _tile(AW, tm), _tile(KVW2, tm), _tile(LANE, tm), _res((PROJ, D)),
                  _tile(D, tm), _tile(1, tm), _tile(D, tm), _acc((8, D))],
        out_specs=(_tile(D, tm), _acc((8, D))),
        out_shape=(_S((L, D), F32), _S((8, D), F32)),
        compiler_params=_params(("arbitrary",)))(dz, dxbc, dq, dkv, ddt, w_t, xhat0, rstd0, dh0p, vecs)


_WEIGHTS = ['ln_in_g', 'ln_in_b', 'ada_w', 'ada_b', 'w_in', 'conv_w', 'conv_b', 'dt_bias', 'a_log', 'd_skip', 'ssd_norm_w',
            'attn_sinks', 'w_out', 'ln1_g', 'ln1_b', 'w_ff1', 'b_ff1', 'w_ff2', 'b_ff2', 'ln2_g', 'ln2_b']
_BIG = ('w_in', 'w_out', 'w_ff1', 'w_ff2')
_SMALL = ('ada_b', 'ln_in_g', 'ln_in_b', 'conv_b', 'dt_bias', 'a_log', 'd_skip', 'ssd_norm_w', 'attn_sinks', 'ln1_g', 'ln1_b',
          'b_ff1', 'b_ff2', 'ln2_g', 'ln2_b')


def _pad_lanes(v, n=None):
    v = v.reshape(1, -1)
    n = n or -(-v.shape[1] // LANE) * LANE
    return jnp.pad(v, ((0, 0), (0, n - v.shape[1])))


def _vec8(rows, D):
    rows = [r.reshape(1, D) for r in rows]
    return jnp.concatenate(rows + [jnp.zeros((8 - len(rows), D), F32)], axis=0)


def _pack(segs):
    flat, offs, sizes, o = [], [], [], 0
    for s in segs:
        p = _pad_lanes(s)
        flat.append(p)
        offs.append(o)
        sizes.append(s.size)
        o += p.shape[1]
    total = -(-o // (8 * LANE)) * (8 * LANE)
    if total > o:
        flat.append(jnp.zeros((1, total - o), F32))
    return jnp.concatenate(flat, axis=1).reshape(8, total // 8), offs, sizes


def kernel(x, c, ln_in_g, ln_in_b, ada_w, ada_b, w_in, conv_w, conv_b, dt_bias, a_log, d_skip, ssd_norm_w, attn_sinks, w_out, ln1_g, ln1_b, w_ff1, b_ff1, w_ff2, b_ff2, ln2_g, ln2_b, loss_target, m_ln_in_g, m_ln_in_b, m_ada_w, m_ada_b, m_w_in, m_conv_w, m_conv_b, m_dt_bias, m_a_log, m_d_skip, m_ssd_norm_w, m_attn_sinks, m_w_out, m_ln1_g, m_ln1_b, m_w_ff1, m_b_ff1, m_w_ff2, m_b_ff2, m_ln2_g, m_ln2_b, v_ln_in_g, v_ln_in_b, v_ada_w, v_ada_b, v_w_in, v_conv_w, v_conv_b, v_dt_bias, v_a_log, v_d_skip, v_ssd_norm_w, v_attn_sinks, v_w_out, v_ln1_g, v_ln1_b, v_w_ff1, v_b_ff1, v_w_ff2, v_b_ff2, v_ln2_g, v_ln2_b):
    wts = dict(ln_in_g=ln_in_g, ln_in_b=ln_in_b, ada_w=ada_w, ada_b=ada_b, w_in=w_in, conv_w=conv_w, conv_b=conv_b,
               dt_bias=dt_bias, a_log=a_log, d_skip=d_skip, ssd_norm_w=ssd_norm_w, attn_sinks=attn_sinks, w_out=w_out,
               ln1_g=ln1_g, ln1_b=ln1_b, w_ff1=w_ff1, b_ff1=b_ff1, w_ff2=w_ff2, b_ff2=b_ff2, ln2_g=ln2_g, ln2_b=ln2_b)
    ms = dict(ln_in_g=m_ln_in_g, ln_in_b=m_ln_in_b, ada_w=m_ada_w, ada_b=m_ada_b, w_in=m_w_in, conv_w=m_conv_w,
              conv_b=m_conv_b, dt_bias=m_dt_bias, a_log=m_a_log, d_skip=m_d_skip, ssd_norm_w=m_ssd_norm_w,
              attn_sinks=m_attn_sinks, w_out=m_w_out, ln1_g=m_ln1_g, ln1_b=m_ln1_b, w_ff1=m_w_ff1, b_ff1=m_b_ff1,
              w_ff2=m_w_ff2, b_ff2=m_b_ff2, ln2_g=m_ln2_g, ln2_b=m_ln2_b)
    vs = dict(ln_in_g=v_ln_in_g, ln_in_b=v_ln_in_b, ada_w=v_ada_w, ada_b=v_ada_b, w_in=v_w_in, conv_w=v_conv_w,
              conv_b=v_conv_b, dt_bias=v_dt_bias, a_log=v_a_log, d_skip=v_d_skip, ssd_norm_w=v_ssd_norm_w,
              attn_sinks=v_attn_sinks, w_out=v_w_out, ln1_g=v_ln1_g, ln1_b=v_ln1_b, w_ff1=v_w_ff1, b_ff1=v_b_ff1,
              w_ff2=v_w_ff2, b_ff2=v_b_ff2, ln2_g=v_ln2_g, ln2_b=v_ln2_b)

    L, D = x.shape[1], x.shape[2]
    depth = w_in.shape[0]
    assert depth == 1 and x.shape[0] == 1 and L % CHUNK == 0
    W = D
    H = W // HEAD_DIM
    CD = W + 2 * SSD_GROUPS * SSD_STATE
    AW = D
    AH = AW // HEAD_DIM
    KV = AH // GQA
    KVW2 = 2 * KV * HEAD_DIM
    PROJ = W + CD + H + AW + KVW2
    FF = w_ff1.shape[2] * N_DEV
    MIX = w_out.shape[1] * N_DEV
    assert w_in.shape[2] * N_DEV == PROJ and MIX == W + AW and H <= LANE and AH <= LANE
    dims = dict(W=W, H=H, CD=CD, AW=AW, AH=AH, KV=KV, KVW2=KVW2)
    alpha = (2.0 * depth) ** 0.25
    C6 = ada_w.shape[2]
    CW = conv_w.shape[2]

    ax, ay, ac = _my_pos()
    me = 4 * ax + 2 * ay + ac
    x2 = x.reshape(L, D)
    tgt = loss_target.reshape(L, D)
    r1 = lambda a: a.reshape(1, -1)

    ada_b_cols = lax.dynamic_slice(ada_b, (0, me * C6), (1, C6))
    cs_all, mod = _mod_fwd(c, ada_w[0], ada_b_cols)
    sh1, sc1, g1, sh2, sc2, g2 = [r1(t) for t in jnp.split(mod.reshape(-1), 6)]

    wg_in, cwg = _ag_weights([w_in[0].T.astype(BF16), conv_w[0]], cs_all)
    shards2 = [w_out[0].astype(BF16), w_ff1[0].astype(BF16), w_ff2[0].astype(BF16)]
    lands2 = [lax.dynamic_update_slice(lax.empty((N_DEV,) + s.shape, s.dtype), s[None], (me, 0, 0)) for s in shards2]
    ag_ss, ag_rs, ag_arr, ag_token = _split_start(shards2 + lands2, _plan_gather(3), cwg, "ag_ici_start")
    sh1 = sh1 + ag_token[0:1, 0:1]
    w_pad = _merge_blocks(wg_in)
    cw_full = cwg.transpose(1, 0, 2).reshape(CONV_K, CD)

    dtb = _pad_lanes(dt_bias, LANE)
    alog = _pad_lanes(a_log, LANE)
    dsk = jnp.repeat(d_skip.reshape(-1), HEAD_DIM).reshape(1, W)
    sinks = attn_sinks.reshape(-1)
    g_in, b_in = r1(ln_in_g), r1(ln_in_b)

    xhat0, rstd0, u1, z, xbc, q, kv, dt_raw = _ln_in_proj(x2, g_in, b_in, sc1, sh1, w_pad, dims)
    y, yn, sprev = _conv_ssd(xbc, dt_raw, z, cw_full, conv_b, dtb, alog, dsk, ssd_norm_w, dims)
    ag_arr = _split_wait(ag_ss, ag_rs, ag_arr, _plan_gather(3), yn, "ag_ici_wait")
    fw_ss, fw_rs, ag_land, fw_token = _split_start(ag_arr[3:], _plan_forward(3), yn, "ag_fwd_start")
    o = _swa_fwd(q, kv, sinks + fw_token[0, 0], dims)
    wg_out, wg_ff1, wg_ff2 = _split_wait(fw_ss, fw_rs, ag_land, _plan_forward(3), o, "ag_fwd_wait")
    w_out_full = wg_out.reshape(MIX, D)
    w1_full = wg_ff1.transpose(1, 0, 2).reshape(D, FF)
    w2_full = wg_ff2.reshape(FF, D)
    mix, xhat1, rstd1, u2 = _out_proj_ln1(yn, o, w_out_full, xhat0, _vec8([g_in, b_in, g1, ln1_g, ln1_b, sc2, sh2], D), alpha)
    rr, dr2, acc_f, loss_loc = _mlp_loss(u2, w1_full, w2_full, xhat1, tgt,
                                         _vec8([ln1_g, ln1_b, g2, ln2_g, ln2_b, b_ff2], D), b_ff1, alpha)

    df, da, gb2, gb1 = _mlp_bwd_a(dr2, rr, w2_full, g2)
    gw_ff2 = _matmul_tn(rr, df, "gw_ff2", square_a=True)
    gw_ff1t = _matmul_tn(da, u2, "gw_ff1")
    dmix, dh0p, dyn, do, acc_b = _mlp_bwd_b(da, w1_full, dr2, xhat1, rstd1, mix, w_out_full,
                                            _vec8([ln1_g, ln1_b, sc2, g1], D), alpha, W)
    gw_out = jnp.concatenate([_matmul_tn(yn, dmix, "gw_out_ssd"), _matmul_tn(o, dmix, "gw_out_attn")], axis=0)

    core = jnp.reshape(ac, (1,)).astype(jnp.int32)
    blocked1 = [gw_out.reshape(N_DEV, MIX // N_DEV, D), gw_ff1t.reshape(N_DEV, FF // N_DEV, D),
                gw_ff2.reshape(N_DEV, FF // N_DEV, D)]
    lands1 = [lax.empty(b.shape, b.dtype) for b in blocked1]
    rs_ss, rs_rs, rs_arr, rs_token = _split_start(blocked1 + lands1, _plan_scatter_all(3), do, "rs_all_start")
    dq, dkv, dsink = _swa_bwd(q, kv, do, sinks + rs_token[0, 0], dims)
    dz, dpre, ddt, acc_s, hacc = _ssd_bwd(dyn, y, z, xbc, dt_raw, sprev, cw_full, conv_b, dtb + rs_token[0:1, 0:1], alog, dsk,
                                          ssd_norm_w, dims)
    dxbc, acc_c = _conv_bwd(dpre, xbc, cw_full)
    gw_in = _gw_in((dz, dxbc, ddt, dq, dkv), u1, dims)

    blocked2 = [_split_blocks(gw_in, N_DEV)]
    pairs2 = [_pair_sum(b, r, core) for b, r in zip(blocked2, _rs_d2d(blocked2, "rs_d2d_2"))]
    lands2 = [lax.empty(p.shape, p.dtype) for p in pairs2]
    r2_ss, r2_rs, r2_arr, r2_token = _split_start(pairs2 + lands2, _plan_scatter(1), gw_in, "rs_ici_start_2")
    grad_x, acc_i = _in_proj_bwd(dz, dxbc, dq, dkv, ddt, w_pad, xhat0, rstd0, dh0p,
                                 _vec8([g_in, b_in, sc1], D) + r2_token[0:1, 0:1], dims)

    srcs = [acc_i, acc_b, acc_f, acc_s, acc_c, hacc, dsink, gb1, gb2, loss_loc]
    I_, B_, F_, S_, C_, H_, K_, G1_, G2_, L_ = range(10)
    seg_of = dict(ada_b=[(I_, 1, D), (I_, 0, D), (B_, 4, D), (B_, 1, D), (B_, 0, D), (F_, 2, D)],
                  ln_in_g=[(I_, 2, D)], ln_in_b=[(I_, 3, D)], conv_b=[(C_, CONV_K, CD)], dt_bias=[(H_, 0, H)],
                  a_log=[(H_, 1, H)], d_skip=[(H_, 2, H)], ssd_norm_w=[(S_, 0, W)], attn_sinks=[(K_, 0, AH)],
                  ln1_g=[(B_, 2, D)], ln1_b=[(B_, 3, D)], b_ff1=[(G1_, 0, FF)], b_ff2=[(G2_, 0, D)],
                  ln2_g=[(F_, 0, D)], ln2_b=[(F_, 1, D)])
    pieces = [seg_of[n] for n in _SMALL] + [[(C_, t, CD) for t in range(CONV_K)], [(L_, 0, 1)]]
    params = [tuple(t[n].reshape(1, -1) for t in (wts, ms, vs)) for n in _SMALL]
    res = _small_sync_adamw(srcs, pieces, params, 6 * D)
    grads, deltas, new_m, new_v = {}, {}, {}, {}
    for k, n in enumerate(_SMALL):
        grads[n], deltas[n], new_m[n], new_v[n] = (t.reshape(wts[n].shape) for t in res[4 * k:4 * k + 4])
    gcw_full, dmod_all, loss_row = res[4 * len(_SMALL):]
    loss = loss_row[0, 0]

    g_ = lax.dynamic_slice(gcw_full, (0, me * CW), (CONV_K, CW))
    d_, m_, v_ = _adamw(conv_w[0], g_, m_conv_w[0], v_conv_w[0])
    grads['conv_w'], deltas['conv_w'], new_m['conv_w'], new_v['conv_w'] = (t[None] for t in (g_, d_, m_, v_))

    dmod_cols = lax.dynamic_slice(dmod_all, (0, me * C6), (N_DEV, C6))
    pad16 = lambda t: jnp.concatenate([t, jnp.zeros((16 - N_DEV,) + t.shape[1:], t.dtype)], axis=0)
    g_, d_, m_, v_ = _ada_grad_adamw(pad16(cs_all), pad16(dmod_cols), ada_w[0], m_ada_w[0], v_ada_w[0])
    grads['ada_w'], deltas['ada_w'], new_m['ada_w'], new_v['ada_w'] = (t[None] for t in (g_, d_, m_, v_))

    rs_arr = _split_wait(rs_ss, rs_rs, rs_arr, _plan_scatter_all(3), g_, "rs_all_wait")
    mychip = 2 * ax + ay
    chips = jnp.stack([(mychip + k) % N_CHIP for k in range(N_CHIP)]).astype(jnp.int32)
    devs = jnp.stack([(me + k) % N_DEV for k in range(N_DEV)]).astype(jnp.int32)
    for n, own, land in zip(('w_out', 'w_ff1', 'w_ff2'), rs_arr[:3], rs_arr[3:]):
        g_, d_, m_, v_ = _sum_adamw_split(own, land, devs, wts[n][0], ms[n][0], vs[n][0], transposed=(n == 'w_ff1'))
        grads[n], deltas[n], new_m[n], new_v[n] = (t[None] for t in (g_, d_, m_, v_))
    r2_arr = _split_wait(r2_ss, r2_rs, r2_arr, _plan_scatter(1), g_, "rs_ici_wait_2")
    g_, d_, m_, v_ = _sum_adamw_split(r2_arr[0], r2_arr[1], chips, wts['w_in'][0].T, ms['w_in'][0].T, vs['w_in'][0].T)
    grads['w_in'], deltas['w_in'], new_m['w_in'], new_v['w_in'] = (t.T[None] for t in (g_, d_, m_, v_))

    return (loss, grad_x.reshape(x.shape), *[grads[n] for n in _WEIGHTS], *[deltas[n] for n in _WEIGHTS],
            *[new_m[n] for n in _WEIGHTS], *[new_v[n] for n in _WEIGHTS])
```

```python
import functools
import math

import numpy as np
import jax
import jax.numpy as jnp
from jax import lax
from jax.experimental import pallas as pl
from jax.experimental.pallas import tpu as pltpu

F32 = jnp.float32
BF16 = jnp.bfloat16
MESH = pl.DeviceIdType.MESH

N_DEV = 8
N_CHIP = 4
HEAD_DIM = 64
SSD_GROUPS = 2
SSD_STATE = 128
CHUNK = 128
CONV_K = 4
GQA = 8
LANE = 128
HALO = 8
LN_EPS = 1e-5
RMS_EPS = 1e-5
NEG = -1e30
ADAM_LR, ADAM_B1, ADAM_B2, ADAM_EPS, ADAM_WD, ADAM_STEP = 0.001, 0.9, 0.999, 1e-08, 0.01, 10
V7X_VMEM_BYTES = 64 * 1024 * 1024
VMEM_LIMIT = V7X_VMEM_BYTES - 8 * 1024 * 1024
HI = lax.Precision.HIGHEST
MLP_TM = 512
MLP_SUB = 512
MLP_FC = 512
CONV_TM = 512
CONV_CB = 2048
ROW_CHUNK = 32


def _alibi_slopes(n):
    def pow2(m):
        start = 2.0 ** (-8.0 / m)
        return [start ** (i + 1) for i in range(m)]
    if math.log2(n).is_integer():
        s = pow2(n)
    else:
        c = 2 ** math.floor(math.log2(n))
        s = pow2(c) + pow2(2 * c)[0::2][: n - c]
    return [float(v) for v in np.array(s, dtype=np.float32)]


def _dot(a, b):
    return jnp.dot(a, b, preferred_element_type=F32)


def _dot_nt(a, b):
    return lax.dot_general(a, b, (((1,), (1,)), ((), ())), preferred_element_type=F32)


def _dot_tn(a, b):
    return lax.dot_general(a, b, (((0,), (0,)), ((), ())), preferred_element_type=F32)


def _dot_hi(a, b):
    return jnp.dot(a, b, precision=HI, preferred_element_type=F32)


def _sigmoid(x):
    return 0.5 * jnp.tanh(0.5 * x) + 0.5


def _softplus(x):
    return jnp.maximum(x, 0.0) + jnp.log(1.0 + jnp.exp(-jnp.abs(x)))


def _mean(x):
    return jnp.mean(x, axis=-1, keepdims=True)


def _ln_fwd(x):
    xc = x - _mean(x)
    rstd = lax.rsqrt(_mean(xc * xc) + LN_EPS)
    return xc * rstd, rstd


def _ln_bwd(dxhat, xhat, rstd):
    return rstd * (dxhat - _mean(dxhat) - xhat * _mean(dxhat * xhat))


def _colsum(x):
    return jnp.sum(x, axis=0, keepdims=True)


def _params(sem):
    return pltpu.CompilerParams(dimension_semantics=sem, vmem_limit_bytes=VMEM_LIMIT)


def _tile(i_map_cols, tm):
    return pl.BlockSpec((tm, i_map_cols), lambda i: (i, 0))


def _res(shape):
    return pl.BlockSpec(shape, lambda *_: (0,) * len(shape), pipeline_mode=pl.Buffered(1))


def _acc(shape):
    return pl.BlockSpec(shape, lambda *_: (0,) * len(shape))


def _S(shape, dtype):
    return jax.ShapeDtypeStruct(shape, dtype)


def _my_pos():
    return lax.axis_index("x"), lax.axis_index("y"), lax.axis_index("c")


def _peer(pos, k):
    x, y, c = pos
    px = 1 - x if k & 4 else x
    py = 1 - y if k & 2 else y
    pc = 1 - c if k & 1 else c
    return (px, py, pc)


def _lin(p):
    return 4 * p[0] + 2 * p[1] + p[2]


def _mod_fwd(c_loc, ada_w_loc, ada_b_cols):
    D = c_loc.shape[1]
    C6 = ada_w_loc.shape[1]

    def body(c_ref, w_ref, b_ref, cs_ref, mod_ref, call_ref, modp_ref, ssem, rsem):
        pos = _my_pos()
        me = _lin(pos)
        call_ref[me] = c_ref[...]
        sends = []
        for k in range(1, N_DEV):
            cp = pltpu.make_async_remote_copy(src_ref=c_ref, dst_ref=call_ref.at[me], send_sem=ssem.at[k - 1],
                                              recv_sem=rsem.at[k - 1], device_id=_peer(pos, k), device_id_type=MESH)
            cp.start()
            sends.append(cp)
        for k in range(1, N_DEV):
            src = _lin(_peer(pos, k))
            pltpu.make_async_remote_copy(src_ref=c_ref, dst_ref=call_ref.at[src], send_sem=ssem.at[k - 1],
                                         recv_sem=rsem.at[k - 1], device_id=pos, device_id_type=MESH).wait_recv()
        for cp in sends:
            cp.wait_send()
        call = jnp.concatenate([call_ref[b] for b in range(N_DEV)], axis=0)
        cs = call * _sigmoid(call)
        cs_ref[...] = cs
        modp = _dot(cs.astype(BF16), w_ref[...].astype(BF16)) + b_ref[...]
        for b in range(N_DEV):
            modp_ref[b] = modp[b:b + 1, :]
        mod_ref[me] = modp_ref[me]
        sends = []
        for k in range(1, N_DEV):
            peer = _peer(pos, k)
            cp = pltpu.make_async_remote_copy(src_ref=modp_ref.at[_lin(peer)], dst_ref=mod_ref.at[me],
                                              send_sem=ssem.at[N_DEV - 2 + k], recv_sem=rsem.at[N_DEV - 2 + k],
                                              device_id=peer, device_id_type=MESH)
            cp.start()
            sends.append(cp)
        for k in range(1, N_DEV):
            src = _lin(_peer(pos, k))
            pltpu.make_async_remote_copy(src_ref=modp_ref.at[src], dst_ref=mod_ref.at[src],
                                         send_sem=ssem.at[N_DEV - 2 + k], recv_sem=rsem.at[N_DEV - 2 + k],
                                         device_id=pos, device_id_type=MESH).wait_recv()
        for cp in sends:
            cp.wait_send()

    vm = pl.BlockSpec(memory_space=pltpu.VMEM)
    return pl.pallas_call(
        body, name="mod_fwd",
        out_shape=(_S((N_DEV, D), F32), _S((N_DEV, 1, C6), F32)),
        in_specs=[vm, vm, vm], out_specs=(vm, vm),
        scratch_shapes=[pltpu.VMEM((N_DEV, 1, D), F32), pltpu.VMEM((N_DEV, 1, C6), F32),
                        pltpu.SemaphoreType.DMA((2 * (N_DEV - 1),)), pltpu.SemaphoreType.DMA((2 * (N_DEV - 1),))],
        compiler_params=pltpu.CompilerParams(vmem_limit_bytes=VMEM_LIMIT),
    )(c_loc, ada_w_loc, ada_b_cols)


def _small_gather_sum(pack):
    P8 = pack.shape[1]

    def body(p_ref, gat_ref, sum_ref, ssem, rsem):
        pos = _my_pos()
        me = _lin(pos)
        gat_ref[me] = p_ref[...]
        sends = []
        for k in range(1, N_DEV):
            cp = pltpu.make_async_remote_copy(src_ref=p_ref, dst_ref=gat_ref.at[me], send_sem=ssem.at[k - 1],
                                              recv_sem=rsem.at[k - 1], device_id=_peer(pos, k), device_id_type=MESH)
            cp.start()
            sends.append(cp)
        for k in range(1, N_DEV):
            src = _lin(_peer(pos, k))
            pltpu.make_async_remote_copy(src_ref=p_ref, dst_ref=gat_ref.at[src], send_sem=ssem.at[k - 1],
                                         recv_sem=rsem.at[k - 1], device_id=pos, device_id_type=MESH).wait_recv()
        for cp in sends:
            cp.wait_send()
        acc = gat_ref[0]
        for j in range(1, N_DEV):
            acc = acc + gat_ref[j]
        sum_ref[...] = acc

    vm = pl.BlockSpec(memory_space=pltpu.VMEM)
    return pl.pallas_call(
        body, name="small_gather_sum",
        out_shape=(_S((N_DEV, 8, P8), F32), _S((8, P8), F32)),
        in_specs=[vm], out_specs=(vm, vm),
        scratch_shapes=[pltpu.SemaphoreType.DMA((N_DEV - 1,)), pltpu.SemaphoreType.DMA((N_DEV - 1,))],
        compiler_params=pltpu.CompilerParams(vmem_limit_bytes=VMEM_LIMIT),
    )(pack)


def _small_sync_adamw(srcs, pieces, params, n_mod):
    n_src, n_par = len(srcs), len(params)
    rows_of = [sum(-(-w // LANE) for _, _, w in seg) for seg in pieces]
    starts = [sum(rows_of[:k]) for k in range(len(pieces))]
    NR = -(-sum(rows_of) // 8) * 8
    cd = pieces[n_par][0][2]

    def seg_row(arr, k, width):
        r = starts[k]
        if width <= LANE:
            return arr[r:r + 1, 0:width]
        return jnp.concatenate([arr[r + q:r + q + 1, :] for q in range(width // LANE)], axis=1)

    def exchange(*refs):
        src = refs[:n_src]
        total_ref, dmod_ref, pack_ref, gat_ref, ssem, rsem = refs[n_src:]
        pos = _my_pos()
        me = _lin(pos)
        pack_ref[...] = jnp.zeros_like(pack_ref)
        for k, seg in enumerate(pieces):
            r = starts[k]
            for (si, row, width) in seg:
                for q in range(-(-width // LANE)):
                    wq = min(LANE, width - q * LANE)
                    pack_ref[r:r + 1, 0:wq] = src[si][row:row + 1, q * LANE:q * LANE + wq]
                    r += 1
        gat_ref[me] = pack_ref[...]
        sends = []
        for k in range(1, N_DEV):
            cp = pltpu.make_async_remote_copy(src_ref=pack_ref, dst_ref=gat_ref.at[me], send_sem=ssem.at[k - 1],
                                              recv_sem=rsem.at[k - 1], device_id=_peer(pos, k), device_id_type=MESH)
            cp.start()
            sends.append(cp)
        for k in range(1, N_DEV):
            frm = _lin(_peer(pos, k))
            pltpu.make_async_remote_copy(src_ref=pack_ref, dst_ref=gat_ref.at[frm], send_sem=ssem.at[k - 1],
                                         recv_sem=rsem.at[k - 1], device_id=pos, device_id_type=MESH).wait_recv()
        for cp in sends:
            cp.wait_send()
        total = gat_ref[0]
        for j in range(1, N_DEV):
            total = total + gat_ref[j]
        total_ref[...] = total
        for j in range(N_DEV):
            dmod_ref[j:j + 1, :] = seg_row(gat_ref[j], 0, n_mod)

    def update(*refs):
        total = refs[0][...]
        wmv = refs[1:1 + 3 * n_par]
        outs = refs[1 + 3 * n_par:]
        for k in range(n_par):
            n = params[k][0].shape[1]
            g = seg_row(total, k, n)
            w_ref, m_ref, v_ref = wmv[3 * k:3 * k + 3]
            g_ref, d_ref, m2_ref, v2_ref = outs[4 * k:4 * k + 4]
            g_ref[...] = g
            d_ref[...], m2_ref[...], v2_ref[...] = _adamw_math(w_ref[...], g, m_ref[...], v_ref[...])
        gcw_ref, loss_ref = outs[4 * n_par:]
        for t in range(CONV_K):
            r = starts[n_par] + t * (cd // LANE)
            gcw_ref[t:t + 1, :] = jnp.concatenate([total[r + q:r + q + 1, :] for q in range(cd // LANE)], axis=1)
        loss_ref[...] = total[starts[n_par + 1]:starts[n_par + 1] + 1, :]

    vm = pl.BlockSpec(memory_space=pltpu.VMEM)
    total, dmod_all = pl.pallas_call(
        exchange, name="small_sync", out_shape=(_S((NR, LANE), F32), _S((N_DEV, n_mod), F32)),
        in_specs=[vm] * n_src, out_specs=(vm, vm),
        scratch_shapes=[pltpu.VMEM((NR, LANE), F32), pltpu.VMEM((N_DEV, NR, LANE), F32),
                        pltpu.SemaphoreType.DMA((N_DEV - 1,)), pltpu.SemaphoreType.DMA((N_DEV - 1,))],
        compiler_params=pltpu.CompilerParams(vmem_limit_bytes=VMEM_LIMIT),
    )(*srcs)
    out_shape = []
    for w, _, _ in params:
        out_shape += [_S(w.shape, F32)] * 4
    out_shape += [_S((CONV_K, cd), F32), _S((1, LANE), F32)]
    flat = [t for p in params for t in p]
    res = pl.pallas_call(
        update, name="small_adamw", out_shape=tuple(out_shape),
        in_specs=[vm] * (1 + 3 * n_par), out_specs=tuple([vm] * len(out_shape)),
        compiler_params=pltpu.CompilerParams(vmem_limit_bytes=VMEM_LIMIT),
    )(total, *flat)
    return (*res[:-1], dmod_all, res[-1])


def _ag_weights(shards, after):
    n = len(shards)

    def body(*refs):
        ins, outs = refs[:n], refs[n + 1:2 * n + 1]
        ssem, rsem, lsem = refs[2 * n + 1:]
        x, y, c = pos = _my_pos()
        me = _lin(pos)
        sib = (x, y, 1 - c)
        chips = [(1 - x, y), (x, 1 - y), (1 - x, 1 - y)]

        def copy(a, k, block, to, src=None):
            return pltpu.make_async_remote_copy(
                src_ref=outs[a].at[block] if src is None else src, dst_ref=outs[a].at[block],
                send_sem=ssem.at[a * 7 + k], recv_sem=rsem.at[a * 7 + k], device_id=to, device_id_type=MESH)

        local = [pltpu.make_async_copy(ins[a], outs[a].at[me], lsem.at[a]) for a in range(n)]
        for cp in local:
            cp.start()
        first = []
        for a in range(n):
            first.append(copy(a, 0, me, sib, src=ins[a]))
            first += [copy(a, 1 + j, me, (*chip, c), src=ins[a]) for j, chip in enumerate(chips)]
        for cp in first:
            cp.start()
        passed = []
        for a in range(n):
            for j, chip in enumerate(chips):
                blk = _lin((*chip, c))
                copy(a, 1 + j, blk, pos).wait_recv()
                cp = copy(a, 4 + j, blk, sib)
                cp.start()
                passed.append(cp)
        for a in range(n):
            copy(a, 0, _lin(sib), pos).wait_recv()
            for j, chip in enumerate(chips):
                copy(a, 4 + j, _lin((*chip, 1 - c)), pos).wait_recv()
        for cp in first + passed:
            cp.wait_send()
        for cp in local:
            cp.wait()

    hbm = pl.BlockSpec(memory_space=pl.ANY)
    return pl.pallas_call(
        body, name="ag_weights",
        out_shape=tuple(_S((N_DEV,) + s.shape, s.dtype) for s in shards),
        in_specs=[hbm] * (n + 1), out_specs=tuple([hbm] * n),
        scratch_shapes=[pltpu.SemaphoreType.DMA((7 * n,)), pltpu.SemaphoreType.DMA((7 * n,)),
                        pltpu.SemaphoreType.DMA((n,))],
    )(*shards, after)


def _rs_d2d(blocked, name):
    n = len(blocked)

    def body(*refs):
        ins, outs = refs[:n], refs[n:2 * n]
        ssem, rsem = refs[2 * n:]
        x, y, c = pos = _my_pos()
        sib = (x, y, 1 - c)
        cps = []
        for a in range(n):
            for j in range(N_CHIP):
                cp = pltpu.make_async_remote_copy(
                    src_ref=ins[a].at[2 * j + (1 - c)], dst_ref=outs[a].at[j], send_sem=ssem.at[a * N_CHIP + j],
                    recv_sem=rsem.at[a * N_CHIP + j], device_id=sib, device_id_type=MESH)
                cp.start()
                cps.append(cp)
        for cp in cps:
            cp.wait_recv()
        for cp in cps:
            cp.wait_send()

    hbm = pl.BlockSpec(memory_space=pl.ANY)
    return pl.pallas_call(
        body, name=name,
        out_shape=tuple(_S((N_CHIP,) + b.shape[1:], b.dtype) for b in blocked),
        in_specs=[hbm] * n, out_specs=tuple([hbm] * n),
        scratch_shapes=[pltpu.SemaphoreType.DMA((N_CHIP * n,)), pltpu.SemaphoreType.DMA((N_CHIP * n,))],
    )(*blocked)


_HBM = pl.BlockSpec(memory_space=pltpu.HBM)
_SEM = pl.BlockSpec(memory_space=pltpu.SEMAPHORE)
_ANY = pl.BlockSpec(memory_space=pl.ANY)
_EFFECT = pltpu.SideEffectType.DATAFLOW_SIDE_EFFECTING


def _in_hbm(a):
    return pltpu.with_memory_space_constraint(a, pltpu.HBM)


def _plan_gather(n):
    def copies(pos):
        x, y, c = pos
        out = []
        for a in range(n):
            for dev in [(x, y, 1 - c)] + [(*_peer(pos, 2 * k)[:2], c) for k in range(1, N_CHIP)]:
                out.append((a, None, n + a, _lin(pos), dev, _lin(dev)))
        return out
    return copies


def _plan_forward(n):
    def copies(pos):
        x, y, c = pos
        out = []
        for a in range(n):
            for k in range(1, N_CHIP):
                tx, ty, _ = _peer(pos, 2 * k)
                out.append((a, _lin((tx, ty, c)), a, _lin((tx, ty, c)), (x, y, 1 - c), _lin((tx, ty, 1 - c))))
        return out
    return copies


def _plan_scatter_all(n):
    def copies(pos):
        out = []
        for a in range(n):
            for k in range(1, N_DEV):
                dev = _peer(pos, k)
                out.append((a, _lin(dev), n + a, _lin(pos), dev, _lin(dev)))
        return out
    return copies


def _plan_scatter(n):
    def copies(pos):
        x, y, c = pos
        out = []
        for a in range(n):
            for k in range(1, N_CHIP):
                tx, ty, _ = _peer(pos, 2 * k)
                out.append((a, 2 * tx + ty, n + a, 2 * x + y, (tx, ty, c), 2 * tx + ty))
        return out
    return copies


def _split_copy(refs, cp, ssem, rsem, i, arrival):
    si, s_slot, di, d_slot, dev, a_slot = cp
    return pltpu.make_async_remote_copy(
        src_ref=refs[si] if s_slot is None else refs[si].at[s_slot], dst_ref=refs[di].at[a_slot if arrival else d_slot],
        send_sem=ssem.at[i], recv_sem=rsem.at[i], device_id=dev, device_id_type=MESH)


def _split_start(arrays, copies, after, name):
    n = len(arrays)
    n_cp = len(copies((0, 0, 0)))

    def body(*refs):
        ssem, rsem, token = refs[n + 1], refs[n + 2], refs[-1]
        for i, cp in enumerate(copies(_my_pos())):
            _split_copy(refs, cp, ssem, rsem, i, False).start()
        token[...] = jnp.zeros_like(token)

    res = pl.pallas_call(
        body, name=name,
        out_shape=(pltpu.SemaphoreType.DMA((n_cp,)), pltpu.SemaphoreType.DMA((n_cp,)),
                   *[pltpu.HBM(a.shape, a.dtype) for a in arrays], _S((8, LANE), F32)),
        in_specs=[_HBM] * n + [_ANY],
        out_specs=(_SEM, _SEM, *[_HBM] * n, pl.BlockSpec(memory_space=pltpu.VMEM)),
        input_output_aliases={a: 2 + a for a in range(n)},
        compiler_params=pltpu.CompilerParams(has_side_effects=_EFFECT),
    )(*[_in_hbm(a) for a in arrays], after)
    return res[0], res[1], list(res[2:2 + n]), res[-1]


def _split_wait(ssem, rsem, arrays, copies, after, name):
    n = len(arrays)

    def body(*refs):
        for i, cp in enumerate(copies(_my_pos())):
            d = _split_copy(refs, cp, refs[n], refs[n + 1], i, True)
            d.wait_send()
            d.wait_recv()

    res = pl.pallas_call(
        body, name=name,
        out_shape=tuple(pltpu.HBM(a.shape, a.dtype) for a in arrays),
        in_specs=[_HBM] * n + [_SEM, _SEM, _ANY], out_specs=tuple([_HBM] * n),
        input_output_aliases={a: a for a in range(n)},
        compiler_params=pltpu.CompilerParams(has_side_effects=_EFFECT),
    )(*arrays, ssem, rsem, after)
    return list(res)


def _row_tile(R, itemsize_rows=16, cap=256):
    t = cap - cap % itemsize_rows
    while t >= itemsize_rows:
        if R % t == 0:
            return t
        t -= itemsize_rows
    return R


def _pair_sum(blocked, recv, core):
    _, R, C = blocked.shape
    tr = _row_tile(R)

    def body(ids_ref, a_ref, b_ref, o_ref):
        del ids_ref
        o_ref[...] = (a_ref[...].astype(F32) + b_ref[...].astype(F32)).astype(BF16)

    gs = pltpu.PrefetchScalarGridSpec(
        num_scalar_prefetch=1, grid=(N_CHIP, R // tr),
        in_specs=[pl.BlockSpec((1, tr, C), lambda j, r, ids: (2 * j + ids[0], r, 0)),
                  pl.BlockSpec((1, tr, C), lambda j, r, ids: (j, r, 0))],
        out_specs=pl.BlockSpec((1, tr, C), lambda j, r, ids: (j, r, 0)))
    return pl.pallas_call(body, name="pair_sum", grid_spec=gs, out_shape=_S((N_CHIP, R, C), BF16),
                          compiler_params=_params(("arbitrary", "arbitrary")))(core, blocked, recv)


def _adamw_math(w, g, m, v):
    m2 = ADAM_B1 * m + (1.0 - ADAM_B1) * g
    v2 = ADAM_B2 * v + (1.0 - ADAM_B2) * (g * g)
    m_hat = m2 / (1.0 - ADAM_B1 ** ADAM_STEP)
    v_hat = v2 / (1.0 - ADAM_B2 ** ADAM_STEP)
    delta = -ADAM_LR * (m_hat / (jnp.sqrt(v_hat) + ADAM_EPS) + ADAM_WD * w)
    return delta, m2, v2


def _sum_adamw_split(pairs, land, chips, w, m, v, transposed=False):
    R, C = w.shape
    n_slots = chips.shape[0]
    tr = _row_tile(R, 128 if transposed else 16)

    def body(ids_ref, *refs):
        del ids_ref
        parts, (w_ref, m_ref, v_ref, g_ref, d_ref, m2_ref, v2_ref) = refs[:n_slots], refs[n_slots:]
        g = parts[0][0].astype(F32)
        for p_ref in parts[1:]:
            g = g + p_ref[0].astype(F32)
        if transposed:
            g = g.T
        g_ref[...] = g
        d_ref[...], m2_ref[...], v2_ref[...] = _adamw_math(w_ref[...], g, m_ref[...], v_ref[...])

    t = pl.BlockSpec((tr, C), lambda r, ids: (r, 0))
    if transposed:
        slot = lambda k: pl.BlockSpec((1, C, tr), lambda r, ids: (ids[k], 0, r))
    else:
        slot = lambda k: pl.BlockSpec((1, tr, C), lambda r, ids: (ids[k], r, 0))
    gs = pltpu.PrefetchScalarGridSpec(num_scalar_prefetch=1, grid=(R // tr,),
                                      in_specs=[slot(k) for k in range(n_slots)] + [t, t, t], out_specs=(t, t, t, t))
    return pl.pallas_call(body, name="sum_adamw_split", grid_spec=gs, out_shape=tuple(_S((R, C), F32) for _ in range(4)),
                          compiler_params=_params(("arbitrary",)))(chips, pairs, *[land] * (n_slots - 1), w, m, v)


def _adamw(w, g, m, v):
    R, C = w.shape
    tr = _row_tile(R, 8)

    def body(w_ref, g_ref, m_ref, v_ref, d_ref, m2_ref, v2_ref):
        d_ref[...], m2_ref[...], v2_ref[...] = _adamw_math(w_ref[...], g_ref[...], m_ref[...], v_ref[...])

    t = pl.BlockSpec((tr, C), lambda r: (r, 0))
    return pl.pallas_call(body, name="adamw", grid=(R // tr,), in_specs=[t, t, t, t], out_specs=(t, t, t),
                          out_shape=tuple(_S((R, C), F32) for _ in range(3)),
                          compiler_params=_params(("arbitrary",)))(w, g, m, v)


def _ada_grad_adamw(cs16, dmod16, w, m, v):
    D, C6 = w.shape
    tr = _row_tile(D, 8, 256)

    def body(cs_ref, dm_ref, w_ref, m_ref, v_ref, g_ref, d_ref, m2_ref, v2_ref):
        g = _dot_tn(cs_ref[...].astype(BF16), dm_ref[...].astype(BF16))
        g_ref[...] = g
        d_ref[...], m2_ref[...], v2_ref[...] = _adamw_math(w_ref[...], g, m_ref[...], v_ref[...])

    t = pl.BlockSpec((tr, C6), lambda r: (r, 0))
    return pl.pallas_call(
        body, name="ada_grad_adamw", grid=(D // tr,),
        in_specs=[pl.BlockSpec((16, tr), lambda r: (0, r)), _acc((16, C6)), t, t, t], out_specs=(t, t, t, t),
        out_shape=tuple(_S((D, C6), F32) for _ in range(4)), compiler_params=_params(("arbitrary",)))(cs16, dmod16, w, m, v)


def _pick(n, cands):
    for c in cands:
        if n % c == 0:
            return c
    return n


def _matmul_tn(a, b, name, square_a=False):
    L, K = a.shape
    N = b.shape[1]
    bk = _pick(K, (1024, 512, 256, 128))
    bn = _pick(N, (1024, 768, 512, 256, 128))
    tl = _pick(L, (1024, 512, 256, 128))
    n_l = L // tl

    def body(a_ref, b_ref, o_ref, acc_ref):
        l = pl.program_id(2)

        @pl.when(l == 0)
        def _():
            acc_ref[...] = jnp.zeros_like(acc_ref)
        av = a_ref[...]
        if square_a:
            av = av.astype(F32)
            av = av * av
        acc_ref[...] += _dot_tn(av.astype(BF16), b_ref[...].astype(BF16))

        @pl.when(l == n_l - 1)
        def _():
            o_ref[...] = acc_ref[...].astype(BF16)

    return pl.pallas_call(
        body, name=name, grid=(K // bk, N // bn, n_l),
        in_specs=[pl.BlockSpec((tl, bk), lambda k, n, l: (l, k)), pl.BlockSpec((tl, bn), lambda k, n, l: (l, n))],
        out_specs=pl.BlockSpec((bk, bn), lambda k, n, l: (k, n)), out_shape=_S((K, N), BF16),
        scratch_shapes=[pltpu.VMEM((bk, bn), F32)],
        compiler_params=_params(("arbitrary", "arbitrary", "arbitrary")))(a, b)


def _merge_blocks(a):
    n, R, C = a.shape
    cb = _pick(C, (256, 128))

    def body(i_ref, o_ref):
        for j in range(n):
            o_ref[R * j:R * (j + 1), :] = i_ref[j]

    return pl.pallas_call(body, name="merge_blocks", out_shape=_S((n * R, C), a.dtype), grid=(C // cb,),
                          in_specs=[pl.BlockSpec((n, R, cb), lambda c: (0, 0, c))],
                          out_specs=pl.BlockSpec((n * R, cb), lambda c: (0, c)),
                          compiler_params=_params(("arbitrary",)))(a)


def _split_blocks(a, n):
    NR, C = a.shape
    R = NR // n
    cb = _pick(C, (256, 128))

    def body(i_ref, o_ref):
        for j in range(n):
            o_ref[j] = i_ref[R * j:R * (j + 1), :].astype(BF16)

    return pl.pallas_call(body, name="split_blocks", out_shape=_S((n, R, C), BF16), grid=(C // cb,),
                          in_specs=[pl.BlockSpec((NR, cb), lambda c: (0, c))],
                          out_specs=pl.BlockSpec((n, R, cb), lambda c: (0, 0, c)),
                          compiler_params=_params(("arbitrary",)))(a)


def _gw_in(pieces, u1, dims):
    L, D = u1.shape
    H = dims["H"]
    r_z, r_xbc, r_dt, r_q, r_kv = _proj_rows(dims)
    PROJ = r_kv[1]
    tl = _pick(L, (512, 256, 128))
    n_l = L // tl

    def body(dz_ref, dxbc_ref, ddt_ref, dq_ref, dkv_ref, u_ref, o_ref):
        @pl.when(pl.program_id(0) == 0)
        def _():
            o_ref[...] = jnp.zeros_like(o_ref)
        u = u_ref[...]
        for ref, (r0, r1) in ((dz_ref, r_z), (dxbc_ref, r_xbc), (dq_ref, r_q), (dkv_ref, r_kv)):
            o_ref[r0:r1, :] += _dot_tn(ref[...].astype(BF16), u)
        o_ref[r_dt[0]:r_dt[0] + H, :] += _dot_tn(ddt_ref[...].astype(BF16), u)[0:H, :]

    return pl.pallas_call(
        body, name="gw_in", grid=(n_l,),
        in_specs=[_tile(p.shape[1], tl) for p in pieces] + [_tile(D, tl)],
        out_specs=_acc((PROJ, D)), out_shape=_S((PROJ, D), F32),
        compiler_params=_params(("arbitrary",)))(*pieces, u1)


def _proj_rows(dims):
    W, CD, H, AW, KVW2 = dims["W"], dims["CD"], dims["H"], dims["AW"], dims["KVW2"]
    o_dt = W + CD
    o_q = o_dt + H
    return (0, W), (W, o_dt), (o_dt, o_dt + LANE), (o_q, o_q + AW), (o_q + AW, o_q + AW + KVW2)


def _ln_in_proj(x, g, b, sc, sh, w_t, dims):
    L, D = x.shape
    W, CD, AW, KVW2 = dims["W"], dims["CD"], dims["AW"], dims["KVW2"]
    PROJ = w_t.shape[0]
    tm = _pick(L, (MLP_TM, 128))
    r_z, r_xbc, r_dt, r_q, r_kv = _proj_rows(dims)

    def body(x_ref, g_ref, b_ref, sc_ref, sh_ref, w_ref, xhat_ref, rstd_ref, u1_ref, z_ref, xbc_ref, q_ref, kv_ref, dt_ref):
        xhat, rstd = _ln_fwd(x_ref[...])
        xhat_ref[...] = xhat
        rstd_ref[...] = rstd
        h0 = xhat * g_ref[...] + b_ref[...]
        u1 = (h0 * (1.0 + sc_ref[...]) + sh_ref[...]).astype(BF16)
        u1_ref[...] = u1
        z_ref[...] = _dot_nt(u1, w_ref[r_z[0]:r_z[1], :])
        xbc_ref[...] = _dot_nt(u1, w_ref[r_xbc[0]:r_xbc[1], :])
        q_ref[...] = _dot_nt(u1, w_ref[r_q[0]:r_q[1], :]).astype(BF16)
        kv_ref[...] = _dot_nt(u1, w_ref[r_kv[0]:r_kv[1], :]).astype(BF16)
        dt_ref[...] = _dot_nt(u1, w_ref[r_dt[0]:r_dt[1], :])

    v = _acc((1, D))
    return pl.pallas_call(
        body, name="ln_in_proj", grid=(L // tm,),
        in_specs=[_tile(D, tm), v, v, v, v, _res((PROJ, D))],
        out_specs=(_tile(D, tm), _tile(1, tm), _tile(D, tm), _tile(W, tm), _tile(CD, tm), _tile(AW, tm),
                   _tile(KVW2, tm), _tile(LANE, tm)),
        out_shape=(_S((L, D), F32), _S((L, 1), F32), _S((L, D), BF16), _S((L, W), F32), _S((L, CD), F32),
                   _S((L, AW), BF16), _S((L, KVW2), BF16), _S((L, LANE), F32)),
        compiler_params=_params(("arbitrary",)))(x, g, b, sc, sh, w_t)


def _conv_act(cur_ref, prev_ref, cw_ref, cb_ref, ext_ref, first):
    T = cur_ref.shape[0]
    ext_ref[0:HALO, :] = jnp.where(first, 0.0, prev_ref[...])
    ext_ref[HALO:HALO + T, :] = cur_ref[...]
    pre = cb_ref[...] + cw_ref[0:1, :] * ext_ref[HALO - 3:HALO - 3 + T, :]
    for k in range(1, CONV_K):
        pre = pre + cw_ref[k:k + 1, :] * ext_ref[HALO - 3 + k:HALO - 3 + k + T, :]
    sig = _sigmoid(pre)
    return pre * sig, pre, sig


def _tri(T, upper=False):
    r = lax.broadcasted_iota(jnp.int32, (T, T), 0)
    c = lax.broadcasted_iota(jnp.int32, (T, T), 1)
    return (r <= c) if upper else (r >= c)


def _expand_heads_lanes(dst_ref, v, n_heads):
    for h in range(n_heads):
        dst_ref[:, h * HEAD_DIM:(h + 1) * HEAD_DIM] = jnp.broadcast_to(v[:, h:h + 1], (v.shape[0], HEAD_DIM))


def _head_onehot(n_heads):
    hd = np.arange(3 * LANE)[:, None] % LANE
    ch = np.arange(n_heads * HEAD_DIM)[None, :] // HEAD_DIM
    return jnp.asarray((hd == ch).astype(np.float32)).astype(BF16)


def _expand_heads(dst_ref, v, n_heads, onehot_ref):
    onehot = onehot_ref[...]
    v = jnp.where(lax.broadcasted_iota(jnp.int32, v.shape, 1) < n_heads, v, 0.0)
    hi = v.astype(BF16)
    r1 = v - hi.astype(F32)
    mid = r1.astype(BF16)
    lo = (r1 - mid.astype(F32)).astype(BF16)
    dst_ref[...] = _dot(jnp.concatenate([hi, mid, lo], axis=1), onehot)


def _head_reduce(v):
    wdt = v.shape[1]
    ch = lax.broadcasted_iota(jnp.int32, (wdt, LANE), 0)
    lo = lax.broadcasted_iota(jnp.int32, (wdt, LANE), 1) * HEAD_DIM
    onehot = ((ch >= lo) & (ch < lo + HEAD_DIM)).astype(BF16)
    hi = v.astype(BF16)
    rest = (v - hi.astype(F32)).astype(BF16)
    return _dot(hi, onehot) + _dot(rest, onehot)


def _conv_ssd(xbc, dt_raw, z, cw, cb, dtb, alog, dsk, nw, dims):
    L, CD = xbc.shape
    W, H, G, N = dims["W"], dims["H"], SSD_GROUPS, SSD_STATE
    T = CHUNK
    R = H // G
    GW = W // G
    nc = L // T
    HP = H * HEAD_DIM

    def body(xbc_ref, prev_ref, dt_ref, z_ref, cw_ref, cb_ref, dtb_ref, alog_ref, dsk_ref, nw_ref, oh_ref,
             y_ref, yn_ref, sp_ref, ext_ref, s_ref, ybuf_ref, dtx_ref, acx_ref, xb_ref):
        i = pl.program_id(0)

        @pl.when(i == 0)
        def _():
            s_ref[...] = jnp.zeros_like(s_ref)

        act, _, _ = _conv_act(xbc_ref, prev_ref, cw_ref, cb_ref, ext_ref, i == 0)
        xs = act[:, :W]
        dt = _softplus(dt_ref[...] + dtb_ref[...])
        a = dt * (-jnp.exp(alog_ref[...]))
        low = _tri(T)
        acum = _dot_hi(low.astype(F32), a)
        acum_t = acum.T
        _expand_heads(dtx_ref, dt, H, oh_ref)
        _expand_heads(acx_ref, acum, H, oh_ref)
        acx = acx_ref[...]
        lastx = acx[T - 1:T, :]
        xd = xs * dtx_ref[...]
        xb_ref[...] = xd.astype(BF16)
        xdb = (xd * jnp.exp(lastx - acx)).astype(BF16)
        ex = jnp.exp(acx)
        elx = jnp.exp(lastx)
        for g in range(G):
            gs = slice(g * GW, (g + 1) * GW)
            bgb = act[:, W + g * N:W + (g + 1) * N].astype(BF16)
            cgb = act[:, W + G * N + g * N:W + G * N + (g + 1) * N].astype(BF16)
            stg = s_ref[:, gs]
            sp_ref[0, :, gs] = stg
            yoff = ex[:, gs] * _dot(cgb, stg.astype(BF16))
            s_ref[:, gs] = stg * elx[:, gs] + _dot_tn(bgb, xdb[:, gs])
            cb_g = _dot_nt(cgb, bgb)
            for r in range(R):
                h = g * R + r
                hs = slice(h * HEAD_DIM, (h + 1) * HEAD_DIM)
                lm = jnp.where(low, jnp.exp(acum[:, h:h + 1] - acum_t[h:h + 1, :]), 0.0)
                ybuf_ref[:, hs] = _dot((cb_g * lm).astype(BF16), xb_ref[:, hs]) + yoff[:, r * HEAD_DIM:(r + 1) * HEAD_DIM]
        y = ybuf_ref[...] + dsk_ref[...] * xs
        y_ref[...] = y
        zz = z_ref[...]
        hh = y * (zz * _sigmoid(zz))
        for g in range(G):
            gs = slice(g * GW, (g + 1) * GW)
            hg = hh[:, gs]
            yn_ref[:, gs] = (hg * lax.rsqrt(_mean(hg * hg) + RMS_EPS) * nw_ref[:, gs]).astype(BF16)

    return pl.pallas_call(
        body, name="conv_ssd", grid=(nc,),
        in_specs=[_tile(CD, T), pl.BlockSpec((HALO, CD), lambda i: (jnp.maximum(i * (T // HALO) - 1, 0), 0)),
                  _tile(LANE, T), _tile(W, T), _acc((CONV_K, CD)), _acc((1, CD)), _acc((1, LANE)), _acc((1, LANE)),
                  _acc((1, W)), _acc((1, W)), _acc((3 * LANE, W))],
        out_specs=(_tile(W, T), _tile(W, T), pl.BlockSpec((1, N, HP), lambda i: (i, 0, 0))),
        out_shape=(_S((L, W), F32), _S((L, W), BF16), _S((nc, N, HP), F32)),
        scratch_shapes=[pltpu.VMEM((T + HALO, CD), F32), pltpu.VMEM((N, HP), F32), pltpu.VMEM((T, W), F32),
                        pltpu.VMEM((T, W), F32), pltpu.VMEM((T, W), F32), pltpu.VMEM((T, W), BF16)],
        compiler_params=_params(("arbitrary",)))(xbc, xbc, dt_raw, z, cw, cb, dtb, alog, dsk, nw, _head_onehot(H))


def _attn_mask(T, i):
    r = lax.broadcasted_iota(jnp.int32, (T, 2 * T), 0)
    c = lax.broadcasted_iota(jnp.int32, (T, 2 * T), 1)
    dist = r + T - c
    valid = (dist >= 0) & (dist < CHUNK) & ((c >= T) | (i > 0))
    return dist.astype(F32), valid


def _attn_probs(s_raw, dist, valid, slope, sink, axis):
    s = s_raw * (HEAD_DIM ** -0.5) - slope * dist
    s = jnp.where(valid, s, NEG)
    m = jnp.maximum(jnp.max(s, axis=axis, keepdims=True), sink)
    p = jnp.exp(s - m)
    e_sink = jnp.exp(sink - m)
    inv = 1.0 / (jnp.sum(p, axis=axis, keepdims=True) + e_sink)
    return p * inv, e_sink * inv


def _kv_heads(kvc_ref, kvp_ref, g, n_kv):
    ks = slice(g * HEAD_DIM, (g + 1) * HEAD_DIM)
    vs = slice((n_kv + g) * HEAD_DIM, (n_kv + g + 1) * HEAD_DIM)
    kk = jnp.concatenate([kvp_ref[:, ks], kvc_ref[:, ks]], axis=0)
    vv = jnp.concatenate([kvp_ref[:, vs], kvc_ref[:, vs]], axis=0)
    return kk, vv


def _swa_fwd(q, kv, sinks, dims):
    L, AW = q.shape
    KV, KVW2 = dims["KV"], dims["KVW2"]
    T = CHUNK
    nb = L // T
    slopes = _alibi_slopes(dims["AH"])

    def body(q_ref, kvc_ref, kvp_ref, sink_ref, o_ref, qg_ref, p_ref):
        i = pl.program_id(0)
        dist, valid = _attn_mask(T, i)
        for g in range(KV):
            kk, vv = _kv_heads(kvc_ref, kvp_ref, g, KV)
            for r in range(GQA):
                h = g * GQA + r
                qg_ref[r * T:(r + 1) * T, :] = q_ref[:, h * HEAD_DIM:(h + 1) * HEAD_DIM]
            s_all = _dot_nt(qg_ref[...], kk)
            for r in range(GQA):
                h = g * GQA + r
                p, _ = _attn_probs(s_all[r * T:(r + 1) * T, :], dist, valid, slopes[h], sink_ref[h], -1)
                p_ref[r * T:(r + 1) * T, :] = p.astype(BF16)
            o_all = _dot(p_ref[...], vv)
            for r in range(GQA):
                h = g * GQA + r
                o_ref[:, h * HEAD_DIM:(h + 1) * HEAD_DIM] = o_all[r * T:(r + 1) * T, :].astype(BF16)

    return pl.pallas_call(
        body, name="swa_fwd", grid=(nb,),
        in_specs=[_tile(AW, T), _tile(KVW2, T), pl.BlockSpec((T, KVW2), lambda i: (jnp.maximum(i - 1, 0), 0)),
                  pl.BlockSpec(memory_space=pltpu.SMEM)],
        out_specs=_tile(AW, T), out_shape=_S((L, AW), BF16),
        scratch_shapes=[pltpu.VMEM((GQA * T, HEAD_DIM), BF16), pltpu.VMEM((GQA * T, 2 * T), BF16)],
        compiler_params=_params(("arbitrary",)))(q, kv, kv, sinks)


def _out_proj_ln1(yn, o, w_out, xhat0, vecs, alpha):
    L, W = yn.shape
    D = xhat0.shape[1]
    MIX = w_out.shape[0]
    tm = _pick(L, (MLP_TM, 128))

    def body(yn_ref, o_ref, w_ref, xh_ref, v_ref, mix_ref, xhat1_ref, rstd1_ref, u2_ref):
        mix = _dot(yn_ref[...], w_ref[0:W, :]) + _dot(o_ref[...], w_ref[W:MIX, :])
        mix_ref[...] = mix
        h0 = xh_ref[...] * v_ref[0:1, :] + v_ref[1:2, :]
        xhat1, rstd1 = _ln_fwd(alpha * h0 + (1.0 + v_ref[2:3, :]) * mix)
        xhat1_ref[...] = xhat1
        rstd1_ref[...] = rstd1
        h1 = xhat1 * v_ref[3:4, :] + v_ref[4:5, :]
        u2_ref[...] = (h1 * (1.0 + v_ref[5:6, :]) + v_ref[6:7, :]).astype(BF16)

    return pl.pallas_call(
        body, name="out_proj_ln1", grid=(L // tm,),
        in_specs=[_tile(W, tm), _tile(MIX - W, tm), _res((MIX, D)), _tile(D, tm), _acc((8, D))],
        out_specs=(_tile(D, tm), _tile(D, tm), _tile(1, tm), _tile(D, tm)),
        out_shape=(_S((L, D), F32), _S((L, D), F32), _S((L, 1), F32), _S((L, D), BF16)),
        compiler_params=_params(("arbitrary",)))(yn, o, w_out, xhat0, vecs)


def _mlp_loss(u2, w1, w2, xhat1, tgt, vecs, b1, alpha):
    L, D = xhat1.shape
    FF = w1.shape[1]
    tm = _pick(L, (MLP_TM, 128))
    sub = min(tm, MLP_SUB)
    fc = _pick(FF, (MLP_FC, 256, 128))

    def body(u2_ref, w1_ref, w2_ref, xh_ref, t_ref, v_ref, b1_ref, rr_ref, dr2_ref, acc_ref, loss_ref):
        @pl.when(pl.program_id(0) == 0)
        def _():
            acc_ref[...] = jnp.zeros_like(acc_ref)
            loss_ref[...] = jnp.zeros_like(loss_ref)

        for s in range(tm // sub):
            rs = slice(s * sub, (s + 1) * sub)
            u2 = u2_ref[rs, :]
            f = jnp.zeros((sub, D), F32) + v_ref[5:6, :]
            for j in range(FF // fc):
                cs = slice(j * fc, (j + 1) * fc)
                rr = jnp.maximum(_dot(u2, w1_ref[:, cs]) + b1_ref[:, cs], 0.0)
                rr_ref[rs, cs] = rr.astype(BF16)
                f = f + _dot((rr * rr).astype(BF16), w2_ref[cs, :])
            xhat1 = xh_ref[rs, :]
            h1 = xhat1 * v_ref[0:1, :] + v_ref[1:2, :]
            xhat2, rstd2 = _ln_fwd(alpha * h1 + (1.0 + v_ref[2:3, :]) * f)
            e = xhat2 * v_ref[3:4, :] + v_ref[4:5, :] - t_ref[rs, :]
            loss_ref[...] += 0.5 * jnp.sum(_mean(e * e))
            dy = e * (1.0 / D)
            dr2 = _ln_bwd(dy * v_ref[3:4, :], xhat2, rstd2)
            dr2_ref[rs, :] = dr2
            acc_ref[0:1, :] += _colsum(dy * xhat2)
            acc_ref[1:2, :] += _colsum(dy)
            acc_ref[2:3, :] += _colsum(dr2 * f)

    return pl.pallas_call(
        body, name="mlp_loss", grid=(L // tm,),
        in_specs=[_tile(D, tm), _res((D, FF)), _res((FF, D)), _tile(D, tm), _tile(D, tm), _acc((8, D)), _acc((1, FF))],
        out_specs=(_tile(FF, tm), _tile(D, tm), _acc((8, D)), _acc((1, LANE))),
        out_shape=(_S((L, FF), BF16), _S((L, D), F32), _S((8, D), F32), _S((1, LANE), F32)),
        compiler_params=_params(("arbitrary",)))(u2, w1, w2, xhat1, tgt, vecs, b1)


def _mlp_bwd_a(dr2, rr, w2, g2):
    L, D = dr2.shape
    FF = w2.shape[0]
    tm = _pick(L, (MLP_TM, 128))
    fc = _pick(FF, (MLP_FC, 256, 128))

    def body(dr2_ref, rr_ref, w2_ref, g2_ref, df_ref, da_ref, gb2_ref, gb1_ref):
        @pl.when(pl.program_id(0) == 0)
        def _():
            gb2_ref[...] = jnp.zeros_like(gb2_ref)
            gb1_ref[...] = jnp.zeros_like(gb1_ref)

        df = (1.0 + g2_ref[...]) * dr2_ref[...]
        gb2_ref[...] += _colsum(df)
        dfb = df.astype(BF16)
        df_ref[...] = dfb
        for j in range(FF // fc):
            cs = slice(j * fc, (j + 1) * fc)
            da = _dot_nt(dfb, w2_ref[cs, :]) * (2.0 * rr_ref[:, cs].astype(F32))
            gb1_ref[:, cs] += _colsum(da)
            da_ref[:, cs] = da.astype(BF16)

    return pl.pallas_call(
        body, name="mlp_bwd_a", grid=(L // tm,),
        in_specs=[_tile(D, tm), _tile(FF, tm), _res((FF, D)), _acc((1, D))],
        out_specs=(_tile(D, tm), _tile(FF, tm), _acc((1, D)), _acc((1, FF))),
        out_shape=(_S((L, D), BF16), _S((L, FF), BF16), _S((1, D), F32), _S((1, FF), F32)),
        compiler_params=_params(("arbitrary",)))(dr2, rr, w2, g2)


def _mlp_bwd_b(da, w1, dr2, xhat1, rstd1, mix, w_out, vecs, alpha, W):
    L, FF = da.shape
    D = dr2.shape[1]
    MIX = w_out.shape[0]
    tm = _pick(L, (MLP_TM, 128))

    def body(da_ref, w1_ref, dr2_ref, xh_ref, rs_ref, mix_ref, wo_ref, v_ref, dmix_ref, dh0_ref, dyn_ref, do_ref, acc_ref):
        @pl.when(pl.program_id(0) == 0)
        def _():
            acc_ref[...] = jnp.zeros_like(acc_ref)

        du2 = _dot_nt(da_ref[...], w1_ref[...])
        xhat1 = xh_ref[...]
        h1 = xhat1 * v_ref[0:1, :] + v_ref[1:2, :]
        acc_ref[0:1, :] += _colsum(du2 * h1)
        acc_ref[1:2, :] += _colsum(du2)
        dh1 = alpha * dr2_ref[...] + du2 * (1.0 + v_ref[2:3, :])
        acc_ref[2:3, :] += _colsum(dh1 * xhat1)
        acc_ref[3:4, :] += _colsum(dh1)
        dr1 = _ln_bwd(dh1 * v_ref[0:1, :], xhat1, rs_ref[...])
        acc_ref[4:5, :] += _colsum(dr1 * mix_ref[...])
        dh0_ref[...] = alpha * dr1
        dmix = ((1.0 + v_ref[3:4, :]) * dr1).astype(BF16)
        dmix_ref[...] = dmix
        dyn_ref[...] = _dot_nt(dmix, wo_ref[0:W, :])
        do_ref[...] = _dot_nt(dmix, wo_ref[W:MIX, :]).astype(BF16)

    return pl.pallas_call(
        body, name="mlp_bwd_b", grid=(L // tm,),
        in_specs=[_tile(FF, tm), _res((D, FF)), _tile(D, tm), _tile(D, tm), _tile(1, tm), _tile(D, tm), _res((MIX, D)),
                  _acc((8, D))],
        out_specs=(_tile(D, tm), _tile(D, tm), _tile(W, tm), _tile(MIX - W, tm), _acc((8, D))),
        out_shape=(_S((L, D), BF16), _S((L, D), F32), _S((L, W), F32), _S((L, MIX - W), BF16), _S((8, D), F32)),
        compiler_params=_params(("arbitrary",)))(da, w1, dr2, xhat1, rstd1, mix, w_out, vecs)


def _swa_bwd(q, kv, do, sinks, dims):
    L, AW = q.shape
    KV, KVW2 = dims["KV"], dims["KVW2"]
    T = CHUNK
    nb = L // T
    slopes = _alibi_slopes(dims["AH"])
    scale = HEAD_DIM ** -0.5

    def body(q_ref, kvc_ref, kvp_ref, do_ref, sink_ref, dq_ref, dkv_ref, dsink_ref, carry_ref,
             qg_ref, dog_ref, pt_ref, dst_ref):
        i = pl.program_id(0)

        @pl.when(i == 0)
        def _():
            carry_ref[...] = jnp.zeros_like(carry_ref)
            dsink_ref[...] = jnp.zeros_like(dsink_ref)

        @pl.when(i < nb)
        def _():
            c = lax.broadcasted_iota(jnp.int32, (2 * T, T), 0)
            r_ = lax.broadcasted_iota(jnp.int32, (2 * T, T), 1)
            dist_i = r_ + T - c
            valid = (dist_i >= 0) & (dist_i < CHUNK) & ((c >= T) | (i > 0))
            dist = dist_i.astype(F32)
            lane = lax.broadcasted_iota(jnp.int32, (1, LANE), 1)
            dsink = jnp.zeros((1, LANE), F32)
            dks, dvs = [], []
            for g in range(KV):
                kk, vv = _kv_heads(kvc_ref, kvp_ref, g, KV)
                for r in range(GQA):
                    hs = slice((g * GQA + r) * HEAD_DIM, (g * GQA + r + 1) * HEAD_DIM)
                    qg_ref[r * T:(r + 1) * T, :] = q_ref[:, hs]
                    dog_ref[r * T:(r + 1) * T, :] = do_ref[:, hs]
                st_all = _dot_nt(kk, qg_ref[...])
                dpt_all = _dot_nt(vv, dog_ref[...])
                for r in range(GQA):
                    h = g * GQA + r
                    cs = slice(r * T, (r + 1) * T)
                    p, p_sink = _attn_probs(st_all[:, cs], dist, valid, slopes[h], sink_ref[h], 0)
                    dp = dpt_all[:, cs]
                    delta = jnp.sum(p * dp, axis=0, keepdims=True)
                    pt_ref[:, cs] = p.astype(BF16)
                    dst_ref[:, cs] = (p * (dp - delta)).astype(BF16)
                    dsink = dsink + jnp.where(lane == h, -jnp.sum(p_sink * delta), 0.0)
                dst = dst_ref[...]
                dks.append(_dot(dst, qg_ref[...]) * scale)
                dvs.append(_dot(pt_ref[...], dog_ref[...]))
                dq_all = _dot_tn(dst, kk) * scale
                for r in range(GQA):
                    hs = slice((g * GQA + r) * HEAD_DIM, (g * GQA + r + 1) * HEAD_DIM)
                    dq_ref[:, hs] = dq_all[r * T:(r + 1) * T, :].astype(BF16)
            dkv = jnp.concatenate(dks + dvs, axis=1)
            dsink_ref[...] += dsink
            dkv_ref[...] = carry_ref[...] + dkv[0:T, :]
            carry_ref[...] = dkv[T:2 * T, :]

        @pl.when(i == nb)
        def _():
            dkv_ref[...] = carry_ref[...]

    last = nb - 1
    return pl.pallas_call(
        body, name="swa_bwd", grid=(nb + 1,),
        in_specs=[pl.BlockSpec((T, AW), lambda i: (jnp.minimum(i, last), 0)),
                  pl.BlockSpec((T, KVW2), lambda i: (jnp.minimum(i, last), 0)),
                  pl.BlockSpec((T, KVW2), lambda i: (jnp.clip(i - 1, 0, last), 0)),
                  pl.BlockSpec((T, AW), lambda i: (jnp.minimum(i, last), 0)),
                  pl.BlockSpec(memory_space=pltpu.SMEM)],
        out_specs=(pl.BlockSpec((T, AW), lambda i: (jnp.minimum(i, last), 0)),
                   pl.BlockSpec((T, KVW2), lambda i: (jnp.maximum(i - 1, 0), 0)), _acc((1, LANE))),
        out_shape=(_S((L, AW), BF16), _S((L, KVW2), F32), _S((1, LANE), F32)),
        scratch_shapes=[pltpu.VMEM((T, KVW2), F32), pltpu.VMEM((GQA * T, HEAD_DIM), BF16),
                        pltpu.VMEM((GQA * T, HEAD_DIM), BF16), pltpu.VMEM((2 * T, GQA * T), BF16),
                        pltpu.VMEM((2 * T, GQA * T), BF16)],
        compiler_params=_params(("arbitrary",)))(q, kv, kv, do, sinks)


def _ssd_bwd(dyn, y, z, xbc, dt_raw, sprev, cw, cb, dtb, alog, dsk, nw, dims):
    L, CD = xbc.shape
    W, H, G, N = dims["W"], dims["H"], SSD_GROUPS, SSD_STATE
    T = CHUNK
    R = H // G
    GW = W // G
    nc = L // T
    HP = H * HEAD_DIM

    def body(dyn_ref, y_ref, z_ref, xbc_ref, prev_ref, dt_ref, sp_ref, cw_ref, cb_ref, dtb_ref, alog_ref, dsk_ref, nw_ref,
             dz_ref, dpre_ref, ddt_ref, acc_ref, hacc_ref, ext_ref, ds_ref, dtx_ref, acx_ref, xb_ref, dyb_ref, r12_ref,
             dx_ref, rows_ref):
        i = pl.program_id(0)

        @pl.when(i == 0)
        def _():
            ds_ref[...] = jnp.zeros_like(ds_ref)
            acc_ref[...] = jnp.zeros_like(acc_ref)
            hacc_ref[...] = jnp.zeros_like(hacc_ref)

        act, pre, spre = _conv_act(xbc_ref, prev_ref, cw_ref, cb_ref, ext_ref, i == nc - 1)
        xs = act[:, :W]
        dt_in = dt_ref[...] + dtb_ref[...]
        dt = _softplus(dt_in)
        a_neg = -jnp.exp(alog_ref[...])
        a = dt * a_neg
        low = _tri(T)
        upf = _tri(T, upper=True).astype(F32)
        acum = _dot_hi(low.astype(F32), a)
        acum_t = acum.T

        y = y_ref[...]
        zz = z_ref[...]
        sg = _sigmoid(zz)
        sz = zz * sg
        hh = y * sz
        dyn_v = dyn_ref[...]
        parts = []
        for g in range(G):
            gs = slice(g * GW, (g + 1) * GW)
            hg = hh[:, gs]
            hhat = hg * lax.rsqrt(_mean(hg * hg) + RMS_EPS)
            rg = lax.rsqrt(_mean(hg * hg) + RMS_EPS)
            acc_ref[0:1, gs] += _colsum(dyn_v[:, gs] * hhat)
            dhhat = dyn_v[:, gs] * nw_ref[:, gs]
            parts.append(rg * (dhhat - hhat * _mean(dhhat * hhat)))
        dhh = jnp.concatenate(parts, axis=1)
        dy = dhh * sz
        dz_ref[...] = (dhh * y * (sg * (1.0 + zz * (1.0 - sg)))).astype(BF16)
        acc_ref[1:2, :] += _colsum(dy * xs)
        dyb_ref[...] = dy.astype(BF16)

        _expand_heads_lanes(dtx_ref, dt, H)
        _expand_heads_lanes(acx_ref, acum, H)
        dtx = dtx_ref[...]
        acx = acx_ref[...]
        lastx = acx[T - 1:T, :]
        ex = jnp.exp(acx)
        decx = jnp.exp(lastx - acx)
        elx = jnp.exp(lastx)
        xd = xs * dtx
        xb_ref[...] = xd.astype(BF16)
        xdecb = (xd * decx).astype(BF16)
        dgb = (ex * dy).astype(BF16)
        rows_ref[...] = jnp.zeros_like(rows_ref)

        lane = lax.broadcasted_iota(jnp.int32, (T, LANE), 1)
        sub = lax.broadcasted_iota(jnp.int32, (T, LANE), 0)
        subr = lax.broadcasted_iota(jnp.int32, (LANE, T), 0)
        da_col = jnp.zeros((T, LANE), F32)
        da_row = jnp.zeros((LANE, T), F32)
        dbs, dcs = [], []
        for g in range(G):
            gs = slice(g * GW, (g + 1) * GW)
            bgb = act[:, W + g * N:W + (g + 1) * N].astype(BF16)
            cgb = act[:, W + G * N + g * N:W + G * N + (g + 1) * N].astype(BF16)
            stg = sp_ref[0, :, gs]
            stb = stg.astype(BF16)
            dsn = ds_ref[:, gs]
            dsnb = dsn.astype(BF16)
            gm = _dot(cgb, stb)
            dc = _dot_nt(dgb[:, gs], stb)
            dsp = _dot_tn(cgb, dgb[:, gs])
            dxs_ = decx[:, gs] * _dot(bgb, dsnb)
            db = _dot_nt(xdecb[:, gs], dsnb)
            xdg = xd[:, gs]
            r12_ref[:, gs] = dy[:, gs] * ex[:, gs] * gm - xdg * dxs_
            rows_ref[0:1, gs] = _colsum(dsn * stg) * elx[:, gs]
            rows_ref[1:2, gs] = _colsum(xdg * dxs_)
            ds_ref[:, gs] = dsp + dsn * elx[:, gs]
            cb_g = _dot_nt(cgb, bgb)
            dcb = jnp.zeros((T, T), F32)
            for r in range(R):
                h = g * R + r
                hs = slice(h * HEAD_DIM, (h + 1) * HEAD_DIM)
                lm = jnp.where(low, jnp.exp(acum[:, h:h + 1] - acum_t[h:h + 1, :]), 0.0)
                mm = cb_g * lm
                dyb = dyb_ref[:, hs]
                dm = _dot_nt(dyb, xb_ref[:, hs])
                dx_ref[:, hs] = dxs_[:, r * HEAD_DIM:(r + 1) * HEAD_DIM] + _dot_tn(mm.astype(BF16), dyb)
                dcb = dcb + dm * lm
                qm = dm * mm
                da_col = jnp.where(lane == h, jnp.sum(qm, axis=1, keepdims=True), da_col)
                da_row = jnp.where(subr == h, jnp.sum(qm, axis=0, keepdims=True), da_row)
            dcbb = dcb.astype(BF16)
            dcs.append(dc + _dot(dcbb, bgb))
            dbs.append(db + _dot_tn(dcbb, cgb))
        dx = dx_ref[...]
        rows = _head_reduce(rows_ref[...])
        dlast = rows[0:1, :] + rows[1:2, :]
        da_col = da_col + _head_reduce(r12_ref[...]) + jnp.where(sub == T - 1, dlast, 0.0)
        dacum = da_col - da_row.T
        da = _dot_hi(upf, dacum)
        ddt = _head_reduce(dx * xs) + da * a_neg
        hacc_ref[1:2, :] += _colsum(da * dt) * a_neg
        ddt_raw = ddt * _sigmoid(dt_in)
        hacc_ref[0:1, :] += _colsum(ddt_raw)
        ddt_ref[...] = ddt_raw
        dact = jnp.concatenate([dsk_ref[...] * dy + dx * dtx] + dbs + dcs, axis=1)
        dpre_ref[...] = dact * (spre * (1.0 + pre * (1.0 - spre)))

        @pl.when(i == nc - 1)
        def _():
            ch = lax.broadcasted_iota(jnp.int32, (W, LANE), 0)
            lo = lax.broadcasted_iota(jnp.int32, (W, LANE), 1) * HEAD_DIM
            hacc_ref[2:3, :] = _dot_hi(acc_ref[1:2, :], ((ch >= lo) & (ch < lo + HEAD_DIM)).astype(F32))

    rev = lambda i: (nc - 1 - i, 0)
    return pl.pallas_call(
        body, name="ssd_bwd", grid=(nc,),
        in_specs=[pl.BlockSpec((T, W), rev), pl.BlockSpec((T, W), rev), pl.BlockSpec((T, W), rev), pl.BlockSpec((T, CD), rev),
                  pl.BlockSpec((HALO, CD), lambda i: (jnp.maximum((nc - 1 - i) * (T // HALO) - 1, 0), 0)),
                  pl.BlockSpec((T, LANE), rev), pl.BlockSpec((1, N, HP), lambda i: (nc - 1 - i, 0, 0)),
                  _acc((CONV_K, CD)), _acc((1, CD)), _acc((1, LANE)), _acc((1, LANE)), _acc((1, W)), _acc((1, W))],
        out_specs=(pl.BlockSpec((T, W), rev), pl.BlockSpec((T, CD), rev), pl.BlockSpec((T, LANE), rev), _acc((8, W)),
                   _acc((8, LANE))),
        out_shape=(_S((L, W), BF16), _S((L, CD), F32), _S((L, LANE), F32), _S((8, W), F32), _S((8, LANE), F32)),
        scratch_shapes=[pltpu.VMEM((T + HALO, CD), F32), pltpu.VMEM((N, HP), F32), pltpu.VMEM((T, W), F32),
                        pltpu.VMEM((T, W), F32), pltpu.VMEM((T, W), BF16), pltpu.VMEM((T, W), BF16), pltpu.VMEM((T, W), F32),
                        pltpu.VMEM((T, W), F32), pltpu.VMEM((8, W), F32)],
        compiler_params=_params(("arbitrary",)))(dyn, y, z, xbc, xbc, dt_raw, sprev, cw, cb, dtb, alog, dsk, nw)


def _conv_bwd(dpre, xbc, cw):
    L, CD = xbc.shape
    tm = _pick(L, (CONV_TM, 128))
    cb = CD if CONV_CB >= CD else _pick(CD, (CONV_CB, 128))
    nt = L // tm
    hb = tm // HALO

    def body(dp_ref, dn_ref, u_ref, cw_ref, du_ref, acc_ref, extd_ref):
        i = pl.program_id(1)

        @pl.when(i == 0)
        def _():
            acc_ref[...] = jnp.zeros_like(acc_ref)

        extd_ref[0:tm, :] = dp_ref[...]
        extd_ref[tm:tm + HALO, :] = jnp.where(i == nt - 1, 0.0, dn_ref[...])
        for c in range(tm // ROW_CHUNK):
            r0 = c * ROW_CHUNK
            rows = slice(r0, r0 + ROW_CHUNK)
            dp = dp_ref[rows, :]
            u = u_ref[rows, :]
            du = cw_ref[CONV_K - 1:CONV_K, :] * dp
            acc_ref[CONV_K - 1:CONV_K, :] += _colsum(dp * u)
            for k in range(CONV_K - 1):
                s = CONV_K - 1 - k
                dsh = extd_ref[r0 + s:r0 + s + ROW_CHUNK, :]
                du = du + cw_ref[k:k + 1, :] * dsh
                acc_ref[k:k + 1, :] += _colsum(u * dsh)
            acc_ref[CONV_K:CONV_K + 1, :] += _colsum(dp)
            du_ref[rows, :] = du.astype(BF16)

    tile = pl.BlockSpec((tm, cb), lambda j, i: (i, j))
    return pl.pallas_call(
        body, name="conv_bwd", grid=(CD // cb, nt),
        in_specs=[tile, pl.BlockSpec((HALO, cb), lambda j, i: (jnp.minimum((i + 1) * hb, nt * hb - 1), j)),
                  tile, pl.BlockSpec((CONV_K, cb), lambda j, i: (0, j))],
        out_specs=(tile, pl.BlockSpec((8, cb), lambda j, i: (0, j))),
        out_shape=(_S((L, CD), BF16), _S((8, CD), F32)),
        scratch_shapes=[pltpu.VMEM((tm + HALO, cb), F32)],
        compiler_params=_params(("arbitrary", "arbitrary")))(dpre, dpre, xbc, cw)


def _in_proj_bwd(dz, dxbc, dq, dkv, ddt, w_t, xhat0, rstd0, dh0p, vecs, dims):
    L, D = xhat0.shape
    W, CD, AW, KVW2 = dims["W"], dims["CD"], dims["AW"], dims["KVW2"]
    PROJ = w_t.shape[0]
    tm = _pick(L, (MLP_TM, 128))
    r_z, r_xbc, r_dt, r_q, r_kv = _proj_rows(dims)

    def body(dz_ref, dxbc_ref, dq_ref, dkv_ref, ddt_ref, w_ref, xh_ref, rs_ref, dh0_ref, v_ref, gx_ref, acc_ref):
        @pl.when(pl.program_id(0) == 0)
        def _():
            acc_ref[...] = jnp.zeros_like(acc_ref)

        du1 = _dot(dz_ref[...], w_ref[r_z[0]:r_z[1], :])
        du1 = du1 + _dot(dxbc_ref[...], w_ref[r_xbc[0]:r_xbc[1], :])
        du1 = du1 + _dot(dq_ref[...], w_ref[r_q[0]:r_q[1], :])
        du1 = du1 + _dot(dkv_ref[...].astype(BF16), w_ref[r_kv[0]:r_kv[1], :])
        du1 = du1 + _dot(ddt_ref[...].astype(BF16), w_ref[r_dt[0]:r_dt[1], :])
        xhat0 = xh_ref[...]
        h0 = xhat0 * v_ref[0:1, :] + v_ref[1:2, :]
        acc_ref[0:1, :] += _colsum(du1 * h0)
        acc_ref[1:2, :] += _colsum(du1)
        dh0 = dh0_ref[...] + du1 * (1.0 + v_ref[2:3, :])
        acc_ref[2:3, :] += _colsum(dh0 * xhat0)
        acc_ref[3:4, :] += _colsum(dh0)
        gx_ref[...] = _ln_bwd(dh0 * v_ref[0:1, :], xhat0, rs_ref[...])

    return pl.pallas_call(
        body, name="in_proj_bwd", grid=(L // tm,),
        in_specs=[_tile(W, tm), _tile(CD, tm), _tile(AW, tm), _tile(KVW2, tm), _tile(LANE, tm), _res((PROJ, D)),
                  _tile(D, tm), _tile(1, tm), _tile(D, tm), _acc((8, D))],
        out_specs=(_tile(D, tm), _acc((8, D))),
        out_shape=(_S((L, D), F32), _S((8, D), F32)),
        compiler_params=_params(("arbitrary",)))(dz, dxbc, dq, dkv, ddt, w_t, xhat0, rstd0, dh0p, vecs)


_WEIGHTS = ['ln_in_g', 'ln_in_b', 'ada_w', 'ada_b', 'w_in', 'conv_w', 'conv_b', 'dt_bias', 'a_log', 'd_skip', 'ssd_norm_w',
            'attn_sinks', 'w_out', 'ln1_g', 'ln1_b', 'w_ff1', 'b_ff1', 'w_ff2', 'b_ff2', 'ln2_g', 'ln2_b']
_BIG = ('w_in', 'w_out', 'w_ff1', 'w_ff2')
_SMALL = ('ada_b', 'ln_in_g', 'ln_in_b', 'conv_b', 'dt_bias', 'a_log', 'd_skip', 'ssd_norm_w', 'attn_sinks', 'ln1_g', 'ln1_b',
          'b_ff1', 'b_ff2', 'ln2_g', 'ln2_b')


def _pad_lanes(v, n=None):
    v = v.reshape(1, -1)
    n = n or -(-v.shape[1] // LANE) * LANE
    return jnp.pad(v, ((0, 0), (0, n - v.shape[1])))


def _vec8(rows, D):
    rows = [r.reshape(1, D) for r in rows]
    return jnp.concatenate(rows + [jnp.zeros((8 - len(rows), D), F32)], axis=0)


def _pack(segs):
    flat, offs, sizes, o = [], [], [], 0
    for s in segs:
        p = _pad_lanes(s)
        flat.append(p)
        offs.append(o)
        sizes.append(s.size)
        o += p.shape[1]
    total = -(-o // (8 * LANE)) * (8 * LANE)
    if total > o:
        flat.append(jnp.zeros((1, total - o), F32))
    return jnp.concatenate(flat, axis=1).reshape(8, total // 8), offs, sizes


def kernel(x, c, ln_in_g, ln_in_b, ada_w, ada_b, w_in, conv_w, conv_b, dt_bias, a_log, d_skip, ssd_norm_w, attn_sinks, w_out, ln1_g, ln1_b, w_ff1, b_ff1, w_ff2, b_ff2, ln2_g, ln2_b, loss_target, m_ln_in_g, m_ln_in_b, m_ada_w, m_ada_b, m_w_in, m_conv_w, m_conv_b, m_dt_bias, m_a_log, m_d_skip, m_ssd_norm_w, m_attn_sinks, m_w_out, m_ln1_g, m_ln1_b, m_w_ff1, m_b_ff1, m_w_ff2, m_b_ff2, m_ln2_g, m_ln2_b, v_ln_in_g, v_ln_in_b, v_ada_w, v_ada_b, v_w_in, v_conv_w, v_conv_b, v_dt_bias, v_a_log, v_d_skip, v_ssd_norm_w, v_attn_sinks, v_w_out, v_ln1_g, v_ln1_b, v_w_ff1, v_b_ff1, v_w_ff2, v_b_ff2, v_ln2_g, v_ln2_b):
    wts = dict(ln_in_g=ln_in_g, ln_in_b=ln_in_b, ada_w=ada_w, ada_b=ada_b, w_in=w_in, conv_w=conv_w, conv_b=conv_b,
               dt_bias=dt_bias, a_log=a_log, d_skip=d_skip, ssd_norm_w=ssd_norm_w, attn_sinks=attn_sinks, w_out=w_out,
               ln1_g=ln1_g, ln1_b=ln1_b, w_ff1=w_ff1, b_ff1=b_ff1, w_ff2=w_ff2, b_ff2=b_ff2, ln2_g=ln2_g, ln2_b=ln2_b)
    ms = dict(ln_in_g=m_ln_in_g, ln_in_b=m_ln_in_b, ada_w=m_ada_w, ada_b=m_ada_b, w_in=m_w_in, conv_w=m_conv_w,
              conv_b=m_conv_b, dt_bias=m_dt_bias, a_log=m_a_log, d_skip=m_d_skip, ssd_norm_w=m_ssd_norm_w,
              attn_sinks=m_attn_sinks, w_out=m_w_out, ln1_g=m_ln1_g, ln1_b=m_ln1_b, w_ff1=m_w_ff1, b_ff1=m_b_ff1,
              w_ff2=m_w_ff2, b_ff2=m_b_ff2, ln2_g=m_ln2_g, ln2_b=m_ln2_b)
    vs = dict(ln_in_g=v_ln_in_g, ln_in_b=v_ln_in_b, ada_w=v_ada_w, ada_b=v_ada_b, w_in=v_w_in, conv_w=v_conv_w,
              conv_b=v_conv_b, dt_bias=v_dt_bias, a_log=v_a_log, d_skip=v_d_skip, ssd_norm_w=v_ssd_norm_w,
              attn_sinks=v_attn_sinks, w_out=v_w_out, ln1_g=v_ln1_g, ln1_b=v_ln1_b, w_ff1=v_w_ff1, b_ff1=v_b_ff1,
              w_ff2=v_w_ff2, b_ff2=v_b_ff2, ln2_g=v_ln2_g, ln2_b=v_ln2_b)

    L, D = x.shape[1], x.shape[2]
    depth = w_in.shape[0]
    assert depth == 1 and x.shape[0] == 1 and L % CHUNK == 0
    W = D
    H = W // HEAD_DIM
    CD = W + 2 * SSD_GROUPS * SSD_STATE
    AW = D
    AH = AW // HEAD_DIM
    KV = AH // GQA
    KVW2 = 2 * KV * HEAD_DIM
    PROJ = W + CD + H + AW + KVW2
    FF = w_ff1.shape[2] * N_DEV
    MIX = w_out.shape[1] * N_DEV
    assert w_in.shape[2] * N_DEV == PROJ and MIX == W + AW and H <= LANE and AH <= LANE
    dims = dict(W=W, H=H, CD=CD, AW=AW, AH=AH, KV=KV, KVW2=KVW2)
    alpha = (2.0 * depth) ** 0.25
    C6 = ada_w.shape[2]
    CW = conv_w.shape[2]

    ax, ay, ac = _my_pos()
    me = 4 * ax + 2 * ay + ac
    x2 = x.reshape(L, D)
    tgt = loss_target.reshape(L, D)
    r1 = lambda a: a.reshape(1, -1)

    ada_b_cols = lax.dynamic_slice(ada_b, (0, me * C6), (1, C6))
    cs_all, mod = _mod_fwd(c, ada_w[0], ada_b_cols)
    sh1, sc1, g1, sh2, sc2, g2 = [r1(t) for t in jnp.split(mod.reshape(-1), 6)]

    wg_in, cwg = _ag_weights([w_in[0].T.astype(BF16), conv_w[0]], cs_all)
    shards2 = [w_out[0].astype(BF16), w_ff1[0].astype(BF16), w_ff2[0].astype(BF16)]
    lands2 = [lax.dynamic_update_slice(lax.empty((N_DEV,) + s.shape, s.dtype), s[None], (me, 0, 0)) for s in shards2]
    ag_ss, ag_rs, ag_arr, ag_token = _split_start(shards2 + lands2, _plan_gather(3), cwg, "ag_ici_start")
    sh1 = sh1 + ag_token[0:1, 0:1]
    w_pad = _merge_blocks(wg_in)
    cw_full = cwg.transpose(1, 0, 2).reshape(CONV_K, CD)

    dtb = _pad_lanes(dt_bias, LANE)
    alog = _pad_lanes(a_log, LANE)
    dsk = jnp.repeat(d_skip.reshape(-1), HEAD_DIM).reshape(1, W)
    sinks = attn_sinks.reshape(-1)
    g_in, b_in = r1(ln_in_g), r1(ln_in_b)

    xhat0, rstd0, u1, z, xbc, q, kv, dt_raw = _ln_in_proj(x2, g_in, b_in, sc1, sh1, w_pad, dims)
    y, yn, sprev = _conv_ssd(xbc, dt_raw, z, cw_full, conv_b, dtb, alog, dsk, ssd_norm_w, dims)
    ag_arr = _split_wait(ag_ss, ag_rs, ag_arr, _plan_gather(3), yn, "ag_ici_wait")
    fw_ss, fw_rs, ag_land, fw_token = _split_start(ag_arr[3:], _plan_forward(3), yn, "ag_fwd_start")
    o = _swa_fwd(q, kv, sinks + fw_token[0, 0], dims)
    wg_out, wg_ff1, wg_ff2 = _split_wait(fw_ss, fw_rs, ag_land, _plan_forward(3), o, "ag_fwd_wait")
    w_out_full = wg_out.reshape(MIX, D)
    w1_full = wg_ff1.transpose(1, 0, 2).reshape(D, FF)
    w2_full = wg_ff2.reshape(FF, D)
    mix, xhat1, rstd1, u2 = _out_proj_ln1(yn, o, w_out_full, xhat0, _vec8([g_in, b_in, g1, ln1_g, ln1_b, sc2, sh2], D), alpha)
    rr, dr2, acc_f, loss_loc = _mlp_loss(u2, w1_full, w2_full, xhat1, tgt,
                                         _vec8([ln1_g, ln1_b, g2, ln2_g, ln2_b, b_ff2], D), b_ff1, alpha)

    df, da, gb2, gb1 = _mlp_bwd_a(dr2, rr, w2_full, g2)
    gw_ff2 = _matmul_tn(rr, df, "gw_ff2", square_a=True)
    gw_ff1t = _matmul_tn(da, u2, "gw_ff1")
    dmix, dh0p, dyn, do, acc_b = _mlp_bwd_b(da, w1_full, dr2, xhat1, rstd1, mix, w_out_full,
                                            _vec8([ln1_g, ln1_b, sc2, g1], D), alpha, W)
    gw_out = jnp.concatenate([_matmul_tn(yn, dmix, "gw_out_ssd"), _matmul_tn(o, dmix, "gw_out_attn")], axis=0)

    core = jnp.reshape(ac, (1,)).astype(jnp.int32)
    blocked1 = [gw_out.reshape(N_DEV, MIX // N_DEV, D), gw_ff1t.reshape(N_DEV, FF // N_DEV, D),
                gw_ff2.reshape(N_DEV, FF // N_DEV, D)]
    lands1 = [lax.empty(b.shape, b.dtype) for b in blocked1]
    rs_ss, rs_rs, rs_arr, rs_token = _split_start(blocked1 + lands1, _plan_scatter_all(3), do, "rs_all_start")
    dq, dkv, dsink = _swa_bwd(q, kv, do, sinks + rs_token[0, 0], dims)
    dz, dpre, ddt, acc_s, hacc = _ssd_bwd(dyn, y, z, xbc, dt_raw, sprev, cw_full, conv_b, dtb + rs_token[0:1, 0:1], alog, dsk,
                                          ssd_norm_w, dims)
    dxbc, acc_c = _conv_bwd(dpre, xbc, cw_full)
    gw_in = _gw_in((dz, dxbc, ddt, dq, dkv), u1, dims)

    blocked2 = [_split_blocks(gw_in, N_DEV)]
    pairs2 = [_pair_sum(b, r, core) for b, r in zip(blocked2, _rs_d2d(blocked2, "rs_d2d_2"))]
    lands2 = [lax.empty(p.shape, p.dtype) for p in pairs2]
    r2_ss, r2_rs, r2_arr, r2_token = _split_start(pairs2 + lands2, _plan_scatter(1), gw_in, "rs_ici_start_2")
    grad_x, acc_i = _in_proj_bwd(dz, dxbc, dq, dkv, ddt, w_pad, xhat0, rstd0, dh0p,
                                 _vec8([g_in, b_in, sc1], D) + r2_token[0:1, 0:1], dims)

    srcs = [acc_i, acc_b, acc_f, acc_s, acc_c, hacc, dsink, gb1, gb2, loss_loc]
    I_, B_, F_, S_, C_, H_, K_, G1_, G2_, L_ = range(10)
    seg_of = dict(ada_b=[(I_, 1, D), (I_, 0, D), (B_, 4, D), (B_, 1, D), (B_, 0, D), (F_, 2, D)],
                  ln_in_g=[(I_, 2, D)], ln_in_b=[(I_, 3, D)], conv_b=[(C_, CONV_K, CD)], dt_bias=[(H_, 0, H)],
                  a_log=[(H_, 1, H)], d_skip=[(H_, 2, H)], ssd_norm_w=[(S_, 0, W)], attn_sinks=[(K_, 0, AH)],
                  ln1_g=[(B_, 2, D)], ln1_b=[(B_, 3, D)], b_ff1=[(G1_, 0, FF)], b_ff2=[(G2_, 0, D)],
                  ln2_g=[(F_, 0, D)], ln2_b=[(F_, 1, D)])
    pieces = [seg_of[n] for n in _SMALL] + [[(C_, t, CD) for t in range(CONV_K)], [(L_, 0, 1)]]
    params = [tuple(t[n].reshape(1, -1) for t in (wts, ms, vs)) for n in _SMALL]
    res = _small_sync_adamw(srcs, pieces, params, 6 * D)
    grads, deltas, new_m, new_v = {}, {}, {}, {}
    for k, n in enumerate(_SMALL):
        grads[n], deltas[n], new_m[n], new_v[n] = (t.reshape(wts[n].shape) for t in res[4 * k:4 * k + 4])
    gcw_full, dmod_all, loss_row = res[4 * len(_SMALL):]
    loss = loss_row[0, 0]

    g_ = lax.dynamic_slice(gcw_full, (0, me * CW), (CONV_K, CW))
    d_, m_, v_ = _adamw(conv_w[0], g_, m_conv_w[0], v_conv_w[0])
    grads['conv_w'], deltas['conv_w'], new_m['conv_w'], new_v['conv_w'] = (t[None] for t in (g_, d_, m_, v_))

    dmod_cols = lax.dynamic_slice(dmod_all, (0, me * C6), (N_DEV, C6))
    pad16 = lambda t: jnp.concatenate([t, jnp.zeros((16 - N_DEV,) + t.shape[1:], t.dtype)], axis=0)
    g_, d_, m_, v_ = _ada_grad_adamw(pad16(cs_all), pad16(dmod_cols), ada_w[0], m_ada_w[0], v_ada_w[0])
    grads['ada_w'], deltas['ada_w'], new_m['ada_w'], new_v['ada_w'] = (t[None] for t in (g_, d_, m_, v_))

    rs_arr = _split_wait(rs_ss, rs_rs, rs_arr, _plan_scatter_all(3), g_, "rs_all_wait")
    mychip = 2 * ax + ay
    chips = jnp.stack([(mychip + k) % N_CHIP for k in range(N_CHIP)]).astype(jnp.int32)
    devs = jnp.stack([(me + k) % N_DEV for k in range(N_DEV)]).astype(jnp.int32)
    for n, own, land in zip(('w_out', 'w_ff1', 'w_ff2'), rs_arr[:3], rs_arr[3:]):
        g_, d_, m_, v_ = _sum_adamw_split(own, land, devs, wts[n][0], ms[n][0], vs[n][0], transposed=(n == 'w_ff1'))
        grads[n], deltas[n], new_m[n], new_v[n] = (t[None] for t in (g_, d_, m_, v_))
    r2_arr = _split_wait(r2_ss, r2_rs, r2_arr, _plan_scatter(1), g_, "rs_ici_wait_2")
    g_, d_, m_, v_ = _sum_adamw_split(r2_arr[0], r2_arr[1], chips, wts['w_in'][0].T, ms['w_in'][0].T, vs['w_in'][0].T)
    grads['w_in'], deltas['w_in'], new_m['w_in'], new_v['w_in'] = (t.T[None] for t in (g_, d_, m_, v_))

    return (loss, grad_x.reshape(x.shape), *[grads[n] for n in _WEIGHTS], *[deltas[n] for n in _WEIGHTS],
            *[new_m[n] for n in _WEIGHTS], *[new_v[n] for n in _WEIGHTS])
```

```python
import functools
import math

import numpy as np
import jax
import jax.numpy as jnp
from jax import lax
from jax.experimental import pallas as pl
from jax.experimental.pallas import tpu as pltpu

F32 = jnp.float32
BF16 = jnp.bfloat16
MESH = pl.DeviceIdType.MESH

N_DEV = 8
N_CHIP = 4
HEAD_DIM = 64
SSD_GROUPS = 2
SSD_STATE = 128
CHUNK = 128
CONV_K = 4
GQA = 8
LANE = 128
HALO = 8
LN_EPS = 1e-5
RMS_EPS = 1e-5
NEG = -1e30
ADAM_LR, ADAM_B1, ADAM_B2, ADAM_EPS, ADAM_WD, ADAM_STEP = 0.001, 0.9, 0.999, 1e-08, 0.01, 10
V7X_VMEM_BYTES = 64 * 1024 * 1024
VMEM_LIMIT = V7X_VMEM_BYTES - 8 * 1024 * 1024
HI = lax.Precision.HIGHEST
MLP_TM = 512
MLP_SUB = 512
MLP_FC = 512
CONV_TM = 512
CONV_CB = 2048
ROW_CHUNK = 32


def _alibi_slopes(n):
    def pow2(m):
        start = 2.0 ** (-8.0 / m)
        return [start ** (i + 1) for i in range(m)]
    if math.log2(n).is_integer():
        s = pow2(n)
    else:
        c = 2 ** math.floor(math.log2(n))
        s = pow2(c) + pow2(2 * c)[0::2][: n - c]
    return [float(v) for v in np.array(s, dtype=np.float32)]


def _dot(a, b):
    return jnp.dot(a, b, preferred_element_type=F32)


def _dot_nt(a, b):
    return lax.dot_general(a, b, (((1,), (1,)), ((), ())), preferred_element_type=F32)


def _dot_tn(a, b):
    return lax.dot_general(a, b, (((0,), (0,)), ((), ())), preferred_element_type=F32)


def _dot_hi(a, b):
    return jnp.dot(a, b, precision=HI, preferred_element_type=F32)


def _sigmoid(x):
    return 0.5 * jnp.tanh(0.5 * x) + 0.5


def _softplus(x):
    return jnp.maximum(x, 0.0) + jnp.log(1.0 + jnp.exp(-jnp.abs(x)))


def _mean(x):
    return jnp.mean(x, axis=-1, keepdims=True)


def _ln_fwd(x):
    xc = x - _mean(x)
    rstd = lax.rsqrt(_mean(xc * xc) + LN_EPS)
    return xc * rstd, rstd


def _ln_bwd(dxhat, xhat, rstd):
    return rstd * (dxhat - _mean(dxhat) - xhat * _mean(dxhat * xhat))


def _colsum(x):
    return jnp.sum(x, axis=0, keepdims=True)


def _params(sem):
    return pltpu.CompilerParams(dimension_semantics=sem, vmem_limit_bytes=VMEM_LIMIT)


def _tile(i_map_cols, tm):
    return pl.BlockSpec((tm, i_map_cols), lambda i: (i, 0))


def _res(shape):
    return pl.BlockSpec(shape, lambda *_: (0,) * len(shape), pipeline_mode=pl.Buffered(1))


def _acc(shape):
    return pl.BlockSpec(shape, lambda *_: (0,) * len(shape))


def _S(shape, dtype):
    return jax.ShapeDtypeStruct(shape, dtype)


def _my_pos():
    return lax.axis_index("x"), lax.axis_index("y"), lax.axis_index("c")


def _peer(pos, k):
    x, y, c = pos
    px = 1 - x if k & 4 else x
    py = 1 - y if k & 2 else y
    pc = 1 - c if k & 1 else c
    return (px, py, pc)


def _lin(p):
    return 4 * p[0] + 2 * p[1] + p[2]


def _mod_fwd(c_loc, ada_w_loc, ada_b_cols):
    D = c_loc.shape[1]
    C6 = ada_w_loc.shape[1]

    def body(c_ref, w_ref, b_ref, cs_ref, mod_ref, call_ref, modp_ref, ssem, rsem):
        pos = _my_pos()
        me = _lin(pos)
        call_ref[me] = c_ref[...]
        sends = []
        for k in range(1, N_DEV):
            cp = pltpu.make_async_remote_copy(src_ref=c_ref, dst_ref=call_ref.at[me], send_sem=ssem.at[k - 1],
                                              recv_sem=rsem.at[k - 1], device_id=_peer(pos, k), device_id_type=MESH)
            cp.start()
            sends.append(cp)
        for k in range(1, N_DEV):
            src = _lin(_peer(pos, k))
            pltpu.make_async_remote_copy(src_ref=c_ref, dst_ref=call_ref.at[src], send_sem=ssem.at[k - 1],
                                         recv_sem=rsem.at[k - 1], device_id=pos, device_id_type=MESH).wait_recv()
        for cp in sends:
            cp.wait_send()
        call = jnp.concatenate([call_ref[b] for b in range(N_DEV)], axis=0)
        cs = call * _sigmoid(call)
        cs_ref[...] = cs
        modp = _dot(cs.astype(BF16), w_ref[...].astype(BF16)) + b_ref[...]
        for b in range(N_DEV):
            modp_ref[b] = modp[b:b + 1, :]
        mod_ref[me] = modp_ref[me]
        sends = []
        for k in range(1, N_DEV):
            peer = _peer(pos, k)
            cp = pltpu.make_async_remote_copy(src_ref=modp_ref.at[_lin(peer)], dst_ref=mod_ref.at[me],
                                              send_sem=ssem.at[N_DEV - 2 + k], recv_sem=rsem.at[N_DEV - 2 + k],
                                              device_id=peer, device_id_type=MESH)
            cp.start()
            sends.append(cp)
        for k in range(1, N_DEV):
            src = _lin(_peer(pos, k))
            pltpu.make_async_remote_copy(src_ref=modp_ref.at[src], dst_ref=mod_ref.at[src],
                                         send_sem=ssem.at[N_DEV - 2 + k], recv_sem=rsem.at[N_DEV - 2 + k],
                                         device_id=pos, device_id_type=MESH).wait_recv()
        for cp in sends:
            cp.wait_send()

    vm = pl.BlockSpec(memory_space=pltpu.VMEM)
    return pl.pallas_call(
        body, name="mod_fwd",
        out_shape=(_S((N_DEV, D), F32), _S((N_DEV, 1, C6), F32)),
        in_specs=[vm, vm, vm], out_specs=(vm, vm),
        scratch_shapes=[pltpu.VMEM((N_DEV, 1, D), F32), pltpu.VMEM((N_DEV, 1, C6), F32),
                        pltpu.SemaphoreType.DMA((2 * (N_DEV - 1),)), pltpu.SemaphoreType.DMA((2 * (N_DEV - 1),))],
        compiler_params=pltpu.CompilerParams(vmem_limit_bytes=VMEM_LIMIT),
    )(c_loc, ada_w_loc, ada_b_cols)


def _small_sync_adamw(srcs, pieces, params, n_mod):
    n_src, n_par = len(srcs), len(params)
    rows_of = [sum(-(-w // LANE) for _, _, w in seg) for seg in pieces]
    starts = [sum(rows_of[:k]) for k in range(len(pieces))]
    NR = -(-sum(rows_of) // 8) * 8
    cd = pieces[n_par][0][2]

    def seg_row(arr, k, width):
        r = starts[k]
        if width <= LANE:
            return arr[r:r + 1, 0:width]
        return jnp.concatenate([arr[r + q:r + q + 1, :] for q in range(width // LANE)], axis=1)

    def exchange(*refs):
        src = refs[:n_src]
        total_ref, dmod_ref, pack_ref, gat_ref, ssem, rsem = refs[n_src:]
        pos = _my_pos()
        me = _lin(pos)
        pack_ref[...] = jnp.zeros_like(pack_ref)
        for k, seg in enumerate(pieces):
            r = starts[k]
            for (si, row, width) in seg:
                for q in range(-(-width // LANE)):
                    wq = min(LANE, width - q * LANE)
                    pack_ref[r:r + 1, 0:wq] = src[si][row:row + 1, q * LANE:q * LANE + wq]
                    r += 1
        gat_ref[me] = pack_ref[...]
        sends = []
        for k in range(1, N_DEV):
            cp = pltpu.make_async_remote_copy(src_ref=pack_ref, dst_ref=gat_ref.at[me], send_sem=ssem.at[k - 1],
                                              recv_sem=rsem.at[k - 1], device_id=_peer(pos, k), device_id_type=MESH)
            cp.start()
            sends.append(cp)
        for k in range(1, N_DEV):
            frm = _lin(_peer(pos, k))
            pltpu.make_async_remote_copy(src_ref=pack_ref, dst_ref=gat_ref.at[frm], send_sem=ssem.at[k - 1],
                                         recv_sem=rsem.at[k - 1], device_id=pos, device_id_type=MESH).wait_recv()
        for cp in sends:
            cp.wait_send()
        total = gat_ref[0]
        for j in range(1, N_DEV):
            total = total + gat_ref[j]
        total_ref[...] = total
        for j in range(N_DEV):
            dmod_ref[j:j + 1, :] = seg_row(gat_ref[j], 0, n_mod)

    def update(*refs):
        total = refs[0][...]
        wmv = refs[1:1 + 3 * n_par]
        outs = refs[1 + 3 * n_par:]
        for k in range(n_par):
            n = params[k][0].shape[1]
            g = seg_row(total, k, n)
            w_ref, m_ref, v_ref = wmv[3 * k:3 * k + 3]
            g_ref, d_ref, m2_ref, v2_ref = outs[4 * k:4 * k + 4]
            g_ref[...] = g
            d_ref[...], m2_ref[...], v2_ref[...] = _adamw_math(w_ref[...], g, m_ref[...], v_ref[...])
        gcw_ref, loss_ref = outs[4 * n_par:]
        for t in range(CONV_K):
            r = starts[n_par] + t * (cd // LANE)
            gcw_ref[t:t + 1, :] = jnp.concatenate([total[r + q:r + q + 1, :] for q in range(cd // LANE)], axis=1)
        loss_ref[...] = total[starts[n_par + 1]:starts[n_par + 1] + 1, :]

    vm = pl.BlockSpec(memory_space=pltpu.VMEM)
    total, dmod_all = pl.pallas_call(
        exchange, name="small_sync", out_shape=(_S((NR, LANE), F32), _S((N_DEV, n_mod), F32)),
        in_specs=[vm] * n_src, out_specs=(vm, vm),
        scratch_shapes=[pltpu.VMEM((NR, LANE), F32), pltpu.VMEM((N_DEV, NR, LANE), F32),
                        pltpu.SemaphoreType.DMA((N_DEV - 1,)), pltpu.SemaphoreType.DMA((N_DEV - 1,))],
        compiler_params=pltpu.CompilerParams(vmem_limit_bytes=VMEM_LIMIT),
    )(*srcs)
    out_shape = []
    for w, _, _ in params:
        out_shape += [_S(w.shape, F32)] * 4
    out_shape += [_S((CONV_K, cd), F32), _S((1, LANE), F32)]
    flat = [t for p in params for t in p]
    res = pl.pallas_call(
        update, name="small_adamw", out_shape=tuple(out_shape),
        in_specs=[vm] * (1 + 3 * n_par), out_specs=tuple([vm] * len(out_shape)),
        compiler_params=pltpu.CompilerParams(vmem_limit_bytes=VMEM_LIMIT),
    )(total, *flat)
    return (*res[:-1], dmod_all, res[-1])


def _ag_weights(shards, after):
    n = len(shards)

    def body(*refs):
        ins, outs = refs[:n], refs[n + 1:2 * n + 1]
        ssem, rsem, lsem = refs[2 * n + 1:]
        x, y, c = pos = _my_pos()
        me = _lin(pos)
        sib = (x, y, 1 - c)
        chips = [(1 - x, y), (x, 1 - y), (1 - x, 1 - y)]

        def copy(a, k, block, to, src=None):
            return pltpu.make_async_remote_copy(
                src_ref=outs[a].at[block] if src is None else src, dst_ref=outs[a].at[block],
                send_sem=ssem.at[a * 7 + k], recv_sem=rsem.at[a * 7 + k], device_id=to, device_id_type=MESH)

        local = [pltpu.make_async_copy(ins[a], outs[a].at[me], lsem.at[a]) for a in range(n)]
        for cp in local:
            cp.start()
        first = []
        for a in range(n):
            first.append(copy(a, 0, me, sib, src=ins[a]))
            first += [copy(a, 1 + j, me, (*chip, c), src=ins[a]) for j, chip in enumerate(chips)]
        for cp in first:
            cp.start()
        passed = []
        for a in range(n):
            for j, chip in enumerate(chips):
                blk = _lin((*chip, c))
                copy(a, 1 + j, blk, pos).wait_recv()
                cp = copy(a, 4 + j, blk, sib)
                cp.start()
                passed.append(cp)
        for a in range(n):
            copy(a, 0, _lin(sib), pos).wait_recv()
            for j, chip in enumerate(chips):
                copy(a, 4 + j, _lin((*chip, 1 - c)), pos).wait_recv()
        for cp in first + passed:
            cp.wait_send()
        for cp in local:
            cp.wait()

    hbm = pl.BlockSpec(memory_space=pl.ANY)
    return pl.pallas_call(
        body, name="ag_weights",
        out_shape=tuple(_S((N_DEV,) + s.shape, s.dtype) for s in shards),
        in_specs=[hbm] * (n + 1), out_specs=tuple([hbm] * n),
        scratch_shapes=[pltpu.SemaphoreType.DMA((7 * n,)), pltpu.SemaphoreType.DMA((7 * n,)),
                        pltpu.SemaphoreType.DMA((n,))],
    )(*shards, after)


def _rs_d2d(blocked, name):
    n = len(blocked)

    def body(*refs):
        ins, outs = refs[:n], refs[n:2 * n]
        ssem, rsem = refs[2 * n:]
        x, y, c = pos = _my_pos()
        sib = (x, y, 1 - c)
        cps = []
        for a in range(n):
            for j in range(N_CHIP):
                cp = pltpu.make_async_remote_copy(
                    src_ref=ins[a].at[2 * j + (1 - c)], dst_ref=outs[a].at[j], send_sem=ssem.at[a * N_CHIP + j],
                    recv_sem=rsem.at[a * N_CHIP + j], device_id=sib, device_id_type=MESH)
                cp.start()
                cps.append(cp)
        for cp in cps:
            cp.wait_recv()
        for cp in cps:
            cp.wait_send()

    hbm = pl.BlockSpec(memory_space=pl.ANY)
    return pl.pallas_call(
        body, name=name,
        out_shape=tuple(_S((N_CHIP,) + b.shape[1:], b.dtype) for b in blocked),
        in_specs=[hbm] * n, out_specs=tuple([hbm] * n),
        scratch_shapes=[pltpu.SemaphoreType.DMA((N_CHIP * n,)), pltpu.SemaphoreType.DMA((N_CHIP * n,))],
    )(*blocked)


_HBM = pl.BlockSpec(memory_space=pltpu.HBM)
_SEM = pl.BlockSpec(memory_space=pltpu.SEMAPHORE)
_ANY = pl.BlockSpec(memory_space=pl.ANY)
_EFFECT = pltpu.SideEffectType.DATAFLOW_SIDE_EFFECTING


def _in_hbm(a):
    return pltpu.with_memory_space_constraint(a, pltpu.HBM)


def _plan_gather(n):
    def copies(pos):
        x, y, c = pos
        out = []
        for a in range(n):
            for dev in [(x, y, 1 - c)] + [(*_peer(pos, 2 * k)[:2], c) for k in range(1, N_CHIP)]:
                out.append((a, None, n + a, _lin(pos), dev, _lin(dev)))
        return out
    return copies


def _plan_forward(n):
    def copies(pos):
        x, y, c = pos
        out = []
        for a in range(n):
            for k in range(1, N_CHIP):
                tx, ty, _ = _peer(pos, 2 * k)
                out.append((a, _lin((tx, ty, c)), a, _lin((tx, ty, c)), (x, y, 1 - c), _lin((tx, ty, 1 - c))))
        return out
    return copies


def _plan_scatter_all(n):
    def copies(pos):
        out = []
        for a in range(n):
            for k in range(1, N_DEV):
                dev = _peer(pos, k)
                out.append((a, _lin(dev), n + a, _lin(pos), dev, _lin(dev)))
        return out
    return copies


def _plan_scatter(n):
    def copies(pos):
        x, y, c = pos
        out = []
        for a in range(n):
            for k in range(1, N_CHIP):
                tx, ty, _ = _peer(pos, 2 * k)
                out.append((a, 2 * tx + ty, n + a, 2 * x + y, (tx, ty, c), 2 * tx + ty))
        return out
    return copies


def _split_copy(refs, cp, ssem, rsem, i, arrival):
    si, s_slot, di, d_slot, dev, a_slot = cp
    return pltpu.make_async_remote_copy(
        src_ref=refs[si] if s_slot is None else refs[si].at[s_slot], dst_ref=refs[di].at[a_slot if arrival else d_slot],
        send_sem=ssem.at[i], recv_sem=rsem.at[i], device_id=dev, device_id_type=MESH)


def _split_start(arrays, copies, after, name):
    n = len(arrays)
    n_cp = len(copies((0, 0, 0)))

    def body(*refs):
        ssem, rsem, token = refs[n + 1], refs[n + 2], refs[-1]
        for i, cp in enumerate(copies(_my_pos())):
            _split_copy(refs, cp, ssem, rsem, i, False).start()
        token[...] = jnp.zeros_like(token)

    res = pl.pallas_call(
        body, name=name,
        out_shape=(pltpu.SemaphoreType.DMA((n_cp,)), pltpu.SemaphoreType.DMA((n_cp,)),
                   *[pltpu.HBM(a.shape, a.dtype) for a in arrays], _S((8, LANE), F32)),
        in_specs=[_HBM] * n + [_ANY],
        out_specs=(_SEM, _SEM, *[_HBM] * n, pl.BlockSpec(memory_space=pltpu.VMEM)),
        input_output_aliases={a: 2 + a for a in range(n)},
        compiler_params=pltpu.CompilerParams(has_side_effects=_EFFECT),
    )(*[_in_hbm(a) for a in arrays], after)
    return res[0], res[1], list(res[2:2 + n]), res[-1]


def _split_wait(ssem, rsem, arrays, copies, after, name):
    n = len(arrays)

    def body(*refs):
        for i, cp in enumerate(copies(_my_pos())):
            d = _split_copy(refs, cp, refs[n], refs[n + 1], i, True)
            d.wait_send()
            d.wait_recv()

    res = pl.pallas_call(
        body, name=name,
        out_shape=tuple(pltpu.HBM(a.shape, a.dtype) for a in arrays),
        in_specs=[_HBM] * n + [_SEM, _SEM, _ANY], out_specs=tuple([_HBM] * n),
        input_output_aliases={a: a for a in range(n)},
        compiler_params=pltpu.CompilerParams(has_side_effects=_EFFECT),
    )(*arrays, ssem, rsem, after)
    return list(res)


def _row_tile(R, itemsize_rows=16, cap=256):
    t = cap - cap % itemsize_rows
    while t >= itemsize_rows:
        if R % t == 0:
            return t
        t -= itemsize_rows
    return R


def _pair_sum(blocked, recv, core):
    _, R, C = blocked.shape
    tr = _row_tile(R)

    def body(ids_ref, a_ref, b_ref, o_ref):
        del ids_ref
        o_ref[...] = (a_ref[...].astype(F32) + b_ref[...].astype(F32)).astype(BF16)

    gs = pltpu.PrefetchScalarGridSpec(
        num_scalar_prefetch=1, grid=(N_CHIP, R // tr),
        in_specs=[pl.BlockSpec((1, tr, C), lambda j, r, ids: (2 * j + ids[0], r, 0)),
                  pl.BlockSpec((1, tr, C), lambda j, r, ids: (j, r, 0))],
        out_specs=pl.BlockSpec((1, tr, C), lambda j, r, ids: (j, r, 0)))
    return pl.pallas_call(body, name="pair_sum", grid_spec=gs, out_shape=_S((N_CHIP, R, C), BF16),
                          compiler_params=_params(("arbitrary", "arbitrary")))(core, blocked, recv)


def _adamw_math(w, g, m, v):
    m2 = ADAM_B1 * m + (1.0 - ADAM_B1) * g
    v2 = ADAM_B2 * v + (1.0 - ADAM_B2) * (g * g)
    m_hat = m2 / (1.0 - ADAM_B1 ** ADAM_STEP)
    v_hat = v2 / (1.0 - ADAM_B2 ** ADAM_STEP)
    delta = -ADAM_LR * (m_hat / (jnp.sqrt(v_hat) + ADAM_EPS) + ADAM_WD * w)
    return delta, m2, v2


def _sum_adamw_split(pairs, land, chips, w, m, v, transposed=False):
    R, C = w.shape
    n_slots = chips.shape[0]
    tr = _row_tile(R, 128, 256) if transposed else R
    tc = C if transposed else _pick(C, (256, 128))

    def body(ids_ref, *refs):
        del ids_ref
        parts, (w_ref, m_ref, v_ref, g_ref, d_ref, m2_ref, v2_ref) = refs[:n_slots], refs[n_slots:]
        g = parts[0][0].astype(F32)
        for p_ref in parts[1:]:
            g = g + p_ref[0].astype(F32)
        if transposed:
            g = g.T
        g_ref[...] = g
        d_ref[...], m2_ref[...], v2_ref[...] = _adamw_math(w_ref[...], g, m_ref[...], v_ref[...])

    if transposed:
        t = pl.BlockSpec((tr, C), lambda r, ids: (r, 0))
        slot = lambda k: pl.BlockSpec((1, C, tr), lambda r, ids: (ids[k], 0, r))
    else:
        t = pl.BlockSpec((R, tc), lambda c, ids: (0, c))
        slot = lambda k: pl.BlockSpec((1, R, tc), lambda c, ids: (ids[k], 0, c))
    gs = pltpu.PrefetchScalarGridSpec(num_scalar_prefetch=1, grid=((R // tr) * (C // tc),),
                                      in_specs=[slot(k) for k in range(n_slots)] + [t, t, t], out_specs=(t, t, t, t))
    return pl.pallas_call(body, name="sum_adamw_split", grid_spec=gs, out_shape=tuple(_S((R, C), F32) for _ in range(4)),
                          compiler_params=_params(("arbitrary",)))(chips, pairs, *[land] * (n_slots - 1), w, m, v)


def _adamw(w, g, m, v):
    R, C = w.shape
    tr = _row_tile(R, 8)

    def body(w_ref, g_ref, m_ref, v_ref, d_ref, m2_ref, v2_ref):
        d_ref[...], m2_ref[...], v2_ref[...] = _adamw_math(w_ref[...], g_ref[...], m_ref[...], v_ref[...])

    t = pl.BlockSpec((tr, C), lambda r: (r, 0))
    return pl.pallas_call(body, name="adamw", grid=(R // tr,), in_specs=[t, t, t, t], out_specs=(t, t, t),
                          out_shape=tuple(_S((R, C), F32) for _ in range(3)),
                          compiler_params=_params(("arbitrary",)))(w, g, m, v)


def _ada_grad_adamw(cs16, dmod16, w, m, v):
    D, C6 = w.shape
    tr = _row_tile(D, 8, 256)

    def body(cs_ref, dm_ref, w_ref, m_ref, v_ref, g_ref, d_ref, m2_ref, v2_ref):
        g = _dot_tn(cs_ref[...].astype(BF16), dm_ref[...].astype(BF16))
        g_ref[...] = g
        d_ref[...], m2_ref[...], v2_ref[...] = _adamw_math(w_ref[...], g, m_ref[...], v_ref[...])

    t = pl.BlockSpec((tr, C6), lambda r: (r, 0))
    return pl.pallas_call(
        body, name="ada_grad_adamw", grid=(D // tr,),
        in_specs=[pl.BlockSpec((16, tr), lambda r: (0, r)), _acc((16, C6)), t, t, t], out_specs=(t, t, t, t),
        out_shape=tuple(_S((D, C6), F32) for _ in range(4)), compiler_params=_params(("arbitrary",)))(cs16, dmod16, w, m, v)


def _pick(n, cands):
    for c in cands:
        if n % c == 0:
            return c
    return n


def _matmul_tn(a, b, name, square_a=False):
    L, K = a.shape
    N = b.shape[1]
    bk = _pick(K, (1024, 512, 256, 128))
    bn = _pick(N, (1024, 768, 512, 256, 128))
    tl = _pick(L, (1024, 512, 256, 128))
    n_l = L // tl

    def body(a_ref, b_ref, o_ref, acc_ref):
        l = pl.program_id(2)

        @pl.when(l == 0)
        def _():
            acc_ref[...] = jnp.zeros_like(acc_ref)
        av = a_ref[...]
        if square_a:
            av = av.astype(F32)
            av = av * av
        acc_ref[...] += _dot_tn(av.astype(BF16), b_ref[...].astype(BF16))

        @pl.when(l == n_l - 1)
        def _():
            o_ref[...] = acc_ref[...].astype(BF16)

    return pl.pallas_call(
        body, name=name, grid=(K // bk, N // bn, n_l),
        in_specs=[pl.BlockSpec((tl, bk), lambda k, n, l: (l, k)), pl.BlockSpec((tl, bn), lambda k, n, l: (l, n))],
        out_specs=pl.BlockSpec((bk, bn), lambda k, n, l: (k, n)), out_shape=_S((K, N), BF16),
        scratch_shapes=[pltpu.VMEM((bk, bn), F32)],
        compiler_params=_params(("arbitrary", "arbitrary", "arbitrary")))(a, b)


def _merge_blocks(a):
    n, R, C = a.shape
    cb = _pick(C, (256, 128))

    def body(i_ref, o_ref):
        for j in range(n):
            o_ref[R * j:R * (j + 1), :] = i_ref[j]

    return pl.pallas_call(body, name="merge_blocks", out_shape=_S((n * R, C), a.dtype), grid=(C // cb,),
                          in_specs=[pl.BlockSpec((n, R, cb), lambda c: (0, 0, c))],
                          out_specs=pl.BlockSpec((n * R, cb), lambda c: (0, c)),
                          compiler_params=_params(("arbitrary",)))(a)


def _split_blocks(a, n):
    NR, C = a.shape
    R = NR // n
    cb = _pick(C, (256, 128))

    def body(i_ref, o_ref):
        for j in range(n):
            o_ref[j] = i_ref[R * j:R * (j + 1), :].astype(BF16)

    return pl.pallas_call(body, name="split_blocks", out_shape=_S((n, R, C), BF16), grid=(C // cb,),
                          in_specs=[pl.BlockSpec((NR, cb), lambda c: (0, c))],
                          out_specs=pl.BlockSpec((n, R, cb), lambda c: (0, 0, c)),
                          compiler_params=_params(("arbitrary",)))(a)


def _gw_in(pieces, u1, dims):
    L, D = u1.shape
    H = dims["H"]
    r_z, r_xbc, r_dt, r_q, r_kv = _proj_rows(dims)
    PROJ = r_kv[1]
    tl = _pick(L, (512, 256, 128))
    n_l = L // tl

    def body(dz_ref, dxbc_ref, ddt_ref, dq_ref, dkv_ref, u_ref, o_ref):
        @pl.when(pl.program_id(0) == 0)
        def _():
            o_ref[...] = jnp.zeros_like(o_ref)
        u = u_ref[...]
        for ref, (r0, r1) in ((dz_ref, r_z), (dxbc_ref, r_xbc), (dq_ref, r_q), (dkv_ref, r_kv)):
            o_ref[r0:r1, :] += _dot_tn(ref[...].astype(BF16), u)
        o_ref[r_dt[0]:r_dt[0] + H, :] += _dot_tn(ddt_ref[...].astype(BF16), u)[0:H, :]

    return pl.pallas_call(
        body, name="gw_in", grid=(n_l,),
        in_specs=[_tile(p.shape[1], tl) for p in pieces] + [_tile(D, tl)],
        out_specs=_acc((PROJ, D)), out_shape=_S((PROJ, D), F32),
        compiler_params=_params(("arbitrary",)))(*pieces, u1)


def _proj_rows(dims):
    W, CD, H, AW, KVW2 = dims["W"], dims["CD"], dims["H"], dims["AW"], dims["KVW2"]
    o_dt = W + CD
    o_q = o_dt + H
    return (0, W), (W, o_dt), (o_dt, o_dt + LANE), (o_q, o_q + AW), (o_q + AW, o_q + AW + KVW2)


def _ln_in_proj(x, g, b, sc, sh, w_t, dims):
    L, D = x.shape
    W, CD, AW, KVW2 = dims["W"], dims["CD"], dims["AW"], dims["KVW2"]
    PROJ = w_t.shape[0]
    tm = _pick(L, (MLP_TM, 128))
    r_z, r_xbc, r_dt, r_q, r_kv = _proj_rows(dims)

    def body(x_ref, g_ref, b_ref, sc_ref, sh_ref, w_ref, xhat_ref, rstd_ref, u1_ref, z_ref, xbc_ref, q_ref, kv_ref, dt_ref):
        xhat, rstd = _ln_fwd(x_ref[...])
        xhat_ref[...] = xhat
        rstd_ref[...] = rstd
        h0 = xhat * g_ref[...] + b_ref[...]
        u1 = (h0 * (1.0 + sc_ref[...]) + sh_ref[...]).astype(BF16)
        u1_ref[...] = u1
        z_ref[...] = _dot_nt(u1, w_ref[r_z[0]:r_z[1], :])
        xbc_ref[...] = _dot_nt(u1, w_ref[r_xbc[0]:r_xbc[1], :])
        q_ref[...] = _dot_nt(u1, w_ref[r_q[0]:r_q[1], :]).astype(BF16)
        kv_ref[...] = _dot_nt(u1, w_ref[r_kv[0]:r_kv[1], :]).astype(BF16)
        dt_ref[...] = _dot_nt(u1, w_ref[r_dt[0]:r_dt[1], :])

    v = _acc((1, D))
    return pl.pallas_call(
        body, name="ln_in_proj", grid=(L // tm,),
        in_specs=[_tile(D, tm), v, v, v, v, _res((PROJ, D))],
        out_specs=(_tile(D, tm), _tile(1, tm), _tile(D, tm), _tile(W, tm), _tile(CD, tm), _tile(AW, tm),
                   _tile(KVW2, tm), _tile(LANE, tm)),
        out_shape=(_S((L, D), F32), _S((L, 1), F32), _S((L, D), BF16), _S((L, W), F32), _S((L, CD), F32),
                   _S((L, AW), BF16), _S((L, KVW2), BF16), _S((L, LANE), F32)),
        compiler_params=_params(("arbitrary",)))(x, g, b, sc, sh, w_t)


def _conv_act(cur_ref, prev_ref, cw_ref, cb_ref, ext_ref, first):
    T = cur_ref.shape[0]
    ext_ref[0:HALO, :] = jnp.where(first, 0.0, prev_ref[...])
    ext_ref[HALO:HALO + T, :] = cur_ref[...]
    pre = cb_ref[...] + cw_ref[0:1, :] * ext_ref[HALO - 3:HALO - 3 + T, :]
    for k in range(1, CONV_K):
        pre = pre + cw_ref[k:k + 1, :] * ext_ref[HALO - 3 + k:HALO - 3 + k + T, :]
    sig = _sigmoid(pre)
    return pre * sig, pre, sig


def _tri(T, upper=False):
    r = lax.broadcasted_iota(jnp.int32, (T, T), 0)
    c = lax.broadcasted_iota(jnp.int32, (T, T), 1)
    return (r <= c) if upper else (r >= c)


def _expand_heads_lanes(dst_ref, v, n_heads):
    for h in range(n_heads):
        dst_ref[:, h * HEAD_DIM:(h + 1) * HEAD_DIM] = jnp.broadcast_to(v[:, h:h + 1], (v.shape[0], HEAD_DIM))


def _head_onehot(n_heads):
    hd = np.arange(3 * LANE)[:, None] % LANE
    ch = np.arange(n_heads * HEAD_DIM)[None, :] // HEAD_DIM
    return jnp.asarray((hd == ch).astype(np.float32)).astype(BF16)


def _expand_heads(dst_ref, v, n_heads, onehot_ref):
    onehot = onehot_ref[...]
    v = jnp.where(lax.broadcasted_iota(jnp.int32, v.shape, 1) < n_heads, v, 0.0)
    hi = v.astype(BF16)
    r1 = v - hi.astype(F32)
    mid = r1.astype(BF16)
    lo = (r1 - mid.astype(F32)).astype(BF16)
    dst_ref[...] = _dot(jnp.concatenate([hi, mid, lo], axis=1), onehot)


def _head_reduce(v):
    wdt = v.shape[1]
    ch = lax.broadcasted_iota(jnp.int32, (wdt, LANE), 0)
    lo = lax.broadcasted_iota(jnp.int32, (wdt, LANE), 1) * HEAD_DIM
    onehot = ((ch >= lo) & (ch < lo + HEAD_DIM)).astype(BF16)
    hi = v.astype(BF16)
    rest = (v - hi.astype(F32)).astype(BF16)
    return _dot(hi, onehot) + _dot(rest, onehot)


def _conv_ssd(xbc, dt_raw, z, cw, cb, dtb, alog, dsk, nw, dims):
    L, CD = xbc.shape
    W, H, G, N = dims["W"], dims["H"], SSD_GROUPS, SSD_STATE
    T = CHUNK
    R = H // G
    GW = W // G
    nc = L // T
    HP = H * HEAD_DIM

    def body(xbc_ref, prev_ref, dt_ref, z_ref, cw_ref, cb_ref, dtb_ref, alog_ref, dsk_ref, nw_ref, oh_ref,
             y_ref, yn_ref, sp_ref, ext_ref, s_ref, ybuf_ref, dtx_ref, acx_ref, xb_ref):
        i = pl.program_id(0)

        @pl.when(i == 0)
        def _():
            s_ref[...] = jnp.zeros_like(s_ref)

        act, _, _ = _conv_act(xbc_ref, prev_ref, cw_ref, cb_ref, ext_ref, i == 0)
        xs = act[:, :W]
        dt = _softplus(dt_ref[...] + dtb_ref[...])
        a = dt * (-jnp.exp(alog_ref[...]))
        low = _tri(T)
        acum = _dot_hi(low.astype(F32), a)
        acum_t = acum.T
        _expand_heads(dtx_ref, dt, H, oh_ref)
        _expand_heads(acx_ref, acum, H, oh_ref)
        acx = acx_ref[...]
        lastx = acx[T - 1:T, :]
        xd = xs * dtx_ref[...]
        xb_ref[...] = xd.astype(BF16)
        xdb = (xd * jnp.exp(lastx - acx)).astype(BF16)
        ex = jnp.exp(acx)
        elx = jnp.exp(lastx)
        for g in range(G):
            gs = slice(g * GW, (g + 1) * GW)
            bgb = act[:, W + g * N:W + (g + 1) * N].astype(BF16)
            cgb = act[:, W + G * N + g * N:W + G * N + (g + 1) * N].astype(BF16)
            stg = s_ref[:, gs]
            sp_ref[0, :, gs] = stg
            yoff = ex[:, gs] * _dot(cgb, stg.astype(BF16))
            s_ref[:, gs] = stg * elx[:, gs] + _dot_tn(bgb, xdb[:, gs])
            cb_g = _dot_nt(cgb, bgb)
            for r in range(R):
                h = g * R + r
                hs = slice(h * HEAD_DIM, (h + 1) * HEAD_DIM)
                lm = jnp.where(low, jnp.exp(acum[:, h:h + 1] - acum_t[h:h + 1, :]), 0.0)
                ybuf_ref[:, hs] = _dot((cb_g * lm).astype(BF16), xb_ref[:, hs]) + yoff[:, r * HEAD_DIM:(r + 1) * HEAD_DIM]
        y = ybuf_ref[...] + dsk_ref[...] * xs
        y_ref[...] = y
        zz = z_ref[...]
        hh = y * (zz * _sigmoid(zz))
        for g in range(G):
            gs = slice(g * GW, (g + 1) * GW)
            hg = hh[:, gs]
            yn_ref[:, gs] = (hg * lax.rsqrt(_mean(hg * hg) + RMS_EPS) * nw_ref[:, gs]).astype(BF16)

    return pl.pallas_call(
        body, name="conv_ssd", grid=(nc,),
        in_specs=[_tile(CD, T), pl.BlockSpec((HALO, CD), lambda i: (jnp.maximum(i * (T // HALO) - 1, 0), 0)),
                  _tile(LANE, T), _tile(W, T), _acc((CONV_K, CD)), _acc((1, CD)), _acc((1, LANE)), _acc((1, LANE)),
                  _acc((1, W)), _acc((1, W)), _acc((3 * LANE, W))],
        out_specs=(_tile(W, T), _tile(W, T), pl.BlockSpec((1, N, HP), lambda i: (i, 0, 0))),
        out_shape=(_S((L, W), F32), _S((L, W), BF16), _S((nc, N, HP), F32)),
        scratch_shapes=[pltpu.VMEM((T + HALO, CD), F32), pltpu.VMEM((N, HP), F32), pltpu.VMEM((T, W), F32),
                        pltpu.VMEM((T, W), F32), pltpu.VMEM((T, W), F32), pltpu.VMEM((T, W), BF16)],
        compiler_params=_params(("arbitrary",)))(xbc, xbc, dt_raw, z, cw, cb, dtb, alog, dsk, nw, _head_onehot(H))


def _attn_mask(T, i):
    r = lax.broadcasted_iota(jnp.int32, (T, 2 * T), 0)
    c = lax.broadcasted_iota(jnp.int32, (T, 2 * T), 1)
    dist = r + T - c
    valid = (dist >= 0) & (dist < CHUNK) & ((c >= T) | (i > 0))
    return dist.astype(F32), valid


def _attn_probs(s_raw, dist, valid, slope, sink, axis):
    s = s_raw * (HEAD_DIM ** -0.5) - slope * dist
    s = jnp.where(valid, s, NEG)
    m = jnp.maximum(jnp.max(s, axis=axis, keepdims=True), sink)
    p = jnp.exp(s - m)
    e_sink = jnp.exp(sink - m)
    inv = 1.0 / (jnp.sum(p, axis=axis, keepdims=True) + e_sink)
    return p * inv, e_sink * inv


def _kv_heads(kvc_ref, kvp_ref, g, n_kv):
    ks = slice(g * HEAD_DIM, (g + 1) * HEAD_DIM)
    vs = slice((n_kv + g) * HEAD_DIM, (n_kv + g + 1) * HEAD_DIM)
    kk = jnp.concatenate([kvp_ref[:, ks], kvc_ref[:, ks]], axis=0)
    vv = jnp.concatenate([kvp_ref[:, vs], kvc_ref[:, vs]], axis=0)
    return kk, vv


def _swa_fwd(q, kv, sinks, dims):
    L, AW = q.shape
    KV, KVW2 = dims["KV"], dims["KVW2"]
    T = CHUNK
    nb = L // T
    slopes = _alibi_slopes(dims["AH"])

    def body(q_ref, kvc_ref, kvp_ref, sink_ref, o_ref, qg_ref, p_ref):
        i = pl.program_id(0)
        dist, valid = _attn_mask(T, i)
        for g in range(KV):
            kk, vv = _kv_heads(kvc_ref, kvp_ref, g, KV)
            for r in range(GQA):
                h = g * GQA + r
                qg_ref[r * T:(r + 1) * T, :] = q_ref[:, h * HEAD_DIM:(h + 1) * HEAD_DIM]
            s_all = _dot_nt(qg_ref[...], kk)
            for r in range(GQA):
                h = g * GQA + r
                p, _ = _attn_probs(s_all[r * T:(r + 1) * T, :], dist, valid, slopes[h], sink_ref[h], -1)
                p_ref[r * T:(r + 1) * T, :] = p.astype(BF16)
            o_all = _dot(p_ref[...], vv)
            for r in range(GQA):
                h = g * GQA + r
                o_ref[:, h * HEAD_DIM:(h + 1) * HEAD_DIM] = o_all[r * T:(r + 1) * T, :].astype(BF16)

    return pl.pallas_call(
        body, name="swa_fwd", grid=(nb,),
        in_specs=[_tile(AW, T), _tile(KVW2, T), pl.BlockSpec((T, KVW2), lambda i: (jnp.maximum(i - 1, 0), 0)),
                  pl.BlockSpec(memory_space=pltpu.SMEM)],
        out_specs=_tile(AW, T), out_shape=_S((L, AW), BF16),
        scratch_shapes=[pltpu.VMEM((GQA * T, HEAD_DIM), BF16), pltpu.VMEM((GQA * T, 2 * T), BF16)],
        compiler_params=_params(("arbitrary",)))(q, kv, kv, sinks)


def _out_proj_ln1(yn, o, w_out, xhat0, vecs, alpha):
    L, W = yn.shape
    D = xhat0.shape[1]
    MIX = w_out.shape[0]
    tm = _pick(L, (MLP_TM, 128))

    def body(yn_ref, o_ref, w_ref, xh_ref, v_ref, mix_ref, xhat1_ref, rstd1_ref, u2_ref):
        mix = _dot(yn_ref[...], w_ref[0:W, :]) + _dot(o_ref[...], w_ref[W:MIX, :])
        mix_ref[...] = mix
        h0 = xh_ref[...] * v_ref[0:1, :] + v_ref[1:2, :]
        xhat1, rstd1 = _ln_fwd(alpha * h0 + (1.0 + v_ref[2:3, :]) * mix)
        xhat1_ref[...] = xhat1
        rstd1_ref[...] = rstd1
        h1 = xhat1 * v_ref[3:4, :] + v_ref[4:5, :]
        u2_ref[...] = (h1 * (1.0 + v_ref[5:6, :]) + v_ref[6:7, :]).astype(BF16)

    return pl.pallas_call(
        body, name="out_proj_ln1", grid=(L // tm,),
        in_specs=[_tile(W, tm), _tile(MIX - W, tm), _res((MIX, D)), _tile(D, tm), _acc((8, D))],
        out_specs=(_tile(D, tm), _tile(D, tm), _tile(1, tm), _tile(D, tm)),
        out_shape=(_S((L, D), F32), _S((L, D), F32), _S((L, 1), F32), _S((L, D), BF16)),
        compiler_params=_params(("arbitrary",)))(yn, o, w_out, xhat0, vecs)


def _mlp_loss(u2, w1, w2, xhat1, tgt, vecs, b1, alpha):
    L, D = xhat1.shape
    FF = w1.shape[1]
    tm = _pick(L, (MLP_TM, 128))
    sub = min(tm, MLP_SUB)
    fc = _pick(FF, (MLP_FC, 256, 128))

    def body(u2_ref, w1_ref, w2_ref, xh_ref, t_ref, v_ref, b1_ref, rr_ref, dr2_ref, acc_ref, loss_ref):
        @pl.when(pl.program_id(0) == 0)
        def _():
            acc_ref[...] = jnp.zeros_like(acc_ref)
            loss_ref[...] = jnp.zeros_like(loss_ref)

        for s in range(tm // sub):
            rs = slice(s * sub, (s + 1) * sub)
            u2 = u2_ref[rs, :]
            f = jnp.zeros((sub, D), F32) + v_ref[5:6, :]
            for j in range(FF // fc):
                cs = slice(j * fc, (j + 1) * fc)
                rr = jnp.maximum(_dot(u2, w1_ref[:, cs]) + b1_ref[:, cs], 0.0)
                rr_ref[rs, cs] = rr.astype(BF16)
                f = f + _dot((rr * rr).astype(BF16), w2_ref[cs, :])
            xhat1 = xh_ref[rs, :]
            h1 = xhat1 * v_ref[0:1, :] + v_ref[1:2, :]
            xhat2, rstd2 = _ln_fwd(alpha * h1 + (1.0 + v_ref[2:3, :]) * f)
            e = xhat2 * v_ref[3:4, :] + v_ref[4:5, :] - t_ref[rs, :]
            loss_ref[...] += 0.5 * jnp.sum(_mean(e * e))
            dy = e * (1.0 / D)
            dr2 = _ln_bwd(dy * v_ref[3:4, :], xhat2, rstd2)
            dr2_ref[rs, :] = dr2
            acc_ref[0:1, :] += _colsum(dy * xhat2)
            acc_ref[1:2, :] += _colsum(dy)
            acc_ref[2:3, :] += _colsum(dr2 * f)

    return pl.pallas_call(
        body, name="mlp_loss", grid=(L // tm,),
        in_specs=[_tile(D, tm), _res((D, FF)), _res((FF, D)), _tile(D, tm), _tile(D, tm), _acc((8, D)), _acc((1, FF))],
        out_specs=(_tile(FF, tm), _tile(D, tm), _acc((8, D)), _acc((1, LANE))),
        out_shape=(_S((L, FF), BF16), _S((L, D), F32), _S((8, D), F32), _S((1, LANE), F32)),
        compiler_params=_params(("arbitrary",)))(u2, w1, w2, xhat1, tgt, vecs, b1)


def _mlp_bwd_a(dr2, rr, w2, g2):
    L, D = dr2.shape
    FF = w2.shape[0]
    tm = _pick(L, (MLP_TM, 128))
    fc = _pick(FF, (MLP_FC, 256, 128))

    def body(dr2_ref, rr_ref, w2_ref, g2_ref, df_ref, da_ref, gb2_ref, gb1_ref):
        @pl.when(pl.program_id(0) == 0)
        def _():
            gb2_ref[...] = jnp.zeros_like(gb2_ref)
            gb1_ref[...] = jnp.zeros_like(gb1_ref)

        df = (1.0 + g2_ref[...]) * dr2_ref[...]
        gb2_ref[...] += _colsum(df)
        dfb = df.astype(BF16)
        df_ref[...] = dfb
        for j in range(FF // fc):
            cs = slice(j * fc, (j + 1) * fc)
            da = _dot_nt(dfb, w2_ref[cs, :]) * (2.0 * rr_ref[:, cs].astype(F32))
            gb1_ref[:, cs] += _colsum(da)
            da_ref[:, cs] = da.astype(BF16)

    return pl.pallas_call(
        body, name="mlp_bwd_a", grid=(L // tm,),
        in_specs=[_tile(D, tm), _tile(FF, tm), _res((FF, D)), _acc((1, D))],
        out_specs=(_tile(D, tm), _tile(FF, tm), _acc((1, D)), _acc((1, FF))),
        out_shape=(_S((L, D), BF16), _S((L, FF), BF16), _S((1, D), F32), _S((1, FF), F32)),
        compiler_params=_params(("arbitrary",)))(dr2, rr, w2, g2)


def _mlp_bwd_b(da, w1, dr2, xhat1, rstd1, mix, w_out, vecs, alpha, W):
    L, FF = da.shape
    D = dr2.shape[1]
    MIX = w_out.shape[0]
    tm = _pick(L, (MLP_TM, 128))

    def body(da_ref, w1_ref, dr2_ref, xh_ref, rs_ref, mix_ref, wo_ref, v_ref, dmix_ref, dh0_ref, dyn_ref, do_ref, acc_ref):
        @pl.when(pl.program_id(0) == 0)
        def _():
            acc_ref[...] = jnp.zeros_like(acc_ref)

        du2 = _dot_nt(da_ref[...], w1_ref[...])
        xhat1 = xh_ref[...]
        h1 = xhat1 * v_ref[0:1, :] + v_ref[1:2, :]
        acc_ref[0:1, :] += _colsum(du2 * h1)
        acc_ref[1:2, :] += _colsum(du2)
        dh1 = alpha * dr2_ref[...] + du2 * (1.0 + v_ref[2:3, :])
        acc_ref[2:3, :] += _colsum(dh1 * xhat1)
        acc_ref[3:4, :] += _colsum(dh1)
        dr1 = _ln_bwd(dh1 * v_ref[0:1, :], xhat1, rs_ref[...])
        acc_ref[4:5, :] += _colsum(dr1 * mix_ref[...])
        dh0_ref[...] = alpha * dr1
        dmix = ((1.0 + v_ref[3:4, :]) * dr1).astype(BF16)
        dmix_ref[...] = dmix
        dyn_ref[...] = _dot_nt(dmix, wo_ref[0:W, :])
        do_ref[...] = _dot_nt(dmix, wo_ref[W:MIX, :]).astype(BF16)

    return pl.pallas_call(
        body, name="mlp_bwd_b", grid=(L // tm,),
        in_specs=[_tile(FF, tm), _res((D, FF)), _tile(D, tm), _tile(D, tm), _tile(1, tm), _tile(D, tm), _res((MIX, D)),
                  _acc((8, D))],
        out_specs=(_tile(D, tm), _tile(D, tm), _tile(W, tm), _tile(MIX - W, tm), _acc((8, D))),
        out_shape=(_S((L, D), BF16), _S((L, D), F32), _S((L, W), F32), _S((L, MIX - W), BF16), _S((8, D), F32)),
        compiler_params=_params(("arbitrary",)))(da, w1, dr2, xhat1, rstd1, mix, w_out, vecs)


def _swa_bwd(q, kv, do, sinks, dims):
    L, AW = q.shape
    KV, KVW2 = dims["KV"], dims["KVW2"]
    T = CHUNK
    nb = L // T
    slopes = _alibi_slopes(dims["AH"])
    scale = HEAD_DIM ** -0.5

    def body(q_ref, kvc_ref, kvp_ref, do_ref, sink_ref, dq_ref, dkv_ref, dsink_ref, carry_ref,
             qg_ref, dog_ref, pt_ref, dst_ref):
        i = pl.program_id(0)

        @pl.when(i == 0)
        def _():
            carry_ref[...] = jnp.zeros_like(carry_ref)
            dsink_ref[...] = jnp.zeros_like(dsink_ref)

        @pl.when(i < nb)
        def _():
            c = lax.broadcasted_iota(jnp.int32, (2 * T, T), 0)
            r_ = lax.broadcasted_iota(jnp.int32, (2 * T, T), 1)
            dist_i = r_ + T - c
            valid = (dist_i >= 0) & (dist_i < CHUNK) & ((c >= T) | (i > 0))
            dist = dist_i.astype(F32)
            lane = lax.broadcasted_iota(jnp.int32, (1, LANE), 1)
            dsink = jnp.zeros((1, LANE), F32)
            dks, dvs = [], []
            for g in range(KV):
                kk, vv = _kv_heads(kvc_ref, kvp_ref, g, KV)
                for r in range(GQA):
                    hs = slice((g * GQA + r) * HEAD_DIM, (g * GQA + r + 1) * HEAD_DIM)
                    qg_ref[r * T:(r + 1) * T, :] = q_ref[:, hs]
                    dog_ref[r * T:(r + 1) * T, :] = do_ref[:, hs]
                st_all = _dot_nt(kk, qg_ref[...])
                dpt_all = _dot_nt(vv, dog_ref[...])
                for r in range(GQA):
                    h = g * GQA + r
                    cs = slice(r * T, (r + 1) * T)
                    p, p_sink = _attn_probs(st_all[:, cs], dist, valid, slopes[h], sink_ref[h], 0)
                    dp = dpt_all[:, cs]
                    delta = jnp.sum(p * dp, axis=0, keepdims=True)
                    pt_ref[:, cs] = p.astype(BF16)
                    dst_ref[:, cs] = (p * (dp - delta)).astype(BF16)
                    dsink = dsink + jnp.where(lane == h, -jnp.sum(p_sink * delta), 0.0)
                dst = dst_ref[...]
                dks.append(_dot(dst, qg_ref[...]) * scale)
                dvs.append(_dot(pt_ref[...], dog_ref[...]))
                dq_all = _dot_tn(dst, kk) * scale
                for r in range(GQA):
                    hs = slice((g * GQA + r) * HEAD_DIM, (g * GQA + r + 1) * HEAD_DIM)
                    dq_ref[:, hs] = dq_all[r * T:(r + 1) * T, :].astype(BF16)
            dkv = jnp.concatenate(dks + dvs, axis=1)
            dsink_ref[...] += dsink
            dkv_ref[...] = carry_ref[...] + dkv[0:T, :]
            carry_ref[...] = dkv[T:2 * T, :]

        @pl.when(i == nb)
        def _():
            dkv_ref[...] = carry_ref[...]

    last = nb - 1
    return pl.pallas_call(
        body, name="swa_bwd", grid=(nb + 1,),
        in_specs=[pl.BlockSpec((T, AW), lambda i: (jnp.minimum(i, last), 0)),
                  pl.BlockSpec((T, KVW2), lambda i: (jnp.minimum(i, last), 0)),
                  pl.BlockSpec((T, KVW2), lambda i: (jnp.clip(i - 1, 0, last), 0)),
                  pl.BlockSpec((T, AW), lambda i: (jnp.minimum(i, last), 0)),
                  pl.BlockSpec(memory_space=pltpu.SMEM)],
        out_specs=(pl.BlockSpec((T, AW), lambda i: (jnp.minimum(i, last), 0)),
                   pl.BlockSpec((T, KVW2), lambda i: (jnp.maximum(i - 1, 0), 0)), _acc((1, LANE))),
        out_shape=(_S((L, AW), BF16), _S((L, KVW2), F32), _S((1, LANE), F32)),
        scratch_shapes=[pltpu.VMEM((T, KVW2), F32), pltpu.VMEM((GQA * T, HEAD_DIM), BF16),
                        pltpu.VMEM((GQA * T, HEAD_DIM), BF16), pltpu.VMEM((2 * T, GQA * T), BF16),
                        pltpu.VMEM((2 * T, GQA * T), BF16)],
        compiler_params=_params(("arbitrary",)))(q, kv, kv, do, sinks)


def _ssd_bwd(dyn, y, z, xbc, dt_raw, sprev, cw, cb, dtb, alog, dsk, nw, dims):
    L, CD = xbc.shape
    W, H, G, N = dims["W"], dims["H"], SSD_GROUPS, SSD_STATE
    T = CHUNK
    R = H // G
    GW = W // G
    nc = L // T
    HP = H * HEAD_DIM

    def body(dyn_ref, y_ref, z_ref, xbc_ref, prev_ref, dt_ref, sp_ref, cw_ref, cb_ref, dtb_ref, alog_ref, dsk_ref, nw_ref,
             dz_ref, dpre_ref, ddt_ref, acc_ref, hacc_ref, ext_ref, ds_ref, dtx_ref, acx_ref, xb_ref, dyb_ref, r12_ref,
             dx_ref, rows_ref):
        i = pl.program_id(0)

        @pl.when(i == 0)
        def _():
            ds_ref[...] = jnp.zeros_like(ds_ref)
            acc_ref[...] = jnp.zeros_like(acc_ref)
            hacc_ref[...] = jnp.zeros_like(hacc_ref)

        act, pre, spre = _conv_act(xbc_ref, prev_ref, cw_ref, cb_ref, ext_ref, i == nc - 1)
        xs = act[:, :W]
        dt_in = dt_ref[...] + dtb_ref[...]
        dt = _softplus(dt_in)
        a_neg = -jnp.exp(alog_ref[...])
        a = dt * a_neg
        low = _tri(T)
        upf = _tri(T, upper=True).astype(F32)
        acum = _dot_hi(low.astype(F32), a)
        acum_t = acum.T

        y = y_ref[...]
        zz = z_ref[...]
        sg = _sigmoid(zz)
        sz = zz * sg
        hh = y * sz
        dyn_v = dyn_ref[...]
        parts = []
        for g in range(G):
            gs = slice(g * GW, (g + 1) * GW)
            hg = hh[:, gs]
            hhat = hg * lax.rsqrt(_mean(hg * hg) + RMS_EPS)
            rg = lax.rsqrt(_mean(hg * hg) + RMS_EPS)
            acc_ref[0:1, gs] += _colsum(dyn_v[:, gs] * hhat)
            dhhat = dyn_v[:, gs] * nw_ref[:, gs]
            parts.append(rg * (dhhat - hhat * _mean(dhhat * hhat)))
        dhh = jnp.concatenate(parts, axis=1)
        dy = dhh * sz
        dz_ref[...] = (dhh * y * (sg * (1.0 + zz * (1.0 - sg)))).astype(BF16)
        acc_ref[1:2, :] += _colsum(dy * xs)
        dyb_ref[...] = dy.astype(BF16)

        _expand_heads_lanes(dtx_ref, dt, H)
        _expand_heads_lanes(acx_ref, acum, H)
        dtx = dtx_ref[...]
        acx = acx_ref[...]
        lastx = acx[T - 1:T, :]
        ex = jnp.exp(acx)
        decx = jnp.exp(lastx - acx)
        elx = jnp.exp(lastx)
        xd = xs * dtx
        xb_ref[...] = xd.astype(BF16)
        xdecb = (xd * decx).astype(BF16)
        dgb = (ex * dy).astype(BF16)
        rows_ref[...] = jnp.zeros_like(rows_ref)

        lane = lax.broadcasted_iota(jnp.int32, (T, LANE), 1)
        sub = lax.broadcasted_iota(jnp.int32, (T, LANE), 0)
        subr = lax.broadcasted_iota(jnp.int32, (LANE, T), 0)
        da_col = jnp.zeros((T, LANE), F32)
        da_row = jnp.zeros((LANE, T), F32)
        dbs, dcs = [], []
        for g in range(G):
            gs = slice(g * GW, (g + 1) * GW)
            bgb = act[:, W + g * N:W + (g + 1) * N].astype(BF16)
            cgb = act[:, W + G * N + g * N:W + G * N + (g + 1) * N].astype(BF16)
            stg = sp_ref[0, :, gs]
            stb = stg.astype(BF16)
            dsn = ds_ref[:, gs]
            dsnb = dsn.astype(BF16)
            gm = _dot(cgb, stb)
            dc = _dot_nt(dgb[:, gs], stb)
            dsp = _dot_tn(cgb, dgb[:, gs])
            dxs_ = decx[:, gs] * _dot(bgb, dsnb)
            db = _dot_nt(xdecb[:, gs], dsnb)
            xdg = xd[:, gs]
            r12_ref[:, gs] = dy[:, gs] * ex[:, gs] * gm - xdg * dxs_
            rows_ref[0:1, gs] = _colsum(dsn * stg) * elx[:, gs]
            rows_ref[1:2, gs] = _colsum(xdg * dxs_)
            ds_ref[:, gs] = dsp + dsn * elx[:, gs]
            cb_g = _dot_nt(cgb, bgb)
            dcb = jnp.zeros((T, T), F32)
            for r in range(R):
                h = g * R + r
                hs = slice(h * HEAD_DIM, (h + 1) * HEAD_DIM)
                lm = jnp.where(low, jnp.exp(acum[:, h:h + 1] - acum_t[h:h + 1, :]), 0.0)
                mm = cb_g * lm
                dyb = dyb_ref[:, hs]
                dm = _dot_nt(dyb, xb_ref[:, hs])
                dx_ref[:, hs] = dxs_[:, r * HEAD_DIM:(r + 1) * HEAD_DIM] + _dot_tn(mm.astype(BF16), dyb)
                dcb = dcb + dm * lm
                qm = dm * mm
                da_col = jnp.where(lane == h, jnp.sum(qm, axis=1, keepdims=True), da_col)
                da_row = jnp.where(subr == h, jnp.sum(qm, axis=0, keepdims=True), da_row)
            dcbb = dcb.astype(BF16)
            dcs.append(dc + _dot(dcbb, bgb))
            dbs.append(db + _dot_tn(dcbb, cgb))
        dx = dx_ref[...]
        rows = _head_reduce(rows_ref[...])
        dlast = rows[0:1, :] + rows[1:2, :]
        da_col = da_col + _head_reduce(r12_ref[...]) + jnp.where(sub == T - 1, dlast, 0.0)
        dacum = da_col - da_row.T
        da = _dot_hi(upf, dacum)
        ddt = _head_reduce(dx * xs) + da * a_neg
        hacc_ref[1:2, :] += _colsum(da * dt) * a_neg
        ddt_raw = ddt * _sigmoid(dt_in)
        hacc_ref[0:1, :] += _colsum(ddt_raw)
        ddt_ref[...] = ddt_raw
        dact = jnp.concatenate([dsk_ref[...] * dy + dx * dtx] + dbs + dcs, axis=1)
        dpre_ref[...] = dact * (spre * (1.0 + pre * (1.0 - spre)))

        @pl.when(i == nc - 1)
        def _():
            ch = lax.broadcasted_iota(jnp.int32, (W, LANE), 0)
            lo = lax.broadcasted_iota(jnp.int32, (W, LANE), 1) * HEAD_DIM
            hacc_ref[2:3, :] = _dot_hi(acc_ref[1:2, :], ((ch >= lo) & (ch < lo + HEAD_DIM)).astype(F32))

    rev = lambda i: (nc - 1 - i, 0)
    return pl.pallas_call(
        body, name="ssd_bwd", grid=(nc,),
        in_specs=[pl.BlockSpec((T, W), rev), pl.BlockSpec((T, W), rev), pl.BlockSpec((T, W), rev), pl.BlockSpec((T, CD), rev),
                  pl.BlockSpec((HALO, CD), lambda i: (jnp.maximum((nc - 1 - i) * (T // HALO) - 1, 0), 0)),
                  pl.BlockSpec((T, LANE), rev), pl.BlockSpec((1, N, HP), lambda i: (nc - 1 - i, 0, 0)),
                  _acc((CONV_K, CD)), _acc((1, CD)), _acc((1, LANE)), _acc((1, LANE)), _acc((1, W)), _acc((1, W))],
        out_specs=(pl.BlockSpec((T, W), rev), pl.BlockSpec((T, CD), rev), pl.BlockSpec((T, LANE), rev), _acc((8, W)),
                   _acc((8, LANE))),
        out_shape=(_S((L, W), BF16), _S((L, CD), F32), _S((L, LANE), F32), _S((8, W), F32), _S((8, LANE), F32)),
        scratch_shapes=[pltpu.VMEM((T + HALO, CD), F32), pltpu.VMEM((N, HP), F32), pltpu.VMEM((T, W), F32),
                        pltpu.VMEM((T, W), F32), pltpu.VMEM((T, W), BF16), pltpu.VMEM((T, W), BF16), pltpu.VMEM((T, W), F32),
                        pltpu.VMEM((T, W), F32), pltpu.VMEM((8, W), F32)],
        compiler_params=_params(("arbitrary",)))(dyn, y, z, xbc, xbc, dt_raw, sprev, cw, cb, dtb, alog, dsk, nw)


def _conv_bwd(dpre, xbc, cw):
    L, CD = xbc.shape
    tm = _pick(L, (CONV_TM, 128))
    cb = CD if CONV_CB >= CD else _pick(CD, (CONV_CB, 128))
    nt = L // tm
    hb = tm // HALO

    def body(dp_ref, dn_ref, u_ref, cw_ref, du_ref, acc_ref, extd_ref):
        i = pl.program_id(1)

        @pl.when(i == 0)
        def _():
            acc_ref[...] = jnp.zeros_like(acc_ref)

        extd_ref[0:tm, :] = dp_ref[...]
        extd_ref[tm:tm + HALO, :] = jnp.where(i == nt - 1, 0.0, dn_ref[...])
        for c in range(tm // ROW_CHUNK):
            r0 = c * ROW_CHUNK
            rows = slice(r0, r0 + ROW_CHUNK)
            dp = dp_ref[rows, :]
            u = u_ref[rows, :]
            du = cw_ref[CONV_K - 1:CONV_K, :] * dp
            acc_ref[CONV_K - 1:CONV_K, :] += _colsum(dp * u)
            for k in range(CONV_K - 1):
                s = CONV_K - 1 - k
                dsh = extd_ref[r0 + s:r0 + s + ROW_CHUNK, :]
                du = du + cw_ref[k:k + 1, :] * dsh
                acc_ref[k:k + 1, :] += _colsum(u * dsh)
            acc_ref[CONV_K:CONV_K + 1, :] += _colsum(dp)
            du_ref[rows, :] = du.astype(BF16)

    tile = pl.BlockSpec((tm, cb), lambda j, i: (i, j))
    return pl.pallas_call(
        body, name="conv_bwd", grid=(CD // cb, nt),
        in_specs=[tile, pl.BlockSpec((HALO, cb), lambda j, i: (jnp.minimum((i + 1) * hb, nt * hb - 1), j)),
                  tile, pl.BlockSpec((CONV_K, cb), lambda j, i: (0, j))],
        out_specs=(tile, pl.BlockSpec((8, cb), lambda j, i: (0, j))),
        out_shape=(_S((L, CD), BF16), _S((8, CD), F32)),
        scratch_shapes=[pltpu.VMEM((tm + HALO, cb), F32)],
        compiler_params=_params(("arbitrary", "arbitrary")))(dpre, dpre, xbc, cw)


def _in_proj_bwd(dz, dxbc, dq, dkv, ddt, w_t, xhat0, rstd0, dh0p, vecs, dims):
    L, D = xhat0.shape
    W, CD, AW, KVW2 = dims["W"], dims["CD"], dims["AW"], dims["KVW2"]
    PROJ = w_t.shape[0]
    tm = _pick(L, (MLP_TM, 128))
    r_z, r_xbc, r_dt, r_q, r_kv = _proj_rows(dims)

    def body(dz_ref, dxbc_ref, dq_ref, dkv_ref, ddt_ref, w_ref, xh_ref, rs_ref, dh0_ref, v_ref, gx_ref, acc_ref):
        @pl.when(pl.program_id(0) == 0)
        def _():
            acc_ref[...] = jnp.zeros_like(acc_ref)

        du1 = _dot(dz_ref[...], w_ref[r_z[0]:r_z[1], :])
        du1 = du1 + _dot(dxbc_ref[...], w_ref[r_xbc[0]:r_xbc[1], :])
        du1 = du1 + _dot(dq_ref[...], w_ref[r_q[0]:r_q[1], :])
        du1 = du1 + _dot(dkv_ref[...].astype(BF16), w_ref[r_kv[0]:r_kv[1], :])
        du1 = du1 + _dot(ddt_ref[...].astype(BF16), w_ref[r_dt[0]:r_dt[1], :])
        xhat0 = xh_ref[...]
        h0 = xhat0 * v_ref[0:1, :] + v_ref[1:2, :]
        acc_ref[0:1, :] += _colsum(du1 * h0)
        acc_ref[1:2, :] += _colsum(du1)
        dh0 = dh0_ref[...] + du1 * (1.0 + v_ref[2:3, :])
        acc_ref[2:3, :] += _colsum(dh0 * xhat0)
        acc_ref[3:4, :] += _colsum(dh0)
        gx_ref[...] = _ln_bwd(dh0 * v_ref[0:1, :], xhat0, rs_ref[...])

    return pl.pallas_call(
        body, name="in_proj_bwd", grid=(L // tm,),
        in_specs=[_tile(W, tm), _tile(CD, tm), _tile(AW, tm), _tile(KVW2, tm), _tile(LANE, tm), _res((PROJ, D)),
                  _tile(D, tm), _tile(1, tm), _tile(D, tm), _acc((8, D))],
        out_specs=(_tile(D, tm), _acc((8, D))),
        out_shape=(_S((L, D), F32), _S((8, D), F32)),
        compiler_params=_params(("arbitrary",)))(dz, dxbc, dq, dkv, ddt, w_t, xhat0, rstd0, dh0p, vecs)


_WEIGHTS = ['ln_in_g', 'ln_in_b', 'ada_w', 'ada_b', 'w_in', 'conv_w', 'conv_b', 'dt_bias', 'a_log', 'd_skip', 'ssd_norm_w',
            'attn_sinks', 'w_out', 'ln1_g', 'ln1_b', 'w_ff1', 'b_ff1', 'w_ff2', 'b_ff2', 'ln2_g', 'ln2_b']
_BIG = ('w_in', 'w_out', 'w_ff1', 'w_ff2')
_SMALL = ('ada_b', 'ln_in_g', 'ln_in_b', 'conv_b', 'dt_bias', 'a_log', 'd_skip', 'ssd_norm_w', 'attn_sinks', 'ln1_g', 'ln1_b',
          'b_ff1', 'b_ff2', 'ln2_g', 'ln2_b')


def _pad_lanes(v, n=None):
    v = v.reshape(1, -1)
    n = n or -(-v.shape[1] // LANE) * LANE
    return jnp.pad(v, ((0, 0), (0, n - v.shape[1])))


def _vec8(rows, D):
    rows = [r.reshape(1, D) for r in rows]
    return jnp.concatenate(rows + [jnp.zeros((8 - len(rows), D), F32)], axis=0)


def kernel(x, c, ln_in_g, ln_in_b, ada_w, ada_b, w_in, conv_w, conv_b, dt_bias, a_log, d_skip, ssd_norm_w, attn_sinks, w_out, ln1_g, ln1_b, w_ff1, b_ff1, w_ff2, b_ff2, ln2_g, ln2_b, loss_target, m_ln_in_g, m_ln_in_b, m_ada_w, m_ada_b, m_w_in, m_conv_w, m_conv_b, m_dt_bias, m_a_log, m_d_skip, m_ssd_norm_w, m_attn_sinks, m_w_out, m_ln1_g, m_ln1_b, m_w_ff1, m_b_ff1, m_w_ff2, m_b_ff2, m_ln2_g, m_ln2_b, v_ln_in_g, v_ln_in_b, v_ada_w, v_ada_b, v_w_in, v_conv_w, v_conv_b, v_dt_bias, v_a_log, v_d_skip, v_ssd_norm_w, v_attn_sinks, v_w_out, v_ln1_g, v_ln1_b, v_w_ff1, v_b_ff1, v_w_ff2, v_b_ff2, v_ln2_g, v_ln2_b):
    wts = dict(ln_in_g=ln_in_g, ln_in_b=ln_in_b, ada_w=ada_w, ada_b=ada_b, w_in=w_in, conv_w=conv_w, conv_b=conv_b,
               dt_bias=dt_bias, a_log=a_log, d_skip=d_skip, ssd_norm_w=ssd_norm_w, attn_sinks=attn_sinks, w_out=w_out,
               ln1_g=ln1_g, ln1_b=ln1_b, w_ff1=w_ff1, b_ff1=b_ff1, w_ff2=w_ff2, b_ff2=b_ff2, ln2_g=ln2_g, ln2_b=ln2_b)
    ms = dict(ln_in_g=m_ln_in_g, ln_in_b=m_ln_in_b, ada_w=m_ada_w, ada_b=m_ada_b, w_in=m_w_in, conv_w=m_conv_w,
              conv_b=m_conv_b, dt_bias=m_dt_bias, a_log=m_a_log, d_skip=m_d_skip, ssd_norm_w=m_ssd_norm_w,
              attn_sinks=m_attn_sinks, w_out=m_w_out, ln1_g=m_ln1_g, ln1_b=m_ln1_b, w_ff1=m_w_ff1, b_ff1=m_b_ff1,
              w_ff2=m_w_ff2, b_ff2=m_b_ff2, ln2_g=m_ln2_g, ln2_b=m_ln2_b)
    vs = dict(ln_in_g=v_ln_in_g, ln_in_b=v_ln_in_b, ada_w=v_ada_w, ada_b=v_ada_b, w_in=v_w_in, conv_w=v_conv_w,
              conv_b=v_conv_b, dt_bias=v_dt_bias, a_log=v_a_log, d_skip=v_d_skip, ssd_norm_w=v_ssd_norm_w,
              attn_sinks=v_attn_sinks, w_out=v_w_out, ln1_g=v_ln1_g, ln1_b=v_ln1_b, w_ff1=v_w_ff1, b_ff1=v_b_ff1,
              w_ff2=v_w_ff2, b_ff2=v_b_ff2, ln2_g=v_ln2_g, ln2_b=v_ln2_b)

    L, D = x.shape[1], x.shape[2]
    depth = w_in.shape[0]
    assert depth == 1 and x.shape[0] == 1 and L % CHUNK == 0
    W = D
    H = W // HEAD_DIM
    CD = W + 2 * SSD_GROUPS * SSD_STATE
    AW = D
    AH = AW // HEAD_DIM
    KV = AH // GQA
    KVW2 = 2 * KV * HEAD_DIM
    PROJ = W + CD + H + AW + KVW2
    FF = w_ff1.shape[2] * N_DEV
    MIX = w_out.shape[1] * N_DEV
    assert w_in.shape[2] * N_DEV == PROJ and MIX == W + AW and H <= LANE and AH <= LANE
    dims = dict(W=W, H=H, CD=CD, AW=AW, AH=AH, KV=KV, KVW2=KVW2)
    alpha = (2.0 * depth) ** 0.25
    C6 = ada_w.shape[2]
    CW = conv_w.shape[2]

    ax, ay, ac = _my_pos()
    me = 4 * ax + 2 * ay + ac
    x2 = x.reshape(L, D)
    tgt = loss_target.reshape(L, D)
    r1 = lambda a: a.reshape(1, -1)

    ada_b_cols = lax.dynamic_slice(ada_b, (0, me * C6), (1, C6))
    cs_all, mod = _mod_fwd(c, ada_w[0], ada_b_cols)
    sh1, sc1, g1, sh2, sc2, g2 = [r1(t) for t in jnp.split(mod.reshape(-1), 6)]

    wg_in, cwg = _ag_weights([w_in[0].T.astype(BF16), conv_w[0]], cs_all)
    shards2 = [w_out[0].astype(BF16), w_ff1[0].astype(BF16), w_ff2[0].astype(BF16)]
    lands2 = [lax.dynamic_update_slice(lax.empty((N_DEV,) + s.shape, s.dtype), s[None], (me, 0, 0)) for s in shards2]
    ag_ss, ag_rs, ag_arr, ag_token = _split_start(shards2 + lands2, _plan_gather(3), cwg, "ag_ici_start")
    sh1 = sh1 + ag_token[0:1, 0:1]
    w_pad = _merge_blocks(wg_in)
    cw_full = cwg.transpose(1, 0, 2).reshape(CONV_K, CD)

    dtb = _pad_lanes(dt_bias, LANE)
    alog = _pad_lanes(a_log, LANE)
    dsk = jnp.repeat(d_skip.reshape(-1), HEAD_DIM).reshape(1, W)
    sinks = attn_sinks.reshape(-1)
    g_in, b_in = r1(ln_in_g), r1(ln_in_b)

    xhat0, rstd0, u1, z, xbc, q, kv, dt_raw = _ln_in_proj(x2, g_in, b_in, sc1, sh1, w_pad, dims)
    y, yn, sprev = _conv_ssd(xbc, dt_raw, z, cw_full, conv_b, dtb, alog, dsk, ssd_norm_w, dims)
    ag_arr = _split_wait(ag_ss, ag_rs, ag_arr, _plan_gather(3), yn, "ag_ici_wait")
    fw_ss, fw_rs, ag_land, fw_token = _split_start(ag_arr[3:], _plan_forward(3), yn, "ag_fwd_start")
    o = _swa_fwd(q, kv, sinks + fw_token[0, 0], dims)
    wg_out, wg_ff1, wg_ff2 = _split_wait(fw_ss, fw_rs, ag_land, _plan_forward(3), o, "ag_fwd_wait")
    w_out_full = wg_out.reshape(MIX, D)
    w1_full = wg_ff1.transpose(1, 0, 2).reshape(D, FF)
    w2_full = wg_ff2.reshape(FF, D)
    mix, xhat1, rstd1, u2 = _out_proj_ln1(yn, o, w_out_full, xhat0, _vec8([g_in, b_in, g1, ln1_g, ln1_b, sc2, sh2], D), alpha)
    rr, dr2, acc_f, loss_loc = _mlp_loss(u2, w1_full, w2_full, xhat1, tgt,
                                         _vec8([ln1_g, ln1_b, g2, ln2_g, ln2_b, b_ff2], D), b_ff1, alpha)

    df, da, gb2, gb1 = _mlp_bwd_a(dr2, rr, w2_full, g2)
    gw_ff2 = _matmul_tn(rr, df, "gw_ff2", square_a=True)
    gw_ff1t = _matmul_tn(da, u2, "gw_ff1")
    dmix, dh0p, dyn, do, acc_b = _mlp_bwd_b(da, w1_full, dr2, xhat1, rstd1, mix, w_out_full,
                                            _vec8([ln1_g, ln1_b, sc2, g1], D), alpha, W)
    gw_out = jnp.concatenate([_matmul_tn(yn, dmix, "gw_out_ssd"), _matmul_tn(o, dmix, "gw_out_attn")], axis=0)

    core = jnp.reshape(ac, (1,)).astype(jnp.int32)
    blocked1 = [gw_out.reshape(N_DEV, MIX // N_DEV, D), gw_ff1t.reshape(N_DEV, FF // N_DEV, D),
                gw_ff2.reshape(N_DEV, FF // N_DEV, D)]
    lands1 = [lax.empty(b.shape, b.dtype) for b in blocked1]
    rs_ss, rs_rs, rs_arr, rs_token = _split_start(blocked1 + lands1, _plan_scatter_all(3), do, "rs_all_start")
    dq, dkv, dsink = _swa_bwd(q, kv, do, sinks + rs_token[0, 0], dims)
    dz, dpre, ddt, acc_s, hacc = _ssd_bwd(dyn, y, z, xbc, dt_raw, sprev, cw_full, conv_b, dtb + rs_token[0:1, 0:1], alog, dsk,
                                          ssd_norm_w, dims)
    dxbc, acc_c = _conv_bwd(dpre, xbc, cw_full)
    gw_in = _gw_in((dz, dxbc, ddt, dq, dkv), u1, dims)

    blocked2 = [_split_blocks(gw_in, N_DEV)]
    pairs2 = [_pair_sum(b, r, core) for b, r in zip(blocked2, _rs_d2d(blocked2, "rs_d2d_2"))]
    lands2 = [lax.empty(p.shape, p.dtype) for p in pairs2]
    r2_ss, r2_rs, r2_arr, r2_token = _split_start(pairs2 + lands2, _plan_scatter(1), gw_in, "rs_ici_start_2")
    grad_x, acc_i = _in_proj_bwd(dz, dxbc, dq, dkv, ddt, w_pad, xhat0, rstd0, dh0p,
                                 _vec8([g_in, b_in, sc1], D) + r2_token[0:1, 0:1], dims)

    srcs = [acc_i, acc_b, acc_f, acc_s, acc_c, hacc, dsink, gb1, gb2, loss_loc]
    I_, B_, F_, S_, C_, H_, K_, G1_, G2_, L_ = range(10)
    seg_of = dict(ada_b=[(I_, 1, D), (I_, 0, D), (B_, 4, D), (B_, 1, D), (B_, 0, D), (F_, 2, D)],
                  ln_in_g=[(I_, 2, D)], ln_in_b=[(I_, 3, D)], conv_b=[(C_, CONV_K, CD)], dt_bias=[(H_, 0, H)],
                  a_log=[(H_, 1, H)], d_skip=[(H_, 2, H)], ssd_norm_w=[(S_, 0, W)], attn_sinks=[(K_, 0, AH)],
                  ln1_g=[(B_, 2, D)], ln1_b=[(B_, 3, D)], b_ff1=[(G1_, 0, FF)], b_ff2=[(G2_, 0, D)],
                  ln2_g=[(F_, 0, D)], ln2_b=[(F_, 1, D)])
    pieces = [seg_of[n] for n in _SMALL] + [[(C_, t, CD) for t in range(CONV_K)], [(L_, 0, 1)]]
    params = [tuple(t[n].reshape(1, -1) for t in (wts, ms, vs)) for n in _SMALL]
    res = _small_sync_adamw(srcs, pieces, params, 6 * D)
    grads, deltas, new_m, new_v = {}, {}, {}, {}
    for k, n in enumerate(_SMALL):
        grads[n], deltas[n], new_m[n], new_v[n] = (t.reshape(wts[n].shape) for t in res[4 * k:4 * k + 4])
    gcw_full, dmod_all, loss_row = res[4 * len(_SMALL):]
    loss = loss_row[0, 0]

    g_ = lax.dynamic_slice(gcw_full, (0, me * CW), (CONV_K, CW))
    d_, m_, v_ = _adamw(conv_w[0], g_, m_conv_w[0], v_conv_w[0])
    grads['conv_w'], deltas['conv_w'], new_m['conv_w'], new_v['conv_w'] = (t[None] for t in (g_, d_, m_, v_))

    dmod_cols = lax.dynamic_slice(dmod_all, (0, me * C6), (N_DEV, C6))
    pad16 = lambda t: jnp.concatenate([t, jnp.zeros((16 - N_DEV,) + t.shape[1:], t.dtype)], axis=0)
    g_, d_, m_, v_ = _ada_grad_adamw(pad16(cs_all), pad16(dmod_cols), ada_w[0], m_ada_w[0], v_ada_w[0])
    grads['ada_w'], deltas['ada_w'], new_m['ada_w'], new_v['ada_w'] = (t[None] for t in (g_, d_, m_, v_))

    rs_arr = _split_wait(rs_ss, rs_rs, rs_arr, _plan_scatter_all(3), g_, "rs_all_wait")
    mychip = 2 * ax + ay
    chips = jnp.stack([(mychip + k) % N_CHIP for k in range(N_CHIP)]).astype(jnp.int32)
    devs = jnp.stack([(me + k) % N_DEV for k in range(N_DEV)]).astype(jnp.int32)
    for n, own, land in zip(('w_out', 'w_ff1', 'w_ff2'), rs_arr[:3], rs_arr[3:]):
        g_, d_, m_, v_ = _sum_adamw_split(own, land, devs, wts[n][0], ms[n][0], vs[n][0], transposed=(n == 'w_ff1'))
        grads[n], deltas[n], new_m[n], new_v[n] = (t[None] for t in (g_, d_, m_, v_))
    r2_arr = _split_wait(r2_ss, r2_rs, r2_arr, _plan_scatter(1), g_, "rs_ici_wait_2")
    g_, d_, m_, v_ = _sum_adamw_split(r2_arr[0], r2_arr[1], chips, wts['w_in'][0].T, ms['w_in'][0].T, vs['w_in'][0].T)
    grads['w_in'], deltas['w_in'], new_m['w_in'], new_v['w_in'] = (t.T[None] for t in (g_, d_, m_, v_))

    return (loss, grad_x.reshape(x.shape), *[grads[n] for n in _WEIGHTS], *[deltas[n] for n in _WEIGHTS],
            *[new_m[n] for n in _WEIGHTS], *[new_v[n] for n in _WEIGHTS])
```

```python
import functools
import math

import numpy as np
import jax
import jax.numpy as jnp
from jax import lax
from jax.experimental import pallas as pl
from jax.experimental.pallas import tpu as pltpu

F32 = jnp.float32
BF16 = jnp.bfloat16
MESH = pl.DeviceIdType.MESH

N_DEV = 8
N_CHIP = 4
HEAD_DIM = 64
SSD_GROUPS = 2
SSD_STATE = 128
CHUNK = 128
CONV_K = 4
GQA = 8
LANE = 128
HALO = 8
LN_EPS = 1e-5
RMS_EPS = 1e-5
NEG = -1e30
ADAM_LR, ADAM_B1, ADAM_B2, ADAM_EPS, ADAM_WD, ADAM_STEP = 0.001, 0.9, 0.999, 1e-08, 0.01, 10
V7X_VMEM_BYTES = 64 * 1024 * 1024
VMEM_LIMIT = V7X_VMEM_BYTES - 8 * 1024 * 1024
HI = lax.Precision.HIGHEST
MLP_TM = 512
MLP_SUB = 512
MLP_FC = 512
CONV_TM = 512
CONV_CB = 2048
ROW_CHUNK = 32


def _alibi_slopes(n):
    def pow2(m):
        start = 2.0 ** (-8.0 / m)
        return [start ** (i + 1) for i in range(m)]
    if math.log2(n).is_integer():
        s = pow2(n)
    else:
        c = 2 ** math.floor(math.log2(n))
        s = pow2(c) + pow2(2 * c)[0::2][: n - c]
    return [float(v) for v in np.array(s, dtype=np.float32)]


def _dot(a, b):
    return jnp.dot(a, b, preferred_element_type=F32)


def _dot_nt(a, b):
    return lax.dot_general(a, b, (((1,), (1,)), ((), ())), preferred_element_type=F32)


def _dot_tn(a, b):
    return lax.dot_general(a, b, (((0,), (0,)), ((), ())), preferred_element_type=F32)


def _dot_hi(a, b):
    return jnp.dot(a, b, precision=HI, preferred_element_type=F32)


def _sigmoid(x):
    return 0.5 * jnp.tanh(0.5 * x) + 0.5


def _softplus(x):
    return jnp.maximum(x, 0.0) + jnp.log(1.0 + jnp.exp(-jnp.abs(x)))


def _mean(x):
    return jnp.mean(x, axis=-1, keepdims=True)


def _ln_fwd(x):
    xc = x - _mean(x)
    rstd = lax.rsqrt(_mean(xc * xc) + LN_EPS)
    return xc * rstd, rstd


def _ln_bwd(dxhat, xhat, rstd):
    return rstd * (dxhat - _mean(dxhat) - xhat * _mean(dxhat * xhat))


def _colsum(x):
    return jnp.sum(x, axis=0, keepdims=True)


def _params(sem):
    return pltpu.CompilerParams(dimension_semantics=sem, vmem_limit_bytes=VMEM_LIMIT)


def _tile(i_map_cols, tm):
    return pl.BlockSpec((tm, i_map_cols), lambda i: (i, 0))


def _res(shape):
    return pl.BlockSpec(shape, lambda *_: (0,) * len(shape), pipeline_mode=pl.Buffered(1))


def _acc(shape):
    return pl.BlockSpec(shape, lambda *_: (0,) * len(shape))


def _S(shape, dtype):
    return jax.ShapeDtypeStruct(shape, dtype)


def _my_pos():
    return lax.axis_index("x"), lax.axis_index("y"), lax.axis_index("c")


def _peer(pos, k):
    x, y, c = pos
    px = 1 - x if k & 4 else x
    py = 1 - y if k & 2 else y
    pc = 1 - c if k & 1 else c
    return (px, py, pc)


def _lin(p):
    return 4 * p[0] + 2 * p[1] + p[2]


def _mod_fwd(c_loc, ada_w_loc, ada_b_cols):
    D = c_loc.shape[1]
    C6 = ada_w_loc.shape[1]

    def body(c_ref, w_ref, b_ref, cs_ref, mod_ref, call_ref, modp_ref, ssem, rsem):
        pos = _my_pos()
        me = _lin(pos)
        call_ref[me] = c_ref[...]
        sends = []
        for k in range(1, N_DEV):
            cp = pltpu.make_async_remote_copy(src_ref=c_ref, dst_ref=call_ref.at[me], send_sem=ssem.at[k - 1],
                                              recv_sem=rsem.at[k - 1], device_id=_peer(pos, k), device_id_type=MESH)
            cp.start()
            sends.append(cp)
        for k in range(1, N_DEV):
            src = _lin(_peer(pos, k))
            pltpu.make_async_remote_copy(src_ref=c_ref, dst_ref=call_ref.at[src], send_sem=ssem.at[k - 1],
                                         recv_sem=rsem.at[k - 1], device_id=pos, device_id_type=MESH).wait_recv()
        for cp in sends:
            cp.wait_send()
        call = jnp.concatenate([call_ref[b] for b in range(N_DEV)], axis=0)
        cs = call * _sigmoid(call)
        cs_ref[...] = cs
        modp = _dot(cs.astype(BF16), w_ref[...].astype(BF16)) + b_ref[...]
        for b in range(N_DEV):
            modp_ref[b] = modp[b:b + 1, :]
        mod_ref[me] = modp_ref[me]
        sends = []
        for k in range(1, N_DEV):
            peer = _peer(pos, k)
            cp = pltpu.make_async_remote_copy(src_ref=modp_ref.at[_lin(peer)], dst_ref=mod_ref.at[me],
                                              send_sem=ssem.at[N_DEV - 2 + k], recv_sem=rsem.at[N_DEV - 2 + k],
                                              device_id=peer, device_id_type=MESH)
            cp.start()
            sends.append(cp)
        for k in range(1, N_DEV):
            src = _lin(_peer(pos, k))
            pltpu.make_async_remote_copy(src_ref=modp_ref.at[src], dst_ref=mod_ref.at[src],
                                         send_sem=ssem.at[N_DEV - 2 + k], recv_sem=rsem.at[N_DEV - 2 + k],
                                         device_id=pos, device_id_type=MESH).wait_recv()
        for cp in sends:
            cp.wait_send()

    vm = pl.BlockSpec(memory_space=pltpu.VMEM)
    return pl.pallas_call(
        body, name="mod_fwd",
        out_shape=(_S((N_DEV, D), F32), _S((N_DEV, 1, C6), F32)),
        in_specs=[vm, vm, vm], out_specs=(vm, vm),
        scratch_shapes=[pltpu.VMEM((N_DEV, 1, D), F32), pltpu.VMEM((N_DEV, 1, C6), F32),
                        pltpu.SemaphoreType.DMA((2 * (N_DEV - 1),)), pltpu.SemaphoreType.DMA((2 * (N_DEV - 1),))],
        compiler_params=pltpu.CompilerParams(vmem_limit_bytes=VMEM_LIMIT),
    )(c_loc, ada_w_loc, ada_b_cols)


def _small_sync_adamw(srcs, pieces, params, n_mod):
    n_src, n_par = len(srcs), len(params)
    rows_of = [sum(-(-w // LANE) for _, _, w in seg) for seg in pieces]
    starts = [sum(rows_of[:k]) for k in range(len(pieces))]
    NR = -(-sum(rows_of) // 8) * 8
    cd = pieces[n_par][0][2]

    def seg_row(arr, k, width):
        r = starts[k]
        if width <= LANE:
            return arr[r:r + 1, 0:width]
        return jnp.concatenate([arr[r + q:r + q + 1, :] for q in range(width // LANE)], axis=1)

    def exchange(*refs):
        src = refs[:n_src]
        total_ref, dmod_ref, pack_ref, gat_ref, ssem, rsem = refs[n_src:]
        pos = _my_pos()
        me = _lin(pos)
        pack_ref[...] = jnp.zeros_like(pack_ref)
        for k, seg in enumerate(pieces):
            r = starts[k]
            for (si, row, width) in seg:
                for q in range(-(-width // LANE)):
                    wq = min(LANE, width - q * LANE)
                    pack_ref[r:r + 1, 0:wq] = src[si][row:row + 1, q * LANE:q * LANE + wq]
                    r += 1
        gat_ref[me] = pack_ref[...]
        sends = []
        for k in range(1, N_DEV):
            cp = pltpu.make_async_remote_copy(src_ref=pack_ref, dst_ref=gat_ref.at[me], send_sem=ssem.at[k - 1],
                                              recv_sem=rsem.at[k - 1], device_id=_peer(pos, k), device_id_type=MESH)
            cp.start()
            sends.append(cp)
        for k in range(1, N_DEV):
            frm = _lin(_peer(pos, k))
            pltpu.make_async_remote_copy(src_ref=pack_ref, dst_ref=gat_ref.at[frm], send_sem=ssem.at[k - 1],
                                         recv_sem=rsem.at[k - 1], device_id=pos, device_id_type=MESH).wait_recv()
        for cp in sends:
            cp.wait_send()
        total = gat_ref[0]
        for j in range(1, N_DEV):
            total = total + gat_ref[j]
        total_ref[...] = total
        for j in range(N_DEV):
            dmod_ref[j:j + 1, :] = seg_row(gat_ref[j], 0, n_mod)

    def update(*refs):
        total = refs[0][...]
        wmv = refs[1:1 + 3 * n_par]
        outs = refs[1 + 3 * n_par:]
        for k in range(n_par):
            n = params[k][0].shape[1]
            g = seg_row(total, k, n)
            w_ref, m_ref, v_ref = wmv[3 * k:3 * k + 3]
            g_ref, d_ref, m2_ref, v2_ref = outs[4 * k:4 * k + 4]
            g_ref[...] = g
            d_ref[...], m2_ref[...], v2_ref[...] = _adamw_math(w_ref[...], g, m_ref[...], v_ref[...])
        gcw_ref, loss_ref = outs[4 * n_par:]
        for t in range(CONV_K):
            r = starts[n_par] + t * (cd // LANE)
            gcw_ref[t:t + 1, :] = jnp.concatenate([total[r + q:r + q + 1, :] for q in range(cd // LANE)], axis=1)
        loss_ref[...] = total[starts[n_par + 1]:starts[n_par + 1] + 1, :]

    vm = pl.BlockSpec(memory_space=pltpu.VMEM)
    total, dmod_all = pl.pallas_call(
        exchange, name="small_sync", out_shape=(_S((NR, LANE), F32), _S((N_DEV, n_mod), F32)),
        in_specs=[vm] * n_src, out_specs=(vm, vm),
        scratch_shapes=[pltpu.VMEM((NR, LANE), F32), pltpu.VMEM((N_DEV, NR, LANE), F32),
                        pltpu.SemaphoreType.DMA((N_DEV - 1,)), pltpu.SemaphoreType.DMA((N_DEV - 1,))],
        compiler_params=pltpu.CompilerParams(vmem_limit_bytes=VMEM_LIMIT),
    )(*srcs)
    out_shape = []
    for w, _, _ in params:
        out_shape += [_S(w.shape, F32)] * 4
    out_shape += [_S((CONV_K, cd), F32), _S((1, LANE), F32)]
    flat = [t for p in params for t in p]
    res = pl.pallas_call(
        update, name="small_adamw", out_shape=tuple(out_shape),
        in_specs=[vm] * (1 + 3 * n_par), out_specs=tuple([vm] * len(out_shape)),
        compiler_params=pltpu.CompilerParams(vmem_limit_bytes=VMEM_LIMIT),
    )(total, *flat)
    return (*res[:-1], dmod_all, res[-1])


def _ag_weights(shards, after):
    n = len(shards)

    def body(*refs):
        ins, outs = refs[:n], refs[n + 1:2 * n + 1]
        ssem, rsem, lsem = refs[2 * n + 1:]
        x, y, c = pos = _my_pos()
        me = _lin(pos)
        sib = (x, y, 1 - c)
        chips = [(1 - x, y), (x, 1 - y), (1 - x, 1 - y)]

        def copy(a, k, block, to, src=None):
            return pltpu.make_async_remote_copy(
                src_ref=outs[a].at[block] if src is None else src, dst_ref=outs[a].at[block],
                send_sem=ssem.at[a * 7 + k], recv_sem=rsem.at[a * 7 + k], device_id=to, device_id_type=MESH)

        local = [pltpu.make_async_copy(ins[a], outs[a].at[me], lsem.at[a]) for a in range(n)]
        for cp in local:
            cp.start()
        first = []
        for a in range(n):
            first.append(copy(a, 0, me, sib, src=ins[a]))
            first += [copy(a, 1 + j, me, (*chip, c), src=ins[a]) for j, chip in enumerate(chips)]
        for cp in first:
            cp.start()
        passed = []
        for a in range(n):
            for j, chip in enumerate(chips):
                blk = _lin((*chip, c))
                copy(a, 1 + j, blk, pos).wait_recv()
                cp = copy(a, 4 + j, blk, sib)
                cp.start()
                passed.append(cp)
        for a in range(n):
            copy(a, 0, _lin(sib), pos).wait_recv()
            for j, chip in enumerate(chips):
                copy(a, 4 + j, _lin((*chip, 1 - c)), pos).wait_recv()
        for cp in first + passed:
            cp.wait_send()
        for cp in local:
            cp.wait()

    hbm = pl.BlockSpec(memory_space=pl.ANY)
    return pl.pallas_call(
        body, name="ag_weights",
        out_shape=tuple(_S((N_DEV,) + s.shape, s.dtype) for s in shards),
        in_specs=[hbm] * (n + 1), out_specs=tuple([hbm] * n),
        scratch_shapes=[pltpu.SemaphoreType.DMA((7 * n,)), pltpu.SemaphoreType.DMA((7 * n,)),
                        pltpu.SemaphoreType.DMA((n,))],
    )(*shards, after)


def _rs_d2d(blocked, name):
    n = len(blocked)

    def body(*refs):
        ins, outs = refs[:n], refs[n:2 * n]
        ssem, rsem = refs[2 * n:]
        x, y, c = pos = _my_pos()
        sib = (x, y, 1 - c)
        cps = []
        for a in range(n):
            for j in range(N_CHIP):
                cp = pltpu.make_async_remote_copy(
                    src_ref=ins[a].at[2 * j + (1 - c)], dst_ref=outs[a].at[j], send_sem=ssem.at[a * N_CHIP + j],
                    recv_sem=rsem.at[a * N_CHIP + j], device_id=sib, device_id_type=MESH)
                cp.start()
                cps.append(cp)
        for cp in cps:
            cp.wait_recv()
        for cp in cps:
            cp.wait_send()

    hbm = pl.BlockSpec(memory_space=pl.ANY)
    return pl.pallas_call(
        body, name=name,
        out_shape=tuple(_S((N_CHIP,) + b.shape[1:], b.dtype) for b in blocked),
        in_specs=[hbm] * n, out_specs=tuple([hbm] * n),
        scratch_shapes=[pltpu.SemaphoreType.DMA((N_CHIP * n,)), pltpu.SemaphoreType.DMA((N_CHIP * n,))],
    )(*blocked)


_HBM = pl.BlockSpec(memory_space=pltpu.HBM)
_SEM = pl.BlockSpec(memory_space=pltpu.SEMAPHORE)
_ANY = pl.BlockSpec(memory_space=pl.ANY)
_EFFECT = pltpu.SideEffectType.DATAFLOW_SIDE_EFFECTING


def _in_hbm(a):
    return pltpu.with_memory_space_constraint(a, pltpu.HBM)


def _plan_gather(n):
    def copies(pos):
        x, y, c = pos
        out = []
        for a in range(n):
            for dev in [(x, y, 1 - c)] + [(*_peer(pos, 2 * k)[:2], c) for k in range(1, N_CHIP)]:
                out.append((a, None, n + a, _lin(pos), dev, _lin(dev)))
        return out
    return copies


def _plan_forward(n):
    def copies(pos):
        x, y, c = pos
        out = []
        for a in range(n):
            for k in range(1, N_CHIP):
                tx, ty, _ = _peer(pos, 2 * k)
                out.append((a, _lin((tx, ty, c)), a, _lin((tx, ty, c)), (x, y, 1 - c), _lin((tx, ty, 1 - c))))
        return out
    return copies


def _plan_scatter_all(n):
    def copies(pos):
        out = []
        for a in range(n):
            for k in range(1, N_DEV):
                dev = _peer(pos, k)
                out.append((a, _lin(dev), n + a, _lin(pos), dev, _lin(dev)))
        return out
    return copies


def _plan_scatter(n):
    def copies(pos):
        x, y, c = pos
        out = []
        for a in range(n):
            for k in range(1, N_CHIP):
                tx, ty, _ = _peer(pos, 2 * k)
                out.append((a, 2 * tx + ty, n + a, 2 * x + y, (tx, ty, c), 2 * tx + ty))
        return out
    return copies


def _split_copy(refs, cp, ssem, rsem, i, arrival):
    si, s_slot, di, d_slot, dev, a_slot = cp
    return pltpu.make_async_remote_copy(
        src_ref=refs[si] if s_slot is None else refs[si].at[s_slot], dst_ref=refs[di].at[a_slot if arrival else d_slot],
        send_sem=ssem.at[i], recv_sem=rsem.at[i], device_id=dev, device_id_type=MESH)


def _split_start(arrays, copies, after, name):
    n = len(arrays)
    n_cp = len(copies((0, 0, 0)))

    def body(*refs):
        ssem, rsem, token = refs[n + 1], refs[n + 2], refs[-1]
        for i, cp in enumerate(copies(_my_pos())):
            _split_copy(refs, cp, ssem, rsem, i, False).start()
        token[...] = jnp.zeros_like(token)

    res = pl.pallas_call(
        body, name=name,
        out_shape=(pltpu.SemaphoreType.DMA((n_cp,)), pltpu.SemaphoreType.DMA((n_cp,)),
                   *[pltpu.HBM(a.shape, a.dtype) for a in arrays], _S((8, LANE), F32)),
        in_specs=[_HBM] * n + [_ANY],
        out_specs=(_SEM, _SEM, *[_HBM] * n, pl.BlockSpec(memory_space=pltpu.VMEM)),
        input_output_aliases={a: 2 + a for a in range(n)},
        compiler_params=pltpu.CompilerParams(has_side_effects=_EFFECT),
    )(*[_in_hbm(a) for a in arrays], after)
    return res[0], res[1], list(res[2:2 + n]), res[-1]


def _split_wait(ssem, rsem, arrays, copies, after, name):
    n = len(arrays)

    def body(*refs):
        for i, cp in enumerate(copies(_my_pos())):
            d = _split_copy(refs, cp, refs[n], refs[n + 1], i, True)
            d.wait_send()
            d.wait_recv()

    res = pl.pallas_call(
        body, name=name,
        out_shape=tuple(pltpu.HBM(a.shape, a.dtype) for a in arrays),
        in_specs=[_HBM] * n + [_SEM, _SEM, _ANY], out_specs=tuple([_HBM] * n),
        input_output_aliases={a: a for a in range(n)},
        compiler_params=pltpu.CompilerParams(has_side_effects=_EFFECT),
    )(*arrays, ssem, rsem, after)
    return list(res)


def _row_tile(R, itemsize_rows=16, cap=256):
    t = cap - cap % itemsize_rows
    while t >= itemsize_rows:
        if R % t == 0:
            return t
        t -= itemsize_rows
    return R


def _pair_sum(blocked, recv, core):
    _, R, C = blocked.shape
    tr = _row_tile(R)

    def body(ids_ref, a_ref, b_ref, o_ref):
        del ids_ref
        o_ref[...] = (a_ref[...].astype(F32) + b_ref[...].astype(F32)).astype(BF16)

    gs = pltpu.PrefetchScalarGridSpec(
        num_scalar_prefetch=1, grid=(N_CHIP, R // tr),
        in_specs=[pl.BlockSpec((1, tr, C), lambda j, r, ids: (2 * j + ids[0], r, 0)),
                  pl.BlockSpec((1, tr, C), lambda j, r, ids: (j, r, 0))],
        out_specs=pl.BlockSpec((1, tr, C), lambda j, r, ids: (j, r, 0)))
    return pl.pallas_call(body, name="pair_sum", grid_spec=gs, out_shape=_S((N_CHIP, R, C), BF16),
                          compiler_params=_params(("arbitrary", "arbitrary")))(core, blocked, recv)


def _adamw_math(w, g, m, v):
    m2 = ADAM_B1 * m + (1.0 - ADAM_B1) * g
    v2 = ADAM_B2 * v + (1.0 - ADAM_B2) * (g * g)
    m_hat = m2 / (1.0 - ADAM_B1 ** ADAM_STEP)
    v_hat = v2 / (1.0 - ADAM_B2 ** ADAM_STEP)
    delta = -ADAM_LR * (m_hat / (jnp.sqrt(v_hat) + ADAM_EPS) + ADAM_WD * w)
    return delta, m2, v2


def _sum_adamw_split(pairs, land, chips, w, m, v, transposed=False):
    R, C = w.shape
    n_slots = chips.shape[0]
    tr = _row_tile(R, 128, 256) if transposed else R
    tc = C if transposed else _pick(C, (256, 128))

    def body(ids_ref, *refs):
        del ids_ref
        parts, (w_ref, m_ref, v_ref, g_ref, d_ref, m2_ref, v2_ref) = refs[:n_slots], refs[n_slots:]
        g = parts[0][0].astype(F32)
        for p_ref in parts[1:]:
            g = g + p_ref[0].astype(F32)
        if transposed:
            g = g.T
        g_ref[...] = g
        d_ref[...], m2_ref[...], v2_ref[...] = _adamw_math(w_ref[...], g, m_ref[...], v_ref[...])

    if transposed:
        t = pl.BlockSpec((tr, C), lambda r, ids: (r, 0))
        slot = lambda k: pl.BlockSpec((1, C, tr), lambda r, ids: (ids[k], 0, r))
    else:
        t = pl.BlockSpec((R, tc), lambda c, ids: (0, c))
        slot = lambda k: pl.BlockSpec((1, R, tc), lambda c, ids: (ids[k], 0, c))
    gs = pltpu.PrefetchScalarGridSpec(num_scalar_prefetch=1, grid=((R // tr) * (C // tc),),
                                      in_specs=[slot(k) for k in range(n_slots)] + [t, t, t], out_specs=(t, t, t, t))
    return pl.pallas_call(body, name="sum_adamw_split", grid_spec=gs, out_shape=tuple(_S((R, C), F32) for _ in range(4)),
                          compiler_params=_params(("arbitrary",)))(chips, pairs, *[land] * (n_slots - 1), w, m, v)


def _adamw(w, g, m, v):
    R, C = w.shape
    tr = _row_tile(R, 8)

    def body(w_ref, g_ref, m_ref, v_ref, d_ref, m2_ref, v2_ref):
        d_ref[...], m2_ref[...], v2_ref[...] = _adamw_math(w_ref[...], g_ref[...], m_ref[...], v_ref[...])

    t = pl.BlockSpec((tr, C), lambda r: (r, 0))
    return pl.pallas_call(body, name="adamw", grid=(R // tr,), in_specs=[t, t, t, t], out_specs=(t, t, t),
                          out_shape=tuple(_S((R, C), F32) for _ in range(3)),
                          compiler_params=_params(("arbitrary",)))(w, g, m, v)


def _ada_grad_adamw(cs16, dmod16, w, m, v):
    D, C6 = w.shape
    tr = _row_tile(D, 8, 256)

    def body(cs_ref, dm_ref, w_ref, m_ref, v_ref, g_ref, d_ref, m2_ref, v2_ref):
        g = _dot_tn(cs_ref[...].astype(BF16), dm_ref[...].astype(BF16))
        g_ref[...] = g
        d_ref[...], m2_ref[...], v2_ref[...] = _adamw_math(w_ref[...], g, m_ref[...], v_ref[...])

    t = pl.BlockSpec((tr, C6), lambda r: (r, 0))
    return pl.pallas_call(
        body, name="ada_grad_adamw", grid=(D // tr,),
        in_specs=[pl.BlockSpec((16, tr), lambda r: (0, r)), _acc((16, C6)), t, t, t], out_specs=(t, t, t, t),
        out_shape=tuple(_S((D, C6), F32) for _ in range(4)), compiler_params=_params(("arbitrary",)))(cs16, dmod16, w, m, v)


def _pick(n, cands):
    for c in cands:
        if n % c == 0:
            return c
    return n


def _matmul_tn(a, b, name, square_a=False):
    L, K = a.shape
    N = b.shape[1]
    bk = _pick(K, (1024, 512, 256, 128))
    bn = _pick(N, (1024, 768, 512, 256, 128))
    tl = _pick(L, (1024, 512, 256, 128))
    n_l = L // tl

    def body(a_ref, b_ref, o_ref, acc_ref):
        l = pl.program_id(2)

        @pl.when(l == 0)
        def _():
            acc_ref[...] = jnp.zeros_like(acc_ref)
        av = a_ref[...]
        if square_a:
            av = av.astype(F32)
            av = av * av
        acc_ref[...] += _dot_tn(av.astype(BF16), b_ref[...].astype(BF16))

        @pl.when(l == n_l - 1)
        def _():
            o_ref[...] = acc_ref[...].astype(BF16)

    return pl.pallas_call(
        body, name=name, grid=(K // bk, N // bn, n_l),
        in_specs=[pl.BlockSpec((tl, bk), lambda k, n, l: (l, k)), pl.BlockSpec((tl, bn), lambda k, n, l: (l, n))],
        out_specs=pl.BlockSpec((bk, bn), lambda k, n, l: (k, n)), out_shape=_S((K, N), BF16),
        scratch_shapes=[pltpu.VMEM((bk, bn), F32)],
        compiler_params=_params(("arbitrary", "arbitrary", "arbitrary")))(a, b)


def _merge_blocks(a):
    n, R, C = a.shape
    cb = _pick(C, (256, 128))

    def body(i_ref, o_ref):
        for j in range(n):
            o_ref[R * j:R * (j + 1), :] = i_ref[j]

    return pl.pallas_call(body, name="merge_blocks", out_shape=_S((n * R, C), a.dtype), grid=(C // cb,),
                          in_specs=[pl.BlockSpec((n, R, cb), lambda c: (0, 0, c))],
                          out_specs=pl.BlockSpec((n * R, cb), lambda c: (0, c)),
                          compiler_params=_params(("arbitrary",)))(a)


def _split_blocks(a, n):
    NR, C = a.shape
    R = NR // n
    cb = _pick(C, (256, 128))

    def body(i_ref, o_ref):
        for j in range(n):
            o_ref[j] = i_ref[R * j:R * (j + 1), :].astype(BF16)

    return pl.pallas_call(body, name="split_blocks", out_shape=_S((n, R, C), BF16), grid=(C // cb,),
                          in_specs=[pl.BlockSpec((NR, cb), lambda c: (0, c))],
                          out_specs=pl.BlockSpec((n, R, cb), lambda c: (0, 0, c)),
                          compiler_params=_params(("arbitrary",)))(a)


def _gw_in(pieces, u1, dims):
    L, D = u1.shape
    H = dims["H"]
    r_z, r_xbc, r_dt, r_q, r_kv = _proj_rows(dims)
    PROJ = r_kv[1]
    tl = _pick(L, (512, 256, 128))
    n_l = L // tl

    def body(dz_ref, dxbc_ref, ddt_ref, dq_ref, dkv_ref, u_ref, o_ref):
        @pl.when(pl.program_id(0) == 0)
        def _():
            o_ref[...] = jnp.zeros_like(o_ref)
        u = u_ref[...]
        for ref, (r0, r1) in ((dz_ref, r_z), (dxbc_ref, r_xbc), (dq_ref, r_q), (dkv_ref, r_kv)):
            o_ref[r0:r1, :] += _dot_tn(ref[...].astype(BF16), u)
        o_ref[r_dt[0]:r_dt[0] + H, :] += _dot_tn(ddt_ref[...].astype(BF16), u)[0:H, :]

    return pl.pallas_call(
        body, name="gw_in", grid=(n_l,),
        in_specs=[_tile(p.shape[1], tl) for p in pieces] + [_tile(D, tl)],
        out_specs=_acc((PROJ, D)), out_shape=_S((PROJ, D), F32),
        compiler_params=_params(("arbitrary",)))(*pieces, u1)


def _proj_rows(dims):
    W, CD, H, AW, KVW2 = dims["W"], dims["CD"], dims["H"], dims["AW"], dims["KVW2"]
    o_dt = W + CD
    o_q = o_dt + H
    return (0, W), (W, o_dt), (o_dt, o_dt + LANE), (o_q, o_q + AW), (o_q + AW, o_q + AW + KVW2)


def _ln_in_proj(x, g, b, sc, sh, w_t, dims):
    L, D = x.shape
    W, CD, AW, KVW2 = dims["W"], dims["CD"], dims["AW"], dims["KVW2"]
    PROJ = w_t.shape[0]
    tm = _pick(L, (MLP_TM, 128))
    r_z, r_xbc, r_dt, r_q, r_kv = _proj_rows(dims)

    def body(x_ref, g_ref, b_ref, sc_ref, sh_ref, w_ref, xhat_ref, rstd_ref, u1_ref, z_ref, xbc_ref, q_ref, kv_ref, dt_ref):
        xhat, rstd = _ln_fwd(x_ref[...])
        xhat_ref[...] = xhat
        rstd_ref[...] = rstd
        h0 = xhat * g_ref[...] + b_ref[...]
        u1 = (h0 * (1.0 + sc_ref[...]) + sh_ref[...]).astype(BF16)
        u1_ref[...] = u1
        z_ref[...] = _dot_nt(u1, w_ref[r_z[0]:r_z[1], :])
        xbc_ref[...] = _dot_nt(u1, w_ref[r_xbc[0]:r_xbc[1], :])
        q_ref[...] = _dot_nt(u1, w_ref[r_q[0]:r_q[1], :]).astype(BF16)
        kv_ref[...] = _dot_nt(u1, w_ref[r_kv[0]:r_kv[1], :]).astype(BF16)
        dt_ref[...] = _dot_nt(u1, w_ref[r_dt[0]:r_dt[1], :])

    v = _acc((1, D))
    return pl.pallas_call(
        body, name="ln_in_proj", grid=(L // tm,),
        in_specs=[_tile(D, tm), v, v, v, v, _res((PROJ, D))],
        out_specs=(_tile(D, tm), _tile(1, tm), _tile(D, tm), _tile(W, tm), _tile(CD, tm), _tile(AW, tm),
                   _tile(KVW2, tm), _tile(LANE, tm)),
        out_shape=(_S((L, D), F32), _S((L, 1), F32), _S((L, D), BF16), _S((L, W), F32), _S((L, CD), F32),
                   _S((L, AW), BF16), _S((L, KVW2), BF16), _S((L, LANE), F32)),
        compiler_params=_params(("arbitrary",)))(x, g, b, sc, sh, w_t)


def _conv_act(cur_ref, prev_ref, cw_ref, cb_ref, ext_ref, first):
    T = cur_ref.shape[0]
    ext_ref[0:HALO, :] = jnp.where(first, 0.0, prev_ref[...])
    ext_ref[HALO:HALO + T, :] = cur_ref[...]
    pre = cb_ref[...] + cw_ref[0:1, :] * ext_ref[HALO - 3:HALO - 3 + T, :]
    for k in range(1, CONV_K):
        pre = pre + cw_ref[k:k + 1, :] * ext_ref[HALO - 3 + k:HALO - 3 + k + T, :]
    sig = _sigmoid(pre)
    return pre * sig, pre, sig


def _tri(T, upper=False):
    r = lax.broadcasted_iota(jnp.int32, (T, T), 0)
    c = lax.broadcasted_iota(jnp.int32, (T, T), 1)
    return (r <= c) if upper else (r >= c)


def _expand_heads_lanes(dst_ref, v, n_heads):
    for h in range(n_heads):
        dst_ref[:, h * HEAD_DIM:(h + 1) * HEAD_DIM] = jnp.broadcast_to(v[:, h:h + 1], (v.shape[0], HEAD_DIM))


def _head_onehot(n_heads):
    hd = np.arange(3 * LANE)[:, None] % LANE
    ch = np.arange(n_heads * HEAD_DIM)[None, :] // HEAD_DIM
    return jnp.asarray((hd == ch).astype(np.float32)).astype(BF16)


def _expand_heads(dst_ref, v, n_heads, onehot_ref):
    onehot = onehot_ref[...]
    v = jnp.where(lax.broadcasted_iota(jnp.int32, v.shape, 1) < n_heads, v, 0.0)
    hi = v.astype(BF16)
    r1 = v - hi.astype(F32)
    mid = r1.astype(BF16)
    lo = (r1 - mid.astype(F32)).astype(BF16)
    dst_ref[...] = _dot(jnp.concatenate([hi, mid, lo], axis=1), onehot)


def _head_reduce(v):
    wdt = v.shape[1]
    ch = lax.broadcasted_iota(jnp.int32, (wdt, LANE), 0)
    lo = lax.broadcasted_iota(jnp.int32, (wdt, LANE), 1) * HEAD_DIM
    onehot = ((ch >= lo) & (ch < lo + HEAD_DIM)).astype(BF16)
    hi = v.astype(BF16)
    rest = (v - hi.astype(F32)).astype(BF16)
    return _dot(hi, onehot) + _dot(rest, onehot)


def _conv_ssd(xbc, dt_raw, z, cw, cb, dtb, alog, dsk, nw, dims):
    L, CD = xbc.shape
    W, H, G, N = dims["W"], dims["H"], SSD_GROUPS, SSD_STATE
    T = CHUNK
    R = H // G
    GW = W // G
    nc = L // T
    HP = H * HEAD_DIM

    def body(xbc_ref, prev_ref, dt_ref, z_ref, cw_ref, cb_ref, dtb_ref, alog_ref, dsk_ref, nw_ref, oh_ref,
             y_ref, yn_ref, sp_ref, pre_ref, ext_ref, s_ref, ybuf_ref, dtx_ref, acx_ref, xb_ref):
        i = pl.program_id(0)

        @pl.when(i == 0)
        def _():
            s_ref[...] = jnp.zeros_like(s_ref)

        act, pre, _ = _conv_act(xbc_ref, prev_ref, cw_ref, cb_ref, ext_ref, i == 0)
        pre_ref[...] = pre
        xs = act[:, :W]
        dt = _softplus(dt_ref[...] + dtb_ref[...])
        a = dt * (-jnp.exp(alog_ref[...]))
        low = _tri(T)
        acum = _dot_hi(low.astype(F32), a)
        acum_t = acum.T
        _expand_heads(dtx_ref, dt, H, oh_ref)
        _expand_heads(acx_ref, acum, H, oh_ref)
        acx = acx_ref[...]
        lastx = acx[T - 1:T, :]
        xd = xs * dtx_ref[...]
        xb_ref[...] = xd.astype(BF16)
        xdb = (xd * jnp.exp(lastx - acx)).astype(BF16)
        ex = jnp.exp(acx)
        elx = jnp.exp(lastx)
        for g in range(G):
            gs = slice(g * GW, (g + 1) * GW)
            bgb = act[:, W + g * N:W + (g + 1) * N].astype(BF16)
            cgb = act[:, W + G * N + g * N:W + G * N + (g + 1) * N].astype(BF16)
            stg = s_ref[:, gs]
            sp_ref[0, :, gs] = stg
            yoff = ex[:, gs] * _dot(cgb, stg.astype(BF16))
            s_ref[:, gs] = stg * elx[:, gs] + _dot_tn(bgb, xdb[:, gs])
            cb_g = _dot_nt(cgb, bgb)
            for r in range(R):
                h = g * R + r
                hs = slice(h * HEAD_DIM, (h + 1) * HEAD_DIM)
                lm = jnp.where(low, jnp.exp(acum[:, h:h + 1] - acum_t[h:h + 1, :]), 0.0)
                ybuf_ref[:, hs] = _dot((cb_g * lm).astype(BF16), xb_ref[:, hs]) + yoff[:, r * HEAD_DIM:(r + 1) * HEAD_DIM]
        y = ybuf_ref[...] + dsk_ref[...] * xs
        y_ref[...] = y
        zz = z_ref[...]
        hh = y * (zz * _sigmoid(zz))
        for g in range(G):
            gs = slice(g * GW, (g + 1) * GW)
            hg = hh[:, gs]
            yn_ref[:, gs] = (hg * lax.rsqrt(_mean(hg * hg) + RMS_EPS) * nw_ref[:, gs]).astype(BF16)

    return pl.pallas_call(
        body, name="conv_ssd", grid=(nc,),
        in_specs=[_tile(CD, T), pl.BlockSpec((HALO, CD), lambda i: (jnp.maximum(i * (T // HALO) - 1, 0), 0)),
                  _tile(LANE, T), _tile(W, T), _acc((CONV_K, CD)), _acc((1, CD)), _acc((1, LANE)), _acc((1, LANE)),
                  _acc((1, W)), _acc((1, W)), _acc((3 * LANE, W))],
        out_specs=(_tile(W, T), _tile(W, T), pl.BlockSpec((1, N, HP), lambda i: (i, 0, 0)), _tile(CD, T)),
        out_shape=(_S((L, W), F32), _S((L, W), BF16), _S((nc, N, HP), F32), _S((L, CD), F32)),
        scratch_shapes=[pltpu.VMEM((T + HALO, CD), F32), pltpu.VMEM((N, HP), F32), pltpu.VMEM((T, W), F32),
                        pltpu.VMEM((T, W), F32), pltpu.VMEM((T, W), F32), pltpu.VMEM((T, W), BF16)],
        compiler_params=_params(("arbitrary",)))(xbc, xbc, dt_raw, z, cw, cb, dtb, alog, dsk, nw, _head_onehot(H))


def _attn_mask(T, i):
    r = lax.broadcasted_iota(jnp.int32, (T, 2 * T), 0)
    c = lax.broadcasted_iota(jnp.int32, (T, 2 * T), 1)
    dist = r + T - c
    valid = (dist >= 0) & (dist < CHUNK) & ((c >= T) | (i > 0))
    return dist.astype(F32), valid


def _attn_probs(s_raw, dist, valid, slope, sink, axis):
    s = s_raw * (HEAD_DIM ** -0.5) - slope * dist
    s = jnp.where(valid, s, NEG)
    m = jnp.maximum(jnp.max(s, axis=axis, keepdims=True), sink)
    p = jnp.exp(s - m)
    e_sink = jnp.exp(sink - m)
    inv = 1.0 / (jnp.sum(p, axis=axis, keepdims=True) + e_sink)
    return p * inv, e_sink * inv


def _kv_heads(kvc_ref, kvp_ref, g, n_kv):
    ks = slice(g * HEAD_DIM, (g + 1) * HEAD_DIM)
    vs = slice((n_kv + g) * HEAD_DIM, (n_kv + g + 1) * HEAD_DIM)
    kk = jnp.concatenate([kvp_ref[:, ks], kvc_ref[:, ks]], axis=0)
    vv = jnp.concatenate([kvp_ref[:, vs], kvc_ref[:, vs]], axis=0)
    return kk, vv


def _swa_fwd(q, kv, sinks, dims):
    L, AW = q.shape
    KV, KVW2 = dims["KV"], dims["KVW2"]
    T = CHUNK
    nb = L // T
    slopes = _alibi_slopes(dims["AH"])

    def body(q_ref, kvc_ref, kvp_ref, sink_ref, o_ref, qg_ref, p_ref):
        i = pl.program_id(0)
        dist, valid = _attn_mask(T, i)
        for g in range(KV):
            kk, vv = _kv_heads(kvc_ref, kvp_ref, g, KV)
            for r in range(GQA):
                h = g * GQA + r
                qg_ref[r * T:(r + 1) * T, :] = q_ref[:, h * HEAD_DIM:(h + 1) * HEAD_DIM]
            s_all = _dot_nt(qg_ref[...], kk)
            for r in range(GQA):
                h = g * GQA + r
                p, _ = _attn_probs(s_all[r * T:(r + 1) * T, :], dist, valid, slopes[h], sink_ref[h], -1)
                p_ref[r * T:(r + 1) * T, :] = p.astype(BF16)
            o_all = _dot(p_ref[...], vv)
            for r in range(GQA):
                h = g * GQA + r
                o_ref[:, h * HEAD_DIM:(h + 1) * HEAD_DIM] = o_all[r * T:(r + 1) * T, :].astype(BF16)

    return pl.pallas_call(
        body, name="swa_fwd", grid=(nb,),
        in_specs=[_tile(AW, T), _tile(KVW2, T), pl.BlockSpec((T, KVW2), lambda i: (jnp.maximum(i - 1, 0), 0)),
                  pl.BlockSpec(memory_space=pltpu.SMEM)],
        out_specs=_tile(AW, T), out_shape=_S((L, AW), BF16),
        scratch_shapes=[pltpu.VMEM((GQA * T, HEAD_DIM), BF16), pltpu.VMEM((GQA * T, 2 * T), BF16)],
        compiler_params=_params(("arbitrary",)))(q, kv, kv, sinks)


def _out_proj_ln1(yn, o, w_out, xhat0, vecs, alpha):
    L, W = yn.shape
    D = xhat0.shape[1]
    MIX = w_out.shape[0]
    tm = _pick(L, (MLP_TM, 128))

    def body(yn_ref, o_ref, w_ref, xh_ref, v_ref, mix_ref, xhat1_ref, rstd1_ref, u2_ref):
        mix = _dot(yn_ref[...], w_ref[0:W, :]) + _dot(o_ref[...], w_ref[W:MIX, :])
        mix_ref[...] = mix
        h0 = xh_ref[...] * v_ref[0:1, :] + v_ref[1:2, :]
        xhat1, rstd1 = _ln_fwd(alpha * h0 + (1.0 + v_ref[2:3, :]) * mix)
        xhat1_ref[...] = xhat1
        rstd1_ref[...] = rstd1
        h1 = xhat1 * v_ref[3:4, :] + v_ref[4:5, :]
        u2_ref[...] = (h1 * (1.0 + v_ref[5:6, :]) + v_ref[6:7, :]).astype(BF16)

    return pl.pallas_call(
        body, name="out_proj_ln1", grid=(L // tm,),
        in_specs=[_tile(W, tm), _tile(MIX - W, tm), _res((MIX, D)), _tile(D, tm), _acc((8, D))],
        out_specs=(_tile(D, tm), _tile(D, tm), _tile(1, tm), _tile(D, tm)),
        out_shape=(_S((L, D), F32), _S((L, D), F32), _S((L, 1), F32), _S((L, D), BF16)),
        compiler_params=_params(("arbitrary",)))(yn, o, w_out, xhat0, vecs)


def _mlp_loss(u2, w1, w2, xhat1, tgt, vecs, b1, alpha):
    L, D = xhat1.shape
    FF = w1.shape[1]
    tm = _pick(L, (MLP_TM, 128))
    sub = min(tm, MLP_SUB)
    fc = _pick(FF, (MLP_FC, 256, 128))

    def body(u2_ref, w1_ref, w2_ref, xh_ref, t_ref, v_ref, b1_ref, rr_ref, dr2_ref, acc_ref, loss_ref):
        @pl.when(pl.program_id(0) == 0)
        def _():
            acc_ref[...] = jnp.zeros_like(acc_ref)
            loss_ref[...] = jnp.zeros_like(loss_ref)

        for s in range(tm // sub):
            rs = slice(s * sub, (s + 1) * sub)
            u2 = u2_ref[rs, :]
            f = jnp.zeros((sub, D), F32) + v_ref[5:6, :]
            for j in range(FF // fc):
                cs = slice(j * fc, (j + 1) * fc)
                rr = jnp.maximum(_dot(u2, w1_ref[:, cs]) + b1_ref[:, cs], 0.0)
                rr_ref[rs, cs] = rr.astype(BF16)
                f = f + _dot((rr * rr).astype(BF16), w2_ref[cs, :])
            xhat1 = xh_ref[rs, :]
            h1 = xhat1 * v_ref[0:1, :] + v_ref[1:2, :]
            xhat2, rstd2 = _ln_fwd(alpha * h1 + (1.0 + v_ref[2:3, :]) * f)
            e = xhat2 * v_ref[3:4, :] + v_ref[4:5, :] - t_ref[rs, :]
            loss_ref[...] += 0.5 * jnp.sum(_mean(e * e))
            dy = e * (1.0 / D)
            dr2 = _ln_bwd(dy * v_ref[3:4, :], xhat2, rstd2)
            dr2_ref[rs, :] = dr2
            acc_ref[0:1, :] += _colsum(dy * xhat2)
            acc_ref[1:2, :] += _colsum(dy)
            acc_ref[2:3, :] += _colsum(dr2 * f)

    return pl.pallas_call(
        body, name="mlp_loss", grid=(L // tm,),
        in_specs=[_tile(D, tm), _res((D, FF)), _res((FF, D)), _tile(D, tm), _tile(D, tm), _acc((8, D)), _acc((1, FF))],
        out_specs=(_tile(FF, tm), _tile(D, tm), _acc((8, D)), _acc((1, LANE))),
        out_shape=(_S((L, FF), BF16), _S((L, D), F32), _S((8, D), F32), _S((1, LANE), F32)),
        compiler_params=_params(("arbitrary",)))(u2, w1, w2, xhat1, tgt, vecs, b1)


def _mlp_bwd_a(dr2, rr, w2, g2):
    L, D = dr2.shape
    FF = w2.shape[0]
    tm = _pick(L, (MLP_TM, 128))
    fc = _pick(FF, (MLP_FC, 256, 128))

    def body(dr2_ref, rr_ref, w2_ref, g2_ref, df_ref, da_ref, gb2_ref, gb1_ref):
        @pl.when(pl.program_id(0) == 0)
        def _():
            gb2_ref[...] = jnp.zeros_like(gb2_ref)
            gb1_ref[...] = jnp.zeros_like(gb1_ref)

        df = (1.0 + g2_ref[...]) * dr2_ref[...]
        gb2_ref[...] += _colsum(df)
        dfb = df.astype(BF16)
        df_ref[...] = dfb
        for j in range(FF // fc):
            cs = slice(j * fc, (j + 1) * fc)
            da = _dot_nt(dfb, w2_ref[cs, :]) * (2.0 * rr_ref[:, cs].astype(F32))
            gb1_ref[:, cs] += _colsum(da)
            da_ref[:, cs] = da.astype(BF16)

    return pl.pallas_call(
        body, name="mlp_bwd_a", grid=(L // tm,),
        in_specs=[_tile(D, tm), _tile(FF, tm), _res((FF, D)), _acc((1, D))],
        out_specs=(_tile(D, tm), _tile(FF, tm), _acc((1, D)), _acc((1, FF))),
        out_shape=(_S((L, D), BF16), _S((L, FF), BF16), _S((1, D), F32), _S((1, FF), F32)),
        compiler_params=_params(("arbitrary",)))(dr2, rr, w2, g2)


def _mlp_bwd_b(da, w1, dr2, xhat1, rstd1, mix, w_out, vecs, alpha, W):
    L, FF = da.shape
    D = dr2.shape[1]
    MIX = w_out.shape[0]
    tm = _pick(L, (MLP_TM, 128))

    def body(da_ref, w1_ref, dr2_ref, xh_ref, rs_ref, mix_ref, wo_ref, v_ref, dmix_ref, dh0_ref, dyn_ref, do_ref, acc_ref):
        @pl.when(pl.program_id(0) == 0)
        def _():
            acc_ref[...] = jnp.zeros_like(acc_ref)

        du2 = _dot_nt(da_ref[...], w1_ref[...])
        xhat1 = xh_ref[...]
        h1 = xhat1 * v_ref[0:1, :] + v_ref[1:2, :]
        acc_ref[0:1, :] += _colsum(du2 * h1)
        acc_ref[1:2, :] += _colsum(du2)
        dh1 = alpha * dr2_ref[...] + du2 * (1.0 + v_ref[2:3, :])
        acc_ref[2:3, :] += _colsum(dh1 * xhat1)
        acc_ref[3:4, :] += _colsum(dh1)
        dr1 = _ln_bwd(dh1 * v_ref[0:1, :], xhat1, rs_ref[...])
        acc_ref[4:5, :] += _colsum(dr1 * mix_ref[...])
        dh0_ref[...] = alpha * dr1
        dmix = ((1.0 + v_ref[3:4, :]) * dr1).astype(BF16)
        dmix_ref[...] = dmix
        dyn_ref[...] = _dot_nt(dmix, wo_ref[0:W, :])
        do_ref[...] = _dot_nt(dmix, wo_ref[W:MIX, :]).astype(BF16)

    return pl.pallas_call(
        body, name="mlp_bwd_b", grid=(L // tm,),
        in_specs=[_tile(FF, tm), _res((D, FF)), _tile(D, tm), _tile(D, tm), _tile(1, tm), _tile(D, tm), _res((MIX, D)),
                  _acc((8, D))],
        out_specs=(_tile(D, tm), _tile(D, tm), _tile(W, tm), _tile(MIX - W, tm), _acc((8, D))),
        out_shape=(_S((L, D), BF16), _S((L, D), F32), _S((L, W), F32), _S((L, MIX - W), BF16), _S((8, D), F32)),
        compiler_params=_params(("arbitrary",)))(da, w1, dr2, xhat1, rstd1, mix, w_out, vecs)


def _swa_bwd(q, kv, do, sinks, dims):
    L, AW = q.shape
    KV, KVW2 = dims["KV"], dims["KVW2"]
    T = CHUNK
    nb = L // T
    slopes = _alibi_slopes(dims["AH"])
    scale = HEAD_DIM ** -0.5

    def body(q_ref, kvc_ref, kvp_ref, do_ref, sink_ref, dq_ref, dkv_ref, dsink_ref, carry_ref,
             qg_ref, dog_ref, pt_ref, dst_ref):
        i = pl.program_id(0)

        @pl.when(i == 0)
        def _():
            carry_ref[...] = jnp.zeros_like(carry_ref)
            dsink_ref[...] = jnp.zeros_like(dsink_ref)

        @pl.when(i < nb)
        def _():
            c = lax.broadcasted_iota(jnp.int32, (2 * T, T), 0)
            r_ = lax.broadcasted_iota(jnp.int32, (2 * T, T), 1)
            dist_i = r_ + T - c
            valid = (dist_i >= 0) & (dist_i < CHUNK) & ((c >= T) | (i > 0))
            dist = dist_i.astype(F32)
            lane = lax.broadcasted_iota(jnp.int32, (1, LANE), 1)
            dsink = jnp.zeros((1, LANE), F32)
            dks, dvs = [], []
            for g in range(KV):
                kk, vv = _kv_heads(kvc_ref, kvp_ref, g, KV)
                for r in range(GQA):
                    hs = slice((g * GQA + r) * HEAD_DIM, (g * GQA + r + 1) * HEAD_DIM)
                    qg_ref[r * T:(r + 1) * T, :] = q_ref[:, hs]
                    dog_ref[r * T:(r + 1) * T, :] = do_ref[:, hs]
                st_all = _dot_nt(kk, qg_ref[...])
                dpt_all = _dot_nt(vv, dog_ref[...])
                for r in range(GQA):
                    h = g * GQA + r
                    cs = slice(r * T, (r + 1) * T)
                    p, p_sink = _attn_probs(st_all[:, cs], dist, valid, slopes[h], sink_ref[h], 0)
                    dp = dpt_all[:, cs]
                    delta = jnp.sum(p * dp, axis=0, keepdims=True)
                    pt_ref[:, cs] = p.astype(BF16)
                    dst_ref[:, cs] = (p * (dp - delta)).astype(BF16)
                    dsink = dsink + jnp.where(lane == h, -jnp.sum(p_sink * delta), 0.0)
                dst = dst_ref[...]
                dks.append(_dot(dst, qg_ref[...]) * scale)
                dvs.append(_dot(pt_ref[...], dog_ref[...]))
                dq_all = _dot_tn(dst, kk) * scale
                for r in range(GQA):
                    hs = slice((g * GQA + r) * HEAD_DIM, (g * GQA + r + 1) * HEAD_DIM)
                    dq_ref[:, hs] = dq_all[r * T:(r + 1) * T, :].astype(BF16)
            dkv = jnp.concatenate(dks + dvs, axis=1)
            dsink_ref[...] += dsink
            dkv_ref[...] = carry_ref[...] + dkv[0:T, :]
            carry_ref[...] = dkv[T:2 * T, :]

        @pl.when(i == nb)
        def _():
            dkv_ref[...] = carry_ref[...]

    last = nb - 1
    return pl.pallas_call(
        body, name="swa_bwd", grid=(nb + 1,),
        in_specs=[pl.BlockSpec((T, AW), lambda i: (jnp.minimum(i, last), 0)),
                  pl.BlockSpec((T, KVW2), lambda i: (jnp.minimum(i, last), 0)),
                  pl.BlockSpec((T, KVW2), lambda i: (jnp.clip(i - 1, 0, last), 0)),
                  pl.BlockSpec((T, AW), lambda i: (jnp.minimum(i, last), 0)),
                  pl.BlockSpec(memory_space=pltpu.SMEM)],
        out_specs=(pl.BlockSpec((T, AW), lambda i: (jnp.minimum(i, last), 0)),
                   pl.BlockSpec((T, KVW2), lambda i: (jnp.maximum(i - 1, 0), 0)), _acc((1, LANE))),
        out_shape=(_S((L, AW), BF16), _S((L, KVW2), F32), _S((1, LANE), F32)),
        scratch_shapes=[pltpu.VMEM((T, KVW2), F32), pltpu.VMEM((GQA * T, HEAD_DIM), BF16),
                        pltpu.VMEM((GQA * T, HEAD_DIM), BF16), pltpu.VMEM((2 * T, GQA * T), BF16),
                        pltpu.VMEM((2 * T, GQA * T), BF16)],
        compiler_params=_params(("arbitrary",)))(q, kv, kv, do, sinks)


def _ssd_bwd(dyn, y, z, pre_act, dt_raw, sprev, dtb, alog, dsk, nw, dims):
    L, CD = pre_act.shape
    W, H, G, N = dims["W"], dims["H"], SSD_GROUPS, SSD_STATE
    T = CHUNK
    R = H // G
    GW = W // G
    nc = L // T
    HP = H * HEAD_DIM

    def body(dyn_ref, y_ref, z_ref, pre_ref, dt_ref, sp_ref, dtb_ref, alog_ref, dsk_ref, nw_ref,
             dz_ref, dpre_ref, ddt_ref, acc_ref, hacc_ref, ds_ref, dtx_ref, acx_ref, xb_ref, dyb_ref, r12_ref,
             dx_ref, rows_ref):
        i = pl.program_id(0)

        @pl.when(i == 0)
        def _():
            ds_ref[...] = jnp.zeros_like(ds_ref)
            acc_ref[...] = jnp.zeros_like(acc_ref)
            hacc_ref[...] = jnp.zeros_like(hacc_ref)

        pre = pre_ref[...]
        spre = _sigmoid(pre)
        act = pre * spre
        xs = act[:, :W]
        dt_in = dt_ref[...] + dtb_ref[...]
        dt = _softplus(dt_in)
        a_neg = -jnp.exp(alog_ref[...])
        a = dt * a_neg
        low = _tri(T)
        upf = _tri(T, upper=True).astype(F32)
        acum = _dot_hi(low.astype(F32), a)
        acum_t = acum.T

        y = y_ref[...]
        zz = z_ref[...]
        sg = _sigmoid(zz)
        sz = zz * sg
        hh = y * sz
        dyn_v = dyn_ref[...]
        parts = []
        for g in range(G):
            gs = slice(g * GW, (g + 1) * GW)
            hg = hh[:, gs]
            hhat = hg * lax.rsqrt(_mean(hg * hg) + RMS_EPS)
            rg = lax.rsqrt(_mean(hg * hg) + RMS_EPS)
            acc_ref[0:1, gs] += _colsum(dyn_v[:, gs] * hhat)
            dhhat = dyn_v[:, gs] * nw_ref[:, gs]
            parts.append(rg * (dhhat - hhat * _mean(dhhat * hhat)))
        dhh = jnp.concatenate(parts, axis=1)
        dy = dhh * sz
        dz_ref[...] = (dhh * y * (sg * (1.0 + zz * (1.0 - sg)))).astype(BF16)
        acc_ref[1:2, :] += _colsum(dy * xs)
        dyb_ref[...] = dy.astype(BF16)

        _expand_heads_lanes(dtx_ref, dt, H)
        _expand_heads_lanes(acx_ref, acum, H)
        dtx = dtx_ref[...]
        acx = acx_ref[...]
        lastx = acx[T - 1:T, :]
        ex = jnp.exp(acx)
        decx = jnp.exp(lastx - acx)
        elx = jnp.exp(lastx)
        xd = xs * dtx
        xb_ref[...] = xd.astype(BF16)
        xdecb = (xd * decx).astype(BF16)
        dgb = (ex * dy).astype(BF16)
        rows_ref[...] = jnp.zeros_like(rows_ref)

        lane = lax.broadcasted_iota(jnp.int32, (T, LANE), 1)
        sub = lax.broadcasted_iota(jnp.int32, (T, LANE), 0)
        subr = lax.broadcasted_iota(jnp.int32, (LANE, T), 0)
        da_col = jnp.zeros((T, LANE), F32)
        da_row = jnp.zeros((LANE, T), F32)
        dbs, dcs = [], []
        for g in range(G):
            gs = slice(g * GW, (g + 1) * GW)
            bgb = act[:, W + g * N:W + (g + 1) * N].astype(BF16)
            cgb = act[:, W + G * N + g * N:W + G * N + (g + 1) * N].astype(BF16)
            stg = sp_ref[0, :, gs]
            stb = stg.astype(BF16)
            dsn = ds_ref[:, gs]
            dsnb = dsn.astype(BF16)
            gm = _dot(cgb, stb)
            dc = _dot_nt(dgb[:, gs], stb)
            dsp = _dot_tn(cgb, dgb[:, gs])
            dxs_ = decx[:, gs] * _dot(bgb, dsnb)
            db = _dot_nt(xdecb[:, gs], dsnb)
            xdg = xd[:, gs]
            r12_ref[:, gs] = dy[:, gs] * ex[:, gs] * gm - xdg * dxs_
            rows_ref[0:1, gs] = _colsum(dsn * stg) * elx[:, gs]
            rows_ref[1:2, gs] = _colsum(xdg * dxs_)
            ds_ref[:, gs] = dsp + dsn * elx[:, gs]
            cb_g = _dot_nt(cgb, bgb)
            dcb = jnp.zeros((T, T), F32)
            for r in range(R):
                h = g * R + r
                hs = slice(h * HEAD_DIM, (h + 1) * HEAD_DIM)
                lm = jnp.where(low, jnp.exp(acum[:, h:h + 1] - acum_t[h:h + 1, :]), 0.0)
                mm = cb_g * lm
                dyb = dyb_ref[:, hs]
                dm = _dot_nt(dyb, xb_ref[:, hs])
                dx_ref[:, hs] = dxs_[:, r * HEAD_DIM:(r + 1) * HEAD_DIM] + _dot_tn(mm.astype(BF16), dyb)
                dcb = dcb + dm * lm
                qm = dm * mm
                da_col = jnp.where(lane == h, jnp.sum(qm, axis=1, keepdims=True), da_col)
                da_row = jnp.where(subr == h, jnp.sum(qm, axis=0, keepdims=True), da_row)
            dcbb = dcb.astype(BF16)
            dcs.append(dc + _dot(dcbb, bgb))
            dbs.append(db + _dot_tn(dcbb, cgb))
        dx = dx_ref[...]
        rows = _head_reduce(rows_ref[...])
        dlast = rows[0:1, :] + rows[1:2, :]
        da_col = da_col + _head_reduce(r12_ref[...]) + jnp.where(sub == T - 1, dlast, 0.0)
        dacum = da_col - da_row.T
        da = _dot_hi(upf, dacum)
        ddt = _head_reduce(dx * xs) + da * a_neg
        hacc_ref[1:2, :] += _colsum(da * dt) * a_neg
        ddt_raw = ddt * _sigmoid(dt_in)
        hacc_ref[0:1, :] += _colsum(ddt_raw)
        ddt_ref[...] = ddt_raw
        dact = jnp.concatenate([dsk_ref[...] * dy + dx * dtx] + dbs + dcs, axis=1)
        dpre_ref[...] = dact * (spre * (1.0 + pre * (1.0 - spre)))

        @pl.when(i == nc - 1)
        def _():
            ch = lax.broadcasted_iota(jnp.int32, (W, LANE), 0)
            lo = lax.broadcasted_iota(jnp.int32, (W, LANE), 1) * HEAD_DIM
            hacc_ref[2:3, :] = _dot_hi(acc_ref[1:2, :], ((ch >= lo) & (ch < lo + HEAD_DIM)).astype(F32))

    rev = lambda i: (nc - 1 - i, 0)
    return pl.pallas_call(
        body, name="ssd_bwd", grid=(nc,),
        in_specs=[pl.BlockSpec((T, W), rev), pl.BlockSpec((T, W), rev), pl.BlockSpec((T, W), rev), pl.BlockSpec((T, CD), rev),
                  pl.BlockSpec((T, LANE), rev), pl.BlockSpec((1, N, HP), lambda i: (nc - 1 - i, 0, 0)),
                  _acc((1, LANE)), _acc((1, LANE)), _acc((1, W)), _acc((1, W))],
        out_specs=(pl.BlockSpec((T, W), rev), pl.BlockSpec((T, CD), rev), pl.BlockSpec((T, LANE), rev), _acc((8, W)),
                   _acc((8, LANE))),
        out_shape=(_S((L, W), BF16), _S((L, CD), F32), _S((L, LANE), F32), _S((8, W), F32), _S((8, LANE), F32)),
        scratch_shapes=[pltpu.VMEM((N, HP), F32), pltpu.VMEM((T, W), F32),
                        pltpu.VMEM((T, W), F32), pltpu.VMEM((T, W), BF16), pltpu.VMEM((T, W), BF16), pltpu.VMEM((T, W), F32),
                        pltpu.VMEM((T, W), F32), pltpu.VMEM((8, W), F32)],
        compiler_params=_params(("arbitrary",)))(dyn, y, z, pre_act, dt_raw, sprev, dtb, alog, dsk, nw)


def _conv_bwd(dpre, xbc, cw):
    L, CD = xbc.shape
    tm = _pick(L, (CONV_TM, 128))
    cb = CD if CONV_CB >= CD else _pick(CD, (CONV_CB, 128))
    nt = L // tm
    hb = tm // HALO

    def body(dp_ref, dn_ref, u_ref, cw_ref, du_ref, acc_ref, extd_ref):
        i = pl.program_id(1)

        @pl.when(i == 0)
        def _():
            acc_ref[...] = jnp.zeros_like(acc_ref)

        extd_ref[0:tm, :] = dp_ref[...]
        extd_ref[tm:tm + HALO, :] = jnp.where(i == nt - 1, 0.0, dn_ref[...])
        for c in range(tm // ROW_CHUNK):
            r0 = c * ROW_CHUNK
            rows = slice(r0, r0 + ROW_CHUNK)
            dp = dp_ref[rows, :]
            u = u_ref[rows, :]
            du = cw_ref[CONV_K - 1:CONV_K, :] * dp
            acc_ref[CONV_K - 1:CONV_K, :] += _colsum(dp * u)
            for k in range(CONV_K - 1):
                s = CONV_K - 1 - k
                dsh = extd_ref[r0 + s:r0 + s + ROW_CHUNK, :]
                du = du + cw_ref[k:k + 1, :] * dsh
                acc_ref[k:k + 1, :] += _colsum(u * dsh)
            acc_ref[CONV_K:CONV_K + 1, :] += _colsum(dp)
            du_ref[rows, :] = du.astype(BF16)

    tile = pl.BlockSpec((tm, cb), lambda j, i: (i, j))
    return pl.pallas_call(
        body, name="conv_bwd", grid=(CD // cb, nt),
        in_specs=[tile, pl.BlockSpec((HALO, cb), lambda j, i: (jnp.minimum((i + 1) * hb, nt * hb - 1), j)),
                  tile, pl.BlockSpec((CONV_K, cb), lambda j, i: (0, j))],
        out_specs=(tile, pl.BlockSpec((8, cb), lambda j, i: (0, j))),
        out_shape=(_S((L, CD), BF16), _S((8, CD), F32)),
        scratch_shapes=[pltpu.VMEM((tm + HALO, cb), F32)],
        compiler_params=_params(("arbitrary", "arbitrary")))(dpre, dpre, xbc, cw)


def _in_proj_bwd(dz, dxbc, dq, dkv, ddt, w_t, xhat0, rstd0, dh0p, vecs, dims):
    L, D = xhat0.shape
    W, CD, AW, KVW2 = dims["W"], dims["CD"], dims["AW"], dims["KVW2"]
    PROJ = w_t.shape[0]
    tm = _pick(L, (MLP_TM, 128))
    r_z, r_xbc, r_dt, r_q, r_kv = _proj_rows(dims)

    def body(dz_ref, dxbc_ref, dq_ref, dkv_ref, ddt_ref, w_ref, xh_ref, rs_ref, dh0_ref, v_ref, gx_ref, acc_ref):
        @pl.when(pl.program_id(0) == 0)
        def _():
            acc_ref[...] = jnp.zeros_like(acc_ref)

        du1 = _dot(dz_ref[...], w_ref[r_z[0]:r_z[1], :])
        du1 = du1 + _dot(dxbc_ref[...], w_ref[r_xbc[0]:r_xbc[1], :])
        du1 = du1 + _dot(dq_ref[...], w_ref[r_q[0]:r_q[1], :])
        du1 = du1 + _dot(dkv_ref[...].astype(BF16), w_ref[r_kv[0]:r_kv[1], :])
        du1 = du1 + _dot(ddt_ref[...].astype(BF16), w_ref[r_dt[0]:r_dt[1], :])
        xhat0 = xh_ref[...]
        h0 = xhat0 * v_ref[0:1, :] + v_ref[1:2, :]
        acc_ref[0:1, :] += _colsum(du1 * h0)
        acc_ref[1:2, :] += _colsum(du1)
        dh0 = dh0_ref[...] + du1 * (1.0 + v_ref[2:3, :])
        acc_ref[2:3, :] += _colsum(dh0 * xhat0)
        acc_ref[3:4, :] += _colsum(dh0)
        gx_ref[...] = _ln_bwd(dh0 * v_ref[0:1, :], xhat0, rs_ref[...])

    return pl.pallas_call(
        body, name="in_proj_bwd", grid=(L // tm,),
        in_specs=[_tile(W, tm), _tile(CD, tm), _tile(AW, tm), _tile(KVW2, tm), _tile(LANE, tm), _res((PROJ, D)),
                  _tile(D, tm), _tile(1, tm), _tile(D, tm), _acc((8, D))],
        out_specs=(_tile(D, tm), _acc((8, D))),
        out_shape=(_S((L, D), F32), _S((8, D), F32)),
        compiler_params=_params(("arbitrary",)))(dz, dxbc, dq, dkv, ddt, w_t, xhat0, rstd0, dh0p, vecs)


_WEIGHTS = ['ln_in_g', 'ln_in_b', 'ada_w', 'ada_b', 'w_in', 'conv_w', 'conv_b', 'dt_bias', 'a_log', 'd_skip', 'ssd_norm_w',
            'attn_sinks', 'w_out', 'ln1_g', 'ln1_b', 'w_ff1', 'b_ff1', 'w_ff2', 'b_ff2', 'ln2_g', 'ln2_b']
_BIG = ('w_in', 'w_out', 'w_ff1', 'w_ff2')
_SMALL = ('ada_b', 'ln_in_g', 'ln_in_b', 'conv_b', 'dt_bias', 'a_log', 'd_skip', 'ssd_norm_w', 'attn_sinks', 'ln1_g', 'ln1_b',
          'b_ff1', 'b_ff2', 'ln2_g', 'ln2_b')


def _pad_lanes(v, n=None):
    v = v.reshape(1, -1)
    n = n or -(-v.shape[1] // LANE) * LANE
    return jnp.pad(v, ((0, 0), (0, n - v.shape[1])))


def _vec8(rows, D):
    rows = [r.reshape(1, D) for r in rows]
    return jnp.concatenate(rows + [jnp.zeros((8 - len(rows), D), F32)], axis=0)


def kernel(x, c, ln_in_g, ln_in_b, ada_w, ada_b, w_in, conv_w, conv_b, dt_bias, a_log, d_skip, ssd_norm_w, attn_sinks, w_out, ln1_g, ln1_b, w_ff1, b_ff1, w_ff2, b_ff2, ln2_g, ln2_b, loss_target, m_ln_in_g, m_ln_in_b, m_ada_w, m_ada_b, m_w_in, m_conv_w, m_conv_b, m_dt_bias, m_a_log, m_d_skip, m_ssd_norm_w, m_attn_sinks, m_w_out, m_ln1_g, m_ln1_b, m_w_ff1, m_b_ff1, m_w_ff2, m_b_ff2, m_ln2_g, m_ln2_b, v_ln_in_g, v_ln_in_b, v_ada_w, v_ada_b, v_w_in, v_conv_w, v_conv_b, v_dt_bias, v_a_log, v_d_skip, v_ssd_norm_w, v_attn_sinks, v_w_out, v_ln1_g, v_ln1_b, v_w_ff1, v_b_ff1, v_w_ff2, v_b_ff2, v_ln2_g, v_ln2_b):
    wts = dict(ln_in_g=ln_in_g, ln_in_b=ln_in_b, ada_w=ada_w, ada_b=ada_b, w_in=w_in, conv_w=conv_w, conv_b=conv_b,
               dt_bias=dt_bias, a_log=a_log, d_skip=d_skip, ssd_norm_w=ssd_norm_w, attn_sinks=attn_sinks, w_out=w_out,
               ln1_g=ln1_g, ln1_b=ln1_b, w_ff1=w_ff1, b_ff1=b_ff1, w_ff2=w_ff2, b_ff2=b_ff2, ln2_g=ln2_g, ln2_b=ln2_b)
    ms = dict(ln_in_g=m_ln_in_g, ln_in_b=m_ln_in_b, ada_w=m_ada_w, ada_b=m_ada_b, w_in=m_w_in, conv_w=m_conv_w,
              conv_b=m_conv_b, dt_bias=m_dt_bias, a_log=m_a_log, d_skip=m_d_skip, ssd_norm_w=m_ssd_norm_w,
              attn_sinks=m_attn_sinks, w_out=m_w_out, ln1_g=m_ln1_g, ln1_b=m_ln1_b, w_ff1=m_w_ff1, b_ff1=m_b_ff1,
              w_ff2=m_w_ff2, b_ff2=m_b_ff2, ln2_g=m_ln2_g, ln2_b=m_ln2_b)
    vs = dict(ln_in_g=v_ln_in_g, ln_in_b=v_ln_in_b, ada_w=v_ada_w, ada_b=v_ada_b, w_in=v_w_in, conv_w=v_conv_w,
              conv_b=v_conv_b, dt_bias=v_dt_bias, a_log=v_a_log, d_skip=v_d_skip, ssd_norm_w=v_ssd_norm_w,
              attn_sinks=v_attn_sinks, w_out=v_w_out, ln1_g=v_ln1_g, ln1_b=v_ln1_b, w_ff1=v_w_ff1, b_ff1=v_b_ff1,
              w_ff2=v_w_ff2, b_ff2=v_b_ff2, ln2_g=v_ln2_g, ln2_b=v_ln2_b)

    L, D = x.shape[1], x.shape[2]
    depth = w_in.shape[0]
    assert depth == 1 and x.shape[0] == 1 and L % CHUNK == 0
    W = D
    H = W // HEAD_DIM
    CD = W + 2 * SSD_GROUPS * SSD_STATE
    AW = D
    AH = AW // HEAD_DIM
    KV = AH // GQA
    KVW2 = 2 * KV * HEAD_DIM
    PROJ = W + CD + H + AW + KVW2
    FF = w_ff1.shape[2] * N_DEV
    MIX = w_out.shape[1] * N_DEV
    assert w_in.shape[2] * N_DEV == PROJ and MIX == W + AW and H <= LANE and AH <= LANE
    dims = dict(W=W, H=H, CD=CD, AW=AW, AH=AH, KV=KV, KVW2=KVW2)
    alpha = (2.0 * depth) ** 0.25
    C6 = ada_w.shape[2]
    CW = conv_w.shape[2]

    ax, ay, ac = _my_pos()
    me = 4 * ax + 2 * ay + ac
    x2 = x.reshape(L, D)
    tgt = loss_target.reshape(L, D)
    r1 = lambda a: a.reshape(1, -1)

    ada_b_cols = lax.dynamic_slice(ada_b, (0, me * C6), (1, C6))
    cs_all, mod = _mod_fwd(c, ada_w[0], ada_b_cols)
    sh1, sc1, g1, sh2, sc2, g2 = [r1(t) for t in jnp.split(mod.reshape(-1), 6)]

    wg_in, cwg = _ag_weights([w_in[0].T.astype(BF16), conv_w[0]], cs_all)
    shards2 = [w_out[0].astype(BF16), w_ff1[0].astype(BF16), w_ff2[0].astype(BF16)]
    lands2 = [lax.dynamic_update_slice(lax.empty((N_DEV,) + s.shape, s.dtype), s[None], (me, 0, 0)) for s in shards2]
    ag_ss, ag_rs, ag_arr, ag_token = _split_start(shards2 + lands2, _plan_gather(3), cwg, "ag_ici_start")
    sh1 = sh1 + ag_token[0:1, 0:1]
    w_pad = _merge_blocks(wg_in)
    cw_full = cwg.transpose(1, 0, 2).reshape(CONV_K, CD)

    dtb = _pad_lanes(dt_bias, LANE)
    alog = _pad_lanes(a_log, LANE)
    dsk = jnp.repeat(d_skip.reshape(-1), HEAD_DIM).reshape(1, W)
    sinks = attn_sinks.reshape(-1)
    g_in, b_in = r1(ln_in_g), r1(ln_in_b)

    xhat0, rstd0, u1, z, xbc, q, kv, dt_raw = _ln_in_proj(x2, g_in, b_in, sc1, sh1, w_pad, dims)
    y, yn, sprev, pre = _conv_ssd(xbc, dt_raw, z, cw_full, conv_b, dtb, alog, dsk, ssd_norm_w, dims)
    ag_arr = _split_wait(ag_ss, ag_rs, ag_arr, _plan_gather(3), yn, "ag_ici_wait")
    fw_ss, fw_rs, ag_land, fw_token = _split_start(ag_arr[3:], _plan_forward(3), yn, "ag_fwd_start")
    o = _swa_fwd(q, kv, sinks + fw_token[0, 0], dims)
    wg_out, wg_ff1, wg_ff2 = _split_wait(fw_ss, fw_rs, ag_land, _plan_forward(3), o, "ag_fwd_wait")
    w_out_full = wg_out.reshape(MIX, D)
    w1_full = wg_ff1.transpose(1, 0, 2).reshape(D, FF)
    w2_full = wg_ff2.reshape(FF, D)
    mix, xhat1, rstd1, u2 = _out_proj_ln1(yn, o, w_out_full, xhat0, _vec8([g_in, b_in, g1, ln1_g, ln1_b, sc2, sh2], D), alpha)
    rr, dr2, acc_f, loss_loc = _mlp_loss(u2, w1_full, w2_full, xhat1, tgt,
                                         _vec8([ln1_g, ln1_b, g2, ln2_g, ln2_b, b_ff2], D), b_ff1, alpha)

    df, da, gb2, gb1 = _mlp_bwd_a(dr2, rr, w2_full, g2)
    gw_ff2 = _matmul_tn(rr, df, "gw_ff2", square_a=True)
    gw_ff1t = _matmul_tn(da, u2, "gw_ff1")
    dmix, dh0p, dyn, do, acc_b = _mlp_bwd_b(da, w1_full, dr2, xhat1, rstd1, mix, w_out_full,
                                            _vec8([ln1_g, ln1_b, sc2, g1], D), alpha, W)
    gw_out = jnp.concatenate([_matmul_tn(yn, dmix, "gw_out_ssd"), _matmul_tn(o, dmix, "gw_out_attn")], axis=0)

    core = jnp.reshape(ac, (1,)).astype(jnp.int32)
    blocked1 = [gw_out.reshape(N_DEV, MIX // N_DEV, D), gw_ff1t.reshape(N_DEV, FF // N_DEV, D),
                gw_ff2.reshape(N_DEV, FF // N_DEV, D)]
    lands1 = [lax.empty(b.shape, b.dtype) for b in blocked1]
    rs_ss, rs_rs, rs_arr, rs_token = _split_start(blocked1 + lands1, _plan_scatter_all(3), do, "rs_all_start")
    dq, dkv, dsink = _swa_bwd(q, kv, do, sinks + rs_token[0, 0], dims)
    dz, dpre, ddt, acc_s, hacc = _ssd_bwd(dyn, y, z, pre, dt_raw, sprev, dtb + rs_token[0:1, 0:1], alog, dsk, ssd_norm_w, dims)
    dxbc, acc_c = _conv_bwd(dpre, xbc, cw_full)
    gw_in = _gw_in((dz, dxbc, ddt, dq, dkv), u1, dims)

    blocked2 = [_split_blocks(gw_in, N_DEV)]
    pairs2 = [_pair_sum(b, r, core) for b, r in zip(blocked2, _rs_d2d(blocked2, "rs_d2d_2"))]
    lands2 = [lax.empty(p.shape, p.dtype) for p in pairs2]
    r2_ss, r2_rs, r2_arr, r2_token = _split_start(pairs2 + lands2, _plan_scatter(1), gw_in, "rs_ici_start_2")
    grad_x, acc_i = _in_proj_bwd(dz, dxbc, dq, dkv, ddt, w_pad, xhat0, rstd0, dh0p,
                                 _vec8([g_in, b_in, sc1], D) + r2_token[0:1, 0:1], dims)

    srcs = [acc_i, acc_b, acc_f, acc_s, acc_c, hacc, dsink, gb1, gb2, loss_loc]
    I_, B_, F_, S_, C_, H_, K_, G1_, G2_, L_ = range(10)
    seg_of = dict(ada_b=[(I_, 1, D), (I_, 0, D), (B_, 4, D), (B_, 1, D), (B_, 0, D), (F_, 2, D)],
                  ln_in_g=[(I_, 2, D)], ln_in_b=[(I_, 3, D)], conv_b=[(C_, CONV_K, CD)], dt_bias=[(H_, 0, H)],
                  a_log=[(H_, 1, H)], d_skip=[(H_, 2, H)], ssd_norm_w=[(S_, 0, W)], attn_sinks=[(K_, 0, AH)],
                  ln1_g=[(B_, 2, D)], ln1_b=[(B_, 3, D)], b_ff1=[(G1_, 0, FF)], b_ff2=[(G2_, 0, D)],
                  ln2_g=[(F_, 0, D)], ln2_b=[(F_, 1, D)])
    pieces = [seg_of[n] for n in _SMALL] + [[(C_, t, CD) for t in range(CONV_K)], [(L_, 0, 1)]]
    params = [tuple(t[n].reshape(1, -1) for t in (wts, ms, vs)) for n in _SMALL]
    res = _small_sync_adamw(srcs, pieces, params, 6 * D)
    grads, deltas, new_m, new_v = {}, {}, {}, {}
    for k, n in enumerate(_SMALL):
        grads[n], deltas[n], new_m[n], new_v[n] = (t.reshape(wts[n].shape) for t in res[4 * k:4 * k + 4])
    gcw_full, dmod_all, loss_row = res[4 * len(_SMALL):]
    loss = loss_row[0, 0]

    g_ = lax.dynamic_slice(gcw_full, (0, me * CW), (CONV_K, CW))
    d_, m_, v_ = _adamw(conv_w[0], g_, m_conv_w[0], v_conv_w[0])
    grads['conv_w'], deltas['conv_w'], new_m['conv_w'], new_v['conv_w'] = (t[None] for t in (g_, d_, m_, v_))

    dmod_cols = lax.dynamic_slice(dmod_all, (0, me * C6), (N_DEV, C6))
    pad16 = lambda t: jnp.concatenate([t, jnp.zeros((16 - N_DEV,) + t.shape[1:], t.dtype)], axis=0)
    g_, d_, m_, v_ = _ada_grad_adamw(pad16(cs_all), pad16(dmod_cols), ada_w[0], m_ada_w[0], v_ada_w[0])
    grads['ada_w'], deltas['ada_w'], new_m['ada_w'], new_v['ada_w'] = (t[None] for t in (g_, d_, m_, v_))

    rs_arr = _split_wait(rs_ss, rs_rs, rs_arr, _plan_scatter_all(3), g_, "rs_all_wait")
    mychip = 2 * ax + ay
    chips = jnp.stack([(mychip + k) % N_CHIP for k in range(N_CHIP)]).astype(jnp.int32)
    devs = jnp.stack([(me + k) % N_DEV for k in range(N_DEV)]).astype(jnp.int32)
    for n, own, land in zip(('w_out', 'w_ff1', 'w_ff2'), rs_arr[:3], rs_arr[3:]):
        g_, d_, m_, v_ = _sum_adamw_split(own, land, devs, wts[n][0], ms[n][0], vs[n][0], transposed=(n == 'w_ff1'))
        grads[n], deltas[n], new_m[n], new_v[n] = (t[None] for t in (g_, d_, m_, v_))
    r2_arr = _split_wait(r2_ss, r2_rs, r2_arr, _plan_scatter(1), g_, "rs_ici_wait_2")
    g_, d_, m_, v_ = _sum_adamw_split(r2_arr[0], r2_arr[1], chips, wts['w_in'][0].T, ms['w_in'][0].T, vs['w_in'][0].T)
    grads['w_in'], deltas['w_in'], new_m['w_in'], new_v['w_in'] = (t.T[None] for t in (g_, d_, m_, v_))

    return (loss, grad_x.reshape(x.shape), *[grads[n] for n in _WEIGHTS], *[deltas[n] for n in _WEIGHTS],
            *[new_m[n] for n in _WEIGHTS], *[new_v[n] for n in _WEIGHTS])
```

```python
import functools
import math

import numpy as np
import jax
import jax.numpy as jnp
from jax import lax
from jax.experimental import pallas as pl
from jax.experimental.pallas import tpu as pltpu

F32 = jnp.float32
BF16 = jnp.bfloat16
MESH = pl.DeviceIdType.MESH

N_DEV = 8
N_CHIP = 4
HEAD_DIM = 64
SSD_GROUPS = 2
SSD_STATE = 128
CHUNK = 128
CONV_K = 4
GQA = 8
LANE = 128
HALO = 8
LN_EPS = 1e-5
RMS_EPS = 1e-5
NEG = -1e30
ADAM_LR, ADAM_B1, ADAM_B2, ADAM_EPS, ADAM_WD, ADAM_STEP = 0.001, 0.9, 0.999, 1e-08, 0.01, 10
V7X_VMEM_BYTES = 64 * 1024 * 1024
VMEM_LIMIT = V7X_VMEM_BYTES - 8 * 1024 * 1024
MLP_TM = 512
MLP_SUB = 512
MLP_FC = 512
CONV_TM = 512
CONV_CB = 2048
ROW_CHUNK = 32


def _alibi_slopes(n):
    def pow2(m):
        start = 2.0 ** (-8.0 / m)
        return [start ** (i + 1) for i in range(m)]
    if math.log2(n).is_integer():
        s = pow2(n)
    else:
        c = 2 ** math.floor(math.log2(n))
        s = pow2(c) + pow2(2 * c)[0::2][: n - c]
    return [float(v) for v in np.array(s, dtype=np.float32)]


def _dot(a, b):
    return jnp.dot(a, b, preferred_element_type=F32)


def _dot_nt(a, b):
    return lax.dot_general(a, b, (((1,), (1,)), ((), ())), preferred_element_type=F32)


def _dot_tn(a, b):
    return lax.dot_general(a, b, (((0,), (0,)), ((), ())), preferred_element_type=F32)


def _split3(v):
    hi = v.astype(BF16)
    r1 = v - hi.astype(F32)
    mid = r1.astype(BF16)
    return hi, mid, (r1 - mid.astype(F32)).astype(BF16)


def _dot_hi(a, b, zero_one_rhs=False):
    if zero_one_rhs:
        bb = b.astype(BF16)
        return sum(_dot(t, bb) for t in _split3(a))
    ab = a.astype(BF16)
    return sum(_dot(ab, t) for t in _split3(b))


def _sigmoid(x):
    return 0.5 * jnp.tanh(0.5 * x) + 0.5


def _softplus(x):
    return jnp.maximum(x, 0.0) + jnp.log(1.0 + jnp.exp(-jnp.abs(x)))


def _mean(x):
    return jnp.mean(x, axis=-1, keepdims=True)


def _ln_fwd(x):
    xc = x - _mean(x)
    rstd = lax.rsqrt(_mean(xc * xc) + LN_EPS)
    return xc * rstd, rstd


def _ln_bwd(dxhat, xhat, rstd):
    return rstd * (dxhat - _mean(dxhat) - xhat * _mean(dxhat * xhat))


def _colsum(x):
    return jnp.sum(x, axis=0, keepdims=True)


def _params(sem):
    return pltpu.CompilerParams(dimension_semantics=sem, vmem_limit_bytes=VMEM_LIMIT)


def _tile(i_map_cols, tm):
    return pl.BlockSpec((tm, i_map_cols), lambda i: (i, 0))


def _res(shape):
    return pl.BlockSpec(shape, lambda *_: (0,) * len(shape), pipeline_mode=pl.Buffered(1))


def _acc(shape):
    return pl.BlockSpec(shape, lambda *_: (0,) * len(shape))


def _S(shape, dtype):
    return jax.ShapeDtypeStruct(shape, dtype)


def _my_pos():
    return lax.axis_index("x"), lax.axis_index("y"), lax.axis_index("c")


def _peer(pos, k):
    x, y, c = pos
    px = 1 - x if k & 4 else x
    py = 1 - y if k & 2 else y
    pc = 1 - c if k & 1 else c
    return (px, py, pc)


def _lin(p):
    return 4 * p[0] + 2 * p[1] + p[2]


def _mod_fwd(c_loc, ada_w_loc, ada_b_cols):
    D = c_loc.shape[1]
    C6 = ada_w_loc.shape[1]

    def body(c_ref, w_ref, b_ref, cs_ref, mod_ref, call_ref, modp_ref, ssem, rsem):
        pos = _my_pos()
        me = _lin(pos)
        call_ref[me] = c_ref[...]
        sends = []
        for k in range(1, N_DEV):
            cp = pltpu.make_async_remote_copy(src_ref=c_ref, dst_ref=call_ref.at[me], send_sem=ssem.at[k - 1],
                                              recv_sem=rsem.at[k - 1], device_id=_peer(pos, k), device_id_type=MESH)
            cp.start()
            sends.append(cp)
        for k in range(1, N_DEV):
            src = _lin(_peer(pos, k))
            pltpu.make_async_remote_copy(src_ref=c_ref, dst_ref=call_ref.at[src], send_sem=ssem.at[k - 1],
                                         recv_sem=rsem.at[k - 1], device_id=pos, device_id_type=MESH).wait_recv()
        for cp in sends:
            cp.wait_send()
        call = jnp.concatenate([call_ref[b] for b in range(N_DEV)], axis=0)
        cs = call * _sigmoid(call)
        cs_ref[...] = cs
        modp = _dot(cs.astype(BF16), w_ref[...].astype(BF16)) + b_ref[...]
        for b in range(N_DEV):
            modp_ref[b] = modp[b:b + 1, :]
        mod_ref[me] = modp_ref[me]
        sends = []
        for k in range(1, N_DEV):
            peer = _peer(pos, k)
            cp = pltpu.make_async_remote_copy(src_ref=modp_ref.at[_lin(peer)], dst_ref=mod_ref.at[me],
                                              send_sem=ssem.at[N_DEV - 2 + k], recv_sem=rsem.at[N_DEV - 2 + k],
                                              device_id=peer, device_id_type=MESH)
            cp.start()
            sends.append(cp)
        for k in range(1, N_DEV):
            src = _lin(_peer(pos, k))
            pltpu.make_async_remote_copy(src_ref=modp_ref.at[src], dst_ref=mod_ref.at[src],
                                         send_sem=ssem.at[N_DEV - 2 + k], recv_sem=rsem.at[N_DEV - 2 + k],
                                         device_id=pos, device_id_type=MESH).wait_recv()
        for cp in sends:
            cp.wait_send()

    vm = pl.BlockSpec(memory_space=pltpu.VMEM)
    return pl.pallas_call(
        body, name="mod_fwd",
        out_shape=(_S((N_DEV, D), F32), _S((N_DEV, 1, C6), F32)),
        in_specs=[vm, vm, vm], out_specs=(vm, vm),
        scratch_shapes=[pltpu.VMEM((N_DEV, 1, D), F32), pltpu.VMEM((N_DEV, 1, C6), F32),
                        pltpu.SemaphoreType.DMA((2 * (N_DEV - 1),)), pltpu.SemaphoreType.DMA((2 * (N_DEV - 1),))],
        compiler_params=pltpu.CompilerParams(vmem_limit_bytes=VMEM_LIMIT),
    )(c_loc, ada_w_loc, ada_b_cols)


def _small_sync_adamw(srcs, pieces, params, n_mod):
    n_src, n_par = len(srcs), len(params)
    rows_of = [sum(-(-w // LANE) for _, _, w in seg) for seg in pieces]
    starts = [sum(rows_of[:k]) for k in range(len(pieces))]
    NR = -(-sum(rows_of) // 8) * 8
    cd = pieces[n_par][0][2]

    def seg_row(arr, k, width):
        r = starts[k]
        if width <= LANE:
            return arr[r:r + 1, 0:width]
        return jnp.concatenate([arr[r + q:r + q + 1, :] for q in range(width // LANE)], axis=1)

    def exchange(*refs):
        src = refs[:n_src]
        total_ref, dmod_ref, pack_ref, gat_ref, ssem, rsem = refs[n_src:]
        pos = _my_pos()
        me = _lin(pos)
        pack_ref[...] = jnp.zeros_like(pack_ref)
        for k, seg in enumerate(pieces):
            r = starts[k]
            for (si, row, width) in seg:
                for q in range(-(-width // LANE)):
                    wq = min(LANE, width - q * LANE)
                    pack_ref[r:r + 1, 0:wq] = src[si][row:row + 1, q * LANE:q * LANE + wq]
                    r += 1
        gat_ref[me] = pack_ref[...]
        sends = []
        for k in range(1, N_DEV):
            cp = pltpu.make_async_remote_copy(src_ref=pack_ref, dst_ref=gat_ref.at[me], send_sem=ssem.at[k - 1],
                                              recv_sem=rsem.at[k - 1], device_id=_peer(pos, k), device_id_type=MESH)
            cp.start()
            sends.append(cp)
        for k in range(1, N_DEV):
            frm = _lin(_peer(pos, k))
            pltpu.make_async_remote_copy(src_ref=pack_ref, dst_ref=gat_ref.at[frm], send_sem=ssem.at[k - 1],
                                         recv_sem=rsem.at[k - 1], device_id=pos, device_id_type=MESH).wait_recv()
        for cp in sends:
            cp.wait_send()
        total = gat_ref[0]
        for j in range(1, N_DEV):
            total = total + gat_ref[j]
        total_ref[...] = total
        for j in range(N_DEV):
            dmod_ref[j:j + 1, :] = seg_row(gat_ref[j], 0, n_mod)

    def update(*refs):
        total = refs[0][...]
        wmv = refs[1:1 + 3 * n_par]
        outs = refs[1 + 3 * n_par:]
        for k in range(n_par):
            n = params[k][0].shape[1]
            g = seg_row(total, k, n)
            w_ref, m_ref, v_ref = wmv[3 * k:3 * k + 3]
            g_ref, d_ref, m2_ref, v2_ref = outs[4 * k:4 * k + 4]
            g_ref[...] = g
            d_ref[...], m2_ref[...], v2_ref[...] = _adamw_math(w_ref[...], g, m_ref[...], v_ref[...])
        gcw_ref, loss_ref = outs[4 * n_par:]
        for t in range(CONV_K):
            r = starts[n_par] + t * (cd // LANE)
            gcw_ref[t:t + 1, :] = jnp.concatenate([total[r + q:r + q + 1, :] for q in range(cd // LANE)], axis=1)
        loss_ref[...] = total[starts[n_par + 1]:starts[n_par + 1] + 1, :]

    vm = pl.BlockSpec(memory_space=pltpu.VMEM)
    total, dmod_all = pl.pallas_call(
        exchange, name="small_sync", out_shape=(_S((NR, LANE), F32), _S((N_DEV, n_mod), F32)),
        in_specs=[vm] * n_src, out_specs=(vm, vm),
        scratch_shapes=[pltpu.VMEM((NR, LANE), F32), pltpu.VMEM((N_DEV, NR, LANE), F32),
                        pltpu.SemaphoreType.DMA((N_DEV - 1,)), pltpu.SemaphoreType.DMA((N_DEV - 1,))],
        compiler_params=pltpu.CompilerParams(vmem_limit_bytes=VMEM_LIMIT),
    )(*srcs)
    out_shape = []
    for w, _, _ in params:
        out_shape += [_S(w.shape, F32)] * 4
    out_shape += [_S((CONV_K, cd), F32), _S((1, LANE), F32)]
    flat = [t for p in params for t in p]
    res = pl.pallas_call(
        update, name="small_adamw", out_shape=tuple(out_shape),
        in_specs=[vm] * (1 + 3 * n_par), out_specs=tuple([vm] * len(out_shape)),
        compiler_params=pltpu.CompilerParams(vmem_limit_bytes=VMEM_LIMIT),
    )(total, *flat)
    return (*res[:-1], dmod_all, res[-1])


def _ag_weights(shards, after):
    n = len(shards)

    def body(*refs):
        ins, outs = refs[:n], refs[n + 1:2 * n + 1]
        ssem, rsem, lsem = refs[2 * n + 1:]
        x, y, c = pos = _my_pos()
        me = _lin(pos)
        sib = (x, y, 1 - c)
        chips = [(1 - x, y), (x, 1 - y), (1 - x, 1 - y)]

        def copy(a, k, block, to, src=None):
            return pltpu.make_async_remote_copy(
                src_ref=outs[a].at[block] if src is None else src, dst_ref=outs[a].at[block],
                send_sem=ssem.at[a * 7 + k], recv_sem=rsem.at[a * 7 + k], device_id=to, device_id_type=MESH)

        local = [pltpu.make_async_copy(ins[a], outs[a].at[me], lsem.at[a]) for a in range(n)]
        for cp in local:
            cp.start()
        first = []
        for a in range(n):
            first.append(copy(a, 0, me, sib, src=ins[a]))
            first += [copy(a, 1 + j, me, (*chip, c), src=ins[a]) for j, chip in enumerate(chips)]
        for cp in first:
            cp.start()
        passed = []
        for a in range(n):
            for j, chip in enumerate(chips):
                blk = _lin((*chip, c))
                copy(a, 1 + j, blk, pos).wait_recv()
                cp = copy(a, 4 + j, blk, sib)
                cp.start()
                passed.append(cp)
        for a in range(n):
            copy(a, 0, _lin(sib), pos).wait_recv()
            for j, chip in enumerate(chips):
                copy(a, 4 + j, _lin((*chip, 1 - c)), pos).wait_recv()
        for cp in first + passed:
            cp.wait_send()
        for cp in local:
            cp.wait()

    hbm = pl.BlockSpec(memory_space=pl.ANY)
    return pl.pallas_call(
        body, name="ag_weights",
        out_shape=tuple(_S((N_DEV,) + s.shape, s.dtype) for s in shards),
        in_specs=[hbm] * (n + 1), out_specs=tuple([hbm] * n),
        scratch_shapes=[pltpu.SemaphoreType.DMA((7 * n,)), pltpu.SemaphoreType.DMA((7 * n,)),
                        pltpu.SemaphoreType.DMA((n,))],
    )(*shards, after)


def _rs_d2d(blocked, name):
    n = len(blocked)

    def body(*refs):
        ins, outs = refs[:n], refs[n:2 * n]
        ssem, rsem = refs[2 * n:]
        x, y, c = pos = _my_pos()
        sib = (x, y, 1 - c)
        cps = []
        for a in range(n):
            for j in range(N_CHIP):
                cp = pltpu.make_async_remote_copy(
                    src_ref=ins[a].at[2 * j + (1 - c)], dst_ref=outs[a].at[j], send_sem=ssem.at[a * N_CHIP + j],
                    recv_sem=rsem.at[a * N_CHIP + j], device_id=sib, device_id_type=MESH)
                cp.start()
                cps.append(cp)
        for cp in cps:
            cp.wait_recv()
        for cp in cps:
            cp.wait_send()

    hbm = pl.BlockSpec(memory_space=pl.ANY)
    return pl.pallas_call(
        body, name=name,
        out_shape=tuple(_S((N_CHIP,) + b.shape[1:], b.dtype) for b in blocked),
        in_specs=[hbm] * n, out_specs=tuple([hbm] * n),
        scratch_shapes=[pltpu.SemaphoreType.DMA((N_CHIP * n,)), pltpu.SemaphoreType.DMA((N_CHIP * n,))],
    )(*blocked)


_HBM = pl.BlockSpec(memory_space=pltpu.HBM)
_SEM = pl.BlockSpec(memory_space=pltpu.SEMAPHORE)
_ANY = pl.BlockSpec(memory_space=pl.ANY)
_EFFECT = pltpu.SideEffectType.DATAFLOW_SIDE_EFFECTING


def _in_hbm(a):
    return pltpu.with_memory_space_constraint(a, pltpu.HBM)


def _plan_gather(n):
    def copies(pos):
        x, y, c = pos
        out = []
        for a in range(n):
            for dev in [(x, y, 1 - c)] + [(*_peer(pos, 2 * k)[:2], c) for k in range(1, N_CHIP)]:
                out.append((a, None, n + a, _lin(pos), dev, _lin(dev)))
        return out
    return copies


def _plan_forward(n):
    def copies(pos):
        x, y, c = pos
        out = []
        for a in range(n):
            for k in range(1, N_CHIP):
                tx, ty, _ = _peer(pos, 2 * k)
                out.append((a, _lin((tx, ty, c)), a, _lin((tx, ty, c)), (x, y, 1 - c), _lin((tx, ty, 1 - c))))
        return out
    return copies


def _plan_scatter_all(n):
    def copies(pos):
        out = []
        for a in range(n):
            for k in range(1, N_DEV):
                dev = _peer(pos, k)
                out.append((a, _lin(dev), n + a, _lin(pos), dev, _lin(dev)))
        return out
    return copies


def _plan_scatter(n):
    def copies(pos):
        x, y, c = pos
        out = []
        for a in range(n):
            for k in range(1, N_CHIP):
                tx, ty, _ = _peer(pos, 2 * k)
                out.append((a, 2 * tx + ty, n + a, 2 * x + y, (tx, ty, c), 2 * tx + ty))
        return out
    return copies


def _split_copy(refs, cp, ssem, rsem, i, arrival):
    si, s_slot, di, d_slot, dev, a_slot = cp
    return pltpu.make_async_remote_copy(
        src_ref=refs[si] if s_slot is None else refs[si].at[s_slot], dst_ref=refs[di].at[a_slot if arrival else d_slot],
        send_sem=ssem.at[i], recv_sem=rsem.at[i], device_id=dev, device_id_type=MESH)


def _split_start(arrays, copies, after, name):
    n = len(arrays)
    n_cp = len(copies((0, 0, 0)))

    def body(*refs):
        ssem, rsem, token = refs[n + 1], refs[n + 2], refs[-1]
        for i, cp in enumerate(copies(_my_pos())):
            _split_copy(refs, cp, ssem, rsem, i, False).start()
        token[...] = jnp.zeros_like(token)

    res = pl.pallas_call(
        body, name=name,
        out_shape=(pltpu.SemaphoreType.DMA((n_cp,)), pltpu.SemaphoreType.DMA((n_cp,)),
                   *[pltpu.HBM(a.shape, a.dtype) for a in arrays], _S((8, LANE), F32)),
        in_specs=[_HBM] * n + [_ANY],
        out_specs=(_SEM, _SEM, *[_HBM] * n, pl.BlockSpec(memory_space=pltpu.VMEM)),
        input_output_aliases={a: 2 + a for a in range(n)},
        compiler_params=pltpu.CompilerParams(has_side_effects=_EFFECT),
    )(*[_in_hbm(a) for a in arrays], after)
    return res[0], res[1], list(res[2:2 + n]), res[-1]


def _split_wait(ssem, rsem, arrays, copies, after, name):
    n = len(arrays)

    def body(*refs):
        for i, cp in enumerate(copies(_my_pos())):
            d = _split_copy(refs, cp, refs[n], refs[n + 1], i, True)
            d.wait_send()
            d.wait_recv()

    res = pl.pallas_call(
        body, name=name,
        out_shape=tuple(pltpu.HBM(a.shape, a.dtype) for a in arrays),
        in_specs=[_HBM] * n + [_SEM, _SEM, _ANY], out_specs=tuple([_HBM] * n),
        input_output_aliases={a: a for a in range(n)},
        compiler_params=pltpu.CompilerParams(has_side_effects=_EFFECT),
    )(*arrays, ssem, rsem, after)
    return list(res)


def _row_tile(R, itemsize_rows=16, cap=256):
    t = cap - cap % itemsize_rows
    while t >= itemsize_rows:
        if R % t == 0:
            return t
        t -= itemsize_rows
    return R


def _pair_sum(blocked, recv, core):
    _, R, C = blocked.shape
    tr = _row_tile(R)

    def body(ids_ref, a_ref, b_ref, o_ref):
        del ids_ref
        o_ref[...] = (a_ref[...].astype(F32) + b_ref[...].astype(F32)).astype(BF16)

    gs = pltpu.PrefetchScalarGridSpec(
        num_scalar_prefetch=1, grid=(N_CHIP, R // tr),
        in_specs=[pl.BlockSpec((1, tr, C), lambda j, r, ids: (2 * j + ids[0], r, 0)),
                  pl.BlockSpec((1, tr, C), lambda j, r, ids: (j, r, 0))],
        out_specs=pl.BlockSpec((1, tr, C), lambda j, r, ids: (j, r, 0)))
    return pl.pallas_call(body, name="pair_sum", grid_spec=gs, out_shape=_S((N_CHIP, R, C), BF16),
                          compiler_params=_params(("arbitrary", "arbitrary")))(core, blocked, recv)


def _adamw_math(w, g, m, v):
    m2 = ADAM_B1 * m + (1.0 - ADAM_B1) * g
    v2 = ADAM_B2 * v + (1.0 - ADAM_B2) * (g * g)
    m_hat = m2 / (1.0 - ADAM_B1 ** ADAM_STEP)
    v_hat = v2 / (1.0 - ADAM_B2 ** ADAM_STEP)
    delta = -ADAM_LR * (m_hat / (jnp.sqrt(v_hat) + ADAM_EPS) + ADAM_WD * w)
    return delta, m2, v2


def _sum_adamw_split(pairs, land, chips, w, m, v, transposed=False):
    R, C = w.shape
    n_slots = chips.shape[0]
    tr = _row_tile(R, 128, 256) if transposed else R
    tc = C if transposed else _pick(C, (256, 128))

    def body(ids_ref, *refs):
        del ids_ref
        parts, (w_ref, m_ref, v_ref, g_ref, d_ref, m2_ref, v2_ref) = refs[:n_slots], refs[n_slots:]
        g = parts[0][0].astype(F32)
        for p_ref in parts[1:]:
            g = g + p_ref[0].astype(F32)
        if transposed:
            g = g.T
        g_ref[...] = g
        d_ref[...], m2_ref[...], v2_ref[...] = _adamw_math(w_ref[...], g, m_ref[...], v_ref[...])

    if transposed:
        t = pl.BlockSpec((tr, C), lambda r, ids: (r, 0))
        slot = lambda k: pl.BlockSpec((1, C, tr), lambda r, ids: (ids[k], 0, r))
    else:
        t = pl.BlockSpec((R, tc), lambda c, ids: (0, c))
        slot = lambda k: pl.BlockSpec((1, R, tc), lambda c, ids: (ids[k], 0, c))
    gs = pltpu.PrefetchScalarGridSpec(num_scalar_prefetch=1, grid=((R // tr) * (C // tc),),
                                      in_specs=[slot(k) for k in range(n_slots)] + [t, t, t], out_specs=(t, t, t, t))
    return pl.pallas_call(body, name="sum_adamw_split", grid_spec=gs, out_shape=tuple(_S((R, C), F32) for _ in range(4)),
                          compiler_params=_params(("arbitrary",)))(chips, pairs, *[land] * (n_slots - 1), w, m, v)


def _adamw(w, g, m, v):
    R, C = w.shape
    tr = _row_tile(R, 8)

    def body(w_ref, g_ref, m_ref, v_ref, d_ref, m2_ref, v2_ref):
        d_ref[...], m2_ref[...], v2_ref[...] = _adamw_math(w_ref[...], g_ref[...], m_ref[...], v_ref[...])

    t = pl.BlockSpec((tr, C), lambda r: (r, 0))
    return pl.pallas_call(body, name="adamw", grid=(R // tr,), in_specs=[t, t, t, t], out_specs=(t, t, t),
                          out_shape=tuple(_S((R, C), F32) for _ in range(3)),
                          compiler_params=_params(("arbitrary",)))(w, g, m, v)


def _ada_grad_adamw(cs16, dmod16, w, m, v):
    D, C6 = w.shape
    tr = _row_tile(D, 8, 256)

    def body(cs_ref, dm_ref, w_ref, m_ref, v_ref, g_ref, d_ref, m2_ref, v2_ref):
        g = _dot_tn(cs_ref[...].astype(BF16), dm_ref[...].astype(BF16))
        g_ref[...] = g
        d_ref[...], m2_ref[...], v2_ref[...] = _adamw_math(w_ref[...], g, m_ref[...], v_ref[...])

    t = pl.BlockSpec((tr, C6), lambda r: (r, 0))
    return pl.pallas_call(
        body, name="ada_grad_adamw", grid=(D // tr,),
        in_specs=[pl.BlockSpec((16, tr), lambda r: (0, r)), _acc((16, C6)), t, t, t], out_specs=(t, t, t, t),
        out_shape=tuple(_S((D, C6), F32) for _ in range(4)), compiler_params=_params(("arbitrary",)))(cs16, dmod16, w, m, v)


def _pick(n, cands):
    for c in cands:
        if n % c == 0:
            return c
    return n


def _matmul_tn(a, b, name, square_a=False):
    L, K = a.shape
    N = b.shape[1]
    bk = _pick(K, (1024, 512, 256, 128))
    bn = _pick(N, (1024, 768, 512, 256, 128))
    tl = _pick(L, (1024, 512, 256, 128))
    n_l = L // tl

    def body(a_ref, b_ref, o_ref, acc_ref):
        l = pl.program_id(2)

        @pl.when(l == 0)
        def _():
            acc_ref[...] = jnp.zeros_like(acc_ref)
        av = a_ref[...]
        if square_a:
            av = av.astype(F32)
            av = av * av
        acc_ref[...] += _dot_tn(av.astype(BF16), b_ref[...].astype(BF16))

        @pl.when(l == n_l - 1)
        def _():
            o_ref[...] = acc_ref[...].astype(BF16)

    return pl.pallas_call(
        body, name=name, grid=(K // bk, N // bn, n_l),
        in_specs=[pl.BlockSpec((tl, bk), lambda k, n, l: (l, k)), pl.BlockSpec((tl, bn), lambda k, n, l: (l, n))],
        out_specs=pl.BlockSpec((bk, bn), lambda k, n, l: (k, n)), out_shape=_S((K, N), BF16),
        scratch_shapes=[pltpu.VMEM((bk, bn), F32)],
        compiler_params=_params(("arbitrary", "arbitrary", "arbitrary")))(a, b)


def _merge_blocks(a):
    n, R, C = a.shape
    cb = _pick(C, (256, 128))

    def body(i_ref, o_ref):
        for j in range(n):
            o_ref[R * j:R * (j + 1), :] = i_ref[j]

    return pl.pallas_call(body, name="merge_blocks", out_shape=_S((n * R, C), a.dtype), grid=(C // cb,),
                          in_specs=[pl.BlockSpec((n, R, cb), lambda c: (0, 0, c))],
                          out_specs=pl.BlockSpec((n * R, cb), lambda c: (0, c)),
                          compiler_params=_params(("arbitrary",)))(a)


def _split_blocks(a, n):
    NR, C = a.shape
    R = NR // n
    cb = _pick(C, (256, 128))

    def body(i_ref, o_ref):
        for j in range(n):
            o_ref[j] = i_ref[R * j:R * (j + 1), :].astype(BF16)

    return pl.pallas_call(body, name="split_blocks", out_shape=_S((n, R, C), BF16), grid=(C // cb,),
                          in_specs=[pl.BlockSpec((NR, cb), lambda c: (0, c))],
                          out_specs=pl.BlockSpec((n, R, cb), lambda c: (0, 0, c)),
                          compiler_params=_params(("arbitrary",)))(a)


def _gw_in(pieces, u1, dims):
    L, D = u1.shape
    H = dims["H"]
    r_z, r_xbc, r_dt, r_q, r_kv = _proj_rows(dims)
    PROJ = r_kv[1]
    tl = _pick(L, (512, 256, 128))
    n_l = L // tl

    def body(dz_ref, dxbc_ref, ddt_ref, dq_ref, dkv_ref, u_ref, o_ref):
        @pl.when(pl.program_id(0) == 0)
        def _():
            o_ref[...] = jnp.zeros_like(o_ref)
        u = u_ref[...]
        for ref, (r0, r1) in ((dz_ref, r_z), (dxbc_ref, r_xbc), (dq_ref, r_q), (dkv_ref, r_kv)):
            o_ref[r0:r1, :] += _dot_tn(ref[...].astype(BF16), u)
        o_ref[r_dt[0]:r_dt[0] + H, :] += _dot_tn(ddt_ref[...].astype(BF16), u)[0:H, :]

    return pl.pallas_call(
        body, name="gw_in", grid=(n_l,),
        in_specs=[_tile(p.shape[1], tl) for p in pieces] + [_tile(D, tl)],
        out_specs=_acc((PROJ, D)), out_shape=_S((PROJ, D), F32),
        compiler_params=_params(("arbitrary",)))(*pieces, u1)


def _proj_rows(dims):
    W, CD, H, AW, KVW2 = dims["W"], dims["CD"], dims["H"], dims["AW"], dims["KVW2"]
    o_dt = W + CD
    o_q = o_dt + H
    return (0, W), (W, o_dt), (o_dt, o_dt + LANE), (o_q, o_q + AW), (o_q + AW, o_q + AW + KVW2)


def _ln_in_proj(x, g, b, sc, sh, w_t, dims):
    L, D = x.shape
    W, CD, AW, KVW2 = dims["W"], dims["CD"], dims["AW"], dims["KVW2"]
    PROJ = w_t.shape[0]
    tm = _pick(L, (MLP_TM, 128))
    r_z, r_xbc, r_dt, r_q, r_kv = _proj_rows(dims)

    def body(x_ref, g_ref, b_ref, sc_ref, sh_ref, w_ref, xhat_ref, rstd_ref, u1_ref, z_ref, xbc_ref, q_ref, kv_ref, dt_ref):
        xhat, rstd = _ln_fwd(x_ref[...])
        xhat_ref[...] = xhat
        rstd_ref[...] = rstd
        h0 = xhat * g_ref[...] + b_ref[...]
        u1 = (h0 * (1.0 + sc_ref[...]) + sh_ref[...]).astype(BF16)
        u1_ref[...] = u1
        z_ref[...] = _dot_nt(u1, w_ref[r_z[0]:r_z[1], :])
        xbc_ref[...] = _dot_nt(u1, w_ref[r_xbc[0]:r_xbc[1], :])
        q_ref[...] = _dot_nt(u1, w_ref[r_q[0]:r_q[1], :]).astype(BF16)
        kv_ref[...] = _dot_nt(u1, w_ref[r_kv[0]:r_kv[1], :]).astype(BF16)
        dt_ref[...] = _dot_nt(u1, w_ref[r_dt[0]:r_dt[1], :])

    v = _acc((1, D))
    return pl.pallas_call(
        body, name="ln_in_proj", grid=(L // tm,),
        in_specs=[_tile(D, tm), v, v, v, v, _res((PROJ, D))],
        out_specs=(_tile(D, tm), _tile(1, tm), _tile(D, tm), _tile(W, tm), _tile(CD, tm), _tile(AW, tm),
                   _tile(KVW2, tm), _tile(LANE, tm)),
        out_shape=(_S((L, D), F32), _S((L, 1), F32), _S((L, D), BF16), _S((L, W), F32), _S((L, CD), F32),
                   _S((L, AW), BF16), _S((L, KVW2), BF16), _S((L, LANE), F32)),
        compiler_params=_params(("arbitrary",)))(x, g, b, sc, sh, w_t)


def _conv_act(cur_ref, prev_ref, cw_ref, cb_ref, ext_ref, first):
    T = cur_ref.shape[0]
    ext_ref[0:HALO, :] = jnp.where(first, 0.0, prev_ref[...])
    ext_ref[HALO:HALO + T, :] = cur_ref[...]
    pre = cb_ref[...] + cw_ref[0:1, :] * ext_ref[HALO - 3:HALO - 3 + T, :]
    for k in range(1, CONV_K):
        pre = pre + cw_ref[k:k + 1, :] * ext_ref[HALO - 3 + k:HALO - 3 + k + T, :]
    sig = _sigmoid(pre)
    return pre * sig, pre, sig


def _tri(T, upper=False):
    r = lax.broadcasted_iota(jnp.int32, (T, T), 0)
    c = lax.broadcasted_iota(jnp.int32, (T, T), 1)
    return (r <= c) if upper else (r >= c)


def _expand_heads_lanes(dst_ref, v, n_heads):
    for h in range(n_heads):
        dst_ref[:, h * HEAD_DIM:(h + 1) * HEAD_DIM] = jnp.broadcast_to(v[:, h:h + 1], (v.shape[0], HEAD_DIM))


def _head_onehot(n_heads):
    hd = np.arange(3 * LANE)[:, None] % LANE
    ch = np.arange(n_heads * HEAD_DIM)[None, :] // HEAD_DIM
    return jnp.asarray((hd == ch).astype(np.float32)).astype(BF16)


def _expand_heads(dst_ref, v, n_heads, onehot_ref):
    onehot = onehot_ref[...]
    v = jnp.where(lax.broadcasted_iota(jnp.int32, v.shape, 1) < n_heads, v, 0.0)
    hi = v.astype(BF16)
    r1 = v - hi.astype(F32)
    mid = r1.astype(BF16)
    lo = (r1 - mid.astype(F32)).astype(BF16)
    dst_ref[...] = _dot(jnp.concatenate([hi, mid, lo], axis=1), onehot)


def _head_reduce(v):
    wdt = v.shape[1]
    ch = lax.broadcasted_iota(jnp.int32, (wdt, LANE), 0)
    lo = lax.broadcasted_iota(jnp.int32, (wdt, LANE), 1) * HEAD_DIM
    onehot = ((ch >= lo) & (ch < lo + HEAD_DIM)).astype(BF16)
    hi = v.astype(BF16)
    rest = (v - hi.astype(F32)).astype(BF16)
    return _dot(hi, onehot) + _dot(rest, onehot)


def _conv_ssd(xbc, dt_raw, z, cw, cb, dtb, alog, dsk, nw, dims):
    L, CD = xbc.shape
    W, H, G, N = dims["W"], dims["H"], SSD_GROUPS, SSD_STATE
    T = CHUNK
    R = H // G
    GW = W // G
    nc = L // T
    HP = H * HEAD_DIM

    def body(xbc_ref, prev_ref, dt_ref, z_ref, cw_ref, cb_ref, dtb_ref, alog_ref, dsk_ref, nw_ref, oh_ref,
             y_ref, yn_ref, sp_ref, pre_ref, ext_ref, s_ref, ybuf_ref, dtx_ref, acx_ref, xb_ref):
        i = pl.program_id(0)

        @pl.when(i == 0)
        def _():
            s_ref[...] = jnp.zeros_like(s_ref)

        act, pre, _ = _conv_act(xbc_ref, prev_ref, cw_ref, cb_ref, ext_ref, i == 0)
        pre_ref[...] = pre
        xs = act[:, :W]
        dt = _softplus(dt_ref[...] + dtb_ref[...])
        a = dt * (-jnp.exp(alog_ref[...]))
        low = _tri(T)
        acum = _dot_hi(low.astype(F32), a)
        acum_t = acum.T
        _expand_heads(dtx_ref, dt, H, oh_ref)
        _expand_heads(acx_ref, acum, H, oh_ref)
        acx = acx_ref[...]
        lastx = acx[T - 1:T, :]
        xd = xs * dtx_ref[...]
        xb_ref[...] = xd.astype(BF16)
        xdb = (xd * jnp.exp(lastx - acx)).astype(BF16)
        ex = jnp.exp(acx)
        elx = jnp.exp(lastx)
        for g in range(G):
            gs = slice(g * GW, (g + 1) * GW)
            bgb = act[:, W + g * N:W + (g + 1) * N].astype(BF16)
            cgb = act[:, W + G * N + g * N:W + G * N + (g + 1) * N].astype(BF16)
            stg = s_ref[:, gs]
            sp_ref[0, :, gs] = stg
            yoff = ex[:, gs] * _dot(cgb, stg.astype(BF16))
            s_ref[:, gs] = stg * elx[:, gs] + _dot_tn(bgb, xdb[:, gs])
            cb_g = _dot_nt(cgb, bgb)
            for r in range(R):
                h = g * R + r
                hs = slice(h * HEAD_DIM, (h + 1) * HEAD_DIM)
                lm = jnp.where(low, jnp.exp(acum[:, h:h + 1] - acum_t[h:h + 1, :]), 0.0)
                ybuf_ref[:, hs] = _dot((cb_g * lm).astype(BF16), xb_ref[:, hs]) + yoff[:, r * HEAD_DIM:(r + 1) * HEAD_DIM]
        y = ybuf_ref[...] + dsk_ref[...] * xs
        y_ref[...] = y
        zz = z_ref[...]
        hh = y * (zz * _sigmoid(zz))
        for g in range(G):
            gs = slice(g * GW, (g + 1) * GW)
            hg = hh[:, gs]
            yn_ref[:, gs] = (hg * lax.rsqrt(_mean(hg * hg) + RMS_EPS) * nw_ref[:, gs]).astype(BF16)

    return pl.pallas_call(
        body, name="conv_ssd", grid=(nc,),
        in_specs=[_tile(CD, T), pl.BlockSpec((HALO, CD), lambda i: (jnp.maximum(i * (T // HALO) - 1, 0), 0)),
                  _tile(LANE, T), _tile(W, T), _acc((CONV_K, CD)), _acc((1, CD)), _acc((1, LANE)), _acc((1, LANE)),
                  _acc((1, W)), _acc((1, W)), _acc((3 * LANE, W))],
        out_specs=(_tile(W, T), _tile(W, T), pl.BlockSpec((1, N, HP), lambda i: (i, 0, 0)), _tile(CD, T)),
        out_shape=(_S((L, W), F32), _S((L, W), BF16), _S((nc, N, HP), F32), _S((L, CD), F32)),
        scratch_shapes=[pltpu.VMEM((T + HALO, CD), F32), pltpu.VMEM((N, HP), F32), pltpu.VMEM((T, W), F32),
                        pltpu.VMEM((T, W), F32), pltpu.VMEM((T, W), F32), pltpu.VMEM((T, W), BF16)],
        compiler_params=_params(("arbitrary",)))(xbc, xbc, dt_raw, z, cw, cb, dtb, alog, dsk, nw, _head_onehot(H))


def _attn_mask(T, i):
    r = lax.broadcasted_iota(jnp.int32, (T, 2 * T), 0)
    c = lax.broadcasted_iota(jnp.int32, (T, 2 * T), 1)
    dist = r + T - c
    valid = (dist >= 0) & (dist < CHUNK) & ((c >= T) | (i > 0))
    return dist.astype(F32), valid


def _attn_probs(s_raw, dist, valid, slope, sink, axis):
    s = s_raw * (HEAD_DIM ** -0.5) - slope * dist
    s = jnp.where(valid, s, NEG)
    m = jnp.maximum(jnp.max(s, axis=axis, keepdims=True), sink)
    p = jnp.exp(s - m)
    e_sink = jnp.exp(sink - m)
    inv = 1.0 / (jnp.sum(p, axis=axis, keepdims=True) + e_sink)
    return p * inv, e_sink * inv


def _kv_heads(kvc_ref, kvp_ref, g, n_kv):
    ks = slice(g * HEAD_DIM, (g + 1) * HEAD_DIM)
    vs = slice((n_kv + g) * HEAD_DIM, (n_kv + g + 1) * HEAD_DIM)
    kk = jnp.concatenate([kvp_ref[:, ks], kvc_ref[:, ks]], axis=0)
    vv = jnp.concatenate([kvp_ref[:, vs], kvc_ref[:, vs]], axis=0)
    return kk, vv


def _swa_fwd(q, kv, sinks, dims):
    L, AW = q.shape
    KV, KVW2 = dims["KV"], dims["KVW2"]
    T = CHUNK
    nb = L // T
    slopes = _alibi_slopes(dims["AH"])

    def body(q_ref, kvc_ref, kvp_ref, sink_ref, o_ref, qg_ref, p_ref):
        i = pl.program_id(0)
        dist, valid = _attn_mask(T, i)
        for g in range(KV):
            kk, vv = _kv_heads(kvc_ref, kvp_ref, g, KV)
            for r in range(GQA):
                h = g * GQA + r
                qg_ref[r * T:(r + 1) * T, :] = q_ref[:, h * HEAD_DIM:(h + 1) * HEAD_DIM]
            s_all = _dot_nt(qg_ref[...], kk)
            for r in range(GQA):
                h = g * GQA + r
                p, _ = _attn_probs(s_all[r * T:(r + 1) * T, :], dist, valid, slopes[h], sink_ref[h], -1)
                p_ref[r * T:(r + 1) * T, :] = p.astype(BF16)
            o_all = _dot(p_ref[...], vv)
            for r in range(GQA):
                h = g * GQA + r
                o_ref[:, h * HEAD_DIM:(h + 1) * HEAD_DIM] = o_all[r * T:(r + 1) * T, :].astype(BF16)

    return pl.pallas_call(
        body, name="swa_fwd", grid=(nb,),
        in_specs=[_tile(AW, T), _tile(KVW2, T), pl.BlockSpec((T, KVW2), lambda i: (jnp.maximum(i - 1, 0), 0)),
                  pl.BlockSpec(memory_space=pltpu.SMEM)],
        out_specs=_tile(AW, T), out_shape=_S((L, AW), BF16),
        scratch_shapes=[pltpu.VMEM((GQA * T, HEAD_DIM), BF16), pltpu.VMEM((GQA * T, 2 * T), BF16)],
        compiler_params=_params(("arbitrary",)))(q, kv, kv, sinks)


def _out_proj_ln1(yn, o, w_out, xhat0, vecs, alpha):
    L, W = yn.shape
    D = xhat0.shape[1]
    MIX = w_out.shape[0]
    tm = _pick(L, (MLP_TM, 128))

    def body(yn_ref, o_ref, w_ref, xh_ref, v_ref, mix_ref, xhat1_ref, rstd1_ref, u2_ref):
        mix = _dot(yn_ref[...], w_ref[0:W, :]) + _dot(o_ref[...], w_ref[W:MIX, :])
        mix_ref[...] = mix
        h0 = xh_ref[...] * v_ref[0:1, :] + v_ref[1:2, :]
        xhat1, rstd1 = _ln_fwd(alpha * h0 + (1.0 + v_ref[2:3, :]) * mix)
        xhat1_ref[...] = xhat1
        rstd1_ref[...] = rstd1
        h1 = xhat1 * v_ref[3:4, :] + v_ref[4:5, :]
        u2_ref[...] = (h1 * (1.0 + v_ref[5:6, :]) + v_ref[6:7, :]).astype(BF16)

    return pl.pallas_call(
        body, name="out_proj_ln1", grid=(L // tm,),
        in_specs=[_tile(W, tm), _tile(MIX - W, tm), _res((MIX, D)), _tile(D, tm), _acc((8, D))],
        out_specs=(_tile(D, tm), _tile(D, tm), _tile(1, tm), _tile(D, tm)),
        out_shape=(_S((L, D), F32), _S((L, D), F32), _S((L, 1), F32), _S((L, D), BF16)),
        compiler_params=_params(("arbitrary",)))(yn, o, w_out, xhat0, vecs)


def _mlp_loss(u2, w1, w2, xhat1, tgt, vecs, b1, alpha):
    L, D = xhat1.shape
    FF = w1.shape[1]
    tm = _pick(L, (MLP_TM, 128))
    sub = min(tm, MLP_SUB)
    fc = _pick(FF, (MLP_FC, 256, 128))

    def body(u2_ref, w1_ref, w2_ref, xh_ref, t_ref, v_ref, b1_ref, rr_ref, dr2_ref, acc_ref, loss_ref):
        @pl.when(pl.program_id(0) == 0)
        def _():
            acc_ref[...] = jnp.zeros_like(acc_ref)
            loss_ref[...] = jnp.zeros_like(loss_ref)

        for s in range(tm // sub):
            rs = slice(s * sub, (s + 1) * sub)
            u2 = u2_ref[rs, :]
            f = jnp.zeros((sub, D), F32) + v_ref[5:6, :]
            for j in range(FF // fc):
                cs = slice(j * fc, (j + 1) * fc)
                rr = jnp.maximum(_dot(u2, w1_ref[:, cs]) + b1_ref[:, cs], 0.0)
                rr_ref[rs, cs] = rr.astype(BF16)
                f = f + _dot((rr * rr).astype(BF16), w2_ref[cs, :])
            xhat1 = xh_ref[rs, :]
            h1 = xhat1 * v_ref[0:1, :] + v_ref[1:2, :]
            xhat2, rstd2 = _ln_fwd(alpha * h1 + (1.0 + v_ref[2:3, :]) * f)
            e = xhat2 * v_ref[3:4, :] + v_ref[4:5, :] - t_ref[rs, :]
            loss_ref[...] += 0.5 * jnp.sum(_mean(e * e))
            dy = e * (1.0 / D)
            dr2 = _ln_bwd(dy * v_ref[3:4, :], xhat2, rstd2)
            dr2_ref[rs, :] = dr2
            acc_ref[0:1, :] += _colsum(dy * xhat2)
            acc_ref[1:2, :] += _colsum(dy)
            acc_ref[2:3, :] += _colsum(dr2 * f)

    return pl.pallas_call(
        body, name="mlp_loss", grid=(L // tm,),
        in_specs=[_tile(D, tm), _res((D, FF)), _res((FF, D)), _tile(D, tm), _tile(D, tm), _acc((8, D)), _acc((1, FF))],
        out_specs=(_tile(FF, tm), _tile(D, tm), _acc((8, D)), _acc((1, LANE))),
        out_shape=(_S((L, FF), BF16), _S((L, D), F32), _S((8, D), F32), _S((1, LANE), F32)),
        compiler_params=_params(("arbitrary",)))(u2, w1, w2, xhat1, tgt, vecs, b1)


def _mlp_bwd_a(dr2, rr, w2, g2):
    L, D = dr2.shape
    FF = w2.shape[0]
    tm = _pick(L, (MLP_TM, 128))
    fc = _pick(FF, (MLP_FC, 256, 128))

    def body(dr2_ref, rr_ref, w2_ref, g2_ref, df_ref, da_ref, gb2_ref, gb1_ref):
        @pl.when(pl.program_id(0) == 0)
        def _():
            gb2_ref[...] = jnp.zeros_like(gb2_ref)
            gb1_ref[...] = jnp.zeros_like(gb1_ref)

        df = (1.0 + g2_ref[...]) * dr2_ref[...]
        gb2_ref[...] += _colsum(df)
        dfb = df.astype(BF16)
        df_ref[...] = dfb
        for j in range(FF // fc):
            cs = slice(j * fc, (j + 1) * fc)
            da = _dot_nt(dfb, w2_ref[cs, :]) * (2.0 * rr_ref[:, cs].astype(F32))
            gb1_ref[:, cs] += _colsum(da)
            da_ref[:, cs] = da.astype(BF16)

    return pl.pallas_call(
        body, name="mlp_bwd_a", grid=(L // tm,),
        in_specs=[_tile(D, tm), _tile(FF, tm), _res((FF, D)), _acc((1, D))],
        out_specs=(_tile(D, tm), _tile(FF, tm), _acc((1, D)), _acc((1, FF))),
        out_shape=(_S((L, D), BF16), _S((L, FF), BF16), _S((1, D), F32), _S((1, FF), F32)),
        compiler_params=_params(("arbitrary",)))(dr2, rr, w2, g2)


def _mlp_bwd_b(da, w1, dr2, xhat1, rstd1, mix, w_out, vecs, alpha, W):
    L, FF = da.shape
    D = dr2.shape[1]
    MIX = w_out.shape[0]
    tm = _pick(L, (MLP_TM, 128))

    def body(da_ref, w1_ref, dr2_ref, xh_ref, rs_ref, mix_ref, wo_ref, v_ref, dmix_ref, dh0_ref, dyn_ref, do_ref, acc_ref):
        @pl.when(pl.program_id(0) == 0)
        def _():
            acc_ref[...] = jnp.zeros_like(acc_ref)

        du2 = _dot_nt(da_ref[...], w1_ref[...])
        xhat1 = xh_ref[...]
        h1 = xhat1 * v_ref[0:1, :] + v_ref[1:2, :]
        acc_ref[0:1, :] += _colsum(du2 * h1)
        acc_ref[1:2, :] += _colsum(du2)
        dh1 = alpha * dr2_ref[...] + du2 * (1.0 + v_ref[2:3, :])
        acc_ref[2:3, :] += _colsum(dh1 * xhat1)
        acc_ref[3:4, :] += _colsum(dh1)
        dr1 = _ln_bwd(dh1 * v_ref[0:1, :], xhat1, rs_ref[...])
        acc_ref[4:5, :] += _colsum(dr1 * mix_ref[...])
        dh0_ref[...] = alpha * dr1
        dmix = ((1.0 + v_ref[3:4, :]) * dr1).astype(BF16)
        dmix_ref[...] = dmix
        dyn_ref[...] = _dot_nt(dmix, wo_ref[0:W, :])
        do_ref[...] = _dot_nt(dmix, wo_ref[W:MIX, :]).astype(BF16)

    return pl.pallas_call(
        body, name="mlp_bwd_b", grid=(L // tm,),
        in_specs=[_tile(FF, tm), _res((D, FF)), _tile(D, tm), _tile(D, tm), _tile(1, tm), _tile(D, tm), _res((MIX, D)),
                  _acc((8, D))],
        out_specs=(_tile(D, tm), _tile(D, tm), _tile(W, tm), _tile(MIX - W, tm), _acc((8, D))),
        out_shape=(_S((L, D), BF16), _S((L, D), F32), _S((L, W), F32), _S((L, MIX - W), BF16), _S((8, D), F32)),
        compiler_params=_params(("arbitrary",)))(da, w1, dr2, xhat1, rstd1, mix, w_out, vecs)


def _swa_bwd(q, kv, do, sinks, dims):
    L, AW = q.shape
    KV, KVW2 = dims["KV"], dims["KVW2"]
    T = CHUNK
    nb = L // T
    slopes = _alibi_slopes(dims["AH"])
    scale = HEAD_DIM ** -0.5

    def body(q_ref, kvc_ref, kvp_ref, do_ref, sink_ref, dq_ref, dkv_ref, dsink_ref, carry_ref,
             qg_ref, dog_ref, pt_ref, dst_ref):
        i = pl.program_id(0)

        @pl.when(i == 0)
        def _():
            carry_ref[...] = jnp.zeros_like(carry_ref)
            dsink_ref[...] = jnp.zeros_like(dsink_ref)

        @pl.when(i < nb)
        def _():
            c = lax.broadcasted_iota(jnp.int32, (2 * T, T), 0)
            r_ = lax.broadcasted_iota(jnp.int32, (2 * T, T), 1)
            dist_i = r_ + T - c
            valid = (dist_i >= 0) & (dist_i < CHUNK) & ((c >= T) | (i > 0))
            dist = dist_i.astype(F32)
            lane = lax.broadcasted_iota(jnp.int32, (1, LANE), 1)
            dsink = jnp.zeros((1, LANE), F32)
            dks, dvs = [], []
            for g in range(KV):
                kk, vv = _kv_heads(kvc_ref, kvp_ref, g, KV)
                for r in range(GQA):
                    hs = slice((g * GQA + r) * HEAD_DIM, (g * GQA + r + 1) * HEAD_DIM)
                    qg_ref[r * T:(r + 1) * T, :] = q_ref[:, hs]
                    dog_ref[r * T:(r + 1) * T, :] = do_ref[:, hs]
                st_all = _dot_nt(kk, qg_ref[...])
                dpt_all = _dot_nt(vv, dog_ref[...])
                for r in range(GQA):
                    h = g * GQA + r
                    cs = slice(r * T, (r + 1) * T)
                    p, p_sink = _attn_probs(st_all[:, cs], dist, valid, slopes[h], sink_ref[h], 0)
                    dp = dpt_all[:, cs]
                    delta = jnp.sum(p * dp, axis=0, keepdims=True)
                    pt_ref[:, cs] = p.astype(BF16)
                    dst_ref[:, cs] = (p * (dp - delta)).astype(BF16)
                    dsink = dsink + jnp.where(lane == h, -jnp.sum(p_sink * delta), 0.0)
                dst = dst_ref[...]
                dks.append(_dot(dst, qg_ref[...]) * scale)
                dvs.append(_dot(pt_ref[...], dog_ref[...]))
                dq_all = _dot_tn(dst, kk) * scale
                for r in range(GQA):
                    hs = slice((g * GQA + r) * HEAD_DIM, (g * GQA + r + 1) * HEAD_DIM)
                    dq_ref[:, hs] = dq_all[r * T:(r + 1) * T, :].astype(BF16)
            dkv = jnp.concatenate(dks + dvs, axis=1)
            dsink_ref[...] += dsink
            dkv_ref[...] = carry_ref[...] + dkv[0:T, :]
            carry_ref[...] = dkv[T:2 * T, :]

        @pl.when(i == nb)
        def _():
            dkv_ref[...] = carry_ref[...]

    last = nb - 1
    return pl.pallas_call(
        body, name="swa_bwd", grid=(nb + 1,),
        in_specs=[pl.BlockSpec((T, AW), lambda i: (jnp.minimum(i, last), 0)),
                  pl.BlockSpec((T, KVW2), lambda i: (jnp.minimum(i, last), 0)),
                  pl.BlockSpec((T, KVW2), lambda i: (jnp.clip(i - 1, 0, last), 0)),
                  pl.BlockSpec((T, AW), lambda i: (jnp.minimum(i, last), 0)),
                  pl.BlockSpec(memory_space=pltpu.SMEM)],
        out_specs=(pl.BlockSpec((T, AW), lambda i: (jnp.minimum(i, last), 0)),
                   pl.BlockSpec((T, KVW2), lambda i: (jnp.maximum(i - 1, 0), 0)), _acc((1, LANE))),
        out_shape=(_S((L, AW), BF16), _S((L, KVW2), F32), _S((1, LANE), F32)),
        scratch_shapes=[pltpu.VMEM((T, KVW2), F32), pltpu.VMEM((GQA * T, HEAD_DIM), BF16),
                        pltpu.VMEM((GQA * T, HEAD_DIM), BF16), pltpu.VMEM((2 * T, GQA * T), BF16),
                        pltpu.VMEM((2 * T, GQA * T), BF16)],
        compiler_params=_params(("arbitrary",)))(q, kv, kv, do, sinks)


def _ssd_bwd(dyn, y, z, pre_act, dt_raw, sprev, dtb, alog, dsk, nw, dims):
    L, CD = pre_act.shape
    W, H, G, N = dims["W"], dims["H"], SSD_GROUPS, SSD_STATE
    T = CHUNK
    R = H // G
    GW = W // G
    nc = L // T
    HP = H * HEAD_DIM

    def body(dyn_ref, y_ref, z_ref, pre_ref, dt_ref, sp_ref, dtb_ref, alog_ref, dsk_ref, nw_ref,
             dz_ref, dpre_ref, ddt_ref, acc_ref, hacc_ref, ds_ref, dtx_ref, acx_ref, xb_ref, dyb_ref, r12_ref,
             dx_ref, rows_ref):
        i = pl.program_id(0)

        @pl.when(i == 0)
        def _():
            ds_ref[...] = jnp.zeros_like(ds_ref)
            acc_ref[...] = jnp.zeros_like(acc_ref)
            hacc_ref[...] = jnp.zeros_like(hacc_ref)

        pre = pre_ref[...]
        spre = _sigmoid(pre)
        act = pre * spre
        xs = act[:, :W]
        dt_in = dt_ref[...] + dtb_ref[...]
        dt = _softplus(dt_in)
        a_neg = -jnp.exp(alog_ref[...])
        a = dt * a_neg
        low = _tri(T)
        upf = _tri(T, upper=True).astype(F32)
        acum = _dot_hi(low.astype(F32), a)
        acum_t = acum.T

        y = y_ref[...]
        zz = z_ref[...]
        sg = _sigmoid(zz)
        sz = zz * sg
        hh = y * sz
        dyn_v = dyn_ref[...]
        parts = []
        for g in range(G):
            gs = slice(g * GW, (g + 1) * GW)
            hg = hh[:, gs]
            hhat = hg * lax.rsqrt(_mean(hg * hg) + RMS_EPS)
            rg = lax.rsqrt(_mean(hg * hg) + RMS_EPS)
            acc_ref[0:1, gs] += _colsum(dyn_v[:, gs] * hhat)
            dhhat = dyn_v[:, gs] * nw_ref[:, gs]
            parts.append(rg * (dhhat - hhat * _mean(dhhat * hhat)))
        dhh = jnp.concatenate(parts, axis=1)
        dy = dhh * sz
        dz_ref[...] = (dhh * y * (sg * (1.0 + zz * (1.0 - sg)))).astype(BF16)
        acc_ref[1:2, :] += _colsum(dy * xs)
        dyb_ref[...] = dy.astype(BF16)

        _expand_heads_lanes(dtx_ref, dt, H)
        _expand_heads_lanes(acx_ref, acum, H)
        dtx = dtx_ref[...]
        acx = acx_ref[...]
        lastx = acx[T - 1:T, :]
        ex = jnp.exp(acx)
        decx = jnp.exp(lastx - acx)
        elx = jnp.exp(lastx)
        xd = xs * dtx
        xb_ref[...] = xd.astype(BF16)
        xdecb = (xd * decx).astype(BF16)
        dgb = (ex * dy).astype(BF16)
        rows_ref[...] = jnp.zeros_like(rows_ref)

        lane = lax.broadcasted_iota(jnp.int32, (T, LANE), 1)
        sub = lax.broadcasted_iota(jnp.int32, (T, LANE), 0)
        subr = lax.broadcasted_iota(jnp.int32, (LANE, T), 0)
        da_col = jnp.zeros((T, LANE), F32)
        da_row = jnp.zeros((LANE, T), F32)
        dbs, dcs = [], []
        for g in range(G):
            gs = slice(g * GW, (g + 1) * GW)
            bgb = act[:, W + g * N:W + (g + 1) * N].astype(BF16)
            cgb = act[:, W + G * N + g * N:W + G * N + (g + 1) * N].astype(BF16)
            stg = sp_ref[0, :, gs]
            stb = stg.astype(BF16)
            dsn = ds_ref[:, gs]
            dsnb = dsn.astype(BF16)
            gm = _dot(cgb, stb)
            dc = _dot_nt(dgb[:, gs], stb)
            dsp = _dot_tn(cgb, dgb[:, gs])
            dxs_ = decx[:, gs] * _dot(bgb, dsnb)
            db = _dot_nt(xdecb[:, gs], dsnb)
            xdg = xd[:, gs]
            r12_ref[:, gs] = dy[:, gs] * ex[:, gs] * gm - xdg * dxs_
            rows_ref[0:1, gs] = _colsum(dsn * stg) * elx[:, gs]
            rows_ref[1:2, gs] = _colsum(xdg * dxs_)
            ds_ref[:, gs] = dsp + dsn * elx[:, gs]
            cb_g = _dot_nt(cgb, bgb)
            dcb = jnp.zeros((T, T), F32)
            for r in range(R):
                h = g * R + r
                hs = slice(h * HEAD_DIM, (h + 1) * HEAD_DIM)
                lm = jnp.where(low, jnp.exp(acum[:, h:h + 1] - acum_t[h:h + 1, :]), 0.0)
                mm = cb_g * lm
                dyb = dyb_ref[:, hs]
                dm = _dot_nt(dyb, xb_ref[:, hs])
                dx_ref[:, hs] = dxs_[:, r * HEAD_DIM:(r + 1) * HEAD_DIM] + _dot_tn(mm.astype(BF16), dyb)
                dcb = dcb + dm * lm
                qm = dm * mm
                da_col = jnp.where(lane == h, jnp.sum(qm, axis=1, keepdims=True), da_col)
                da_row = jnp.where(subr == h, jnp.sum(qm, axis=0, keepdims=True), da_row)
            dcbb = dcb.astype(BF16)
            dcs.append(dc + _dot(dcbb, bgb))
            dbs.append(db + _dot_tn(dcbb, cgb))
        dx = dx_ref[...]
        rows = _head_reduce(rows_ref[...])
        dlast = rows[0:1, :] + rows[1:2, :]
        da_col = da_col + _head_reduce(r12_ref[...]) + jnp.where(sub == T - 1, dlast, 0.0)
        dacum = da_col - da_row.T
        da = _dot_hi(upf, dacum)
        ddt = _head_reduce(dx * xs) + da * a_neg
        hacc_ref[1:2, :] += _colsum(da * dt) * a_neg
        ddt_raw = ddt * _sigmoid(dt_in)
        hacc_ref[0:1, :] += _colsum(ddt_raw)
        ddt_ref[...] = ddt_raw
        dact = jnp.concatenate([dsk_ref[...] * dy + dx * dtx] + dbs + dcs, axis=1)
        dpre_ref[...] = dact * (spre * (1.0 + pre * (1.0 - spre)))

        @pl.when(i == nc - 1)
        def _():
            ch = lax.broadcasted_iota(jnp.int32, (W, LANE), 0)
            lo = lax.broadcasted_iota(jnp.int32, (W, LANE), 1) * HEAD_DIM
            hacc_ref[2:3, :] = _dot_hi(acc_ref[1:2, :], ((ch >= lo) & (ch < lo + HEAD_DIM)).astype(F32), zero_one_rhs=True)

    rev = lambda i: (nc - 1 - i, 0)
    return pl.pallas_call(
        body, name="ssd_bwd", grid=(nc,),
        in_specs=[pl.BlockSpec((T, W), rev), pl.BlockSpec((T, W), rev), pl.BlockSpec((T, W), rev), pl.BlockSpec((T, CD), rev),
                  pl.BlockSpec((T, LANE), rev), pl.BlockSpec((1, N, HP), lambda i: (nc - 1 - i, 0, 0)),
                  _acc((1, LANE)), _acc((1, LANE)), _acc((1, W)), _acc((1, W))],
        out_specs=(pl.BlockSpec((T, W), rev), pl.BlockSpec((T, CD), rev), pl.BlockSpec((T, LANE), rev), _acc((8, W)),
                   _acc((8, LANE))),
        out_shape=(_S((L, W), BF16), _S((L, CD), F32), _S((L, LANE), F32), _S((8, W), F32), _S((8, LANE), F32)),
        scratch_shapes=[pltpu.VMEM((N, HP), F32), pltpu.VMEM((T, W), F32),
                        pltpu.VMEM((T, W), F32), pltpu.VMEM((T, W), BF16), pltpu.VMEM((T, W), BF16), pltpu.VMEM((T, W), F32),
                        pltpu.VMEM((T, W), F32), pltpu.VMEM((8, W), F32)],
        compiler_params=_params(("arbitrary",)))(dyn, y, z, pre_act, dt_raw, sprev, dtb, alog, dsk, nw)


def _conv_bwd(dpre, xbc, cw):
    L, CD = xbc.shape
    tm = _pick(L, (CONV_TM, 128))
    cb = CD if CONV_CB >= CD else _pick(CD, (CONV_CB, 128))
    nt = L // tm
    hb = tm // HALO

    def body(dp_ref, dn_ref, u_ref, cw_ref, du_ref, acc_ref, extd_ref):
        i = pl.program_id(1)

        @pl.when(i == 0)
        def _():
            acc_ref[...] = jnp.zeros_like(acc_ref)

        extd_ref[0:tm, :] = dp_ref[...]
        extd_ref[tm:tm + HALO, :] = jnp.where(i == nt - 1, 0.0, dn_ref[...])
        for c in range(tm // ROW_CHUNK):
            r0 = c * ROW_CHUNK
            rows = slice(r0, r0 + ROW_CHUNK)
            dp = dp_ref[rows, :]
            u = u_ref[rows, :]
            du = cw_ref[CONV_K - 1:CONV_K, :] * dp
            acc_ref[CONV_K - 1:CONV_K, :] += _colsum(dp * u)
            for k in range(CONV_K - 1):
                s = CONV_K - 1 - k
                dsh = extd_ref[r0 + s:r0 + s + ROW_CHUNK, :]
                du = du + cw_ref[k:k + 1, :] * dsh
                acc_ref[k:k + 1, :] += _colsum(u * dsh)
            acc_ref[CONV_K:CONV_K + 1, :] += _colsum(dp)
            du_ref[rows, :] = du.astype(BF16)

    tile = pl.BlockSpec((tm, cb), lambda j, i: (i, j))
    return pl.pallas_call(
        body, name="conv_bwd", grid=(CD // cb, nt),
        in_specs=[tile, pl.BlockSpec((HALO, cb), lambda j, i: (jnp.minimum((i + 1) * hb, nt * hb - 1), j)),
                  tile, pl.BlockSpec((CONV_K, cb), lambda j, i: (0, j))],
        out_specs=(tile, pl.BlockSpec((8, cb), lambda j, i: (0, j))),
        out_shape=(_S((L, CD), BF16), _S((8, CD), F32)),
        scratch_shapes=[pltpu.VMEM((tm + HALO, cb), F32)],
        compiler_params=_params(("arbitrary", "arbitrary")))(dpre, dpre, xbc, cw)


def _in_proj_bwd(dz, dxbc, dq, dkv, ddt, w_t, xhat0, rstd0, dh0p, vecs, dims):
    L, D = xhat0.shape
    W, CD, AW, KVW2 = dims["W"], dims["CD"], dims["AW"], dims["KVW2"]
    PROJ = w_t.shape[0]
    tm = _pick(L, (MLP_TM, 128))
    r_z, r_xbc, r_dt, r_q, r_kv = _proj_rows(dims)

    def body(dz_ref, dxbc_ref, dq_ref, dkv_ref, ddt_ref, w_ref, xh_ref, rs_ref, dh0_ref, v_ref, gx_ref, acc_ref):
        @pl.when(pl.program_id(0) == 0)
        def _():
            acc_ref[...] = jnp.zeros_like(acc_ref)

        du1 = _dot(dz_ref[...], w_ref[r_z[0]:r_z[1], :])
        du1 = du1 + _dot(dxbc_ref[...], w_ref[r_xbc[0]:r_xbc[1], :])
        du1 = du1 + _dot(dq_ref[...], w_ref[r_q[0]:r_q[1], :])
        du1 = du1 + _dot(dkv_ref[...].astype(BF16), w_ref[r_kv[0]:r_kv[1], :])
        du1 = du1 + _dot(ddt_ref[...].astype(BF16), w_ref[r_dt[0]:r_dt[1], :])
        xhat0 = xh_ref[...]
        h0 = xhat0 * v_ref[0:1, :] + v_ref[1:2, :]
        acc_ref[0:1, :] += _colsum(du1 * h0)
        acc_ref[1:2, :] += _colsum(du1)
        dh0 = dh0_ref[...] + du1 * (1.0 + v_ref[2:3, :])
        acc_ref[2:3, :] += _colsum(dh0 * xhat0)
        acc_ref[3:4, :] += _colsum(dh0)
        gx_ref[...] = _ln_bwd(dh0 * v_ref[0:1, :], xhat0, rs_ref[...])

    return pl.pallas_call(
        body, name="in_proj_bwd", grid=(L // tm,),
        in_specs=[_tile(W, tm), _tile(CD, tm), _tile(AW, tm), _tile(KVW2, tm), _tile(LANE, tm), _res((PROJ, D)),
                  _tile(D, tm), _tile(1, tm), _tile(D, tm), _acc((8, D))],
        out_specs=(_tile(D, tm), _acc((8, D))),
        out_shape=(_S((L, D), F32), _S((8, D), F32)),
        compiler_params=_params(("arbitrary",)))(dz, dxbc, dq, dkv, ddt, w_t, xhat0, rstd0, dh0p, vecs)


_WEIGHTS = ['ln_in_g', 'ln_in_b', 'ada_w', 'ada_b', 'w_in', 'conv_w', 'conv_b', 'dt_bias', 'a_log', 'd_skip', 'ssd_norm_w',
            'attn_sinks', 'w_out', 'ln1_g', 'ln1_b', 'w_ff1', 'b_ff1', 'w_ff2', 'b_ff2', 'ln2_g', 'ln2_b']
_BIG = ('w_in', 'w_out', 'w_ff1', 'w_ff2')
_SMALL = ('ada_b', 'ln_in_g', 'ln_in_b', 'conv_b', 'dt_bias', 'a_log', 'd_skip', 'ssd_norm_w', 'attn_sinks', 'ln1_g', 'ln1_b',
          'b_ff1', 'b_ff2', 'ln2_g', 'ln2_b')


def _pad_lanes(v, n=None):
    v = v.reshape(1, -1)
    n = n or -(-v.shape[1] // LANE) * LANE
    return jnp.pad(v, ((0, 0), (0, n - v.shape[1])))


def _vec8(rows, D):
    rows = [r.reshape(1, D) for r in rows]
    return jnp.concatenate(rows + [jnp.zeros((8 - len(rows), D), F32)], axis=0)


def kernel(x, c, ln_in_g, ln_in_b, ada_w, ada_b, w_in, conv_w, conv_b, dt_bias, a_log, d_skip, ssd_norm_w, attn_sinks, w_out, ln1_g, ln1_b, w_ff1, b_ff1, w_ff2, b_ff2, ln2_g, ln2_b, loss_target, m_ln_in_g, m_ln_in_b, m_ada_w, m_ada_b, m_w_in, m_conv_w, m_conv_b, m_dt_bias, m_a_log, m_d_skip, m_ssd_norm_w, m_attn_sinks, m_w_out, m_ln1_g, m_ln1_b, m_w_ff1, m_b_ff1, m_w_ff2, m_b_ff2, m_ln2_g, m_ln2_b, v_ln_in_g, v_ln_in_b, v_ada_w, v_ada_b, v_w_in, v_conv_w, v_conv_b, v_dt_bias, v_a_log, v_d_skip, v_ssd_norm_w, v_attn_sinks, v_w_out, v_ln1_g, v_ln1_b, v_w_ff1, v_b_ff1, v_w_ff2, v_b_ff2, v_ln2_g, v_ln2_b):
    wts = dict(ln_in_g=ln_in_g, ln_in_b=ln_in_b, ada_w=ada_w, ada_b=ada_b, w_in=w_in, conv_w=conv_w, conv_b=conv_b,
               dt_bias=dt_bias, a_log=a_log, d_skip=d_skip, ssd_norm_w=ssd_norm_w, attn_sinks=attn_sinks, w_out=w_out,
               ln1_g=ln1_g, ln1_b=ln1_b, w_ff1=w_ff1, b_ff1=b_ff1, w_ff2=w_ff2, b_ff2=b_ff2, ln2_g=ln2_g, ln2_b=ln2_b)
    ms = dict(ln_in_g=m_ln_in_g, ln_in_b=m_ln_in_b, ada_w=m_ada_w, ada_b=m_ada_b, w_in=m_w_in, conv_w=m_conv_w,
              conv_b=m_conv_b, dt_bias=m_dt_bias, a_log=m_a_log, d_skip=m_d_skip, ssd_norm_w=m_ssd_norm_w,
              attn_sinks=m_attn_sinks, w_out=m_w_out, ln1_g=m_ln1_g, ln1_b=m_ln1_b, w_ff1=m_w_ff1, b_ff1=m_b_ff1,
              w_ff2=m_w_ff2, b_ff2=m_b_ff2, ln2_g=m_ln2_g, ln2_b=m_ln2_b)
    vs = dict(ln_in_g=v_ln_in_g, ln_in_b=v_ln_in_b, ada_w=v_ada_w, ada_b=v_ada_b, w_in=v_w_in, conv_w=v_conv_w,
              conv_b=v_conv_b, dt_bias=v_dt_bias, a_log=v_a_log, d_skip=v_d_skip, ssd_norm_w=v_ssd_norm_w,
              attn_sinks=v_attn_sinks, w_out=v_w_out, ln1_g=v_ln1_g, ln1_b=v_ln1_b, w_ff1=v_w_ff1, b_ff1=v_b_ff1,
              w_ff2=v_w_ff2, b_ff2=v_b_ff2, ln2_g=v_ln2_g, ln2_b=v_ln2_b)

    L, D = x.shape[1], x.shape[2]
    depth = w_in.shape[0]
    assert depth == 1 and x.shape[0] == 1 and L % CHUNK == 0
    W = D
    H = W // HEAD_DIM
    CD = W + 2 * SSD_GROUPS * SSD_STATE
    AW = D
    AH = AW // HEAD_DIM
    KV = AH // GQA
    KVW2 = 2 * KV * HEAD_DIM
    PROJ = W + CD + H + AW + KVW2
    FF = w_ff1.shape[2] * N_DEV
    MIX = w_out.shape[1] * N_DEV
    assert w_in.shape[2] * N_DEV == PROJ and MIX == W + AW and H <= LANE and AH <= LANE
    dims = dict(W=W, H=H, CD=CD, AW=AW, AH=AH, KV=KV, KVW2=KVW2)
    alpha = (2.0 * depth) ** 0.25
    C6 = ada_w.shape[2]
    CW = conv_w.shape[2]

    ax, ay, ac = _my_pos()
    me = 4 * ax + 2 * ay + ac
    x2 = x.reshape(L, D)
    tgt = loss_target.reshape(L, D)
    r1 = lambda a: a.reshape(1, -1)

    ada_b_cols = lax.dynamic_slice(ada_b, (0, me * C6), (1, C6))
    cs_all, mod = _mod_fwd(c, ada_w[0], ada_b_cols)
    sh1, sc1, g1, sh2, sc2, g2 = [r1(t) for t in jnp.split(mod.reshape(-1), 6)]

    wg_in, cwg = _ag_weights([w_in[0].T.astype(BF16), conv_w[0]], cs_all)
    shards2 = [w_out[0].astype(BF16), w_ff1[0].astype(BF16), w_ff2[0].astype(BF16)]
    lands2 = [lax.dynamic_update_slice(lax.empty((N_DEV,) + s.shape, s.dtype), s[None], (me, 0, 0)) for s in shards2]
    ag_ss, ag_rs, ag_arr, ag_token = _split_start(shards2 + lands2, _plan_gather(3), cwg, "ag_ici_start")
    sh1 = sh1 + ag_token[0:1, 0:1]
    w_pad = _merge_blocks(wg_in)
    cw_full = cwg.transpose(1, 0, 2).reshape(CONV_K, CD)

    dtb = _pad_lanes(dt_bias, LANE)
    alog = _pad_lanes(a_log, LANE)
    dsk = jnp.repeat(d_skip.reshape(-1), HEAD_DIM).reshape(1, W)
    sinks = attn_sinks.reshape(-1)
    g_in, b_in = r1(ln_in_g), r1(ln_in_b)

    xhat0, rstd0, u1, z, xbc, q, kv, dt_raw = _ln_in_proj(x2, g_in, b_in, sc1, sh1, w_pad, dims)
    y, yn, sprev, pre = _conv_ssd(xbc, dt_raw, z, cw_full, conv_b, dtb, alog, dsk, ssd_norm_w, dims)
    ag_arr = _split_wait(ag_ss, ag_rs, ag_arr, _plan_gather(3), yn, "ag_ici_wait")
    fw_ss, fw_rs, ag_land, fw_token = _split_start(ag_arr[3:], _plan_forward(3), yn, "ag_fwd_start")
    o = _swa_fwd(q, kv, sinks + fw_token[0, 0], dims)
    wg_out, wg_ff1, wg_ff2 = _split_wait(fw_ss, fw_rs, ag_land, _plan_forward(3), o, "ag_fwd_wait")
    w_out_full = wg_out.reshape(MIX, D)
    w1_full = wg_ff1.transpose(1, 0, 2).reshape(D, FF)
    w2_full = wg_ff2.reshape(FF, D)
    mix, xhat1, rstd1, u2 = _out_proj_ln1(yn, o, w_out_full, xhat0, _vec8([g_in, b_in, g1, ln1_g, ln1_b, sc2, sh2], D), alpha)
    rr, dr2, acc_f, loss_loc = _mlp_loss(u2, w1_full, w2_full, xhat1, tgt,
                                         _vec8([ln1_g, ln1_b, g2, ln2_g, ln2_b, b_ff2], D), b_ff1, alpha)

    df, da, gb2, gb1 = _mlp_bwd_a(dr2, rr, w2_full, g2)
    gw_ff2 = _matmul_tn(rr, df, "gw_ff2", square_a=True)
    gw_ff1t = _matmul_tn(da, u2, "gw_ff1")
    dmix, dh0p, dyn, do, acc_b = _mlp_bwd_b(da, w1_full, dr2, xhat1, rstd1, mix, w_out_full,
                                            _vec8([ln1_g, ln1_b, sc2, g1], D), alpha, W)
    gw_out = jnp.concatenate([_matmul_tn(yn, dmix, "gw_out_ssd"), _matmul_tn(o, dmix, "gw_out_attn")], axis=0)

    core = jnp.reshape(ac, (1,)).astype(jnp.int32)
    blocked1 = [gw_out.reshape(N_DEV, MIX // N_DEV, D), gw_ff1t.reshape(N_DEV, FF // N_DEV, D),
                gw_ff2.reshape(N_DEV, FF // N_DEV, D)]
    lands1 = [lax.empty(b.shape, b.dtype) for b in blocked1]
    rs_ss, rs_rs, rs_arr, rs_token = _split_start(blocked1 + lands1, _plan_scatter_all(3), do, "rs_all_start")
    dq, dkv, dsink = _swa_bwd(q, kv, do, sinks + rs_token[0, 0], dims)
    dz, dpre, ddt, acc_s, hacc = _ssd_bwd(dyn, y, z, pre, dt_raw, sprev, dtb + rs_token[0:1, 0:1], alog, dsk, ssd_norm_w, dims)
    dxbc, acc_c = _conv_bwd(dpre, xbc, cw_full)
    gw_in = _gw_in((dz, dxbc, ddt, dq, dkv), u1, dims)

    blocked2 = [_split_blocks(gw_in, N_DEV)]
    pairs2 = [_pair_sum(b, r, core) for b, r in zip(blocked2, _rs_d2d(blocked2, "rs_d2d_2"))]
    lands2 = [lax.empty(p.shape, p.dtype) for p in pairs2]
    r2_ss, r2_rs, r2_arr, r2_token = _split_start(pairs2 + lands2, _plan_scatter(1), gw_in, "rs_ici_start_2")
    grad_x, acc_i = _in_proj_bwd(dz, dxbc, dq, dkv, ddt, w_pad, xhat0, rstd0, dh0p,
                                 _vec8([g_in, b_in, sc1], D) + r2_token[0:1, 0:1], dims)

    srcs = [acc_i, acc_b, acc_f, acc_s, acc_c, hacc, dsink, gb1, gb2, loss_loc]
    I_, B_, F_, S_, C_, H_, K_, G1_, G2_, L_ = range(10)
    seg_of = dict(ada_b=[(I_, 1, D), (I_, 0, D), (B_, 4, D), (B_, 1, D), (B_, 0, D), (F_, 2, D)],
                  ln_in_g=[(I_, 2, D)], ln_in_b=[(I_, 3, D)], conv_b=[(C_, CONV_K, CD)], dt_bias=[(H_, 0, H)],
                  a_log=[(H_, 1, H)], d_skip=[(H_, 2, H)], ssd_norm_w=[(S_, 0, W)], attn_sinks=[(K_, 0, AH)],
                  ln1_g=[(B_, 2, D)], ln1_b=[(B_, 3, D)], b_ff1=[(G1_, 0, FF)], b_ff2=[(G2_, 0, D)],
                  ln2_g=[(F_, 0, D)], ln2_b=[(F_, 1, D)])
    pieces = [seg_of[n] for n in _SMALL] + [[(C_, t, CD) for t in range(CONV_K)], [(L_, 0, 1)]]
    params = [tuple(t[n].reshape(1, -1) for t in (wts, ms, vs)) for n in _SMALL]
    res = _small_sync_adamw(srcs, pieces, params, 6 * D)
    grads, deltas, new_m, new_v = {}, {}, {}, {}
    for k, n in enumerate(_SMALL):
        grads[n], deltas[n], new_m[n], new_v[n] = (t.reshape(wts[n].shape) for t in res[4 * k:4 * k + 4])
    gcw_full, dmod_all, loss_row = res[4 * len(_SMALL):]
    loss = loss_row[0, 0]

    g_ = lax.dynamic_slice(gcw_full, (0, me * CW), (CONV_K, CW))
    d_, m_, v_ = _adamw(conv_w[0], g_, m_conv_w[0], v_conv_w[0])
    grads['conv_w'], deltas['conv_w'], new_m['conv_w'], new_v['conv_w'] = (t[None] for t in (g_, d_, m_, v_))

    dmod_cols = lax.dynamic_slice(dmod_all, (0, me * C6), (N_DEV, C6))
    pad16 = lambda t: jnp.concatenate([t, jnp.zeros((16 - N_DEV,) + t.shape[1:], t.dtype)], axis=0)
    g_, d_, m_, v_ = _ada_grad_adamw(pad16(cs_all), pad16(dmod_cols), ada_w[0], m_ada_w[0], v_ada_w[0])
    grads['ada_w'], deltas['ada_w'], new_m['ada_w'], new_v['ada_w'] = (t[None] for t in (g_, d_, m_, v_))

    rs_arr = _split_wait(rs_ss, rs_rs, rs_arr, _plan_scatter_all(3), g_, "rs_all_wait")
    mychip = 2 * ax + ay
    chips = jnp.stack([(mychip + k) % N_CHIP for k in range(N_CHIP)]).astype(jnp.int32)
    devs = jnp.stack([(me + k) % N_DEV for k in range(N_DEV)]).astype(jnp.int32)
    for n, own, land in zip(('w_out', 'w_ff1', 'w_ff2'), rs_arr[:3], rs_arr[3:]):
        g_, d_, m_, v_ = _sum_adamw_split(own, land, devs, wts[n][0], ms[n][0], vs[n][0], transposed=(n == 'w_ff1'))
        grads[n], deltas[n], new_m[n], new_v[n] = (t[None] for t in (g_, d_, m_, v_))
    r2_arr = _split_wait(r2_ss, r2_rs, r2_arr, _plan_scatter(1), g_, "rs_ici_wait_2")
    g_, d_, m_, v_ = _sum_adamw_split(r2_arr[0], r2_arr[1], chips, wts['w_in'][0].T, ms['w_in'][0].T, vs['w_in'][0].T)
    grads['w_in'], deltas['w_in'], new_m['w_in'], new_v['w_in'] = (t.T[None] for t in (g_, d_, m_, v_))

    return (loss, grad_x.reshape(x.shape), *[grads[n] for n in _WEIGHTS], *[deltas[n] for n in _WEIGHTS],
            *[new_m[n] for n in _WEIGHTS], *[new_v[n] for n in _WEIGHTS])
```

```python
import functools
import math

import numpy as np
import jax
import jax.numpy as jnp
from jax import lax
from jax.experimental import pallas as pl
from jax.experimental.pallas import tpu as pltpu

F32 = jnp.float32
BF16 = jnp.bfloat16
MESH = pl.DeviceIdType.MESH

N_DEV = 8
N_CHIP = 4
HEAD_DIM = 64
SSD_GROUPS = 2
SSD_STATE = 128
CHUNK = 128
CONV_K = 4
GQA = 8
LANE = 128
HALO = 8
LN_EPS = 1e-5
RMS_EPS = 1e-5
NEG = -1e30
ADAM_LR, ADAM_B1, ADAM_B2, ADAM_EPS, ADAM_WD, ADAM_STEP = 0.001, 0.9, 0.999, 1e-08, 0.01, 10
V7X_VMEM_BYTES = 64 * 1024 * 1024
VMEM_LIMIT = V7X_VMEM_BYTES - 8 * 1024 * 1024
MLP_TM = 512
MLP_SUB = 512
MLP_FC = 512
CONV_TM = 512
CONV_CB = 2048
ROW_CHUNK = 32


def _alibi_slopes(n):
    def pow2(m):
        start = 2.0 ** (-8.0 / m)
        return [start ** (i + 1) for i in range(m)]
    if math.log2(n).is_integer():
        s = pow2(n)
    else:
        c = 2 ** math.floor(math.log2(n))
        s = pow2(c) + pow2(2 * c)[0::2][: n - c]
    return [float(v) for v in np.array(s, dtype=np.float32)]


def _dot(a, b):
    return jnp.dot(a, b, preferred_element_type=F32)


def _dot_nt(a, b):
    return lax.dot_general(a, b, (((1,), (1,)), ((), ())), preferred_element_type=F32)


def _dot_tn(a, b):
    return lax.dot_general(a, b, (((0,), (0,)), ((), ())), preferred_element_type=F32)


def _split3(v):
    hi = v.astype(BF16)
    r1 = v - hi.astype(F32)
    mid = r1.astype(BF16)
    return hi, mid, (r1 - mid.astype(F32)).astype(BF16)


def _dot_hi(a, b, zero_one_rhs=False):
    if zero_one_rhs:
        bb = b.astype(BF16)
        return sum(_dot(t, bb) for t in _split3(a))
    ab = a.astype(BF16)
    return sum(_dot(ab, t) for t in _split3(b))


def _sigmoid(x):
    return 0.5 * jnp.tanh(0.5 * x) + 0.5


def _softplus(x):
    return jnp.maximum(x, 0.0) + jnp.log(1.0 + jnp.exp(-jnp.abs(x)))


def _mean(x):
    return jnp.mean(x, axis=-1, keepdims=True)


def _ln_fwd(x):
    xc = x - _mean(x)
    rstd = lax.rsqrt(_mean(xc * xc) + LN_EPS)
    return xc * rstd, rstd


def _ln_bwd(dxhat, xhat, rstd):
    return rstd * (dxhat - _mean(dxhat) - xhat * _mean(dxhat * xhat))


def _colsum(x):
    return jnp.sum(x, axis=0, keepdims=True)


def _params(sem):
    return pltpu.CompilerParams(dimension_semantics=sem, vmem_limit_bytes=VMEM_LIMIT)


def _tile(i_map_cols, tm):
    return pl.BlockSpec((tm, i_map_cols), lambda i: (i, 0))


def _res(shape):
    return pl.BlockSpec(shape, lambda *_: (0,) * len(shape), pipeline_mode=pl.Buffered(1))


def _acc(shape):
    return pl.BlockSpec(shape, lambda *_: (0,) * len(shape))


def _S(shape, dtype):
    return jax.ShapeDtypeStruct(shape, dtype)


def _my_pos():
    return lax.axis_index("x"), lax.axis_index("y"), lax.axis_index("c")


def _peer(pos, k):
    x, y, c = pos
    px = 1 - x if k & 4 else x
    py = 1 - y if k & 2 else y
    pc = 1 - c if k & 1 else c
    return (px, py, pc)


def _lin(p):
    return 4 * p[0] + 2 * p[1] + p[2]


def _mod_fwd(c_loc, ada_w_loc, ada_b_cols):
    D = c_loc.shape[1]
    C6 = ada_w_loc.shape[1]

    def body(c_ref, w_ref, b_ref, cs_ref, mod_ref, call_ref, modp_ref, ssem, rsem):
        pos = _my_pos()
        me = _lin(pos)
        call_ref[me] = c_ref[...]
        sends = []
        for k in range(1, N_DEV):
            cp = pltpu.make_async_remote_copy(src_ref=c_ref, dst_ref=call_ref.at[me], send_sem=ssem.at[k - 1],
                                              recv_sem=rsem.at[k - 1], device_id=_peer(pos, k), device_id_type=MESH)
            cp.start()
            sends.append(cp)
        for k in range(1, N_DEV):
            src = _lin(_peer(pos, k))
            pltpu.make_async_remote_copy(src_ref=c_ref, dst_ref=call_ref.at[src], send_sem=ssem.at[k - 1],
                                         recv_sem=rsem.at[k - 1], device_id=pos, device_id_type=MESH).wait_recv()
        for cp in sends:
            cp.wait_send()
        call = jnp.concatenate([call_ref[b] for b in range(N_DEV)], axis=0)
        cs = call * _sigmoid(call)
        cs_ref[...] = cs
        modp = _dot(cs.astype(BF16), w_ref[...].astype(BF16)) + b_ref[...]
        for b in range(N_DEV):
            modp_ref[b] = modp[b:b + 1, :]
        mod_ref[me] = modp_ref[me]
        sends = []
        for k in range(1, N_DEV):
            peer = _peer(pos, k)
            cp = pltpu.make_async_remote_copy(src_ref=modp_ref.at[_lin(peer)], dst_ref=mod_ref.at[me],
                                              send_sem=ssem.at[N_DEV - 2 + k], recv_sem=rsem.at[N_DEV - 2 + k],
                                              device_id=peer, device_id_type=MESH)
            cp.start()
            sends.append(cp)
        for k in range(1, N_DEV):
            src = _lin(_peer(pos, k))
            pltpu.make_async_remote_copy(src_ref=modp_ref.at[src], dst_ref=mod_ref.at[src],
                                         send_sem=ssem.at[N_DEV - 2 + k], recv_sem=rsem.at[N_DEV - 2 + k],
                                         device_id=pos, device_id_type=MESH).wait_recv()
        for cp in sends:
            cp.wait_send()

    vm = pl.BlockSpec(memory_space=pltpu.VMEM)
    return pl.pallas_call(
        body, name="mod_fwd",
        out_shape=(_S((N_DEV, D), F32), _S((N_DEV, 1, C6), F32)),
        in_specs=[vm, vm, vm], out_specs=(vm, vm),
        scratch_shapes=[pltpu.VMEM((N_DEV, 1, D), F32), pltpu.VMEM((N_DEV, 1, C6), F32),
                        pltpu.SemaphoreType.DMA((2 * (N_DEV - 1),)), pltpu.SemaphoreType.DMA((2 * (N_DEV - 1),))],
        compiler_params=pltpu.CompilerParams(vmem_limit_bytes=VMEM_LIMIT),
    )(c_loc, ada_w_loc, ada_b_cols)


def _small_sync_adamw(srcs, pieces, params, n_mod):
    n_src, n_par = len(srcs), len(params)
    rows_of = [sum(-(-w // LANE) for _, _, w in seg) for seg in pieces]
    starts = [sum(rows_of[:k]) for k in range(len(pieces))]
    NR = -(-sum(rows_of) // 8) * 8
    cd = pieces[n_par][0][2]

    def seg_row(arr, k, width):
        r = starts[k]
        if width <= LANE:
            return arr[r:r + 1, 0:width]
        return jnp.concatenate([arr[r + q:r + q + 1, :] for q in range(width // LANE)], axis=1)

    def exchange(*refs):
        src = refs[:n_src]
        total_ref, dmod_ref, pack_ref, gat_ref, ssem, rsem = refs[n_src:]
        pos = _my_pos()
        me = _lin(pos)
        pack_ref[...] = jnp.zeros_like(pack_ref)
        for k, seg in enumerate(pieces):
            r = starts[k]
            for (si, row, width) in seg:
                for q in range(-(-width // LANE)):
                    wq = min(LANE, width - q * LANE)
                    pack_ref[r:r + 1, 0:wq] = src[si][row:row + 1, q * LANE:q * LANE + wq]
                    r += 1
        gat_ref[me] = pack_ref[...]
        sends = []
        for k in range(1, N_DEV):
            cp = pltpu.make_async_remote_copy(src_ref=pack_ref, dst_ref=gat_ref.at[me], send_sem=ssem.at[k - 1],
                                              recv_sem=rsem.at[k - 1], device_id=_peer(pos, k), device_id_type=MESH)
            cp.start()
            sends.append(cp)
        for k in range(1, N_DEV):
            frm = _lin(_peer(pos, k))
            pltpu.make_async_remote_copy(src_ref=pack_ref, dst_ref=gat_ref.at[frm], send_sem=ssem.at[k - 1],
                                         recv_sem=rsem.at[k - 1], device_id=pos, device_id_type=MESH).wait_recv()
        for cp in sends:
            cp.wait_send()
        total = gat_ref[0]
        for j in range(1, N_DEV):
            total = total + gat_ref[j]
        total_ref[...] = total
        for j in range(N_DEV):
            dmod_ref[j:j + 1, :] = seg_row(gat_ref[j], 0, n_mod)

    def update(*refs):
        total = refs[0][...]
        wmv = refs[1:1 + 3 * n_par]
        outs = refs[1 + 3 * n_par:]
        for k in range(n_par):
            n = params[k][0].shape[1]
            g = seg_row(total, k, n)
            w_ref, m_ref, v_ref = wmv[3 * k:3 * k + 3]
            g_ref, d_ref, m2_ref, v2_ref = outs[4 * k:4 * k + 4]
            g_ref[...] = g
            d_ref[...], m2_ref[...], v2_ref[...] = _adamw_math(w_ref[...], g, m_ref[...], v_ref[...])
        gcw_ref, loss_ref = outs[4 * n_par:]
        for t in range(CONV_K):
            r = starts[n_par] + t * (cd // LANE)
            gcw_ref[t:t + 1, :] = jnp.concatenate([total[r + q:r + q + 1, :] for q in range(cd // LANE)], axis=1)
        loss_ref[...] = total[starts[n_par + 1]:starts[n_par + 1] + 1, :]

    vm = pl.BlockSpec(memory_space=pltpu.VMEM)
    total, dmod_all = pl.pallas_call(
        exchange, name="small_sync", out_shape=(_S((NR, LANE), F32), _S((N_DEV, n_mod), F32)),
        in_specs=[vm] * n_src, out_specs=(vm, vm),
        scratch_shapes=[pltpu.VMEM((NR, LANE), F32), pltpu.VMEM((N_DEV, NR, LANE), F32),
                        pltpu.SemaphoreType.DMA((N_DEV - 1,)), pltpu.SemaphoreType.DMA((N_DEV - 1,))],
        compiler_params=pltpu.CompilerParams(vmem_limit_bytes=VMEM_LIMIT),
    )(*srcs)
    out_shape = []
    for w, _, _ in params:
        out_shape += [_S(w.shape, F32)] * 4
    out_shape += [_S((CONV_K, cd), F32), _S((1, LANE), F32)]
    flat = [t for p in params for t in p]
    res = pl.pallas_call(
        update, name="small_adamw", out_shape=tuple(out_shape),
        in_specs=[vm] * (1 + 3 * n_par), out_specs=tuple([vm] * len(out_shape)),
        compiler_params=pltpu.CompilerParams(vmem_limit_bytes=VMEM_LIMIT),
    )(total, *flat)
    return (*res[:-1], dmod_all, res[-1])


def _ag_weights(shards, after):
    n = len(shards)

    def body(*refs):
        ins, outs = refs[:n], refs[n + 1:2 * n + 1]
        ssem, rsem, lsem = refs[2 * n + 1:]
        x, y, c = pos = _my_pos()
        me = _lin(pos)
        sib = (x, y, 1 - c)
        chips = [(1 - x, y), (x, 1 - y), (1 - x, 1 - y)]

        def copy(a, k, block, to, src=None):
            return pltpu.make_async_remote_copy(
                src_ref=outs[a].at[block] if src is None else src, dst_ref=outs[a].at[block],
                send_sem=ssem.at[a * 7 + k], recv_sem=rsem.at[a * 7 + k], device_id=to, device_id_type=MESH)

        local = [pltpu.make_async_copy(ins[a], outs[a].at[me], lsem.at[a]) for a in range(n)]
        for cp in local:
            cp.start()
        first = []
        for a in range(n):
            first.append(copy(a, 0, me, sib, src=ins[a]))
            first += [copy(a, 1 + j, me, (*chip, c), src=ins[a]) for j, chip in enumerate(chips)]
        for cp in first:
            cp.start()
        passed = []
        for a in range(n):
            for j, chip in enumerate(chips):
                blk = _lin((*chip, c))
                copy(a, 1 + j, blk, pos).wait_recv()
                cp = copy(a, 4 + j, blk, sib)
                cp.start()
                passed.append(cp)
        for a in range(n):
            copy(a, 0, _lin(sib), pos).wait_recv()
            for j, chip in enumerate(chips):
                copy(a, 4 + j, _lin((*chip, 1 - c)), pos).wait_recv()
        for cp in first + passed:
            cp.wait_send()
        for cp in local:
            cp.wait()

    hbm = pl.BlockSpec(memory_space=pl.ANY)
    return pl.pallas_call(
        body, name="ag_weights",
        out_shape=tuple(_S((N_DEV,) + s.shape, s.dtype) for s in shards),
        in_specs=[hbm] * (n + 1), out_specs=tuple([hbm] * n),
        scratch_shapes=[pltpu.SemaphoreType.DMA((7 * n,)), pltpu.SemaphoreType.DMA((7 * n,)),
                        pltpu.SemaphoreType.DMA((n,))],
    )(*shards, after)


def _rs_d2d(blocked, name):
    n = len(blocked)

    def body(*refs):
        ins, outs = refs[:n], refs[n:2 * n]
        ssem, rsem = refs[2 * n:]
        x, y, c = pos = _my_pos()
        sib = (x, y, 1 - c)
        cps = []
        for a in range(n):
            for j in range(N_CHIP):
                cp = pltpu.make_async_remote_copy(
                    src_ref=ins[a].at[2 * j + (1 - c)], dst_ref=outs[a].at[j], send_sem=ssem.at[a * N_CHIP + j],
                    recv_sem=rsem.at[a * N_CHIP + j], device_id=sib, device_id_type=MESH)
                cp.start()
                cps.append(cp)
        for cp in cps:
            cp.wait_recv()
        for cp in cps:
            cp.wait_send()

    hbm = pl.BlockSpec(memory_space=pl.ANY)
    return pl.pallas_call(
        body, name=name,
        out_shape=tuple(_S((N_CHIP,) + b.shape[1:], b.dtype) for b in blocked),
        in_specs=[hbm] * n, out_specs=tuple([hbm] * n),
        scratch_shapes=[pltpu.SemaphoreType.DMA((N_CHIP * n,)), pltpu.SemaphoreType.DMA((N_CHIP * n,))],
    )(*blocked)


_HBM = pl.BlockSpec(memory_space=pltpu.HBM)
_SEM = pl.BlockSpec(memory_space=pltpu.SEMAPHORE)
_ANY = pl.BlockSpec(memory_space=pl.ANY)
_EFFECT = pltpu.SideEffectType.DATAFLOW_SIDE_EFFECTING


def _in_hbm(a):
    return pltpu.with_memory_space_constraint(a, pltpu.HBM)


def _plan_gather(n):
    def copies(pos):
        x, y, c = pos
        out = []
        for a in range(n):
            for dev in [(x, y, 1 - c)] + [(*_peer(pos, 2 * k)[:2], c) for k in range(1, N_CHIP)]:
                out.append((a, None, n + a, _lin(pos), dev, _lin(dev)))
        return out
    return copies


def _plan_forward(n):
    def copies(pos):
        x, y, c = pos
        out = []
        for a in range(n):
            for k in range(1, N_CHIP):
                tx, ty, _ = _peer(pos, 2 * k)
                out.append((a, _lin((tx, ty, c)), a, _lin((tx, ty, c)), (x, y, 1 - c), _lin((tx, ty, 1 - c))))
        return out
    return copies


def _plan_scatter_all(n):
    def copies(pos):
        out = []
        for a in range(n):
            for k in range(1, N_DEV):
                dev = _peer(pos, k)
                out.append((a, _lin(dev), n + a, _lin(pos), dev, _lin(dev)))
        return out
    return copies


def _plan_scatter(n):
    def copies(pos):
        x, y, c = pos
        out = []
        for a in range(n):
            for k in range(1, N_CHIP):
                tx, ty, _ = _peer(pos, 2 * k)
                out.append((a, 2 * tx + ty, n + a, 2 * x + y, (tx, ty, c), 2 * tx + ty))
        return out
    return copies


def _split_copy(refs, cp, ssem, rsem, i, arrival):
    si, s_slot, di, d_slot, dev, a_slot = cp
    return pltpu.make_async_remote_copy(
        src_ref=refs[si] if s_slot is None else refs[si].at[s_slot], dst_ref=refs[di].at[a_slot if arrival else d_slot],
        send_sem=ssem.at[i], recv_sem=rsem.at[i], device_id=dev, device_id_type=MESH)


def _split_start(arrays, copies, after, name):
    n = len(arrays)
    n_cp = len(copies((0, 0, 0)))

    def body(*refs):
        ssem, rsem, token = refs[n + 1], refs[n + 2], refs[-1]
        for i, cp in enumerate(copies(_my_pos())):
            _split_copy(refs, cp, ssem, rsem, i, False).start()
        token[...] = jnp.zeros_like(token)

    res = pl.pallas_call(
        body, name=name,
        out_shape=(pltpu.SemaphoreType.DMA((n_cp,)), pltpu.SemaphoreType.DMA((n_cp,)),
                   *[pltpu.HBM(a.shape, a.dtype) for a in arrays], _S((8, LANE), F32)),
        in_specs=[_HBM] * n + [_ANY],
        out_specs=(_SEM, _SEM, *[_HBM] * n, pl.BlockSpec(memory_space=pltpu.VMEM)),
        input_output_aliases={a: 2 + a for a in range(n)},
        compiler_params=pltpu.CompilerParams(has_side_effects=_EFFECT),
    )(*[_in_hbm(a) for a in arrays], after)
    return res[0], res[1], list(res[2:2 + n]), res[-1]


def _split_wait(ssem, rsem, arrays, copies, after, name):
    n = len(arrays)

    def body(*refs):
        for i, cp in enumerate(copies(_my_pos())):
            d = _split_copy(refs, cp, refs[n], refs[n + 1], i, True)
            d.wait_send()
            d.wait_recv()

    res = pl.pallas_call(
        body, name=name,
        out_shape=tuple(pltpu.HBM(a.shape, a.dtype) for a in arrays),
        in_specs=[_HBM] * n + [_SEM, _SEM, _ANY], out_specs=tuple([_HBM] * n),
        input_output_aliases={a: a for a in range(n)},
        compiler_params=pltpu.CompilerParams(has_side_effects=_EFFECT),
    )(*arrays, ssem, rsem, after)
    return list(res)


def _row_tile(R, itemsize_rows=16, cap=256):
    t = cap - cap % itemsize_rows
    while t >= itemsize_rows:
        if R % t == 0:
            return t
        t -= itemsize_rows
    return R


def _pair_sum(blocked, recv, core):
    _, R, C = blocked.shape
    tr = _row_tile(R)

    def body(ids_ref, a_ref, b_ref, o_ref):
        del ids_ref
        o_ref[...] = (a_ref[...].astype(F32) + b_ref[...].astype(F32)).astype(BF16)

    gs = pltpu.PrefetchScalarGridSpec(
        num_scalar_prefetch=1, grid=(N_CHIP, R // tr),
        in_specs=[pl.BlockSpec((1, tr, C), lambda j, r, ids: (2 * j + ids[0], r, 0)),
                  pl.BlockSpec((1, tr, C), lambda j, r, ids: (j, r, 0))],
        out_specs=pl.BlockSpec((1, tr, C), lambda j, r, ids: (j, r, 0)))
    return pl.pallas_call(body, name="pair_sum", grid_spec=gs, out_shape=_S((N_CHIP, R, C), BF16),
                          compiler_params=_params(("arbitrary", "arbitrary")))(core, blocked, recv)


def _adamw_math(w, g, m, v):
    m2 = ADAM_B1 * m + (1.0 - ADAM_B1) * g
    v2 = ADAM_B2 * v + (1.0 - ADAM_B2) * (g * g)
    m_hat = m2 / (1.0 - ADAM_B1 ** ADAM_STEP)
    v_hat = v2 / (1.0 - ADAM_B2 ** ADAM_STEP)
    delta = -ADAM_LR * (m_hat / (jnp.sqrt(v_hat) + ADAM_EPS) + ADAM_WD * w)
    return delta, m2, v2


def _sum_adamw_split(pairs, land, chips, w, m, v, transposed=False):
    R, C = w.shape
    n_slots = chips.shape[0]
    tr = _row_tile(R, 128, 256) if transposed else R
    tc = C if transposed else _pick(C, (256, 128))

    def body(ids_ref, *refs):
        del ids_ref
        parts, (w_ref, m_ref, v_ref, g_ref, d_ref, m2_ref, v2_ref) = refs[:n_slots], refs[n_slots:]
        g = parts[0][0].astype(F32)
        for p_ref in parts[1:]:
            g = g + p_ref[0].astype(F32)
        if transposed:
            g = g.T
        g_ref[...] = g
        d_ref[...], m2_ref[...], v2_ref[...] = _adamw_math(w_ref[...], g, m_ref[...], v_ref[...])

    if transposed:
        t = pl.BlockSpec((tr, C), lambda r, ids: (r, 0))
        slot = lambda k: pl.BlockSpec((1, C, tr), lambda r, ids: (ids[k], 0, r))
    else:
        t = pl.BlockSpec((R, tc), lambda c, ids: (0, c))
        slot = lambda k: pl.BlockSpec((1, R, tc), lambda c, ids: (ids[k], 0, c))
    gs = pltpu.PrefetchScalarGridSpec(num_scalar_prefetch=1, grid=((R // tr) * (C // tc),),
                                      in_specs=[slot(k) for k in range(n_slots)] + [t, t, t], out_specs=(t, t, t, t))
    return pl.pallas_call(body, name="sum_adamw_split", grid_spec=gs, out_shape=tuple(_S((R, C), F32) for _ in range(4)),
                          compiler_params=_params(("arbitrary",)))(chips, pairs, *[land] * (n_slots - 1), w, m, v)


def _adamw(w, g, m, v):
    R, C = w.shape
    tr = _row_tile(R, 8)

    def body(w_ref, g_ref, m_ref, v_ref, d_ref, m2_ref, v2_ref):
        d_ref[...], m2_ref[...], v2_ref[...] = _adamw_math(w_ref[...], g_ref[...], m_ref[...], v_ref[...])

    t = pl.BlockSpec((tr, C), lambda r: (r, 0))
    return pl.pallas_call(body, name="adamw", grid=(R // tr,), in_specs=[t, t, t, t], out_specs=(t, t, t),
                          out_shape=tuple(_S((R, C), F32) for _ in range(3)),
                          compiler_params=_params(("arbitrary",)))(w, g, m, v)


def _ada_grad_adamw(cs16, dmod16, w, m, v):
    D, C6 = w.shape
    tr = _row_tile(D, 8, 256)

    def body(cs_ref, dm_ref, w_ref, m_ref, v_ref, g_ref, d_ref, m2_ref, v2_ref):
        g = _dot_tn(cs_ref[...].astype(BF16), dm_ref[...].astype(BF16))
        g_ref[...] = g
        d_ref[...], m2_ref[...], v2_ref[...] = _adamw_math(w_ref[...], g, m_ref[...], v_ref[...])

    t = pl.BlockSpec((tr, C6), lambda r: (r, 0))
    return pl.pallas_call(
        body, name="ada_grad_adamw", grid=(D // tr,),
        in_specs=[pl.BlockSpec((16, tr), lambda r: (0, r)), _acc((16, C6)), t, t, t], out_specs=(t, t, t, t),
        out_shape=tuple(_S((D, C6), F32) for _ in range(4)), compiler_params=_params(("arbitrary",)))(cs16, dmod16, w, m, v)


def _pick(n, cands):
    for c in cands:
        if n % c == 0:
            return c
    return n


def _matmul_tn(a, b, name, square_a=False):
    L, K = a.shape
    N = b.shape[1]
    bk = _pick(K, (1024, 512, 256, 128))
    bn = _pick(N, (1024, 768, 512, 256, 128))
    tl = _pick(L, (1024, 512, 256, 128))
    n_l = L // tl

    def body(a_ref, b_ref, o_ref, acc_ref):
        l = pl.program_id(2)

        @pl.when(l == 0)
        def _():
            acc_ref[...] = jnp.zeros_like(acc_ref)
        av = a_ref[...]
        if square_a:
            av = av.astype(F32)
            av = av * av
        acc_ref[...] += _dot_tn(av.astype(BF16), b_ref[...].astype(BF16))

        @pl.when(l == n_l - 1)
        def _():
            o_ref[...] = acc_ref[...].astype(BF16)

    return pl.pallas_call(
        body, name=name, grid=(K // bk, N // bn, n_l),
        in_specs=[pl.BlockSpec((tl, bk), lambda k, n, l: (l, k)), pl.BlockSpec((tl, bn), lambda k, n, l: (l, n))],
        out_specs=pl.BlockSpec((bk, bn), lambda k, n, l: (k, n)), out_shape=_S((K, N), BF16),
        scratch_shapes=[pltpu.VMEM((bk, bn), F32)],
        compiler_params=_params(("arbitrary", "arbitrary", "arbitrary")))(a, b)


def _merge_blocks(a):
    n, R, C = a.shape
    cb = _pick(C, (256, 128))

    def body(i_ref, o_ref):
        for j in range(n):
            o_ref[R * j:R * (j + 1), :] = i_ref[j]

    return pl.pallas_call(body, name="merge_blocks", out_shape=_S((n * R, C), a.dtype), grid=(C // cb,),
                          in_specs=[pl.BlockSpec((n, R, cb), lambda c: (0, 0, c))],
                          out_specs=pl.BlockSpec((n * R, cb), lambda c: (0, c)),
                          compiler_params=_params(("arbitrary",)))(a)


def _split_blocks(a, n):
    NR, C = a.shape
    R = NR // n
    cb = _pick(C, (256, 128))

    def body(i_ref, o_ref):
        for j in range(n):
            o_ref[j] = i_ref[R * j:R * (j + 1), :].astype(BF16)

    return pl.pallas_call(body, name="split_blocks", out_shape=_S((n, R, C), BF16), grid=(C // cb,),
                          in_specs=[pl.BlockSpec((NR, cb), lambda c: (0, c))],
                          out_specs=pl.BlockSpec((n, R, cb), lambda c: (0, 0, c)),
                          compiler_params=_params(("arbitrary",)))(a)


def _gw_in(pieces, u1, dims):
    L, D = u1.shape
    H = dims["H"]
    r_z, r_xbc, r_dt, r_q, r_kv = _proj_rows(dims)
    PROJ = r_kv[1]
    tl = _pick(L, (512, 256, 128))
    n_l = L // tl

    def body(dz_ref, dxbc_ref, ddt_ref, dq_ref, dkv_ref, u_ref, o_ref):
        @pl.when(pl.program_id(0) == 0)
        def _():
            o_ref[...] = jnp.zeros_like(o_ref)
        u = u_ref[...]
        for ref, (r0, r1) in ((dz_ref, r_z), (dxbc_ref, r_xbc), (dq_ref, r_q), (dkv_ref, r_kv)):
            o_ref[r0:r1, :] += _dot_tn(ref[...].astype(BF16), u)
        o_ref[r_dt[0]:r_dt[0] + H, :] += _dot_tn(ddt_ref[...].astype(BF16), u)[0:H, :]

    return pl.pallas_call(
        body, name="gw_in", grid=(n_l,),
        in_specs=[_tile(p.shape[1], tl) for p in pieces] + [_tile(D, tl)],
        out_specs=_acc((PROJ, D)), out_shape=_S((PROJ, D), F32),
        compiler_params=_params(("arbitrary",)))(*pieces, u1)


def _proj_rows(dims):
    W, CD, H, AW, KVW2 = dims["W"], dims["CD"], dims["H"], dims["AW"], dims["KVW2"]
    o_dt = W + CD
    o_q = o_dt + H
    return (0, W), (W, o_dt), (o_dt, o_dt + LANE), (o_q, o_q + AW), (o_q + AW, o_q + AW + KVW2)


def _ln_in_proj(x, g, b, sc, sh, w_t, dims):
    L, D = x.shape
    W, CD, AW, KVW2 = dims["W"], dims["CD"], dims["AW"], dims["KVW2"]
    PROJ = w_t.shape[0]
    tm = _pick(L, (MLP_TM, 128))
    r_z, r_xbc, r_dt, r_q, r_kv = _proj_rows(dims)

    def body(x_ref, g_ref, b_ref, sc_ref, sh_ref, w_ref, xhat_ref, rstd_ref, u1_ref, z_ref, xbc_ref, q_ref, kv_ref, dt_ref):
        xhat, rstd = _ln_fwd(x_ref[...])
        xhat_ref[...] = xhat
        rstd_ref[...] = rstd
        h0 = xhat * g_ref[...] + b_ref[...]
        u1 = (h0 * (1.0 + sc_ref[...]) + sh_ref[...]).astype(BF16)
        u1_ref[...] = u1
        z_ref[...] = _dot_nt(u1, w_ref[r_z[0]:r_z[1], :])
        xbc_ref[...] = _dot_nt(u1, w_ref[r_xbc[0]:r_xbc[1], :])
        q_ref[...] = _dot_nt(u1, w_ref[r_q[0]:r_q[1], :]).astype(BF16)
        kv_ref[...] = _dot_nt(u1, w_ref[r_kv[0]:r_kv[1], :]).astype(BF16)
        dt_ref[...] = _dot_nt(u1, w_ref[r_dt[0]:r_dt[1], :])

    v = _acc((1, D))
    return pl.pallas_call(
        body, name="ln_in_proj", grid=(L // tm,),
        in_specs=[_tile(D, tm), v, v, v, v, _res((PROJ, D))],
        out_specs=(_tile(D, tm), _tile(1, tm), _tile(D, tm), _tile(W, tm), _tile(CD, tm), _tile(AW, tm),
                   _tile(KVW2, tm), _tile(LANE, tm)),
        out_shape=(_S((L, D), F32), _S((L, 1), F32), _S((L, D), BF16), _S((L, W), F32), _S((L, CD), F32),
                   _S((L, AW), BF16), _S((L, KVW2), BF16), _S((L, LANE), F32)),
        compiler_params=_params(("arbitrary",)))(x, g, b, sc, sh, w_t)


def _conv_act(cur_ref, prev_ref, cw_ref, cb_ref, ext_ref, first):
    T = cur_ref.shape[0]
    ext_ref[0:HALO, :] = jnp.where(first, 0.0, prev_ref[...])
    ext_ref[HALO:HALO + T, :] = cur_ref[...]
    pre = cb_ref[...] + cw_ref[0:1, :] * ext_ref[HALO - 3:HALO - 3 + T, :]
    for k in range(1, CONV_K):
        pre = pre + cw_ref[k:k + 1, :] * ext_ref[HALO - 3 + k:HALO - 3 + k + T, :]
    sig = _sigmoid(pre)
    return pre * sig, pre, sig


def _tri(T, upper=False):
    r = lax.broadcasted_iota(jnp.int32, (T, T), 0)
    c = lax.broadcasted_iota(jnp.int32, (T, T), 1)
    return (r <= c) if upper else (r >= c)


def _expand_heads_lanes(dst_ref, v, n_heads):
    for h in range(n_heads):
        dst_ref[:, h * HEAD_DIM:(h + 1) * HEAD_DIM] = jnp.broadcast_to(v[:, h:h + 1], (v.shape[0], HEAD_DIM))


def _head_onehot(n_heads):
    hd = np.arange(3 * LANE)[:, None] % LANE
    ch = np.arange(n_heads * HEAD_DIM)[None, :] // HEAD_DIM
    return jnp.asarray((hd == ch).astype(np.float32)).astype(BF16)


def _expand_heads(dst_ref, v, n_heads, onehot_ref):
    onehot = onehot_ref[...]
    v = jnp.where(lax.broadcasted_iota(jnp.int32, v.shape, 1) < n_heads, v, 0.0)
    hi = v.astype(BF16)
    r1 = v - hi.astype(F32)
    mid = r1.astype(BF16)
    lo = (r1 - mid.astype(F32)).astype(BF16)
    dst_ref[...] = _dot(jnp.concatenate([hi, mid, lo], axis=1), onehot)


def _head_reduce(v):
    wdt = v.shape[1]
    ch = lax.broadcasted_iota(jnp.int32, (wdt, LANE), 0)
    lo = lax.broadcasted_iota(jnp.int32, (wdt, LANE), 1) * HEAD_DIM
    onehot = ((ch >= lo) & (ch < lo + HEAD_DIM)).astype(BF16)
    hi = v.astype(BF16)
    rest = (v - hi.astype(F32)).astype(BF16)
    return _dot(hi, onehot) + _dot(rest, onehot)


def _conv_ssd(xbc, dt_raw, z, cw, cb, dtb, alog, dsk, nw, dims):
    L, CD = xbc.shape
    W, H, G, N = dims["W"], dims["H"], SSD_GROUPS, SSD_STATE
    T = CHUNK
    R = H // G
    GW = W // G
    nc = L // T
    HP = H * HEAD_DIM

    def body(xbc_ref, prev_ref, dt_ref, z_ref, cw_ref, cb_ref, dtb_ref, alog_ref, dsk_ref, nw_ref, oh_ref,
             y_ref, yn_ref, sp_ref, pre_ref, ext_ref, s_ref, ybuf_ref, dtx_ref, acx_ref, xb_ref):
        i = pl.program_id(0)

        @pl.when(i == 0)
        def _():
            s_ref[...] = jnp.zeros_like(s_ref)

        act, pre, _ = _conv_act(xbc_ref, prev_ref, cw_ref, cb_ref, ext_ref, i == 0)
        pre_ref[...] = pre
        xs = act[:, :W]
        dt = _softplus(dt_ref[...] + dtb_ref[...])
        a = dt * (-jnp.exp(alog_ref[...]))
        low = _tri(T)
        acum = _dot_hi(low.astype(F32), a)
        acum_t = acum.T
        _expand_heads(dtx_ref, dt, H, oh_ref)
        _expand_heads(acx_ref, acum, H, oh_ref)
        acx = acx_ref[...]
        lastx = acx[T - 1:T, :]
        xd = xs * dtx_ref[...]
        xb_ref[...] = xd.astype(BF16)
        xdb = (xd * jnp.exp(lastx - acx)).astype(BF16)
        ex = jnp.exp(acx)
        elx = jnp.exp(lastx)
        for g in range(G):
            gs = slice(g * GW, (g + 1) * GW)
            bgb = act[:, W + g * N:W + (g + 1) * N].astype(BF16)
            cgb = act[:, W + G * N + g * N:W + G * N + (g + 1) * N].astype(BF16)
            stg = s_ref[:, gs]
            sp_ref[0, :, gs] = stg
            yoff = ex[:, gs] * _dot(cgb, stg.astype(BF16))
            s_ref[:, gs] = stg * elx[:, gs] + _dot_tn(bgb, xdb[:, gs])
            cb_g = _dot_nt(cgb, bgb)
            for r in range(R):
                h = g * R + r
                hs = slice(h * HEAD_DIM, (h + 1) * HEAD_DIM)
                lm = jnp.where(low, jnp.exp(acum[:, h:h + 1] - acum_t[h:h + 1, :]), 0.0)
                ybuf_ref[:, hs] = _dot((cb_g * lm).astype(BF16), xb_ref[:, hs]) + yoff[:, r * HEAD_DIM:(r + 1) * HEAD_DIM]
        y = ybuf_ref[...] + dsk_ref[...] * xs
        y_ref[...] = y
        zz = z_ref[...]
        hh = y * (zz * _sigmoid(zz))
        for g in range(G):
            gs = slice(g * GW, (g + 1) * GW)
            hg = hh[:, gs]
            yn_ref[:, gs] = (hg * lax.rsqrt(_mean(hg * hg) + RMS_EPS) * nw_ref[:, gs]).astype(BF16)

    return pl.pallas_call(
        body, name="conv_ssd", grid=(nc,),
        in_specs=[_tile(CD, T), pl.BlockSpec((HALO, CD), lambda i: (jnp.maximum(i * (T // HALO) - 1, 0), 0)),
                  _tile(LANE, T), _tile(W, T), _acc((CONV_K, CD)), _acc((1, CD)), _acc((1, LANE)), _acc((1, LANE)),
                  _acc((1, W)), _acc((1, W)), _acc((3 * LANE, W))],
        out_specs=(_tile(W, T), _tile(W, T), pl.BlockSpec((1, N, HP), lambda i: (i, 0, 0)), _tile(CD, T)),
        out_shape=(_S((L, W), F32), _S((L, W), BF16), _S((nc, N, HP), F32), _S((L, CD), F32)),
        scratch_shapes=[pltpu.VMEM((T + HALO, CD), F32), pltpu.VMEM((N, HP), F32), pltpu.VMEM((T, W), F32),
                        pltpu.VMEM((T, W), F32), pltpu.VMEM((T, W), F32), pltpu.VMEM((T, W), BF16)],
        compiler_params=_params(("arbitrary",)))(xbc, xbc, dt_raw, z, cw, cb, dtb, alog, dsk, nw, _head_onehot(H))


def _attn_mask(T, i):
    r = lax.broadcasted_iota(jnp.int32, (T, 2 * T), 0)
    c = lax.broadcasted_iota(jnp.int32, (T, 2 * T), 1)
    dist = r + T - c
    valid = (dist >= 0) & (dist < CHUNK) & ((c >= T) | (i > 0))
    return dist.astype(F32), valid


def _attn_probs(s_raw, dist, valid, slope, sink, axis):
    s = s_raw * (HEAD_DIM ** -0.5) - slope * dist
    s = jnp.where(valid, s, NEG)
    m = jnp.maximum(jnp.max(s, axis=axis, keepdims=True), sink)
    p = jnp.exp(s - m)
    e_sink = jnp.exp(sink - m)
    inv = 1.0 / (jnp.sum(p, axis=axis, keepdims=True) + e_sink)
    return p * inv, e_sink * inv


def _kv_heads(kvc_ref, kvp_ref, g, n_kv):
    ks = slice(g * HEAD_DIM, (g + 1) * HEAD_DIM)
    vs = slice((n_kv + g) * HEAD_DIM, (n_kv + g + 1) * HEAD_DIM)
    kk = jnp.concatenate([kvp_ref[:, ks], kvc_ref[:, ks]], axis=0)
    vv = jnp.concatenate([kvp_ref[:, vs], kvc_ref[:, vs]], axis=0)
    return kk, vv


def _swa_fwd(q, kv, sinks, dims):
    L, AW = q.shape
    KV, KVW2 = dims["KV"], dims["KVW2"]
    T = CHUNK
    nb = L // T
    slopes = _alibi_slopes(dims["AH"])

    NB = 2 if nb % 2 == 0 else 1

    def body(q_ref, kvc_ref, kvp_ref, sink_ref, o_ref, qg_ref, p_ref):
        i = pl.program_id(0)
        for sb in range(NB):
            rows = slice(sb * T, (sb + 1) * T)
            dist, valid = _attn_mask(T, i * NB + sb)
            for g in range(KV):
                ks = slice(g * HEAD_DIM, (g + 1) * HEAD_DIM)
                vs = slice((KV + g) * HEAD_DIM, (KV + g + 1) * HEAD_DIM)
                k_prev = kvp_ref[:, ks] if sb == 0 else kvc_ref[(sb - 1) * T:sb * T, ks]
                v_prev = kvp_ref[:, vs] if sb == 0 else kvc_ref[(sb - 1) * T:sb * T, vs]
                kk = jnp.concatenate([k_prev, kvc_ref[rows, ks]], axis=0)
                vv = jnp.concatenate([v_prev, kvc_ref[rows, vs]], axis=0)
                for r in range(GQA):
                    h = g * GQA + r
                    qg_ref[r * T:(r + 1) * T, :] = q_ref[rows, h * HEAD_DIM:(h + 1) * HEAD_DIM]
                s_all = _dot_nt(qg_ref[...], kk)
                for r in range(GQA):
                    h = g * GQA + r
                    p, _ = _attn_probs(s_all[r * T:(r + 1) * T, :], dist, valid, slopes[h], sink_ref[h], -1)
                    p_ref[r * T:(r + 1) * T, :] = p.astype(BF16)
                o_all = _dot(p_ref[...], vv)
                for r in range(GQA):
                    h = g * GQA + r
                    o_ref[rows, h * HEAD_DIM:(h + 1) * HEAD_DIM] = o_all[r * T:(r + 1) * T, :].astype(BF16)

    return pl.pallas_call(
        body, name="swa_fwd", grid=(nb // NB,),
        in_specs=[_tile(AW, NB * T), _tile(KVW2, NB * T),
                  pl.BlockSpec((T, KVW2), lambda i: (jnp.maximum(i * NB - 1, 0), 0)),
                  pl.BlockSpec(memory_space=pltpu.SMEM)],
        out_specs=_tile(AW, NB * T), out_shape=_S((L, AW), BF16),
        scratch_shapes=[pltpu.VMEM((GQA * T, HEAD_DIM), BF16), pltpu.VMEM((GQA * T, 2 * T), BF16)],
        compiler_params=_params(("arbitrary",)))(q, kv, kv, sinks)


def _out_proj_ln1(yn, o, w_out, xhat0, vecs, alpha):
    L, W = yn.shape
    D = xhat0.shape[1]
    MIX = w_out.shape[0]
    tm = _pick(L, (MLP_TM, 128))

    def body(yn_ref, o_ref, w_ref, xh_ref, v_ref, mix_ref, xhat1_ref, rstd1_ref, u2_ref):
        mix = _dot(yn_ref[...], w_ref[0:W, :]) + _dot(o_ref[...], w_ref[W:MIX, :])
        mix_ref[...] = mix
        h0 = xh_ref[...] * v_ref[0:1, :] + v_ref[1:2, :]
        xhat1, rstd1 = _ln_fwd(alpha * h0 + (1.0 + v_ref[2:3, :]) * mix)
        xhat1_ref[...] = xhat1
        rstd1_ref[...] = rstd1
        h1 = xhat1 * v_ref[3:4, :] + v_ref[4:5, :]
        u2_ref[...] = (h1 * (1.0 + v_ref[5:6, :]) + v_ref[6:7, :]).astype(BF16)

    return pl.pallas_call(
        body, name="out_proj_ln1", grid=(L // tm,),
        in_specs=[_tile(W, tm), _tile(MIX - W, tm), _res((MIX, D)), _tile(D, tm), _acc((8, D))],
        out_specs=(_tile(D, tm), _tile(D, tm), _tile(1, tm), _tile(D, tm)),
        out_shape=(_S((L, D), F32), _S((L, D), F32), _S((L, 1), F32), _S((L, D), BF16)),
        compiler_params=_params(("arbitrary",)))(yn, o, w_out, xhat0, vecs)


def _mlp_loss(u2, w1, w2, xhat1, tgt, vecs, b1, alpha):
    L, D = xhat1.shape
    FF = w1.shape[1]
    tm = _pick(L, (MLP_TM, 128))
    sub = min(tm, MLP_SUB)
    fc = _pick(FF, (MLP_FC, 256, 128))

    def body(u2_ref, w1_ref, w2_ref, xh_ref, t_ref, v_ref, b1_ref, rr_ref, dr2_ref, acc_ref, loss_ref):
        @pl.when(pl.program_id(0) == 0)
        def _():
            acc_ref[...] = jnp.zeros_like(acc_ref)
            loss_ref[...] = jnp.zeros_like(loss_ref)

        for s in range(tm // sub):
            rs = slice(s * sub, (s + 1) * sub)
            u2 = u2_ref[rs, :]
            f = jnp.zeros((sub, D), F32) + v_ref[5:6, :]
            for j in range(FF // fc):
                cs = slice(j * fc, (j + 1) * fc)
                rr = jnp.maximum(_dot(u2, w1_ref[:, cs]) + b1_ref[:, cs], 0.0)
                rr_ref[rs, cs] = rr.astype(BF16)
                f = f + _dot((rr * rr).astype(BF16), w2_ref[cs, :])
            xhat1 = xh_ref[rs, :]
            h1 = xhat1 * v_ref[0:1, :] + v_ref[1:2, :]
            xhat2, rstd2 = _ln_fwd(alpha * h1 + (1.0 + v_ref[2:3, :]) * f)
            e = xhat2 * v_ref[3:4, :] + v_ref[4:5, :] - t_ref[rs, :]
            loss_ref[...] += 0.5 * jnp.sum(_mean(e * e))
            dy = e * (1.0 / D)
            dr2 = _ln_bwd(dy * v_ref[3:4, :], xhat2, rstd2)
            dr2_ref[rs, :] = dr2
            acc_ref[0:1, :] += _colsum(dy * xhat2)
            acc_ref[1:2, :] += _colsum(dy)
            acc_ref[2:3, :] += _colsum(dr2 * f)

    return pl.pallas_call(
        body, name="mlp_loss", grid=(L // tm,),
        in_specs=[_tile(D, tm), _res((D, FF)), _res((FF, D)), _tile(D, tm), _tile(D, tm), _acc((8, D)), _acc((1, FF))],
        out_specs=(_tile(FF, tm), _tile(D, tm), _acc((8, D)), _acc((1, LANE))),
        out_shape=(_S((L, FF), BF16), _S((L, D), F32), _S((8, D), F32), _S((1, LANE), F32)),
        compiler_params=_params(("arbitrary",)))(u2, w1, w2, xhat1, tgt, vecs, b1)


def _mlp_bwd_a(dr2, rr, w2, g2):
    L, D = dr2.shape
    FF = w2.shape[0]
    tm = _pick(L, (MLP_TM, 128))
    fc = _pick(FF, (MLP_FC, 256, 128))

    def body(dr2_ref, rr_ref, w2_ref, g2_ref, df_ref, da_ref, gb2_ref, gb1_ref):
        @pl.when(pl.program_id(0) == 0)
        def _():
            gb2_ref[...] = jnp.zeros_like(gb2_ref)
            gb1_ref[...] = jnp.zeros_like(gb1_ref)

        df = (1.0 + g2_ref[...]) * dr2_ref[...]
        gb2_ref[...] += _colsum(df)
        dfb = df.astype(BF16)
        df_ref[...] = dfb
        for j in range(FF // fc):
            cs = slice(j * fc, (j + 1) * fc)
            da = _dot_nt(dfb, w2_ref[cs, :]) * (2.0 * rr_ref[:, cs].astype(F32))
            gb1_ref[:, cs] += _colsum(da)
            da_ref[:, cs] = da.astype(BF16)

    return pl.pallas_call(
        body, name="mlp_bwd_a", grid=(L // tm,),
        in_specs=[_tile(D, tm), _tile(FF, tm), _res((FF, D)), _acc((1, D))],
        out_specs=(_tile(D, tm), _tile(FF, tm), _acc((1, D)), _acc((1, FF))),
        out_shape=(_S((L, D), BF16), _S((L, FF), BF16), _S((1, D), F32), _S((1, FF), F32)),
        compiler_params=_params(("arbitrary",)))(dr2, rr, w2, g2)


def _mlp_bwd_b(da, w1, dr2, xhat1, rstd1, mix, w_out, vecs, alpha, W):
    L, FF = da.shape
    D = dr2.shape[1]
    MIX = w_out.shape[0]
    tm = _pick(L, (MLP_TM, 128))

    def body(da_ref, w1_ref, dr2_ref, xh_ref, rs_ref, mix_ref, wo_ref, v_ref, dmix_ref, dh0_ref, dyn_ref, do_ref, acc_ref):
        @pl.when(pl.program_id(0) == 0)
        def _():
            acc_ref[...] = jnp.zeros_like(acc_ref)

        du2 = _dot_nt(da_ref[...], w1_ref[...])
        xhat1 = xh_ref[...]
        h1 = xhat1 * v_ref[0:1, :] + v_ref[1:2, :]
        acc_ref[0:1, :] += _colsum(du2 * h1)
        acc_ref[1:2, :] += _colsum(du2)
        dh1 = alpha * dr2_ref[...] + du2 * (1.0 + v_ref[2:3, :])
        acc_ref[2:3, :] += _colsum(dh1 * xhat1)
        acc_ref[3:4, :] += _colsum(dh1)
        dr1 = _ln_bwd(dh1 * v_ref[0:1, :], xhat1, rs_ref[...])
        acc_ref[4:5, :] += _colsum(dr1 * mix_ref[...])
        dh0_ref[...] = alpha * dr1
        dmix = ((1.0 + v_ref[3:4, :]) * dr1).astype(BF16)
        dmix_ref[...] = dmix
        dyn_ref[...] = _dot_nt(dmix, wo_ref[0:W, :])
        do_ref[...] = _dot_nt(dmix, wo_ref[W:MIX, :]).astype(BF16)

    return pl.pallas_call(
        body, name="mlp_bwd_b", grid=(L // tm,),
        in_specs=[_tile(FF, tm), _res((D, FF)), _tile(D, tm), _tile(D, tm), _tile(1, tm), _tile(D, tm), _res((MIX, D)),
                  _acc((8, D))],
        out_specs=(_tile(D, tm), _tile(D, tm), _tile(W, tm), _tile(MIX - W, tm), _acc((8, D))),
        out_shape=(_S((L, D), BF16), _S((L, D), F32), _S((L, W), F32), _S((L, MIX - W), BF16), _S((8, D), F32)),
        compiler_params=_params(("arbitrary",)))(da, w1, dr2, xhat1, rstd1, mix, w_out, vecs)


def _swa_bwd(q, kv, do, sinks, dims):
    L, AW = q.shape
    KV, KVW2 = dims["KV"], dims["KVW2"]
    T = CHUNK
    nb = L // T
    slopes = _alibi_slopes(dims["AH"])
    scale = HEAD_DIM ** -0.5

    def body(q_ref, kvc_ref, kvp_ref, do_ref, sink_ref, dq_ref, dkv_ref, dsink_ref, carry_ref,
             qg_ref, dog_ref, pt_ref, dst_ref):
        i = pl.program_id(0)

        @pl.when(i == 0)
        def _():
            carry_ref[...] = jnp.zeros_like(carry_ref)
            dsink_ref[...] = jnp.zeros_like(dsink_ref)

        @pl.when(i < nb)
        def _():
            c = lax.broadcasted_iota(jnp.int32, (2 * T, T), 0)
            r_ = lax.broadcasted_iota(jnp.int32, (2 * T, T), 1)
            dist_i = r_ + T - c
            valid = (dist_i >= 0) & (dist_i < CHUNK) & ((c >= T) | (i > 0))
            dist = dist_i.astype(F32)
            lane = lax.broadcasted_iota(jnp.int32, (1, LANE), 1)
            dsink = jnp.zeros((1, LANE), F32)
            dks, dvs = [], []
            for g in range(KV):
                kk, vv = _kv_heads(kvc_ref, kvp_ref, g, KV)
                for r in range(GQA):
                    hs = slice((g * GQA + r) * HEAD_DIM, (g * GQA + r + 1) * HEAD_DIM)
                    qg_ref[r * T:(r + 1) * T, :] = q_ref[:, hs]
                    dog_ref[r * T:(r + 1) * T, :] = do_ref[:, hs]
                st_all = _dot_nt(kk, qg_ref[...])
                dpt_all = _dot_nt(vv, dog_ref[...])
                for r in range(GQA):
                    h = g * GQA + r
                    cs = slice(r * T, (r + 1) * T)
                    p, p_sink = _attn_probs(st_all[:, cs], dist, valid, slopes[h], sink_ref[h], 0)
                    dp = dpt_all[:, cs]
                    delta = jnp.sum(p * dp, axis=0, keepdims=True)
                    pt_ref[:, cs] = p.astype(BF16)
                    dst_ref[:, cs] = (p * (dp - delta)).astype(BF16)
                    dsink = dsink + jnp.where(lane == h, -jnp.sum(p_sink * delta), 0.0)
                dst = dst_ref[...]
                dks.append(_dot(dst, qg_ref[...]) * scale)
                dvs.append(_dot(pt_ref[...], dog_ref[...]))
                dq_all = _dot_tn(dst, kk) * scale
                for r in range(GQA):
                    hs = slice((g * GQA + r) * HEAD_DIM, (g * GQA + r + 1) * HEAD_DIM)
                    dq_ref[:, hs] = dq_all[r * T:(r + 1) * T, :].astype(BF16)
            dkv = jnp.concatenate(dks + dvs, axis=1)
            dsink_ref[...] += dsink
            dkv_ref[...] = carry_ref[...] + dkv[0:T, :]
            carry_ref[...] = dkv[T:2 * T, :]

        @pl.when(i == nb)
        def _():
            dkv_ref[...] = carry_ref[...]

    last = nb - 1
    return pl.pallas_call(
        body, name="swa_bwd", grid=(nb + 1,),
        in_specs=[pl.BlockSpec((T, AW), lambda i: (jnp.minimum(i, last), 0)),
                  pl.BlockSpec((T, KVW2), lambda i: (jnp.minimum(i, last), 0)),
                  pl.BlockSpec((T, KVW2), lambda i: (jnp.clip(i - 1, 0, last), 0)),
                  pl.BlockSpec((T, AW), lambda i: (jnp.minimum(i, last), 0)),
                  pl.BlockSpec(memory_space=pltpu.SMEM)],
        out_specs=(pl.BlockSpec((T, AW), lambda i: (jnp.minimum(i, last), 0)),
                   pl.BlockSpec((T, KVW2), lambda i: (jnp.maximum(i - 1, 0), 0)), _acc((1, LANE))),
        out_shape=(_S((L, AW), BF16), _S((L, KVW2), F32), _S((1, LANE), F32)),
        scratch_shapes=[pltpu.VMEM((T, KVW2), F32), pltpu.VMEM((GQA * T, HEAD_DIM), BF16),
                        pltpu.VMEM((GQA * T, HEAD_DIM), BF16), pltpu.VMEM((2 * T, GQA * T), BF16),
                        pltpu.VMEM((2 * T, GQA * T), BF16)],
        compiler_params=_params(("arbitrary",)))(q, kv, kv, do, sinks)


def _ssd_bwd(dyn, y, z, pre_act, dt_raw, sprev, dtb, alog, dsk, nw, dims):
    L, CD = pre_act.shape
    W, H, G, N = dims["W"], dims["H"], SSD_GROUPS, SSD_STATE
    T = CHUNK
    R = H // G
    GW = W // G
    nc = L // T
    HP = H * HEAD_DIM

    def body(dyn_ref, y_ref, z_ref, pre_ref, dt_ref, sp_ref, dtb_ref, alog_ref, dsk_ref, nw_ref,
             dz_ref, dpre_ref, ddt_ref, acc_ref, hacc_ref, ds_ref, dtx_ref, acx_ref, xb_ref, dyb_ref, r12_ref,
             dx_ref, rows_ref):
        i = pl.program_id(0)

        @pl.when(i == 0)
        def _():
            ds_ref[...] = jnp.zeros_like(ds_ref)
            acc_ref[...] = jnp.zeros_like(acc_ref)
            hacc_ref[...] = jnp.zeros_like(hacc_ref)

        pre = pre_ref[...]
        spre = _sigmoid(pre)
        act = pre * spre
        xs = act[:, :W]
        dt_in = dt_ref[...] + dtb_ref[...]
        dt = _softplus(dt_in)
        a_neg = -jnp.exp(alog_ref[...])
        a = dt * a_neg
        low = _tri(T)
        upf = _tri(T, upper=True).astype(F32)
        acum = _dot_hi(low.astype(F32), a)
        acum_t = acum.T

        y = y_ref[...]
        zz = z_ref[...]
        sg = _sigmoid(zz)
        sz = zz * sg
        hh = y * sz
        dyn_v = dyn_ref[...]
        parts = []
        for g in range(G):
            gs = slice(g * GW, (g + 1) * GW)
            hg = hh[:, gs]
            hhat = hg * lax.rsqrt(_mean(hg * hg) + RMS_EPS)
            rg = lax.rsqrt(_mean(hg * hg) + RMS_EPS)
            acc_ref[0:1, gs] += _colsum(dyn_v[:, gs] * hhat)
            dhhat = dyn_v[:, gs] * nw_ref[:, gs]
            parts.append(rg * (dhhat - hhat * _mean(dhhat * hhat)))
        dhh = jnp.concatenate(parts, axis=1)
        dy = dhh * sz
        dz_ref[...] = (dhh * y * (sg * (1.0 + zz * (1.0 - sg)))).astype(BF16)
        acc_ref[1:2, :] += _colsum(dy * xs)
        dyb_ref[...] = dy.astype(BF16)

        _expand_heads_lanes(dtx_ref, dt, H)
        _expand_heads_lanes(acx_ref, acum, H)
        dtx = dtx_ref[...]
        acx = acx_ref[...]
        lastx = acx[T - 1:T, :]
        ex = jnp.exp(acx)
        decx = jnp.exp(lastx - acx)
        elx = jnp.exp(lastx)
        xd = xs * dtx
        xb_ref[...] = xd.astype(BF16)
        xdecb = (xd * decx).astype(BF16)
        dgb = (ex * dy).astype(BF16)
        rows_ref[...] = jnp.zeros_like(rows_ref)

        lane = lax.broadcasted_iota(jnp.int32, (T, LANE), 1)
        sub = lax.broadcasted_iota(jnp.int32, (T, LANE), 0)
        subr = lax.broadcasted_iota(jnp.int32, (LANE, T), 0)
        da_col = jnp.zeros((T, LANE), F32)
        da_row = jnp.zeros((LANE, T), F32)
        dbs, dcs = [], []
        for g in range(G):
            gs = slice(g * GW, (g + 1) * GW)
            bgb = act[:, W + g * N:W + (g + 1) * N].astype(BF16)
            cgb = act[:, W + G * N + g * N:W + G * N + (g + 1) * N].astype(BF16)
            stg = sp_ref[0, :, gs]
            stb = stg.astype(BF16)
            dsn = ds_ref[:, gs]
            dsnb = dsn.astype(BF16)
            gm = _dot(cgb, stb)
            dc = _dot_nt(dgb[:, gs], stb)
            dsp = _dot_tn(cgb, dgb[:, gs])
            dxs_ = decx[:, gs] * _dot(bgb, dsnb)
            db = _dot_nt(xdecb[:, gs], dsnb)
            xdg = xd[:, gs]
            r12_ref[:, gs] = dy[:, gs] * ex[:, gs] * gm - xdg * dxs_
            rows_ref[0:1, gs] = _colsum(dsn * stg) * elx[:, gs]
            rows_ref[1:2, gs] = _colsum(xdg * dxs_)
            ds_ref[:, gs] = dsp + dsn * elx[:, gs]
            cb_g = _dot_nt(cgb, bgb)
            dcb = jnp.zeros((T, T), F32)
            for r in range(R):
                h = g * R + r
                hs = slice(h * HEAD_DIM, (h + 1) * HEAD_DIM)
                lm = jnp.where(low, jnp.exp(acum[:, h:h + 1] - acum_t[h:h + 1, :]), 0.0)
                mm = cb_g * lm
                dyb = dyb_ref[:, hs]
                dm = _dot_nt(dyb, xb_ref[:, hs])
                dx_ref[:, hs] = dxs_[:, r * HEAD_DIM:(r + 1) * HEAD_DIM] + _dot_tn(mm.astype(BF16), dyb)
                dcb = dcb + dm * lm
                qm = dm * mm
                da_col = jnp.where(lane == h, jnp.sum(qm, axis=1, keepdims=True), da_col)
                da_row = jnp.where(subr == h, jnp.sum(qm, axis=0, keepdims=True), da_row)
            dcbb = dcb.astype(BF16)
            dcs.append(dc + _dot(dcbb, bgb))
            dbs.append(db + _dot_tn(dcbb, cgb))
        dx = dx_ref[...]
        rows = _head_reduce(rows_ref[...])
        dlast = rows[0:1, :] + rows[1:2, :]
        da_col = da_col + _head_reduce(r12_ref[...]) + jnp.where(sub == T - 1, dlast, 0.0)
        dacum = da_col - da_row.T
        da = _dot_hi(upf, dacum)
        ddt = _head_reduce(dx * xs) + da * a_neg
        hacc_ref[1:2, :] += _colsum(da * dt) * a_neg
        ddt_raw = ddt * _sigmoid(dt_in)
        hacc_ref[0:1, :] += _colsum(ddt_raw)
        ddt_ref[...] = ddt_raw
        dact = jnp.concatenate([dsk_ref[...] * dy + dx * dtx] + dbs + dcs, axis=1)
        dpre_ref[...] = dact * (spre * (1.0 + pre * (1.0 - spre)))

        @pl.when(i == nc - 1)
        def _():
            ch = lax.broadcasted_iota(jnp.int32, (W, LANE), 0)
            lo = lax.broadcasted_iota(jnp.int32, (W, LANE), 1) * HEAD_DIM
            hacc_ref[2:3, :] = _dot_hi(acc_ref[1:2, :], ((ch >= lo) & (ch < lo + HEAD_DIM)).astype(F32), zero_one_rhs=True)

    rev = lambda i: (nc - 1 - i, 0)
    return pl.pallas_call(
        body, name="ssd_bwd", grid=(nc,),
        in_specs=[pl.BlockSpec((T, W), rev), pl.BlockSpec((T, W), rev), pl.BlockSpec((T, W), rev), pl.BlockSpec((T, CD), rev),
                  pl.BlockSpec((T, LANE), rev), pl.BlockSpec((1, N, HP), lambda i: (nc - 1 - i, 0, 0)),
                  _acc((1, LANE)), _acc((1, LANE)), _acc((1, W)), _acc((1, W))],
        out_specs=(pl.BlockSpec((T, W), rev), pl.BlockSpec((T, CD), rev), pl.BlockSpec((T, LANE), rev), _acc((8, W)),
                   _acc((8, LANE))),
        out_shape=(_S((L, W), BF16), _S((L, CD), F32), _S((L, LANE), F32), _S((8, W), F32), _S((8, LANE), F32)),
        scratch_shapes=[pltpu.VMEM((N, HP), F32), pltpu.VMEM((T, W), F32),
                        pltpu.VMEM((T, W), F32), pltpu.VMEM((T, W), BF16), pltpu.VMEM((T, W), BF16), pltpu.VMEM((T, W), F32),
                        pltpu.VMEM((T, W), F32), pltpu.VMEM((8, W), F32)],
        compiler_params=_params(("arbitrary",)))(dyn, y, z, pre_act, dt_raw, sprev, dtb, alog, dsk, nw)


def _conv_bwd(dpre, xbc, cw):
    L, CD = xbc.shape
    tm = _pick(L, (CONV_TM, 128))
    cb = CD if CONV_CB >= CD else _pick(CD, (CONV_CB, 128))
    nt = L // tm
    hb = tm // HALO

    def body(dp_ref, dn_ref, u_ref, cw_ref, du_ref, acc_ref, extd_ref):
        i = pl.program_id(1)

        @pl.when(i == 0)
        def _():
            acc_ref[...] = jnp.zeros_like(acc_ref)

        extd_ref[0:tm, :] = dp_ref[...]
        extd_ref[tm:tm + HALO, :] = jnp.where(i == nt - 1, 0.0, dn_ref[...])
        for c in range(tm // ROW_CHUNK):
            r0 = c * ROW_CHUNK
            rows = slice(r0, r0 + ROW_CHUNK)
            dp = dp_ref[rows, :]
            u = u_ref[rows, :]
            du = cw_ref[CONV_K - 1:CONV_K, :] * dp
            acc_ref[CONV_K - 1:CONV_K, :] += _colsum(dp * u)
            for k in range(CONV_K - 1):
                s = CONV_K - 1 - k
                dsh = extd_ref[r0 + s:r0 + s + ROW_CHUNK, :]
                du = du + cw_ref[k:k + 1, :] * dsh
                acc_ref[k:k + 1, :] += _colsum(u * dsh)
            acc_ref[CONV_K:CONV_K + 1, :] += _colsum(dp)
            du_ref[rows, :] = du.astype(BF16)

    tile = pl.BlockSpec((tm, cb), lambda j, i: (i, j))
    return pl.pallas_call(
        body, name="conv_bwd", grid=(CD // cb, nt),
        in_specs=[tile, pl.BlockSpec((HALO, cb), lambda j, i: (jnp.minimum((i + 1) * hb, nt * hb - 1), j)),
                  tile, pl.BlockSpec((CONV_K, cb), lambda j, i: (0, j))],
        out_specs=(tile, pl.BlockSpec((8, cb), lambda j, i: (0, j))),
        out_shape=(_S((L, CD), BF16), _S((8, CD), F32)),
        scratch_shapes=[pltpu.VMEM((tm + HALO, cb), F32)],
        compiler_params=_params(("arbitrary", "arbitrary")))(dpre, dpre, xbc, cw)


def _in_proj_bwd(dz, dxbc, dq, dkv, ddt, w_t, xhat0, rstd0, dh0p, vecs, dims):
    L, D = xhat0.shape
    W, CD, AW, KVW2 = dims["W"], dims["CD"], dims["AW"], dims["KVW2"]
    PROJ = w_t.shape[0]
    tm = _pick(L, (MLP_TM, 128))
    r_z, r_xbc, r_dt, r_q, r_kv = _proj_rows(dims)

    def body(dz_ref, dxbc_ref, dq_ref, dkv_ref, ddt_ref, w_ref, xh_ref, rs_ref, dh0_ref, v_ref, gx_ref, acc_ref):
        @pl.when(pl.program_id(0) == 0)
        def _():
            acc_ref[...] = jnp.zeros_like(acc_ref)

        du1 = _dot(dz_ref[...], w_ref[r_z[0]:r_z[1], :])
        du1 = du1 + _dot(dxbc_ref[...], w_ref[r_xbc[0]:r_xbc[1], :])
        du1 = du1 + _dot(dq_ref[...], w_ref[r_q[0]:r_q[1], :])
        du1 = du1 + _dot(dkv_ref[...].astype(BF16), w_ref[r_kv[0]:r_kv[1], :])
        du1 = du1 + _dot(ddt_ref[...].astype(BF16), w_ref[r_dt[0]:r_dt[1], :])
        xhat0 = xh_ref[...]
        h0 = xhat0 * v_ref[0:1, :] + v_ref[1:2, :]
        acc_ref[0:1, :] += _colsum(du1 * h0)
        acc_ref[1:2, :] += _colsum(du1)
        dh0 = dh0_ref[...] + du1 * (1.0 + v_ref[2:3, :])
        acc_ref[2:3, :] += _colsum(dh0 * xhat0)
        acc_ref[3:4, :] += _colsum(dh0)
        gx_ref[...] = _ln_bwd(dh0 * v_ref[0:1, :], xhat0, rs_ref[...])

    return pl.pallas_call(
        body, name="in_proj_bwd", grid=(L // tm,),
        in_specs=[_tile(W, tm), _tile(CD, tm), _tile(AW, tm), _tile(KVW2, tm), _tile(LANE, tm), _res((PROJ, D)),
                  _tile(D, tm), _tile(1, tm), _tile(D, tm), _acc((8, D))],
        out_specs=(_tile(D, tm), _acc((8, D))),
        out_shape=(_S((L, D), F32), _S((8, D), F32)),
        compiler_params=_params(("arbitrary",)))(dz, dxbc, dq, dkv, ddt, w_t, xhat0, rstd0, dh0p, vecs)


_WEIGHTS = ['ln_in_g', 'ln_in_b', 'ada_w', 'ada_b', 'w_in', 'conv_w', 'conv_b', 'dt_bias', 'a_log', 'd_skip', 'ssd_norm_w',
            'attn_sinks', 'w_out', 'ln1_g', 'ln1_b', 'w_ff1', 'b_ff1', 'w_ff2', 'b_ff2', 'ln2_g', 'ln2_b']
_BIG = ('w_in', 'w_out', 'w_ff1', 'w_ff2')
_SMALL = ('ada_b', 'ln_in_g', 'ln_in_b', 'conv_b', 'dt_bias', 'a_log', 'd_skip', 'ssd_norm_w', 'attn_sinks', 'ln1_g', 'ln1_b',
          'b_ff1', 'b_ff2', 'ln2_g', 'ln2_b')


def _pad_lanes(v, n=None):
    v = v.reshape(1, -1)
    n = n or -(-v.shape[1] // LANE) * LANE
    return jnp.pad(v, ((0, 0), (0, n - v.shape[1])))


def _vec8(rows, D):
    rows = [r.reshape(1, D) for r in rows]
    return jnp.concatenate(rows + [jnp.zeros((8 - len(rows), D), F32)], axis=0)


def kernel(x, c, ln_in_g, ln_in_b, ada_w, ada_b, w_in, conv_w, conv_b, dt_bias, a_log, d_skip, ssd_norm_w, attn_sinks, w_out, ln1_g, ln1_b, w_ff1, b_ff1, w_ff2, b_ff2, ln2_g, ln2_b, loss_target, m_ln_in_g, m_ln_in_b, m_ada_w, m_ada_b, m_w_in, m_conv_w, m_conv_b, m_dt_bias, m_a_log, m_d_skip, m_ssd_norm_w, m_attn_sinks, m_w_out, m_ln1_g, m_ln1_b, m_w_ff1, m_b_ff1, m_w_ff2, m_b_ff2, m_ln2_g, m_ln2_b, v_ln_in_g, v_ln_in_b, v_ada_w, v_ada_b, v_w_in, v_conv_w, v_conv_b, v_dt_bias, v_a_log, v_d_skip, v_ssd_norm_w, v_attn_sinks, v_w_out, v_ln1_g, v_ln1_b, v_w_ff1, v_b_ff1, v_w_ff2, v_b_ff2, v_ln2_g, v_ln2_b):
    wts = dict(ln_in_g=ln_in_g, ln_in_b=ln_in_b, ada_w=ada_w, ada_b=ada_b, w_in=w_in, conv_w=conv_w, conv_b=conv_b,
               dt_bias=dt_bias, a_log=a_log, d_skip=d_skip, ssd_norm_w=ssd_norm_w, attn_sinks=attn_sinks, w_out=w_out,
               ln1_g=ln1_g, ln1_b=ln1_b, w_ff1=w_ff1, b_ff1=b_ff1, w_ff2=w_ff2, b_ff2=b_ff2, ln2_g=ln2_g, ln2_b=ln2_b)
    ms = dict(ln_in_g=m_ln_in_g, ln_in_b=m_ln_in_b, ada_w=m_ada_w, ada_b=m_ada_b, w_in=m_w_in, conv_w=m_conv_w,
              conv_b=m_conv_b, dt_bias=m_dt_bias, a_log=m_a_log, d_skip=m_d_skip, ssd_norm_w=m_ssd_norm_w,
              attn_sinks=m_attn_sinks, w_out=m_w_out, ln1_g=m_ln1_g, ln1_b=m_ln1_b, w_ff1=m_w_ff1, b_ff1=m_b_ff1,
              w_ff2=m_w_ff2, b_ff2=m_b_ff2, ln2_g=m_ln2_g, ln2_b=m_ln2_b)
    vs = dict(ln_in_g=v_ln_in_g, ln_in_b=v_ln_in_b, ada_w=v_ada_w, ada_b=v_ada_b, w_in=v_w_in, conv_w=v_conv_w,
              conv_b=v_conv_b, dt_bias=v_dt_bias, a_log=v_a_log, d_skip=v_d_skip, ssd_norm_w=v_ssd_norm_w,
              attn_sinks=v_attn_sinks, w_out=v_w_out, ln1_g=v_ln1_g, ln1_b=v_ln1_b, w_ff1=v_w_ff1, b_ff1=v_b_ff1,
              w_ff2=v_w_ff2, b_ff2=v_b_ff2, ln2_g=v_ln2_g, ln2_b=v_ln2_b)

    L, D = x.shape[1], x.shape[2]
    depth = w_in.shape[0]
    assert depth == 1 and x.shape[0] == 1 and L % CHUNK == 0
    W = D
    H = W // HEAD_DIM
    CD = W + 2 * SSD_GROUPS * SSD_STATE
    AW = D
    AH = AW // HEAD_DIM
    KV = AH // GQA
    KVW2 = 2 * KV * HEAD_DIM
    PROJ = W + CD + H + AW + KVW2
    FF = w_ff1.shape[2] * N_DEV
    MIX = w_out.shape[1] * N_DEV
    assert w_in.shape[2] * N_DEV == PROJ and MIX == W + AW and H <= LANE and AH <= LANE
    dims = dict(W=W, H=H, CD=CD, AW=AW, AH=AH, KV=KV, KVW2=KVW2)
    alpha = (2.0 * depth) ** 0.25
    C6 = ada_w.shape[2]
    CW = conv_w.shape[2]

    ax, ay, ac = _my_pos()
    me = 4 * ax + 2 * ay + ac
    x2 = x.reshape(L, D)
    tgt = loss_target.reshape(L, D)
    r1 = lambda a: a.reshape(1, -1)

    ada_b_cols = lax.dynamic_slice(ada_b, (0, me * C6), (1, C6))
    cs_all, mod = _mod_fwd(c, ada_w[0], ada_b_cols)
    sh1, sc1, g1, sh2, sc2, g2 = [r1(t) for t in jnp.split(mod.reshape(-1), 6)]

    wg_in, cwg = _ag_weights([w_in[0].T.astype(BF16), conv_w[0]], cs_all)
    shards2 = [w_out[0].astype(BF16), w_ff1[0].astype(BF16), w_ff2[0].astype(BF16)]
    lands2 = [lax.dynamic_update_slice(lax.empty((N_DEV,) + s.shape, s.dtype), s[None], (me, 0, 0)) for s in shards2]
    ag_ss, ag_rs, ag_arr, ag_token = _split_start(shards2 + lands2, _plan_gather(3), cwg, "ag_ici_start")
    sh1 = sh1 + ag_token[0:1, 0:1]
    w_pad = _merge_blocks(wg_in)
    cw_full = cwg.transpose(1, 0, 2).reshape(CONV_K, CD)

    dtb = _pad_lanes(dt_bias, LANE)
    alog = _pad_lanes(a_log, LANE)
    dsk = jnp.repeat(d_skip.reshape(-1), HEAD_DIM).reshape(1, W)
    sinks = attn_sinks.reshape(-1)
    g_in, b_in = r1(ln_in_g), r1(ln_in_b)

    xhat0, rstd0, u1, z, xbc, q, kv, dt_raw = _ln_in_proj(x2, g_in, b_in, sc1, sh1, w_pad, dims)
    y, yn, sprev, pre = _conv_ssd(xbc, dt_raw, z, cw_full, conv_b, dtb, alog, dsk, ssd_norm_w, dims)
    ag_arr = _split_wait(ag_ss, ag_rs, ag_arr, _plan_gather(3), yn, "ag_ici_wait")
    fw_ss, fw_rs, ag_land, fw_token = _split_start(ag_arr[3:], _plan_forward(3), yn, "ag_fwd_start")
    o = _swa_fwd(q, kv, sinks + fw_token[0, 0], dims)
    wg_out, wg_ff1, wg_ff2 = _split_wait(fw_ss, fw_rs, ag_land, _plan_forward(3), o, "ag_fwd_wait")
    w_out_full = wg_out.reshape(MIX, D)
    w1_full = wg_ff1.transpose(1, 0, 2).reshape(D, FF)
    w2_full = wg_ff2.reshape(FF, D)
    mix, xhat1, rstd1, u2 = _out_proj_ln1(yn, o, w_out_full, xhat0, _vec8([g_in, b_in, g1, ln1_g, ln1_b, sc2, sh2], D), alpha)
    rr, dr2, acc_f, loss_loc = _mlp_loss(u2, w1_full, w2_full, xhat1, tgt,
                                         _vec8([ln1_g, ln1_b, g2, ln2_g, ln2_b, b_ff2], D), b_ff1, alpha)

    df, da, gb2, gb1 = _mlp_bwd_a(dr2, rr, w2_full, g2)
    gw_ff2 = _matmul_tn(rr, df, "gw_ff2", square_a=True)
    gw_ff1t = _matmul_tn(da, u2, "gw_ff1")
    dmix, dh0p, dyn, do, acc_b = _mlp_bwd_b(da, w1_full, dr2, xhat1, rstd1, mix, w_out_full,
                                            _vec8([ln1_g, ln1_b, sc2, g1], D), alpha, W)
    gw_out = jnp.concatenate([_matmul_tn(yn, dmix, "gw_out_ssd"), _matmul_tn(o, dmix, "gw_out_attn")], axis=0)

    core = jnp.reshape(ac, (1,)).astype(jnp.int32)
    blocked1 = [gw_out.reshape(N_DEV, MIX // N_DEV, D), gw_ff1t.reshape(N_DEV, FF // N_DEV, D),
                gw_ff2.reshape(N_DEV, FF // N_DEV, D)]
    lands1 = [lax.empty(b.shape, b.dtype) for b in blocked1]
    rs_ss, rs_rs, rs_arr, rs_token = _split_start(blocked1 + lands1, _plan_scatter_all(3), do, "rs_all_start")
    dq, dkv, dsink = _swa_bwd(q, kv, do, sinks + rs_token[0, 0], dims)
    dz, dpre, ddt, acc_s, hacc = _ssd_bwd(dyn, y, z, pre, dt_raw, sprev, dtb + rs_token[0:1, 0:1], alog, dsk, ssd_norm_w, dims)
    dxbc, acc_c = _conv_bwd(dpre, xbc, cw_full)
    gw_in = _gw_in((dz, dxbc, ddt, dq, dkv), u1, dims)

    blocked2 = [_split_blocks(gw_in, N_DEV)]
    pairs2 = [_pair_sum(b, r, core) for b, r in zip(blocked2, _rs_d2d(blocked2, "rs_d2d_2"))]
    lands2 = [lax.empty(p.shape, p.dtype) for p in pairs2]
    r2_ss, r2_rs, r2_arr, r2_token = _split_start(pairs2 + lands2, _plan_scatter(1), gw_in, "rs_ici_start_2")
    grad_x, acc_i = _in_proj_bwd(dz, dxbc, dq, dkv, ddt, w_pad, xhat0, rstd0, dh0p,
                                 _vec8([g_in, b_in, sc1], D) + r2_token[0:1, 0:1], dims)

    srcs = [acc_i, acc_b, acc_f, acc_s, acc_c, hacc, dsink, gb1, gb2, loss_loc]
    I_, B_, F_, S_, C_, H_, K_, G1_, G2_, L_ = range(10)
    seg_of = dict(ada_b=[(I_, 1, D), (I_, 0, D), (B_, 4, D), (B_, 1, D), (B_, 0, D), (F_, 2, D)],
                  ln_in_g=[(I_, 2, D)], ln_in_b=[(I_, 3, D)], conv_b=[(C_, CONV_K, CD)], dt_bias=[(H_, 0, H)],
                  a_log=[(H_, 1, H)], d_skip=[(H_, 2, H)], ssd_norm_w=[(S_, 0, W)], attn_sinks=[(K_, 0, AH)],
                  ln1_g=[(B_, 2, D)], ln1_b=[(B_, 3, D)], b_ff1=[(G1_, 0, FF)], b_ff2=[(G2_, 0, D)],
                  ln2_g=[(F_, 0, D)], ln2_b=[(F_, 1, D)])
    pieces = [seg_of[n] for n in _SMALL] + [[(C_, t, CD) for t in range(CONV_K)], [(L_, 0, 1)]]
    params = [tuple(t[n].reshape(1, -1) for t in (wts, ms, vs)) for n in _SMALL]
    res = _small_sync_adamw(srcs, pieces, params, 6 * D)
    grads, deltas, new_m, new_v = {}, {}, {}, {}
    for k, n in enumerate(_SMALL):
        grads[n], deltas[n], new_m[n], new_v[n] = (t.reshape(wts[n].shape) for t in res[4 * k:4 * k + 4])
    gcw_full, dmod_all, loss_row = res[4 * len(_SMALL):]
    loss = loss_row[0, 0]

    g_ = lax.dynamic_slice(gcw_full, (0, me * CW), (CONV_K, CW))
    d_, m_, v_ = _adamw(conv_w[0], g_, m_conv_w[0], v_conv_w[0])
    grads['conv_w'], deltas['conv_w'], new_m['conv_w'], new_v['conv_w'] = (t[None] for t in (g_, d_, m_, v_))

    dmod_cols = lax.dynamic_slice(dmod_all, (0, me * C6), (N_DEV, C6))
    pad16 = lambda t: jnp.concatenate([t, jnp.zeros((16 - N_DEV,) + t.shape[1:], t.dtype)], axis=0)
    g_, d_, m_, v_ = _ada_grad_adamw(pad16(cs_all), pad16(dmod_cols), ada_w[0], m_ada_w[0], v_ada_w[0])
    grads['ada_w'], deltas['ada_w'], new_m['ada_w'], new_v['ada_w'] = (t[None] for t in (g_, d_, m_, v_))

    rs_arr = _split_wait(rs_ss, rs_rs, rs_arr, _plan_scatter_all(3), g_, "rs_all_wait")
    mychip = 2 * ax + ay
    chips = jnp.stack([(mychip + k) % N_CHIP for k in range(N_CHIP)]).astype(jnp.int32)
    devs = jnp.stack([(me + k) % N_DEV for k in range(N_DEV)]).astype(jnp.int32)
    for n, own, land in zip(('w_out', 'w_ff1', 'w_ff2'), rs_arr[:3], rs_arr[3:]):
        g_, d_, m_, v_ = _sum_adamw_split(own, land, devs, wts[n][0], ms[n][0], vs[n][0], transposed=(n == 'w_ff1'))
        grads[n], deltas[n], new_m[n], new_v[n] = (t[None] for t in (g_, d_, m_, v_))
    r2_arr = _split_wait(r2_ss, r2_rs, r2_arr, _plan_scatter(1), g_, "rs_ici_wait_2")
    g_, d_, m_, v_ = _sum_adamw_split(r2_arr[0], r2_arr[1], chips, wts['w_in'][0].T, ms['w_in'][0].T, vs['w_in'][0].T)
    grads['w_in'], deltas['w_in'], new_m['w_in'], new_v['w_in'] = (t.T[None] for t in (g_, d_, m_, v_))

    return (loss, grad_x.reshape(x.shape), *[grads[n] for n in _WEIGHTS], *[deltas[n] for n in _WEIGHTS],
            *[new_m[n] for n in _WEIGHTS], *[new_v[n] for n in _WEIGHTS])
```
